```python
import math
import jax, jax.numpy as jnp
from jax import lax
import numpy as np

D_MODEL = 1024
BATCH = 8
SEQ = 2048
DEPTH = 1
DEC_BATCH = 32
DEC_SEQ = 8
PAST_LEN = 8192
PAGE_SIZE = 128

H_A = 4
DK_A = 64
DV_A = 2 * DK_A
ROPE_THETA = 10000.0
Q_BLOCK = 128
H_M = 4
DH_M = 128
CONV_W = 4
M_CHUNK = 64
N_EXP = 32
TOP_K = 4
D_FF = D_MODEL
SWIGLU_LIMIT = 7.0
SWIGLU_ALPHA = 1.702
MOE_BLOCK = 128
EPS = 1e-6

W_QA = H_A * 2 * DK_A
W_VA = H_A * DV_A
W_M = H_M * DH_M
SPLIT_SIZES = (W_QA, W_QA, W_VA, W_M, W_M, W_M, W_M, 2 * H_M, D_MODEL, D_MODEL)
D_IN = sum(SPLIT_SIZES)
SPLIT_POINTS = tuple(sum(SPLIT_SIZES[: i + 1]) for i in range(len(SPLIT_SIZES) - 1))

kernel_name = "hybrid_diffattn_mlstm_moe_adaln_step"

F32 = jnp.float32


def rmsnorm(x, g):
    xf = x.astype(F32)
    y = xf * lax.rsqrt(jnp.mean(xf * xf, axis=-1, keepdims=True) + EPS)
    return (y * g.astype(F32)).astype(x.dtype)


def rope(x, pos):
    half = DK_A // 2
    inv = ROPE_THETA ** (-jnp.arange(half, dtype=F32) * 2.0 / DK_A)
    ang = pos.astype(F32)[:, None] * inv[None, :]
    cos = jnp.cos(ang)[None, :, None, None, :]
    sin = jnp.sin(ang)[None, :, None, None, :]
    xf = x.astype(F32)
    x1, x2 = xf[..., :half], xf[..., half:]
    return jnp.concatenate([x1 * cos - x2 * sin, x2 * cos + x1 * sin], axis=-1).astype(x.dtype)


def diff_attend(q, k, v, q_pos, k_pos, lam):
    s = jnp.einsum('bqhcd,bkhcd->bhcqk', q, k, preferred_element_type=F32) * (DK_A ** -0.5)
    mask = k_pos[None, :] <= q_pos[:, None]
    p = jax.nn.softmax(jnp.where(mask, s, -jnp.inf), axis=-1)
    a = p[:, :, 0] - lam * p[:, :, 1]
    return jnp.einsum('bhqk,bkhd->bqhd', a, v.astype(F32))


def causal_conv(u, buf, w, b):
    T = u.shape[1]
    full = jnp.concatenate([buf.astype(u.dtype), u], axis=1)
    out = sum(full[:, j:j + T] * w[j] for j in range(CONV_W)) + b
    return jax.nn.silu(out), full[:, -(CONV_W - 1):]


def mlstm_chunkwise(q, k, v, log_i, log_f, C0, n0, m0):
    B, T, H, D = q.shape
    L = M_CHUNK if T % M_CHUNK == 0 else T
    nc = T // L

    def to_chunks(a):
        a = a.reshape((B, nc, L) + a.shape[2:])
        return jnp.moveaxis(jnp.moveaxis(a, 1, 0), 3, 2)

    tri = jnp.tril(jnp.ones((L, L), dtype=bool))

    def step(carry, xs):
        C, n, m = carry
        qc, kc, vc, li, lf = xs
        b = jnp.cumsum(lf, axis=-1)
        logD = jnp.where(tri, b[..., :, None] - b[..., None, :] + li[..., None, :], -jnp.inf)
        inter = b + m[..., None]
        mt = jnp.maximum(inter, jnp.max(logD, axis=-1))
        S = jnp.einsum('bhtd,bhsd->bhts', qc, kc) * jnp.exp(logD - mt[..., None])
        ei = jnp.exp(inter - mt)
        num = ei[..., None] * jnp.einsum('bhtd,bhde->bhte', qc, C) + jnp.einsum('bhts,bhse->bhte', S, vc)
        den = ei * jnp.einsum('bhtd,bhd->bht', qc, n) + jnp.sum(S, axis=-1)
        h = num / jnp.maximum(jnp.abs(den), jnp.exp(-mt))[..., None]
        g = b[..., -1:] - b + li
        bl = b[..., -1] + m
        m_new = jnp.maximum(bl, jnp.max(g, axis=-1))
        wg = jnp.exp(g - m_new[..., None])
        decay = jnp.exp(bl - m_new)
        C_new = decay[..., None, None] * C + jnp.einsum('bhs,bhsd,bhse->bhde', wg, kc, vc)
        n_new = decay[..., None] * n + jnp.einsum('bhs,bhsd->bhd', wg, kc)
        return (C_new, n_new, m_new), h

    xs = (to_chunks(q), to_chunks(k), to_chunks(v), to_chunks(log_i), to_chunks(log_f))
    (C, n, m), hs = lax.scan(step, (C0.astype(F32), n0.astype(F32), m0.astype(F32)), xs)
    hs = jnp.swapaxes(jnp.moveaxis(hs, 0, 1), 2, 3).reshape(B, T, H, D)
    return hs, C, n, m


def mixer(h, p, lam_init, pos, past, conv_buf, C0, n0, m0):
    B, T, _ = h.shape
    z = h @ p['w_in']
    qa, ka, va, qm, km, vm, om, ifm, ga, gb = jnp.split(z, SPLIT_POINTS, axis=-1)
    qa = rope(qa.reshape(B, T, H_A, 2, DK_A), pos)
    ka = rope(ka.reshape(B, T, H_A, 2, DK_A), pos)
    va = va.reshape(B, T, H_A, DV_A)
    lam = (jnp.exp(jnp.sum(p['lambda_q1'].astype(F32) * p['lambda_k1'].astype(F32)))
           - jnp.exp(jnp.sum(p['lambda_q2'].astype(F32) * p['lambda_k2'].astype(F32))) + lam_init)
    if past is None:
        nb = T // Q_BLOCK

        def one_block(i):
            qs = lax.dynamic_slice_in_dim(qa, i * Q_BLOCK, Q_BLOCK, axis=1)
            qp = lax.dynamic_slice_in_dim(pos, i * Q_BLOCK, Q_BLOCK)
            return diff_attend(qs, ka, va, qp, pos, lam)

        o = lax.map(one_block, jnp.arange(nb))
        o = jnp.moveaxis(o, 0, 1).reshape(B, T, H_A, DV_A)
    else:
        k_all = jnp.concatenate([past[0].astype(ka.dtype), ka], axis=1)
        v_all = jnp.concatenate([past[1].astype(va.dtype), va], axis=1)
        o = diff_attend(qa, k_all, v_all, pos, jnp.arange(k_all.shape[1]), lam)
    o = rmsnorm(o, p['g_subln']) * (1.0 - lam_init)
    qk, new_buf = causal_conv(jnp.concatenate([qm, km], axis=-1), conv_buf, p['w_conv'], p['b_conv'])
    qm2, km2 = jnp.split(qk.astype(F32), 2, axis=-1)
    q = qm2.reshape(B, T, H_M, DH_M)
    k = km2.reshape(B, T, H_M, DH_M) * (DH_M ** -0.5)
    v = vm.astype(F32).reshape(B, T, H_M, DH_M)
    gates = ifm.astype(F32) + p['b_gates'].astype(F32)
    log_i = gates[..., :H_M]
    log_f = jax.nn.log_sigmoid(gates[..., H_M:])
    hm, C, n, m = mlstm_chunkwise(q, k, v, log_i, log_f, C0, n0, m0)
    hm = rmsnorm(hm, p['g_mnorm']) * jax.nn.sigmoid(om.astype(F32)).reshape(B, T, H_M, DH_M)
    ya = o.reshape(B, T, W_VA).astype(h.dtype) @ p['w_up_a']
    yb = hm.reshape(B, T, W_M).astype(h.dtype) @ p['w_up_b']
    y = (jax.nn.sigmoid(ga) * ya + jax.nn.sigmoid(gb) * yb) @ p['w_out']
    return y, (ka, va, new_buf, C, n, m)


def moe(h, p):
    B, T, D = h.shape
    xt = h.reshape(-1, D)
    N = xt.shape[0]
    A = N * TOP_K
    logits = (xt @ p['w_router']).astype(F32) + p['b_router'].astype(F32)
    topv, topi = lax.top_k(logits, TOP_K)
    wts = jax.nn.softmax(topv, axis=-1)
    flat_e = topi.reshape(-1).astype(jnp.int32)
    flat_tok = jnp.repeat(jnp.arange(N, dtype=jnp.int32), TOP_K)
    flat_w = wts.reshape(-1)
    order = jnp.argsort(flat_e)
    se = flat_e[order]
    counts = jnp.bincount(flat_e, length=N_EXP)
    padded = (counts + MOE_BLOCK - 1) // MOE_BLOCK * MOE_BLOCK
    pad_end = jnp.cumsum(padded)
    pad_start = pad_end - padded
    start = jnp.cumsum(counts) - counts
    dest = pad_start[se] + jnp.arange(A, dtype=jnp.int32) - start[se]
    nblk = -(-(A + N_EXP * (MOE_BLOCK - 1)) // MOE_BLOCK)
    P = nblk * MOE_BLOCK
    tok_slot = jnp.full((P,), N, jnp.int32).at[dest].set(flat_tok[order])
    w_slot = jnp.zeros((P,), F32).at[dest].set(flat_w[order])
    blk_exp = jnp.minimum(jnp.searchsorted(pad_end, jnp.arange(nblk) * MOE_BLOCK, side='right'), N_EXP - 1)
    xpad = jnp.concatenate([xt, jnp.zeros((1, D), xt.dtype)], axis=0)

    def run(args):
        tok, w, e = args
        xb = xpad[tok]
        gu = xb @ p['w_gu'][e] + p['b_gu'][e]
        gate = jnp.minimum(gu[:, :D_FF], SWIGLU_LIMIT)
        up = jnp.clip(gu[:, D_FF:], -SWIGLU_LIMIT, SWIGLU_LIMIT)
        act = (up + 1.0) * gate * jax.nn.sigmoid(SWIGLU_ALPHA * gate)
        return (act @ p['w_down'][e] + p['b_down'][e]).astype(F32) * w[:, None]

    yb = lax.map(run, (tok_slot.reshape(nblk, MOE_BLOCK), w_slot.reshape(nblk, MOE_BLOCK), blk_exp))
    out = jnp.zeros((N + 1, D), F32).at[tok_slot].add(yb.reshape(P, D))
    return out[:N].astype(h.dtype).reshape(B, T, D)


def layer(x, c, p, lam_init, pos, past, conv_buf, C0, n0, m0):
    mod = (jax.nn.silu(c) @ p['w_ada'] + p['b_ada'])[:, None, :]
    sh1, sc1, gt1, sh2, sc2, gt2 = jnp.split(mod, 6, axis=-1)
    h = rmsnorm(x, p['g_norm1']) * (1.0 + sc1) + sh1
    y, st = mixer(h, p, lam_init, pos, past, conv_buf, C0, n0, m0)
    x = x + gt1 * y
    h = rmsnorm(x, p['g_norm2']) * (1.0 + sc2) + sh2
    x = x + gt2 * moe(h, p)
    return x, st


def setup_inputs(seed: int = 0) -> dict:
    key = jax.random.key(seed)
    ks = iter(jax.random.split(key, 48))
    n_pages = PAST_LEN // PAGE_SIZE
    n_pool = (5 * DEC_BATCH * n_pages + 3) // 4

    def nrm(shape, s):
        return jax.random.normal(next(ks), shape, F32) * s

    perm = jax.random.permutation(next(ks), n_pool)[: DEC_BATCH * n_pages]
    b_gates = jnp.concatenate([
        nrm((DEPTH, H_M), 0.1),
        jnp.broadcast_to(jnp.linspace(3.0, 6.0, H_M, dtype=F32), (DEPTH, H_M)) + nrm((DEPTH, H_M), 0.01)], axis=-1)
    return {
        'x_prompt': nrm((BATCH, SEQ, D_MODEL), 1.0),
        'x_sample': nrm((DEC_BATCH, DEC_SEQ, D_MODEL), 1.0),
        'c_prompt': nrm((BATCH, D_MODEL), 1.0),
        'c_sample': nrm((DEC_BATCH, D_MODEL), 1.0),
        'cache_k': nrm((DEPTH, n_pool, PAGE_SIZE, H_A, 2, DK_A), 1.0),
        'cache_v': nrm((DEPTH, n_pool, PAGE_SIZE, H_A, DV_A), 1.0),
        'state_conv': nrm((DEPTH, DEC_BATCH, CONV_W - 1, 2 * W_M), 1.0),
        'state_C': nrm((DEPTH, DEC_BATCH, H_M, DH_M, DH_M), 0.05),
        'state_n': nrm((DEPTH, DEC_BATCH, H_M, DH_M), 0.05),
        'state_m': nrm((DEPTH, DEC_BATCH, H_M), 0.5),
        'page_table': perm.reshape(DEC_BATCH, n_pages).astype(jnp.int32),
        'w_ada': nrm((DEPTH, D_MODEL, 6 * D_MODEL), 0.5 * D_MODEL ** -0.5),
        'b_ada': nrm((DEPTH, 6 * D_MODEL), 0.01),
        'g_norm1': 1.0 + nrm((DEPTH, D_MODEL), 0.02),
        'g_norm2': 1.0 + nrm((DEPTH, D_MODEL), 0.02),
        'w_in': nrm((DEPTH, D_MODEL, D_IN), D_MODEL ** -0.5),
        'b_gates': b_gates,
        'lambda_q1': nrm((DEPTH, DK_A), 0.1),
        'lambda_k1': nrm((DEPTH, DK_A), 0.1),
        'lambda_q2': nrm((DEPTH, DK_A), 0.1),
        'lambda_k2': nrm((DEPTH, DK_A), 0.1),
        'g_subln': 1.0 + nrm((DEPTH, DV_A), 0.02),
        'w_conv': nrm((DEPTH, CONV_W, 2 * W_M), CONV_W ** -0.5),
        'b_conv': nrm((DEPTH, 2 * W_M), 0.01),
        'g_mnorm': 1.0 + nrm((DEPTH, H_M, DH_M), 0.02),
        'w_up_a': nrm((DEPTH, W_VA, D_MODEL), W_VA ** -0.5),
        'w_up_b': nrm((DEPTH, W_M, D_MODEL), W_M ** -0.5),
        'w_out': nrm((DEPTH, D_MODEL, D_MODEL), D_MODEL ** -0.5),
        'w_router': nrm((DEPTH, D_MODEL, N_EXP), D_MODEL ** -0.5),
        'b_router': nrm((DEPTH, N_EXP), 0.01),
        'w_gu': nrm((DEPTH, N_EXP, D_MODEL, 2 * D_FF), D_MODEL ** -0.5),
        'b_gu': nrm((DEPTH, N_EXP, 2 * D_FF), 0.01),
        'w_down': nrm((DEPTH, N_EXP, D_FF, D_MODEL), D_FF ** -0.5),
        'b_down': nrm((DEPTH, N_EXP, D_MODEL), 0.01),
        'g_final': 1.0 + nrm((D_MODEL,), 0.02),
    }


def reference(x_prompt, x_sample, c_prompt, c_sample, cache_k, cache_v, state_conv, state_C, state_n, state_m,
              page_table, w_ada, b_ada, g_norm1, g_norm2, w_in, b_gates, lambda_q1, lambda_k1, lambda_q2, lambda_k2,
              g_subln, w_conv, b_conv, g_mnorm, w_up_a, w_up_b, w_out, w_router, b_router, w_gu, b_gu, w_down,
              b_down, g_final):
    B, S, _ = x_prompt.shape
    Bd, Td, _ = x_sample.shape
    past_len = page_table.shape[1] * PAGE_SIZE
    pos_p = jnp.arange(S, dtype=jnp.int32)
    pos_s = past_len + jnp.arange(Td, dtype=jnp.int32)
    hp, hs = x_prompt, x_sample
    kp_l, vp_l, bp_l, Cp_l, np_l, mp_l = [], [], [], [], [], []
    ks_l, vs_l, bs_l, Cs_l, ns_l, ms_l = [], [], [], [], [], []
    for l in range(DEPTH):
        p = dict(w_ada=w_ada[l], b_ada=b_ada[l], g_norm1=g_norm1[l], g_norm2=g_norm2[l], w_in=w_in[l],
                 b_gates=b_gates[l], lambda_q1=lambda_q1[l], lambda_k1=lambda_k1[l], lambda_q2=lambda_q2[l],
                 lambda_k2=lambda_k2[l], g_subln=g_subln[l], w_conv=w_conv[l], b_conv=b_conv[l],
                 g_mnorm=g_mnorm[l], w_up_a=w_up_a[l], w_up_b=w_up_b[l], w_out=w_out[l], w_router=w_router[l],
                 b_router=b_router[l], w_gu=w_gu[l], b_gu=b_gu[l], w_down=w_down[l], b_down=b_down[l])
        lam_init = 0.8 - 0.6 * math.exp(-0.3 * l)
        hp, (kp, vp, bp, Cp, np_, mp) = layer(
            hp, c_prompt, p, lam_init, pos_p, None,
            jnp.zeros((B, CONV_W - 1, 2 * W_M), hp.dtype), jnp.zeros((B, H_M, DH_M, DH_M), F32),
            jnp.zeros((B, H_M, DH_M), F32), jnp.zeros((B, H_M), F32))
        past = (cache_k[l][page_table].reshape(Bd, past_len, H_A, 2, DK_A),
                cache_v[l][page_table].reshape(Bd, past_len, H_A, DV_A))
        hs, (ksn, vsn, bsn, Csn, nsn, msn) = layer(
            hs, c_sample, p, lam_init, pos_s, past, state_conv[l], state_C[l], state_n[l], state_m[l])
        kp_l.append(kp); vp_l.append(vp); bp_l.append(bp); Cp_l.append(Cp); np_l.append(np_); mp_l.append(mp)
        ks_l.append(ksn); vs_l.append(vsn); bs_l.append(bsn); Cs_l.append(Csn); ns_l.append(nsn); ms_l.append(msn)
    y_prompt = rmsnorm(hp, g_final)
    y_sample = rmsnorm(hs, g_final)
    return (y_prompt, y_sample,
            jnp.stack(kp_l), jnp.stack(vp_l), jnp.stack(bp_l), jnp.stack(Cp_l), jnp.stack(np_l), jnp.stack(mp_l),
            jnp.stack(ks_l), jnp.stack(vs_l), jnp.stack(bs_l), jnp.stack(Cs_l), jnp.stack(ns_l), jnp.stack(ms_l))
```

```python
import functools
import math

import numpy as np
import jax
import jax.numpy as jnp
from jax import lax
from jax.experimental import pallas as pl
from jax.experimental.pallas import tpu as pltpu

F32 = jnp.float32
BF16 = jnp.bfloat16

D_MODEL = 1024
H_A = 4
DK_A = 64
DV_A = 2 * DK_A
ROPE_THETA = 10000.0
H_M = 4
DH_M = 128
CONV_W = 4
N_EXP = 32
TOP_K = 4
D_FF = D_MODEL
SWIGLU_LIMIT = 7.0
SWIGLU_ALPHA = 1.702
EPS = 1e-6
PAGE_SIZE = 128

W_QA = H_A * 2 * DK_A
W_VA = H_A * DV_A
W_M = H_M * DH_M
N_GATE = 2 * H_M

LANES = 128
ROW_ALIGN = 16
TOK_TILE = 256
FFN_TILE = 256
ATT_TILE = 256
MLSTM_CHUNK = 256
PAGES_PER_STEP = 8
NEG_BIG = -1e30
VMEM_LIMIT = 56 * 1024 * 1024

C_QA, C_KA, C_VA, C_QK, C_VM, C_OM, C_GA, C_GB, C_GT = 0, 512, 1024, 1536, 2560, 3072, 3584, 4608, 5632
D_IN_PACKED = C_GT + LANES
M_LOC = ((TOK_TILE * TOP_K + N_EXP * (ROW_ALIGN - 1)) + 255) // 256 * 256


def _dot(a, b):
    return jnp.dot(a, b, preferred_element_type=F32)


def _dot_nt(a, b):
    return lax.dot_general(a, b, (((1,), (1,)), ((), ())), preferred_element_type=F32)


def _sigmoid(x):
    return 1.0 / (1.0 + jnp.exp(-x))


def _params(sem):
    return pltpu.CompilerParams(dimension_semantics=sem, vmem_limit_bytes=VMEM_LIMIT)


def _ada_body(c_ref, w_ref, b_ref, o_ref):
    c = c_ref[...]
    s = c * _sigmoid(c)
    s_hi = s.astype(BF16)
    s_lo = (s - s_hi.astype(F32)).astype(BF16)
    w = w_ref[...]
    w_hi = w.astype(BF16)
    w_lo = (w - w_hi.astype(F32)).astype(BF16)
    o_ref[...] = _dot(s_hi, w_hi) + _dot(s_lo, w_hi) + _dot(s_hi, w_lo) + b_ref[...]


def _ada(c_all, w_ada, b_ada):
    rows = c_all.shape[0]
    n_out = w_ada.shape[1]
    blk = 1024
    return pl.pallas_call(
        _ada_body,
        grid=(n_out // blk,),
        in_specs=[pl.BlockSpec((rows, D_MODEL), lambda j: (0, 0)),
                  pl.BlockSpec((D_MODEL, blk), lambda j: (0, j)),
                  pl.BlockSpec((1, blk), lambda j: (0, j))],
        out_specs=pl.BlockSpec((rows, blk), lambda j: (0, j)),
        out_shape=jax.ShapeDtypeStruct((rows, n_out), F32),
        compiler_params=_params(("arbitrary",)),
        name="ada",
    )(c_all, w_ada, b_ada.reshape(1, n_out))


def _rope(z, cos, sin):
    lane = lax.broadcasted_iota(jnp.int32, (z.shape[0], LANES), 1)
    first_half = (lane % DK_A) < (DK_A // 2)
    out = []
    for h in range(H_A):
        xh = z[:, h * LANES:(h + 1) * LANES]
        partner = jnp.where(first_half, pltpu.roll(xh, LANES - DK_A // 2, 1), pltpu.roll(xh, DK_A // 2, 1))
        out.append(xh * cos + partner * sin)
    return jnp.concatenate(out, axis=1)


def _inproj_body(x_ref, sc_ref, sh_ref, g_ref, cos_ref, sin_ref, w_ref,
                 q_ref, k_ref, kb_ref, v_ref, vb_ref, qk_ref, vm_ref, om_ref, ga_ref, gb_ref, gt_ref):
    x = x_ref[...]
    ms = jnp.mean(x * x, axis=1, keepdims=True)
    h = (x * lax.rsqrt(ms + EPS)) * g_ref[...] * (1.0 + sc_ref[...]) + sh_ref[...]
    hb = h.astype(BF16)
    cos = cos_ref[...]
    sin = sin_ref[...]

    def seg(lo, n):
        return _dot(hb, w_ref[:, lo:lo + n])

    q = _rope(seg(C_QA, W_QA), cos, sin) * (DK_A ** -0.5)
    q_ref[...] = q.astype(BF16)
    k = _rope(seg(C_KA, W_QA), cos, sin)
    k_ref[...] = k
    kb_ref[...] = k.astype(BF16)
    v = seg(C_VA, W_VA)
    v_ref[...] = v
    vb_ref[...] = v.astype(BF16)
    qk_ref[...] = seg(C_QK, 2 * W_M)
    vm_ref[...] = seg(C_VM, W_M).astype(BF16)
    om_ref[...] = seg(C_OM, W_M)
    ga_ref[...] = seg(C_GA, D_MODEL)
    gb_ref[...] = seg(C_GB, D_MODEL)
    gt_ref[...] = seg(C_GT, LANES)


def _inproj(x, sc, sh, g1, cos, sin, w_packed, rows_per_mod):
    n = x.shape[0]
    nt = n // TOK_TILE
    tiles_per_group = nt // sc.shape[0]
    tab_tiles = cos.shape[0] // TOK_TILE
    tok = lambda w: pl.BlockSpec((TOK_TILE, w), lambda i: (i, 0))
    mod = pl.BlockSpec((None, rows_per_mod, D_MODEL), lambda i: (i // tiles_per_group, 0, 0))
    tab = pl.BlockSpec((TOK_TILE, LANES), lambda i: (i % tab_tiles, 0))
    widths = [(W_QA, BF16), (W_QA, F32), (W_QA, BF16), (W_VA, F32), (W_VA, BF16), (2 * W_M, F32),
              (W_M, BF16), (W_M, F32), (D_MODEL, F32), (D_MODEL, F32), (LANES, F32)]
    return pl.pallas_call(
        _inproj_body,
        grid=(nt,),
        in_specs=[tok(D_MODEL), mod, mod, pl.BlockSpec((1, D_MODEL), lambda i: (0, 0)), tab, tab,
                  pl.BlockSpec((D_MODEL, D_IN_PACKED), lambda i: (0, 0), pipeline_mode=pl.Buffered(1))],
        out_specs=[tok(w) for w, _ in widths],
        out_shape=[jax.ShapeDtypeStruct((n, w), dt) for w, dt in widths],
        compiler_params=_params(("arbitrary",)),
        name="inproj",
    )(x, sc, sh, g1, cos, sin, w_packed)


def _rope_tables(pos):
    half = DK_A // 2
    inv = ROPE_THETA ** (-np.arange(half, dtype=np.float64) * 2.0 / DK_A)
    ang = np.asarray(pos, np.float64)[:, None] * inv[None, :]
    cos = np.cos(ang)
    sin = np.sin(ang)
    cos64 = np.concatenate([cos, cos], axis=1)
    sin64 = np.concatenate([-sin, sin], axis=1)
    return (jnp.asarray(np.tile(cos64, (1, LANES // DK_A)), F32),
            jnp.asarray(np.tile(sin64, (1, LANES // DK_A)), F32))


def _lambda_value(lam_ref, lam_init):
    lv = lam_ref[...]
    l1 = jnp.sum(lv[0:1, :] * lv[1:2, :], axis=1, keepdims=True)
    l2 = jnp.sum(lv[2:3, :] * lv[3:4, :], axis=1, keepdims=True)
    return jnp.exp(l1) - jnp.exp(l2) + lam_init


def _subln(o, g, lam_init):
    ms = jnp.mean(o * o, axis=1, keepdims=True)
    return (o * lax.rsqrt(ms + EPS)) * g * (1.0 - lam_init)


def _attn_prompt_body(q_ref, k_ref, v_ref, lam_ref, g_ref, o_ref, m_scr, l_scr, acc_scr, *, lam_init):
    i = pl.program_id(2)
    tq = ATT_TILE
    q = q_ref[...]
    lane = lax.broadcasted_iota(jnp.int32, q.shape, 1)
    zero = jnp.zeros_like(q)
    qs = jnp.concatenate([jnp.where(lane < DK_A, q, zero), jnp.where(lane >= DK_A, q, zero)], axis=0)
    m_scr[...] = jnp.full(m_scr.shape, NEG_BIG, F32)
    l_scr[...] = jnp.zeros(l_scr.shape, F32)
    acc_scr[...] = jnp.zeros(acc_scr.shape, F32)

    def update(s, v):
        m_old = m_scr[...]
        m_new = jnp.maximum(m_old, jnp.max(s, axis=1, keepdims=True))
        alpha = jnp.exp(m_old - m_new)
        p = jnp.exp(s - m_new)
        l_scr[...] = alpha * l_scr[...] + jnp.sum(p, axis=1, keepdims=True)
        acc_scr[...] = alpha * acc_scr[...] + _dot(p.astype(BF16), v)
        m_scr[...] = m_new

    def off_diag(j, carry):
        start = pl.multiple_of(j * tq, tq)
        update(_dot_nt(qs, k_ref[pl.ds(start, tq), :]), v_ref[pl.ds(start, tq), :])
        return carry

    lax.fori_loop(0, i, off_diag, 0)
    start = pl.multiple_of(i * tq, tq)
    s = _dot_nt(qs, k_ref[pl.ds(start, tq), :])
    row = lax.broadcasted_iota(jnp.int32, s.shape, 0) % tq
    col = lax.broadcasted_iota(jnp.int32, s.shape, 1)
    update(jnp.where(col <= row, s, NEG_BIG), v_ref[pl.ds(start, tq), :])

    o = acc_scr[...] / l_scr[...]
    lam = _lambda_value(lam_ref, lam_init)
    o_ref[...] = _subln(o[:tq] - lam * o[tq:], g_ref[...], lam_init).astype(BF16)


def _attn_prompt(q, k, v, lam_vec, g_subln, batch, seq, lam_init):
    nq = seq // ATT_TILE
    kv = pl.BlockSpec((seq, LANES), lambda b, h, i: (b, h))
    qo = pl.BlockSpec((ATT_TILE, LANES), lambda b, h, i: (b * nq + i, h))
    return pl.pallas_call(
        functools.partial(_attn_prompt_body, lam_init=lam_init),
        grid=(batch, H_A, nq),
        in_specs=[qo, kv, kv, pl.BlockSpec((4, DK_A), lambda b, h, i: (0, 0)),
                  pl.BlockSpec((1, DV_A), lambda b, h, i: (0, 0))],
        out_specs=qo,
        out_shape=jax.ShapeDtypeStruct((batch * seq, W_VA), BF16),
        scratch_shapes=[pltpu.VMEM((2 * ATT_TILE, 1), F32), pltpu.VMEM((2 * ATT_TILE, 1), F32),
                        pltpu.VMEM((2 * ATT_TILE, DV_A), F32)],
        compiler_params=_params(("arbitrary", "arbitrary", "arbitrary")),
        name="attn_prompt",
    )(q, k, v, lam_vec, g_subln)


def _attn_sample_body(pt_ref, q_ref, kn_ref, vn_ref, lam_ref, g_ref, *rest, lam_init, n_steps, t_new):
    k_refs = rest[:PAGES_PER_STEP]
    v_refs = rest[PAGES_PER_STEP:2 * PAGES_PER_STEP]
    o_ref = rest[2 * PAGES_PER_STEP]
    qbd_scr, m_scr, l_scr, acc_scr = rest[2 * PAGES_PER_STEP + 1:]
    s_idx = pl.program_id(1)
    n_rows = 2 * H_A * t_new

    @pl.when(s_idx == 0)
    def _():
        q = q_ref[...].astype(F32)
        qt = jnp.concatenate([q] * (2 * H_A), axis=0)
        row = lax.broadcasted_iota(jnp.int32, qt.shape, 0)
        col = lax.broadcasted_iota(jnp.int32, qt.shape, 1)
        qbd_scr[...] = jnp.where(col // DK_A == row // t_new, qt, 0.0).astype(BF16)
        m_scr[...] = jnp.full(m_scr.shape, NEG_BIG, F32)
        l_scr[...] = jnp.zeros(l_scr.shape, F32)
        acc_scr[...] = jnp.zeros(acc_scr.shape, F32)

    def update(s, v):
        m_old = m_scr[...]
        m_new = jnp.maximum(m_old, jnp.max(s, axis=1, keepdims=True))
        alpha = jnp.exp(m_old - m_new)
        p = jnp.exp(s - m_new)
        l_scr[...] = alpha * l_scr[...] + jnp.sum(p, axis=1, keepdims=True)
        acc_scr[...] = alpha * acc_scr[...] + _dot(p.astype(BF16), v)
        m_scr[...] = m_new

    qbd = qbd_scr[...]
    kp = jnp.concatenate([r[...].astype(BF16) for r in k_refs], axis=0)
    vp = jnp.concatenate([r[...].astype(BF16) for r in v_refs], axis=0)
    update(_dot_nt(qbd, kp), vp)

    @pl.when(s_idx == n_steps - 1)
    def _():
        zpad = jnp.zeros((PAGE_SIZE - t_new, W_QA), F32)
        kn = jnp.concatenate([kn_ref[...].astype(F32), zpad], axis=0).astype(BF16)
        vn = jnp.concatenate([vn_ref[...].astype(F32), zpad], axis=0).astype(BF16)
        s = _dot_nt(qbd, kn)
        row = lax.broadcasted_iota(jnp.int32, s.shape, 0) % t_new
        col = lax.broadcasted_iota(jnp.int32, s.shape, 1)
        update(jnp.where(col <= row, s, NEG_BIG), vn)
        o = acc_scr[...] / l_scr[...]
        lam = _lambda_value(lam_ref, lam_init)
        outs = []
        for h in range(H_A):
            r0 = 2 * h * t_new
            o0 = o[r0:r0 + t_new, h * DV_A:(h + 1) * DV_A]
            o1 = o[r0 + t_new:r0 + 2 * t_new, h * DV_A:(h + 1) * DV_A]
            outs.append(_subln(o0 - lam * o1, g_ref[...], lam_init))
        o_ref[...] = jnp.concatenate(outs, axis=1).astype(BF16)


def _attn_sample(q, k_new, v_new, cache_k, cache_v, page_table, lam_vec, g_subln, lam_init, t_new):
    bd, n_pages = page_table.shape
    n_steps = n_pages // PAGES_PER_STEP
    n_rows = 2 * H_A * t_new
    new = pl.BlockSpec((None, t_new, W_QA), lambda b, s, pt: (b, 0, 0))
    page = lambda j: pl.BlockSpec((None, PAGE_SIZE, W_QA),
                                  lambda b, s, pt, j=j: (pt[b, s * PAGES_PER_STEP + j], 0, 0))
    grid_spec = pltpu.PrefetchScalarGridSpec(
        num_scalar_prefetch=1,
        grid=(bd, n_steps),
        in_specs=[new, new, new, pl.BlockSpec((4, DK_A), lambda b, s, pt: (0, 0)),
                  pl.BlockSpec((1, DV_A), lambda b, s, pt: (0, 0))]
                 + [page(j) for j in range(PAGES_PER_STEP)] * 2,
        out_specs=new,
        scratch_shapes=[pltpu.VMEM((n_rows, W_QA), BF16), pltpu.VMEM((n_rows, 1), F32),
                        pltpu.VMEM((n_rows, 1), F32), pltpu.VMEM((n_rows, W_VA), F32)],
    )
    return pl.pallas_call(
        functools.partial(_attn_sample_body, lam_init=lam_init, n_steps=n_steps, t_new=t_new),
        grid_spec=grid_spec,
        out_shape=jax.ShapeDtypeStruct((bd, t_new, W_VA), BF16),
        compiler_params=_params(("arbitrary", "arbitrary")),
        name="attn_sample",
    )(page_table, q.reshape(bd, t_new, W_QA), k_new.reshape(bd, t_new, W_QA), v_new.reshape(bd, t_new, W_VA),
      lam_vec, g_subln, *([cache_k] * PAGES_PER_STEP), *([cache_v] * PAGES_PER_STEP)).reshape(bd * t_new, W_VA)


def _mlstm_body(qk_ref, vm_ref, om_ref, gt_ref, cw_ref, cb_ref, bg_ref, gm_ref, cbuf_ref, c0_ref, n0_ref, m0_ref,
                h_ref, cst_ref, cout_ref, nout_ref, mout_ref, ext_scr, c_scr, n_scr, m_scr, *, tb, L, nc):
    c_idx = pl.program_id(1)

    @pl.when(c_idx == 0)
    def _():
        ext_scr[...] = jnp.zeros(ext_scr.shape, F32)
        ext_scr[8 - (CONV_W - 1):8, :] = cbuf_ref[...]
        c_scr[...] = c0_ref[...]
        n_scr[...] = n0_ref[...]
        m_scr[...] = m0_ref[...]

    pad = L - tb
    u = qk_ref[...]
    if pad:
        u = jnp.concatenate([u, jnp.zeros((pad, u.shape[1]), F32)], axis=0)
    full = jnp.concatenate([ext_scr[...], u], axis=0)
    conv = cb_ref[...] + cw_ref[CONV_W - 1:CONV_W, :] * u
    for j in range(CONV_W - 1):
        conv = conv + cw_ref[j:j + 1, :] * pltpu.roll(full, CONV_W - 1 - j, 0)[8:8 + L]
    a = conv * _sigmoid(conv)
    if not pad:
        ext_scr[...] = u[L - 8:L]

    @pl.when(c_idx == nc - 1)
    def _():
        cst_ref[...] = qk_ref[tb - (CONV_W - 1):tb, :]

    g = gt_ref[...] + bg_ref[...]
    li = g
    lf = jnp.minimum(g, 0.0) - jnp.log1p(jnp.exp(-jnp.abs(g)))
    if pad:
        zpad = jnp.zeros((pad, LANES), F32)
        li = jnp.concatenate([li, zpad + NEG_BIG], axis=0)
        lf = jnp.concatenate([lf, zpad], axis=0)
    row = lax.broadcasted_iota(jnp.int32, (L, LANES), 0)
    lane = lax.broadcasted_iota(jnp.int32, (L, LANES), 1)
    bcum = lf
    shift = 1
    while shift < L:
        bcum = bcum + jnp.where(row >= shift, pltpu.roll(bcum, shift, 0), 0.0)
        shift *= 2
    gates = jnp.where(lane < H_M, li, bcum)
    gates_t = gates.T
    tri = lax.broadcasted_iota(jnp.int32, (L, L), 0) >= lax.broadcasted_iota(jnp.int32, (L, L), 1)
    m_all = m_scr[...]
    lane1 = lax.broadcasted_iota(jnp.int32, (1, LANES), 1)
    m_next = m_all
    vall = vm_ref[...]
    if pad:
        vall = jnp.concatenate([vall, jnp.zeros((pad, vall.shape[1]), BF16)], axis=0)

    for h in range(H_M):
        li_col = gates[:, h:h + 1]
        b_col = gates[:, H_M + h:H_M + h + 1]
        li_row = gates_t[h:h + 1, :]
        b_row = gates_t[H_M + h:H_M + h + 1, :]
        m_prev = m_all[:, h:h + 1]
        b_last = b_col[L - 1:L, :]
        log_d = jnp.where(tri, b_col - b_row + li_row, NEG_BIG)
        inter = b_col + m_prev
        mt = jnp.maximum(inter, jnp.max(log_d, axis=1, keepdims=True))
        q = a[:, h * DH_M:(h + 1) * DH_M]
        k = a[:, W_M + h * DH_M:W_M + (h + 1) * DH_M] * (DH_M ** -0.5)
        v = vall[:, h * DH_M:(h + 1) * DH_M]
        qb = q.astype(BF16)
        s = _dot_nt(qb, k.astype(BF16)) * jnp.exp(log_d - mt)
        ei = jnp.exp(inter - mt)
        c_old = c_scr[h]
        n_old = n_scr[h:h + 1, :]
        num = ei * _dot(qb, c_old.astype(BF16)) + _dot(s.astype(BF16), v)
        den = ei * jnp.sum(q * n_old, axis=1, keepdims=True) + jnp.sum(s, axis=1, keepdims=True)
        hh = num / jnp.maximum(jnp.abs(den), jnp.exp(-mt))
        g_col = b_last - b_col + li_col
        bl = b_last + m_prev
        m_new = jnp.maximum(bl, jnp.max(g_col, axis=0, keepdims=True))
        wg = jnp.exp(g_col - m_new)
        decay = jnp.exp(bl - m_new)
        kw = k * wg
        c_scr[h] = decay * c_old + _dot(kw.T.astype(BF16), v)
        n_scr[h:h + 1, :] = decay * n_old + jnp.sum(kw, axis=0, keepdims=True)
        m_next = jnp.where(lane1 == h, m_new, m_next)
        ms = jnp.mean(hh * hh, axis=1, keepdims=True)
        hn = (hh * lax.rsqrt(ms + EPS)) * gm_ref[h:h + 1, :]
        og = _sigmoid(om_ref[:, h * DH_M:(h + 1) * DH_M])
        h_ref[:, h * DH_M:(h + 1) * DH_M] = (hn[:tb] * og).astype(BF16)

    m_scr[...] = m_next

    @pl.when(c_idx == nc - 1)
    def _():
        cout_ref[...] = c_scr[...]
        nout_ref[...] = n_scr[...]
        mout_ref[...] = m_scr[...]


def _mlstm(qk, vm, om, gt, w_conv, b_conv, b_gates_pad, g_mnorm, conv_buf, c0, n0, m0_pad, batch, seq):
    tb = min(seq, MLSTM_CHUNK)
    L = max(tb, LANES)
    nc = seq // tb
    tok = lambda w: pl.BlockSpec((None, tb, w), lambda b, c: (b * nc + c, 0, 0))
    chunks = lambda a: a.reshape(batch * nc, tb, a.shape[-1])
    const = lambda shape: pl.BlockSpec(shape, lambda b, c: (0,) * len(shape))
    per_b = lambda shape: pl.BlockSpec((None,) + shape, lambda b, c: (b,) + (0,) * len(shape))
    h, cst, c_out, n_out, m_out = pl.pallas_call(
        functools.partial(_mlstm_body, tb=tb, L=L, nc=nc),
        grid=(batch, nc),
        in_specs=[tok(2 * W_M), tok(W_M), tok(W_M), tok(LANES), const((CONV_W, 2 * W_M)), const((1, 2 * W_M)),
                  const((1, LANES)), const((H_M, DH_M)), per_b((CONV_W - 1, 2 * W_M)),
                  per_b((H_M, DH_M, DH_M)), per_b((H_M, DH_M)), per_b((1, LANES))],
        out_specs=[tok(W_M), per_b((CONV_W - 1, 2 * W_M)), per_b((H_M, DH_M, DH_M)), per_b((H_M, DH_M)),
                   per_b((1, LANES))],
        out_shape=[jax.ShapeDtypeStruct((batch * nc, tb, W_M), BF16),
                   jax.ShapeDtypeStruct((batch, CONV_W - 1, 2 * W_M), F32),
                   jax.ShapeDtypeStruct((batch, H_M, DH_M, DH_M), F32),
                   jax.ShapeDtypeStruct((batch, H_M, DH_M), F32),
                   jax.ShapeDtypeStruct((batch, 1, LANES), F32)],
        scratch_shapes=[pltpu.VMEM((8, 2 * W_M), F32), pltpu.VMEM((H_M, DH_M, DH_M), F32),
                        pltpu.VMEM((H_M, DH_M), F32), pltpu.VMEM((1, LANES), F32)],
        compiler_params=_params(("arbitrary", "arbitrary")),
        name="mlstm",
    )(chunks(qk), chunks(vm), chunks(om), chunks(gt), w_conv, b_conv, b_gates_pad, g_mnorm, conv_buf, c0, n0,
      m0_pad)
    return h.reshape(batch * seq, W_M), cst, c_out, n_out, m_out


def _merge_body(oa_ref, hm_ref, ga_ref, gb_ref, x_ref, gt1_ref, sc2_ref, sh2_ref, g2_ref, wa_ref, wb_ref, wo_ref,
                wr_ref, br_ref, *rest):
    x2_ref, h2_ref, route_ref, cnt_ref = rest[-4:]
    ya = _dot(oa_ref[...], wa_ref[...])
    yb = _dot(hm_ref[...], wb_ref[...])
    mix = _sigmoid(ga_ref[...]) * ya + _sigmoid(gb_ref[...]) * yb
    y = _dot(mix.astype(BF16), wo_ref[...])
    x2 = x_ref[...] + gt1_ref[...] * y
    x2_ref[...] = x2
    ms = jnp.mean(x2 * x2, axis=1, keepdims=True)
    h2 = (x2 * lax.rsqrt(ms + EPS)) * g2_ref[...] * (1.0 + sc2_ref[...]) + sh2_ref[...]
    h2b = h2.astype(BF16)
    h2_ref[...] = h2b

    tm = h2b.shape[0]
    lane = lax.broadcasted_iota(jnp.int32, (tm, LANES), 1)
    logits = jnp.where(lane < N_EXP, _dot(h2b, wr_ref[...]) + br_ref[...], NEG_BIG)
    work = logits
    vals, hots = [], []
    for _ in range(TOP_K):
        mx = jnp.max(work, axis=1, keepdims=True)
        idx = jnp.min(jnp.where(work == mx, lane, LANES), axis=1, keepdims=True)
        hot = lane == idx
        vals.append(mx)
        hots.append(hot)
        work = jnp.where(hot, 2.0 * NEG_BIG, work)
    es = [jnp.exp(v - vals[0]) for v in vals]
    den = es[0]
    for e in es[1:]:
        den = den + e
    sel = jnp.zeros((tm, LANES), F32)
    for hot in hots:
        sel = jnp.where(hot, 1.0, sel)
    r_i = lax.broadcasted_iota(jnp.int32, (tm, tm), 0)
    c_i = lax.broadcasted_iota(jnp.int32, (tm, tm), 1)
    rank = _dot(jnp.where(c_i < r_i, 1.0, 0.0).astype(BF16), sel.astype(BF16))
    lane_f = lane.astype(F32)
    route = jnp.zeros((tm, LANES), F32)
    for k in range(TOP_K):
        e_k = jnp.sum(jnp.where(hots[k], lane_f, 0.0), axis=1, keepdims=True)
        r_k = jnp.sum(jnp.where(hots[k], rank, 0.0), axis=1, keepdims=True)
        route = jnp.where(lane == k, e_k, route)
        route = jnp.where(lane == TOP_K + k, es[k] / den, route)
        route = jnp.where(lane == 2 * TOP_K + k, r_k, route)
    route_ref[...] = route
    cnt_ref[...] = jnp.sum(sel, axis=0, keepdims=True)


def _merge(oa, hm, ga, gb, x, gt1, sc2, sh2, g2, wa, wb, wo, wr, br, rows_per_mod, tile0, n_all, prev=None):
    n = x.shape[0]
    nt = n // TOK_TILE
    nt_all = n_all // TOK_TILE
    tiles_per_group = nt // gt1.shape[0]
    tok = lambda w: pl.BlockSpec((TOK_TILE, w), lambda i: (i, 0))
    mod = pl.BlockSpec((None, rows_per_mod, D_MODEL), lambda i: (i // tiles_per_group, 0, 0))
    res = lambda shape: pl.BlockSpec(shape, lambda i: (0, 0), pipeline_mode=pl.Buffered(1))
    out_tok = lambda w: pl.BlockSpec((TOK_TILE, w), lambda i: (tile0 + i, 0))
    in_specs = [tok(W_VA), tok(W_M), tok(D_MODEL), tok(D_MODEL), tok(D_MODEL), mod, mod, mod,
                pl.BlockSpec((1, D_MODEL), lambda i: (0, 0)),
                res((W_VA, D_MODEL)), res((W_M, D_MODEL)), res((D_MODEL, D_MODEL)), res((D_MODEL, LANES)),
                pl.BlockSpec((1, LANES), lambda i: (0, 0))]
    args = [oa, hm, ga, gb, x, gt1, sc2, sh2, g2, wa, wb, wo, wr, br]
    aliases = {}
    if prev is not None:
        in_specs += [pl.BlockSpec(memory_space=pl.ANY)] * 4
        aliases = {len(args) + j: j for j in range(4)}
        args += list(prev)
    return pl.pallas_call(
        _merge_body,
        grid=(nt,),
        in_specs=in_specs,
        out_specs=[out_tok(D_MODEL), out_tok(D_MODEL), out_tok(LANES),
                   pl.BlockSpec((None, 1, LANES), lambda i: (tile0 + i, 0, 0))],
        out_shape=[jax.ShapeDtypeStruct((n_all, D_MODEL), F32), jax.ShapeDtypeStruct((n_all, D_MODEL), BF16),
                   jax.ShapeDtypeStruct((n_all, LANES), F32), jax.ShapeDtypeStruct((nt_all, 1, LANES), F32)],
        input_output_aliases=aliases,
        compiler_params=_params(("arbitrary",)),
        name="merge",
    )(*args)


def _segment_copies(src, dst, sem, src_row, dst_row, n_groups, max_groups):
    out = []
    bit = 1
    while bit * 2 <= max_groups:
        bit *= 2
    while bit >= 1:
        off = (n_groups // (2 * bit)) * (2 * bit) * ROW_ALIGN
        rows = bit * ROW_ALIGN
        cp = pltpu.make_async_copy(src.at[pl.ds(pl.multiple_of(src_row + off, ROW_ALIGN), rows)],
                                   dst.at[pl.ds(pl.multiple_of(dst_row + off, ROW_ALIGN), rows)], sem)
        out.append(((n_groups // bit) % 2 == 1, cp))
        bit //= 2
    return out


def _run_copies(copies):
    for pred, cp in copies:
        pl.when(pred)(cp.start)
    for pred, cp in copies:
        pl.when(pred)(cp.wait)


def _slot_rows(route_t, loff_col, k):
    e_row = route_t[k:k + 1, :]
    r_row = route_t[2 * TOP_K + k:2 * TOP_K + k + 1, :]
    sub = lax.broadcasted_iota(jnp.int32, (LANES, route_t.shape[1]), 0).astype(F32)
    return jnp.sum(jnp.where(sub == e_row, loff_col, 0.0), axis=0, keepdims=True) + r_row


def _dispatch_body(g_ref, n_ref, lo_ref, tail_ref, h2_ref, route_ref, loff_ref, xs_ref, loc_scr, zero_scr, sem,
                   *, nt):
    t = pl.program_id(0)
    route_t = route_ref[...].T
    loff_col = loff_ref[...]
    r_i = lax.broadcasted_iota(jnp.int32, (M_LOC, TOK_TILE), 0).astype(F32)
    hit = r_i == _slot_rows(route_t, loff_col, 0)
    for k in range(1, TOP_K):
        hit = jnp.logical_or(hit, r_i == _slot_rows(route_t, loff_col, k))
    onehot = jnp.where(hit, 1.0, 0.0).astype(BF16)
    loc_scr[...] = _dot(onehot, h2_ref[...]).astype(BF16)

    copies = []
    for e in range(N_EXP):
        n16 = (n_ref[t * N_EXP + e] + ROW_ALIGN - 1) // ROW_ALIGN
        copies += _segment_copies(loc_scr, xs_ref, sem, lo_ref[t * N_EXP + e], g_ref[t * N_EXP + e], n16,
                                  TOK_TILE // ROW_ALIGN)
    _run_copies(copies)

    @pl.when(t == nt - 1)
    def _():
        zero_scr[...] = jnp.zeros(zero_scr.shape, BF16)
        tails = []
        for e in range(N_EXP):
            tails += _segment_copies(zero_scr, xs_ref, sem, 0, tail_ref[e], tail_ref[N_EXP + e],
                                     FFN_TILE // ROW_ALIGN - 1)
        _run_copies(tails)


def _dispatch(g_off, cnt, l_off, tail, h2, route, loff_col, rows):
    nt = h2.shape[0] // TOK_TILE
    grid_spec = pltpu.PrefetchScalarGridSpec(
        num_scalar_prefetch=4,
        grid=(nt,),
        in_specs=[pl.BlockSpec((TOK_TILE, D_MODEL), lambda t, *_: (t, 0)),
                  pl.BlockSpec((TOK_TILE, LANES), lambda t, *_: (t, 0)),
                  pl.BlockSpec((None, LANES, 1), lambda t, *_: (t, 0, 0))],
        out_specs=pl.BlockSpec(memory_space=pl.ANY),
        scratch_shapes=[pltpu.VMEM((M_LOC, D_MODEL), BF16), pltpu.VMEM((FFN_TILE, D_MODEL), BF16),
                        pltpu.SemaphoreType.DMA],
    )
    return pl.pallas_call(
        functools.partial(_dispatch_body, nt=nt),
        grid_spec=grid_spec,
        out_shape=jax.ShapeDtypeStruct((rows, D_MODEL), BF16),
        compiler_params=_params(("arbitrary",)),
        name="moe_dispatch",
    )(g_off, cnt, l_off, tail, h2, route, loff_col)


def _ffn_body(be_ref, nu_ref, x_ref, wgu_ref, bgu_ref, wd_ref, bd_ref, y_ref, wgu_scr, wd_scr):
    i = pl.program_id(0)
    prev = be_ref[jnp.maximum(i - 1, 0)]

    @pl.when(jnp.logical_or(i == 0, be_ref[i] != prev))
    def _():
        wgu_scr[...] = wgu_ref[...].astype(BF16)
        wd_scr[...] = wd_ref[...].astype(BF16)

    @pl.when(i < nu_ref[0])
    def _():
        gu = _dot(x_ref[...], wgu_scr[...]) + bgu_ref[...]
        gate = jnp.minimum(gu[:, :D_FF], SWIGLU_LIMIT)
        up = jnp.clip(gu[:, D_FF:], -SWIGLU_LIMIT, SWIGLU_LIMIT)
        act = (up + 1.0) * gate * _sigmoid(SWIGLU_ALPHA * gate)
        y_ref[...] = (_dot(act.astype(BF16), wd_scr[...]) + bd_ref[...]).astype(BF16)


def _ffn(blk_exp, n_used, xs, w_gu, b_gu, w_down, b_down):
    rows = xs.shape[0]
    nblk = rows // FFN_TILE
    row_blk = pl.BlockSpec((FFN_TILE, D_MODEL), lambda i, be, nu: (jnp.minimum(i, nu[0] - 1), 0))
    grid_spec = pltpu.PrefetchScalarGridSpec(
        num_scalar_prefetch=2,
        grid=(nblk,),
        in_specs=[row_blk,
                  pl.BlockSpec((None, D_MODEL, 2 * D_FF), lambda i, be, nu: (be[i], 0, 0)),
                  pl.BlockSpec((None, 1, 2 * D_FF), lambda i, be, nu: (be[i], 0, 0)),
                  pl.BlockSpec((None, D_FF, D_MODEL), lambda i, be, nu: (be[i], 0, 0)),
                  pl.BlockSpec((None, 1, D_MODEL), lambda i, be, nu: (be[i], 0, 0))],
        out_specs=row_blk,
        scratch_shapes=[pltpu.VMEM((D_MODEL, 2 * D_FF), BF16), pltpu.VMEM((D_FF, D_MODEL), BF16)],
    )
    return pl.pallas_call(
        _ffn_body,
        grid_spec=grid_spec,
        out_shape=jax.ShapeDtypeStruct((rows, D_MODEL), BF16),
        compiler_params=_params(("arbitrary",)),
        name="moe_ffn",
    )(blk_exp, n_used, xs, w_gu, b_gu.reshape(N_EXP, 1, 2 * D_FF), w_down, b_down.reshape(N_EXP, 1, D_MODEL))


def _combine_body(g_ref, n_ref, lo_ref, route_ref, loff_ref, x2_ref, gtp_ref, gts_ref, gf_ref, ys_ref,
                  yp_ref, ysm_ref, loc_scr, sem, *, nt_prompt, final):
    t = pl.program_id(0)

    @pl.when(t == 0)
    def _():
        loc_scr[...] = jnp.zeros(loc_scr.shape, BF16)

    copies = []
    for e in range(N_EXP):
        n16 = (n_ref[t * N_EXP + e] + ROW_ALIGN - 1) // ROW_ALIGN
        copies += _segment_copies(ys_ref, loc_scr, sem, g_ref[t * N_EXP + e], lo_ref[t * N_EXP + e], n16,
                                  TOK_TILE // ROW_ALIGN)
    _run_copies(copies)

    route = route_ref[...]
    loff_row = loff_ref[...]
    lane = lax.broadcasted_iota(jnp.int32, (TOK_TILE, LANES), 1).astype(F32)
    c_i = lax.broadcasted_iota(jnp.int32, (TOK_TILE, M_LOC), 1).astype(F32)
    wmat = jnp.zeros((TOK_TILE, M_LOC), F32)
    for k in range(TOP_K):
        e_k = route[:, k:k + 1]
        slot = jnp.sum(jnp.where(lane == e_k, loff_row, 0.0), axis=1, keepdims=True) \
            + route[:, 2 * TOP_K + k:2 * TOP_K + k + 1]
        wmat = jnp.where(c_i == slot, route[:, TOP_K + k:TOP_K + k + 1], wmat)
    moe = _dot(wmat.astype(BF16), loc_scr[...])

    gate = jnp.where(t >= nt_prompt, gts_ref[...], gtp_ref[...])
    xo = x2_ref[...] + gate * moe
    if final:
        ms = jnp.mean(xo * xo, axis=1, keepdims=True)
        xo = (xo * lax.rsqrt(ms + EPS)) * gf_ref[...]

    @pl.when(t < nt_prompt)
    def _():
        yp_ref[...] = xo

    @pl.when(t >= nt_prompt)
    def _():
        ysm_ref[...] = xo


def _combine(g_off, cnt, l_off, route, loff_row, x2, gt2_p, gt2_s, g_final, ys, n_prompt, n_sample, final):
    nt = x2.shape[0] // TOK_TILE
    nt_prompt = n_prompt // TOK_TILE
    tiles_per_batch = nt_prompt // gt2_p.shape[0]
    grid_spec = pltpu.PrefetchScalarGridSpec(
        num_scalar_prefetch=3,
        grid=(nt,),
        in_specs=[pl.BlockSpec((TOK_TILE, LANES), lambda t, *_: (t, 0)),
                  pl.BlockSpec((None, 1, LANES), lambda t, *_: (t, 0, 0)),
                  pl.BlockSpec((TOK_TILE, D_MODEL), lambda t, *_: (t, 0)),
                  pl.BlockSpec((None, 1, D_MODEL),
                               lambda t, *_: (jnp.minimum(t, nt_prompt - 1) // tiles_per_batch, 0, 0)),
                  pl.BlockSpec((TOK_TILE, D_MODEL), lambda t, *_: (0, 0)),
                  pl.BlockSpec((1, D_MODEL), lambda t, *_: (0, 0)),
                  pl.BlockSpec(memory_space=pl.ANY)],
        out_specs=[pl.BlockSpec((TOK_TILE, D_MODEL), lambda t, *_: (jnp.minimum(t, nt_prompt - 1), 0)),
                   pl.BlockSpec((TOK_TILE, D_MODEL), lambda t, *_: (0, 0))],
        scratch_shapes=[pltpu.VMEM((M_LOC, D_MODEL), BF16), pltpu.SemaphoreType.DMA],
    )
    return pl.pallas_call(
        functools.partial(_combine_body, nt_prompt=nt_prompt, final=final),
        grid_spec=grid_spec,
        out_shape=[jax.ShapeDtypeStruct((n_prompt, D_MODEL), F32), jax.ShapeDtypeStruct((n_sample, D_MODEL), F32)],
        compiler_params=_params(("arbitrary",)),
        name="moe_combine",
    )(g_off, cnt, l_off, route, loff_row, x2, gt2_p, gt2_s, g_final, ys)


def _moe_offsets(cnt):
    nt = cnt.shape[0]
    seg = (cnt + ROW_ALIGN - 1) // ROW_ALIGN * ROW_ALIGN
    l_off = jnp.cumsum(seg, axis=1) - seg
    gsz = jnp.sum(seg, axis=0)
    gpad = (gsz + FFN_TILE - 1) // FFN_TILE * FFN_TILE
    gstart = jnp.cumsum(gpad) - gpad
    g_off = gstart[None, :] + jnp.cumsum(seg, axis=0) - seg
    nblk_e = gpad // FFN_TILE
    blk_end = jnp.cumsum(nblk_e)
    n_used = jnp.maximum(blk_end[-1], 1)
    rows = (nt * TOK_TILE * TOP_K + nt * N_EXP * (ROW_ALIGN - 1) + N_EXP * (FFN_TILE - 1)
            + FFN_TILE - 1) // FFN_TILE * FFN_TILE
    blk = jnp.arange(rows // FFN_TILE, dtype=jnp.int32)
    blk_exp = jnp.minimum(jnp.searchsorted(blk_end, jnp.minimum(blk, n_used - 1), side='right'), N_EXP - 1)
    tail = jnp.concatenate([gstart + gsz, (gpad - gsz) // ROW_ALIGN])
    i32 = lambda a: a.astype(jnp.int32)
    return (i32(g_off).reshape(-1), i32(cnt).reshape(-1), i32(l_off).reshape(-1), i32(tail), i32(blk_exp),
            i32(n_used).reshape(1), l_off.astype(F32), rows)


def _pack_w_in(w_in):
    pad = jnp.zeros((D_MODEL, LANES - N_GATE), w_in.dtype)
    g0 = C_GA
    return jnp.concatenate([w_in[:, :g0], w_in[:, g0 + N_GATE:], w_in[:, g0:g0 + N_GATE], pad],
                           axis=1).astype(BF16)


def _pad_lanes(a, value=0.0):
    return jnp.pad(a, [(0, 0)] * (a.ndim - 1) + [(0, LANES - a.shape[-1])], constant_values=value)


def kernel(x_prompt, x_sample, c_prompt, c_sample, cache_k, cache_v, state_conv, state_C, state_n, state_m, page_table, w_ada, b_ada, g_norm1, g_norm2, w_in, b_gates, lambda_q1, lambda_k1, lambda_q2, lambda_k2, g_subln, w_conv, b_conv, g_mnorm, w_up_a, w_up_b, w_out, w_router, b_router, w_gu, b_gu, w_down, b_down, g_final):
    B, S, D = x_prompt.shape
    Bd, Td, _ = x_sample.shape
    depth = w_in.shape[0]
    n_pool = cache_k.shape[1]
    past_len = page_table.shape[1] * PAGE_SIZE
    n_p, n_s = B * S, Bd * Td
    n_all = n_p + n_s
    assert D == D_MODEL and n_s == TOK_TILE and S % MLSTM_CHUNK == 0 and n_p % TOK_TILE == 0
    assert page_table.shape[1] % PAGES_PER_STEP == 0

    cos_p, sin_p = _rope_tables(np.arange(S))
    cos_s, sin_s = _rope_tables(np.tile(past_len + np.arange(Td), Bd))
    hp = x_prompt.reshape(n_p, D)
    hs = x_sample.reshape(n_s, D)
    c_all = jnp.concatenate([c_prompt, c_sample], axis=0)
    outs = [[] for _ in range(12)]

    for l in range(depth):
        lam_init = 0.8 - 0.6 * math.exp(-0.3 * l)
        mod = _ada(c_all, w_ada[l], b_ada[l])
        mods = [mod[:, j * D:(j + 1) * D] for j in range(6)]
        mp = [m[:B].reshape(B, 1, D) for m in mods]
        ms_ = [jnp.repeat(m[B:], Td, axis=0).reshape(1, n_s, D) for m in mods]
        w_packed = _pack_w_in(w_in[l])
        g1 = g_norm1[l].reshape(1, D)
        lam_vec = jnp.stack([lambda_q1[l], lambda_k1[l], lambda_q2[l], lambda_k2[l]])
        gsub = g_subln[l].reshape(1, DV_A)
        bg = _pad_lanes(b_gates[l].reshape(1, N_GATE))
        cw, cb = w_conv[l], b_conv[l].reshape(1, 2 * W_M)

        (q_p, k_p, kb_p, v_p, vb_p, qk_p, vm_p, om_p, ga_p, gb_p, gt_p) = _inproj(
            hp, mp[1], mp[0], g1, cos_p, sin_p, w_packed, 1)
        (q_s, k_s, kb_s, v_s, vb_s, qk_s, vm_s, om_s, ga_s, gb_s, gt_s) = _inproj(
            hs, ms_[1], ms_[0], g1, cos_s, sin_s, w_packed, n_s)

        oa_p = _attn_prompt(q_p, kb_p, vb_p, lam_vec, gsub, B, S, lam_init)
        oa_s = _attn_sample(q_s, kb_s, vb_s, cache_k[l].reshape(n_pool, PAGE_SIZE, W_QA),
                            cache_v[l].reshape(n_pool, PAGE_SIZE, W_VA), page_table, lam_vec, gsub, lam_init, Td)

        zeros = lambda *shape: jnp.zeros(shape, F32)
        hm_p, cst_p, C_p, nn_p, m_p = _mlstm(qk_p, vm_p, om_p, gt_p, cw, cb, bg, g_mnorm[l],
                                             zeros(B, CONV_W - 1, 2 * W_M), zeros(B, H_M, DH_M, DH_M),
                                             zeros(B, H_M, DH_M), zeros(B, 1, LANES), B, S)
        hm_s, cst_s, C_s, nn_s, m_s = _mlstm(qk_s, vm_s, om_s, gt_s, cw, cb, bg, g_mnorm[l],
                                             state_conv[l], state_C[l], state_n[l],
                                             _pad_lanes(state_m[l]).reshape(Bd, 1, LANES), Bd, Td)

        wa, wb, wo = w_up_a[l].astype(BF16), w_up_b[l].astype(BF16), w_out[l].astype(BF16)
        wr = _pad_lanes(w_router[l]).astype(BF16)
        br = _pad_lanes(b_router[l].reshape(1, N_EXP))
        g2 = g_norm2[l].reshape(1, D)
        part = _merge(oa_p, hm_p, ga_p, gb_p, hp, mp[2], mp[4], mp[3], g2, wa, wb, wo, wr, br, 1, 0, n_all)
        x2, h2, route, cnt = _merge(oa_s, hm_s, ga_s, gb_s, hs, ms_[2], ms_[4], ms_[3], g2, wa, wb, wo, wr, br,
                                    n_s, n_p // TOK_TILE, n_all, prev=part)

        g_off, cnt_i, l_off, tail, blk_exp, n_used, loff_f, rows = _moe_offsets(
            jnp.round(cnt[:, 0, :N_EXP]).astype(jnp.int32))
        loff_pad = _pad_lanes(loff_f)
        xs = _dispatch(g_off, cnt_i, l_off, tail, h2, route, loff_pad[:, :, None], rows)
        ys = _ffn(blk_exp, n_used, xs, w_gu[l], b_gu[l], w_down[l], b_down[l])
        final = l == depth - 1
        hp, hs = _combine(g_off, cnt_i, l_off, route, loff_pad[:, None, :], x2, mp[5], ms_[5][0],
                          g_final.reshape(1, D), ys, n_p, n_s, final)

        for j, a in enumerate([k_p.reshape(B, S, H_A, 2, DK_A), v_p.reshape(B, S, H_A, DV_A), cst_p, C_p, nn_p,
                               m_p[:, 0, :H_M],
                               k_s.reshape(Bd, Td, H_A, 2, DK_A), v_s.reshape(Bd, Td, H_A, DV_A), cst_s, C_s, nn_s,
                               m_s[:, 0, :H_M]]):
            outs[j].append(a)

    return (hp.reshape(B, S, D), hs.reshape(Bd, Td, D)) + tuple(jnp.stack(o) for o in outs)
```

```python
import functools
import math

import numpy as np
import jax
import jax.numpy as jnp
from jax import lax
from jax.experimental import pallas as pl
from jax.experimental.pallas import tpu as pltpu

F32 = jnp.float32
BF16 = jnp.bfloat16

D_MODEL = 1024
H_A = 4
DK_A = 64
DV_A = 2 * DK_A
ROPE_THETA = 10000.0
H_M = 4
DH_M = 128
CONV_W = 4
N_EXP = 32
TOP_K = 4
D_FF = D_MODEL
SWIGLU_LIMIT = 7.0
SWIGLU_ALPHA = 1.702
EPS = 1e-6
PAGE_SIZE = 128

W_QA = H_A * 2 * DK_A
W_VA = H_A * DV_A
W_M = H_M * DH_M
N_GATE = 2 * H_M

LANES = 128
ROW_ALIGN = 16
TOK_TILE = 256
FFN_TILE = 256
ATT_TILE = 256
ATT_HEADS = 2
MLSTM_CHUNK = 256
PAGES_PER_STEP = 8
NEG_BIG = -1e30
VMEM_LIMIT = 56 * 1024 * 1024

C_QA, C_KA, C_VA, C_QK, C_VM, C_OM, C_GA, C_GB, C_GT = 0, 512, 1024, 1536, 2560, 3072, 3584, 4608, 5632
D_IN_PACKED = C_GT + LANES
M_LOC = ((TOK_TILE * TOP_K + N_EXP * (ROW_ALIGN - 1)) + 255) // 256 * 256


def _dot(a, b):
    return jnp.dot(a, b, preferred_element_type=F32)


def _dot_nt(a, b):
    return lax.dot_general(a, b, (((1,), (1,)), ((), ())), preferred_element_type=F32)


def _sigmoid(x):
    return 1.0 / (1.0 + jnp.exp(-x))


def _params(sem):
    return pltpu.CompilerParams(dimension_semantics=sem, vmem_limit_bytes=VMEM_LIMIT)


def _ada_body(c_ref, w_ref, b_ref, o_ref):
    c = c_ref[...]
    s = c * _sigmoid(c)
    s_hi = s.astype(BF16)
    s_lo = (s - s_hi.astype(F32)).astype(BF16)
    w = w_ref[...]
    w_hi = w.astype(BF16)
    w_lo = (w - w_hi.astype(F32)).astype(BF16)
    o_ref[...] = _dot(s_hi, w_hi) + _dot(s_lo, w_hi) + _dot(s_hi, w_lo) + b_ref[...]


def _ada(c_all, w_ada, b_ada):
    rows = c_all.shape[0]
    n_out = w_ada.shape[1]
    blk = 1024
    return pl.pallas_call(
        _ada_body,
        grid=(n_out // blk,),
        in_specs=[pl.BlockSpec((rows, D_MODEL), lambda j: (0, 0)),
                  pl.BlockSpec((D_MODEL, blk), lambda j: (0, j)),
                  pl.BlockSpec((1, blk), lambda j: (0, j))],
        out_specs=pl.BlockSpec((rows, blk), lambda j: (0, j)),
        out_shape=jax.ShapeDtypeStruct((rows, n_out), F32),
        compiler_params=_params(("arbitrary",)),
        name="ada",
    )(c_all, w_ada, b_ada.reshape(1, n_out))


def _rope(z, cos, sin):
    lane = lax.broadcasted_iota(jnp.int32, (z.shape[0], LANES), 1)
    first_half = (lane % DK_A) < (DK_A // 2)
    out = []
    for h in range(H_A):
        xh = z[:, h * LANES:(h + 1) * LANES]
        partner = jnp.where(first_half, pltpu.roll(xh, LANES - DK_A // 2, 1), pltpu.roll(xh, DK_A // 2, 1))
        out.append(xh * cos + partner * sin)
    return jnp.concatenate(out, axis=1)


def _inproj_body(x_ref, sc_ref, sh_ref, g_ref, cos_ref, sin_ref, w_ref,
                 q_ref, k_ref, kb_ref, v_ref, vb_ref, qk_ref, vm_ref, om_ref, ga_ref, gb_ref, gt_ref, vt_ref):
    x = x_ref[...]
    ms = jnp.mean(x * x, axis=1, keepdims=True)
    h = (x * lax.rsqrt(ms + EPS)) * g_ref[...] * (1.0 + sc_ref[...]) + sh_ref[...]
    hb = h.astype(BF16)
    cos = cos_ref[...]
    sin = sin_ref[...]

    def seg(lo, n):
        return _dot(hb, w_ref[:, lo:lo + n])

    q = _rope(seg(C_QA, W_QA), cos, sin) * (DK_A ** -0.5)
    q_ref[...] = q.astype(BF16)
    k = _rope(seg(C_KA, W_QA), cos, sin)
    k_ref[...] = k
    kb_ref[...] = k.astype(BF16)
    v = seg(C_VA, W_VA)
    v_ref[...] = v
    vb_ref[...] = v.astype(BF16)
    vt_ref[...] = v.T.astype(BF16)
    qk_ref[...] = seg(C_QK, 2 * W_M)
    vm_ref[...] = seg(C_VM, W_M).astype(BF16)
    om_ref[...] = seg(C_OM, W_M)
    ga_ref[...] = seg(C_GA, D_MODEL)
    gb_ref[...] = seg(C_GB, D_MODEL)
    gt_ref[...] = seg(C_GT, LANES)


def _inproj(x, sc, sh, g1, cos, sin, w_packed, rows_per_mod):
    n = x.shape[0]
    nt = n // TOK_TILE
    tiles_per_group = nt // sc.shape[0]
    tab_tiles = cos.shape[0] // TOK_TILE
    tok = lambda w: pl.BlockSpec((TOK_TILE, w), lambda i: (i, 0))
    mod = pl.BlockSpec((None, rows_per_mod, D_MODEL), lambda i: (i // tiles_per_group, 0, 0))
    tab = pl.BlockSpec((TOK_TILE, LANES), lambda i: (i % tab_tiles, 0))
    widths = [(W_QA, BF16), (W_QA, F32), (W_QA, BF16), (W_VA, F32), (W_VA, BF16), (2 * W_M, F32),
              (W_M, BF16), (W_M, F32), (D_MODEL, F32), (D_MODEL, F32), (LANES, F32)]
    vt_spec = pl.BlockSpec((None, W_VA, TOK_TILE), lambda i: (i, 0, 0))
    return pl.pallas_call(
        _inproj_body,
        grid=(nt,),
        in_specs=[tok(D_MODEL), mod, mod, pl.BlockSpec((1, D_MODEL), lambda i: (0, 0)), tab, tab,
                  pl.BlockSpec((D_MODEL, D_IN_PACKED), lambda i: (0, 0), pipeline_mode=pl.Buffered(1))],
        out_specs=[tok(w) for w, _ in widths] + [vt_spec],
        out_shape=[jax.ShapeDtypeStruct((n, w), dt) for w, dt in widths]
                  + [jax.ShapeDtypeStruct((nt, W_VA, TOK_TILE), BF16)],
        compiler_params=_params(("arbitrary",)),
        name="inproj",
    )(x, sc, sh, g1, cos, sin, w_packed)


def _rope_tables(pos):
    half = DK_A // 2
    inv = ROPE_THETA ** (-np.arange(half, dtype=np.float64) * 2.0 / DK_A)
    ang = np.asarray(pos, np.float64)[:, None] * inv[None, :]
    cos = np.cos(ang)
    sin = np.sin(ang)
    cos64 = np.concatenate([cos, cos], axis=1)
    sin64 = np.concatenate([-sin, sin], axis=1)
    return (jnp.asarray(np.tile(cos64, (1, LANES // DK_A)), F32),
            jnp.asarray(np.tile(sin64, (1, LANES // DK_A)), F32))


def _lambda_value(lam_ref, lam_init):
    lv = lam_ref[...]
    l1 = jnp.sum(lv[0:1, :] * lv[1:2, :], axis=1, keepdims=True)
    l2 = jnp.sum(lv[2:3, :] * lv[3:4, :], axis=1, keepdims=True)
    return jnp.exp(l1) - jnp.exp(l2) + lam_init


def _subln(o, g, lam_init):
    ms = jnp.mean(o * o, axis=1, keepdims=True)
    return (o * lax.rsqrt(ms + EPS)) * g * (1.0 - lam_init)


def _attn_prompt_body(q_ref, k_ref, vt_ref, lam_ref, g_ref, o_ref, m_scr, l_scr, acc_scr, *, lam_init):
    i = pl.program_id(2)
    tq = ATT_TILE
    lane = lax.broadcasted_iota(jnp.int32, (tq, LANES), 1)
    qs = []
    for hh in range(ATT_HEADS):
        q = q_ref[:, hh * LANES:(hh + 1) * LANES]
        zero = jnp.zeros_like(q)
        qs.append(jnp.concatenate([jnp.where(lane < DK_A, q, zero), jnp.where(lane >= DK_A, q, zero)], axis=0))
    m_scr[...] = jnp.full(m_scr.shape, NEG_BIG, F32)
    l_scr[...] = jnp.zeros(l_scr.shape, F32)
    acc_scr[...] = jnp.zeros(acc_scr.shape, F32)

    def update(hh, st, vt):
        m_old = m_scr[hh]
        m_new = jnp.maximum(m_old, jnp.max(st, axis=0, keepdims=True))
        alpha = jnp.exp(m_old - m_new)
        pt = jnp.exp(st - m_new)
        l_scr[hh] = alpha * l_scr[hh] + jnp.sum(pt, axis=0, keepdims=True)
        acc_scr[hh] = alpha * acc_scr[hh] + _dot(vt, pt.astype(BF16))
        m_scr[hh] = m_new

    def scores(hh, j):
        start = pl.multiple_of(j * tq, tq)
        return _dot_nt(k_ref[pl.ds(start, tq), hh * LANES:(hh + 1) * LANES], qs[hh])

    def off_diag(j, carry):
        for hh in range(ATT_HEADS):
            update(hh, scores(hh, j), vt_ref[j, hh * LANES:(hh + 1) * LANES, :])
        return carry

    lax.fori_loop(0, i, off_diag, 0)
    key = lax.broadcasted_iota(jnp.int32, (tq, 2 * tq), 0)
    qry = lax.broadcasted_iota(jnp.int32, (tq, 2 * tq), 1) % tq
    lam = _lambda_value(lam_ref, lam_init)
    for hh in range(ATT_HEADS):
        update(hh, jnp.where(key <= qry, scores(hh, i), NEG_BIG), vt_ref[i, hh * LANES:(hh + 1) * LANES, :])
        ot = acc_scr[hh] / l_scr[hh]
        at = ot[:, :tq] - lam * ot[:, tq:]
        ms = jnp.mean(at * at, axis=0, keepdims=True)
        at = (at * lax.rsqrt(ms + EPS)) * g_ref[...] * (1.0 - lam_init)
        o_ref[:, hh * LANES:(hh + 1) * LANES] = at.T.astype(BF16)


def _attn_prompt(q, k, vt, lam_vec, g_subln_col, batch, seq, lam_init):
    nq = seq // ATT_TILE
    width = ATT_HEADS * LANES
    kv = pl.BlockSpec((seq, width), lambda b, g, i: (b, g))
    vts = pl.BlockSpec((nq, width, ATT_TILE), lambda b, g, i: (b, g, 0))
    qo = pl.BlockSpec((ATT_TILE, width), lambda b, g, i: (b * nq + i, g))
    return pl.pallas_call(
        functools.partial(_attn_prompt_body, lam_init=lam_init),
        grid=(batch, H_A // ATT_HEADS, nq),
        in_specs=[qo, kv, vts, pl.BlockSpec((4, DK_A), lambda b, g, i: (0, 0)),
                  pl.BlockSpec((DV_A, 1), lambda b, g, i: (0, 0))],
        out_specs=qo,
        out_shape=jax.ShapeDtypeStruct((batch * seq, W_VA), BF16),
        scratch_shapes=[pltpu.VMEM((ATT_HEADS, 1, 2 * ATT_TILE), F32), pltpu.VMEM((ATT_HEADS, 1, 2 * ATT_TILE), F32),
                        pltpu.VMEM((ATT_HEADS, DV_A, 2 * ATT_TILE), F32)],
        compiler_params=_params(("arbitrary", "arbitrary", "arbitrary")),
        name="attn_prompt",
    )(q, k, vt, lam_vec, g_subln_col)


def _attn_sample_body(pt_ref, q_ref, kn_ref, vn_ref, lam_ref, g_ref, *rest, lam_init, n_steps, t_new):
    k_refs = rest[:PAGES_PER_STEP]
    v_refs = rest[PAGES_PER_STEP:2 * PAGES_PER_STEP]
    o_ref = rest[2 * PAGES_PER_STEP]
    qbd_scr, m_scr, l_scr, acc_scr = rest[2 * PAGES_PER_STEP + 1:]
    s_idx = pl.program_id(1)
    n_rows = 2 * H_A * t_new

    @pl.when(s_idx == 0)
    def _():
        q = q_ref[...].astype(F32)
        qt = jnp.concatenate([q] * (2 * H_A), axis=0)
        row = lax.broadcasted_iota(jnp.int32, qt.shape, 0)
        col = lax.broadcasted_iota(jnp.int32, qt.shape, 1)
        qbd_scr[...] = jnp.where(col // DK_A == row // t_new, qt, 0.0).astype(BF16)
        m_scr[...] = jnp.full(m_scr.shape, NEG_BIG, F32)
        l_scr[...] = jnp.zeros(l_scr.shape, F32)
        acc_scr[...] = jnp.zeros(acc_scr.shape, F32)

    rows_h = 2 * t_new

    def update(s, v_of_head):
        m_old = m_scr[...]
        m_new = jnp.maximum(m_old, jnp.max(s, axis=1, keepdims=True))
        alpha = jnp.exp(m_old - m_new)
        p = jnp.exp(s - m_new)
        l_scr[...] = alpha * l_scr[...] + jnp.sum(p, axis=1, keepdims=True)
        pb = p.astype(BF16)
        pv = [_dot(pb[h * rows_h:(h + 1) * rows_h, :], v_of_head(h)) for h in range(H_A)]
        acc_scr[...] = alpha * acc_scr[...] + jnp.concatenate(pv, axis=0)
        m_scr[...] = m_new

    qbd = qbd_scr[...]
    kt = jnp.concatenate([r[...].astype(BF16) for r in k_refs], axis=1)

    def cached_v(h):
        return jnp.concatenate([r[pl.ds(h, PAGE_SIZE, stride=H_A), :].astype(BF16) for r in v_refs], axis=0)

    update(_dot(qbd, kt), cached_v)

    @pl.when(s_idx == n_steps - 1)
    def _():
        zpad = jnp.zeros((PAGE_SIZE - t_new, W_QA), F32)
        kn = jnp.concatenate([kn_ref[...].astype(F32), zpad], axis=0).astype(BF16)
        vn = jnp.concatenate([vn_ref[...].astype(F32), zpad], axis=0).astype(BF16)
        s = _dot_nt(qbd, kn)
        row = lax.broadcasted_iota(jnp.int32, s.shape, 0) % t_new
        col = lax.broadcasted_iota(jnp.int32, s.shape, 1)
        update(jnp.where(col <= row, s, NEG_BIG), lambda h: vn[:, h * DV_A:(h + 1) * DV_A])
        o = acc_scr[...] / l_scr[...]
        lam = _lambda_value(lam_ref, lam_init)
        outs = []
        for h in range(H_A):
            r0 = h * rows_h
            outs.append(_subln(o[r0:r0 + t_new] - lam * o[r0 + t_new:r0 + rows_h], g_ref[...], lam_init))
        o_ref[...] = jnp.concatenate(outs, axis=1).astype(BF16)


def _attn_sample(q, k_new, v_new, cache_k, cache_v, page_table, lam_vec, g_subln, lam_init, t_new):
    bd, n_pages = page_table.shape
    n_steps = n_pages // PAGES_PER_STEP
    n_rows = 2 * H_A * t_new
    new = pl.BlockSpec((None, t_new, W_QA), lambda b, s, pt: (b, 0, 0))
    page = lambda j: pl.BlockSpec((None, W_QA, PAGE_SIZE),
                                  lambda b, s, pt, j=j: (pt[b, s * PAGES_PER_STEP + j], 0, 0))
    grid_spec = pltpu.PrefetchScalarGridSpec(
        num_scalar_prefetch=1,
        grid=(bd, n_steps),
        in_specs=[new, new, new, pl.BlockSpec((4, DK_A), lambda b, s, pt: (0, 0)),
                  pl.BlockSpec((1, DV_A), lambda b, s, pt: (0, 0))]
                 + [page(j) for j in range(PAGES_PER_STEP)] * 2,
        out_specs=new,
        scratch_shapes=[pltpu.VMEM((n_rows, W_QA), BF16), pltpu.VMEM((n_rows, 1), F32),
                        pltpu.VMEM((n_rows, 1), F32), pltpu.VMEM((n_rows, DV_A), F32)],
    )
    return pl.pallas_call(
        functools.partial(_attn_sample_body, lam_init=lam_init, n_steps=n_steps, t_new=t_new),
        grid_spec=grid_spec,
        out_shape=jax.ShapeDtypeStruct((bd, t_new, W_VA), BF16),
        compiler_params=_params(("arbitrary", "arbitrary")),
        name="attn_sample",
    )(page_table, q.reshape(bd, t_new, W_QA), k_new.reshape(bd, t_new, W_QA), v_new.reshape(bd, t_new, W_VA),
      lam_vec, g_subln, *([cache_k] * PAGES_PER_STEP), *([cache_v] * PAGES_PER_STEP)).reshape(bd * t_new, W_VA)


def _mlstm_body(qk_ref, vm_ref, om_ref, gt_ref, cw_ref, cb_ref, bg_ref, gm_ref, cbuf_ref, c0_ref, n0_ref, m0_ref,
                h_ref, cst_ref, cout_ref, nout_ref, mout_ref, ext_scr, c_scr, n_scr, m_scr, *, tb, L, nc):
    c_idx = pl.program_id(1)

    @pl.when(c_idx == 0)
    def _():
        ext_scr[...] = jnp.zeros(ext_scr.shape, F32)
        ext_scr[8 - (CONV_W - 1):8, :] = cbuf_ref[...]
        c_scr[...] = c0_ref[...]
        n_scr[...] = n0_ref[...]
        m_scr[...] = m0_ref[...]

    pad = L - tb
    u = qk_ref[...]
    if pad:
        u = jnp.concatenate([u, jnp.zeros((pad, u.shape[1]), F32)], axis=0)
    full = jnp.concatenate([ext_scr[...], u], axis=0)
    conv = cb_ref[...] + cw_ref[CONV_W - 1:CONV_W, :] * u
    for j in range(CONV_W - 1):
        conv = conv + cw_ref[j:j + 1, :] * pltpu.roll(full, CONV_W - 1 - j, 0)[8:8 + L]
    a = conv * _sigmoid(conv)
    if not pad:
        ext_scr[...] = u[L - 8:L]

    @pl.when(c_idx == nc - 1)
    def _():
        cst_ref[...] = qk_ref[tb - (CONV_W - 1):tb, :]

    g = gt_ref[...] + bg_ref[...]
    li = g
    lf = jnp.minimum(g, 0.0) - jnp.log1p(jnp.exp(-jnp.abs(g)))
    if pad:
        zpad = jnp.zeros((pad, LANES), F32)
        li = jnp.concatenate([li, zpad + NEG_BIG], axis=0)
        lf = jnp.concatenate([lf, zpad], axis=0)
    row = lax.broadcasted_iota(jnp.int32, (L, LANES), 0)
    lane = lax.broadcasted_iota(jnp.int32, (L, LANES), 1)
    bcum = lf
    shift = 1
    while shift < L:
        bcum = bcum + jnp.where(row >= shift, pltpu.roll(bcum, shift, 0), 0.0)
        shift *= 2
    gates = jnp.where(lane < H_M, li, bcum)
    gates_t = gates.T
    tri = lax.broadcasted_iota(jnp.int32, (L, L), 0) >= lax.broadcasted_iota(jnp.int32, (L, L), 1)
    m_all = m_scr[...]
    lane1 = lax.broadcasted_iota(jnp.int32, (1, LANES), 1)
    m_next = m_all
    vall = vm_ref[...]
    if pad:
        vall = jnp.concatenate([vall, jnp.zeros((pad, vall.shape[1]), BF16)], axis=0)

    for h in range(H_M):
        li_col = gates[:, h:h + 1]
        b_col = gates[:, H_M + h:H_M + h + 1]
        li_row = gates_t[h:h + 1, :]
        b_row = gates_t[H_M + h:H_M + h + 1, :]
        m_prev = m_all[:, h:h + 1]
        b_last = b_col[L - 1:L, :]
        log_d = jnp.where(tri, b_col - b_row + li_row, NEG_BIG)
        inter = b_col + m_prev
        mt = jnp.maximum(inter, jnp.max(log_d, axis=1, keepdims=True))
        q = a[:, h * DH_M:(h + 1) * DH_M]
        k = a[:, W_M + h * DH_M:W_M + (h + 1) * DH_M] * (DH_M ** -0.5)
        v = vall[:, h * DH_M:(h + 1) * DH_M]
        qb = q.astype(BF16)
        s = _dot_nt(qb, k.astype(BF16)) * jnp.exp(log_d - mt)
        ei = jnp.exp(inter - mt)
        c_old = c_scr[h]
        n_old = n_scr[h:h + 1, :]
        num = ei * _dot(qb, c_old.astype(BF16)) + _dot(s.astype(BF16), v)
        den = ei * jnp.sum(q * n_old, axis=1, keepdims=True) + jnp.sum(s, axis=1, keepdims=True)
        hh = num / jnp.maximum(jnp.abs(den), jnp.exp(-mt))
        g_col = b_last - b_col + li_col
        bl = b_last + m_prev
        m_new = jnp.maximum(bl, jnp.max(g_col, axis=0, keepdims=True))
        wg = jnp.exp(g_col - m_new)
        decay = jnp.exp(bl - m_new)
        kw = k * wg
        c_scr[h] = decay * c_old + _dot(kw.T.astype(BF16), v)
        n_scr[h:h + 1, :] = decay * n_old + jnp.sum(kw, axis=0, keepdims=True)
        m_next = jnp.where(lane1 == h, m_new, m_next)
        ms = jnp.mean(hh * hh, axis=1, keepdims=True)
        hn = (hh * lax.rsqrt(ms + EPS)) * gm_ref[h:h + 1, :]
        og = _sigmoid(om_ref[:, h * DH_M:(h + 1) * DH_M])
        h_ref[:, h * DH_M:(h + 1) * DH_M] = (hn[:tb] * og).astype(BF16)

    m_scr[...] = m_next

    @pl.when(c_idx == nc - 1)
    def _():
        cout_ref[...] = c_scr[...]
        nout_ref[...] = n_scr[...]
        mout_ref[...] = m_scr[...]


def _mlstm(qk, vm, om, gt, w_conv, b_conv, b_gates_pad, g_mnorm, conv_buf, c0, n0, m0_pad, batch, seq):
    tb = min(seq, MLSTM_CHUNK)
    L = max(tb, LANES)
    nc = seq // tb
    tok = lambda w: pl.BlockSpec((None, tb, w), lambda b, c: (b * nc + c, 0, 0))
    chunks = lambda a: a.reshape(batch * nc, tb, a.shape[-1])
    const = lambda shape: pl.BlockSpec(shape, lambda b, c: (0,) * len(shape))
    per_b = lambda shape: pl.BlockSpec((None,) + shape, lambda b, c: (b,) + (0,) * len(shape))
    h, cst, c_out, n_out, m_out = pl.pallas_call(
        functools.partial(_mlstm_body, tb=tb, L=L, nc=nc),
        grid=(batch, nc),
        in_specs=[tok(2 * W_M), tok(W_M), tok(W_M), tok(LANES), const((CONV_W, 2 * W_M)), const((1, 2 * W_M)),
                  const((1, LANES)), const((H_M, DH_M)), per_b((CONV_W - 1, 2 * W_M)),
                  per_b((H_M, DH_M, DH_M)), per_b((H_M, DH_M)), per_b((1, LANES))],
        out_specs=[tok(W_M), per_b((CONV_W - 1, 2 * W_M)), per_b((H_M, DH_M, DH_M)), per_b((H_M, DH_M)),
                   per_b((1, LANES))],
        out_shape=[jax.ShapeDtypeStruct((batch * nc, tb, W_M), BF16),
                   jax.ShapeDtypeStruct((batch, CONV_W - 1, 2 * W_M), F32),
                   jax.ShapeDtypeStruct((batch, H_M, DH_M, DH_M), F32),
                   jax.ShapeDtypeStruct((batch, H_M, DH_M), F32),
                   jax.ShapeDtypeStruct((batch, 1, LANES), F32)],
        scratch_shapes=[pltpu.VMEM((8, 2 * W_M), F32), pltpu.VMEM((H_M, DH_M, DH_M), F32),
                        pltpu.VMEM((H_M, DH_M), F32), pltpu.VMEM((1, LANES), F32)],
        compiler_params=_params(("arbitrary", "arbitrary")),
        name="mlstm",
    )(chunks(qk), chunks(vm), chunks(om), chunks(gt), w_conv, b_conv, b_gates_pad, g_mnorm, conv_buf, c0, n0,
      m0_pad)
    return h.reshape(batch * seq, W_M), cst, c_out, n_out, m_out


def _merge_body(oa_ref, hm_ref, ga_ref, gb_ref, x_ref, gt1_ref, sc2_ref, sh2_ref, g2_ref, wa_ref, wb_ref, wo_ref,
                wr_ref, br_ref, *rest):
    x2_ref, h2_ref, route_ref, cnt_ref = rest[-4:]
    ya = _dot(oa_ref[...], wa_ref[...])
    yb = _dot(hm_ref[...], wb_ref[...])
    mix = _sigmoid(ga_ref[...]) * ya + _sigmoid(gb_ref[...]) * yb
    y = _dot(mix.astype(BF16), wo_ref[...])
    x2 = x_ref[...] + gt1_ref[...] * y
    x2_ref[...] = x2
    ms = jnp.mean(x2 * x2, axis=1, keepdims=True)
    h2 = (x2 * lax.rsqrt(ms + EPS)) * g2_ref[...] * (1.0 + sc2_ref[...]) + sh2_ref[...]
    h2b = h2.astype(BF16)
    h2_ref[...] = h2b

    tm = h2b.shape[0]
    lane = lax.broadcasted_iota(jnp.int32, (tm, LANES), 1)
    logits = jnp.where(lane < N_EXP, _dot(h2b, wr_ref[...]) + br_ref[...], NEG_BIG)
    work = logits
    vals, hots = [], []
    for _ in range(TOP_K):
        mx = jnp.max(work, axis=1, keepdims=True)
        idx = jnp.min(jnp.where(work == mx, lane, LANES), axis=1, keepdims=True)
        hot = lane == idx
        vals.append(mx)
        hots.append(hot)
        work = jnp.where(hot, 2.0 * NEG_BIG, work)
    es = [jnp.exp(v - vals[0]) for v in vals]
    den = es[0]
    for e in es[1:]:
        den = den + e
    sel = jnp.zeros((tm, LANES), F32)
    for hot in hots:
        sel = jnp.where(hot, 1.0, sel)
    r_i = lax.broadcasted_iota(jnp.int32, (tm, tm), 0)
    c_i = lax.broadcasted_iota(jnp.int32, (tm, tm), 1)
    rank = _dot(jnp.where(c_i < r_i, 1.0, 0.0).astype(BF16), sel.astype(BF16))
    lane_f = lane.astype(F32)
    route = jnp.zeros((tm, LANES), F32)
    for k in range(TOP_K):
        e_k = jnp.sum(jnp.where(hots[k], lane_f, 0.0), axis=1, keepdims=True)
        r_k = jnp.sum(jnp.where(hots[k], rank, 0.0), axis=1, keepdims=True)
        route = jnp.where(lane == k, e_k, route)
        route = jnp.where(lane == TOP_K + k, es[k] / den, route)
        route = jnp.where(lane == 2 * TOP_K + k, r_k, route)
    route_ref[...] = route
    cnt_ref[...] = jnp.sum(sel, axis=0, keepdims=True)


def _merge(oa, hm, ga, gb, x, gt1, sc2, sh2, g2, wa, wb, wo, wr, br, rows_per_mod, tile0, n_all, prev=None):
    n = x.shape[0]
    nt = n // TOK_TILE
    nt_all = n_all // TOK_TILE
    tiles_per_group = nt // gt1.shape[0]
    tok = lambda w: pl.BlockSpec((TOK_TILE, w), lambda i: (i, 0))
    mod = pl.BlockSpec((None, rows_per_mod, D_MODEL), lambda i: (i // tiles_per_group, 0, 0))
    res = lambda shape: pl.BlockSpec(shape, lambda i: (0, 0), pipeline_mode=pl.Buffered(1))
    out_tok = lambda w: pl.BlockSpec((TOK_TILE, w), lambda i: (tile0 + i, 0))
    in_specs = [tok(W_VA), tok(W_M), tok(D_MODEL), tok(D_MODEL), tok(D_MODEL), mod, mod, mod,
                pl.BlockSpec((1, D_MODEL), lambda i: (0, 0)),
                res((W_VA, D_MODEL)), res((W_M, D_MODEL)), res((D_MODEL, D_MODEL)), res((D_MODEL, LANES)),
                pl.BlockSpec((1, LANES), lambda i: (0, 0))]
    args = [oa, hm, ga, gb, x, gt1, sc2, sh2, g2, wa, wb, wo, wr, br]
    aliases = {}
    if prev is not None:
        in_specs += [pl.BlockSpec(memory_space=pl.ANY)] * 4
        aliases = {len(args) + j: j for j in range(4)}
        args += list(prev)
    return pl.pallas_call(
        _merge_body,
        grid=(nt,),
        in_specs=in_specs,
        out_specs=[out_tok(D_MODEL), out_tok(D_MODEL), out_tok(LANES),
                   pl.BlockSpec((None, 1, LANES), lambda i: (tile0 + i, 0, 0))],
        out_shape=[jax.ShapeDtypeStruct((n_all, D_MODEL), F32), jax.ShapeDtypeStruct((n_all, D_MODEL), BF16),
                   jax.ShapeDtypeStruct((n_all, LANES), F32), jax.ShapeDtypeStruct((nt_all, 1, LANES), F32)],
        input_output_aliases=aliases,
        compiler_params=_params(("arbitrary",)),
        name="merge",
    )(*args)


def _segment_copies(src, dst, sem, src_row, dst_row, n_groups, max_groups):
    out = []
    bit = 1
    while bit * 2 <= max_groups:
        bit *= 2
    while bit >= 1:
        off = (n_groups // (2 * bit)) * (2 * bit) * ROW_ALIGN
        rows = bit * ROW_ALIGN
        cp = pltpu.make_async_copy(src.at[pl.ds(pl.multiple_of(src_row + off, ROW_ALIGN), rows)],
                                   dst.at[pl.ds(pl.multiple_of(dst_row + off, ROW_ALIGN), rows)], sem)
        out.append(((n_groups // bit) % 2 == 1, cp))
        bit //= 2
    return out


def _run_copies(copies):
    for pred, cp in copies:
        pl.when(pred)(cp.start)
    for pred, cp in copies:
        pl.when(pred)(cp.wait)


def _slot_rows(route_t, loff_col, k):
    e_row = route_t[k:k + 1, :]
    r_row = route_t[2 * TOP_K + k:2 * TOP_K + k + 1, :]
    sub = lax.broadcasted_iota(jnp.int32, (LANES, route_t.shape[1]), 0).astype(F32)
    return jnp.sum(jnp.where(sub == e_row, loff_col, 0.0), axis=0, keepdims=True) + r_row


def _dispatch_body(g_ref, n_ref, lo_ref, tail_ref, h2_ref, route_ref, loff_ref, xs_ref, loc_scr, zero_scr, sem,
                   *, nt):
    t = pl.program_id(0)
    route_t = route_ref[...].T
    loff_col = loff_ref[...]
    r_i = lax.broadcasted_iota(jnp.int32, (M_LOC, TOK_TILE), 0).astype(F32)
    hit = r_i == _slot_rows(route_t, loff_col, 0)
    for k in range(1, TOP_K):
        hit = jnp.logical_or(hit, r_i == _slot_rows(route_t, loff_col, k))
    onehot = jnp.where(hit, 1.0, 0.0).astype(BF16)
    loc_scr[...] = _dot(onehot, h2_ref[...]).astype(BF16)

    copies = []
    for e in range(N_EXP):
        n16 = (n_ref[t * N_EXP + e] + ROW_ALIGN - 1) // ROW_ALIGN
        copies += _segment_copies(loc_scr, xs_ref, sem, lo_ref[t * N_EXP + e], g_ref[t * N_EXP + e], n16,
                                  TOK_TILE // ROW_ALIGN)
    _run_copies(copies)

    @pl.when(t == nt - 1)
    def _():
        zero_scr[...] = jnp.zeros(zero_scr.shape, BF16)
        tails = []
        for e in range(N_EXP):
            tails += _segment_copies(zero_scr, xs_ref, sem, 0, tail_ref[e], tail_ref[N_EXP + e],
                                     FFN_TILE // ROW_ALIGN - 1)
        _run_copies(tails)


def _dispatch(g_off, cnt, l_off, tail, h2, route, loff_col, rows):
    nt = h2.shape[0] // TOK_TILE
    grid_spec = pltpu.PrefetchScalarGridSpec(
        num_scalar_prefetch=4,
        grid=(nt,),
        in_specs=[pl.BlockSpec((TOK_TILE, D_MODEL), lambda t, *_: (t, 0)),
                  pl.BlockSpec((TOK_TILE, LANES), lambda t, *_: (t, 0)),
                  pl.BlockSpec((None, LANES, 1), lambda t, *_: (t, 0, 0))],
        out_specs=pl.BlockSpec(memory_space=pl.ANY),
        scratch_shapes=[pltpu.VMEM((M_LOC, D_MODEL), BF16), pltpu.VMEM((FFN_TILE, D_MODEL), BF16),
                        pltpu.SemaphoreType.DMA],
    )
    return pl.pallas_call(
        functools.partial(_dispatch_body, nt=nt),
        grid_spec=grid_spec,
        out_shape=jax.ShapeDtypeStruct((rows, D_MODEL), BF16),
        compiler_params=_params(("arbitrary",)),
        name="moe_dispatch",
    )(g_off, cnt, l_off, tail, h2, route, loff_col)


def _ffn_body(be_ref, nu_ref, x_ref, wgu_ref, bgu_ref, wd_ref, bd_ref, y_ref, wgu_scr, wd_scr):
    i = pl.program_id(0)
    prev = be_ref[jnp.maximum(i - 1, 0)]

    @pl.when(jnp.logical_or(i == 0, be_ref[i] != prev))
    def _():
        wgu_scr[...] = wgu_ref[...].astype(BF16)
        wd_scr[...] = wd_ref[...].astype(BF16)

    @pl.when(i < nu_ref[0])
    def _():
        gu = _dot(x_ref[...], wgu_scr[...]) + bgu_ref[...]
        gate = jnp.minimum(gu[:, :D_FF], SWIGLU_LIMIT)
        up = jnp.clip(gu[:, D_FF:], -SWIGLU_LIMIT, SWIGLU_LIMIT)
        act = (up + 1.0) * gate * _sigmoid(SWIGLU_ALPHA * gate)
        y_ref[...] = (_dot(act.astype(BF16), wd_scr[...]) + bd_ref[...]).astype(BF16)


def _ffn(blk_exp, n_used, xs, w_gu, b_gu, w_down, b_down):
    rows = xs.shape[0]
    nblk = rows // FFN_TILE
    row_blk = pl.BlockSpec((FFN_TILE, D_MODEL), lambda i, be, nu: (jnp.minimum(i, nu[0] - 1), 0))
    grid_spec = pltpu.PrefetchScalarGridSpec(
        num_scalar_prefetch=2,
        grid=(nblk,),
        in_specs=[row_blk,
                  pl.BlockSpec((None, D_MODEL, 2 * D_FF), lambda i, be, nu: (be[i], 0, 0)),
                  pl.BlockSpec((None, 1, 2 * D_FF), lambda i, be, nu: (be[i], 0, 0)),
                  pl.BlockSpec((None, D_FF, D_MODEL), lambda i, be, nu: (be[i], 0, 0)),
                  pl.BlockSpec((None, 1, D_MODEL), lambda i, be, nu: (be[i], 0, 0))],
        out_specs=row_blk,
        scratch_shapes=[pltpu.VMEM((D_MODEL, 2 * D_FF), BF16), pltpu.VMEM((D_FF, D_MODEL), BF16)],
    )
    return pl.pallas_call(
        _ffn_body,
        grid_spec=grid_spec,
        out_shape=jax.ShapeDtypeStruct((rows, D_MODEL), BF16),
        compiler_params=_params(("arbitrary",)),
        name="moe_ffn",
    )(blk_exp, n_used, xs, w_gu, b_gu.reshape(N_EXP, 1, 2 * D_FF), w_down, b_down.reshape(N_EXP, 1, D_MODEL))


def _combine_body(g_ref, n_ref, lo_ref, route_ref, loff_ref, x2_ref, gtp_ref, gts_ref, gf_ref, ys_ref,
                  yp_ref, ysm_ref, loc_scr, sem, *, nt_prompt, final):
    t = pl.program_id(0)

    @pl.when(t == 0)
    def _():
        loc_scr[...] = jnp.zeros(loc_scr.shape, BF16)

    copies = []
    for e in range(N_EXP):
        n16 = (n_ref[t * N_EXP + e] + ROW_ALIGN - 1) // ROW_ALIGN
        copies += _segment_copies(ys_ref, loc_scr, sem, g_ref[t * N_EXP + e], lo_ref[t * N_EXP + e], n16,
                                  TOK_TILE // ROW_ALIGN)
    _run_copies(copies)

    route = route_ref[...]
    loff_row = loff_ref[...]
    lane = lax.broadcasted_iota(jnp.int32, (TOK_TILE, LANES), 1).astype(F32)
    c_i = lax.broadcasted_iota(jnp.int32, (TOK_TILE, M_LOC), 1).astype(F32)
    wmat = jnp.zeros((TOK_TILE, M_LOC), F32)
    for k in range(TOP_K):
        e_k = route[:, k:k + 1]
        slot = jnp.sum(jnp.where(lane == e_k, loff_row, 0.0), axis=1, keepdims=True) \
            + route[:, 2 * TOP_K + k:2 * TOP_K + k + 1]
        wmat = jnp.where(c_i == slot, route[:, TOP_K + k:TOP_K + k + 1], wmat)
    moe = _dot(wmat.astype(BF16), loc_scr[...])

    gate = jnp.where(t >= nt_prompt, gts_ref[...], gtp_ref[...])
    xo = x2_ref[...] + gate * moe
    if final:
        ms = jnp.mean(xo * xo, axis=1, keepdims=True)
        xo = (xo * lax.rsqrt(ms + EPS)) * gf_ref[...]

    @pl.when(t < nt_prompt)
    def _():
        yp_ref[...] = xo

    @pl.when(t >= nt_prompt)
    def _():
        ysm_ref[...] = xo


def _combine(g_off, cnt, l_off, route, loff_row, x2, gt2_p, gt2_s, g_final, ys, n_prompt, n_sample, final):
    nt = x2.shape[0] // TOK_TILE
    nt_prompt = n_prompt // TOK_TILE
    tiles_per_batch = nt_prompt // gt2_p.shape[0]
    grid_spec = pltpu.PrefetchScalarGridSpec(
        num_scalar_prefetch=3,
        grid=(nt,),
        in_specs=[pl.BlockSpec((TOK_TILE, LANES), lambda t, *_: (t, 0)),
                  pl.BlockSpec((None, 1, LANES), lambda t, *_: (t, 0, 0)),
                  pl.BlockSpec((TOK_TILE, D_MODEL), lambda t, *_: (t, 0)),
                  pl.BlockSpec((None, 1, D_MODEL),
                               lambda t, *_: (jnp.minimum(t, nt_prompt - 1) // tiles_per_batch, 0, 0)),
                  pl.BlockSpec((TOK_TILE, D_MODEL), lambda t, *_: (0, 0)),
                  pl.BlockSpec((1, D_MODEL), lambda t, *_: (0, 0)),
                  pl.BlockSpec(memory_space=pl.ANY)],
        out_specs=[pl.BlockSpec((TOK_TILE, D_MODEL), lambda t, *_: (jnp.minimum(t, nt_prompt - 1), 0)),
                   pl.BlockSpec((TOK_TILE, D_MODEL), lambda t, *_: (0, 0))],
        scratch_shapes=[pltpu.VMEM((M_LOC, D_MODEL), BF16), pltpu.SemaphoreType.DMA],
    )
    return pl.pallas_call(
        functools.partial(_combine_body, nt_prompt=nt_prompt, final=final),
        grid_spec=grid_spec,
        out_shape=[jax.ShapeDtypeStruct((n_prompt, D_MODEL), F32), jax.ShapeDtypeStruct((n_sample, D_MODEL), F32)],
        compiler_params=_params(("arbitrary",)),
        name="moe_combine",
    )(g_off, cnt, l_off, route, loff_row, x2, gt2_p, gt2_s, g_final, ys)


def _moe_offsets(cnt):
    nt = cnt.shape[0]
    seg = (cnt + ROW_ALIGN - 1) // ROW_ALIGN * ROW_ALIGN
    l_off = jnp.cumsum(seg, axis=1) - seg
    gsz = jnp.sum(seg, axis=0)
    gpad = (gsz + FFN_TILE - 1) // FFN_TILE * FFN_TILE
    gstart = jnp.cumsum(gpad) - gpad
    g_off = gstart[None, :] + jnp.cumsum(seg, axis=0) - seg
    nblk_e = gpad // FFN_TILE
    blk_end = jnp.cumsum(nblk_e)
    n_used = jnp.maximum(blk_end[-1], 1)
    rows = (nt * TOK_TILE * TOP_K + nt * N_EXP * (ROW_ALIGN - 1) + N_EXP * (FFN_TILE - 1)
            + FFN_TILE - 1) // FFN_TILE * FFN_TILE
    blk = jnp.minimum(jnp.arange(rows // FFN_TILE, dtype=jnp.int32), n_used - 1)
    blk_exp = jnp.minimum(jnp.sum(blk[:, None] >= blk_end[None, :], axis=1), N_EXP - 1)
    tail = jnp.concatenate([gstart + gsz, (gpad - gsz) // ROW_ALIGN])
    i32 = lambda a: a.astype(jnp.int32)
    return (i32(g_off).reshape(-1), i32(cnt).reshape(-1), i32(l_off).reshape(-1), i32(tail), i32(blk_exp),
            i32(n_used).reshape(1), l_off.astype(F32), rows)


def _pack_w_in(w_in):
    pad = jnp.zeros((D_MODEL, LANES - N_GATE), w_in.dtype)
    g0 = C_GA
    return jnp.concatenate([w_in[:, :g0], w_in[:, g0 + N_GATE:], w_in[:, g0:g0 + N_GATE], pad],
                           axis=1).astype(BF16)


def _pad_lanes(a, value=0.0):
    return jnp.pad(a, [(0, 0)] * (a.ndim - 1) + [(0, LANES - a.shape[-1])], constant_values=value)


def kernel(x_prompt, x_sample, c_prompt, c_sample, cache_k, cache_v, state_conv, state_C, state_n, state_m, page_table, w_ada, b_ada, g_norm1, g_norm2, w_in, b_gates, lambda_q1, lambda_k1, lambda_q2, lambda_k2, g_subln, w_conv, b_conv, g_mnorm, w_up_a, w_up_b, w_out, w_router, b_router, w_gu, b_gu, w_down, b_down, g_final):
    B, S, D = x_prompt.shape
    Bd, Td, _ = x_sample.shape
    depth = w_in.shape[0]
    n_pool = cache_k.shape[1]
    past_len = page_table.shape[1] * PAGE_SIZE
    n_p, n_s = B * S, Bd * Td
    n_all = n_p + n_s
    assert D == D_MODEL and n_s == TOK_TILE and S % MLSTM_CHUNK == 0 and n_p % TOK_TILE == 0
    assert page_table.shape[1] % PAGES_PER_STEP == 0

    cos_p, sin_p = _rope_tables(np.arange(S))
    cos_s, sin_s = _rope_tables(np.tile(past_len + np.arange(Td), Bd))
    hp = x_prompt.reshape(n_p, D)
    hs = x_sample.reshape(n_s, D)
    c_all = jnp.concatenate([c_prompt, c_sample], axis=0)
    outs = [[] for _ in range(12)]

    for l in range(depth):
        lam_init = 0.8 - 0.6 * math.exp(-0.3 * l)
        mod = _ada(c_all, w_ada[l], b_ada[l])
        mods = [mod[:, j * D:(j + 1) * D] for j in range(6)]
        mp = [m[:B].reshape(B, 1, D) for m in mods]
        ms_ = [jnp.repeat(m[B:], Td, axis=0).reshape(1, n_s, D) for m in mods]
        w_packed = _pack_w_in(w_in[l])
        g1 = g_norm1[l].reshape(1, D)
        lam_vec = jnp.stack([lambda_q1[l], lambda_k1[l], lambda_q2[l], lambda_k2[l]])
        gsub = g_subln[l].reshape(1, DV_A)
        bg = _pad_lanes(b_gates[l].reshape(1, N_GATE))
        cw, cb = w_conv[l], b_conv[l].reshape(1, 2 * W_M)

        (q_p, k_p, kb_p, v_p, _, qk_p, vm_p, om_p, ga_p, gb_p, gt_p, vt_p) = _inproj(
            hp, mp[1], mp[0], g1, cos_p, sin_p, w_packed, 1)
        (q_s, k_s, kb_s, v_s, vb_s, qk_s, vm_s, om_s, ga_s, gb_s, gt_s, _) = _inproj(
            hs, ms_[1], ms_[0], g1, cos_s, sin_s, w_packed, n_s)

        oa_p = _attn_prompt(q_p, kb_p, vt_p, lam_vec, gsub.reshape(DV_A, 1), B, S, lam_init)
        kt_pool = jnp.transpose(cache_k[l], (0, 2, 3, 4, 1)).reshape(n_pool, W_QA, PAGE_SIZE)
        v_pool = cache_v[l].reshape(n_pool, PAGE_SIZE * H_A, DV_A)
        oa_s = _attn_sample(q_s, kb_s, vb_s, kt_pool, v_pool, page_table, lam_vec, gsub, lam_init, Td)

        zeros = lambda *shape: jnp.zeros(shape, F32)
        hm_p, cst_p, C_p, nn_p, m_p = _mlstm(qk_p, vm_p, om_p, gt_p, cw, cb, bg, g_mnorm[l],
                                             zeros(B, CONV_W - 1, 2 * W_M), zeros(B, H_M, DH_M, DH_M),
                                             zeros(B, H_M, DH_M), zeros(B, 1, LANES), B, S)
        hm_s, cst_s, C_s, nn_s, m_s = _mlstm(qk_s, vm_s, om_s, gt_s, cw, cb, bg, g_mnorm[l],
                                             state_conv[l], state_C[l], state_n[l],
                                             _pad_lanes(state_m[l]).reshape(Bd, 1, LANES), Bd, Td)

        wa, wb, wo = w_up_a[l].astype(BF16), w_up_b[l].astype(BF16), w_out[l].astype(BF16)
        wr = _pad_lanes(w_router[l]).astype(BF16)
        br = _pad_lanes(b_router[l].reshape(1, N_EXP))
        g2 = g_norm2[l].reshape(1, D)
        part = _merge(oa_p, hm_p, ga_p, gb_p, hp, mp[2], mp[4], mp[3], g2, wa, wb, wo, wr, br, 1, 0, n_all)
        x2, h2, route, cnt = _merge(oa_s, hm_s, ga_s, gb_s, hs, ms_[2], ms_[4], ms_[3], g2, wa, wb, wo, wr, br,
                                    n_s, n_p // TOK_TILE, n_all, prev=part)

        g_off, cnt_i, l_off, tail, blk_exp, n_used, loff_f, rows = _moe_offsets(
            jnp.round(cnt[:, 0, :N_EXP]).astype(jnp.int32))
        loff_pad = _pad_lanes(loff_f)
        xs = _dispatch(g_off, cnt_i, l_off, tail, h2, route, loff_pad[:, :, None], rows)
        ys = _ffn(blk_exp, n_used, xs, w_gu[l], b_gu[l], w_down[l], b_down[l])
        final = l == depth - 1
        hp, hs = _combine(g_off, cnt_i, l_off, route, loff_pad[:, None, :], x2, mp[5], ms_[5][0],
                          g_final.reshape(1, D), ys, n_p, n_s, final)

        for j, a in enumerate([k_p.reshape(B, S, H_A, 2, DK_A), v_p.reshape(B, S, H_A, DV_A), cst_p, C_p, nn_p,
                               m_p[:, 0, :H_M],
                               k_s.reshape(Bd, Td, H_A, 2, DK_A), v_s.reshape(Bd, Td, H_A, DV_A), cst_s, C_s, nn_s,
                               m_s[:, 0, :H_M]]):
            outs[j].append(a)

    return (hp.reshape(B, S, D), hs.reshape(Bd, Td, D)) + tuple(jnp.stack(o) for o in outs)
```

```python
import functools
import math

import numpy as np
import jax
import jax.numpy as jnp
from jax import lax
from jax.experimental import pallas as pl
from jax.experimental.pallas import tpu as pltpu

F32 = jnp.float32
BF16 = jnp.bfloat16

D_MODEL = 1024
H_A = 4
DK_A = 64
DV_A = 2 * DK_A
ROPE_THETA = 10000.0
H_M = 4
DH_M = 128
CONV_W = 4
N_EXP = 32
TOP_K = 4
D_FF = D_MODEL
SWIGLU_LIMIT = 7.0
SWIGLU_ALPHA = 1.702
EPS = 1e-6
PAGE_SIZE = 128

W_QA = H_A * 2 * DK_A
W_VA = H_A * DV_A
W_M = H_M * DH_M
N_GATE = 2 * H_M

LANES = 128
ROW_ALIGN = 16
TOK_TILE = 256
FFN_TILE = 256
ATT_TILE = 256
ATT_HEADS = 4
MLSTM_CHUNK = 256
PAGES_PER_STEP = 8
NEG_BIG = -1e30
LOG2_E = math.log2(math.e)
VMEM_LIMIT = 56 * 1024 * 1024

C_QA, C_KA, C_VA, C_QK, C_VM, C_OM, C_GA, C_GB, C_GT = 0, 512, 1024, 1536, 2560, 3072, 3584, 4608, 5632
D_IN_PACKED = C_GT + LANES
M_LOC = ((TOK_TILE * TOP_K + N_EXP * (ROW_ALIGN - 1)) + 255) // 256 * 256


def _dot(a, b):
    return jnp.dot(a, b, preferred_element_type=F32)


def _dot_nt(a, b):
    return lax.dot_general(a, b, (((1,), (1,)), ((), ())), preferred_element_type=F32)


def _sigmoid(x):
    return 1.0 / (1.0 + jnp.exp(-x))


def _params(sem):
    return pltpu.CompilerParams(dimension_semantics=sem, vmem_limit_bytes=VMEM_LIMIT)


def _ada_body(c_ref, w_ref, b_ref, o_ref):
    c = c_ref[...]
    s = c * _sigmoid(c)
    s_hi = s.astype(BF16)
    s_lo = (s - s_hi.astype(F32)).astype(BF16)
    w = w_ref[...]
    w_hi = w.astype(BF16)
    w_lo = (w - w_hi.astype(F32)).astype(BF16)
    o_ref[...] = _dot(s_hi, w_hi) + _dot(s_lo, w_hi) + _dot(s_hi, w_lo) + b_ref[...]


def _ada(c_all, w_ada, b_ada):
    rows = c_all.shape[0]
    n_out = w_ada.shape[1]
    blk = 1024
    return pl.pallas_call(
        _ada_body,
        grid=(n_out // blk,),
        in_specs=[pl.BlockSpec((rows, D_MODEL), lambda j: (0, 0)),
                  pl.BlockSpec((D_MODEL, blk), lambda j: (0, j)),
                  pl.BlockSpec((1, blk), lambda j: (0, j))],
        out_specs=pl.BlockSpec((rows, blk), lambda j: (0, j)),
        out_shape=jax.ShapeDtypeStruct((rows, n_out), F32),
        compiler_params=_params(("arbitrary",)),
        name="ada",
    )(c_all, w_ada, b_ada.reshape(1, n_out))


def _rope(z, cos, sin):
    lane = lax.broadcasted_iota(jnp.int32, (z.shape[0], LANES), 1)
    first_half = (lane % DK_A) < (DK_A // 2)
    out = []
    for h in range(H_A):
        xh = z[:, h * LANES:(h + 1) * LANES]
        partner = jnp.where(first_half, pltpu.roll(xh, LANES - DK_A // 2, 1), pltpu.roll(xh, DK_A // 2, 1))
        out.append(xh * cos + partner * sin)
    return jnp.concatenate(out, axis=1)


def _inproj_body(x_ref, sc_ref, sh_ref, g_ref, cos_ref, sin_ref, w_ref,
                 q_ref, k_ref, kb_ref, v_ref, vb_ref, qk_ref, vm_ref, om_ref, ga_ref, gb_ref, gt_ref, vt_ref, kt_ref):
    x = x_ref[...]
    ms = jnp.mean(x * x, axis=1, keepdims=True)
    h = (x * lax.rsqrt(ms + EPS)) * g_ref[...] * (1.0 + sc_ref[...]) + sh_ref[...]
    hb = h.astype(BF16)
    cos = cos_ref[...]
    sin = sin_ref[...]

    def seg(lo, n):
        return _dot(hb, w_ref[:, lo:lo + n])

    q = _rope(seg(C_QA, W_QA), cos, sin) * (DK_A ** -0.5 * LOG2_E)
    q_ref[...] = q.astype(BF16)
    k = _rope(seg(C_KA, W_QA), cos, sin)
    k_ref[...] = k
    kb_ref[...] = k.astype(BF16)
    kt_ref[...] = k.T
    v = seg(C_VA, W_VA)
    v_ref[...] = v
    vb_ref[...] = v.astype(BF16)
    vt_ref[...] = v.T.astype(BF16)
    qk_ref[...] = seg(C_QK, 2 * W_M)
    vm_ref[...] = seg(C_VM, W_M).astype(BF16)
    om_ref[...] = seg(C_OM, W_M)
    ga_ref[...] = seg(C_GA, D_MODEL)
    gb_ref[...] = seg(C_GB, D_MODEL)
    gt_ref[...] = seg(C_GT, LANES)


def _inproj(x, sc, sh, g1, cos, sin, w_packed, rows_per_mod):
    n = x.shape[0]
    nt = n // TOK_TILE
    tiles_per_group = nt // sc.shape[0]
    tab_tiles = cos.shape[0] // TOK_TILE
    tok = lambda w: pl.BlockSpec((TOK_TILE, w), lambda i: (i, 0))
    mod = pl.BlockSpec((None, rows_per_mod, D_MODEL), lambda i: (i // tiles_per_group, 0, 0))
    tab = pl.BlockSpec((TOK_TILE, LANES), lambda i: (i % tab_tiles, 0))
    widths = [(W_QA, BF16), (W_QA, F32), (W_QA, BF16), (W_VA, F32), (W_VA, BF16), (2 * W_M, F32),
              (W_M, BF16), (W_M, F32), (D_MODEL, F32), (D_MODEL, F32), (LANES, F32)]
    vt_spec = pl.BlockSpec((None, W_VA, TOK_TILE), lambda i: (i, 0, 0))
    kt_spec = pl.BlockSpec((None, W_QA, TOK_TILE), lambda i: (i // tab_tiles, 0, i % tab_tiles))
    return pl.pallas_call(
        _inproj_body,
        grid=(nt,),
        in_specs=[tok(D_MODEL), mod, mod, pl.BlockSpec((1, D_MODEL), lambda i: (0, 0)), tab, tab,
                  pl.BlockSpec((D_MODEL, D_IN_PACKED), lambda i: (0, 0), pipeline_mode=pl.Buffered(1))],
        out_specs=[tok(w) for w, _ in widths] + [vt_spec, kt_spec],
        out_shape=[jax.ShapeDtypeStruct((n, w), dt) for w, dt in widths]
                  + [jax.ShapeDtypeStruct((nt, W_VA, TOK_TILE), BF16),
                     jax.ShapeDtypeStruct((nt // tab_tiles, W_QA, tab_tiles * TOK_TILE), F32)],
        compiler_params=_params(("arbitrary",)),
        name="inproj",
    )(x, sc, sh, g1, cos, sin, w_packed)


def _rope_tables(pos):
    half = DK_A // 2
    inv = ROPE_THETA ** (-np.arange(half, dtype=np.float64) * 2.0 / DK_A)
    ang = np.asarray(pos, np.float64)[:, None] * inv[None, :]
    cos = np.cos(ang)
    sin = np.sin(ang)
    cos64 = np.concatenate([cos, cos], axis=1)
    sin64 = np.concatenate([-sin, sin], axis=1)
    return (jnp.asarray(np.tile(cos64, (1, LANES // DK_A)), F32),
            jnp.asarray(np.tile(sin64, (1, LANES // DK_A)), F32))


def _lambda_value(lam_ref, lam_init):
    lv = lam_ref[...]
    l1 = jnp.sum(lv[0:1, :] * lv[1:2, :], axis=1, keepdims=True)
    l2 = jnp.sum(lv[2:3, :] * lv[3:4, :], axis=1, keepdims=True)
    return jnp.exp(l1) - jnp.exp(l2) + lam_init


def _subln(o, g, lam_init):
    ms = jnp.mean(o * o, axis=1, keepdims=True)
    return (o * lax.rsqrt(ms + EPS)) * g * (1.0 - lam_init)


def _attn_prompt_body(q_ref, k_ref, vt_ref, lam_ref, g_ref, o_ref, *scratch, lam_init):
    m_scr, acc_scr = scratch[:ATT_HEADS], scratch[ATT_HEADS:]
    i = pl.program_id(2)
    tq = ATT_TILE
    lane = lax.broadcasted_iota(jnp.int32, (tq, LANES), 1)
    qs = []
    for hh in range(ATT_HEADS):
        q = q_ref[:, hh * LANES:(hh + 1) * LANES]
        zero = jnp.zeros_like(q)
        qs.append(jnp.concatenate([jnp.where(lane < DK_A, q, zero), jnp.where(lane >= DK_A, q, zero)], axis=0))
    for hh in range(ATT_HEADS):
        m_scr[hh][...] = jnp.full(m_scr[hh].shape, NEG_BIG, F32)
        acc_scr[hh][...] = jnp.zeros(acc_scr[hh].shape, F32)
    ones = jnp.ones((ROW_ALIGN, tq), BF16)

    def scores(hh, j):
        start = pl.multiple_of(j * tq, tq)
        return _dot_nt(k_ref[pl.ds(start, tq), hh * LANES:(hh + 1) * LANES], qs[hh])

    def update_all(j, mask):
        sts = [scores(hh, j) for hh in range(ATT_HEADS)]
        if mask is not None:
            sts = [jnp.where(mask, st, NEG_BIG) for st in sts]
        pts, alphas = [], []
        for hh in range(ATT_HEADS):
            m_old = m_scr[hh][...]
            m_new = jnp.maximum(m_old, jnp.max(sts[hh], axis=0, keepdims=True))
            alphas.append(jnp.exp2(m_old - m_new))
            pts.append(jnp.exp2(sts[hh] - m_new).astype(BF16))
            m_scr[hh][...] = m_new
        for hh in range(ATT_HEADS):
            vt = jnp.concatenate([vt_ref[j, hh * LANES:(hh + 1) * LANES, :], ones], axis=0)
            acc_scr[hh][...] = alphas[hh] * acc_scr[hh][...] + _dot(vt, pts[hh])

    def off_diag(j, carry):
        update_all(j, None)
        return carry

    lax.fori_loop(0, i, off_diag, 0)
    key = lax.broadcasted_iota(jnp.int32, (tq, 2 * tq), 0)
    qry = lax.broadcasted_iota(jnp.int32, (tq, 2 * tq), 1) % tq
    update_all(i, key <= qry)
    lam = _lambda_value(lam_ref, lam_init)
    for hh in range(ATT_HEADS):
        acc = acc_scr[hh][...]
        ot = acc[:DV_A] / acc[DV_A:DV_A + 1]
        at = ot[:, :tq] - lam * ot[:, tq:]
        ms = jnp.mean(at * at, axis=0, keepdims=True)
        at = (at * lax.rsqrt(ms + EPS)) * g_ref[...] * (1.0 - lam_init)
        o_ref[:, hh * LANES:(hh + 1) * LANES] = at.T.astype(BF16)


def _attn_prompt(q, k, vt, lam_vec, g_subln_col, batch, seq, lam_init):
    nq = seq // ATT_TILE
    width = ATT_HEADS * LANES
    kv = pl.BlockSpec((seq, width), lambda b, g, i: (b, g))
    vts = pl.BlockSpec((nq, width, ATT_TILE), lambda b, g, i: (b, g, 0))
    qo = pl.BlockSpec((ATT_TILE, width), lambda b, g, i: (b * nq + i, g))
    return pl.pallas_call(
        functools.partial(_attn_prompt_body, lam_init=lam_init),
        grid=(batch, H_A // ATT_HEADS, nq),
        in_specs=[qo, kv, vts, pl.BlockSpec((4, DK_A), lambda b, g, i: (0, 0)),
                  pl.BlockSpec((DV_A, 1), lambda b, g, i: (0, 0))],
        out_specs=qo,
        out_shape=jax.ShapeDtypeStruct((batch * seq, W_VA), BF16),
        scratch_shapes=[pltpu.VMEM((1, 2 * ATT_TILE), F32)] * ATT_HEADS
                       + [pltpu.VMEM((DV_A + ROW_ALIGN, 2 * ATT_TILE), F32)] * ATT_HEADS,
        compiler_params=_params(("arbitrary", "arbitrary", "arbitrary")),
        name="attn_prompt",
    )(q, k, vt, lam_vec, g_subln_col)


def _attn_sample_body(pt_ref, q_ref, kn_ref, vn_ref, lam_ref, g_ref, *rest, lam_init, n_steps, t_new):
    k_refs = rest[:PAGES_PER_STEP]
    v_refs = rest[PAGES_PER_STEP:2 * PAGES_PER_STEP]
    o_ref = rest[2 * PAGES_PER_STEP]
    qbd_scr, m_scr, l_scr, acc_scr = rest[2 * PAGES_PER_STEP + 1:]
    s_idx = pl.program_id(1)
    n_rows = 2 * H_A * t_new

    @pl.when(s_idx == 0)
    def _():
        q = q_ref[...].astype(F32)
        qt = jnp.concatenate([q] * (2 * H_A), axis=0)
        row = lax.broadcasted_iota(jnp.int32, qt.shape, 0)
        col = lax.broadcasted_iota(jnp.int32, qt.shape, 1)
        qbd_scr[...] = jnp.where(col // DK_A == row // t_new, qt, 0.0).astype(BF16)
        m_scr[...] = jnp.full(m_scr.shape, NEG_BIG, F32)
        l_scr[...] = jnp.zeros(l_scr.shape, F32)
        acc_scr[...] = jnp.zeros(acc_scr.shape, F32)

    rows_h = 2 * t_new

    def update(s, v_of_head):
        m_old = m_scr[...]
        m_new = jnp.maximum(m_old, jnp.max(s, axis=1, keepdims=True))
        alpha = jnp.exp2(m_old - m_new)
        p = jnp.exp2(s - m_new)
        l_scr[...] = alpha * l_scr[...] + jnp.sum(p, axis=1, keepdims=True)
        pb = p.astype(BF16)
        pv = [_dot(pb[h * rows_h:(h + 1) * rows_h, :], v_of_head(h)) for h in range(H_A)]
        acc_scr[...] = alpha * acc_scr[...] + jnp.concatenate(pv, axis=0)
        m_scr[...] = m_new

    qbd = qbd_scr[...]
    kt = jnp.concatenate([r[...].astype(BF16) for r in k_refs], axis=1)

    def cached_v(h):
        return jnp.concatenate([r[pl.ds(h, PAGE_SIZE, stride=H_A), :].astype(BF16) for r in v_refs], axis=0)

    update(_dot(qbd, kt), cached_v)

    @pl.when(s_idx == n_steps - 1)
    def _():
        zpad = jnp.zeros((PAGE_SIZE - t_new, W_QA), F32)
        kn = jnp.concatenate([kn_ref[...].astype(F32), zpad], axis=0).astype(BF16)
        vn = jnp.concatenate([vn_ref[...].astype(F32), zpad], axis=0).astype(BF16)
        s = _dot_nt(qbd, kn)
        row = lax.broadcasted_iota(jnp.int32, s.shape, 0) % t_new
        col = lax.broadcasted_iota(jnp.int32, s.shape, 1)
        update(jnp.where(col <= row, s, NEG_BIG), lambda h: vn[:, h * DV_A:(h + 1) * DV_A])
        o = acc_scr[...] / l_scr[...]
        lam = _lambda_value(lam_ref, lam_init)
        outs = []
        for h in range(H_A):
            r0 = h * rows_h
            outs.append(_subln(o[r0:r0 + t_new] - lam * o[r0 + t_new:r0 + rows_h], g_ref[...], lam_init))
        o_ref[...] = jnp.concatenate(outs, axis=1).astype(BF16)


def _attn_sample(q, k_new, v_new, cache_k, cache_v, page_table, lam_vec, g_subln, lam_init, t_new):
    bd, n_pages = page_table.shape
    n_steps = n_pages // PAGES_PER_STEP
    n_rows = 2 * H_A * t_new
    new = pl.BlockSpec((None, t_new, W_QA), lambda b, s, pt: (b, 0, 0))
    page = lambda j: pl.BlockSpec((None, W_QA, PAGE_SIZE),
                                  lambda b, s, pt, j=j: (pt[b, s * PAGES_PER_STEP + j], 0, 0))
    grid_spec = pltpu.PrefetchScalarGridSpec(
        num_scalar_prefetch=1,
        grid=(bd, n_steps),
        in_specs=[new, new, new, pl.BlockSpec((4, DK_A), lambda b, s, pt: (0, 0)),
                  pl.BlockSpec((1, DV_A), lambda b, s, pt: (0, 0))]
                 + [page(j) for j in range(PAGES_PER_STEP)] * 2,
        out_specs=new,
        scratch_shapes=[pltpu.VMEM((n_rows, W_QA), BF16), pltpu.VMEM((n_rows, 1), F32),
                        pltpu.VMEM((n_rows, 1), F32), pltpu.VMEM((n_rows, DV_A), F32)],
    )
    return pl.pallas_call(
        functools.partial(_attn_sample_body, lam_init=lam_init, n_steps=n_steps, t_new=t_new),
        grid_spec=grid_spec,
        out_shape=jax.ShapeDtypeStruct((bd, t_new, W_VA), BF16),
        compiler_params=_params(("arbitrary", "arbitrary")),
        name="attn_sample",
    )(page_table, q.reshape(bd, t_new, W_QA), k_new.reshape(bd, t_new, W_QA), v_new.reshape(bd, t_new, W_VA),
      lam_vec, g_subln, *([cache_k] * PAGES_PER_STEP), *([cache_v] * PAGES_PER_STEP)).reshape(bd * t_new, W_VA)


def _mlstm_body(qk_ref, vm_ref, om_ref, gt_ref, cw_ref, cb_ref, bg_ref, gm_ref, cbuf_ref, c0_ref, n0_ref, m0_ref,
                h_ref, cst_ref, cout_ref, nout_ref, mout_ref, ext_scr, c_scr, n_scr, m_scr, *, tb, L, nc):
    c_idx = pl.program_id(1)

    @pl.when(c_idx == 0)
    def _():
        ext_scr[...] = jnp.zeros(ext_scr.shape, F32)
        ext_scr[8 - (CONV_W - 1):8, :] = cbuf_ref[...]
        c_scr[...] = c0_ref[...]
        n_scr[...] = n0_ref[...]
        m_scr[...] = m0_ref[...]

    pad = L - tb
    u = qk_ref[...]
    if pad:
        u = jnp.concatenate([u, jnp.zeros((pad, u.shape[1]), F32)], axis=0)
    full = jnp.concatenate([ext_scr[...], u], axis=0)
    conv = cb_ref[...] + cw_ref[CONV_W - 1:CONV_W, :] * u
    for j in range(CONV_W - 1):
        conv = conv + cw_ref[j:j + 1, :] * pltpu.roll(full, CONV_W - 1 - j, 0)[8:8 + L]
    a = conv * _sigmoid(conv)
    if not pad:
        ext_scr[...] = u[L - 8:L]

    @pl.when(c_idx == nc - 1)
    def _():
        cst_ref[...] = qk_ref[tb - (CONV_W - 1):tb, :]

    g = gt_ref[...] + bg_ref[...]
    li = g
    lf = jnp.minimum(g, 0.0) - jnp.log1p(jnp.exp(-jnp.abs(g)))
    if pad:
        zpad = jnp.zeros((pad, LANES), F32)
        li = jnp.concatenate([li, zpad + NEG_BIG], axis=0)
        lf = jnp.concatenate([lf, zpad], axis=0)
    row = lax.broadcasted_iota(jnp.int32, (L, LANES), 0)
    lane = lax.broadcasted_iota(jnp.int32, (L, LANES), 1)
    bcum = lf
    shift = 1
    while shift < L:
        bcum = bcum + jnp.where(row >= shift, pltpu.roll(bcum, shift, 0), 0.0)
        shift *= 2
    gates = jnp.where(lane < H_M, li, bcum)
    gates_t = gates.T
    tri = lax.broadcasted_iota(jnp.int32, (L, L), 0) >= lax.broadcasted_iota(jnp.int32, (L, L), 1)
    m_all = m_scr[...]
    lane1 = lax.broadcasted_iota(jnp.int32, (1, LANES), 1)
    m_next = m_all
    vall = vm_ref[...]
    if pad:
        vall = jnp.concatenate([vall, jnp.zeros((pad, vall.shape[1]), BF16)], axis=0)

    for h in range(H_M):
        li_col = gates[:, h:h + 1]
        b_col = gates[:, H_M + h:H_M + h + 1]
        li_row = gates_t[h:h + 1, :]
        b_row = gates_t[H_M + h:H_M + h + 1, :]
        m_prev = m_all[:, h:h + 1]
        b_last = b_col[L - 1:L, :]
        log_d = jnp.where(tri, b_col - b_row + li_row, NEG_BIG)
        inter = b_col + m_prev
        mt = jnp.maximum(inter, jnp.max(log_d, axis=1, keepdims=True))
        q = a[:, h * DH_M:(h + 1) * DH_M]
        k = a[:, W_M + h * DH_M:W_M + (h + 1) * DH_M] * (DH_M ** -0.5)
        v = vall[:, h * DH_M:(h + 1) * DH_M]
        qb = q.astype(BF16)
        s = _dot_nt(qb, k.astype(BF16)) * jnp.exp(log_d - mt)
        ei = jnp.exp(inter - mt)
        c_old = c_scr[h]
        n_old = n_scr[h:h + 1, :]
        num = ei * _dot(qb, c_old.astype(BF16)) + _dot(s.astype(BF16), v)
        den = ei * jnp.sum(q * n_old, axis=1, keepdims=True) + jnp.sum(s, axis=1, keepdims=True)
        hh = num / jnp.maximum(jnp.abs(den), jnp.exp(-mt))
        g_col = b_last - b_col + li_col
        bl = b_last + m_prev
        m_new = jnp.maximum(bl, jnp.max(g_col, axis=0, keepdims=True))
        wg = jnp.exp(g_col - m_new)
        decay = jnp.exp(bl - m_new)
        kw = k * wg
        c_scr[h] = decay * c_old + _dot(kw.T.astype(BF16), v)
        n_scr[h:h + 1, :] = decay * n_old + jnp.sum(kw, axis=0, keepdims=True)
        m_next = jnp.where(lane1 == h, m_new, m_next)
        ms = jnp.mean(hh * hh, axis=1, keepdims=True)
        hn = (hh * lax.rsqrt(ms + EPS)) * gm_ref[h:h + 1, :]
        og = _sigmoid(om_ref[:, h * DH_M:(h + 1) * DH_M])
        h_ref[:, h * DH_M:(h + 1) * DH_M] = (hn[:tb] * og).astype(BF16)

    m_scr[...] = m_next

    @pl.when(c_idx == nc - 1)
    def _():
        cout_ref[...] = c_scr[...]
        nout_ref[...] = n_scr[...]
        mout_ref[...] = m_scr[...]


def _mlstm(qk, vm, om, gt, w_conv, b_conv, b_gates_pad, g_mnorm, conv_buf, c0, n0, m0_pad, batch, seq):
    tb = min(seq, MLSTM_CHUNK)
    L = max(tb, LANES)
    nc = seq // tb
    tok = lambda w: pl.BlockSpec((None, tb, w), lambda b, c: (b * nc + c, 0, 0))
    chunks = lambda a: a.reshape(batch * nc, tb, a.shape[-1])
    const = lambda shape: pl.BlockSpec(shape, lambda b, c: (0,) * len(shape))
    per_b = lambda shape: pl.BlockSpec((None,) + shape, lambda b, c: (b,) + (0,) * len(shape))
    h, cst, c_out, n_out, m_out = pl.pallas_call(
        functools.partial(_mlstm_body, tb=tb, L=L, nc=nc),
        grid=(batch, nc),
        in_specs=[tok(2 * W_M), tok(W_M), tok(W_M), tok(LANES), const((CONV_W, 2 * W_M)), const((1, 2 * W_M)),
                  const((1, LANES)), const((H_M, DH_M)), per_b((CONV_W - 1, 2 * W_M)),
                  per_b((H_M, DH_M, DH_M)), per_b((H_M, DH_M)), per_b((1, LANES))],
        out_specs=[tok(W_M), per_b((CONV_W - 1, 2 * W_M)), per_b((H_M, DH_M, DH_M)), per_b((H_M, DH_M)),
                   per_b((1, LANES))],
        out_shape=[jax.ShapeDtypeStruct((batch * nc, tb, W_M), BF16),
                   jax.ShapeDtypeStruct((batch, CONV_W - 1, 2 * W_M), F32),
                   jax.ShapeDtypeStruct((batch, H_M, DH_M, DH_M), F32),
                   jax.ShapeDtypeStruct((batch, H_M, DH_M), F32),
                   jax.ShapeDtypeStruct((batch, 1, LANES), F32)],
        scratch_shapes=[pltpu.VMEM((8, 2 * W_M), F32), pltpu.VMEM((H_M, DH_M, DH_M), F32),
                        pltpu.VMEM((H_M, DH_M), F32), pltpu.VMEM((1, LANES), F32)],
        compiler_params=_params(("arbitrary", "arbitrary")),
        name="mlstm",
    )(chunks(qk), chunks(vm), chunks(om), chunks(gt), w_conv, b_conv, b_gates_pad, g_mnorm, conv_buf, c0, n0,
      m0_pad)
    return h.reshape(batch * seq, W_M), cst, c_out, n_out, m_out


def _merge_body(oa_ref, hm_ref, ga_ref, gb_ref, x_ref, gt1_ref, sc2_ref, sh2_ref, g2_ref, wa_ref, wb_ref, wo_ref,
                wr_ref, br_ref, *rest):
    x2_ref, h2_ref, route_ref, cnt_ref = rest[-4:]
    ya = _dot(oa_ref[...], wa_ref[...])
    yb = _dot(hm_ref[...], wb_ref[...])
    mix = _sigmoid(ga_ref[...]) * ya + _sigmoid(gb_ref[...]) * yb
    y = _dot(mix.astype(BF16), wo_ref[...])
    x2 = x_ref[...] + gt1_ref[...] * y
    x2_ref[...] = x2
    ms = jnp.mean(x2 * x2, axis=1, keepdims=True)
    h2 = (x2 * lax.rsqrt(ms + EPS)) * g2_ref[...] * (1.0 + sc2_ref[...]) + sh2_ref[...]
    h2b = h2.astype(BF16)
    h2_ref[...] = h2b

    tm = h2b.shape[0]
    lane = lax.broadcasted_iota(jnp.int32, (tm, LANES), 1)
    logits = jnp.where(lane < N_EXP, _dot(h2b, wr_ref[...]) + br_ref[...], NEG_BIG)
    work = logits
    vals, hots = [], []
    for _ in range(TOP_K):
        mx = jnp.max(work, axis=1, keepdims=True)
        idx = jnp.min(jnp.where(work == mx, lane, LANES), axis=1, keepdims=True)
        hot = lane == idx
        vals.append(mx)
        hots.append(hot)
        work = jnp.where(hot, 2.0 * NEG_BIG, work)
    es = [jnp.exp(v - vals[0]) for v in vals]
    den = es[0]
    for e in es[1:]:
        den = den + e
    sel = jnp.zeros((tm, LANES), F32)
    for hot in hots:
        sel = jnp.where(hot, 1.0, sel)
    r_i = lax.broadcasted_iota(jnp.int32, (tm, tm), 0)
    c_i = lax.broadcasted_iota(jnp.int32, (tm, tm), 1)
    rank = _dot(jnp.where(c_i < r_i, 1.0, 0.0).astype(BF16), sel.astype(BF16))
    lane_f = lane.astype(F32)
    route = jnp.zeros((tm, LANES), F32)
    for k in range(TOP_K):
        e_k = jnp.sum(jnp.where(hots[k], lane_f, 0.0), axis=1, keepdims=True)
        r_k = jnp.sum(jnp.where(hots[k], rank, 0.0), axis=1, keepdims=True)
        route = jnp.where(lane == k, e_k, route)
        route = jnp.where(lane == TOP_K + k, es[k] / den, route)
        route = jnp.where(lane == 2 * TOP_K + k, r_k, route)
    route_ref[...] = route
    cnt_ref[...] = jnp.sum(sel, axis=0, keepdims=True)


def _merge(oa, hm, ga, gb, x, gt1, sc2, sh2, g2, wa, wb, wo, wr, br, rows_per_mod, tile0, n_all, prev=None):
    n = x.shape[0]
    nt = n // TOK_TILE
    nt_all = n_all // TOK_TILE
    tiles_per_group = nt // gt1.shape[0]
    tok = lambda w: pl.BlockSpec((TOK_TILE, w), lambda i: (i, 0))
    mod = pl.BlockSpec((None, rows_per_mod, D_MODEL), lambda i: (i // tiles_per_group, 0, 0))
    res = lambda shape: pl.BlockSpec(shape, lambda i: (0, 0), pipeline_mode=pl.Buffered(1))
    out_tok = lambda w: pl.BlockSpec((TOK_TILE, w), lambda i: (tile0 + i, 0))
    in_specs = [tok(W_VA), tok(W_M), tok(D_MODEL), tok(D_MODEL), tok(D_MODEL), mod, mod, mod,
                pl.BlockSpec((1, D_MODEL), lambda i: (0, 0)),
                res((W_VA, D_MODEL)), res((W_M, D_MODEL)), res((D_MODEL, D_MODEL)), res((D_MODEL, LANES)),
                pl.BlockSpec((1, LANES), lambda i: (0, 0))]
    args = [oa, hm, ga, gb, x, gt1, sc2, sh2, g2, wa, wb, wo, wr, br]
    aliases = {}
    if prev is not None:
        in_specs += [pl.BlockSpec(memory_space=pl.ANY)] * 4
        aliases = {len(args) + j: j for j in range(4)}
        args += list(prev)
    return pl.pallas_call(
        _merge_body,
        grid=(nt,),
        in_specs=in_specs,
        out_specs=[out_tok(D_MODEL), out_tok(D_MODEL), out_tok(LANES),
                   pl.BlockSpec((None, 1, LANES), lambda i: (tile0 + i, 0, 0))],
        out_shape=[jax.ShapeDtypeStruct((n_all, D_MODEL), F32), jax.ShapeDtypeStruct((n_all, D_MODEL), BF16),
                   jax.ShapeDtypeStruct((n_all, LANES), F32), jax.ShapeDtypeStruct((nt_all, 1, LANES), F32)],
        input_output_aliases=aliases,
        compiler_params=_params(("arbitrary",)),
        name="merge",
    )(*args)


def _segment_copies(src, dst, sem, src_row, dst_row, n_groups, max_groups):
    out = []
    bit = 1
    while bit * 2 <= max_groups:
        bit *= 2
    while bit >= 1:
        off = (n_groups // (2 * bit)) * (2 * bit) * ROW_ALIGN
        rows = bit * ROW_ALIGN
        cp = pltpu.make_async_copy(src.at[pl.ds(pl.multiple_of(src_row + off, ROW_ALIGN), rows)],
                                   dst.at[pl.ds(pl.multiple_of(dst_row + off, ROW_ALIGN), rows)], sem)
        out.append(((n_groups // bit) % 2 == 1, cp))
        bit //= 2
    return out


def _run_copies(copies):
    for pred, cp in copies:
        pl.when(pred)(cp.start)
    for pred, cp in copies:
        pl.when(pred)(cp.wait)


def _slot_rows(route_t, loff_col, k):
    e_row = route_t[k:k + 1, :]
    r_row = route_t[2 * TOP_K + k:2 * TOP_K + k + 1, :]
    sub = lax.broadcasted_iota(jnp.int32, (LANES, route_t.shape[1]), 0).astype(F32)
    return jnp.sum(jnp.where(sub == e_row, loff_col, 0.0), axis=0, keepdims=True) + r_row


def _dispatch_body(g_ref, n_ref, lo_ref, tail_ref, h2_ref, route_ref, loff_ref, xs_ref, loc_scr, zero_scr, sem,
                   *, nt):
    t = pl.program_id(0)
    route_t = route_ref[...].T
    loff_col = loff_ref[...]
    r_i = lax.broadcasted_iota(jnp.int32, (M_LOC, TOK_TILE), 0).astype(F32)
    hit = r_i == _slot_rows(route_t, loff_col, 0)
    for k in range(1, TOP_K):
        hit = jnp.logical_or(hit, r_i == _slot_rows(route_t, loff_col, k))
    onehot = jnp.where(hit, 1.0, 0.0).astype(BF16)
    loc_scr[...] = _dot(onehot, h2_ref[...]).astype(BF16)

    copies = []
    for e in range(N_EXP):
        n16 = (n_ref[t * N_EXP + e] + ROW_ALIGN - 1) // ROW_ALIGN
        copies += _segment_copies(loc_scr, xs_ref, sem, lo_ref[t * N_EXP + e], g_ref[t * N_EXP + e], n16,
                                  TOK_TILE // ROW_ALIGN)
    _run_copies(copies)

    @pl.when(t == nt - 1)
    def _():
        zero_scr[...] = jnp.zeros(zero_scr.shape, BF16)
        tails = []
        for e in range(N_EXP):
            tails += _segment_copies(zero_scr, xs_ref, sem, 0, tail_ref[e], tail_ref[N_EXP + e],
                                     FFN_TILE // ROW_ALIGN - 1)
        _run_copies(tails)


def _dispatch(g_off, cnt, l_off, tail, h2, route, loff_col, rows):
    nt = h2.shape[0] // TOK_TILE
    grid_spec = pltpu.PrefetchScalarGridSpec(
        num_scalar_prefetch=4,
        grid=(nt,),
        in_specs=[pl.BlockSpec((TOK_TILE, D_MODEL), lambda t, *_: (t, 0)),
                  pl.BlockSpec((TOK_TILE, LANES), lambda t, *_: (t, 0)),
                  pl.BlockSpec((None, LANES, 1), lambda t, *_: (t, 0, 0))],
        out_specs=pl.BlockSpec(memory_space=pl.ANY),
        scratch_shapes=[pltpu.VMEM((M_LOC, D_MODEL), BF16), pltpu.VMEM((FFN_TILE, D_MODEL), BF16),
                        pltpu.SemaphoreType.DMA],
    )
    return pl.pallas_call(
        functools.partial(_dispatch_body, nt=nt),
        grid_spec=grid_spec,
        out_shape=jax.ShapeDtypeStruct((rows, D_MODEL), BF16),
        compiler_params=_params(("arbitrary",)),
        name="moe_dispatch",
    )(g_off, cnt, l_off, tail, h2, route, loff_col)


def _ffn_body(be_ref, nu_ref, x_ref, wgu_ref, bgu_ref, wd_ref, bd_ref, y_ref, wgu_scr, wd_scr):
    i = pl.program_id(0)
    prev = be_ref[jnp.maximum(i - 1, 0)]

    @pl.when(jnp.logical_or(i == 0, be_ref[i] != prev))
    def _():
        wgu_scr[...] = wgu_ref[...].astype(BF16)
        wd_scr[...] = wd_ref[...].astype(BF16)

    @pl.when(i < nu_ref[0])
    def _():
        gu = _dot(x_ref[...], wgu_scr[...]) + bgu_ref[...]
        gate = jnp.minimum(gu[:, :D_FF], SWIGLU_LIMIT)
        up = jnp.clip(gu[:, D_FF:], -SWIGLU_LIMIT, SWIGLU_LIMIT)
        act = (up + 1.0) * gate * _sigmoid(SWIGLU_ALPHA * gate)
        y_ref[...] = (_dot(act.astype(BF16), wd_scr[...]) + bd_ref[...]).astype(BF16)


def _ffn(blk_exp, n_used, xs, w_gu, b_gu, w_down, b_down):
    rows = xs.shape[0]
    nblk = rows // FFN_TILE
    row_blk = pl.BlockSpec((FFN_TILE, D_MODEL), lambda i, be, nu: (jnp.minimum(i, nu[0] - 1), 0))
    grid_spec = pltpu.PrefetchScalarGridSpec(
        num_scalar_prefetch=2,
        grid=(nblk,),
        in_specs=[row_blk,
                  pl.BlockSpec((None, D_MODEL, 2 * D_FF), lambda i, be, nu: (be[i], 0, 0)),
                  pl.BlockSpec((None, 1, 2 * D_FF), lambda i, be, nu: (be[i], 0, 0)),
                  pl.BlockSpec((None, D_FF, D_MODEL), lambda i, be, nu: (be[i], 0, 0)),
                  pl.BlockSpec((None, 1, D_MODEL), lambda i, be, nu: (be[i], 0, 0))],
        out_specs=row_blk,
        scratch_shapes=[pltpu.VMEM((D_MODEL, 2 * D_FF), BF16), pltpu.VMEM((D_FF, D_MODEL), BF16)],
    )
    return pl.pallas_call(
        _ffn_body,
        grid_spec=grid_spec,
        out_shape=jax.ShapeDtypeStruct((rows, D_MODEL), BF16),
        compiler_params=_params(("arbitrary",)),
        name="moe_ffn",
    )(blk_exp, n_used, xs, w_gu, b_gu.reshape(N_EXP, 1, 2 * D_FF), w_down, b_down.reshape(N_EXP, 1, D_MODEL))


def _combine_body(g_ref, n_ref, lo_ref, route_ref, loff_ref, x2_ref, gtp_ref, gts_ref, gf_ref, ys_ref,
                  yp_ref, ysm_ref, loc_scr, sem, *, nt_prompt, final):
    t = pl.program_id(0)

    @pl.when(t == 0)
    def _():
        loc_scr[...] = jnp.zeros(loc_scr.shape, BF16)

    copies = []
    for e in range(N_EXP):
        n16 = (n_ref[t * N_EXP + e] + ROW_ALIGN - 1) // ROW_ALIGN
        copies += _segment_copies(ys_ref, loc_scr, sem, g_ref[t * N_EXP + e], lo_ref[t * N_EXP + e], n16,
                                  TOK_TILE // ROW_ALIGN)
    _run_copies(copies)

    route = route_ref[...]
    loff_row = loff_ref[...]
    lane = lax.broadcasted_iota(jnp.int32, (TOK_TILE, LANES), 1).astype(F32)
    c_i = lax.broadcasted_iota(jnp.int32, (TOK_TILE, M_LOC), 1).astype(F32)
    wmat = jnp.zeros((TOK_TILE, M_LOC), F32)
    for k in range(TOP_K):
        e_k = route[:, k:k + 1]
        slot = jnp.sum(jnp.where(lane == e_k, loff_row, 0.0), axis=1, keepdims=True) \
            + route[:, 2 * TOP_K + k:2 * TOP_K + k + 1]
        wmat = jnp.where(c_i == slot, route[:, TOP_K + k:TOP_K + k + 1], wmat)
    moe = _dot(wmat.astype(BF16), loc_scr[...])

    gate = jnp.where(t >= nt_prompt, gts_ref[...], gtp_ref[...])
    xo = x2_ref[...] + gate * moe
    if final:
        ms = jnp.mean(xo * xo, axis=1, keepdims=True)
        xo = (xo * lax.rsqrt(ms + EPS)) * gf_ref[...]

    @pl.when(t < nt_prompt)
    def _():
        yp_ref[...] = xo

    @pl.when(t >= nt_prompt)
    def _():
        ysm_ref[...] = xo


def _combine(g_off, cnt, l_off, route, loff_row, x2, gt2_p, gt2_s, g_final, ys, n_prompt, n_sample, final):
    nt = x2.shape[0] // TOK_TILE
    nt_prompt = n_prompt // TOK_TILE
    tiles_per_batch = nt_prompt // gt2_p.shape[0]
    grid_spec = pltpu.PrefetchScalarGridSpec(
        num_scalar_prefetch=3,
        grid=(nt,),
        in_specs=[pl.BlockSpec((TOK_TILE, LANES), lambda t, *_: (t, 0)),
                  pl.BlockSpec((None, 1, LANES), lambda t, *_: (t, 0, 0)),
                  pl.BlockSpec((TOK_TILE, D_MODEL), lambda t, *_: (t, 0)),
                  pl.BlockSpec((None, 1, D_MODEL),
                               lambda t, *_: (jnp.minimum(t, nt_prompt - 1) // tiles_per_batch, 0, 0)),
                  pl.BlockSpec((TOK_TILE, D_MODEL), lambda t, *_: (0, 0)),
                  pl.BlockSpec((1, D_MODEL), lambda t, *_: (0, 0)),
                  pl.BlockSpec(memory_space=pl.ANY)],
        out_specs=[pl.BlockSpec((TOK_TILE, D_MODEL), lambda t, *_: (jnp.minimum(t, nt_prompt - 1), 0)),
                   pl.BlockSpec((TOK_TILE, D_MODEL), lambda t, *_: (0, 0))],
        scratch_shapes=[pltpu.VMEM((M_LOC, D_MODEL), BF16), pltpu.SemaphoreType.DMA],
    )
    return pl.pallas_call(
        functools.partial(_combine_body, nt_prompt=nt_prompt, final=final),
        grid_spec=grid_spec,
        out_shape=[jax.ShapeDtypeStruct((n_prompt, D_MODEL), F32), jax.ShapeDtypeStruct((n_sample, D_MODEL), F32)],
        compiler_params=_params(("arbitrary",)),
        name="moe_combine",
    )(g_off, cnt, l_off, route, loff_row, x2, gt2_p, gt2_s, g_final, ys)


def _moe_offsets(cnt):
    nt = cnt.shape[0]
    seg = (cnt + ROW_ALIGN - 1) // ROW_ALIGN * ROW_ALIGN
    l_off = jnp.cumsum(seg, axis=1) - seg
    gsz = jnp.sum(seg, axis=0)
    gpad = (gsz + FFN_TILE - 1) // FFN_TILE * FFN_TILE
    gstart = jnp.cumsum(gpad) - gpad
    g_off = gstart[None, :] + jnp.cumsum(seg, axis=0) - seg
    nblk_e = gpad // FFN_TILE
    blk_end = jnp.cumsum(nblk_e)
    n_used = jnp.maximum(blk_end[-1], 1)
    rows = (nt * TOK_TILE * TOP_K + nt * N_EXP * (ROW_ALIGN - 1) + N_EXP * (FFN_TILE - 1)
            + FFN_TILE - 1) // FFN_TILE * FFN_TILE
    blk = jnp.minimum(jnp.arange(rows // FFN_TILE, dtype=jnp.int32), n_used - 1)
    blk_exp = jnp.minimum(jnp.sum(blk[:, None] >= blk_end[None, :], axis=1), N_EXP - 1)
    tail = jnp.concatenate([gstart + gsz, (gpad - gsz) // ROW_ALIGN])
    i32 = lambda a: a.astype(jnp.int32)
    return (i32(g_off).reshape(-1), i32(cnt).reshape(-1), i32(l_off).reshape(-1), i32(tail), i32(blk_exp),
            i32(n_used).reshape(1), l_off.astype(F32), rows)


def _pack_w_in(w_in):
    pad = jnp.zeros((D_MODEL, LANES - N_GATE), w_in.dtype)
    g0 = C_GA
    return jnp.concatenate([w_in[:, :g0], w_in[:, g0 + N_GATE:], w_in[:, g0:g0 + N_GATE], pad],
                           axis=1).astype(BF16)


def _pad_lanes(a, value=0.0):
    return jnp.pad(a, [(0, 0)] * (a.ndim - 1) + [(0, LANES - a.shape[-1])], constant_values=value)


def kernel(x_prompt, x_sample, c_prompt, c_sample, cache_k, cache_v, state_conv, state_C, state_n, state_m, page_table, w_ada, b_ada, g_norm1, g_norm2, w_in, b_gates, lambda_q1, lambda_k1, lambda_q2, lambda_k2, g_subln, w_conv, b_conv, g_mnorm, w_up_a, w_up_b, w_out, w_router, b_router, w_gu, b_gu, w_down, b_down, g_final):
    B, S, D = x_prompt.shape
    Bd, Td, _ = x_sample.shape
    depth = w_in.shape[0]
    n_pool = cache_k.shape[1]
    past_len = page_table.shape[1] * PAGE_SIZE
    n_p, n_s = B * S, Bd * Td
    n_all = n_p + n_s
    assert D == D_MODEL and n_s == TOK_TILE and S % MLSTM_CHUNK == 0 and n_p % TOK_TILE == 0
    assert page_table.shape[1] % PAGES_PER_STEP == 0

    cos_p, sin_p = _rope_tables(np.arange(S))
    cos_s, sin_s = _rope_tables(np.tile(past_len + np.arange(Td), Bd))
    hp = x_prompt.reshape(n_p, D)
    hs = x_sample.reshape(n_s, D)
    c_all = jnp.concatenate([c_prompt, c_sample], axis=0)
    outs = [[] for _ in range(12)]

    for l in range(depth):
        lam_init = 0.8 - 0.6 * math.exp(-0.3 * l)
        mod = _ada(c_all, w_ada[l], b_ada[l])
        mods = [mod[:, j * D:(j + 1) * D] for j in range(6)]
        mp = [m[:B].reshape(B, 1, D) for m in mods]
        ms_ = [jnp.repeat(m[B:], Td, axis=0).reshape(1, n_s, D) for m in mods]
        w_packed = _pack_w_in(w_in[l])
        g1 = g_norm1[l].reshape(1, D)
        lam_vec = jnp.stack([lambda_q1[l], lambda_k1[l], lambda_q2[l], lambda_k2[l]])
        gsub = g_subln[l].reshape(1, DV_A)
        bg = _pad_lanes(b_gates[l].reshape(1, N_GATE))
        cw, cb = w_conv[l], b_conv[l].reshape(1, 2 * W_M)

        (q_p, _, kb_p, v_p, _, qk_p, vm_p, om_p, ga_p, gb_p, gt_p, vt_p, kt_p) = _inproj(
            hp, mp[1], mp[0], g1, cos_p, sin_p, w_packed, 1)
        (q_s, k_s, kb_s, v_s, vb_s, qk_s, vm_s, om_s, ga_s, gb_s, gt_s, _, _) = _inproj(
            hs, ms_[1], ms_[0], g1, cos_s, sin_s, w_packed, n_s)
        k_p = jnp.transpose(kt_p.reshape(B, H_A, 2, DK_A, S), (0, 4, 1, 2, 3))

        oa_p = _attn_prompt(q_p, kb_p, vt_p, lam_vec, gsub.reshape(DV_A, 1), B, S, lam_init)
        kt_pool = jnp.transpose(cache_k[l], (0, 2, 3, 4, 1)).reshape(n_pool, W_QA, PAGE_SIZE)
        v_pool = cache_v[l].reshape(n_pool, PAGE_SIZE * H_A, DV_A)
        oa_s = _attn_sample(q_s, kb_s, vb_s, kt_pool, v_pool, page_table, lam_vec, gsub, lam_init, Td)

        zeros = lambda *shape: jnp.zeros(shape, F32)
        hm_p, cst_p, C_p, nn_p, m_p = _mlstm(qk_p, vm_p, om_p, gt_p, cw, cb, bg, g_mnorm[l],
                                             zeros(B, CONV_W - 1, 2 * W_M), zeros(B, H_M, DH_M, DH_M),
                                             zeros(B, H_M, DH_M), zeros(B, 1, LANES), B, S)
        hm_s, cst_s, C_s, nn_s, m_s = _mlstm(qk_s, vm_s, om_s, gt_s, cw, cb, bg, g_mnorm[l],
                                             state_conv[l], state_C[l], state_n[l],
                                             _pad_lanes(state_m[l]).reshape(Bd, 1, LANES), Bd, Td)

        wa, wb, wo = w_up_a[l].astype(BF16), w_up_b[l].astype(BF16), w_out[l].astype(BF16)
        wr = _pad_lanes(w_router[l]).astype(BF16)
        br = _pad_lanes(b_router[l].reshape(1, N_EXP))
        g2 = g_norm2[l].reshape(1, D)
        part = _merge(oa_p, hm_p, ga_p, gb_p, hp, mp[2], mp[4], mp[3], g2, wa, wb, wo, wr, br, 1, 0, n_all)
        x2, h2, route, cnt = _merge(oa_s, hm_s, ga_s, gb_s, hs, ms_[2], ms_[4], ms_[3], g2, wa, wb, wo, wr, br,
                                    n_s, n_p // TOK_TILE, n_all, prev=part)

        g_off, cnt_i, l_off, tail, blk_exp, n_used, loff_f, rows = _moe_offsets(
            jnp.round(cnt[:, 0, :N_EXP]).astype(jnp.int32))
        loff_pad = _pad_lanes(loff_f)
        xs = _dispatch(g_off, cnt_i, l_off, tail, h2, route, loff_pad[:, :, None], rows)
        ys = _ffn(blk_exp, n_used, xs, w_gu[l], b_gu[l], w_down[l], b_down[l])
        final = l == depth - 1
        hp, hs = _combine(g_off, cnt_i, l_off, route, loff_pad[:, None, :], x2, mp[5], ms_[5][0],
                          g_final.reshape(1, D), ys, n_p, n_s, final)

        for j, a in enumerate([k_p, v_p.reshape(B, S, H_A, DV_A), cst_p, C_p, nn_p,
                               m_p[:, 0, :H_M],
                               k_s.reshape(Bd, Td, H_A, 2, DK_A), v_s.reshape(Bd, Td, H_A, DV_A), cst_s, C_s, nn_s,
                               m_s[:, 0, :H_M]]):
            outs[j].append(a)

    return (hp.reshape(B, S, D), hs.reshape(Bd, Td, D)) + tuple(jnp.stack(o) for o in outs)
```

```python
import functools
import math

import numpy as np
import jax
import jax.numpy as jnp
from jax import lax
from jax.experimental import pallas as pl
from jax.experimental.pallas import tpu as pltpu

F32 = jnp.float32
BF16 = jnp.bfloat16

D_MODEL = 1024
H_A = 4
DK_A = 64
DV_A = 2 * DK_A
ROPE_THETA = 10000.0
H_M = 4
DH_M = 128
CONV_W = 4
N_EXP = 32
TOP_K = 4
D_FF = D_MODEL
SWIGLU_LIMIT = 7.0
SWIGLU_ALPHA = 1.702
EPS = 1e-6
PAGE_SIZE = 128

W_QA = H_A * 2 * DK_A
W_VA = H_A * DV_A
W_M = H_M * DH_M
N_GATE = 2 * H_M

LANES = 128
ROW_ALIGN = 16
TOK_TILE = 256
FFN_TILE = 256
ATT_TILE = 256
ATT_HEADS = 4
MLSTM_CHUNK = 256
PAGES_PER_STEP = 8
NEG_BIG = -1e30
LOG2_E = math.log2(math.e)
VMEM_LIMIT = 56 * 1024 * 1024

C_QA, C_KA, C_VA, C_QK, C_VM, C_OM, C_GA, C_GB, C_GT = 0, 512, 1024, 1536, 2560, 3072, 3584, 4608, 5632
D_IN_PACKED = C_GT + LANES
M_LOC = ((TOK_TILE * TOP_K + 2 * N_EXP * (ROW_ALIGN - 1)) + 255) // 256 * 256
GROUPS = M_LOC // ROW_ALIGN
GROUP_UNROLL = 4


def _dot(a, b):
    return jnp.dot(a, b, preferred_element_type=F32)


def _dot_nt(a, b):
    return lax.dot_general(a, b, (((1,), (1,)), ((), ())), preferred_element_type=F32)


def _sigmoid(x):
    return 1.0 / (1.0 + jnp.exp(-x))


def _params(sem):
    return pltpu.CompilerParams(dimension_semantics=sem, vmem_limit_bytes=VMEM_LIMIT)


def _ada_body(c_ref, w_ref, b_ref, o_ref):
    c = c_ref[...]
    s = c * _sigmoid(c)
    s_hi = s.astype(BF16)
    s_lo = (s - s_hi.astype(F32)).astype(BF16)
    w = w_ref[...]
    w_hi = w.astype(BF16)
    w_lo = (w - w_hi.astype(F32)).astype(BF16)
    o_ref[...] = _dot(s_hi, w_hi) + _dot(s_lo, w_hi) + _dot(s_hi, w_lo) + b_ref[...]


def _ada(c_all, w_ada, b_ada):
    rows = c_all.shape[0]
    n_out = w_ada.shape[1]
    blk = 1024
    return pl.pallas_call(
        _ada_body,
        grid=(n_out // blk,),
        in_specs=[pl.BlockSpec((rows, D_MODEL), lambda j: (0, 0)),
                  pl.BlockSpec((D_MODEL, blk), lambda j: (0, j)),
                  pl.BlockSpec((1, blk), lambda j: (0, j))],
        out_specs=pl.BlockSpec((rows, blk), lambda j: (0, j)),
        out_shape=jax.ShapeDtypeStruct((rows, n_out), F32),
        compiler_params=_params(("arbitrary",)),
        name="ada",
    )(c_all, w_ada, b_ada.reshape(1, n_out))


def _rope(z, cos, sin):
    lane = lax.broadcasted_iota(jnp.int32, (z.shape[0], LANES), 1)
    first_half = (lane % DK_A) < (DK_A // 2)
    out = []
    for h in range(H_A):
        xh = z[:, h * LANES:(h + 1) * LANES]
        partner = jnp.where(first_half, pltpu.roll(xh, LANES - DK_A // 2, 1), pltpu.roll(xh, DK_A // 2, 1))
        out.append(xh * cos + partner * sin)
    return jnp.concatenate(out, axis=1)


def _inproj_body(x_ref, sc_ref, sh_ref, g_ref, cos_ref, sin_ref, w_ref,
                 q_ref, k_ref, kb_ref, v_ref, vb_ref, qk_ref, vm_ref, om_ref, ga_ref, gb_ref, gt_ref, vt_ref, kt_ref):
    x = x_ref[...]
    ms = jnp.mean(x * x, axis=1, keepdims=True)
    h = (x * lax.rsqrt(ms + EPS)) * g_ref[...] * (1.0 + sc_ref[...]) + sh_ref[...]
    hb = h.astype(BF16)
    cos = cos_ref[...]
    sin = sin_ref[...]

    def seg(lo, n):
        return _dot(hb, w_ref[:, lo:lo + n])

    q = _rope(seg(C_QA, W_QA), cos, sin) * (DK_A ** -0.5 * LOG2_E)
    q_ref[...] = q.astype(BF16)
    k = _rope(seg(C_KA, W_QA), cos, sin)
    k_ref[...] = k
    kb_ref[...] = k.astype(BF16)
    kt_ref[...] = k.T
    v = seg(C_VA, W_VA)
    v_ref[...] = v
    vb_ref[...] = v.astype(BF16)
    vt_ref[...] = v.T.astype(BF16)
    qk_ref[...] = seg(C_QK, 2 * W_M)
    vm_ref[...] = seg(C_VM, W_M).astype(BF16)
    om_ref[...] = seg(C_OM, W_M)
    ga_ref[...] = seg(C_GA, D_MODEL)
    gb_ref[...] = seg(C_GB, D_MODEL)
    gt_ref[...] = seg(C_GT, LANES)


def _inproj(x, sc, sh, g1, cos, sin, w_packed, rows_per_mod):
    n = x.shape[0]
    nt = n // TOK_TILE
    tiles_per_group = nt // sc.shape[0]
    tab_tiles = cos.shape[0] // TOK_TILE
    tok = lambda w: pl.BlockSpec((TOK_TILE, w), lambda i: (i, 0))
    mod = pl.BlockSpec((None, rows_per_mod, D_MODEL), lambda i: (i // tiles_per_group, 0, 0))
    tab = pl.BlockSpec((TOK_TILE, LANES), lambda i: (i % tab_tiles, 0))
    widths = [(W_QA, BF16), (W_QA, F32), (W_QA, BF16), (W_VA, F32), (W_VA, BF16), (2 * W_M, F32),
              (W_M, BF16), (W_M, F32), (D_MODEL, F32), (D_MODEL, F32), (LANES, F32)]
    vt_spec = pl.BlockSpec((None, W_VA, TOK_TILE), lambda i: (i, 0, 0))
    kt_spec = pl.BlockSpec((None, W_QA, TOK_TILE), lambda i: (i // tab_tiles, 0, i % tab_tiles))
    return pl.pallas_call(
        _inproj_body,
        grid=(nt,),
        in_specs=[tok(D_MODEL), mod, mod, pl.BlockSpec((1, D_MODEL), lambda i: (0, 0)), tab, tab,
                  pl.BlockSpec((D_MODEL, D_IN_PACKED), lambda i: (0, 0), pipeline_mode=pl.Buffered(1))],
        out_specs=[tok(w) for w, _ in widths] + [vt_spec, kt_spec],
        out_shape=[jax.ShapeDtypeStruct((n, w), dt) for w, dt in widths]
                  + [jax.ShapeDtypeStruct((nt, W_VA, TOK_TILE), BF16),
                     jax.ShapeDtypeStruct((nt // tab_tiles, W_QA, tab_tiles * TOK_TILE), F32)],
        compiler_params=_params(("arbitrary",)),
        name="inproj",
    )(x, sc, sh, g1, cos, sin, w_packed)


def _rope_tables(pos):
    half = DK_A // 2
    inv = ROPE_THETA ** (-np.arange(half, dtype=np.float64) * 2.0 / DK_A)
    ang = np.asarray(pos, np.float64)[:, None] * inv[None, :]
    cos = np.cos(ang)
    sin = np.sin(ang)
    cos64 = np.concatenate([cos, cos], axis=1)
    sin64 = np.concatenate([-sin, sin], axis=1)
    return (jnp.asarray(np.tile(cos64, (1, LANES // DK_A)), F32),
            jnp.asarray(np.tile(sin64, (1, LANES // DK_A)), F32))


def _lambda_value(lam_ref, lam_init):
    lv = lam_ref[...]
    l1 = jnp.sum(lv[0:1, :] * lv[1:2, :], axis=1, keepdims=True)
    l2 = jnp.sum(lv[2:3, :] * lv[3:4, :], axis=1, keepdims=True)
    return jnp.exp(l1) - jnp.exp(l2) + lam_init


def _subln(o, g, lam_init):
    ms = jnp.mean(o * o, axis=1, keepdims=True)
    return (o * lax.rsqrt(ms + EPS)) * g * (1.0 - lam_init)


def _attn_prompt_body(q_ref, k_ref, vt_ref, lam_ref, g_ref, o_ref, *scratch, lam_init):
    m_scr, acc_scr = scratch[:ATT_HEADS], scratch[ATT_HEADS:]
    i = pl.program_id(2)
    tq = ATT_TILE
    lane = lax.broadcasted_iota(jnp.int32, (tq, LANES), 1)
    qs = []
    for hh in range(ATT_HEADS):
        q = q_ref[:, hh * LANES:(hh + 1) * LANES]
        zero = jnp.zeros_like(q)
        qs.append(jnp.concatenate([jnp.where(lane < DK_A, q, zero), jnp.where(lane >= DK_A, q, zero)], axis=0))
    for hh in range(ATT_HEADS):
        m_scr[hh][...] = jnp.full(m_scr[hh].shape, NEG_BIG, F32)
        acc_scr[hh][...] = jnp.zeros(acc_scr[hh].shape, F32)
    ones = jnp.ones((ROW_ALIGN, tq), BF16)

    def scores(hh, j):
        start = pl.multiple_of(j * tq, tq)
        return _dot_nt(k_ref[pl.ds(start, tq), hh * LANES:(hh + 1) * LANES], qs[hh])

    def update_all(j, mask):
        sts = [scores(hh, j) for hh in range(ATT_HEADS)]
        if mask is not None:
            sts = [jnp.where(mask, st, NEG_BIG) for st in sts]
        pts, alphas = [], []
        for hh in range(ATT_HEADS):
            m_old = m_scr[hh][...]
            m_new = jnp.maximum(m_old, jnp.max(sts[hh], axis=0, keepdims=True))
            alphas.append(jnp.exp2(m_old - m_new))
            pts.append(jnp.exp2(sts[hh] - m_new).astype(BF16))
            m_scr[hh][...] = m_new
        for hh in range(ATT_HEADS):
            vt = jnp.concatenate([vt_ref[j, hh * LANES:(hh + 1) * LANES, :], ones], axis=0)
            acc_scr[hh][...] = alphas[hh] * acc_scr[hh][...] + _dot(vt, pts[hh])

    def off_diag(j, carry):
        update_all(j, None)
        return carry

    lax.fori_loop(0, i, off_diag, 0)
    key = lax.broadcasted_iota(jnp.int32, (tq, 2 * tq), 0)
    qry = lax.broadcasted_iota(jnp.int32, (tq, 2 * tq), 1) % tq
    update_all(i, key <= qry)
    lam = _lambda_value(lam_ref, lam_init)
    for hh in range(ATT_HEADS):
        acc = acc_scr[hh][...]
        ot = acc[:DV_A] / acc[DV_A:DV_A + 1]
        at = ot[:, :tq] - lam * ot[:, tq:]
        ms = jnp.mean(at * at, axis=0, keepdims=True)
        at = (at * lax.rsqrt(ms + EPS)) * g_ref[...] * (1.0 - lam_init)
        o_ref[:, hh * LANES:(hh + 1) * LANES] = at.T.astype(BF16)


def _attn_prompt(q, k, vt, lam_vec, g_subln_col, batch, seq, lam_init):
    nq = seq // ATT_TILE
    width = ATT_HEADS * LANES
    kv = pl.BlockSpec((seq, width), lambda b, g, i: (b, g))
    vts = pl.BlockSpec((nq, width, ATT_TILE), lambda b, g, i: (b, g, 0))
    qo = pl.BlockSpec((ATT_TILE, width), lambda b, g, i: (b * nq + i, g))
    return pl.pallas_call(
        functools.partial(_attn_prompt_body, lam_init=lam_init),
        grid=(batch, H_A // ATT_HEADS, nq),
        in_specs=[qo, kv, vts, pl.BlockSpec((4, DK_A), lambda b, g, i: (0, 0)),
                  pl.BlockSpec((DV_A, 1), lambda b, g, i: (0, 0))],
        out_specs=qo,
        out_shape=jax.ShapeDtypeStruct((batch * seq, W_VA), BF16),
        scratch_shapes=[pltpu.VMEM((1, 2 * ATT_TILE), F32)] * ATT_HEADS
                       + [pltpu.VMEM((DV_A + ROW_ALIGN, 2 * ATT_TILE), F32)] * ATT_HEADS,
        compiler_params=_params(("arbitrary", "arbitrary", "arbitrary")),
        name="attn_prompt",
    )(q, k, vt, lam_vec, g_subln_col)


def _attn_sample_body(pt_ref, q_ref, kn_ref, vn_ref, lam_ref, g_ref, *rest, lam_init, n_steps, t_new):
    k_refs = rest[:PAGES_PER_STEP]
    v_refs = rest[PAGES_PER_STEP:2 * PAGES_PER_STEP]
    o_ref = rest[2 * PAGES_PER_STEP]
    qbd_scr, m_scr, l_scr, acc_scr = rest[2 * PAGES_PER_STEP + 1:]
    s_idx = pl.program_id(1)
    n_rows = 2 * H_A * t_new

    @pl.when(s_idx == 0)
    def _():
        q = q_ref[...].astype(F32)
        qt = jnp.concatenate([q] * (2 * H_A), axis=0)
        row = lax.broadcasted_iota(jnp.int32, qt.shape, 0)
        col = lax.broadcasted_iota(jnp.int32, qt.shape, 1)
        qbd_scr[...] = jnp.where(col // DK_A == row // t_new, qt, 0.0).astype(BF16)
        m_scr[...] = jnp.full(m_scr.shape, NEG_BIG, F32)
        l_scr[...] = jnp.zeros(l_scr.shape, F32)
        acc_scr[...] = jnp.zeros(acc_scr.shape, F32)

    rows_h = 2 * t_new

    def update(s, v_of_head):
        m_old = m_scr[...]
        m_new = jnp.maximum(m_old, jnp.max(s, axis=1, keepdims=True))
        alpha = jnp.exp2(m_old - m_new)
        p = jnp.exp2(s - m_new)
        l_scr[...] = alpha * l_scr[...] + jnp.sum(p, axis=1, keepdims=True)
        pb = p.astype(BF16)
        pv = [_dot(pb[h * rows_h:(h + 1) * rows_h, :], v_of_head(h)) for h in range(H_A)]
        acc_scr[...] = alpha * acc_scr[...] + jnp.concatenate(pv, axis=0)
        m_scr[...] = m_new

    qbd = qbd_scr[...]
    kt = jnp.concatenate([r[...].astype(BF16) for r in k_refs], axis=1)

    def cached_v(h):
        return jnp.concatenate([r[pl.ds(h, PAGE_SIZE, stride=H_A), :].astype(BF16) for r in v_refs], axis=0)

    update(_dot(qbd, kt), cached_v)

    @pl.when(s_idx == n_steps - 1)
    def _():
        zpad = jnp.zeros((PAGE_SIZE - t_new, W_QA), F32)
        kn = jnp.concatenate([kn_ref[...].astype(F32), zpad], axis=0).astype(BF16)
        vn = jnp.concatenate([vn_ref[...].astype(F32), zpad], axis=0).astype(BF16)
        s = _dot_nt(qbd, kn)
        row = lax.broadcasted_iota(jnp.int32, s.shape, 0) % t_new
        col = lax.broadcasted_iota(jnp.int32, s.shape, 1)
        update(jnp.where(col <= row, s, NEG_BIG), lambda h: vn[:, h * DV_A:(h + 1) * DV_A])
        o = acc_scr[...] / l_scr[...]
        lam = _lambda_value(lam_ref, lam_init)
        outs = []
        for h in range(H_A):
            r0 = h * rows_h
            outs.append(_subln(o[r0:r0 + t_new] - lam * o[r0 + t_new:r0 + rows_h], g_ref[...], lam_init))
        o_ref[...] = jnp.concatenate(outs, axis=1).astype(BF16)


def _attn_sample(q, k_new, v_new, cache_k, cache_v, page_table, lam_vec, g_subln, lam_init, t_new):
    bd, n_pages = page_table.shape
    n_steps = n_pages // PAGES_PER_STEP
    n_rows = 2 * H_A * t_new
    new = pl.BlockSpec((None, t_new, W_QA), lambda b, s, pt: (b, 0, 0))
    page = lambda j: pl.BlockSpec((None, W_QA, PAGE_SIZE),
                                  lambda b, s, pt, j=j: (pt[b, s * PAGES_PER_STEP + j], 0, 0))
    grid_spec = pltpu.PrefetchScalarGridSpec(
        num_scalar_prefetch=1,
        grid=(bd, n_steps),
        in_specs=[new, new, new, pl.BlockSpec((4, DK_A), lambda b, s, pt: (0, 0)),
                  pl.BlockSpec((1, DV_A), lambda b, s, pt: (0, 0))]
                 + [page(j) for j in range(PAGES_PER_STEP)] * 2,
        out_specs=new,
        scratch_shapes=[pltpu.VMEM((n_rows, W_QA), BF16), pltpu.VMEM((n_rows, 1), F32),
                        pltpu.VMEM((n_rows, 1), F32), pltpu.VMEM((n_rows, DV_A), F32)],
    )
    return pl.pallas_call(
        functools.partial(_attn_sample_body, lam_init=lam_init, n_steps=n_steps, t_new=t_new),
        grid_spec=grid_spec,
        out_shape=jax.ShapeDtypeStruct((bd, t_new, W_VA), BF16),
        compiler_params=_params(("arbitrary", "arbitrary")),
        name="attn_sample",
    )(page_table, q.reshape(bd, t_new, W_QA), k_new.reshape(bd, t_new, W_QA), v_new.reshape(bd, t_new, W_VA),
      lam_vec, g_subln, *([cache_k] * PAGES_PER_STEP), *([cache_v] * PAGES_PER_STEP)).reshape(bd * t_new, W_VA)


def _mlstm_body(qk_ref, vm_ref, om_ref, gt_ref, cw_ref, cb_ref, bg_ref, gm_ref, cbuf_ref, c0_ref, n0_ref, m0_ref,
                h_ref, cst_ref, cout_ref, nout_ref, mout_ref, ext_scr, c_scr, n_scr, m_scr, *, tb, L, nc):
    c_idx = pl.program_id(1)

    @pl.when(c_idx == 0)
    def _():
        ext_scr[...] = jnp.zeros(ext_scr.shape, F32)
        ext_scr[8 - (CONV_W - 1):8, :] = cbuf_ref[...]
        c_scr[...] = c0_ref[...]
        n_scr[...] = n0_ref[...]
        m_scr[...] = m0_ref[...]

    pad = L - tb
    u = qk_ref[...]
    if pad:
        u = jnp.concatenate([u, jnp.zeros((pad, u.shape[1]), F32)], axis=0)
    full = jnp.concatenate([ext_scr[...], u], axis=0)
    conv = cb_ref[...] + cw_ref[CONV_W - 1:CONV_W, :] * u
    for j in range(CONV_W - 1):
        conv = conv + cw_ref[j:j + 1, :] * pltpu.roll(full, CONV_W - 1 - j, 0)[8:8 + L]
    a = conv * _sigmoid(conv)
    if not pad:
        ext_scr[...] = u[L - 8:L]

    @pl.when(c_idx == nc - 1)
    def _():
        cst_ref[...] = qk_ref[tb - (CONV_W - 1):tb, :]

    g = gt_ref[...] + bg_ref[...]
    li = g
    lf = jnp.minimum(g, 0.0) - jnp.log1p(jnp.exp(-jnp.abs(g)))
    if pad:
        zpad = jnp.zeros((pad, LANES), F32)
        li = jnp.concatenate([li, zpad + NEG_BIG], axis=0)
        lf = jnp.concatenate([lf, zpad], axis=0)
    row = lax.broadcasted_iota(jnp.int32, (L, LANES), 0)
    lane = lax.broadcasted_iota(jnp.int32, (L, LANES), 1)
    bcum = lf
    shift = 1
    while shift < L:
        bcum = bcum + jnp.where(row >= shift, pltpu.roll(bcum, shift, 0), 0.0)
        shift *= 2
    gates = jnp.where(lane < H_M, li, bcum)
    gates_t = gates.T
    tri = lax.broadcasted_iota(jnp.int32, (L, L), 0) >= lax.broadcasted_iota(jnp.int32, (L, L), 1)
    m_all = m_scr[...]
    lane1 = lax.broadcasted_iota(jnp.int32, (1, LANES), 1)
    m_next = m_all
    vall = vm_ref[...]
    if pad:
        vall = jnp.concatenate([vall, jnp.zeros((pad, vall.shape[1]), BF16)], axis=0)

    for h in range(H_M):
        li_col = gates[:, h:h + 1]
        b_col = gates[:, H_M + h:H_M + h + 1]
        li_row = gates_t[h:h + 1, :]
        b_row = gates_t[H_M + h:H_M + h + 1, :]
        m_prev = m_all[:, h:h + 1]
        b_last = b_col[L - 1:L, :]
        log_d = jnp.where(tri, b_col - b_row + li_row, NEG_BIG)
        inter = b_col + m_prev
        mt = jnp.maximum(inter, jnp.max(log_d, axis=1, keepdims=True))
        q = a[:, h * DH_M:(h + 1) * DH_M]
        k = a[:, W_M + h * DH_M:W_M + (h + 1) * DH_M] * (DH_M ** -0.5)
        v = vall[:, h * DH_M:(h + 1) * DH_M]
        qb = q.astype(BF16)
        s = _dot_nt(qb, k.astype(BF16)) * jnp.exp(log_d - mt)
        ei = jnp.exp(inter - mt)
        c_old = c_scr[h]
        n_old = n_scr[h:h + 1, :]
        num = ei * _dot(qb, c_old.astype(BF16)) + _dot(s.astype(BF16), v)
        den = ei * jnp.sum(q * n_old, axis=1, keepdims=True) + jnp.sum(s, axis=1, keepdims=True)
        hh = num / jnp.maximum(jnp.abs(den), jnp.exp(-mt))
        g_col = b_last - b_col + li_col
        bl = b_last + m_prev
        m_new = jnp.maximum(bl, jnp.max(g_col, axis=0, keepdims=True))
        wg = jnp.exp(g_col - m_new)
        decay = jnp.exp(bl - m_new)
        kw = k * wg
        c_scr[h] = decay * c_old + _dot(kw.T.astype(BF16), v)
        n_scr[h:h + 1, :] = decay * n_old + jnp.sum(kw, axis=0, keepdims=True)
        m_next = jnp.where(lane1 == h, m_new, m_next)
        ms = jnp.mean(hh * hh, axis=1, keepdims=True)
        hn = (hh * lax.rsqrt(ms + EPS)) * gm_ref[h:h + 1, :]
        og = _sigmoid(om_ref[:, h * DH_M:(h + 1) * DH_M])
        h_ref[:, h * DH_M:(h + 1) * DH_M] = (hn[:tb] * og).astype(BF16)

    m_scr[...] = m_next

    @pl.when(c_idx == nc - 1)
    def _():
        cout_ref[...] = c_scr[...]
        nout_ref[...] = n_scr[...]
        mout_ref[...] = m_scr[...]


def _mlstm(qk, vm, om, gt, w_conv, b_conv, b_gates_pad, g_mnorm, conv_buf, c0, n0, m0_pad, batch, seq):
    tb = min(seq, MLSTM_CHUNK)
    L = max(tb, LANES)
    nc = seq // tb
    tok = lambda w: pl.BlockSpec((None, tb, w), lambda b, c: (b * nc + c, 0, 0))
    chunks = lambda a: a.reshape(batch * nc, tb, a.shape[-1])
    const = lambda shape: pl.BlockSpec(shape, lambda b, c: (0,) * len(shape))
    per_b = lambda shape: pl.BlockSpec((None,) + shape, lambda b, c: (b,) + (0,) * len(shape))
    h, cst, c_out, n_out, m_out = pl.pallas_call(
        functools.partial(_mlstm_body, tb=tb, L=L, nc=nc),
        grid=(batch, nc),
        in_specs=[tok(2 * W_M), tok(W_M), tok(W_M), tok(LANES), const((CONV_W, 2 * W_M)), const((1, 2 * W_M)),
                  const((1, LANES)), const((H_M, DH_M)), per_b((CONV_W - 1, 2 * W_M)),
                  per_b((H_M, DH_M, DH_M)), per_b((H_M, DH_M)), per_b((1, LANES))],
        out_specs=[tok(W_M), per_b((CONV_W - 1, 2 * W_M)), per_b((H_M, DH_M, DH_M)), per_b((H_M, DH_M)),
                   per_b((1, LANES))],
        out_shape=[jax.ShapeDtypeStruct((batch * nc, tb, W_M), BF16),
                   jax.ShapeDtypeStruct((batch, CONV_W - 1, 2 * W_M), F32),
                   jax.ShapeDtypeStruct((batch, H_M, DH_M, DH_M), F32),
                   jax.ShapeDtypeStruct((batch, H_M, DH_M), F32),
                   jax.ShapeDtypeStruct((batch, 1, LANES), F32)],
        scratch_shapes=[pltpu.VMEM((8, 2 * W_M), F32), pltpu.VMEM((H_M, DH_M, DH_M), F32),
                        pltpu.VMEM((H_M, DH_M), F32), pltpu.VMEM((1, LANES), F32)],
        compiler_params=_params(("arbitrary", "arbitrary")),
        name="mlstm",
    )(chunks(qk), chunks(vm), chunks(om), chunks(gt), w_conv, b_conv, b_gates_pad, g_mnorm, conv_buf, c0, n0,
      m0_pad)
    return h.reshape(batch * seq, W_M), cst, c_out, n_out, m_out


def _merge_body(oa_ref, hm_ref, ga_ref, gb_ref, x_ref, gt1_ref, sc2_ref, sh2_ref, g2_ref, wa_ref, wb_ref, wo_ref,
                wr_ref, br_ref, *rest):
    x2_ref, h2_ref, route_ref, cnt_ref = rest[-4:]
    ya = _dot(oa_ref[...], wa_ref[...])
    yb = _dot(hm_ref[...], wb_ref[...])
    mix = _sigmoid(ga_ref[...]) * ya + _sigmoid(gb_ref[...]) * yb
    y = _dot(mix.astype(BF16), wo_ref[...])
    x2 = x_ref[...] + gt1_ref[...] * y
    x2_ref[...] = x2
    ms = jnp.mean(x2 * x2, axis=1, keepdims=True)
    h2 = (x2 * lax.rsqrt(ms + EPS)) * g2_ref[...] * (1.0 + sc2_ref[...]) + sh2_ref[...]
    h2b = h2.astype(BF16)
    h2_ref[...] = h2b

    tm = h2b.shape[0]
    lane = lax.broadcasted_iota(jnp.int32, (tm, LANES), 1)
    logits = jnp.where(lane < N_EXP, _dot(h2b, wr_ref[...]) + br_ref[...], NEG_BIG)
    work = logits
    vals, hots = [], []
    for _ in range(TOP_K):
        mx = jnp.max(work, axis=1, keepdims=True)
        idx = jnp.min(jnp.where(work == mx, lane, LANES), axis=1, keepdims=True)
        hot = lane == idx
        vals.append(mx)
        hots.append(hot)
        work = jnp.where(hot, 2.0 * NEG_BIG, work)
    es = [jnp.exp(v - vals[0]) for v in vals]
    den = es[0]
    for e in es[1:]:
        den = den + e
    sel = jnp.zeros((tm, LANES), F32)
    for hot in hots:
        sel = jnp.where(hot, 1.0, sel)
    r_i = lax.broadcasted_iota(jnp.int32, (tm, tm), 0)
    c_i = lax.broadcasted_iota(jnp.int32, (tm, tm), 1)
    rank = _dot(jnp.where(c_i < r_i, 1.0, 0.0).astype(BF16), sel.astype(BF16))
    lane_f = lane.astype(F32)
    route = jnp.zeros((tm, LANES), F32)
    for k in range(TOP_K):
        e_k = jnp.sum(jnp.where(hots[k], lane_f, 0.0), axis=1, keepdims=True)
        r_k = jnp.sum(jnp.where(hots[k], rank, 0.0), axis=1, keepdims=True)
        route = jnp.where(lane == k, e_k, route)
        route = jnp.where(lane == TOP_K + k, es[k] / den, route)
        route = jnp.where(lane == 2 * TOP_K + k, r_k, route)
    route_ref[...] = route
    cnt_ref[...] = jnp.sum(sel, axis=0, keepdims=True)


def _merge(oa, hm, ga, gb, x, gt1, sc2, sh2, g2, wa, wb, wo, wr, br, rows_per_mod, tile0, n_all, prev=None):
    n = x.shape[0]
    nt = n // TOK_TILE
    nt_all = n_all // TOK_TILE
    tiles_per_group = nt // gt1.shape[0]
    tok = lambda w: pl.BlockSpec((TOK_TILE, w), lambda i: (i, 0))
    mod = pl.BlockSpec((None, rows_per_mod, D_MODEL), lambda i: (i // tiles_per_group, 0, 0))
    res = lambda shape: pl.BlockSpec(shape, lambda i: (0, 0), pipeline_mode=pl.Buffered(1))
    out_tok = lambda w: pl.BlockSpec((TOK_TILE, w), lambda i: (tile0 + i, 0))
    in_specs = [tok(W_VA), tok(W_M), tok(D_MODEL), tok(D_MODEL), tok(D_MODEL), mod, mod, mod,
                pl.BlockSpec((1, D_MODEL), lambda i: (0, 0)),
                res((W_VA, D_MODEL)), res((W_M, D_MODEL)), res((D_MODEL, D_MODEL)), res((D_MODEL, LANES)),
                pl.BlockSpec((1, LANES), lambda i: (0, 0))]
    args = [oa, hm, ga, gb, x, gt1, sc2, sh2, g2, wa, wb, wo, wr, br]
    aliases = {}
    if prev is not None:
        in_specs += [pl.BlockSpec(memory_space=pl.ANY)] * 4
        aliases = {len(args) + j: j for j in range(4)}
        args += list(prev)
    return pl.pallas_call(
        _merge_body,
        grid=(nt,),
        in_specs=in_specs,
        out_specs=[out_tok(D_MODEL), out_tok(D_MODEL), out_tok(LANES),
                   pl.BlockSpec((None, 1, LANES), lambda i: (tile0 + i, 0, 0))],
        out_shape=[jax.ShapeDtypeStruct((n_all, D_MODEL), F32), jax.ShapeDtypeStruct((n_all, D_MODEL), BF16),
                   jax.ShapeDtypeStruct((n_all, LANES), F32), jax.ShapeDtypeStruct((nt_all, 1, LANES), F32)],
        input_output_aliases=aliases,
        compiler_params=_params(("arbitrary",)),
        name="merge",
    )(*args)


def _segment_copies(src, dst, sem, src_row, dst_row, n_groups, max_groups):
    out = []
    bit = 1
    while bit * 2 <= max_groups:
        bit *= 2
    while bit >= 1:
        off = (n_groups // (2 * bit)) * (2 * bit) * ROW_ALIGN
        rows = bit * ROW_ALIGN
        cp = pltpu.make_async_copy(src.at[pl.ds(pl.multiple_of(src_row + off, ROW_ALIGN), rows)],
                                   dst.at[pl.ds(pl.multiple_of(dst_row + off, ROW_ALIGN), rows)], sem)
        out.append(((n_groups // bit) % 2 == 1, cp))
        bit //= 2
    return out


def _run_copies(copies):
    for pred, cp in copies:
        pl.when(pred)(cp.start)
    for pred, cp in copies:
        pl.when(pred)(cp.wait)


def _slot_rows(route_t, loff_col, k):
    e_row = route_t[k:k + 1, :]
    r_row = route_t[2 * TOP_K + k:2 * TOP_K + k + 1, :]
    sub = lax.broadcasted_iota(jnp.int32, (LANES, route_t.shape[1]), 0).astype(F32)
    return jnp.sum(jnp.where(sub == e_row, loff_col, 0.0), axis=0, keepdims=True) + r_row


def _for_groups(n_groups, table_ref, base, make_copy, action):
    def body(i, carry):
        for j in range(GROUP_UNROLL):
            g = i * GROUP_UNROLL + j
            action(make_copy(g, table_ref[base + g]))
        return carry
    lax.fori_loop(0, (n_groups + GROUP_UNROLL - 1) // GROUP_UNROLL, body, 0)


def _dispatch_body(ng_ref, dt_ref, lo_ref, sv_ref, rm_ref, tail_ref, h2_ref, route_ref, soff_ref, xs_ref,
                   loc_scr, carry_scr, zero_scr, sem, *, nt):
    t = pl.program_id(0)
    buf = t % 2

    def copies_of(tt):
        def make(g, row):
            return pltpu.make_async_copy(
                loc_scr.at[tt % 2, pl.ds(pl.multiple_of(g * ROW_ALIGN, ROW_ALIGN), ROW_ALIGN)],
                xs_ref.at[pl.ds(pl.multiple_of(row, ROW_ALIGN), ROW_ALIGN)], sem.at[tt % 2])
        return ng_ref[tt], dt_ref, tt * GROUPS, make

    @pl.when(t == 0)
    def _():
        carry_scr[...] = jnp.zeros(carry_scr.shape, BF16)
        loc_scr[:, M_LOC:, :] = jnp.zeros((2, ROW_ALIGN, D_MODEL), BF16)

    @pl.when(t >= 2)
    def _():
        _for_groups(*copies_of(t - 2), lambda cp: cp.wait())

    route_t = route_ref[...].T
    soff_col = soff_ref[...]
    r_i = lax.broadcasted_iota(jnp.int32, (M_LOC, TOK_TILE), 0).astype(F32)
    onehot = jnp.zeros((M_LOC, TOK_TILE), F32)
    for k in range(TOP_K):
        onehot = jnp.where(r_i == _slot_rows(route_t, soff_col, k), 1.0, onehot)
    loc_scr[buf, 0:M_LOC, :] = _dot(onehot.astype(BF16), h2_ref[...]).astype(BF16)

    for e in range(N_EXP):
        lo = pl.multiple_of(lo_ref[t * N_EXP + e], ROW_ALIGN)
        sv = pl.multiple_of(sv_ref[t * N_EXP + e], ROW_ALIGN)
        first = loc_scr[buf, pl.ds(lo, ROW_ALIGN), :].astype(F32) + carry_scr[e].astype(F32)
        loc_scr[buf, pl.ds(lo, ROW_ALIGN), :] = first.astype(BF16)
        pending = loc_scr[buf, pl.ds(sv, ROW_ALIGN), :]
        carry_scr[e] = jnp.where(rm_ref[t * N_EXP + e] > 0, pending, jnp.zeros_like(pending))

    _for_groups(*copies_of(t), lambda cp: cp.start())

    @pl.when(t == nt - 1)
    def _():
        if nt >= 2:
            _for_groups(*copies_of(t - 1), lambda cp: cp.wait())
        _for_groups(*copies_of(t), lambda cp: cp.wait())
        zero_scr[...] = jnp.zeros(zero_scr.shape, BF16)
        tails = []
        for e in range(N_EXP):
            tails += _segment_copies(zero_scr, xs_ref, sem.at[0], 0, tail_ref[e], tail_ref[N_EXP + e],
                                     FFN_TILE // ROW_ALIGN - 1)
        _run_copies(tails)


def _dispatch(meta, h2, route, soff_col):
    nt = h2.shape[0] // TOK_TILE
    grid_spec = pltpu.PrefetchScalarGridSpec(
        num_scalar_prefetch=6,
        grid=(nt,),
        in_specs=[pl.BlockSpec((TOK_TILE, D_MODEL), lambda t, *_: (t, 0)),
                  pl.BlockSpec((TOK_TILE, LANES), lambda t, *_: (t, 0)),
                  pl.BlockSpec((None, LANES, 1), lambda t, *_: (t, 0, 0))],
        out_specs=pl.BlockSpec(memory_space=pl.ANY),
        scratch_shapes=[pltpu.VMEM((2, M_LOC + ROW_ALIGN, D_MODEL), BF16),
                        pltpu.VMEM((N_EXP, ROW_ALIGN, D_MODEL), BF16),
                        pltpu.VMEM((FFN_TILE, D_MODEL), BF16), pltpu.SemaphoreType.DMA((2,))],
    )
    return pl.pallas_call(
        functools.partial(_dispatch_body, nt=nt),
        grid_spec=grid_spec,
        out_shape=jax.ShapeDtypeStruct((meta["rows"] + 2 * M_LOC, D_MODEL), BF16),
        compiler_params=_params(("arbitrary",)),
        name="moe_dispatch",
    )(meta["n_groups"], meta["dispatch_rows"], meta["slot_start"], meta["carry_start"], meta["carry_rows"],
      meta["tail"], h2, route, soff_col)


def _ffn_body(be_ref, nu_ref, x_ref, wgu_ref, bgu_ref, wd_ref, bd_ref, y_ref, wgu_scr, wd_scr):
    i = pl.program_id(0)
    prev = be_ref[jnp.maximum(i - 1, 0)]

    @pl.when(jnp.logical_or(i == 0, be_ref[i] != prev))
    def _():
        wgu_scr[...] = wgu_ref[...].astype(BF16)
        wd_scr[...] = wd_ref[...].astype(BF16)

    @pl.when(i < nu_ref[0])
    def _():
        gu = _dot(x_ref[...], wgu_scr[...]) + bgu_ref[...]
        gate = jnp.minimum(gu[:, :D_FF], SWIGLU_LIMIT)
        up = jnp.clip(gu[:, D_FF:], -SWIGLU_LIMIT, SWIGLU_LIMIT)
        act = (up + 1.0) * gate * _sigmoid(SWIGLU_ALPHA * gate)
        y_ref[...] = (_dot(act.astype(BF16), wd_scr[...]) + bd_ref[...]).astype(BF16)


def _ffn(blk_exp, n_used, xs, w_gu, b_gu, w_down, b_down, rows):
    nblk = rows // FFN_TILE
    row_blk = pl.BlockSpec((FFN_TILE, D_MODEL), lambda i, be, nu: (jnp.minimum(i, nu[0] - 1), 0))
    grid_spec = pltpu.PrefetchScalarGridSpec(
        num_scalar_prefetch=2,
        grid=(nblk,),
        in_specs=[row_blk,
                  pl.BlockSpec((None, D_MODEL, 2 * D_FF), lambda i, be, nu: (be[i], 0, 0)),
                  pl.BlockSpec((None, 1, 2 * D_FF), lambda i, be, nu: (be[i], 0, 0)),
                  pl.BlockSpec((None, D_FF, D_MODEL), lambda i, be, nu: (be[i], 0, 0)),
                  pl.BlockSpec((None, 1, D_MODEL), lambda i, be, nu: (be[i], 0, 0))],
        out_specs=row_blk,
        scratch_shapes=[pltpu.VMEM((D_MODEL, 2 * D_FF), BF16), pltpu.VMEM((D_FF, D_MODEL), BF16)],
    )
    return pl.pallas_call(
        _ffn_body,
        grid_spec=grid_spec,
        out_shape=jax.ShapeDtypeStruct((rows, D_MODEL), BF16),
        compiler_params=_params(("arbitrary",)),
        name="moe_ffn",
    )(blk_exp, n_used, xs, w_gu, b_gu.reshape(N_EXP, 1, 2 * D_FF), w_down, b_down.reshape(N_EXP, 1, D_MODEL))


def _combine_body(ng_ref, ct_ref, route_ref, soff_ref, x2_ref, gtp_ref, gts_ref, gf_ref, ys_ref,
                  yp_ref, ysm_ref, loc_scr, sem, *, nt, nt_prompt, final):
    t = pl.program_id(0)
    buf = t % 2

    def copies_of(tt):
        def make(g, row):
            return pltpu.make_async_copy(
                ys_ref.at[pl.ds(pl.multiple_of(row, ROW_ALIGN), ROW_ALIGN)],
                loc_scr.at[tt % 2, pl.ds(pl.multiple_of(g * ROW_ALIGN, ROW_ALIGN), ROW_ALIGN)], sem.at[tt % 2])
        return ng_ref[tt], ct_ref, tt * GROUPS, make

    @pl.when(t == 0)
    def _():
        loc_scr[...] = jnp.zeros(loc_scr.shape, BF16)
        _for_groups(*copies_of(0), lambda cp: cp.start())

    @pl.when(t + 1 < nt)
    def _():
        _for_groups(*copies_of(t + 1), lambda cp: cp.start())

    _for_groups(*copies_of(t), lambda cp: cp.wait())

    route = route_ref[...]
    soff_row = soff_ref[...]
    lane = lax.broadcasted_iota(jnp.int32, (TOK_TILE, LANES), 1).astype(F32)
    c_i = lax.broadcasted_iota(jnp.int32, (TOK_TILE, M_LOC), 1).astype(F32)
    wmat = jnp.zeros((TOK_TILE, M_LOC), F32)
    for k in range(TOP_K):
        e_k = route[:, k:k + 1]
        slot = jnp.sum(jnp.where(lane == e_k, soff_row, 0.0), axis=1, keepdims=True) \
            + route[:, 2 * TOP_K + k:2 * TOP_K + k + 1]
        wmat = jnp.where(c_i == slot, route[:, TOP_K + k:TOP_K + k + 1], wmat)
    moe = _dot(wmat.astype(BF16), loc_scr[buf])

    gate = jnp.where(t >= nt_prompt, gts_ref[...], gtp_ref[...])
    xo = x2_ref[...] + gate * moe
    if final:
        ms = jnp.mean(xo * xo, axis=1, keepdims=True)
        xo = (xo * lax.rsqrt(ms + EPS)) * gf_ref[...]

    @pl.when(t < nt_prompt)
    def _():
        yp_ref[...] = xo

    @pl.when(t >= nt_prompt)
    def _():
        ysm_ref[...] = xo


def _combine(meta, route, soff_row, x2, gt2_p, gt2_s, g_final, ys, n_prompt, n_sample, final):
    nt = x2.shape[0] // TOK_TILE
    nt_prompt = n_prompt // TOK_TILE
    tiles_per_batch = nt_prompt // gt2_p.shape[0]
    grid_spec = pltpu.PrefetchScalarGridSpec(
        num_scalar_prefetch=2,
        grid=(nt,),
        in_specs=[pl.BlockSpec((TOK_TILE, LANES), lambda t, *_: (t, 0)),
                  pl.BlockSpec((None, 1, LANES), lambda t, *_: (t, 0, 0)),
                  pl.BlockSpec((TOK_TILE, D_MODEL), lambda t, *_: (t, 0)),
                  pl.BlockSpec((None, 1, D_MODEL),
                               lambda t, *_: (jnp.minimum(t, nt_prompt - 1) // tiles_per_batch, 0, 0)),
                  pl.BlockSpec((TOK_TILE, D_MODEL), lambda t, *_: (0, 0)),
                  pl.BlockSpec((1, D_MODEL), lambda t, *_: (0, 0)),
                  pl.BlockSpec(memory_space=pl.ANY)],
        out_specs=[pl.BlockSpec((TOK_TILE, D_MODEL), lambda t, *_: (jnp.minimum(t, nt_prompt - 1), 0)),
                   pl.BlockSpec((TOK_TILE, D_MODEL), lambda t, *_: (0, 0))],
        scratch_shapes=[pltpu.VMEM((2, M_LOC, D_MODEL), BF16), pltpu.SemaphoreType.DMA((2,))],
    )
    return pl.pallas_call(
        functools.partial(_combine_body, nt=nt, nt_prompt=nt_prompt, final=final),
        grid_spec=grid_spec,
        out_shape=[jax.ShapeDtypeStruct((n_prompt, D_MODEL), F32), jax.ShapeDtypeStruct((n_sample, D_MODEL), F32)],
        compiler_params=_params(("arbitrary",)),
        name="moe_combine",
    )(meta["n_groups"], meta["combine_rows"], route, soff_row, x2, gt2_p, gt2_s, g_final, ys)


def _moe_offsets(cnt):
    nt = cnt.shape[0]
    ra = ROW_ALIGN
    prefix = jnp.cumsum(cnt, axis=0) - cnt
    total = jnp.sum(cnt, axis=0)
    pending = prefix % ra
    used = pending + cnt
    seg = (used + ra - 1) // ra * ra
    lo = jnp.cumsum(seg, axis=1) - seg
    n_groups = jnp.sum(seg, axis=1) // ra
    gpad = (total + FFN_TILE - 1) // FFN_TILE * FFN_TILE
    gstart = jnp.cumsum(gpad) - gpad
    base = gstart[None, :] + prefix // ra * ra
    last = (jnp.arange(nt) == nt - 1)[:, None]
    n_write = jnp.where(last, seg // ra, used // ra)
    carry_start = lo + used // ra * ra
    carry_rows = jnp.where(last, 0, used % ra)
    g = jnp.arange(GROUPS)
    slot_end = (lo + seg) // ra
    owner = jnp.minimum(jnp.sum(g[None, :, None] >= slot_end[:, None, :], axis=2), N_EXP - 1)
    pick = lambda a: jnp.sum(jnp.where(owner[:, :, None] == jnp.arange(N_EXP), a[:, None, :], 0), axis=2)
    k = g[None, :] - pick(lo) // ra
    row = pick(base) + k * ra
    valid = g[None, :] < n_groups[:, None]
    rows = (nt * TOK_TILE * TOP_K + N_EXP * (FFN_TILE - 1) + FFN_TILE - 1) // FFN_TILE * FFN_TILE
    spare = rows + (jnp.arange(nt) % 2)[:, None] * M_LOC + g[None, :] * ra
    combine_rows = jnp.where(valid, row, 0)
    dispatch_rows = jnp.where(valid & (k < pick(n_write)), row, spare)

    nblk_e = gpad // FFN_TILE
    blk_end = jnp.cumsum(nblk_e)
    n_used = jnp.maximum(blk_end[-1], 1)
    blk = jnp.minimum(jnp.arange(rows // FFN_TILE, dtype=jnp.int32), n_used - 1)
    blk_exp = jnp.minimum(jnp.sum(blk[:, None] >= blk_end[None, :], axis=1), N_EXP - 1)
    total16 = (total + ra - 1) // ra * ra
    tail = jnp.concatenate([gstart + total16, (gpad - total16) // ra])
    i32 = lambda a: a.astype(jnp.int32).reshape(-1)
    return dict(n_groups=i32(n_groups), dispatch_rows=i32(dispatch_rows), combine_rows=i32(combine_rows),
                slot_start=i32(lo), carry_start=i32(carry_start), carry_rows=i32(carry_rows), tail=i32(tail),
                blk_exp=i32(blk_exp), n_used=i32(n_used), slot_off=(lo + pending).astype(F32), rows=rows)


def _pack_w_in(w_in):
    pad = jnp.zeros((D_MODEL, LANES - N_GATE), w_in.dtype)
    g0 = C_GA
    return jnp.concatenate([w_in[:, :g0], w_in[:, g0 + N_GATE:], w_in[:, g0:g0 + N_GATE], pad],
                           axis=1).astype(BF16)


def _pad_lanes(a, value=0.0):
    return jnp.pad(a, [(0, 0)] * (a.ndim - 1) + [(0, LANES - a.shape[-1])], constant_values=value)


def kernel(x_prompt, x_sample, c_prompt, c_sample, cache_k, cache_v, state_conv, state_C, state_n, state_m, page_table, w_ada, b_ada, g_norm1, g_norm2, w_in, b_gates, lambda_q1, lambda_k1, lambda_q2, lambda_k2, g_subln, w_conv, b_conv, g_mnorm, w_up_a, w_up_b, w_out, w_router, b_router, w_gu, b_gu, w_down, b_down, g_final):
    B, S, D = x_prompt.shape
    Bd, Td, _ = x_sample.shape
    depth = w_in.shape[0]
    n_pool = cache_k.shape[1]
    past_len = page_table.shape[1] * PAGE_SIZE
    n_p, n_s = B * S, Bd * Td
    n_all = n_p + n_s
    assert D == D_MODEL and n_s == TOK_TILE and S % MLSTM_CHUNK == 0 and n_p % TOK_TILE == 0
    assert page_table.shape[1] % PAGES_PER_STEP == 0

    cos_p, sin_p = _rope_tables(np.arange(S))
    cos_s, sin_s = _rope_tables(np.tile(past_len + np.arange(Td), Bd))
    hp = x_prompt.reshape(n_p, D)
    hs = x_sample.reshape(n_s, D)
    c_all = jnp.concatenate([c_prompt, c_sample], axis=0)
    outs = [[] for _ in range(12)]

    for l in range(depth):
        lam_init = 0.8 - 0.6 * math.exp(-0.3 * l)
        mod = _ada(c_all, w_ada[l], b_ada[l])
        mods = [mod[:, j * D:(j + 1) * D] for j in range(6)]
        mp = [m[:B].reshape(B, 1, D) for m in mods]
        ms_ = [jnp.repeat(m[B:], Td, axis=0).reshape(1, n_s, D) for m in mods]
        w_packed = _pack_w_in(w_in[l])
        g1 = g_norm1[l].reshape(1, D)
        lam_vec = jnp.stack([lambda_q1[l], lambda_k1[l], lambda_q2[l], lambda_k2[l]])
        gsub = g_subln[l].reshape(1, DV_A)
        bg = _pad_lanes(b_gates[l].reshape(1, N_GATE))
        cw, cb = w_conv[l], b_conv[l].reshape(1, 2 * W_M)

        (q_p, _, kb_p, v_p, _, qk_p, vm_p, om_p, ga_p, gb_p, gt_p, vt_p, kt_p) = _inproj(
            hp, mp[1], mp[0], g1, cos_p, sin_p, w_packed, 1)
        (q_s, k_s, kb_s, v_s, vb_s, qk_s, vm_s, om_s, ga_s, gb_s, gt_s, _, _) = _inproj(
            hs, ms_[1], ms_[0], g1, cos_s, sin_s, w_packed, n_s)
        k_p = jnp.transpose(kt_p.reshape(B, H_A, 2, DK_A, S), (0, 4, 1, 2, 3))

        oa_p = _attn_prompt(q_p, kb_p, vt_p, lam_vec, gsub.reshape(DV_A, 1), B, S, lam_init)
        kt_pool = jnp.transpose(cache_k[l], (0, 2, 3, 4, 1)).reshape(n_pool, W_QA, PAGE_SIZE)
        v_pool = cache_v[l].reshape(n_pool, PAGE_SIZE * H_A, DV_A)
        oa_s = _attn_sample(q_s, kb_s, vb_s, kt_pool, v_pool, page_table, lam_vec, gsub, lam_init, Td)

        zeros = lambda *shape: jnp.zeros(shape, F32)
        hm_p, cst_p, C_p, nn_p, m_p = _mlstm(qk_p, vm_p, om_p, gt_p, cw, cb, bg, g_mnorm[l],
                                             zeros(B, CONV_W - 1, 2 * W_M), zeros(B, H_M, DH_M, DH_M),
                                             zeros(B, H_M, DH_M), zeros(B, 1, LANES), B, S)
        hm_s, cst_s, C_s, nn_s, m_s = _mlstm(qk_s, vm_s, om_s, gt_s, cw, cb, bg, g_mnorm[l],
                                             state_conv[l], state_C[l], state_n[l],
                                             _pad_lanes(state_m[l]).reshape(Bd, 1, LANES), Bd, Td)

        wa, wb, wo = w_up_a[l].astype(BF16), w_up_b[l].astype(BF16), w_out[l].astype(BF16)
        wr = _pad_lanes(w_router[l]).astype(BF16)
        br = _pad_lanes(b_router[l].reshape(1, N_EXP))
        g2 = g_norm2[l].reshape(1, D)
        part = _merge(oa_p, hm_p, ga_p, gb_p, hp, mp[2], mp[4], mp[3], g2, wa, wb, wo, wr, br, 1, 0, n_all)
        x2, h2, route, cnt = _merge(oa_s, hm_s, ga_s, gb_s, hs, ms_[2], ms_[4], ms_[3], g2, wa, wb, wo, wr, br,
                                    n_s, n_p // TOK_TILE, n_all, prev=part)

        meta = _moe_offsets(jnp.round(cnt[:, 0, :N_EXP]).astype(jnp.int32))
        soff = _pad_lanes(meta["slot_off"])
        xs = _dispatch(meta, h2, route, soff[:, :, None])
        ys = _ffn(meta["blk_exp"], meta["n_used"], xs, w_gu[l], b_gu[l], w_down[l], b_down[l], meta["rows"])
        final = l == depth - 1
        hp, hs = _combine(meta, route, soff[:, None, :], x2, mp[5], ms_[5][0], g_final.reshape(1, D), ys,
                          n_p, n_s, final)

        for j, a in enumerate([k_p, v_p.reshape(B, S, H_A, DV_A), cst_p, C_p, nn_p,
                               m_p[:, 0, :H_M],
                               k_s.reshape(Bd, Td, H_A, 2, DK_A), v_s.reshape(Bd, Td, H_A, DV_A), cst_s, C_s, nn_s,
                               m_s[:, 0, :H_M]]):
            outs[j].append(a)

    return (hp.reshape(B, S, D), hs.reshape(Bd, Td, D)) + tuple(jnp.stack(o) for o in outs)
```

```python
import functools
import math

import numpy as np
import jax
import jax.numpy as jnp
from jax import lax
from jax.experimental import pallas as pl
from jax.experimental.pallas import tpu as pltpu

F32 = jnp.float32
BF16 = jnp.bfloat16

D_MODEL = 1024
H_A = 4
DK_A = 64
DV_A = 2 * DK_A
ROPE_THETA = 10000.0
H_M = 4
DH_M = 128
CONV_W = 4
N_EXP = 32
TOP_K = 4
D_FF = D_MODEL
SWIGLU_LIMIT = 7.0
SWIGLU_ALPHA = 1.702
EPS = 1e-6
PAGE_SIZE = 128

W_QA = H_A * 2 * DK_A
W_VA = H_A * DV_A
W_M = H_M * DH_M
N_GATE = 2 * H_M

LANES = 128
ROW_ALIGN = 16
TOK_TILE = 256
FFN_TILE = 256
ATT_TILE = 256
ATT_HEADS = 4
MLSTM_CHUNK = 256
PAGES_PER_STEP = 8
NEG_BIG = -1e30
LOG2_E = math.log2(math.e)
VMEM_LIMIT = 56 * 1024 * 1024

C_QA, C_KA, C_VA, C_QK, C_VM, C_OM, C_GA, C_GB, C_GT = 0, 512, 1024, 1536, 2560, 3072, 3584, 4608, 5632
D_IN_PACKED = C_GT + LANES
M_LOC = ((TOK_TILE * TOP_K + 2 * N_EXP * (ROW_ALIGN - 1)) + 255) // 256 * 256
GROUPS = M_LOC // ROW_ALIGN
GROUP_UNROLL = 4


def _dot(a, b):
    return jnp.dot(a, b, preferred_element_type=F32)


def _dot_nt(a, b):
    return lax.dot_general(a, b, (((1,), (1,)), ((), ())), preferred_element_type=F32)


def _sigmoid(x):
    return 1.0 / (1.0 + jnp.exp(-x))


def _params(sem):
    return pltpu.CompilerParams(dimension_semantics=sem, vmem_limit_bytes=VMEM_LIMIT)


def _ada_body(c_ref, w_ref, b_ref, o_ref):
    c = c_ref[...]
    s = c * _sigmoid(c)
    s_hi = s.astype(BF16)
    s_lo = (s - s_hi.astype(F32)).astype(BF16)
    w = w_ref[...]
    w_hi = w.astype(BF16)
    w_lo = (w - w_hi.astype(F32)).astype(BF16)
    o_ref[...] = _dot(s_hi, w_hi) + _dot(s_lo, w_hi) + _dot(s_hi, w_lo) + b_ref[...]


def _ada(c_all, w_ada, b_ada):
    rows = c_all.shape[0]
    n_out = w_ada.shape[1]
    blk = 1024
    return pl.pallas_call(
        _ada_body,
        grid=(n_out // blk,),
        in_specs=[pl.BlockSpec((rows, D_MODEL), lambda j: (0, 0)),
                  pl.BlockSpec((D_MODEL, blk), lambda j: (0, j)),
                  pl.BlockSpec((1, blk), lambda j: (0, j))],
        out_specs=pl.BlockSpec((rows, blk), lambda j: (0, j)),
        out_shape=jax.ShapeDtypeStruct((rows, n_out), F32),
        compiler_params=_params(("arbitrary",)),
        name="ada",
    )(c_all, w_ada, b_ada.reshape(1, n_out))


def _rope(z, cos, sin):
    lane = lax.broadcasted_iota(jnp.int32, (z.shape[0], LANES), 1)
    first_half = (lane % DK_A) < (DK_A // 2)
    out = []
    for h in range(H_A):
        xh = z[:, h * LANES:(h + 1) * LANES]
        partner = jnp.where(first_half, pltpu.roll(xh, LANES - DK_A // 2, 1), pltpu.roll(xh, DK_A // 2, 1))
        out.append(xh * cos + partner * sin)
    return jnp.concatenate(out, axis=1)


def _inproj_body(x_ref, sc_ref, sh_ref, g_ref, cos_ref, sin_ref, w_ref,
                 q_ref, k_ref, kb_ref, v_ref, vb_ref, qk_ref, vm_ref, om_ref, ga_ref, gb_ref, gt_ref, vt_ref, kt_ref):
    x = x_ref[...]
    ms = jnp.mean(x * x, axis=1, keepdims=True)
    h = (x * lax.rsqrt(ms + EPS)) * g_ref[...] * (1.0 + sc_ref[...]) + sh_ref[...]
    hb = h.astype(BF16)
    cos = cos_ref[...]
    sin = sin_ref[...]

    def seg(lo, n):
        return _dot(hb, w_ref[:, lo:lo + n])

    q = _rope(seg(C_QA, W_QA), cos, sin) * (DK_A ** -0.5 * LOG2_E)
    q_ref[...] = q.astype(BF16)
    k = _rope(seg(C_KA, W_QA), cos, sin)
    k_ref[...] = k
    kb_ref[...] = k.astype(BF16)
    kt_ref[...] = k.T
    v = seg(C_VA, W_VA)
    v_ref[...] = v
    vb_ref[...] = v.astype(BF16)
    vt_ref[...] = v.T.astype(BF16)
    qk_ref[...] = seg(C_QK, 2 * W_M)
    vm_ref[...] = seg(C_VM, W_M).astype(BF16)
    om_ref[...] = seg(C_OM, W_M)
    ga_ref[...] = seg(C_GA, D_MODEL)
    gb_ref[...] = seg(C_GB, D_MODEL)
    gt_ref[...] = seg(C_GT, LANES)


def _inproj(x, sc, sh, g1, cos, sin, w_packed, rows_per_mod):
    n = x.shape[0]
    nt = n // TOK_TILE
    tiles_per_group = nt // sc.shape[0]
    tab_tiles = cos.shape[0] // TOK_TILE
    tok = lambda w: pl.BlockSpec((TOK_TILE, w), lambda i: (i, 0))
    mod = pl.BlockSpec((None, rows_per_mod, D_MODEL), lambda i: (i // tiles_per_group, 0, 0))
    tab = pl.BlockSpec((TOK_TILE, LANES), lambda i: (i % tab_tiles, 0))
    widths = [(W_QA, BF16), (W_QA, F32), (W_QA, BF16), (W_VA, F32), (W_VA, BF16), (2 * W_M, F32),
              (W_M, BF16), (W_M, F32), (D_MODEL, F32), (D_MODEL, F32), (LANES, F32)]
    vt_spec = pl.BlockSpec((None, W_VA, TOK_TILE), lambda i: (i, 0, 0))
    kt_spec = pl.BlockSpec((None, W_QA, TOK_TILE), lambda i: (i // tab_tiles, 0, i % tab_tiles))
    return pl.pallas_call(
        _inproj_body,
        grid=(nt,),
        in_specs=[tok(D_MODEL), mod, mod, pl.BlockSpec((1, D_MODEL), lambda i: (0, 0)), tab, tab,
                  pl.BlockSpec((D_MODEL, D_IN_PACKED), lambda i: (0, 0), pipeline_mode=pl.Buffered(1))],
        out_specs=[tok(w) for w, _ in widths] + [vt_spec, kt_spec],
        out_shape=[jax.ShapeDtypeStruct((n, w), dt) for w, dt in widths]
                  + [jax.ShapeDtypeStruct((nt, W_VA, TOK_TILE), BF16),
                     jax.ShapeDtypeStruct((nt // tab_tiles, W_QA, tab_tiles * TOK_TILE), F32)],
        compiler_params=_params(("arbitrary",)),
        name="inproj",
    )(x, sc, sh, g1, cos, sin, w_packed)


def _rope_tables(pos):
    half = DK_A // 2
    inv = ROPE_THETA ** (-np.arange(half, dtype=np.float64) * 2.0 / DK_A)
    ang = np.asarray(pos, np.float64)[:, None] * inv[None, :]
    cos = np.cos(ang)
    sin = np.sin(ang)
    cos64 = np.concatenate([cos, cos], axis=1)
    sin64 = np.concatenate([-sin, sin], axis=1)
    return (jnp.asarray(np.tile(cos64, (1, LANES // DK_A)), F32),
            jnp.asarray(np.tile(sin64, (1, LANES // DK_A)), F32))


def _lambda_value(lam_ref, lam_init):
    lv = lam_ref[...]
    l1 = jnp.sum(lv[0:1, :] * lv[1:2, :], axis=1, keepdims=True)
    l2 = jnp.sum(lv[2:3, :] * lv[3:4, :], axis=1, keepdims=True)
    return jnp.exp(l1) - jnp.exp(l2) + lam_init


def _subln(o, g, lam_init):
    ms = jnp.mean(o * o, axis=1, keepdims=True)
    return (o * lax.rsqrt(ms + EPS)) * g * (1.0 - lam_init)


def _attn_prompt_body(q_ref, k_ref, vt_ref, lam_ref, g_ref, o_ref, *scratch, lam_init):
    m_scr, acc_scr = scratch[:ATT_HEADS], scratch[ATT_HEADS:]
    i = pl.program_id(2)
    tq = ATT_TILE
    lane = lax.broadcasted_iota(jnp.int32, (tq, LANES), 1)
    qs = []
    for hh in range(ATT_HEADS):
        q = q_ref[:, hh * LANES:(hh + 1) * LANES]
        zero = jnp.zeros_like(q)
        qs.append(jnp.concatenate([jnp.where(lane < DK_A, q, zero), jnp.where(lane >= DK_A, q, zero)], axis=0))
    for hh in range(ATT_HEADS):
        m_scr[hh][...] = jnp.full(m_scr[hh].shape, NEG_BIG, F32)
        acc_scr[hh][...] = jnp.zeros(acc_scr[hh].shape, F32)
    ones = jnp.ones((ROW_ALIGN, tq), BF16)

    def scores(hh, j):
        start = pl.multiple_of(j * tq, tq)
        return _dot_nt(k_ref[pl.ds(start, tq), hh * LANES:(hh + 1) * LANES], qs[hh])

    def update_all(j, mask):
        sts = [scores(hh, j) for hh in range(ATT_HEADS)]
        if mask is not None:
            sts = [jnp.where(mask, st, NEG_BIG) for st in sts]
        pts, alphas = [], []
        for hh in range(ATT_HEADS):
            m_old = m_scr[hh][...]
            m_new = jnp.maximum(m_old, jnp.max(sts[hh], axis=0, keepdims=True))
            alphas.append(jnp.exp2(m_old - m_new))
            pts.append(jnp.exp2(sts[hh] - m_new).astype(BF16))
            m_scr[hh][...] = m_new
        for hh in range(ATT_HEADS):
            vt = jnp.concatenate([vt_ref[j, hh * LANES:(hh + 1) * LANES, :], ones], axis=0)
            acc_scr[hh][...] = alphas[hh] * acc_scr[hh][...] + _dot(vt, pts[hh])

    def off_diag(j, carry):
        update_all(j, None)
        return carry

    lax.fori_loop(0, i, off_diag, 0)
    key = lax.broadcasted_iota(jnp.int32, (tq, 2 * tq), 0)
    qry = lax.broadcasted_iota(jnp.int32, (tq, 2 * tq), 1) % tq
    update_all(i, key <= qry)
    lam = _lambda_value(lam_ref, lam_init)
    for hh in range(ATT_HEADS):
        acc = acc_scr[hh][...]
        ot = acc[:DV_A] / acc[DV_A:DV_A + 1]
        at = ot[:, :tq] - lam * ot[:, tq:]
        ms = jnp.mean(at * at, axis=0, keepdims=True)
        at = (at * lax.rsqrt(ms + EPS)) * g_ref[...] * (1.0 - lam_init)
        o_ref[:, hh * LANES:(hh + 1) * LANES] = at.T.astype(BF16)


def _attn_prompt(q, k, vt, lam_vec, g_subln_col, batch, seq, lam_init):
    nq = seq // ATT_TILE
    width = ATT_HEADS * LANES
    kv = pl.BlockSpec((seq, width), lambda b, g, i: (b, g))
    vts = pl.BlockSpec((nq, width, ATT_TILE), lambda b, g, i: (b, g, 0))
    qo = pl.BlockSpec((ATT_TILE, width), lambda b, g, i: (b * nq + i, g))
    return pl.pallas_call(
        functools.partial(_attn_prompt_body, lam_init=lam_init),
        grid=(batch, H_A // ATT_HEADS, nq),
        in_specs=[qo, kv, vts, pl.BlockSpec((4, DK_A), lambda b, g, i: (0, 0)),
                  pl.BlockSpec((DV_A, 1), lambda b, g, i: (0, 0))],
        out_specs=qo,
        out_shape=jax.ShapeDtypeStruct((batch * seq, W_VA), BF16),
        scratch_shapes=[pltpu.VMEM((1, 2 * ATT_TILE), F32)] * ATT_HEADS
                       + [pltpu.VMEM((DV_A + ROW_ALIGN, 2 * ATT_TILE), F32)] * ATT_HEADS,
        compiler_params=_params(("arbitrary", "arbitrary", "arbitrary")),
        name="attn_prompt",
    )(q, k, vt, lam_vec, g_subln_col)


def _attn_sample_body(pt_ref, q_ref, kn_ref, vn_ref, lam_ref, g_ref, kpool_ref, vpool_ref, o_ref,
                      kbuf, vbuf, sem, m_scr, l_scr, acc_scr, *, lam_init, n_chunks, n_seq, t_new):
    b = pl.program_id(0)

    def chunk_copies(bb, c, slot):
        out = []
        for j in range(PAGES_PER_STEP):
            page = pt_ref[bb, c * PAGES_PER_STEP + j]
            out.append(pltpu.make_async_copy(kpool_ref.at[page], kbuf.at[slot, j], sem.at[slot]))
            out.append(pltpu.make_async_copy(vpool_ref.at[page], vbuf.at[slot, j], sem.at[slot]))
        return out

    @pl.when(b == 0)
    def _():
        for cp in chunk_copies(0, 0, 0):
            cp.start()

    q = q_ref[...].astype(F32)
    qt = jnp.concatenate([q] * (2 * H_A), axis=0)
    row = lax.broadcasted_iota(jnp.int32, qt.shape, 0)
    col = lax.broadcasted_iota(jnp.int32, qt.shape, 1)
    qbd = jnp.where(col // DK_A == row // t_new, qt, 0.0).astype(BF16)
    m_scr[...] = jnp.full(m_scr.shape, NEG_BIG, F32)
    l_scr[...] = jnp.zeros(l_scr.shape, F32)
    acc_scr[...] = jnp.zeros(acc_scr.shape, F32)

    rows_h = 2 * t_new

    def update(s, v_of_head):
        m_old = m_scr[...]
        m_new = jnp.maximum(m_old, jnp.max(s, axis=1, keepdims=True))
        alpha = jnp.exp2(m_old - m_new)
        p = jnp.exp2(s - m_new)
        l_scr[...] = alpha * l_scr[...] + jnp.sum(p, axis=1, keepdims=True)
        pb = p.astype(BF16)
        pv = [_dot(pb[h * rows_h:(h + 1) * rows_h, :], v_of_head(h)) for h in range(H_A)]
        acc_scr[...] = alpha * acc_scr[...] + jnp.concatenate(pv, axis=0)
        m_scr[...] = m_new

    def chunk(c, slot):
        @pl.when(c + 1 < n_chunks)
        def _():
            for cp in chunk_copies(b, c + 1, 1 - slot):
                cp.start()

        @pl.when(jnp.logical_and(c + 1 == n_chunks, b + 1 < n_seq))
        def _():
            for cp in chunk_copies(b + 1, 0, 1 - slot):
                cp.start()

        for cp in chunk_copies(b, c, slot):
            cp.wait()
        kt = jnp.concatenate([kbuf[slot, j].astype(BF16) for j in range(PAGES_PER_STEP)], axis=1)

        def cached_v(h):
            return jnp.concatenate([vbuf.at[slot, j][pl.ds(h, PAGE_SIZE, stride=H_A), :].astype(BF16)
                                    for j in range(PAGES_PER_STEP)], axis=0)

        update(_dot(qbd, kt), cached_v)

    def chunk_pair(i, carry):
        chunk(2 * i, 0)
        chunk(2 * i + 1, 1)
        return carry

    lax.fori_loop(0, n_chunks // 2, chunk_pair, 0)

    zpad = jnp.zeros((PAGE_SIZE - t_new, W_QA), F32)
    kn = jnp.concatenate([kn_ref[...].astype(F32), zpad], axis=0).astype(BF16)
    vn = jnp.concatenate([vn_ref[...].astype(F32), zpad], axis=0).astype(BF16)
    s = _dot_nt(qbd, kn)
    row = lax.broadcasted_iota(jnp.int32, s.shape, 0) % t_new
    col = lax.broadcasted_iota(jnp.int32, s.shape, 1)
    update(jnp.where(col <= row, s, NEG_BIG), lambda h: vn[:, h * DV_A:(h + 1) * DV_A])
    o = acc_scr[...] / l_scr[...]
    lam = _lambda_value(lam_ref, lam_init)
    outs = []
    for h in range(H_A):
        r0 = h * rows_h
        outs.append(_subln(o[r0:r0 + t_new] - lam * o[r0 + t_new:r0 + rows_h], g_ref[...], lam_init))
    o_ref[...] = jnp.concatenate(outs, axis=1).astype(BF16)


def _attn_sample(q, k_new, v_new, cache_k, cache_v, page_table, lam_vec, g_subln, lam_init, t_new):
    bd, n_pages = page_table.shape
    n_chunks = n_pages // PAGES_PER_STEP
    assert n_chunks % 2 == 0
    n_rows = 2 * H_A * t_new
    new = pl.BlockSpec((None, t_new, W_QA), lambda b, pt: (b, 0, 0))
    page_buf = pltpu.VMEM((2, PAGES_PER_STEP, W_QA, PAGE_SIZE), F32)
    grid_spec = pltpu.PrefetchScalarGridSpec(
        num_scalar_prefetch=1,
        grid=(bd,),
        in_specs=[new, new, new, pl.BlockSpec((4, DK_A), lambda b, pt: (0, 0)),
                  pl.BlockSpec((1, DV_A), lambda b, pt: (0, 0)),
                  pl.BlockSpec(memory_space=pl.ANY), pl.BlockSpec(memory_space=pl.ANY)],
        out_specs=new,
        scratch_shapes=[page_buf, page_buf, pltpu.SemaphoreType.DMA((2,)), pltpu.VMEM((n_rows, 1), F32),
                        pltpu.VMEM((n_rows, 1), F32), pltpu.VMEM((n_rows, DV_A), F32)],
    )
    return pl.pallas_call(
        functools.partial(_attn_sample_body, lam_init=lam_init, n_chunks=n_chunks, n_seq=bd, t_new=t_new),
        grid_spec=grid_spec,
        out_shape=jax.ShapeDtypeStruct((bd, t_new, W_VA), BF16),
        compiler_params=_params(("arbitrary",)),
        name="attn_sample",
    )(page_table, q.reshape(bd, t_new, W_QA), k_new.reshape(bd, t_new, W_QA), v_new.reshape(bd, t_new, W_VA),
      lam_vec, g_subln, cache_k, cache_v).reshape(bd * t_new, W_VA)


def _mlstm_body(qk_ref, vm_ref, om_ref, gt_ref, cw_ref, cb_ref, bg_ref, gm_ref, cbuf_ref, c0_ref, n0_ref, m0_ref,
                h_ref, cst_ref, cout_ref, nout_ref, mout_ref, ext_scr, c_scr, n_scr, m_scr, *, tb, L, nc):
    c_idx = pl.program_id(1)

    @pl.when(c_idx == 0)
    def _():
        ext_scr[...] = jnp.zeros(ext_scr.shape, F32)
        ext_scr[8 - (CONV_W - 1):8, :] = cbuf_ref[...]
        c_scr[...] = c0_ref[...]
        n_scr[...] = n0_ref[...]
        m_scr[...] = m0_ref[...]

    pad = L - tb
    u = qk_ref[...]
    if pad:
        u = jnp.concatenate([u, jnp.zeros((pad, u.shape[1]), F32)], axis=0)
    full = jnp.concatenate([ext_scr[...], u], axis=0)
    conv = cb_ref[...] + cw_ref[CONV_W - 1:CONV_W, :] * u
    for j in range(CONV_W - 1):
        conv = conv + cw_ref[j:j + 1, :] * pltpu.roll(full, CONV_W - 1 - j, 0)[8:8 + L]
    a = conv * _sigmoid(conv)
    if not pad:
        ext_scr[...] = u[L - 8:L]

    @pl.when(c_idx == nc - 1)
    def _():
        cst_ref[...] = qk_ref[tb - (CONV_W - 1):tb, :]

    g = gt_ref[...] + bg_ref[...]
    li = g
    lf = jnp.minimum(g, 0.0) - jnp.log1p(jnp.exp(-jnp.abs(g)))
    if pad:
        zpad = jnp.zeros((pad, LANES), F32)
        li = jnp.concatenate([li, zpad + NEG_BIG], axis=0)
        lf = jnp.concatenate([lf, zpad], axis=0)
    row = lax.broadcasted_iota(jnp.int32, (L, LANES), 0)
    lane = lax.broadcasted_iota(jnp.int32, (L, LANES), 1)
    bcum = lf
    shift = 1
    while shift < L:
        bcum = bcum + jnp.where(row >= shift, pltpu.roll(bcum, shift, 0), 0.0)
        shift *= 2
    gates = jnp.where(lane < H_M, li, bcum)
    gates_t = gates.T
    tri = lax.broadcasted_iota(jnp.int32, (L, L), 0) >= lax.broadcasted_iota(jnp.int32, (L, L), 1)
    m_all = m_scr[...]
    lane1 = lax.broadcasted_iota(jnp.int32, (1, LANES), 1)
    m_next = m_all
    vall = vm_ref[...]
    if pad:
        vall = jnp.concatenate([vall, jnp.zeros((pad, vall.shape[1]), BF16)], axis=0)

    for h in range(H_M):
        li_col = gates[:, h:h + 1]
        b_col = gates[:, H_M + h:H_M + h + 1]
        li_row = gates_t[h:h + 1, :]
        b_row = gates_t[H_M + h:H_M + h + 1, :]
        m_prev = m_all[:, h:h + 1]
        b_last = b_col[L - 1:L, :]
        log_d = jnp.where(tri, b_col - b_row + li_row, NEG_BIG)
        inter = b_col + m_prev
        mt = jnp.maximum(inter, jnp.max(log_d, axis=1, keepdims=True))
        q = a[:, h * DH_M:(h + 1) * DH_M]
        k = a[:, W_M + h * DH_M:W_M + (h + 1) * DH_M] * (DH_M ** -0.5)
        v = vall[:, h * DH_M:(h + 1) * DH_M]
        qb = q.astype(BF16)
        s = _dot_nt(qb, k.astype(BF16)) * jnp.exp(log_d - mt)
        ei = jnp.exp(inter - mt)
        c_old = c_scr[h]
        n_old = n_scr[h:h + 1, :]
        num = ei * _dot(qb, c_old.astype(BF16)) + _dot(s.astype(BF16), v)
        den = ei * jnp.sum(q * n_old, axis=1, keepdims=True) + jnp.sum(s, axis=1, keepdims=True)
        hh = num / jnp.maximum(jnp.abs(den), jnp.exp(-mt))
        g_col = b_last - b_col + li_col
        bl = b_last + m_prev
        m_new = jnp.maximum(bl, jnp.max(g_col, axis=0, keepdims=True))
        wg = jnp.exp(g_col - m_new)
        decay = jnp.exp(bl - m_new)
        kw = k * wg
        c_scr[h] = decay * c_old + _dot(kw.T.astype(BF16), v)
        n_scr[h:h + 1, :] = decay * n_old + jnp.sum(kw, axis=0, keepdims=True)
        m_next = jnp.where(lane1 == h, m_new, m_next)
        ms = jnp.mean(hh * hh, axis=1, keepdims=True)
        hn = (hh * lax.rsqrt(ms + EPS)) * gm_ref[h:h + 1, :]
        og = _sigmoid(om_ref[:, h * DH_M:(h + 1) * DH_M])
        h_ref[:, h * DH_M:(h + 1) * DH_M] = (hn[:tb] * og).astype(BF16)

    m_scr[...] = m_next

    @pl.when(c_idx == nc - 1)
    def _():
        cout_ref[...] = c_scr[...]
        nout_ref[...] = n_scr[...]
        mout_ref[...] = m_scr[...]


def _mlstm(qk, vm, om, gt, w_conv, b_conv, b_gates_pad, g_mnorm, conv_buf, c0, n0, m0_pad, batch, seq):
    tb = min(seq, MLSTM_CHUNK)
    L = max(tb, LANES)
    nc = seq // tb
    tok = lambda w: pl.BlockSpec((None, tb, w), lambda b, c: (b * nc + c, 0, 0))
    chunks = lambda a: a.reshape(batch * nc, tb, a.shape[-1])
    const = lambda shape: pl.BlockSpec(shape, lambda b, c: (0,) * len(shape))
    per_b = lambda shape: pl.BlockSpec((None,) + shape, lambda b, c: (b,) + (0,) * len(shape))
    h, cst, c_out, n_out, m_out = pl.pallas_call(
        functools.partial(_mlstm_body, tb=tb, L=L, nc=nc),
        grid=(batch, nc),
        in_specs=[tok(2 * W_M), tok(W_M), tok(W_M), tok(LANES), const((CONV_W, 2 * W_M)), const((1, 2 * W_M)),
                  const((1, LANES)), const((H_M, DH_M)), per_b((CONV_W - 1, 2 * W_M)),
                  per_b((H_M, DH_M, DH_M)), per_b((H_M, DH_M)), per_b((1, LANES))],
        out_specs=[tok(W_M), per_b((CONV_W - 1, 2 * W_M)), per_b((H_M, DH_M, DH_M)), per_b((H_M, DH_M)),
                   per_b((1, LANES))],
        out_shape=[jax.ShapeDtypeStruct((batch * nc, tb, W_M), BF16),
                   jax.ShapeDtypeStruct((batch, CONV_W - 1, 2 * W_M), F32),
                   jax.ShapeDtypeStruct((batch, H_M, DH_M, DH_M), F32),
                   jax.ShapeDtypeStruct((batch, H_M, DH_M), F32),
                   jax.ShapeDtypeStruct((batch, 1, LANES), F32)],
        scratch_shapes=[pltpu.VMEM((8, 2 * W_M), F32), pltpu.VMEM((H_M, DH_M, DH_M), F32),
                        pltpu.VMEM((H_M, DH_M), F32), pltpu.VMEM((1, LANES), F32)],
        compiler_params=_params(("arbitrary", "arbitrary")),
        name="mlstm",
    )(chunks(qk), chunks(vm), chunks(om), chunks(gt), w_conv, b_conv, b_gates_pad, g_mnorm, conv_buf, c0, n0,
      m0_pad)
    return h.reshape(batch * seq, W_M), cst, c_out, n_out, m_out


def _merge_body(oa_ref, hm_ref, ga_ref, gb_ref, x_ref, gt1_ref, sc2_ref, sh2_ref, g2_ref, wa_ref, wb_ref, wo_ref,
                wr_ref, br_ref, *rest):
    x2_ref, h2_ref, route_ref, cnt_ref = rest[-4:]
    ya = _dot(oa_ref[...], wa_ref[...])
    yb = _dot(hm_ref[...], wb_ref[...])
    mix = _sigmoid(ga_ref[...]) * ya + _sigmoid(gb_ref[...]) * yb
    y = _dot(mix.astype(BF16), wo_ref[...])
    x2 = x_ref[...] + gt1_ref[...] * y
    x2_ref[...] = x2
    ms = jnp.mean(x2 * x2, axis=1, keepdims=True)
    h2 = (x2 * lax.rsqrt(ms + EPS)) * g2_ref[...] * (1.0 + sc2_ref[...]) + sh2_ref[...]
    h2b = h2.astype(BF16)
    h2_ref[...] = h2b

    tm = h2b.shape[0]
    lane = lax.broadcasted_iota(jnp.int32, (tm, LANES), 1)
    logits = jnp.where(lane < N_EXP, _dot(h2b, wr_ref[...]) + br_ref[...], NEG_BIG)
    work = logits
    vals, hots = [], []
    for _ in range(TOP_K):
        mx = jnp.max(work, axis=1, keepdims=True)
        idx = jnp.min(jnp.where(work == mx, lane, LANES), axis=1, keepdims=True)
        hot = lane == idx
        vals.append(mx)
        hots.append(hot)
        work = jnp.where(hot, 2.0 * NEG_BIG, work)
    es = [jnp.exp(v - vals[0]) for v in vals]
    den = es[0]
    for e in es[1:]:
        den = den + e
    sel = jnp.zeros((tm, LANES), F32)
    for hot in hots:
        sel = jnp.where(hot, 1.0, sel)
    r_i = lax.broadcasted_iota(jnp.int32, (tm, tm), 0)
    c_i = lax.broadcasted_iota(jnp.int32, (tm, tm), 1)
    rank = _dot(jnp.where(c_i < r_i, 1.0, 0.0).astype(BF16), sel.astype(BF16))
    lane_f = lane.astype(F32)
    route = jnp.zeros((tm, LANES), F32)
    for k in range(TOP_K):
        e_k = jnp.sum(jnp.where(hots[k], lane_f, 0.0), axis=1, keepdims=True)
        r_k = jnp.sum(jnp.where(hots[k], rank, 0.0), axis=1, keepdims=True)
        route = jnp.where(lane == k, e_k, route)
        route = jnp.where(lane == TOP_K + k, es[k] / den, route)
        route = jnp.where(lane == 2 * TOP_K + k, r_k, route)
    route_ref[...] = route
    cnt_ref[...] = jnp.sum(sel, axis=0, keepdims=True)


def _merge(oa, hm, ga, gb, x, gt1, sc2, sh2, g2, wa, wb, wo, wr, br, rows_per_mod, tile0, n_all, prev=None):
    n = x.shape[0]
    nt = n // TOK_TILE
    nt_all = n_all // TOK_TILE
    tiles_per_group = nt // gt1.shape[0]
    tok = lambda w: pl.BlockSpec((TOK_TILE, w), lambda i: (i, 0))
    mod = pl.BlockSpec((None, rows_per_mod, D_MODEL), lambda i: (i // tiles_per_group, 0, 0))
    res = lambda shape: pl.BlockSpec(shape, lambda i: (0, 0), pipeline_mode=pl.Buffered(1))
    out_tok = lambda w: pl.BlockSpec((TOK_TILE, w), lambda i: (tile0 + i, 0))
    in_specs = [tok(W_VA), tok(W_M), tok(D_MODEL), tok(D_MODEL), tok(D_MODEL), mod, mod, mod,
                pl.BlockSpec((1, D_MODEL), lambda i: (0, 0)),
                res((W_VA, D_MODEL)), res((W_M, D_MODEL)), res((D_MODEL, D_MODEL)), res((D_MODEL, LANES)),
                pl.BlockSpec((1, LANES), lambda i: (0, 0))]
    args = [oa, hm, ga, gb, x, gt1, sc2, sh2, g2, wa, wb, wo, wr, br]
    aliases = {}
    if prev is not None:
        in_specs += [pl.BlockSpec(memory_space=pl.ANY)] * 4
        aliases = {len(args) + j: j for j in range(4)}
        args += list(prev)
    return pl.pallas_call(
        _merge_body,
        grid=(nt,),
        in_specs=in_specs,
        out_specs=[out_tok(D_MODEL), out_tok(D_MODEL), out_tok(LANES),
                   pl.BlockSpec((None, 1, LANES), lambda i: (tile0 + i, 0, 0))],
        out_shape=[jax.ShapeDtypeStruct((n_all, D_MODEL), F32), jax.ShapeDtypeStruct((n_all, D_MODEL), BF16),
                   jax.ShapeDtypeStruct((n_all, LANES), F32), jax.ShapeDtypeStruct((nt_all, 1, LANES), F32)],
        input_output_aliases=aliases,
        compiler_params=_params(("arbitrary",)),
        name="merge",
    )(*args)


def _segment_copies(src, dst, sem, src_row, dst_row, n_groups, max_groups):
    out = []
    bit = 1
    while bit * 2 <= max_groups:
        bit *= 2
    while bit >= 1:
        off = (n_groups // (2 * bit)) * (2 * bit) * ROW_ALIGN
        rows = bit * ROW_ALIGN
        cp = pltpu.make_async_copy(src.at[pl.ds(pl.multiple_of(src_row + off, ROW_ALIGN), rows)],
                                   dst.at[pl.ds(pl.multiple_of(dst_row + off, ROW_ALIGN), rows)], sem)
        out.append(((n_groups // bit) % 2 == 1, cp))
        bit //= 2
    return out


def _run_copies(copies):
    for pred, cp in copies:
        pl.when(pred)(cp.start)
    for pred, cp in copies:
        pl.when(pred)(cp.wait)


def _slot_rows(route_t, loff_col, k):
    e_row = route_t[k:k + 1, :]
    r_row = route_t[2 * TOP_K + k:2 * TOP_K + k + 1, :]
    sub = lax.broadcasted_iota(jnp.int32, (LANES, route_t.shape[1]), 0).astype(F32)
    return jnp.sum(jnp.where(sub == e_row, loff_col, 0.0), axis=0, keepdims=True) + r_row


def _for_groups(n_groups, table_ref, base, make_copy, action):
    def body(i, carry):
        for j in range(GROUP_UNROLL):
            g = i * GROUP_UNROLL + j
            action(make_copy(g, table_ref[base + g]))
        return carry
    lax.fori_loop(0, (n_groups + GROUP_UNROLL - 1) // GROUP_UNROLL, body, 0)


def _dispatch_body(ng_ref, dt_ref, lo_ref, sv_ref, rm_ref, tail_ref, h2_ref, route_ref, soff_ref, xs_ref,
                   loc_scr, carry_scr, zero_scr, sem, *, nt):
    t = pl.program_id(0)
    buf = t % 2

    def copies_of(tt):
        def make(g, row):
            return pltpu.make_async_copy(
                loc_scr.at[tt % 2, pl.ds(pl.multiple_of(g * ROW_ALIGN, ROW_ALIGN), ROW_ALIGN)],
                xs_ref.at[pl.ds(pl.multiple_of(row, ROW_ALIGN), ROW_ALIGN)], sem.at[tt % 2])
        return ng_ref[tt], dt_ref, tt * GROUPS, make

    @pl.when(t == 0)
    def _():
        carry_scr[...] = jnp.zeros(carry_scr.shape, BF16)
        loc_scr[:, M_LOC:, :] = jnp.zeros((2, ROW_ALIGN, D_MODEL), BF16)

    @pl.when(t >= 2)
    def _():
        _for_groups(*copies_of(t - 2), lambda cp: cp.wait())

    route_t = route_ref[...].T
    soff_col = soff_ref[...]
    r_i = lax.broadcasted_iota(jnp.int32, (M_LOC, TOK_TILE), 0).astype(F32)
    onehot = jnp.zeros((M_LOC, TOK_TILE), F32)
    for k in range(TOP_K):
        onehot = jnp.where(r_i == _slot_rows(route_t, soff_col, k), 1.0, onehot)
    loc_scr[buf, 0:M_LOC, :] = _dot(onehot.astype(BF16), h2_ref[...]).astype(BF16)

    for e in range(N_EXP):
        lo = pl.multiple_of(lo_ref[t * N_EXP + e], ROW_ALIGN)
        sv = pl.multiple_of(sv_ref[t * N_EXP + e], ROW_ALIGN)
        first = loc_scr[buf, pl.ds(lo, ROW_ALIGN), :].astype(F32) + carry_scr[e].astype(F32)
        loc_scr[buf, pl.ds(lo, ROW_ALIGN), :] = first.astype(BF16)
        pending = loc_scr[buf, pl.ds(sv, ROW_ALIGN), :]
        carry_scr[e] = jnp.where(rm_ref[t * N_EXP + e] > 0, pending, jnp.zeros_like(pending))

    _for_groups(*copies_of(t), lambda cp: cp.start())

    @pl.when(t == nt - 1)
    def _():
        if nt >= 2:
            _for_groups(*copies_of(t - 1), lambda cp: cp.wait())
        _for_groups(*copies_of(t), lambda cp: cp.wait())
        zero_scr[...] = jnp.zeros(zero_scr.shape, BF16)
        tails = []
        for e in range(N_EXP):
            tails += _segment_copies(zero_scr, xs_ref, sem.at[0], 0, tail_ref[e], tail_ref[N_EXP + e],
                                     FFN_TILE // ROW_ALIGN - 1)
        _run_copies(tails)


def _dispatch(meta, h2, route, soff_col):
    nt = h2.shape[0] // TOK_TILE
    grid_spec = pltpu.PrefetchScalarGridSpec(
        num_scalar_prefetch=6,
        grid=(nt,),
        in_specs=[pl.BlockSpec((TOK_TILE, D_MODEL), lambda t, *_: (t, 0)),
                  pl.BlockSpec((TOK_TILE, LANES), lambda t, *_: (t, 0)),
                  pl.BlockSpec((None, LANES, 1), lambda t, *_: (t, 0, 0))],
        out_specs=pl.BlockSpec(memory_space=pl.ANY),
        scratch_shapes=[pltpu.VMEM((2, M_LOC + ROW_ALIGN, D_MODEL), BF16),
                        pltpu.VMEM((N_EXP, ROW_ALIGN, D_MODEL), BF16),
                        pltpu.VMEM((FFN_TILE, D_MODEL), BF16), pltpu.SemaphoreType.DMA((2,))],
    )
    return pl.pallas_call(
        functools.partial(_dispatch_body, nt=nt),
        grid_spec=grid_spec,
        out_shape=jax.ShapeDtypeStruct((meta["rows"] + 2 * M_LOC, D_MODEL), BF16),
        compiler_params=_params(("arbitrary",)),
        name="moe_dispatch",
    )(meta["n_groups"], meta["dispatch_rows"], meta["slot_start"], meta["carry_start"], meta["carry_rows"],
      meta["tail"], h2, route, soff_col)


def _ffn_body(be_ref, nu_ref, nx_ref, sl_ref, x_ref, bgu_ref, bd_ref, wgu_hbm, wd_hbm, y_ref,
              wgu_f32, wd_f32, wgu_scr, wd_scr, sem):
    i = pl.program_id(0)
    expert = be_ref[i]
    slot = sl_ref[i]
    prev = be_ref[jnp.maximum(i - 1, 0)]

    def weight_copies(e, s):
        return (pltpu.make_async_copy(wgu_hbm.at[e], wgu_f32.at[s], sem.at[s]),
                pltpu.make_async_copy(wd_hbm.at[e], wd_f32.at[s], sem.at[s]))

    @pl.when(i == 0)
    def _():
        for cp in weight_copies(expert, slot):
            cp.start()

    @pl.when(jnp.logical_and(i < nu_ref[0], jnp.logical_or(i == 0, expert != prev)))
    def _():
        for cp in weight_copies(expert, slot):
            cp.wait()
        wgu_scr[...] = wgu_f32[slot].astype(BF16)
        wd_scr[...] = wd_f32[slot].astype(BF16)

        @pl.when(nx_ref[i] >= 0)
        def _():
            for cp in weight_copies(nx_ref[i], 1 - slot):
                cp.start()

    @pl.when(i < nu_ref[0])
    def _():
        gu = _dot(x_ref[...], wgu_scr[...]) + bgu_ref[...]
        gate = jnp.minimum(gu[:, :D_FF], SWIGLU_LIMIT)
        up = jnp.clip(gu[:, D_FF:], -SWIGLU_LIMIT, SWIGLU_LIMIT)
        act = (up + 1.0) * gate * _sigmoid(SWIGLU_ALPHA * gate)
        y_ref[...] = (_dot(act.astype(BF16), wd_scr[...]) + bd_ref[...]).astype(BF16)


def _ffn(meta, xs, w_gu, b_gu, w_down, b_down):
    rows = meta["rows"]
    nblk = rows // FFN_TILE
    row_blk = pl.BlockSpec((FFN_TILE, D_MODEL), lambda i, be, nu, *_: (jnp.minimum(i, nu[0] - 1), 0))
    grid_spec = pltpu.PrefetchScalarGridSpec(
        num_scalar_prefetch=4,
        grid=(nblk,),
        in_specs=[row_blk,
                  pl.BlockSpec((None, 1, 2 * D_FF), lambda i, be, *_: (be[i], 0, 0)),
                  pl.BlockSpec((None, 1, D_MODEL), lambda i, be, *_: (be[i], 0, 0)),
                  pl.BlockSpec(memory_space=pl.ANY), pl.BlockSpec(memory_space=pl.ANY)],
        out_specs=row_blk,
        scratch_shapes=[pltpu.VMEM((2, D_MODEL, 2 * D_FF), F32), pltpu.VMEM((2, D_FF, D_MODEL), F32),
                        pltpu.VMEM((D_MODEL, 2 * D_FF), BF16), pltpu.VMEM((D_FF, D_MODEL), BF16),
                        pltpu.SemaphoreType.DMA((2,))],
    )
    return pl.pallas_call(
        _ffn_body,
        grid_spec=grid_spec,
        out_shape=jax.ShapeDtypeStruct((rows, D_MODEL), BF16),
        compiler_params=_params(("arbitrary",)),
        name="moe_ffn",
    )(meta["blk_exp"], meta["n_used"], meta["next_exp"], meta["blk_slot"], xs,
      b_gu.reshape(N_EXP, 1, 2 * D_FF), b_down.reshape(N_EXP, 1, D_MODEL), w_gu, w_down)


def _combine_body(ng_ref, ct_ref, route_ref, soff_ref, x2_ref, gtp_ref, gts_ref, gf_ref, ys_ref,
                  yp_ref, ysm_ref, loc_scr, sem, *, nt, nt_prompt, final):
    t = pl.program_id(0)
    buf = t % 2

    def copies_of(tt):
        def make(g, row):
            return pltpu.make_async_copy(
                ys_ref.at[pl.ds(pl.multiple_of(row, ROW_ALIGN), ROW_ALIGN)],
                loc_scr.at[tt % 2, pl.ds(pl.multiple_of(g * ROW_ALIGN, ROW_ALIGN), ROW_ALIGN)], sem.at[tt % 2])
        return ng_ref[tt], ct_ref, tt * GROUPS, make

    @pl.when(t == 0)
    def _():
        loc_scr[...] = jnp.zeros(loc_scr.shape, BF16)
        _for_groups(*copies_of(0), lambda cp: cp.start())

    @pl.when(t + 1 < nt)
    def _():
        _for_groups(*copies_of(t + 1), lambda cp: cp.start())

    _for_groups(*copies_of(t), lambda cp: cp.wait())

    route = route_ref[...]
    soff_row = soff_ref[...]
    lane = lax.broadcasted_iota(jnp.int32, (TOK_TILE, LANES), 1).astype(F32)
    c_i = lax.broadcasted_iota(jnp.int32, (TOK_TILE, M_LOC), 1).astype(F32)
    wmat = jnp.zeros((TOK_TILE, M_LOC), F32)
    for k in range(TOP_K):
        e_k = route[:, k:k + 1]
        slot = jnp.sum(jnp.where(lane == e_k, soff_row, 0.0), axis=1, keepdims=True) \
            + route[:, 2 * TOP_K + k:2 * TOP_K + k + 1]
        wmat = jnp.where(c_i == slot, route[:, TOP_K + k:TOP_K + k + 1], wmat)
    moe = _dot(wmat.astype(BF16), loc_scr[buf])

    gate = jnp.where(t >= nt_prompt, gts_ref[...], gtp_ref[...])
    xo = x2_ref[...] + gate * moe
    if final:
        ms = jnp.mean(xo * xo, axis=1, keepdims=True)
        xo = (xo * lax.rsqrt(ms + EPS)) * gf_ref[...]

    @pl.when(t < nt_prompt)
    def _():
        yp_ref[...] = xo

    @pl.when(t >= nt_prompt)
    def _():
        ysm_ref[...] = xo


def _combine(meta, route, soff_row, x2, gt2_p, gt2_s, g_final, ys, n_prompt, n_sample, final):
    nt = x2.shape[0] // TOK_TILE
    nt_prompt = n_prompt // TOK_TILE
    tiles_per_batch = nt_prompt // gt2_p.shape[0]
    grid_spec = pltpu.PrefetchScalarGridSpec(
        num_scalar_prefetch=2,
        grid=(nt,),
        in_specs=[pl.BlockSpec((TOK_TILE, LANES), lambda t, *_: (t, 0)),
                  pl.BlockSpec((None, 1, LANES), lambda t, *_: (t, 0, 0)),
                  pl.BlockSpec((TOK_TILE, D_MODEL), lambda t, *_: (t, 0)),
                  pl.BlockSpec((None, 1, D_MODEL),
                               lambda t, *_: (jnp.minimum(t, nt_prompt - 1) // tiles_per_batch, 0, 0)),
                  pl.BlockSpec((TOK_TILE, D_MODEL), lambda t, *_: (0, 0)),
                  pl.BlockSpec((1, D_MODEL), lambda t, *_: (0, 0)),
                  pl.BlockSpec(memory_space=pl.ANY)],
        out_specs=[pl.BlockSpec((TOK_TILE, D_MODEL), lambda t, *_: (jnp.minimum(t, nt_prompt - 1), 0)),
                   pl.BlockSpec((TOK_TILE, D_MODEL), lambda t, *_: (0, 0))],
        scratch_shapes=[pltpu.VMEM((2, M_LOC, D_MODEL), BF16), pltpu.SemaphoreType.DMA((2,))],
    )
    return pl.pallas_call(
        functools.partial(_combine_body, nt=nt, nt_prompt=nt_prompt, final=final),
        grid_spec=grid_spec,
        out_shape=[jax.ShapeDtypeStruct((n_prompt, D_MODEL), F32), jax.ShapeDtypeStruct((n_sample, D_MODEL), F32)],
        compiler_params=_params(("arbitrary",)),
        name="moe_combine",
    )(meta["n_groups"], meta["combine_rows"], route, soff_row, x2, gt2_p, gt2_s, g_final, ys)


def _moe_offsets(cnt):
    nt = cnt.shape[0]
    ra = ROW_ALIGN
    prefix = jnp.cumsum(cnt, axis=0) - cnt
    total = jnp.sum(cnt, axis=0)
    pending = prefix % ra
    used = pending + cnt
    seg = (used + ra - 1) // ra * ra
    lo = jnp.cumsum(seg, axis=1) - seg
    n_groups = jnp.sum(seg, axis=1) // ra
    gpad = (total + FFN_TILE - 1) // FFN_TILE * FFN_TILE
    gstart = jnp.cumsum(gpad) - gpad
    base = gstart[None, :] + prefix // ra * ra
    last = (jnp.arange(nt) == nt - 1)[:, None]
    n_write = jnp.where(last, seg // ra, used // ra)
    carry_start = lo + used // ra * ra
    carry_rows = jnp.where(last, 0, used % ra)
    g = jnp.arange(GROUPS)
    slot_end = (lo + seg) // ra
    owner = jnp.minimum(jnp.sum(g[None, :, None] >= slot_end[:, None, :], axis=2), N_EXP - 1)
    pick = lambda a: jnp.sum(jnp.where(owner[:, :, None] == jnp.arange(N_EXP), a[:, None, :], 0), axis=2)
    k = g[None, :] - pick(lo) // ra
    row = pick(base) + k * ra
    valid = g[None, :] < n_groups[:, None]
    rows = (nt * TOK_TILE * TOP_K + N_EXP * (FFN_TILE - 1) + FFN_TILE - 1) // FFN_TILE * FFN_TILE
    spare = rows + (jnp.arange(nt) % 2)[:, None] * M_LOC + g[None, :] * ra
    combine_rows = jnp.where(valid, row, 0)
    dispatch_rows = jnp.where(valid & (k < pick(n_write)), row, spare)

    nblk_e = gpad // FFN_TILE
    blk_end = jnp.cumsum(nblk_e)
    n_used = jnp.maximum(blk_end[-1], 1)
    blk = jnp.minimum(jnp.arange(rows // FFN_TILE, dtype=jnp.int32), n_used - 1)
    blk_exp = jnp.minimum(jnp.sum(blk[:, None] >= blk_end[None, :], axis=1), N_EXP - 1)
    experts = jnp.arange(N_EXP)
    following = lax.cummin(jnp.where(nblk_e > 0, experts, N_EXP), reverse=True)
    next_of = jnp.concatenate([following[1:], jnp.full((1,), N_EXP, following.dtype)])
    next_of = jnp.where(next_of >= N_EXP, -1, next_of)
    parity = (jnp.cumsum(nblk_e > 0) - 1) % 2
    pick_e = lambda a: jnp.sum(jnp.where(blk_exp[:, None] == experts[None, :], a[None, :], 0), axis=1)
    total16 = (total + ra - 1) // ra * ra
    tail = jnp.concatenate([gstart + total16, (gpad - total16) // ra])
    i32 = lambda a: a.astype(jnp.int32).reshape(-1)
    return dict(n_groups=i32(n_groups), dispatch_rows=i32(dispatch_rows), combine_rows=i32(combine_rows),
                slot_start=i32(lo), carry_start=i32(carry_start), carry_rows=i32(carry_rows), tail=i32(tail),
                blk_exp=i32(blk_exp), n_used=i32(n_used), next_exp=i32(pick_e(next_of)), blk_slot=i32(pick_e(parity)),
                slot_off=(lo + pending).astype(F32), rows=rows)


def _pack_w_in(w_in):
    pad = jnp.zeros((D_MODEL, LANES - N_GATE), w_in.dtype)
    g0 = C_GA
    return jnp.concatenate([w_in[:, :g0], w_in[:, g0 + N_GATE:], w_in[:, g0:g0 + N_GATE], pad],
                           axis=1).astype(BF16)


def _pad_lanes(a, value=0.0):
    return jnp.pad(a, [(0, 0)] * (a.ndim - 1) + [(0, LANES - a.shape[-1])], constant_values=value)


def kernel(x_prompt, x_sample, c_prompt, c_sample, cache_k, cache_v, state_conv, state_C, state_n, state_m, page_table, w_ada, b_ada, g_norm1, g_norm2, w_in, b_gates, lambda_q1, lambda_k1, lambda_q2, lambda_k2, g_subln, w_conv, b_conv, g_mnorm, w_up_a, w_up_b, w_out, w_router, b_router, w_gu, b_gu, w_down, b_down, g_final):
    B, S, D = x_prompt.shape
    Bd, Td, _ = x_sample.shape
    depth = w_in.shape[0]
    n_pool = cache_k.shape[1]
    past_len = page_table.shape[1] * PAGE_SIZE
    n_p, n_s = B * S, Bd * Td
    n_all = n_p + n_s
    assert D == D_MODEL and n_s == TOK_TILE and S % MLSTM_CHUNK == 0 and n_p % TOK_TILE == 0
    assert page_table.shape[1] % PAGES_PER_STEP == 0

    cos_p, sin_p = _rope_tables(np.arange(S))
    cos_s, sin_s = _rope_tables(np.tile(past_len + np.arange(Td), Bd))
    hp = x_prompt.reshape(n_p, D)
    hs = x_sample.reshape(n_s, D)
    c_all = jnp.concatenate([c_prompt, c_sample], axis=0)
    outs = [[] for _ in range(12)]

    for l in range(depth):
        lam_init = 0.8 - 0.6 * math.exp(-0.3 * l)
        mod = _ada(c_all, w_ada[l], b_ada[l])
        mods = [mod[:, j * D:(j + 1) * D] for j in range(6)]
        mp = [m[:B].reshape(B, 1, D) for m in mods]
        ms_ = [jnp.repeat(m[B:], Td, axis=0).reshape(1, n_s, D) for m in mods]
        w_packed = _pack_w_in(w_in[l])
        g1 = g_norm1[l].reshape(1, D)
        lam_vec = jnp.stack([lambda_q1[l], lambda_k1[l], lambda_q2[l], lambda_k2[l]])
        gsub = g_subln[l].reshape(1, DV_A)
        bg = _pad_lanes(b_gates[l].reshape(1, N_GATE))
        cw, cb = w_conv[l], b_conv[l].reshape(1, 2 * W_M)

        (q_p, _, kb_p, v_p, _, qk_p, vm_p, om_p, ga_p, gb_p, gt_p, vt_p, kt_p) = _inproj(
            hp, mp[1], mp[0], g1, cos_p, sin_p, w_packed, 1)
        (q_s, k_s, kb_s, v_s, vb_s, qk_s, vm_s, om_s, ga_s, gb_s, gt_s, _, _) = _inproj(
            hs, ms_[1], ms_[0], g1, cos_s, sin_s, w_packed, n_s)
        k_p = jnp.transpose(kt_p.reshape(B, H_A, 2, DK_A, S), (0, 4, 1, 2, 3))

        oa_p = _attn_prompt(q_p, kb_p, vt_p, lam_vec, gsub.reshape(DV_A, 1), B, S, lam_init)
        kt_pool = jnp.transpose(cache_k[l], (0, 2, 3, 4, 1)).reshape(n_pool, W_QA, PAGE_SIZE)
        v_pool = cache_v[l].reshape(n_pool, PAGE_SIZE * H_A, DV_A)
        oa_s = _attn_sample(q_s, kb_s, vb_s, kt_pool, v_pool, page_table, lam_vec, gsub, lam_init, Td)

        zeros = lambda *shape: jnp.zeros(shape, F32)
        hm_p, cst_p, C_p, nn_p, m_p = _mlstm(qk_p, vm_p, om_p, gt_p, cw, cb, bg, g_mnorm[l],
                                             zeros(B, CONV_W - 1, 2 * W_M), zeros(B, H_M, DH_M, DH_M),
                                             zeros(B, H_M, DH_M), zeros(B, 1, LANES), B, S)
        hm_s, cst_s, C_s, nn_s, m_s = _mlstm(qk_s, vm_s, om_s, gt_s, cw, cb, bg, g_mnorm[l],
                                             state_conv[l], state_C[l], state_n[l],
                                             _pad_lanes(state_m[l]).reshape(Bd, 1, LANES), Bd, Td)

        wa, wb, wo = w_up_a[l].astype(BF16), w_up_b[l].astype(BF16), w_out[l].astype(BF16)
        wr = _pad_lanes(w_router[l]).astype(BF16)
        br = _pad_lanes(b_router[l].reshape(1, N_EXP))
        g2 = g_norm2[l].reshape(1, D)
        part = _merge(oa_p, hm_p, ga_p, gb_p, hp, mp[2], mp[4], mp[3], g2, wa, wb, wo, wr, br, 1, 0, n_all)
        x2, h2, route, cnt = _merge(oa_s, hm_s, ga_s, gb_s, hs, ms_[2], ms_[4], ms_[3], g2, wa, wb, wo, wr, br,
                                    n_s, n_p // TOK_TILE, n_all, prev=part)

        meta = _moe_offsets(jnp.round(cnt[:, 0, :N_EXP]).astype(jnp.int32))
        soff = _pad_lanes(meta["slot_off"])
        xs = _dispatch(meta, h2, route, soff[:, :, None])
        ys = _ffn(meta, xs, w_gu[l], b_gu[l], w_down[l], b_down[l])
        final = l == depth - 1
        hp, hs = _combine(meta, route, soff[:, None, :], x2, mp[5], ms_[5][0], g_final.reshape(1, D), ys,
                          n_p, n_s, final)

        for j, a in enumerate([k_p, v_p.reshape(B, S, H_A, DV_A), cst_p, C_p, nn_p,
                               m_p[:, 0, :H_M],
                               k_s.reshape(Bd, Td, H_A, 2, DK_A), v_s.reshape(Bd, Td, H_A, DV_A), cst_s, C_s, nn_s,
                               m_s[:, 0, :H_M]]):
            outs[j].append(a)

    return (hp.reshape(B, S, D), hs.reshape(Bd, Td, D)) + tuple(jnp.stack(o) for o in outs)
```

```python
import functools
import math

import numpy as np
import jax
import jax.numpy as jnp
from jax import lax
from jax.experimental import pallas as pl
from jax.experimental.pallas import tpu as pltpu

F32 = jnp.float32
BF16 = jnp.bfloat16

D_MODEL = 1024
H_A = 4
DK_A = 64
DV_A = 2 * DK_A
ROPE_THETA = 10000.0
H_M = 4
DH_M = 128
CONV_W = 4
N_EXP = 32
TOP_K = 4
D_FF = D_MODEL
SWIGLU_LIMIT = 7.0
SWIGLU_ALPHA = 1.702
EPS = 1e-6
PAGE_SIZE = 128

W_QA = H_A * 2 * DK_A
W_VA = H_A * DV_A
W_M = H_M * DH_M
N_GATE = 2 * H_M

LANES = 128
ROW_ALIGN = 16
TOK_TILE = 256
FFN_TILE = 256
ATT_TILE = 256
ATT_HEADS = 4
MLSTM_CHUNK = 256
PAGES_PER_STEP = 16
NEG_BIG = -1e30
LOG2_E = math.log2(math.e)
VMEM_LIMIT = 56 * 1024 * 1024

C_QA, C_KA, C_VA, C_QK, C_VM, C_OM, C_GA, C_GB, C_GT = 0, 512, 1024, 1536, 2560, 3072, 3584, 4608, 5632
D_IN_PACKED = C_GT + LANES
M_LOC = ((TOK_TILE * TOP_K + 2 * N_EXP * (ROW_ALIGN - 1)) + 255) // 256 * 256
GROUPS = M_LOC // ROW_ALIGN
ROUTE_SEL = 32
GROUP_UNROLL = 4


def _dot(a, b):
    return jnp.dot(a, b, preferred_element_type=F32)


def _dot_nt(a, b):
    return lax.dot_general(a, b, (((1,), (1,)), ((), ())), preferred_element_type=F32)


def _sigmoid(x):
    return 0.5 * jnp.tanh(0.5 * x) + 0.5


def _params(sem):
    return pltpu.CompilerParams(dimension_semantics=sem, vmem_limit_bytes=VMEM_LIMIT)


def _ada_body(c_ref, w_ref, b_ref, o_ref):
    c = c_ref[...]
    s = c * _sigmoid(c)
    s_hi = s.astype(BF16)
    s_lo = (s - s_hi.astype(F32)).astype(BF16)
    w = w_ref[...]
    w_hi = w.astype(BF16)
    w_lo = (w - w_hi.astype(F32)).astype(BF16)
    o_ref[...] = _dot(s_hi, w_hi) + _dot(s_lo, w_hi) + _dot(s_hi, w_lo) + b_ref[...]


def _ada(c_all, w_ada, b_ada):
    rows = c_all.shape[0]
    n_out = w_ada.shape[1]
    blk = 1024
    return pl.pallas_call(
        _ada_body,
        grid=(n_out // blk,),
        in_specs=[pl.BlockSpec((rows, D_MODEL), lambda j: (0, 0)),
                  pl.BlockSpec((D_MODEL, blk), lambda j: (0, j)),
                  pl.BlockSpec((1, blk), lambda j: (0, j))],
        out_specs=pl.BlockSpec((rows, blk), lambda j: (0, j)),
        out_shape=jax.ShapeDtypeStruct((rows, n_out), F32),
        compiler_params=_params(("arbitrary",)),
        name="ada",
    )(c_all, w_ada, b_ada.reshape(1, n_out))


def _rope(z, cos, sin):
    lane = lax.broadcasted_iota(jnp.int32, (z.shape[0], LANES), 1)
    first_half = (lane % DK_A) < (DK_A // 2)
    out = []
    for h in range(H_A):
        xh = z[:, h * LANES:(h + 1) * LANES]
        partner = jnp.where(first_half, pltpu.roll(xh, LANES - DK_A // 2, 1), pltpu.roll(xh, DK_A // 2, 1))
        out.append(xh * cos + partner * sin)
    return jnp.concatenate(out, axis=1)


def _inproj_body(x_ref, sc_ref, sh_ref, g_ref, cos_ref, sin_ref, w_ref,
                 q_ref, k_ref, kb_ref, v_ref, vb_ref, qk_ref, vm_ref, om_ref, ga_ref, gb_ref, gt_ref, vt_ref, kt_ref):
    x = x_ref[...]
    ms = jnp.mean(x * x, axis=1, keepdims=True)
    h = (x * lax.rsqrt(ms + EPS)) * g_ref[...] * (1.0 + sc_ref[...]) + sh_ref[...]
    hb = h.astype(BF16)
    cos = cos_ref[...]
    sin = sin_ref[...]

    def seg(lo, n):
        return _dot(hb, w_ref[:, lo:lo + n])

    q = _rope(seg(C_QA, W_QA), cos, sin) * (DK_A ** -0.5 * LOG2_E)
    q_ref[...] = q.astype(BF16)
    k = _rope(seg(C_KA, W_QA), cos, sin)
    k_ref[...] = k
    kb_ref[...] = k.astype(BF16)
    kt_ref[...] = k.T
    v = seg(C_VA, W_VA)
    for h in range(H_A):
        v_ref[pl.ds(h, v.shape[0], stride=H_A), :] = v[:, h * DV_A:(h + 1) * DV_A]
    vb_ref[...] = v.astype(BF16)
    vt_ref[...] = v.T.astype(BF16)
    qk_ref[...] = seg(C_QK, 2 * W_M)
    vm_ref[...] = seg(C_VM, W_M).astype(BF16)
    om_ref[...] = seg(C_OM, W_M)
    ga_ref[...] = seg(C_GA, D_MODEL)
    gb_ref[...] = seg(C_GB, D_MODEL)
    gt_ref[...] = seg(C_GT, LANES)


def _inproj(x, sc, sh, g1, cos, sin, w_packed, rows_per_mod):
    n = x.shape[0]
    nt = n // TOK_TILE
    tiles_per_group = nt // sc.shape[0]
    tab_tiles = cos.shape[0] // TOK_TILE
    tok = lambda w: pl.BlockSpec((TOK_TILE, w), lambda i: (i, 0))
    mod = pl.BlockSpec((None, rows_per_mod, D_MODEL), lambda i: (i // tiles_per_group, 0, 0))
    tab = pl.BlockSpec((TOK_TILE, LANES), lambda i: (i % tab_tiles, 0))
    widths = [(W_QA, BF16), (W_QA, F32), (W_QA, BF16), (W_VA, F32), (W_VA, BF16), (2 * W_M, F32),
              (W_M, BF16), (W_M, F32), (D_MODEL, F32), (D_MODEL, F32), (LANES, F32)]
    vt_spec = pl.BlockSpec((None, W_VA, TOK_TILE), lambda i: (i, 0, 0))
    kt_spec = pl.BlockSpec((None, W_QA, TOK_TILE), lambda i: (i // tab_tiles, 0, i % tab_tiles))
    return pl.pallas_call(
        _inproj_body,
        grid=(nt,),
        in_specs=[tok(D_MODEL), mod, mod, pl.BlockSpec((1, D_MODEL), lambda i: (0, 0)), tab, tab,
                  pl.BlockSpec((D_MODEL, D_IN_PACKED), lambda i: (0, 0), pipeline_mode=pl.Buffered(1))],
        out_specs=[pl.BlockSpec((TOK_TILE * H_A, DV_A), lambda i: (i, 0)) if j == 3 else tok(w)
                   for j, (w, _) in enumerate(widths)] + [vt_spec, kt_spec],
        out_shape=[jax.ShapeDtypeStruct((n * H_A, DV_A) if j == 3 else (n, w), dt)
                   for j, (w, dt) in enumerate(widths)]
                  + [jax.ShapeDtypeStruct((nt, W_VA, TOK_TILE), BF16),
                     jax.ShapeDtypeStruct((nt // tab_tiles, W_QA, tab_tiles * TOK_TILE), F32)],
        compiler_params=_params(("arbitrary",)),
        name="inproj",
    )(x, sc, sh, g1, cos, sin, w_packed)


def _rope_tables(pos):
    half = DK_A // 2
    inv = ROPE_THETA ** (-np.arange(half, dtype=np.float64) * 2.0 / DK_A)
    ang = np.asarray(pos, np.float64)[:, None] * inv[None, :]
    cos = np.cos(ang)
    sin = np.sin(ang)
    cos64 = np.concatenate([cos, cos], axis=1)
    sin64 = np.concatenate([-sin, sin], axis=1)
    return (jnp.asarray(np.tile(cos64, (1, LANES // DK_A)), F32),
            jnp.asarray(np.tile(sin64, (1, LANES // DK_A)), F32))


def _lambda_value(lam_ref, lam_init):
    lv = lam_ref[...]
    l1 = jnp.sum(lv[0:1, :] * lv[1:2, :], axis=1, keepdims=True)
    l2 = jnp.sum(lv[2:3, :] * lv[3:4, :], axis=1, keepdims=True)
    return jnp.exp(l1) - jnp.exp(l2) + lam_init


def _subln(o, g, lam_init):
    ms = jnp.mean(o * o, axis=1, keepdims=True)
    return (o * lax.rsqrt(ms + EPS)) * g * (1.0 - lam_init)


def _attn_prompt_body(q_ref, k_ref, vt_ref, lam_ref, g_ref, o_ref, *scratch, lam_init):
    m_scr, acc_scr = scratch[:ATT_HEADS], scratch[ATT_HEADS:]
    i = pl.program_id(2)
    tq = ATT_TILE
    lane = lax.broadcasted_iota(jnp.int32, (tq, LANES), 1)
    qs = []
    for hh in range(ATT_HEADS):
        q = q_ref[:, hh * LANES:(hh + 1) * LANES]
        zero = jnp.zeros_like(q)
        qs.append(jnp.concatenate([jnp.where(lane < DK_A, q, zero), jnp.where(lane >= DK_A, q, zero)], axis=0))
    for hh in range(ATT_HEADS):
        m_scr[hh][...] = jnp.full(m_scr[hh].shape, NEG_BIG, F32)
        acc_scr[hh][...] = jnp.zeros(acc_scr[hh].shape, F32)
    ones = jnp.ones((ROW_ALIGN, tq), BF16)

    def scores(hh, j):
        start = pl.multiple_of(j * tq, tq)
        return _dot_nt(k_ref[pl.ds(start, tq), hh * LANES:(hh + 1) * LANES], qs[hh])

    def update_all(j, mask):
        sts = [scores(hh, j) for hh in range(ATT_HEADS)]
        if mask is not None:
            sts = [jnp.where(mask, st, NEG_BIG) for st in sts]
        pts, alphas = [], []
        for hh in range(ATT_HEADS):
            m_old = m_scr[hh][...]
            m_new = jnp.maximum(m_old, jnp.max(sts[hh], axis=0, keepdims=True))
            alphas.append(jnp.exp2(m_old - m_new))
            pts.append(jnp.exp2(sts[hh] - m_new).astype(BF16))
            m_scr[hh][...] = m_new
        for hh in range(ATT_HEADS):
            vt = jnp.concatenate([vt_ref[j, hh * LANES:(hh + 1) * LANES, :], ones], axis=0)
            acc_scr[hh][...] = alphas[hh] * acc_scr[hh][...] + _dot(vt, pts[hh])

    def off_diag(j, carry):
        update_all(j, None)
        return carry

    lax.fori_loop(0, i, off_diag, 0)
    key = lax.broadcasted_iota(jnp.int32, (tq, 2 * tq), 0)
    qry = lax.broadcasted_iota(jnp.int32, (tq, 2 * tq), 1) % tq
    update_all(i, key <= qry)
    lam = _lambda_value(lam_ref, lam_init)
    for hh in range(ATT_HEADS):
        acc = acc_scr[hh][...]
        ot = acc[:DV_A] / acc[DV_A:DV_A + 1]
        at = ot[:, :tq] - lam * ot[:, tq:]
        ms = jnp.mean(at * at, axis=0, keepdims=True)
        at = (at * lax.rsqrt(ms + EPS)) * g_ref[...] * (1.0 - lam_init)
        o_ref[:, hh * LANES:(hh + 1) * LANES] = at.T.astype(BF16)


def _attn_prompt(q, k, vt, lam_vec, g_subln_col, batch, seq, lam_init):
    nq = seq // ATT_TILE
    width = ATT_HEADS * LANES
    kv = pl.BlockSpec((seq, width), lambda b, g, i: (b, g))
    vts = pl.BlockSpec((nq, width, ATT_TILE), lambda b, g, i: (b, g, 0))
    qo = pl.BlockSpec((ATT_TILE, width), lambda b, g, i: (b * nq + i, g))
    return pl.pallas_call(
        functools.partial(_attn_prompt_body, lam_init=lam_init),
        grid=(batch, H_A // ATT_HEADS, nq),
        in_specs=[qo, kv, vts, pl.BlockSpec((4, DK_A), lambda b, g, i: (0, 0)),
                  pl.BlockSpec((DV_A, 1), lambda b, g, i: (0, 0))],
        out_specs=qo,
        out_shape=jax.ShapeDtypeStruct((batch * seq, W_VA), BF16),
        scratch_shapes=[pltpu.VMEM((1, 2 * ATT_TILE), F32)] * ATT_HEADS
                       + [pltpu.VMEM((DV_A + ROW_ALIGN, 2 * ATT_TILE), F32)] * ATT_HEADS,
        compiler_params=_params(("arbitrary", "arbitrary", "arbitrary")),
        name="attn_prompt",
    )(q, k, vt, lam_vec, g_subln_col)


def _attn_sample_body(pt_ref, q_ref, kn_ref, vn_ref, lam_ref, g_ref, kpool_ref, vpool_ref, o_ref,
                      kbuf, vbuf, sem, m_scr, l_scr, acc_scr, *, lam_init, n_chunks, n_seq, t_new):
    b = pl.program_id(0)

    def chunk_copies(bb, c, slot):
        out = []
        for j in range(PAGES_PER_STEP):
            page = pt_ref[bb, c * PAGES_PER_STEP + j]
            out.append(pltpu.make_async_copy(kpool_ref.at[page], kbuf.at[slot, j], sem.at[slot]))
            out.append(pltpu.make_async_copy(vpool_ref.at[page], vbuf.at[slot, j], sem.at[slot]))
        return out

    @pl.when(b == 0)
    def _():
        for cp in chunk_copies(0, 0, 0):
            cp.start()

    q = q_ref[...].astype(F32)
    qt = jnp.concatenate([q] * (2 * H_A), axis=0)
    row = lax.broadcasted_iota(jnp.int32, qt.shape, 0)
    col = lax.broadcasted_iota(jnp.int32, qt.shape, 1)
    qbd = jnp.where(col // DK_A == row // t_new, qt, 0.0).astype(BF16)
    m_scr[...] = jnp.full(m_scr.shape, NEG_BIG, F32)
    l_scr[...] = jnp.zeros(l_scr.shape, F32)
    acc_scr[...] = jnp.zeros(acc_scr.shape, F32)

    rows_h = 2 * t_new

    def update(s, v_of_head):
        m_old = m_scr[...]
        m_new = jnp.maximum(m_old, jnp.max(s, axis=1, keepdims=True))
        alpha = jnp.exp2(m_old - m_new)
        p = jnp.exp2(s - m_new)
        l_scr[...] = alpha * l_scr[...] + jnp.sum(p, axis=1, keepdims=True)
        pb = p.astype(BF16)
        pv = [_dot(pb[h * rows_h:(h + 1) * rows_h, :], v_of_head(h)) for h in range(H_A)]
        acc_scr[...] = alpha * acc_scr[...] + jnp.concatenate(pv, axis=0)
        m_scr[...] = m_new

    def chunk(c, slot):
        @pl.when(c + 1 < n_chunks)
        def _():
            for cp in chunk_copies(b, c + 1, 1 - slot):
                cp.start()

        @pl.when(jnp.logical_and(c + 1 == n_chunks, b + 1 < n_seq))
        def _():
            for cp in chunk_copies(b + 1, 0, 1 - slot):
                cp.start()

        for cp in chunk_copies(b, c, slot):
            cp.wait()
        kt = jnp.concatenate([kbuf[slot, j].astype(BF16) for j in range(PAGES_PER_STEP)], axis=1)

        def cached_v(h):
            return jnp.concatenate([vbuf.at[slot, j][pl.ds(h, PAGE_SIZE, stride=H_A), :].astype(BF16)
                                    for j in range(PAGES_PER_STEP)], axis=0)

        update(_dot(qbd, kt), cached_v)

    def chunk_pair(i, carry):
        chunk(2 * i, 0)
        chunk(2 * i + 1, 1)
        return carry

    lax.fori_loop(0, n_chunks // 2, chunk_pair, 0)

    zpad = jnp.zeros((PAGE_SIZE - t_new, W_QA), F32)
    kn = jnp.concatenate([kn_ref[...].astype(F32), zpad], axis=0).astype(BF16)
    vn = jnp.concatenate([vn_ref[...].astype(F32), zpad], axis=0).astype(BF16)
    s = _dot_nt(qbd, kn)
    row = lax.broadcasted_iota(jnp.int32, s.shape, 0) % t_new
    col = lax.broadcasted_iota(jnp.int32, s.shape, 1)
    update(jnp.where(col <= row, s, NEG_BIG), lambda h: vn[:, h * DV_A:(h + 1) * DV_A])
    o = acc_scr[...] / l_scr[...]
    lam = _lambda_value(lam_ref, lam_init)
    outs = []
    for h in range(H_A):
        r0 = h * rows_h
        outs.append(_subln(o[r0:r0 + t_new] - lam * o[r0 + t_new:r0 + rows_h], g_ref[...], lam_init))
    o_ref[...] = jnp.concatenate(outs, axis=1).astype(BF16)


def _attn_sample(q, k_new, v_new, cache_k, cache_v, page_table, lam_vec, g_subln, lam_init, t_new):
    bd, n_pages = page_table.shape
    n_chunks = n_pages // PAGES_PER_STEP
    assert n_chunks % 2 == 0
    n_rows = 2 * H_A * t_new
    new = pl.BlockSpec((None, t_new, W_QA), lambda b, pt: (b, 0, 0))
    page_buf = pltpu.VMEM((2, PAGES_PER_STEP, W_QA, PAGE_SIZE), F32)
    grid_spec = pltpu.PrefetchScalarGridSpec(
        num_scalar_prefetch=1,
        grid=(bd,),
        in_specs=[new, new, new, pl.BlockSpec((4, DK_A), lambda b, pt: (0, 0)),
                  pl.BlockSpec((1, DV_A), lambda b, pt: (0, 0)),
                  pl.BlockSpec(memory_space=pl.ANY), pl.BlockSpec(memory_space=pl.ANY)],
        out_specs=new,
        scratch_shapes=[page_buf, page_buf, pltpu.SemaphoreType.DMA((2,)), pltpu.VMEM((n_rows, 1), F32),
                        pltpu.VMEM((n_rows, 1), F32), pltpu.VMEM((n_rows, DV_A), F32)],
    )
    return pl.pallas_call(
        functools.partial(_attn_sample_body, lam_init=lam_init, n_chunks=n_chunks, n_seq=bd, t_new=t_new),
        grid_spec=grid_spec,
        out_shape=jax.ShapeDtypeStruct((bd, t_new, W_VA), BF16),
        compiler_params=_params(("arbitrary",)),
        name="attn_sample",
    )(page_table, q.reshape(bd, t_new, W_QA), k_new.reshape(bd, t_new, W_QA), v_new.reshape(bd, t_new, W_VA),
      lam_vec, g_subln, cache_k, cache_v).reshape(bd * t_new, W_VA)


def _mlstm_body(qk_ref, vm_ref, om_ref, gt_ref, cw_ref, cb_ref, bg_ref, gm_ref, cbuf_ref, c0_ref, n0_ref, m0_ref,
                h_ref, cst_ref, cout_ref, nout_ref, mout_ref, ext_scr, c_scr, n_scr, m_scr, *, tb, L, nc):
    c_idx = pl.program_id(1)

    @pl.when(c_idx == 0)
    def _():
        ext_scr[...] = jnp.zeros(ext_scr.shape, F32)
        ext_scr[8 - (CONV_W - 1):8, :] = cbuf_ref[...]
        c_scr[...] = c0_ref[...]
        n_scr[...] = n0_ref[...]
        m_scr[...] = m0_ref[...]

    pad = L - tb
    u = qk_ref[...]
    if pad:
        u = jnp.concatenate([u, jnp.zeros((pad, u.shape[1]), F32)], axis=0)
    full = jnp.concatenate([ext_scr[...], u], axis=0)
    conv = cb_ref[...] + cw_ref[CONV_W - 1:CONV_W, :] * u
    for j in range(CONV_W - 1):
        conv = conv + cw_ref[j:j + 1, :] * pltpu.roll(full, CONV_W - 1 - j, 0)[8:8 + L]
    a = conv * _sigmoid(conv)
    if not pad:
        ext_scr[...] = u[L - 8:L]

    @pl.when(c_idx == nc - 1)
    def _():
        cst_ref[...] = qk_ref[tb - (CONV_W - 1):tb, :]

    g = gt_ref[...] + bg_ref[...]
    li = g
    lf = jnp.minimum(g, 0.0) - jnp.log1p(jnp.exp(-jnp.abs(g)))
    if pad:
        zpad = jnp.zeros((pad, LANES), F32)
        li = jnp.concatenate([li, zpad + NEG_BIG], axis=0)
        lf = jnp.concatenate([lf, zpad], axis=0)
    row = lax.broadcasted_iota(jnp.int32, (L, LANES), 0)
    lane = lax.broadcasted_iota(jnp.int32, (L, LANES), 1)
    bcum = lf
    shift = 1
    while shift < L:
        bcum = bcum + jnp.where(row >= shift, pltpu.roll(bcum, shift, 0), 0.0)
        shift *= 2
    gates = jnp.where(lane < H_M, li, bcum)
    gates_t = gates.T
    tri = lax.broadcasted_iota(jnp.int32, (L, L), 0) >= lax.broadcasted_iota(jnp.int32, (L, L), 1)
    m_all = m_scr[...]
    lane1 = lax.broadcasted_iota(jnp.int32, (1, LANES), 1)
    m_next = m_all
    vall = vm_ref[...]
    if pad:
        vall = jnp.concatenate([vall, jnp.zeros((pad, vall.shape[1]), BF16)], axis=0)

    for h in range(H_M):
        li_col = gates[:, h:h + 1]
        b_col = gates[:, H_M + h:H_M + h + 1]
        li_row = gates_t[h:h + 1, :]
        b_row = gates_t[H_M + h:H_M + h + 1, :]
        m_prev = m_all[:, h:h + 1]
        b_last = b_col[L - 1:L, :]
        log_d = jnp.where(tri, b_col - b_row + li_row, NEG_BIG)
        inter = b_col + m_prev
        mt = jnp.maximum(inter, jnp.max(log_d, axis=1, keepdims=True))
        q = a[:, h * DH_M:(h + 1) * DH_M]
        k = a[:, W_M + h * DH_M:W_M + (h + 1) * DH_M] * (DH_M ** -0.5)
        v = vall[:, h * DH_M:(h + 1) * DH_M]
        qb = q.astype(BF16)
        s = _dot_nt(qb, k.astype(BF16)) * jnp.exp(log_d - mt)
        ei = jnp.exp(inter - mt)
        c_old = c_scr[h]
        n_old = n_scr[h:h + 1, :]
        num = ei * _dot(qb, c_old.astype(BF16)) + _dot(s.astype(BF16), v)
        den = ei * jnp.sum(q * n_old, axis=1, keepdims=True) + jnp.sum(s, axis=1, keepdims=True)
        hh = num / jnp.maximum(jnp.abs(den), jnp.exp(-mt))
        g_col = b_last - b_col + li_col
        bl = b_last + m_prev
        m_new = jnp.maximum(bl, jnp.max(g_col, axis=0, keepdims=True))
        wg = jnp.exp(g_col - m_new)
        decay = jnp.exp(bl - m_new)
        kw = k * wg
        c_scr[h] = decay * c_old + _dot(kw.T.astype(BF16), v)
        n_scr[h:h + 1, :] = decay * n_old + jnp.sum(kw, axis=0, keepdims=True)
        m_next = jnp.where(lane1 == h, m_new, m_next)
        ms = jnp.mean(hh * hh, axis=1, keepdims=True)
        hn = (hh * lax.rsqrt(ms + EPS)) * gm_ref[h:h + 1, :]
        og = _sigmoid(om_ref[:, h * DH_M:(h + 1) * DH_M])
        h_ref[:, h * DH_M:(h + 1) * DH_M] = (hn[:tb] * og).astype(BF16)

    m_scr[...] = m_next

    @pl.when(c_idx == nc - 1)
    def _():
        cout_ref[...] = c_scr[...]
        nout_ref[...] = n_scr[...]
        mout_ref[...] = m_scr[...]


def _mlstm(qk, vm, om, gt, w_conv, b_conv, b_gates_pad, g_mnorm, conv_buf, c0, n0, m0_pad, batch, seq):
    tb = min(seq, MLSTM_CHUNK)
    L = max(tb, LANES)
    nc = seq // tb
    tok = lambda w: pl.BlockSpec((None, tb, w), lambda b, c: (b * nc + c, 0, 0))
    chunks = lambda a: a.reshape(batch * nc, tb, a.shape[-1])
    const = lambda shape: pl.BlockSpec(shape, lambda b, c: (0,) * len(shape))
    per_b = lambda shape: pl.BlockSpec((None,) + shape, lambda b, c: (b,) + (0,) * len(shape))
    h, cst, c_out, n_out, m_out = pl.pallas_call(
        functools.partial(_mlstm_body, tb=tb, L=L, nc=nc),
        grid=(batch, nc),
        in_specs=[tok(2 * W_M), tok(W_M), tok(W_M), tok(LANES), const((CONV_W, 2 * W_M)), const((1, 2 * W_M)),
                  const((1, LANES)), const((H_M, DH_M)), per_b((CONV_W - 1, 2 * W_M)),
                  per_b((H_M, DH_M, DH_M)), per_b((H_M, DH_M)), per_b((1, LANES))],
        out_specs=[tok(W_M), per_b((CONV_W - 1, 2 * W_M)), per_b((H_M, DH_M, DH_M)), per_b((H_M, DH_M)),
                   per_b((1, LANES))],
        out_shape=[jax.ShapeDtypeStruct((batch * nc, tb, W_M), BF16),
                   jax.ShapeDtypeStruct((batch, CONV_W - 1, 2 * W_M), F32),
                   jax.ShapeDtypeStruct((batch, H_M, DH_M, DH_M), F32),
                   jax.ShapeDtypeStruct((batch, H_M, DH_M), F32),
                   jax.ShapeDtypeStruct((batch, 1, LANES), F32)],
        scratch_shapes=[pltpu.VMEM((8, 2 * W_M), F32), pltpu.VMEM((H_M, DH_M, DH_M), F32),
                        pltpu.VMEM((H_M, DH_M), F32), pltpu.VMEM((1, LANES), F32)],
        compiler_params=_params(("arbitrary", "arbitrary")),
        name="mlstm",
    )(chunks(qk), chunks(vm), chunks(om), chunks(gt), w_conv, b_conv, b_gates_pad, g_mnorm, conv_buf, c0, n0,
      m0_pad)
    return h.reshape(batch * seq, W_M), cst, c_out, n_out, m_out


def _merge_body(oa_ref, hm_ref, ga_ref, gb_ref, x_ref, gt1_ref, sc2_ref, sh2_ref, g2_ref, wa_ref, wb_ref, wo_ref,
                wr_ref, br_ref, *rest):
    x2_ref, h2_ref, route_ref, cnt_ref = rest[-4:]
    ya = _dot(oa_ref[...], wa_ref[...])
    yb = _dot(hm_ref[...], wb_ref[...])
    mix = _sigmoid(ga_ref[...]) * ya + _sigmoid(gb_ref[...]) * yb
    y = _dot(mix.astype(BF16), wo_ref[...])
    x2 = x_ref[...] + gt1_ref[...] * y
    x2_ref[...] = x2
    ms = jnp.mean(x2 * x2, axis=1, keepdims=True)
    h2 = (x2 * lax.rsqrt(ms + EPS)) * g2_ref[...] * (1.0 + sc2_ref[...]) + sh2_ref[...]
    h2b = h2.astype(BF16)
    h2_ref[...] = h2b

    tm = h2b.shape[0]
    logits_t = (_dot(h2b, wr_ref[...]) + br_ref[...]).T[:N_EXP, :]
    row = lax.broadcasted_iota(jnp.int32, (N_EXP, tm), 0)
    row_f = row.astype(F32)
    work = logits_t
    vals, hots = [], []
    for _ in range(TOP_K):
        mx = jnp.max(work, axis=0, keepdims=True)
        idx = jnp.min(jnp.where(work == mx, row, N_EXP), axis=0, keepdims=True)
        hot = row == idx
        vals.append(mx)
        hots.append(hot)
        work = jnp.where(hot, 2.0 * NEG_BIG, work)
    es = [jnp.exp(v - vals[0]) for v in vals]
    den = es[0]
    for e in es[1:]:
        den = den + e
    sel_t = jnp.zeros((N_EXP, tm), F32)
    for hot in hots:
        sel_t = jnp.where(hot, 1.0, sel_t)
    r_i = lax.broadcasted_iota(jnp.int32, (tm, tm), 0)
    c_i = lax.broadcasted_iota(jnp.int32, (tm, tm), 1)
    rank_t = _dot(sel_t.astype(BF16), jnp.where(r_i < c_i, 1.0, 0.0).astype(BF16))
    sub = lax.broadcasted_iota(jnp.int32, (ROUTE_SEL, tm), 0)
    head = jnp.zeros((ROUTE_SEL, tm), F32)
    for k in range(TOP_K):
        e_k = jnp.sum(jnp.where(hots[k], row_f, 0.0), axis=0, keepdims=True)
        r_k = jnp.sum(jnp.where(hots[k], rank_t, 0.0), axis=0, keepdims=True)
        head = jnp.where(sub == k, e_k, head)
        head = jnp.where(sub == TOP_K + k, es[k] / den, head)
        head = jnp.where(sub == 2 * TOP_K + k, r_k, head)
    pad = jnp.zeros((LANES - ROUTE_SEL - N_EXP, tm), F32)
    route = jnp.concatenate([head, sel_t, pad], axis=0).T
    route_ref[...] = route
    cnt_ref[...] = jnp.sum(route, axis=0, keepdims=True)


def _merge(oa, hm, ga, gb, x, gt1, sc2, sh2, g2, wa, wb, wo, wr, br, rows_per_mod, tile0, n_all, prev=None):
    n = x.shape[0]
    nt = n // TOK_TILE
    nt_all = n_all // TOK_TILE
    tiles_per_group = nt // gt1.shape[0]
    tok = lambda w: pl.BlockSpec((TOK_TILE, w), lambda i: (i, 0))
    mod = pl.BlockSpec((None, rows_per_mod, D_MODEL), lambda i: (i // tiles_per_group, 0, 0))
    res = lambda shape: pl.BlockSpec(shape, lambda i: (0, 0), pipeline_mode=pl.Buffered(1))
    out_tok = lambda w: pl.BlockSpec((TOK_TILE, w), lambda i: (tile0 + i, 0))
    in_specs = [tok(W_VA), tok(W_M), tok(D_MODEL), tok(D_MODEL), tok(D_MODEL), mod, mod, mod,
                pl.BlockSpec((1, D_MODEL), lambda i: (0, 0)),
                res((W_VA, D_MODEL)), res((W_M, D_MODEL)), res((D_MODEL, D_MODEL)), res((D_MODEL, LANES)),
                pl.BlockSpec((1, LANES), lambda i: (0, 0))]
    args = [oa, hm, ga, gb, x, gt1, sc2, sh2, g2, wa, wb, wo, wr, br]
    aliases = {}
    if prev is not None:
        in_specs += [pl.BlockSpec(memory_space=pl.ANY)] * 4
        aliases = {len(args) + j: j for j in range(4)}
        args += list(prev)
    return pl.pallas_call(
        _merge_body,
        grid=(nt,),
        in_specs=in_specs,
        out_specs=[out_tok(D_MODEL), out_tok(D_MODEL), out_tok(LANES),
                   pl.BlockSpec((None, 1, LANES), lambda i: (tile0 + i, 0, 0))],
        out_shape=[jax.ShapeDtypeStruct((n_all, D_MODEL), F32), jax.ShapeDtypeStruct((n_all, D_MODEL), BF16),
                   jax.ShapeDtypeStruct((n_all, LANES), F32), jax.ShapeDtypeStruct((nt_all, 1, LANES), F32)],
        input_output_aliases=aliases,
        compiler_params=_params(("arbitrary",)),
        name="merge",
    )(*args)


def _segment_copies(src, dst, sem, src_row, dst_row, n_groups, max_groups):
    out = []
    bit = 1
    while bit * 2 <= max_groups:
        bit *= 2
    while bit >= 1:
        off = (n_groups // (2 * bit)) * (2 * bit) * ROW_ALIGN
        rows = bit * ROW_ALIGN
        cp = pltpu.make_async_copy(src.at[pl.ds(pl.multiple_of(src_row + off, ROW_ALIGN), rows)],
                                   dst.at[pl.ds(pl.multiple_of(dst_row + off, ROW_ALIGN), rows)], sem)
        out.append(((n_groups // bit) % 2 == 1, cp))
        bit //= 2
    return out


def _run_copies(copies):
    for pred, cp in copies:
        pl.when(pred)(cp.start)
    for pred, cp in copies:
        pl.when(pred)(cp.wait)


def _slot_rows(route_t, loff_col, k):
    e_row = route_t[k:k + 1, :]
    r_row = route_t[2 * TOP_K + k:2 * TOP_K + k + 1, :]
    sub = lax.broadcasted_iota(jnp.int32, (LANES, route_t.shape[1]), 0).astype(F32)
    return jnp.sum(jnp.where(sub == e_row, loff_col, 0.0), axis=0, keepdims=True) + r_row


def _for_groups(n_groups, table_ref, base, make_copy, action):
    def body(i, carry):
        for j in range(GROUP_UNROLL):
            g = i * GROUP_UNROLL + j
            action(make_copy(g, table_ref[base + g]))
        return carry
    lax.fori_loop(0, (n_groups + GROUP_UNROLL - 1) // GROUP_UNROLL, body, 0)


def _dispatch_body(ng_ref, dt_ref, lo_ref, sv_ref, rm_ref, tail_ref, h2_ref, route_ref, soff_ref, xs_ref,
                   loc_scr, carry_scr, zero_scr, sem, *, nt):
    t = pl.program_id(0)
    buf = t % 2

    def copies_of(tt):
        def make(g, row):
            return pltpu.make_async_copy(
                loc_scr.at[tt % 2, pl.ds(pl.multiple_of(g * ROW_ALIGN, ROW_ALIGN), ROW_ALIGN)],
                xs_ref.at[pl.ds(pl.multiple_of(row, ROW_ALIGN), ROW_ALIGN)], sem.at[tt % 2])
        return ng_ref[tt], dt_ref, tt * GROUPS, make

    @pl.when(t == 0)
    def _():
        carry_scr[...] = jnp.zeros(carry_scr.shape, BF16)
        loc_scr[:, M_LOC:, :] = jnp.zeros((2, ROW_ALIGN, D_MODEL), BF16)

    @pl.when(t >= 2)
    def _():
        _for_groups(*copies_of(t - 2), lambda cp: cp.wait())

    route_t = route_ref[...].T
    soff_col = soff_ref[...]
    r_i = lax.broadcasted_iota(jnp.int32, (M_LOC, TOK_TILE), 0).astype(F32)
    onehot = jnp.zeros((M_LOC, TOK_TILE), F32)
    for k in range(TOP_K):
        onehot = jnp.where(r_i == _slot_rows(route_t, soff_col, k), 1.0, onehot)
    loc_scr[buf, 0:M_LOC, :] = _dot(onehot.astype(BF16), h2_ref[...]).astype(BF16)

    for e in range(N_EXP):
        lo = pl.multiple_of(lo_ref[t * N_EXP + e], ROW_ALIGN)
        sv = pl.multiple_of(sv_ref[t * N_EXP + e], ROW_ALIGN)
        first = loc_scr[buf, pl.ds(lo, ROW_ALIGN), :].astype(F32) + carry_scr[e].astype(F32)
        loc_scr[buf, pl.ds(lo, ROW_ALIGN), :] = first.astype(BF16)
        pending = loc_scr[buf, pl.ds(sv, ROW_ALIGN), :]
        carry_scr[e] = jnp.where(rm_ref[t * N_EXP + e] > 0, pending, jnp.zeros_like(pending))

    _for_groups(*copies_of(t), lambda cp: cp.start())

    @pl.when(t == nt - 1)
    def _():
        if nt >= 2:
            _for_groups(*copies_of(t - 1), lambda cp: cp.wait())
        _for_groups(*copies_of(t), lambda cp: cp.wait())
        zero_scr[...] = jnp.zeros(zero_scr.shape, BF16)
        tails = []
        for e in range(N_EXP):
            tails += _segment_copies(zero_scr, xs_ref, sem.at[0], 0, tail_ref[e], tail_ref[N_EXP + e],
                                     FFN_TILE // ROW_ALIGN - 1)
        _run_copies(tails)


def _dispatch(meta, h2, route, soff_col):
    nt = h2.shape[0] // TOK_TILE
    grid_spec = pltpu.PrefetchScalarGridSpec(
        num_scalar_prefetch=6,
        grid=(nt,),
        in_specs=[pl.BlockSpec((TOK_TILE, D_MODEL), lambda t, *_: (t, 0)),
                  pl.BlockSpec((TOK_TILE, LANES), lambda t, *_: (t, 0)),
                  pl.BlockSpec((None, LANES, 1), lambda t, *_: (t, 0, 0))],
        out_specs=pl.BlockSpec(memory_space=pl.ANY),
        scratch_shapes=[pltpu.VMEM((2, M_LOC + ROW_ALIGN, D_MODEL), BF16),
                        pltpu.VMEM((N_EXP, ROW_ALIGN, D_MODEL), BF16),
                        pltpu.VMEM((FFN_TILE, D_MODEL), BF16), pltpu.SemaphoreType.DMA((2,))],
    )
    return pl.pallas_call(
        functools.partial(_dispatch_body, nt=nt),
        grid_spec=grid_spec,
        out_shape=jax.ShapeDtypeStruct((meta["rows"] + 2 * M_LOC, D_MODEL), BF16),
        compiler_params=_params(("arbitrary",)),
        name="moe_dispatch",
    )(meta["n_groups"], meta["dispatch_rows"], meta["slot_start"], meta["carry_start"], meta["carry_rows"],
      meta["tail"], h2, route, soff_col)


def _ffn_body(be_ref, nu_ref, nx_ref, sl_ref, x_ref, bgu_ref, bd_ref, wgu_hbm, wd_hbm, y_ref,
              wgu_f32, wd_f32, wgu_scr, wd_scr, sem):
    i = pl.program_id(0)
    expert = be_ref[i]
    slot = sl_ref[i]
    prev = be_ref[jnp.maximum(i - 1, 0)]

    def weight_copies(e, s):
        return (pltpu.make_async_copy(wgu_hbm.at[e], wgu_f32.at[s], sem.at[s]),
                pltpu.make_async_copy(wd_hbm.at[e], wd_f32.at[s], sem.at[s]))

    @pl.when(i == 0)
    def _():
        for cp in weight_copies(expert, slot):
            cp.start()

    @pl.when(jnp.logical_and(i < nu_ref[0], jnp.logical_or(i == 0, expert != prev)))
    def _():
        for cp in weight_copies(expert, slot):
            cp.wait()
        wgu_scr[...] = wgu_f32[slot].astype(BF16)
        wd_scr[...] = wd_f32[slot].astype(BF16)

        @pl.when(nx_ref[i] >= 0)
        def _():
            for cp in weight_copies(nx_ref[i], 1 - slot):
                cp.start()

    @pl.when(i < nu_ref[0])
    def _():
        gu = _dot(x_ref[...], wgu_scr[...]) + bgu_ref[...]
        gate = jnp.minimum(gu[:, :D_FF], SWIGLU_LIMIT)
        up = jnp.clip(gu[:, D_FF:], -SWIGLU_LIMIT, SWIGLU_LIMIT)
        act = (up + 1.0) * gate * _sigmoid(SWIGLU_ALPHA * gate)
        y_ref[...] = (_dot(act.astype(BF16), wd_scr[...]) + bd_ref[...]).astype(BF16)


def _ffn(meta, xs, w_gu, b_gu, w_down, b_down):
    rows = meta["rows"]
    nblk = rows // FFN_TILE
    row_blk = pl.BlockSpec((FFN_TILE, D_MODEL), lambda i, be, nu, *_: (jnp.minimum(i, nu[0] - 1), 0))
    grid_spec = pltpu.PrefetchScalarGridSpec(
        num_scalar_prefetch=4,
        grid=(nblk,),
        in_specs=[row_blk,
                  pl.BlockSpec((None, 1, 2 * D_FF), lambda i, be, *_: (be[i], 0, 0)),
                  pl.BlockSpec((None, 1, D_MODEL), lambda i, be, *_: (be[i], 0, 0)),
                  pl.BlockSpec(memory_space=pl.ANY), pl.BlockSpec(memory_space=pl.ANY)],
        out_specs=row_blk,
        scratch_shapes=[pltpu.VMEM((2, D_MODEL, 2 * D_FF), F32), pltpu.VMEM((2, D_FF, D_MODEL), F32),
                        pltpu.VMEM((D_MODEL, 2 * D_FF), BF16), pltpu.VMEM((D_FF, D_MODEL), BF16),
                        pltpu.SemaphoreType.DMA((2,))],
    )
    return pl.pallas_call(
        _ffn_body,
        grid_spec=grid_spec,
        out_shape=jax.ShapeDtypeStruct((rows, D_MODEL), BF16),
        compiler_params=_params(("arbitrary",)),
        name="moe_ffn",
    )(meta["blk_exp"], meta["n_used"], meta["next_exp"], meta["blk_slot"], xs,
      b_gu.reshape(N_EXP, 1, 2 * D_FF), b_down.reshape(N_EXP, 1, D_MODEL), w_gu, w_down)


def _combine_body(ng_ref, ct_ref, route_ref, soff_ref, x2_ref, gtp_ref, gts_ref, gf_ref, ys_ref,
                  yp_ref, ysm_ref, loc_scr, sem, *, nt, nt_prompt, final):
    t = pl.program_id(0)
    buf = t % 2

    def copies_of(tt):
        def make(g, row):
            return pltpu.make_async_copy(
                ys_ref.at[pl.ds(pl.multiple_of(row, ROW_ALIGN), ROW_ALIGN)],
                loc_scr.at[tt % 2, pl.ds(pl.multiple_of(g * ROW_ALIGN, ROW_ALIGN), ROW_ALIGN)], sem.at[tt % 2])
        return ng_ref[tt], ct_ref, tt * GROUPS, make

    @pl.when(t == 0)
    def _():
        loc_scr[...] = jnp.zeros(loc_scr.shape, BF16)
        _for_groups(*copies_of(0), lambda cp: cp.start())

    @pl.when(t + 1 < nt)
    def _():
        _for_groups(*copies_of(t + 1), lambda cp: cp.start())

    _for_groups(*copies_of(t), lambda cp: cp.wait())

    route = route_ref[...]
    soff_row = soff_ref[...]
    lane = lax.broadcasted_iota(jnp.int32, (TOK_TILE, LANES), 1).astype(F32)
    c_i = lax.broadcasted_iota(jnp.int32, (TOK_TILE, M_LOC), 1).astype(F32)
    wmat = jnp.zeros((TOK_TILE, M_LOC), F32)
    for k in range(TOP_K):
        e_k = route[:, k:k + 1]
        slot = jnp.sum(jnp.where(lane == e_k, soff_row, 0.0), axis=1, keepdims=True) \
            + route[:, 2 * TOP_K + k:2 * TOP_K + k + 1]
        wmat = jnp.where(c_i == slot, route[:, TOP_K + k:TOP_K + k + 1], wmat)
    moe = _dot(wmat.astype(BF16), loc_scr[buf])

    gate = jnp.where(t >= nt_prompt, gts_ref[...], gtp_ref[...])
    xo = x2_ref[...] + gate * moe
    if final:
        ms = jnp.mean(xo * xo, axis=1, keepdims=True)
        xo = (xo * lax.rsqrt(ms + EPS)) * gf_ref[...]

    @pl.when(t < nt_prompt)
    def _():
        yp_ref[...] = xo

    @pl.when(t >= nt_prompt)
    def _():
        ysm_ref[...] = xo


def _combine(meta, route, soff_row, x2, gt2_p, gt2_s, g_final, ys, n_prompt, n_sample, final):
    nt = x2.shape[0] // TOK_TILE
    nt_prompt = n_prompt // TOK_TILE
    tiles_per_batch = nt_prompt // gt2_p.shape[0]
    grid_spec = pltpu.PrefetchScalarGridSpec(
        num_scalar_prefetch=2,
        grid=(nt,),
        in_specs=[pl.BlockSpec((TOK_TILE, LANES), lambda t, *_: (t, 0)),
                  pl.BlockSpec((None, 1, LANES), lambda t, *_: (t, 0, 0)),
                  pl.BlockSpec((TOK_TILE, D_MODEL), lambda t, *_: (t, 0)),
                  pl.BlockSpec((None, 1, D_MODEL),
                               lambda t, *_: (jnp.minimum(t, nt_prompt - 1) // tiles_per_batch, 0, 0)),
                  pl.BlockSpec((TOK_TILE, D_MODEL), lambda t, *_: (0, 0)),
                  pl.BlockSpec((1, D_MODEL), lambda t, *_: (0, 0)),
                  pl.BlockSpec(memory_space=pl.ANY)],
        out_specs=[pl.BlockSpec((TOK_TILE, D_MODEL), lambda t, *_: (jnp.minimum(t, nt_prompt - 1), 0)),
                   pl.BlockSpec((TOK_TILE, D_MODEL), lambda t, *_: (0, 0))],
        scratch_shapes=[pltpu.VMEM((2, M_LOC, D_MODEL), BF16), pltpu.SemaphoreType.DMA((2,))],
    )
    return pl.pallas_call(
        functools.partial(_combine_body, nt=nt, nt_prompt=nt_prompt, final=final),
        grid_spec=grid_spec,
        out_shape=[jax.ShapeDtypeStruct((n_prompt, D_MODEL), F32), jax.ShapeDtypeStruct((n_sample, D_MODEL), F32)],
        compiler_params=_params(("arbitrary",)),
        name="moe_combine",
    )(meta["n_groups"], meta["combine_rows"], route, soff_row, x2, gt2_p, gt2_s, g_final, ys)


def _moe_offsets(cnt):
    nt = cnt.shape[0]
    ra = ROW_ALIGN
    prefix = jnp.cumsum(cnt, axis=0) - cnt
    total = jnp.sum(cnt, axis=0)
    pending = prefix % ra
    used = pending + cnt
    seg = (used + ra - 1) // ra * ra
    lo = jnp.cumsum(seg, axis=1) - seg
    n_groups = jnp.sum(seg, axis=1) // ra
    gpad = (total + FFN_TILE - 1) // FFN_TILE * FFN_TILE
    gstart = jnp.cumsum(gpad) - gpad
    base = gstart[None, :] + prefix // ra * ra
    last = (jnp.arange(nt) == nt - 1)[:, None]
    n_write = jnp.where(last, seg // ra, used // ra)
    carry_start = lo + used // ra * ra
    carry_rows = jnp.where(last, 0, used % ra)
    g = jnp.arange(GROUPS)
    slot_end = (lo + seg) // ra
    owner = jnp.minimum(jnp.sum(g[None, :, None] >= slot_end[:, None, :], axis=2), N_EXP - 1)
    pick = lambda a: jnp.sum(jnp.where(owner[:, :, None] == jnp.arange(N_EXP), a[:, None, :], 0), axis=2)
    k = g[None, :] - pick(lo) // ra
    row = pick(base) + k * ra
    valid = g[None, :] < n_groups[:, None]
    rows = (nt * TOK_TILE * TOP_K + N_EXP * (FFN_TILE - 1) + FFN_TILE - 1) // FFN_TILE * FFN_TILE
    spare = rows + (jnp.arange(nt) % 2)[:, None] * M_LOC + g[None, :] * ra
    combine_rows = jnp.where(valid, row, 0)
    dispatch_rows = jnp.where(valid & (k < pick(n_write)), row, spare)

    nblk_e = gpad // FFN_TILE
    blk_end = jnp.cumsum(nblk_e)
    n_used = jnp.maximum(blk_end[-1], 1)
    blk = jnp.minimum(jnp.arange(rows // FFN_TILE, dtype=jnp.int32), n_used - 1)
    blk_exp = jnp.minimum(jnp.sum(blk[:, None] >= blk_end[None, :], axis=1), N_EXP - 1)
    experts = jnp.arange(N_EXP)
    following = lax.cummin(jnp.where(nblk_e > 0, experts, N_EXP), reverse=True)
    next_of = jnp.concatenate([following[1:], jnp.full((1,), N_EXP, following.dtype)])
    next_of = jnp.where(next_of >= N_EXP, -1, next_of)
    parity = (jnp.cumsum(nblk_e > 0) - 1) % 2
    pick_e = lambda a: jnp.sum(jnp.where(blk_exp[:, None] == experts[None, :], a[None, :], 0), axis=1)
    total16 = (total + ra - 1) // ra * ra
    tail = jnp.concatenate([gstart + total16, (gpad - total16) // ra])
    i32 = lambda a: a.astype(jnp.int32).reshape(-1)
    return dict(n_groups=i32(n_groups), dispatch_rows=i32(dispatch_rows), combine_rows=i32(combine_rows),
                slot_start=i32(lo), carry_start=i32(carry_start), carry_rows=i32(carry_rows), tail=i32(tail),
                blk_exp=i32(blk_exp), n_used=i32(n_used), next_exp=i32(pick_e(next_of)), blk_slot=i32(pick_e(parity)),
                slot_off=(lo + pending).astype(F32), rows=rows)


def _pack_w_in(w_in):
    pad = jnp.zeros((D_MODEL, LANES - N_GATE), w_in.dtype)
    g0 = C_GA
    return jnp.concatenate([w_in[:, :g0], w_in[:, g0 + N_GATE:], w_in[:, g0:g0 + N_GATE], pad],
                           axis=1).astype(BF16)


def _pad_lanes(a, value=0.0):
    return jnp.pad(a, [(0, 0)] * (a.ndim - 1) + [(0, LANES - a.shape[-1])], constant_values=value)


def kernel(x_prompt, x_sample, c_prompt, c_sample, cache_k, cache_v, state_conv, state_C, state_n, state_m, page_table, w_ada, b_ada, g_norm1, g_norm2, w_in, b_gates, lambda_q1, lambda_k1, lambda_q2, lambda_k2, g_subln, w_conv, b_conv, g_mnorm, w_up_a, w_up_b, w_out, w_router, b_router, w_gu, b_gu, w_down, b_down, g_final):
    B, S, D = x_prompt.shape
    Bd, Td, _ = x_sample.shape
    depth = w_in.shape[0]
    n_pool = cache_k.shape[1]
    past_len = page_table.shape[1] * PAGE_SIZE
    n_p, n_s = B * S, Bd * Td
    n_all = n_p + n_s
    assert D == D_MODEL and n_s == TOK_TILE and S % MLSTM_CHUNK == 0 and n_p % TOK_TILE == 0
    assert page_table.shape[1] % PAGES_PER_STEP == 0

    cos_p, sin_p = _rope_tables(np.arange(S))
    cos_s, sin_s = _rope_tables(np.tile(past_len + np.arange(Td), Bd))
    hp = x_prompt.reshape(n_p, D)
    hs = x_sample.reshape(n_s, D)
    c_all = jnp.concatenate([c_prompt, c_sample], axis=0)
    outs = [[] for _ in range(12)]

    for l in range(depth):
        lam_init = 0.8 - 0.6 * math.exp(-0.3 * l)
        mod = _ada(c_all, w_ada[l], b_ada[l])
        mods = [mod[:, j * D:(j + 1) * D] for j in range(6)]
        mp = [m[:B].reshape(B, 1, D) for m in mods]
        ms_ = [jnp.repeat(m[B:], Td, axis=0).reshape(1, n_s, D) for m in mods]
        w_packed = _pack_w_in(w_in[l])
        g1 = g_norm1[l].reshape(1, D)
        lam_vec = jnp.stack([lambda_q1[l], lambda_k1[l], lambda_q2[l], lambda_k2[l]])
        gsub = g_subln[l].reshape(1, DV_A)
        bg = _pad_lanes(b_gates[l].reshape(1, N_GATE))
        cw, cb = w_conv[l], b_conv[l].reshape(1, 2 * W_M)

        (q_p, _, kb_p, v_p, _, qk_p, vm_p, om_p, ga_p, gb_p, gt_p, vt_p, kt_p) = _inproj(
            hp, mp[1], mp[0], g1, cos_p, sin_p, w_packed, 1)
        (q_s, k_s, kb_s, v_s, vb_s, qk_s, vm_s, om_s, ga_s, gb_s, gt_s, _, _) = _inproj(
            hs, ms_[1], ms_[0], g1, cos_s, sin_s, w_packed, n_s)
        k_p = jnp.transpose(kt_p.reshape(B, H_A, 2, DK_A, S), (0, 4, 1, 2, 3))

        oa_p = _attn_prompt(q_p, kb_p, vt_p, lam_vec, gsub.reshape(DV_A, 1), B, S, lam_init)
        kt_pool = jnp.transpose(cache_k[l], (0, 2, 3, 4, 1)).reshape(n_pool, W_QA, PAGE_SIZE)
        v_pool = cache_v[l].reshape(n_pool, PAGE_SIZE * H_A, DV_A)
        oa_s = _attn_sample(q_s, kb_s, vb_s, kt_pool, v_pool, page_table, lam_vec, gsub, lam_init, Td)

        zeros = lambda *shape: jnp.zeros(shape, F32)
        hm_p, cst_p, C_p, nn_p, m_p = _mlstm(qk_p, vm_p, om_p, gt_p, cw, cb, bg, g_mnorm[l],
                                             zeros(B, CONV_W - 1, 2 * W_M), zeros(B, H_M, DH_M, DH_M),
                                             zeros(B, H_M, DH_M), zeros(B, 1, LANES), B, S)
        hm_s, cst_s, C_s, nn_s, m_s = _mlstm(qk_s, vm_s, om_s, gt_s, cw, cb, bg, g_mnorm[l],
                                             state_conv[l], state_C[l], state_n[l],
                                             _pad_lanes(state_m[l]).reshape(Bd, 1, LANES), Bd, Td)

        wa, wb, wo = w_up_a[l].astype(BF16), w_up_b[l].astype(BF16), w_out[l].astype(BF16)
        wr = _pad_lanes(w_router[l]).astype(BF16)
        br = _pad_lanes(b_router[l].reshape(1, N_EXP))
        g2 = g_norm2[l].reshape(1, D)
        part = _merge(oa_p, hm_p, ga_p, gb_p, hp, mp[2], mp[4], mp[3], g2, wa, wb, wo, wr, br, 1, 0, n_all)
        x2, h2, route, cnt = _merge(oa_s, hm_s, ga_s, gb_s, hs, ms_[2], ms_[4], ms_[3], g2, wa, wb, wo, wr, br,
                                    n_s, n_p // TOK_TILE, n_all, prev=part)

        meta = _moe_offsets(jnp.round(cnt[:, 0, ROUTE_SEL:ROUTE_SEL + N_EXP]).astype(jnp.int32))
        soff = _pad_lanes(meta["slot_off"])
        xs = _dispatch(meta, h2, route, soff[:, :, None])
        ys = _ffn(meta, xs, w_gu[l], b_gu[l], w_down[l], b_down[l])
        final = l == depth - 1
        hp, hs = _combine(meta, route, soff[:, None, :], x2, mp[5], ms_[5][0], g_final.reshape(1, D), ys,
                          n_p, n_s, final)

        for j, a in enumerate([k_p, v_p.reshape(B, S, H_A, DV_A), cst_p, C_p, nn_p,
                               m_p[:, 0, :H_M],
                               k_s.reshape(Bd, Td, H_A, 2, DK_A), v_s.reshape(Bd, Td, H_A, DV_A), cst_s, C_s, nn_s,
                               m_s[:, 0, :H_M]]):
            outs[j].append(a)

    return (hp.reshape(B, S, D), hs.reshape(Bd, Td, D)) + tuple(jnp.stack(o) for o in outs)
```

```python
import functools
import math

import numpy as np
import jax
import jax.numpy as jnp
from jax import lax
from jax.experimental import pallas as pl
from jax.experimental.pallas import tpu as pltpu

F32 = jnp.float32
BF16 = jnp.bfloat16

D_MODEL = 1024
H_A = 4
DK_A = 64
DV_A = 2 * DK_A
ROPE_THETA = 10000.0
H_M = 4
DH_M = 128
CONV_W = 4
N_EXP = 32
TOP_K = 4
D_FF = D_MODEL
SWIGLU_LIMIT = 7.0
SWIGLU_ALPHA = 1.702
EPS = 1e-6
PAGE_SIZE = 128

W_QA = H_A * 2 * DK_A
W_VA = H_A * DV_A
W_M = H_M * DH_M
N_GATE = 2 * H_M

LANES = 128
ROW_ALIGN = 16
TOK_TILE = 256
FFN_TILE = 256
ATT_TILE = 256
ATT_HEADS = 4
MLSTM_CHUNK = 256
PAGES_PER_STEP = 16
NEG_BIG = -1e30
LOG2_E = math.log2(math.e)
VMEM_LIMIT = 56 * 1024 * 1024

C_QA, C_KA, C_VA, C_QK, C_VM, C_OM, C_GA, C_GB, C_GT = 0, 512, 1024, 1536, 2560, 3072, 3584, 4608, 5632
D_IN_PACKED = C_GT + LANES
M_LOC = ((TOK_TILE * TOP_K + 2 * N_EXP * (ROW_ALIGN - 1)) + 255) // 256 * 256
GROUPS = M_LOC // ROW_ALIGN
M_LOC_SHORT = 1536
ROUTE_SEL = 32
GROUP_UNROLL = 4


def _dot(a, b):
    return jnp.dot(a, b, preferred_element_type=F32)


def _dot_nt(a, b):
    return lax.dot_general(a, b, (((1,), (1,)), ((), ())), preferred_element_type=F32)


def _sigmoid(x):
    return 0.5 * jnp.tanh(0.5 * x) + 0.5


def _start_all(copies, first_priority=0):
    for n, cp in enumerate(copies):
        cp.start(priority=(first_priority + n) % 2)


def _params(sem):
    return pltpu.CompilerParams(dimension_semantics=sem, vmem_limit_bytes=VMEM_LIMIT)


def _ada_body(c_ref, w_ref, b_ref, o_ref):
    c = c_ref[...]
    s = c * _sigmoid(c)
    s_hi = s.astype(BF16)
    s_lo = (s - s_hi.astype(F32)).astype(BF16)
    w = w_ref[...]
    w_hi = w.astype(BF16)
    w_lo = (w - w_hi.astype(F32)).astype(BF16)
    o_ref[...] = _dot(s_hi, w_hi) + _dot(s_lo, w_hi) + _dot(s_hi, w_lo) + b_ref[...]


def _ada(c_all, w_ada, b_ada):
    rows = c_all.shape[0]
    n_out = w_ada.shape[1]
    blk = 1024
    return pl.pallas_call(
        _ada_body,
        grid=(n_out // blk,),
        in_specs=[pl.BlockSpec((rows, D_MODEL), lambda j: (0, 0)),
                  pl.BlockSpec((D_MODEL, blk), lambda j: (0, j)),
                  pl.BlockSpec((1, blk), lambda j: (0, j))],
        out_specs=pl.BlockSpec((rows, blk), lambda j: (0, j)),
        out_shape=jax.ShapeDtypeStruct((rows, n_out), F32),
        compiler_params=_params(("arbitrary",)),
        name="ada",
    )(c_all, w_ada, b_ada.reshape(1, n_out))


def _rope(z, cos, sin):
    lane = lax.broadcasted_iota(jnp.int32, (z.shape[0], LANES), 1)
    first_half = (lane % DK_A) < (DK_A // 2)
    out = []
    for h in range(H_A):
        xh = z[:, h * LANES:(h + 1) * LANES]
        partner = jnp.where(first_half, pltpu.roll(xh, LANES - DK_A // 2, 1), pltpu.roll(xh, DK_A // 2, 1))
        out.append(xh * cos + partner * sin)
    return jnp.concatenate(out, axis=1)


def _inproj_body(x_ref, sc_ref, sh_ref, g_ref, cos_ref, sin_ref, w_ref,
                 q_ref, k_ref, kb_ref, v_ref, vb_ref, qk_ref, vm_ref, om_ref, ga_ref, gb_ref, gt_ref, vt_ref, kt_ref):
    x = x_ref[...]
    ms = jnp.mean(x * x, axis=1, keepdims=True)
    h = (x * lax.rsqrt(ms + EPS)) * g_ref[...] * (1.0 + sc_ref[...]) + sh_ref[...]
    hb = h.astype(BF16)
    cos = cos_ref[...]
    sin = sin_ref[...]

    def seg(lo, n):
        return _dot(hb, w_ref[:, lo:lo + n])

    q = _rope(seg(C_QA, W_QA), cos, sin) * (DK_A ** -0.5 * LOG2_E)
    q_ref[...] = q.astype(BF16)
    k = _rope(seg(C_KA, W_QA), cos, sin)
    k_ref[...] = k
    kb_ref[...] = k.astype(BF16)
    kt_ref[...] = k.T
    v = seg(C_VA, W_VA)
    for h in range(H_A):
        v_ref[pl.ds(h, v.shape[0], stride=H_A), :] = v[:, h * DV_A:(h + 1) * DV_A]
    vb_ref[...] = v.astype(BF16)
    vt_ref[...] = v.T.astype(BF16)
    qk_ref[...] = seg(C_QK, 2 * W_M)
    vm_ref[...] = seg(C_VM, W_M).astype(BF16)
    om_ref[...] = seg(C_OM, W_M)
    ga_ref[...] = seg(C_GA, D_MODEL)
    gb_ref[...] = seg(C_GB, D_MODEL)
    gt_ref[...] = seg(C_GT, LANES)


def _inproj(x, sc, sh, g1, cos, sin, w_packed, rows_per_mod):
    n = x.shape[0]
    nt = n // TOK_TILE
    tiles_per_group = nt // sc.shape[0]
    tab_tiles = cos.shape[0] // TOK_TILE
    tok = lambda w: pl.BlockSpec((TOK_TILE, w), lambda i: (i, 0))
    mod = pl.BlockSpec((None, rows_per_mod, D_MODEL), lambda i: (i // tiles_per_group, 0, 0))
    tab = pl.BlockSpec((TOK_TILE, LANES), lambda i: (i % tab_tiles, 0))
    widths = [(W_QA, BF16), (W_QA, F32), (W_QA, BF16), (W_VA, F32), (W_VA, BF16), (2 * W_M, F32),
              (W_M, BF16), (W_M, F32), (D_MODEL, F32), (D_MODEL, F32), (LANES, F32)]
    vt_spec = pl.BlockSpec((None, W_VA, TOK_TILE), lambda i: (i, 0, 0))
    kt_spec = pl.BlockSpec((None, W_QA, TOK_TILE), lambda i: (i // tab_tiles, 0, i % tab_tiles))
    return pl.pallas_call(
        _inproj_body,
        grid=(nt,),
        in_specs=[tok(D_MODEL), mod, mod, pl.BlockSpec((1, D_MODEL), lambda i: (0, 0)), tab, tab,
                  pl.BlockSpec((D_MODEL, D_IN_PACKED), lambda i: (0, 0), pipeline_mode=pl.Buffered(1))],
        out_specs=[pl.BlockSpec((TOK_TILE * H_A, DV_A), lambda i: (i, 0)) if j == 3 else tok(w)
                   for j, (w, _) in enumerate(widths)] + [vt_spec, kt_spec],
        out_shape=[jax.ShapeDtypeStruct((n * H_A, DV_A) if j == 3 else (n, w), dt)
                   for j, (w, dt) in enumerate(widths)]
                  + [jax.ShapeDtypeStruct((nt, W_VA, TOK_TILE), BF16),
                     jax.ShapeDtypeStruct((nt // tab_tiles, W_QA, tab_tiles * TOK_TILE), F32)],
        compiler_params=_params(("arbitrary",)),
        name="inproj",
    )(x, sc, sh, g1, cos, sin, w_packed)


def _rope_tables(pos):
    half = DK_A // 2
    inv = ROPE_THETA ** (-np.arange(half, dtype=np.float64) * 2.0 / DK_A)
    ang = np.asarray(pos, np.float64)[:, None] * inv[None, :]
    cos = np.cos(ang)
    sin = np.sin(ang)
    cos64 = np.concatenate([cos, cos], axis=1)
    sin64 = np.concatenate([-sin, sin], axis=1)
    return (jnp.asarray(np.tile(cos64, (1, LANES // DK_A)), F32),
            jnp.asarray(np.tile(sin64, (1, LANES // DK_A)), F32))


def _lambda_value(lam_ref, lam_init):
    lv = lam_ref[...]
    l1 = jnp.sum(lv[0:1, :] * lv[1:2, :], axis=1, keepdims=True)
    l2 = jnp.sum(lv[2:3, :] * lv[3:4, :], axis=1, keepdims=True)
    return jnp.exp(l1) - jnp.exp(l2) + lam_init


def _subln(o, g, lam_init):
    ms = jnp.mean(o * o, axis=1, keepdims=True)
    return (o * lax.rsqrt(ms + EPS)) * g * (1.0 - lam_init)


def _attn_prompt_body(q_ref, k_ref, vt_ref, lam_ref, g_ref, o_ref, *scratch, lam_init):
    m_scr, acc_scr = scratch[:ATT_HEADS], scratch[ATT_HEADS:]
    i = pl.program_id(2)
    tq = ATT_TILE
    lane = lax.broadcasted_iota(jnp.int32, (tq, LANES), 1)
    qs = []
    for hh in range(ATT_HEADS):
        q = q_ref[:, hh * LANES:(hh + 1) * LANES]
        zero = jnp.zeros_like(q)
        qs.append(jnp.concatenate([jnp.where(lane < DK_A, q, zero), jnp.where(lane >= DK_A, q, zero)], axis=0))
    for hh in range(ATT_HEADS):
        m_scr[hh][...] = jnp.full(m_scr[hh].shape, NEG_BIG, F32)
        acc_scr[hh][...] = jnp.zeros(acc_scr[hh].shape, F32)
    ones = jnp.ones((ROW_ALIGN, tq), BF16)

    def scores(hh, j):
        start = pl.multiple_of(j * tq, tq)
        return _dot_nt(k_ref[pl.ds(start, tq), hh * LANES:(hh + 1) * LANES], qs[hh])

    def update_all(j, mask):
        sts = [scores(hh, j) for hh in range(ATT_HEADS)]
        if mask is not None:
            sts = [jnp.where(mask, st, NEG_BIG) for st in sts]
        pts, alphas = [], []
        for hh in range(ATT_HEADS):
            m_old = m_scr[hh][...]
            m_new = jnp.maximum(m_old, jnp.max(sts[hh], axis=0, keepdims=True))
            alphas.append(jnp.exp2(m_old - m_new))
            pts.append(jnp.exp2(sts[hh] - m_new).astype(BF16))
            m_scr[hh][...] = m_new
        for hh in range(ATT_HEADS):
            vt = jnp.concatenate([vt_ref[j, hh * LANES:(hh + 1) * LANES, :], ones], axis=0)
            acc_scr[hh][...] = alphas[hh] * acc_scr[hh][...] + _dot(vt, pts[hh])

    def off_diag(j, carry):
        update_all(j, None)
        return carry

    lax.fori_loop(0, i, off_diag, 0)
    key = lax.broadcasted_iota(jnp.int32, (tq, 2 * tq), 0)
    qry = lax.broadcasted_iota(jnp.int32, (tq, 2 * tq), 1) % tq
    update_all(i, key <= qry)
    lam = _lambda_value(lam_ref, lam_init)
    for hh in range(ATT_HEADS):
        acc = acc_scr[hh][...]
        ot = acc[:DV_A] / acc[DV_A:DV_A + 1]
        at = ot[:, :tq] - lam * ot[:, tq:]
        ms = jnp.mean(at * at, axis=0, keepdims=True)
        at = (at * lax.rsqrt(ms + EPS)) * g_ref[...] * (1.0 - lam_init)
        o_ref[:, hh * LANES:(hh + 1) * LANES] = at.T.astype(BF16)


def _attn_prompt(q, k, vt, lam_vec, g_subln_col, batch, seq, lam_init):
    nq = seq // ATT_TILE
    width = ATT_HEADS * LANES
    kv = pl.BlockSpec((seq, width), lambda b, g, i: (b, g))
    vts = pl.BlockSpec((nq, width, ATT_TILE), lambda b, g, i: (b, g, 0))
    qo = pl.BlockSpec((ATT_TILE, width), lambda b, g, i: (b * nq + i, g))
    return pl.pallas_call(
        functools.partial(_attn_prompt_body, lam_init=lam_init),
        grid=(batch, H_A // ATT_HEADS, nq),
        in_specs=[qo, kv, vts, pl.BlockSpec((4, DK_A), lambda b, g, i: (0, 0)),
                  pl.BlockSpec((DV_A, 1), lambda b, g, i: (0, 0))],
        out_specs=qo,
        out_shape=jax.ShapeDtypeStruct((batch * seq, W_VA), BF16),
        scratch_shapes=[pltpu.VMEM((1, 2 * ATT_TILE), F32)] * ATT_HEADS
                       + [pltpu.VMEM((DV_A + ROW_ALIGN, 2 * ATT_TILE), F32)] * ATT_HEADS,
        compiler_params=_params(("arbitrary", "arbitrary", "arbitrary")),
        name="attn_prompt",
    )(q, k, vt, lam_vec, g_subln_col)


def _attn_sample_body(pt_ref, q_ref, kn_ref, vn_ref, lam_ref, g_ref, kpool_ref, vpool_ref, o_ref,
                      kbuf, vbuf, sem, m_scr, l_scr, acc_scr, *, lam_init, n_chunks, n_seq, t_new):
    b = pl.program_id(0)

    def chunk_copies(bb, c, slot):
        out = []
        for j in range(PAGES_PER_STEP):
            page = pt_ref[bb, c * PAGES_PER_STEP + j]
            out.append(pltpu.make_async_copy(kpool_ref.at[page], kbuf.at[slot, j], sem.at[slot]))
            out.append(pltpu.make_async_copy(vpool_ref.at[page], vbuf.at[slot, j], sem.at[slot]))
        return out

    @pl.when(b == 0)
    def _():
        _start_all(chunk_copies(0, 0, 0))

    q = q_ref[...].astype(F32)
    qt = jnp.concatenate([q] * (2 * H_A), axis=0)
    row = lax.broadcasted_iota(jnp.int32, qt.shape, 0)
    col = lax.broadcasted_iota(jnp.int32, qt.shape, 1)
    qbd = jnp.where(col // DK_A == row // t_new, qt, 0.0).astype(BF16)
    m_scr[...] = jnp.full(m_scr.shape, NEG_BIG, F32)
    l_scr[...] = jnp.zeros(l_scr.shape, F32)
    acc_scr[...] = jnp.zeros(acc_scr.shape, F32)

    rows_h = 2 * t_new

    def update(s, v_of_head):
        m_old = m_scr[...]
        m_new = jnp.maximum(m_old, jnp.max(s, axis=1, keepdims=True))
        alpha = jnp.exp2(m_old - m_new)
        p = jnp.exp2(s - m_new)
        l_scr[...] = alpha * l_scr[...] + jnp.sum(p, axis=1, keepdims=True)
        pb = p.astype(BF16)
        pv = [_dot(pb[h * rows_h:(h + 1) * rows_h, :], v_of_head(h)) for h in range(H_A)]
        acc_scr[...] = alpha * acc_scr[...] + jnp.concatenate(pv, axis=0)
        m_scr[...] = m_new

    def chunk(c, slot):
        @pl.when(c + 1 < n_chunks)
        def _():
            _start_all(chunk_copies(b, c + 1, 1 - slot))

        @pl.when(jnp.logical_and(c + 1 == n_chunks, b + 1 < n_seq))
        def _():
            _start_all(chunk_copies(b + 1, 0, 1 - slot))

        for cp in chunk_copies(b, c, slot):
            cp.wait()
        kt = jnp.concatenate([kbuf[slot, j].astype(BF16) for j in range(PAGES_PER_STEP)], axis=1)

        def cached_v(h):
            return jnp.concatenate([vbuf.at[slot, j][pl.ds(h, PAGE_SIZE, stride=H_A), :].astype(BF16)
                                    for j in range(PAGES_PER_STEP)], axis=0)

        update(_dot(qbd, kt), cached_v)

    def chunk_pair(i, carry):
        chunk(2 * i, 0)
        chunk(2 * i + 1, 1)
        return carry

    lax.fori_loop(0, n_chunks // 2, chunk_pair, 0)

    zpad = jnp.zeros((PAGE_SIZE - t_new, W_QA), F32)
    kn = jnp.concatenate([kn_ref[...].astype(F32), zpad], axis=0).astype(BF16)
    vn = jnp.concatenate([vn_ref[...].astype(F32), zpad], axis=0).astype(BF16)
    s = _dot_nt(qbd, kn)
    row = lax.broadcasted_iota(jnp.int32, s.shape, 0) % t_new
    col = lax.broadcasted_iota(jnp.int32, s.shape, 1)
    update(jnp.where(col <= row, s, NEG_BIG), lambda h: vn[:, h * DV_A:(h + 1) * DV_A])
    o = acc_scr[...] / l_scr[...]
    lam = _lambda_value(lam_ref, lam_init)
    outs = []
    for h in range(H_A):
        r0 = h * rows_h
        outs.append(_subln(o[r0:r0 + t_new] - lam * o[r0 + t_new:r0 + rows_h], g_ref[...], lam_init))
    o_ref[...] = jnp.concatenate(outs, axis=1).astype(BF16)


def _attn_sample(q, k_new, v_new, cache_k, cache_v, page_table, lam_vec, g_subln, lam_init, t_new):
    bd, n_pages = page_table.shape
    n_chunks = n_pages // PAGES_PER_STEP
    assert n_chunks % 2 == 0
    n_rows = 2 * H_A * t_new
    new = pl.BlockSpec((None, t_new, W_QA), lambda b, pt: (b, 0, 0))
    page_buf = pltpu.VMEM((2, PAGES_PER_STEP, W_QA, PAGE_SIZE), F32)
    grid_spec = pltpu.PrefetchScalarGridSpec(
        num_scalar_prefetch=1,
        grid=(bd,),
        in_specs=[new, new, new, pl.BlockSpec((4, DK_A), lambda b, pt: (0, 0)),
                  pl.BlockSpec((1, DV_A), lambda b, pt: (0, 0)),
                  pl.BlockSpec(memory_space=pl.ANY), pl.BlockSpec(memory_space=pl.ANY)],
        out_specs=new,
        scratch_shapes=[page_buf, page_buf, pltpu.SemaphoreType.DMA((2,)), pltpu.VMEM((n_rows, 1), F32),
                        pltpu.VMEM((n_rows, 1), F32), pltpu.VMEM((n_rows, DV_A), F32)],
    )
    return pl.pallas_call(
        functools.partial(_attn_sample_body, lam_init=lam_init, n_chunks=n_chunks, n_seq=bd, t_new=t_new),
        grid_spec=grid_spec,
        out_shape=jax.ShapeDtypeStruct((bd, t_new, W_VA), BF16),
        compiler_params=_params(("arbitrary",)),
        name="attn_sample",
    )(page_table, q.reshape(bd, t_new, W_QA), k_new.reshape(bd, t_new, W_QA), v_new.reshape(bd, t_new, W_VA),
      lam_vec, g_subln, cache_k, cache_v).reshape(bd * t_new, W_VA)


def _mlstm_body(qk_ref, vm_ref, om_ref, gt_ref, cw_ref, cb_ref, bg_ref, gm_ref, cbuf_ref, c0_ref, n0_ref, m0_ref,
                h_ref, cst_ref, cout_ref, nout_ref, mout_ref, ext_scr, c_scr, n_scr, m_scr, *, tb, L, nc):
    c_idx = pl.program_id(1)

    @pl.when(c_idx == 0)
    def _():
        ext_scr[...] = jnp.zeros(ext_scr.shape, F32)
        ext_scr[8 - (CONV_W - 1):8, :] = cbuf_ref[...]
        c_scr[...] = c0_ref[...]
        n_scr[...] = n0_ref[...]
        m_scr[...] = m0_ref[...]

    pad = L - tb
    u = qk_ref[...]
    if pad:
        u = jnp.concatenate([u, jnp.zeros((pad, u.shape[1]), F32)], axis=0)
    full = jnp.concatenate([ext_scr[...], u], axis=0)
    conv = cb_ref[...] + cw_ref[CONV_W - 1:CONV_W, :] * u
    for j in range(CONV_W - 1):
        conv = conv + cw_ref[j:j + 1, :] * pltpu.roll(full, CONV_W - 1 - j, 0)[8:8 + L]
    a = conv * _sigmoid(conv)
    if not pad:
        ext_scr[...] = u[L - 8:L]

    @pl.when(c_idx == nc - 1)
    def _():
        cst_ref[...] = qk_ref[tb - (CONV_W - 1):tb, :]

    g = gt_ref[...] + bg_ref[...]
    li = g
    lf = jnp.minimum(g, 0.0) - jnp.log1p(jnp.exp(-jnp.abs(g)))
    if pad:
        zpad = jnp.zeros((pad, LANES), F32)
        li = jnp.concatenate([li, zpad + NEG_BIG], axis=0)
        lf = jnp.concatenate([lf, zpad], axis=0)
    row = lax.broadcasted_iota(jnp.int32, (L, LANES), 0)
    lane = lax.broadcasted_iota(jnp.int32, (L, LANES), 1)
    bcum = lf
    shift = 1
    while shift < L:
        bcum = bcum + jnp.where(row >= shift, pltpu.roll(bcum, shift, 0), 0.0)
        shift *= 2
    gates = jnp.where(lane < H_M, li, bcum)
    gates_t = gates.T
    tri = lax.broadcasted_iota(jnp.int32, (L, L), 0) >= lax.broadcasted_iota(jnp.int32, (L, L), 1)
    m_all = m_scr[...]
    lane1 = lax.broadcasted_iota(jnp.int32, (1, LANES), 1)
    m_next = m_all
    vall = vm_ref[...]
    if pad:
        vall = jnp.concatenate([vall, jnp.zeros((pad, vall.shape[1]), BF16)], axis=0)

    for h in range(H_M):
        li_col = gates[:, h:h + 1]
        b_col = gates[:, H_M + h:H_M + h + 1]
        li_row = gates_t[h:h + 1, :]
        b_row = gates_t[H_M + h:H_M + h + 1, :]
        m_prev = m_all[:, h:h + 1]
        b_last = b_col[L - 1:L, :]
        log_d = jnp.where(tri, b_col - b_row + li_row, NEG_BIG)
        inter = b_col + m_prev
        mt = jnp.maximum(inter, jnp.max(log_d, axis=1, keepdims=True))
        q = a[:, h * DH_M:(h + 1) * DH_M]
        k = a[:, W_M + h * DH_M:W_M + (h + 1) * DH_M] * (DH_M ** -0.5)
        v = vall[:, h * DH_M:(h + 1) * DH_M]
        qb = q.astype(BF16)
        s = _dot_nt(qb, k.astype(BF16)) * jnp.exp(log_d - mt)
        ei = jnp.exp(inter - mt)
        c_old = c_scr[h]
        n_old = n_scr[h:h + 1, :]
        num = ei * _dot(qb, c_old.astype(BF16)) + _dot(s.astype(BF16), v)
        den = ei * jnp.sum(q * n_old, axis=1, keepdims=True) + jnp.sum(s, axis=1, keepdims=True)
        hh = num / jnp.maximum(jnp.abs(den), jnp.exp(-mt))
        g_col = b_last - b_col + li_col
        bl = b_last + m_prev
        m_new = jnp.maximum(bl, jnp.max(g_col, axis=0, keepdims=True))
        wg = jnp.exp(g_col - m_new)
        decay = jnp.exp(bl - m_new)
        kw = k * wg
        c_scr[h] = decay * c_old + _dot(kw.T.astype(BF16), v)
        n_scr[h:h + 1, :] = decay * n_old + jnp.sum(kw, axis=0, keepdims=True)
        m_next = jnp.where(lane1 == h, m_new, m_next)
        ms = jnp.mean(hh * hh, axis=1, keepdims=True)
        hn = (hh * lax.rsqrt(ms + EPS)) * gm_ref[h:h + 1, :]
        og = _sigmoid(om_ref[:, h * DH_M:(h + 1) * DH_M])
        h_ref[:, h * DH_M:(h + 1) * DH_M] = (hn[:tb] * og).astype(BF16)

    m_scr[...] = m_next

    @pl.when(c_idx == nc - 1)
    def _():
        cout_ref[...] = c_scr[...]
        nout_ref[...] = n_scr[...]
        mout_ref[...] = m_scr[...]


def _mlstm(qk, vm, om, gt, w_conv, b_conv, b_gates_pad, g_mnorm, conv_buf, c0, n0, m0_pad, batch, seq):
    tb = min(seq, MLSTM_CHUNK)
    L = max(tb, LANES)
    nc = seq // tb
    tok = lambda w: pl.BlockSpec((None, tb, w), lambda b, c: (b * nc + c, 0, 0))
    chunks = lambda a: a.reshape(batch * nc, tb, a.shape[-1])
    const = lambda shape: pl.BlockSpec(shape, lambda b, c: (0,) * len(shape))
    per_b = lambda shape: pl.BlockSpec((None,) + shape, lambda b, c: (b,) + (0,) * len(shape))
    h, cst, c_out, n_out, m_out = pl.pallas_call(
        functools.partial(_mlstm_body, tb=tb, L=L, nc=nc),
        grid=(batch, nc),
        in_specs=[tok(2 * W_M), tok(W_M), tok(W_M), tok(LANES), const((CONV_W, 2 * W_M)), const((1, 2 * W_M)),
                  const((1, LANES)), const((H_M, DH_M)), per_b((CONV_W - 1, 2 * W_M)),
                  per_b((H_M, DH_M, DH_M)), per_b((H_M, DH_M)), per_b((1, LANES))],
        out_specs=[tok(W_M), per_b((CONV_W - 1, 2 * W_M)), per_b((H_M, DH_M, DH_M)), per_b((H_M, DH_M)),
                   per_b((1, LANES))],
        out_shape=[jax.ShapeDtypeStruct((batch * nc, tb, W_M), BF16),
                   jax.ShapeDtypeStruct((batch, CONV_W - 1, 2 * W_M), F32),
                   jax.ShapeDtypeStruct((batch, H_M, DH_M, DH_M), F32),
                   jax.ShapeDtypeStruct((batch, H_M, DH_M), F32),
                   jax.ShapeDtypeStruct((batch, 1, LANES), F32)],
        scratch_shapes=[pltpu.VMEM((8, 2 * W_M), F32), pltpu.VMEM((H_M, DH_M, DH_M), F32),
                        pltpu.VMEM((H_M, DH_M), F32), pltpu.VMEM((1, LANES), F32)],
        compiler_params=_params(("arbitrary", "arbitrary")),
        name="mlstm",
    )(chunks(qk), chunks(vm), chunks(om), chunks(gt), w_conv, b_conv, b_gates_pad, g_mnorm, conv_buf, c0, n0,
      m0_pad)
    return h.reshape(batch * seq, W_M), cst, c_out, n_out, m_out


def _merge_body(oa_ref, hm_ref, ga_ref, gb_ref, x_ref, gt1_ref, sc2_ref, sh2_ref, g2_ref, wa_ref, wb_ref, wo_ref,
                wr_ref, br_ref, *rest):
    x2_ref, h2_ref, route_ref, cnt_ref = rest[-4:]
    ya = _dot(oa_ref[...], wa_ref[...])
    yb = _dot(hm_ref[...], wb_ref[...])
    mix = _sigmoid(ga_ref[...]) * ya + _sigmoid(gb_ref[...]) * yb
    y = _dot(mix.astype(BF16), wo_ref[...])
    x2 = x_ref[...] + gt1_ref[...] * y
    x2_ref[...] = x2
    ms = jnp.mean(x2 * x2, axis=1, keepdims=True)
    h2 = (x2 * lax.rsqrt(ms + EPS)) * g2_ref[...] * (1.0 + sc2_ref[...]) + sh2_ref[...]
    h2b = h2.astype(BF16)
    h2_ref[...] = h2b

    tm = h2b.shape[0]
    logits_t = (_dot(h2b, wr_ref[...]) + br_ref[...]).T[:N_EXP, :]
    row = lax.broadcasted_iota(jnp.int32, (N_EXP, tm), 0)
    row_f = row.astype(F32)
    work = logits_t
    vals, hots = [], []
    for _ in range(TOP_K):
        mx = jnp.max(work, axis=0, keepdims=True)
        idx = jnp.min(jnp.where(work == mx, row, N_EXP), axis=0, keepdims=True)
        hot = row == idx
        vals.append(mx)
        hots.append(hot)
        work = jnp.where(hot, 2.0 * NEG_BIG, work)
    es = [jnp.exp(v - vals[0]) for v in vals]
    den = es[0]
    for e in es[1:]:
        den = den + e
    sel_t = jnp.zeros((N_EXP, tm), F32)
    for hot in hots:
        sel_t = jnp.where(hot, 1.0, sel_t)
    r_i = lax.broadcasted_iota(jnp.int32, (tm, tm), 0)
    c_i = lax.broadcasted_iota(jnp.int32, (tm, tm), 1)
    rank_t = _dot(sel_t.astype(BF16), jnp.where(r_i < c_i, 1.0, 0.0).astype(BF16))
    sub = lax.broadcasted_iota(jnp.int32, (ROUTE_SEL, tm), 0)
    head = jnp.zeros((ROUTE_SEL, tm), F32)
    for k in range(TOP_K):
        e_k = jnp.sum(jnp.where(hots[k], row_f, 0.0), axis=0, keepdims=True)
        r_k = jnp.sum(jnp.where(hots[k], rank_t, 0.0), axis=0, keepdims=True)
        head = jnp.where(sub == k, e_k, head)
        head = jnp.where(sub == TOP_K + k, es[k] / den, head)
        head = jnp.where(sub == 2 * TOP_K + k, r_k, head)
    pad = jnp.zeros((LANES - ROUTE_SEL - N_EXP, tm), F32)
    route = jnp.concatenate([head, sel_t, pad], axis=0).T
    route_ref[...] = route
    cnt_ref[...] = jnp.sum(route, axis=0, keepdims=True)


def _merge(oa, hm, ga, gb, x, gt1, sc2, sh2, g2, wa, wb, wo, wr, br, rows_per_mod, tile0, n_all, prev=None):
    n = x.shape[0]
    nt = n // TOK_TILE
    nt_all = n_all // TOK_TILE
    tiles_per_group = nt // gt1.shape[0]
    tok = lambda w: pl.BlockSpec((TOK_TILE, w), lambda i: (i, 0))
    mod = pl.BlockSpec((None, rows_per_mod, D_MODEL), lambda i: (i // tiles_per_group, 0, 0))
    res = lambda shape: pl.BlockSpec(shape, lambda i: (0, 0), pipeline_mode=pl.Buffered(1))
    out_tok = lambda w: pl.BlockSpec((TOK_TILE, w), lambda i: (tile0 + i, 0))
    in_specs = [tok(W_VA), tok(W_M), tok(D_MODEL), tok(D_MODEL), tok(D_MODEL), mod, mod, mod,
                pl.BlockSpec((1, D_MODEL), lambda i: (0, 0)),
                res((W_VA, D_MODEL)), res((W_M, D_MODEL)), res((D_MODEL, D_MODEL)), res((D_MODEL, LANES)),
                pl.BlockSpec((1, LANES), lambda i: (0, 0))]
    args = [oa, hm, ga, gb, x, gt1, sc2, sh2, g2, wa, wb, wo, wr, br]
    aliases = {}
    if prev is not None:
        in_specs += [pl.BlockSpec(memory_space=pl.ANY)] * 4
        aliases = {len(args) + j: j for j in range(4)}
        args += list(prev)
    return pl.pallas_call(
        _merge_body,
        grid=(nt,),
        in_specs=in_specs,
        out_specs=[out_tok(D_MODEL), out_tok(D_MODEL), out_tok(LANES),
                   pl.BlockSpec((None, 1, LANES), lambda i: (tile0 + i, 0, 0))],
        out_shape=[jax.ShapeDtypeStruct((n_all, D_MODEL), F32), jax.ShapeDtypeStruct((n_all, D_MODEL), BF16),
                   jax.ShapeDtypeStruct((n_all, LANES), F32), jax.ShapeDtypeStruct((nt_all, 1, LANES), F32)],
        input_output_aliases=aliases,
        compiler_params=_params(("arbitrary",)),
        name="merge",
    )(*args)


def _segment_copies(src, dst, sem, src_row, dst_row, n_groups, max_groups):
    out = []
    bit = 1
    while bit * 2 <= max_groups:
        bit *= 2
    while bit >= 1:
        off = (n_groups // (2 * bit)) * (2 * bit) * ROW_ALIGN
        rows = bit * ROW_ALIGN
        cp = pltpu.make_async_copy(src.at[pl.ds(pl.multiple_of(src_row + off, ROW_ALIGN), rows)],
                                   dst.at[pl.ds(pl.multiple_of(dst_row + off, ROW_ALIGN), rows)], sem)
        out.append(((n_groups // bit) % 2 == 1, cp))
        bit //= 2
    return out


def _run_copies(copies):
    for pred, cp in copies:
        pl.when(pred)(cp.start)
    for pred, cp in copies:
        pl.when(pred)(cp.wait)


def _slot_rows(route_t, loff_col, k):
    e_row = route_t[k:k + 1, :]
    r_row = route_t[2 * TOP_K + k:2 * TOP_K + k + 1, :]
    sub = lax.broadcasted_iota(jnp.int32, (LANES, route_t.shape[1]), 0).astype(F32)
    return jnp.sum(jnp.where(sub == e_row, loff_col, 0.0), axis=0, keepdims=True) + r_row


def _for_groups(n_groups, table_ref, base, make_copy, start):
    def body(i, carry):
        for j in range(GROUP_UNROLL):
            g = i * GROUP_UNROLL + j
            cp = make_copy(g, table_ref[base + g])
            if start:
                cp.start(priority=j % 2)
            else:
                cp.wait()
        return carry
    lax.fori_loop(0, (n_groups + GROUP_UNROLL - 1) // GROUP_UNROLL, body, 0)


def _dispatch_body(ng_ref, dt_ref, lo_ref, sv_ref, rm_ref, tail_ref, h2_ref, route_ref, soff_ref, xs_ref,
                   loc_scr, carry_scr, zero_scr, sem, *, nt):
    t = pl.program_id(0)
    buf = t % 2

    def copies_of(tt):
        def make(g, row):
            return pltpu.make_async_copy(
                loc_scr.at[tt % 2, pl.ds(pl.multiple_of(g * ROW_ALIGN, ROW_ALIGN), ROW_ALIGN)],
                xs_ref.at[pl.ds(pl.multiple_of(row, ROW_ALIGN), ROW_ALIGN)], sem.at[tt % 2])
        return ng_ref[tt], dt_ref, tt * GROUPS, make

    @pl.when(t == 0)
    def _():
        carry_scr[...] = jnp.zeros(carry_scr.shape, BF16)
        loc_scr[...] = jnp.zeros(loc_scr.shape, BF16)

    @pl.when(t >= 2)
    def _():
        _for_groups(*copies_of(t - 2), start=False)

    route_t = route_ref[...].T
    soff_col = soff_ref[...]
    slots = [_slot_rows(route_t, soff_col, k) for k in range(TOP_K)]

    def sort_rows(m):
        r_i = lax.broadcasted_iota(jnp.int32, (m, TOK_TILE), 0).astype(F32)
        onehot = jnp.zeros((m, TOK_TILE), F32)
        for k in range(TOP_K):
            onehot = jnp.where(r_i == slots[k], 1.0, onehot)
        loc_scr[buf, 0:m, :] = _dot(onehot.astype(BF16), h2_ref[...]).astype(BF16)

    fits = ng_ref[t] * ROW_ALIGN <= M_LOC_SHORT
    pl.when(fits)(lambda: sort_rows(M_LOC_SHORT))
    pl.when(jnp.logical_not(fits))(lambda: sort_rows(M_LOC))

    for e in range(N_EXP):
        lo = pl.multiple_of(lo_ref[t * N_EXP + e], ROW_ALIGN)
        sv = pl.multiple_of(sv_ref[t * N_EXP + e], ROW_ALIGN)
        first = loc_scr[buf, pl.ds(lo, ROW_ALIGN), :].astype(F32) + carry_scr[e].astype(F32)
        loc_scr[buf, pl.ds(lo, ROW_ALIGN), :] = first.astype(BF16)
        pending = loc_scr[buf, pl.ds(sv, ROW_ALIGN), :]
        carry_scr[e] = jnp.where(rm_ref[t * N_EXP + e] > 0, pending, jnp.zeros_like(pending))

    _for_groups(*copies_of(t), start=True)

    @pl.when(t == nt - 1)
    def _():
        if nt >= 2:
            _for_groups(*copies_of(t - 1), start=False)
        _for_groups(*copies_of(t), start=False)
        zero_scr[...] = jnp.zeros(zero_scr.shape, BF16)
        tails = []
        for e in range(N_EXP):
            tails += _segment_copies(zero_scr, xs_ref, sem.at[0], 0, tail_ref[e], tail_ref[N_EXP + e],
                                     FFN_TILE // ROW_ALIGN - 1)
        _run_copies(tails)


def _dispatch(meta, h2, route, soff_col):
    nt = h2.shape[0] // TOK_TILE
    grid_spec = pltpu.PrefetchScalarGridSpec(
        num_scalar_prefetch=6,
        grid=(nt,),
        in_specs=[pl.BlockSpec((TOK_TILE, D_MODEL), lambda t, *_: (t, 0)),
                  pl.BlockSpec((TOK_TILE, LANES), lambda t, *_: (t, 0)),
                  pl.BlockSpec((None, LANES, 1), lambda t, *_: (t, 0, 0))],
        out_specs=pl.BlockSpec(memory_space=pl.ANY),
        scratch_shapes=[pltpu.VMEM((2, M_LOC + ROW_ALIGN, D_MODEL), BF16),
                        pltpu.VMEM((N_EXP, ROW_ALIGN, D_MODEL), BF16),
                        pltpu.VMEM((FFN_TILE, D_MODEL), BF16), pltpu.SemaphoreType.DMA((2,))],
    )
    return pl.pallas_call(
        functools.partial(_dispatch_body, nt=nt),
        grid_spec=grid_spec,
        out_shape=jax.ShapeDtypeStruct((meta["rows"] + 2 * M_LOC, D_MODEL), BF16),
        compiler_params=_params(("arbitrary",)),
        name="moe_dispatch",
    )(meta["n_groups"], meta["dispatch_rows"], meta["slot_start"], meta["carry_start"], meta["carry_rows"],
      meta["tail"], h2, route, soff_col)


def _ffn_body(be_ref, nu_ref, nx_ref, sl_ref, x_ref, bgu_ref, bd_ref, wgu_hbm, wd_hbm, y_ref,
              wgu_f32, wd_f32, wgu_scr, wd_scr, sem):
    i = pl.program_id(0)
    expert = be_ref[i]
    slot = sl_ref[i]
    prev = be_ref[jnp.maximum(i - 1, 0)]

    def weight_copies(e, s):
        return (pltpu.make_async_copy(wgu_hbm.at[e], wgu_f32.at[s], sem.at[s]),
                pltpu.make_async_copy(wd_hbm.at[e], wd_f32.at[s], sem.at[s]))

    @pl.when(i == 0)
    def _():
        for cp in weight_copies(expert, slot):
            cp.start(priority=1)

    @pl.when(jnp.logical_and(i < nu_ref[0], jnp.logical_or(i == 0, expert != prev)))
    def _():
        for cp in weight_copies(expert, slot):
            cp.wait()
        wgu_scr[...] = wgu_f32[slot].astype(BF16)
        wd_scr[...] = wd_f32[slot].astype(BF16)

        @pl.when(nx_ref[i] >= 0)
        def _():
            for cp in weight_copies(nx_ref[i], 1 - slot):
                cp.start(priority=1)

    @pl.when(i < nu_ref[0])
    def _():
        gu = _dot(x_ref[...], wgu_scr[...]) + bgu_ref[...]
        gate = jnp.minimum(gu[:, :D_FF], SWIGLU_LIMIT)
        up = jnp.clip(gu[:, D_FF:], -SWIGLU_LIMIT, SWIGLU_LIMIT)
        act = (up + 1.0) * gate * _sigmoid(SWIGLU_ALPHA * gate)
        y_ref[...] = (_dot(act.astype(BF16), wd_scr[...]) + bd_ref[...]).astype(BF16)


def _ffn(meta, xs, w_gu, b_gu, w_down, b_down):
    rows = meta["rows"]
    nblk = rows // FFN_TILE
    row_blk = pl.BlockSpec((FFN_TILE, D_MODEL), lambda i, be, nu, *_: (jnp.minimum(i, nu[0] - 1), 0))
    grid_spec = pltpu.PrefetchScalarGridSpec(
        num_scalar_prefetch=4,
        grid=(nblk,),
        in_specs=[row_blk,
                  pl.BlockSpec((None, 1, 2 * D_FF), lambda i, be, *_: (be[i], 0, 0)),
                  pl.BlockSpec((None, 1, D_MODEL), lambda i, be, *_: (be[i], 0, 0)),
                  pl.BlockSpec(memory_space=pl.ANY), pl.BlockSpec(memory_space=pl.ANY)],
        out_specs=row_blk,
        scratch_shapes=[pltpu.VMEM((2, D_MODEL, 2 * D_FF), F32), pltpu.VMEM((2, D_FF, D_MODEL), F32),
                        pltpu.VMEM((D_MODEL, 2 * D_FF), BF16), pltpu.VMEM((D_FF, D_MODEL), BF16),
                        pltpu.SemaphoreType.DMA((2,))],
    )
    return pl.pallas_call(
        _ffn_body,
        grid_spec=grid_spec,
        out_shape=jax.ShapeDtypeStruct((rows, D_MODEL), BF16),
        compiler_params=_params(("arbitrary",)),
        name="moe_ffn",
    )(meta["blk_exp"], meta["n_used"], meta["next_exp"], meta["blk_slot"], xs,
      b_gu.reshape(N_EXP, 1, 2 * D_FF), b_down.reshape(N_EXP, 1, D_MODEL), w_gu, w_down)


def _combine_body(ng_ref, ct_ref, route_ref, soff_ref, x2_ref, gtp_ref, gts_ref, gf_ref, ys_ref,
                  yp_ref, ysm_ref, loc_scr, moe_scr, sem, *, nt, nt_prompt, final):
    t = pl.program_id(0)
    buf = t % 2

    def copies_of(tt):
        def make(g, row):
            return pltpu.make_async_copy(
                ys_ref.at[pl.ds(pl.multiple_of(row, ROW_ALIGN), ROW_ALIGN)],
                loc_scr.at[tt % 2, pl.ds(pl.multiple_of(g * ROW_ALIGN, ROW_ALIGN), ROW_ALIGN)], sem.at[tt % 2])
        return ng_ref[tt], ct_ref, tt * GROUPS, make

    @pl.when(t == 0)
    def _():
        loc_scr[...] = jnp.zeros(loc_scr.shape, BF16)
        _for_groups(*copies_of(0), start=True)

    @pl.when(t + 1 < nt)
    def _():
        _for_groups(*copies_of(t + 1), start=True)

    _for_groups(*copies_of(t), start=False)

    route = route_ref[...]
    soff_row = soff_ref[...]
    lane = lax.broadcasted_iota(jnp.int32, (TOK_TILE, LANES), 1).astype(F32)
    slots = [jnp.sum(jnp.where(lane == route[:, k:k + 1], soff_row, 0.0), axis=1, keepdims=True)
             + route[:, 2 * TOP_K + k:2 * TOP_K + k + 1] for k in range(TOP_K)]

    def weighted_sum(m):
        c_i = lax.broadcasted_iota(jnp.int32, (TOK_TILE, m), 1).astype(F32)
        wmat = jnp.zeros((TOK_TILE, m), F32)
        for k in range(TOP_K):
            wmat = jnp.where(c_i == slots[k], route[:, TOP_K + k:TOP_K + k + 1], wmat)
        moe_scr[...] = _dot(wmat.astype(BF16), loc_scr[buf, 0:m, :])

    fits = ng_ref[t] * ROW_ALIGN <= M_LOC_SHORT
    pl.when(fits)(lambda: weighted_sum(M_LOC_SHORT))
    pl.when(jnp.logical_not(fits))(lambda: weighted_sum(M_LOC))

    gate = jnp.where(t >= nt_prompt, gts_ref[...], gtp_ref[...])
    xo = x2_ref[...] + gate * moe_scr[...]
    if final:
        ms = jnp.mean(xo * xo, axis=1, keepdims=True)
        xo = (xo * lax.rsqrt(ms + EPS)) * gf_ref[...]

    @pl.when(t < nt_prompt)
    def _():
        yp_ref[...] = xo

    @pl.when(t >= nt_prompt)
    def _():
        ysm_ref[...] = xo


def _combine(meta, route, soff_row, x2, gt2_p, gt2_s, g_final, ys, n_prompt, n_sample, final):
    nt = x2.shape[0] // TOK_TILE
    nt_prompt = n_prompt // TOK_TILE
    tiles_per_batch = nt_prompt // gt2_p.shape[0]
    grid_spec = pltpu.PrefetchScalarGridSpec(
        num_scalar_prefetch=2,
        grid=(nt,),
        in_specs=[pl.BlockSpec((TOK_TILE, LANES), lambda t, *_: (t, 0)),
                  pl.BlockSpec((None, 1, LANES), lambda t, *_: (t, 0, 0)),
                  pl.BlockSpec((TOK_TILE, D_MODEL), lambda t, *_: (t, 0)),
                  pl.BlockSpec((None, 1, D_MODEL),
                               lambda t, *_: (jnp.minimum(t, nt_prompt - 1) // tiles_per_batch, 0, 0)),
                  pl.BlockSpec((TOK_TILE, D_MODEL), lambda t, *_: (0, 0)),
                  pl.BlockSpec((1, D_MODEL), lambda t, *_: (0, 0)),
                  pl.BlockSpec(memory_space=pl.ANY)],
        out_specs=[pl.BlockSpec((TOK_TILE, D_MODEL), lambda t, *_: (jnp.minimum(t, nt_prompt - 1), 0)),
                   pl.BlockSpec((TOK_TILE, D_MODEL), lambda t, *_: (0, 0))],
        scratch_shapes=[pltpu.VMEM((2, M_LOC, D_MODEL), BF16), pltpu.VMEM((TOK_TILE, D_MODEL), F32),
                        pltpu.SemaphoreType.DMA((2,))],
    )
    return pl.pallas_call(
        functools.partial(_combine_body, nt=nt, nt_prompt=nt_prompt, final=final),
        grid_spec=grid_spec,
        out_shape=[jax.ShapeDtypeStruct((n_prompt, D_MODEL), F32), jax.ShapeDtypeStruct((n_sample, D_MODEL), F32)],
        compiler_params=_params(("arbitrary",)),
        name="moe_combine",
    )(meta["n_groups"], meta["combine_rows"], route, soff_row, x2, gt2_p, gt2_s, g_final, ys)


def _moe_offsets(cnt):
    nt = cnt.shape[0]
    ra = ROW_ALIGN
    prefix = jnp.cumsum(cnt, axis=0) - cnt
    total = jnp.sum(cnt, axis=0)
    pending = prefix % ra
    used = pending + cnt
    seg = (used + ra - 1) // ra * ra
    lo = jnp.cumsum(seg, axis=1) - seg
    n_groups = jnp.sum(seg, axis=1) // ra
    gpad = (total + FFN_TILE - 1) // FFN_TILE * FFN_TILE
    gstart = jnp.cumsum(gpad) - gpad
    base = gstart[None, :] + prefix // ra * ra
    last = (jnp.arange(nt) == nt - 1)[:, None]
    n_write = jnp.where(last, seg // ra, used // ra)
    carry_start = lo + used // ra * ra
    carry_rows = jnp.where(last, 0, used % ra)
    g = jnp.arange(GROUPS)
    slot_end = (lo + seg) // ra
    owner = jnp.minimum(jnp.sum(g[None, :, None] >= slot_end[:, None, :], axis=2), N_EXP - 1)
    pick = lambda a: jnp.sum(jnp.where(owner[:, :, None] == jnp.arange(N_EXP), a[:, None, :], 0), axis=2)
    k = g[None, :] - pick(lo) // ra
    row = pick(base) + k * ra
    valid = g[None, :] < n_groups[:, None]
    rows = (nt * TOK_TILE * TOP_K + N_EXP * (FFN_TILE - 1) + FFN_TILE - 1) // FFN_TILE * FFN_TILE
    spare = rows + (jnp.arange(nt) % 2)[:, None] * M_LOC + g[None, :] * ra
    combine_rows = jnp.where(valid, row, 0)
    dispatch_rows = jnp.where(valid & (k < pick(n_write)), row, spare)

    nblk_e = gpad // FFN_TILE
    blk_end = jnp.cumsum(nblk_e)
    n_used = jnp.maximum(blk_end[-1], 1)
    blk = jnp.minimum(jnp.arange(rows // FFN_TILE, dtype=jnp.int32), n_used - 1)
    blk_exp = jnp.minimum(jnp.sum(blk[:, None] >= blk_end[None, :], axis=1), N_EXP - 1)
    experts = jnp.arange(N_EXP)
    following = lax.cummin(jnp.where(nblk_e > 0, experts, N_EXP), reverse=True)
    next_of = jnp.concatenate([following[1:], jnp.full((1,), N_EXP, following.dtype)])
    next_of = jnp.where(next_of >= N_EXP, -1, next_of)
    parity = (jnp.cumsum(nblk_e > 0) - 1) % 2
    pick_e = lambda a: jnp.sum(jnp.where(blk_exp[:, None] == experts[None, :], a[None, :], 0), axis=1)
    total16 = (total + ra - 1) // ra * ra
    tail = jnp.concatenate([gstart + total16, (gpad - total16) // ra])
    i32 = lambda a: a.astype(jnp.int32).reshape(-1)
    return dict(n_groups=i32(n_groups), dispatch_rows=i32(dispatch_rows), combine_rows=i32(combine_rows),
                slot_start=i32(lo), carry_start=i32(carry_start), carry_rows=i32(carry_rows), tail=i32(tail),
                blk_exp=i32(blk_exp), n_used=i32(n_used), next_exp=i32(pick_e(next_of)), blk_slot=i32(pick_e(parity)),
                slot_off=(lo + pending).astype(F32), rows=rows)


def _pack_w_in(w_in):
    pad = jnp.zeros((D_MODEL, LANES - N_GATE), w_in.dtype)
    g0 = C_GA
    return jnp.concatenate([w_in[:, :g0], w_in[:, g0 + N_GATE:], w_in[:, g0:g0 + N_GATE], pad],
                           axis=1).astype(BF16)


def _pad_lanes(a, value=0.0):
    return jnp.pad(a, [(0, 0)] * (a.ndim - 1) + [(0, LANES - a.shape[-1])], constant_values=value)


def kernel(x_prompt, x_sample, c_prompt, c_sample, cache_k, cache_v, state_conv, state_C, state_n, state_m, page_table, w_ada, b_ada, g_norm1, g_norm2, w_in, b_gates, lambda_q1, lambda_k1, lambda_q2, lambda_k2, g_subln, w_conv, b_conv, g_mnorm, w_up_a, w_up_b, w_out, w_router, b_router, w_gu, b_gu, w_down, b_down, g_final):
    B, S, D = x_prompt.shape
    Bd, Td, _ = x_sample.shape
    depth = w_in.shape[0]
    n_pool = cache_k.shape[1]
    past_len = page_table.shape[1] * PAGE_SIZE
    n_p, n_s = B * S, Bd * Td
    n_all = n_p + n_s
    assert D == D_MODEL and n_s == TOK_TILE and S % MLSTM_CHUNK == 0 and n_p % TOK_TILE == 0
    assert page_table.shape[1] % PAGES_PER_STEP == 0

    cos_p, sin_p = _rope_tables(np.arange(S))
    cos_s, sin_s = _rope_tables(np.tile(past_len + np.arange(Td), Bd))
    hp = x_prompt.reshape(n_p, D)
    hs = x_sample.reshape(n_s, D)
    c_all = jnp.concatenate([c_prompt, c_sample], axis=0)
    outs = [[] for _ in range(12)]

    for l in range(depth):
        lam_init = 0.8 - 0.6 * math.exp(-0.3 * l)
        mod = _ada(c_all, w_ada[l], b_ada[l])
        mods = [mod[:, j * D:(j + 1) * D] for j in range(6)]
        mp = [m[:B].reshape(B, 1, D) for m in mods]
        ms_ = [jnp.repeat(m[B:], Td, axis=0).reshape(1, n_s, D) for m in mods]
        w_packed = _pack_w_in(w_in[l])
        g1 = g_norm1[l].reshape(1, D)
        lam_vec = jnp.stack([lambda_q1[l], lambda_k1[l], lambda_q2[l], lambda_k2[l]])
        gsub = g_subln[l].reshape(1, DV_A)
        bg = _pad_lanes(b_gates[l].reshape(1, N_GATE))
        cw, cb = w_conv[l], b_conv[l].reshape(1, 2 * W_M)

        (q_p, _, kb_p, v_p, _, qk_p, vm_p, om_p, ga_p, gb_p, gt_p, vt_p, kt_p) = _inproj(
            hp, mp[1], mp[0], g1, cos_p, sin_p, w_packed, 1)
        (q_s, k_s, kb_s, v_s, vb_s, qk_s, vm_s, om_s, ga_s, gb_s, gt_s, _, _) = _inproj(
            hs, ms_[1], ms_[0], g1, cos_s, sin_s, w_packed, n_s)
        k_p = jnp.transpose(kt_p.reshape(B, H_A, 2, DK_A, S), (0, 4, 1, 2, 3))

        oa_p = _attn_prompt(q_p, kb_p, vt_p, lam_vec, gsub.reshape(DV_A, 1), B, S, lam_init)
        kt_pool = jnp.transpose(cache_k[l], (0, 2, 3, 4, 1)).reshape(n_pool, W_QA, PAGE_SIZE)
        v_pool = cache_v[l].reshape(n_pool, PAGE_SIZE * H_A, DV_A)
        oa_s = _attn_sample(q_s, kb_s, vb_s, kt_pool, v_pool, page_table, lam_vec, gsub, lam_init, Td)

        zeros = lambda *shape: jnp.zeros(shape, F32)
        hm_p, cst_p, C_p, nn_p, m_p = _mlstm(qk_p, vm_p, om_p, gt_p, cw, cb, bg, g_mnorm[l],
                                             zeros(B, CONV_W - 1, 2 * W_M), zeros(B, H_M, DH_M, DH_M),
                                             zeros(B, H_M, DH_M), zeros(B, 1, LANES), B, S)
        hm_s, cst_s, C_s, nn_s, m_s = _mlstm(qk_s, vm_s, om_s, gt_s, cw, cb, bg, g_mnorm[l],
                                             state_conv[l], state_C[l], state_n[l],
                                             _pad_lanes(state_m[l]).reshape(Bd, 1, LANES), Bd, Td)

        wa, wb, wo = w_up_a[l].astype(BF16), w_up_b[l].astype(BF16), w_out[l].astype(BF16)
        wr = _pad_lanes(w_router[l]).astype(BF16)
        br = _pad_lanes(b_router[l].reshape(1, N_EXP))
        g2 = g_norm2[l].reshape(1, D)
        part = _merge(oa_p, hm_p, ga_p, gb_p, hp, mp[2], mp[4], mp[3], g2, wa, wb, wo, wr, br, 1, 0, n_all)
        x2, h2, route, cnt = _merge(oa_s, hm_s, ga_s, gb_s, hs, ms_[2], ms_[4], ms_[3], g2, wa, wb, wo, wr, br,
                                    n_s, n_p // TOK_TILE, n_all, prev=part)

        meta = _moe_offsets(jnp.round(cnt[:, 0, ROUTE_SEL:ROUTE_SEL + N_EXP]).astype(jnp.int32))
        soff = _pad_lanes(meta["slot_off"])
        xs = _dispatch(meta, h2, route, soff[:, :, None])
        ys = _ffn(meta, xs, w_gu[l], b_gu[l], w_down[l], b_down[l])
        final = l == depth - 1
        hp, hs = _combine(meta, route, soff[:, None, :], x2, mp[5], ms_[5][0], g_final.reshape(1, D), ys,
                          n_p, n_s, final)

        for j, a in enumerate([k_p, v_p.reshape(B, S, H_A, DV_A), cst_p, C_p, nn_p,
                               m_p[:, 0, :H_M],
                               k_s.reshape(Bd, Td, H_A, 2, DK_A), v_s.reshape(Bd, Td, H_A, DV_A), cst_s, C_s, nn_s,
                               m_s[:, 0, :H_M]]):
            outs[j].append(a)

    return (hp.reshape(B, S, D), hs.reshape(Bd, Td, D)) + tuple(jnp.stack(o) for o in outs)
```

```python
import functools
import math

import numpy as np
import jax
import jax.numpy as jnp
from jax import lax
from jax.experimental import pallas as pl
from jax.experimental.pallas import tpu as pltpu

F32 = jnp.float32
BF16 = jnp.bfloat16

D_MODEL = 1024
H_A = 4
DK_A = 64
DV_A = 2 * DK_A
ROPE_THETA = 10000.0
H_M = 4
DH_M = 128
CONV_W = 4
N_EXP = 32
TOP_K = 4
D_FF = D_MODEL
SWIGLU_LIMIT = 7.0
SWIGLU_ALPHA = 1.702
EPS = 1e-6
PAGE_SIZE = 128

W_QA = H_A * 2 * DK_A
W_VA = H_A * DV_A
W_M = H_M * DH_M
N_GATE = 2 * H_M

LANES = 128
ROW_ALIGN = 16
TOK_TILE = 256
FFN_TILE = 256
ATT_TILE = 256
ATT_HEADS = 4
MLSTM_CHUNK = 256
PAGES_PER_STEP = 16
CHUNKS_PER_STEP = 2
NEG_BIG = -1e30
LOG2_E = math.log2(math.e)
VMEM_LIMIT = 56 * 1024 * 1024

C_QA, C_KA, C_VA, C_QK, C_VM, C_OM, C_GA, C_GB, C_GT = 0, 512, 1024, 1536, 2560, 3072, 3584, 4608, 5632
D_IN_PACKED = C_GT + LANES
M_LOC = ((TOK_TILE * TOP_K + 2 * N_EXP * (ROW_ALIGN - 1)) + 255) // 256 * 256
GROUPS = M_LOC // ROW_ALIGN
M_LOC_SHORT = 1536
ROUTE_SEL = 32
GROUP_UNROLL = 4


def _dot(a, b):
    return jnp.dot(a, b, preferred_element_type=F32)


def _dot_nt(a, b):
    return lax.dot_general(a, b, (((1,), (1,)), ((), ())), preferred_element_type=F32)


def _sigmoid(x):
    return 0.5 * jnp.tanh(0.5 * x) + 0.5


def _start_all(copies):
    for cp in copies:
        cp.start()


def _params(sem):
    return pltpu.CompilerParams(dimension_semantics=sem, vmem_limit_bytes=VMEM_LIMIT)


def _ada_body(c_ref, w_ref, b_ref, o_ref):
    c = c_ref[...]
    s = c * _sigmoid(c)
    s_hi = s.astype(BF16)
    s_lo = (s - s_hi.astype(F32)).astype(BF16)
    w = w_ref[...]
    w_hi = w.astype(BF16)
    w_lo = (w - w_hi.astype(F32)).astype(BF16)
    o_ref[...] = _dot(s_hi, w_hi) + _dot(s_lo, w_hi) + _dot(s_hi, w_lo) + b_ref[...]


def _ada(c_all, w_ada, b_ada):
    rows = c_all.shape[0]
    n_out = w_ada.shape[1]
    blk = 1024
    return pl.pallas_call(
        _ada_body,
        grid=(n_out // blk,),
        in_specs=[pl.BlockSpec((rows, D_MODEL), lambda j: (0, 0)),
                  pl.BlockSpec((D_MODEL, blk), lambda j: (0, j)),
                  pl.BlockSpec((1, blk), lambda j: (0, j))],
        out_specs=pl.BlockSpec((rows, blk), lambda j: (0, j)),
        out_shape=jax.ShapeDtypeStruct((rows, n_out), F32),
        compiler_params=_params(("arbitrary",)),
        name="ada",
    )(c_all, w_ada, b_ada.reshape(1, n_out))


def _rope(z, cos, sin):
    lane = lax.broadcasted_iota(jnp.int32, (z.shape[0], LANES), 1)
    first_half = (lane % DK_A) < (DK_A // 2)
    out = []
    for h in range(H_A):
        xh = z[:, h * LANES:(h + 1) * LANES]
        partner = jnp.where(first_half, pltpu.roll(xh, LANES - DK_A // 2, 1), pltpu.roll(xh, DK_A // 2, 1))
        out.append(xh * cos + partner * sin)
    return jnp.concatenate(out, axis=1)


def _conv_silu(u, ext, cw_ref, cb_ref):
    rows = u.shape[0]
    full = jnp.concatenate([ext, u], axis=0)
    conv = cb_ref[...] + cw_ref[CONV_W - 1:CONV_W, :] * u
    for j in range(CONV_W - 1):
        conv = conv + cw_ref[j:j + 1, :] * pltpu.roll(full, CONV_W - 1 - j, 0)[8:8 + rows]
    return conv * _sigmoid(conv)


def _inproj_body(x_ref, sc_ref, sh_ref, g_ref, cos_ref, sin_ref, w_ref,
                 q_ref, k_ref, kb_ref, v_ref, vb_ref, qk_ref, vm_ref, om_ref, ga_ref, gb_ref, gt_ref, vt_ref, kt_ref):
    x = x_ref[...]
    ms = jnp.mean(x * x, axis=1, keepdims=True)
    h = (x * lax.rsqrt(ms + EPS)) * g_ref[...] * (1.0 + sc_ref[...]) + sh_ref[...]
    hb = h.astype(BF16)
    cos = cos_ref[...]
    sin = sin_ref[...]

    def seg(lo, n):
        return _dot(hb, w_ref[:, lo:lo + n])

    q = _rope(seg(C_QA, W_QA), cos, sin) * (DK_A ** -0.5 * LOG2_E)
    q_ref[...] = q.astype(BF16)
    k = _rope(seg(C_KA, W_QA), cos, sin)
    k_ref[...] = k
    kb_ref[...] = k.astype(BF16)
    kt_ref[...] = k.T
    v = seg(C_VA, W_VA)
    for h in range(H_A):
        v_ref[pl.ds(h, v.shape[0], stride=H_A), :] = v[:, h * DV_A:(h + 1) * DV_A]
    vb_ref[...] = v.astype(BF16)
    vt_ref[...] = v.T.astype(BF16)
    qk_ref[...] = seg(C_QK, 2 * W_M)
    vm_ref[...] = seg(C_VM, W_M).astype(BF16)
    om_ref[...] = seg(C_OM, W_M)
    ga_ref[...] = seg(C_GA, D_MODEL)
    gb_ref[...] = seg(C_GB, D_MODEL)
    gt_ref[...] = seg(C_GT, LANES)


def _inproj(x, sc, sh, g1, cos, sin, w_packed, rows_per_mod):
    n = x.shape[0]
    nt = n // TOK_TILE
    tiles_per_group = nt // sc.shape[0]
    tab_tiles = cos.shape[0] // TOK_TILE
    tok = lambda w: pl.BlockSpec((TOK_TILE, w), lambda i: (i, 0))
    mod = pl.BlockSpec((None, rows_per_mod, D_MODEL), lambda i: (i // tiles_per_group, 0, 0))
    tab = pl.BlockSpec((TOK_TILE, LANES), lambda i: (i % tab_tiles, 0))
    widths = [(W_QA, BF16), (W_QA, F32), (W_QA, BF16), (W_VA, F32), (W_VA, BF16), (2 * W_M, F32),
              (W_M, BF16), (W_M, F32), (D_MODEL, F32), (D_MODEL, F32), (LANES, F32)]
    vt_spec = pl.BlockSpec((None, W_VA, TOK_TILE), lambda i: (i, 0, 0))
    kt_spec = pl.BlockSpec((None, W_QA, TOK_TILE), lambda i: (i // tab_tiles, 0, i % tab_tiles))
    return pl.pallas_call(
        _inproj_body,
        grid=(nt,),
        in_specs=[tok(D_MODEL), mod, mod, pl.BlockSpec((1, D_MODEL), lambda i: (0, 0)), tab, tab,
                  pl.BlockSpec((D_MODEL, D_IN_PACKED), lambda i: (0, 0), pipeline_mode=pl.Buffered(1))],
        out_specs=[pl.BlockSpec((TOK_TILE * H_A, DV_A), lambda i: (i, 0)) if j == 3 else tok(w)
                   for j, (w, _) in enumerate(widths)] + [vt_spec, kt_spec],
        out_shape=[jax.ShapeDtypeStruct((n * H_A, DV_A) if j == 3 else (n, w), dt)
                   for j, (w, dt) in enumerate(widths)]
                  + [jax.ShapeDtypeStruct((nt, W_VA, TOK_TILE), BF16),
                     jax.ShapeDtypeStruct((nt // tab_tiles, W_QA, tab_tiles * TOK_TILE), F32)],
        compiler_params=_params(("arbitrary",)),
        name="inproj",
    )(x, sc, sh, g1, cos, sin, w_packed)


def _rope_tables(pos):
    half = DK_A // 2
    inv = ROPE_THETA ** (-np.arange(half, dtype=np.float64) * 2.0 / DK_A)
    ang = np.asarray(pos, np.float64)[:, None] * inv[None, :]
    cos = np.cos(ang)
    sin = np.sin(ang)
    cos64 = np.concatenate([cos, cos], axis=1)
    sin64 = np.concatenate([-sin, sin], axis=1)
    return (jnp.asarray(np.tile(cos64, (1, LANES // DK_A)), F32),
            jnp.asarray(np.tile(sin64, (1, LANES // DK_A)), F32))


def _lambda_value(lam_ref, lam_init):
    lv = lam_ref[...]
    l1 = jnp.sum(lv[0:1, :] * lv[1:2, :], axis=1, keepdims=True)
    l2 = jnp.sum(lv[2:3, :] * lv[3:4, :], axis=1, keepdims=True)
    return jnp.exp(l1) - jnp.exp(l2) + lam_init


def _subln(o, g, lam_init):
    ms = jnp.mean(o * o, axis=1, keepdims=True)
    return (o * lax.rsqrt(ms + EPS)) * g * (1.0 - lam_init)


def _prompt_tile(q_ref, k_ref, vt_ref, lam, g_ref, o_ref, m_scr, acc_scr, i, lam_init):
    tq = ATT_TILE
    lane = lax.broadcasted_iota(jnp.int32, (tq, LANES), 1)
    qs = []
    for hh in range(ATT_HEADS):
        q = q_ref[:, hh * LANES:(hh + 1) * LANES]
        zero = jnp.zeros_like(q)
        qs.append(jnp.concatenate([jnp.where(lane < DK_A, q, zero), jnp.where(lane >= DK_A, q, zero)], axis=0))
    for hh in range(ATT_HEADS):
        m_scr[hh][...] = jnp.full(m_scr[hh].shape, NEG_BIG, F32)
        acc_scr[hh][...] = jnp.zeros(acc_scr[hh].shape, F32)
    ones = jnp.ones((ROW_ALIGN, tq), BF16)

    def scores(hh, j):
        start = pl.multiple_of(j * tq, tq)
        return _dot_nt(k_ref[pl.ds(start, tq), hh * LANES:(hh + 1) * LANES], qs[hh])

    def update_all(j, mask):
        sts = [scores(hh, j) for hh in range(ATT_HEADS)]
        if mask is not None:
            sts = [jnp.where(mask, st, NEG_BIG) for st in sts]
        pts, alphas = [], []
        for hh in range(ATT_HEADS):
            m_old = m_scr[hh][...]
            m_new = jnp.maximum(m_old, jnp.max(sts[hh], axis=0, keepdims=True))
            alphas.append(jnp.exp2(m_old - m_new))
            pts.append(jnp.exp2(sts[hh] - m_new).astype(BF16))
            m_scr[hh][...] = m_new
        for hh in range(ATT_HEADS):
            vt = jnp.concatenate([vt_ref[j, hh * LANES:(hh + 1) * LANES, :], ones], axis=0)
            acc_scr[hh][...] = alphas[hh] * acc_scr[hh][...] + _dot(vt, pts[hh])

    def off_diag(j, carry):
        update_all(j, None)
        return carry

    lax.fori_loop(0, i, off_diag, 0)
    key = lax.broadcasted_iota(jnp.int32, (tq, 2 * tq), 0)
    qry = lax.broadcasted_iota(jnp.int32, (tq, 2 * tq), 1) % tq
    update_all(i, key <= qry)
    for hh in range(ATT_HEADS):
        acc = acc_scr[hh][...]
        ot = acc[:DV_A] / acc[DV_A:DV_A + 1]
        at = ot[:, :tq] - lam * ot[:, tq:]
        ms = jnp.mean(at * at, axis=0, keepdims=True)
        at = (at * lax.rsqrt(ms + EPS)) * g_ref[...] * (1.0 - lam_init)
        o_ref[:, hh * LANES:(hh + 1) * LANES] = at.T.astype(BF16)


def _sample_chunks(pt_ref, q_ref, kn_ref, vn_ref, lam, g_ref, kpool_ref, vpool_ref, o_ref,
                   kbuf, vbuf, sem, m_scr, l_scr, acc_scr, step, parts, *, lam_init, n_chunks, n_seq, t_new):
    steps_per_seq = n_chunks // CHUNKS_PER_STEP
    b = step // steps_per_seq
    c0 = (step % steps_per_seq) * CHUNKS_PER_STEP
    opens = 0 in parts
    closes = CHUNKS_PER_STEP - 1 in parts

    def chunk_copies(bb, c, slot):
        out = []
        for j in range(PAGES_PER_STEP):
            page = pt_ref[bb, c * PAGES_PER_STEP + j]
            out.append(pltpu.make_async_copy(kpool_ref.at[page], kbuf.at[slot, j], sem.at[slot]))
            out.append(pltpu.make_async_copy(vpool_ref.at[page], vbuf.at[slot, j], sem.at[slot]))
        return out

    if opens:
        @pl.when(step == 0)
        def _():
            _start_all(chunk_copies(0, 0, 0) + chunk_copies(0, 1, 1))

    q = q_ref[...].astype(F32)
    qt = jnp.concatenate([q] * (2 * H_A), axis=0)
    row = lax.broadcasted_iota(jnp.int32, qt.shape, 0)
    col = lax.broadcasted_iota(jnp.int32, qt.shape, 1)
    qbd = jnp.where(col // DK_A == row // t_new, qt, 0.0).astype(BF16)

    if opens:
        @pl.when(c0 == 0)
        def _():
            m_scr[...] = jnp.full(m_scr.shape, NEG_BIG, F32)
            l_scr[...] = jnp.zeros(l_scr.shape, F32)
            acc_scr[...] = jnp.zeros(acc_scr.shape, F32)

    rows_h = 2 * t_new

    def update(s, v_of_head):
        m_old = m_scr[...]
        m_new = jnp.maximum(m_old, jnp.max(s, axis=1, keepdims=True))
        alpha = jnp.exp2(m_old - m_new)
        p = jnp.exp2(s - m_new)
        l_scr[...] = alpha * l_scr[...] + jnp.sum(p, axis=1, keepdims=True)
        pb = p.astype(BF16)
        pv = [_dot(pb[h * rows_h:(h + 1) * rows_h, :], v_of_head(h)) for h in range(H_A)]
        acc_scr[...] = alpha * acc_scr[...] + jnp.concatenate(pv, axis=0)
        m_scr[...] = m_new

    def chunk(c, slot):
        for cp in chunk_copies(b, c, slot):
            cp.wait()
        kt = jnp.concatenate([kbuf[slot, j].astype(BF16) for j in range(PAGES_PER_STEP)], axis=1)

        def cached_v(h):
            return jnp.concatenate([vbuf.at[slot, j][pl.ds(h, PAGE_SIZE, stride=H_A), :].astype(BF16)
                                    for j in range(PAGES_PER_STEP)], axis=0)

        update(_dot(qbd, kt), cached_v)

        @pl.when(c + 2 < n_chunks)
        def _():
            _start_all(chunk_copies(b, c + 2, slot))

        @pl.when(jnp.logical_and(c + 2 >= n_chunks, b + 1 < n_seq))
        def _():
            _start_all(chunk_copies(b + 1, c + 2 - n_chunks, slot))

    for j in parts:
        chunk(c0 + j, j % 2)

    def finish():
        zpad = jnp.zeros((PAGE_SIZE - t_new, W_QA), F32)
        kn = jnp.concatenate([kn_ref[...].astype(F32), zpad], axis=0).astype(BF16)
        vn = jnp.concatenate([vn_ref[...].astype(F32), zpad], axis=0).astype(BF16)
        s = _dot_nt(qbd, kn)
        row = lax.broadcasted_iota(jnp.int32, s.shape, 0) % t_new
        col = lax.broadcasted_iota(jnp.int32, s.shape, 1)
        update(jnp.where(col <= row, s, NEG_BIG), lambda h: vn[:, h * DV_A:(h + 1) * DV_A])
        o = acc_scr[...] / l_scr[...]
        outs = []
        for h in range(H_A):
            r0 = h * rows_h
            outs.append(_subln(o[r0:r0 + t_new] - lam * o[r0 + t_new:r0 + rows_h], g_ref[...], lam_init))
        o_ref[...] = jnp.concatenate(outs, axis=1).astype(BF16)

    if closes:
        pl.when(c0 + CHUNKS_PER_STEP == n_chunks)(finish)


def _attention_body(pt_ref, q_ref, k_ref, vt_ref, lam_ref, gcol_ref, qs_ref, kn_ref, vn_ref, grow_ref,
                    kpool_ref, vpool_ref, o_ref, os_ref, *scratch, lam_init, n_chunks, n_seq, t_new):
    m_scr, acc_scr = scratch[:ATT_HEADS], scratch[ATT_HEADS:2 * ATT_HEADS]
    kbuf, vbuf, sem, ms_scr, ls_scr, accs_scr = scratch[2 * ATT_HEADS:]
    step = pl.program_id(0) * pl.num_programs(1) + pl.program_id(1)
    lam = _lambda_value(lam_ref, lam_init)
    sample = functools.partial(_sample_chunks, pt_ref, qs_ref, kn_ref, vn_ref, lam, grow_ref, kpool_ref, vpool_ref,
                               os_ref, kbuf, vbuf, sem, ms_scr, ls_scr, accs_scr, step,
                               lam_init=lam_init, n_chunks=n_chunks, n_seq=n_seq, t_new=t_new)
    sample((0,))
    _prompt_tile(q_ref, k_ref, vt_ref, lam, gcol_ref, o_ref, m_scr, acc_scr, pl.program_id(1), lam_init)
    sample((1,))


def _attention(q, k, vt, q_s, k_new, v_new, cache_k, cache_v, page_table, lam_vec, g_subln, batch, seq,
               lam_init, t_new):
    nq = seq // ATT_TILE
    bd, n_pages = page_table.shape
    n_chunks = n_pages // PAGES_PER_STEP
    steps_per_seq = n_chunks // CHUNKS_PER_STEP
    assert ATT_HEADS == H_A and CHUNKS_PER_STEP % 2 == 0 and n_chunks % CHUNKS_PER_STEP == 0
    assert batch * nq == bd * steps_per_seq
    n_rows = 2 * H_A * t_new
    kv = pl.BlockSpec((seq, W_QA), lambda b, i, pt: (b, 0))
    vts = pl.BlockSpec((nq, W_VA, ATT_TILE), lambda b, i, pt: (b, 0, 0))
    qo = pl.BlockSpec((ATT_TILE, W_QA), lambda b, i, pt: (b * nq + i, 0))
    new = pl.BlockSpec((None, t_new, W_QA), lambda b, i, pt: ((b * nq + i) // steps_per_seq, 0, 0))
    const = lambda shape: pl.BlockSpec(shape, lambda b, i, pt: (0, 0))
    page_buf = pltpu.VMEM((2, PAGES_PER_STEP, W_QA, PAGE_SIZE), F32)
    grid_spec = pltpu.PrefetchScalarGridSpec(
        num_scalar_prefetch=1,
        grid=(batch, nq),
        in_specs=[qo, kv, vts, const((4, DK_A)), const((DV_A, 1)), new, new, new, const((1, DV_A)),
                  pl.BlockSpec(memory_space=pl.ANY), pl.BlockSpec(memory_space=pl.ANY)],
        out_specs=[qo, new],
        scratch_shapes=[pltpu.VMEM((1, 2 * ATT_TILE), F32)] * ATT_HEADS
                       + [pltpu.VMEM((DV_A + ROW_ALIGN, 2 * ATT_TILE), F32)] * ATT_HEADS
                       + [page_buf, page_buf, pltpu.SemaphoreType.DMA((2,)), pltpu.VMEM((n_rows, 1), F32),
                          pltpu.VMEM((n_rows, 1), F32), pltpu.VMEM((n_rows, DV_A), F32)],
    )
    o_p, o_s = pl.pallas_call(
        functools.partial(_attention_body, lam_init=lam_init, n_chunks=n_chunks, n_seq=bd, t_new=t_new),
        grid_spec=grid_spec,
        out_shape=[jax.ShapeDtypeStruct((batch * seq, W_VA), BF16), jax.ShapeDtypeStruct((bd, t_new, W_VA), BF16)],
        compiler_params=_params(("arbitrary", "arbitrary")),
        name="attention",
    )(page_table, q, k, vt, lam_vec, g_subln.reshape(DV_A, 1), q_s.reshape(bd, t_new, W_QA),
      k_new.reshape(bd, t_new, W_QA), v_new.reshape(bd, t_new, W_VA), g_subln, cache_k, cache_v)
    return o_p, o_s.reshape(bd * t_new, W_VA)


def _mlstm_body(qk_ref, vm_ref, om_ref, gt_ref, cw_ref, cb_ref, bg_ref, gm_ref, cbuf_ref, c0_ref, n0_ref, m0_ref,
                h_ref, cst_ref, cout_ref, nout_ref, mout_ref, ext_scr, c_scr, n_scr, m_scr, *, tb, L, nc):
    c_idx = pl.program_id(1)

    @pl.when(c_idx == 0)
    def _():
        ext_scr[...] = jnp.zeros(ext_scr.shape, F32)
        ext_scr[8 - (CONV_W - 1):8, :] = cbuf_ref[...]
        c_scr[...] = c0_ref[...]
        n_scr[...] = n0_ref[...]
        m_scr[...] = m0_ref[...]

    pad = L - tb
    u = qk_ref[...]
    if pad:
        u = jnp.concatenate([u, jnp.zeros((pad, u.shape[1]), F32)], axis=0)
    a = _conv_silu(u, ext_scr[...], cw_ref, cb_ref)
    if not pad:
        ext_scr[...] = u[L - 8:L]

    @pl.when(c_idx == nc - 1)
    def _():
        cst_ref[...] = qk_ref[tb - (CONV_W - 1):tb, :]

    g = gt_ref[...] + bg_ref[...]
    li = g
    lf = jnp.minimum(g, 0.0) - jnp.log1p(jnp.exp(-jnp.abs(g)))
    if pad:
        zpad = jnp.zeros((pad, LANES), F32)
        li = jnp.concatenate([li, zpad + NEG_BIG], axis=0)
        lf = jnp.concatenate([lf, zpad], axis=0)
    row = lax.broadcasted_iota(jnp.int32, (L, LANES), 0)
    lane = lax.broadcasted_iota(jnp.int32, (L, LANES), 1)
    bcum = lf
    shift = 1
    while shift < L:
        bcum = bcum + jnp.where(row >= shift, pltpu.roll(bcum, shift, 0), 0.0)
        shift *= 2
    gates = jnp.where(lane < H_M, li, bcum)
    gates_t = gates.T
    tri = lax.broadcasted_iota(jnp.int32, (L, L), 0) >= lax.broadcasted_iota(jnp.int32, (L, L), 1)
    m_all = m_scr[...]
    lane1 = lax.broadcasted_iota(jnp.int32, (1, LANES), 1)
    m_next = m_all
    vall = vm_ref[...]
    if pad:
        vall = jnp.concatenate([vall, jnp.zeros((pad, vall.shape[1]), BF16)], axis=0)

    for h in range(H_M):
        li_col = gates[:, h:h + 1]
        b_col = gates[:, H_M + h:H_M + h + 1]
        li_row = gates_t[h:h + 1, :]
        b_row = gates_t[H_M + h:H_M + h + 1, :]
        m_prev = m_all[:, h:h + 1]
        b_last = b_col[L - 1:L, :]
        log_d = jnp.where(tri, b_col - b_row + li_row, NEG_BIG)
        inter = b_col + m_prev
        mt = jnp.maximum(inter, jnp.max(log_d, axis=1, keepdims=True))
        q = a[:, h * DH_M:(h + 1) * DH_M]
        k = a[:, W_M + h * DH_M:W_M + (h + 1) * DH_M] * (DH_M ** -0.5)
        v = vall[:, h * DH_M:(h + 1) * DH_M]
        qb = q.astype(BF16)
        s = _dot_nt(qb, k.astype(BF16)) * jnp.exp(log_d - mt)
        ei = jnp.exp(inter - mt)
        c_old = c_scr[h]
        n_old = n_scr[h:h + 1, :]
        num = ei * _dot(qb, c_old.astype(BF16)) + _dot(s.astype(BF16), v)
        den = ei * jnp.sum(q * n_old, axis=1, keepdims=True) + jnp.sum(s, axis=1, keepdims=True)
        hh = num / jnp.maximum(jnp.abs(den), jnp.exp(-mt))
        g_col = b_last - b_col + li_col
        bl = b_last + m_prev
        m_new = jnp.maximum(bl, jnp.max(g_col, axis=0, keepdims=True))
        wg = jnp.exp(g_col - m_new)
        decay = jnp.exp(bl - m_new)
        kw = k * wg
        c_scr[h] = decay * c_old + _dot(kw.T.astype(BF16), v)
        n_scr[h:h + 1, :] = decay * n_old + jnp.sum(kw, axis=0, keepdims=True)
        m_next = jnp.where(lane1 == h, m_new, m_next)
        ms = jnp.mean(hh * hh, axis=1, keepdims=True)
        hn = (hh * lax.rsqrt(ms + EPS)) * gm_ref[h:h + 1, :]
        og = _sigmoid(om_ref[:, h * DH_M:(h + 1) * DH_M])
        h_ref[:, h * DH_M:(h + 1) * DH_M] = (hn[:tb] * og).astype(BF16)

    m_scr[...] = m_next

    @pl.when(c_idx == nc - 1)
    def _():
        cout_ref[...] = c_scr[...]
        nout_ref[...] = n_scr[...]
        mout_ref[...] = m_scr[...]


def _mlstm(qk, vm, om, gt, w_conv, b_conv, b_gates_pad, g_mnorm, conv_buf, c0, n0, m0_pad, batch, seq):
    tb = min(seq, MLSTM_CHUNK)
    L = max(tb, LANES)
    nc = seq // tb
    tok = lambda w: pl.BlockSpec((None, tb, w), lambda b, c: (b * nc + c, 0, 0))
    chunks = lambda a: a.reshape(batch * nc, tb, a.shape[-1])
    const = lambda shape: pl.BlockSpec(shape, lambda b, c: (0,) * len(shape))
    per_b = lambda shape: pl.BlockSpec((None,) + shape, lambda b, c: (b,) + (0,) * len(shape))
    h, cst, c_out, n_out, m_out = pl.pallas_call(
        functools.partial(_mlstm_body, tb=tb, L=L, nc=nc),
        grid=(batch, nc),
        in_specs=[tok(2 * W_M), tok(W_M), tok(W_M), tok(LANES), const((CONV_W, 2 * W_M)), const((1, 2 * W_M)),
                  const((1, LANES)), const((H_M, DH_M)), per_b((CONV_W - 1, 2 * W_M)),
                  per_b((H_M, DH_M, DH_M)), per_b((H_M, DH_M)), per_b((1, LANES))],
        out_specs=[tok(W_M), per_b((CONV_W - 1, 2 * W_M)), per_b((H_M, DH_M, DH_M)), per_b((H_M, DH_M)),
                   per_b((1, LANES))],
        out_shape=[jax.ShapeDtypeStruct((batch * nc, tb, W_M), BF16),
                   jax.ShapeDtypeStruct((batch, CONV_W - 1, 2 * W_M), F32),
                   jax.ShapeDtypeStruct((batch, H_M, DH_M, DH_M), F32),
                   jax.ShapeDtypeStruct((batch, H_M, DH_M), F32),
                   jax.ShapeDtypeStruct((batch, 1, LANES), F32)],
        scratch_shapes=[pltpu.VMEM((8, 2 * W_M), F32), pltpu.VMEM((H_M, DH_M, DH_M), F32),
                        pltpu.VMEM((H_M, DH_M), F32), pltpu.VMEM((1, LANES), F32)],
        compiler_params=_params(("arbitrary", "arbitrary")),
        name="mlstm",
    )(chunks(qk), chunks(vm), chunks(om), chunks(gt), w_conv, b_conv, b_gates_pad, g_mnorm, conv_buf, c0, n0,
      m0_pad)
    return h.reshape(batch * seq, W_M), cst, c_out, n_out, m_out


def _merge_body(oa_ref, hm_ref, ga_ref, gb_ref, x_ref, gt1_ref, sc2_ref, sh2_ref, g2_ref, wa_ref, wb_ref, wo_ref,
                wr_ref, br_ref, *rest):
    x2_ref, h2_ref, route_ref, cnt_ref = rest[-4:]
    ya = _dot(oa_ref[...], wa_ref[...])
    yb = _dot(hm_ref[...], wb_ref[...])
    mix = _sigmoid(ga_ref[...]) * ya + _sigmoid(gb_ref[...]) * yb
    y = _dot(mix.astype(BF16), wo_ref[...])
    x2 = x_ref[...] + gt1_ref[...] * y
    x2_ref[...] = x2
    ms = jnp.mean(x2 * x2, axis=1, keepdims=True)
    h2 = (x2 * lax.rsqrt(ms + EPS)) * g2_ref[...] * (1.0 + sc2_ref[...]) + sh2_ref[...]
    h2b = h2.astype(BF16)
    h2_ref[...] = h2b

    tm = h2b.shape[0]
    logits_t = (_dot(h2b, wr_ref[...]) + br_ref[...]).T[:N_EXP, :]
    row = lax.broadcasted_iota(jnp.int32, (N_EXP, tm), 0)
    row_f = row.astype(F32)
    work = logits_t
    vals, hots = [], []
    for _ in range(TOP_K):
        mx = jnp.max(work, axis=0, keepdims=True)
        idx = jnp.min(jnp.where(work == mx, row, N_EXP), axis=0, keepdims=True)
        hot = row == idx
        vals.append(mx)
        hots.append(hot)
        work = jnp.where(hot, 2.0 * NEG_BIG, work)
    es = [jnp.exp(v - vals[0]) for v in vals]
    den = es[0]
    for e in es[1:]:
        den = den + e
    sel_t = jnp.zeros((N_EXP, tm), F32)
    for hot in hots:
        sel_t = jnp.where(hot, 1.0, sel_t)
    r_i = lax.broadcasted_iota(jnp.int32, (tm, tm), 0)
    c_i = lax.broadcasted_iota(jnp.int32, (tm, tm), 1)
    rank_t = _dot(sel_t.astype(BF16), jnp.where(r_i < c_i, 1.0, 0.0).astype(BF16))
    sub = lax.broadcasted_iota(jnp.int32, (ROUTE_SEL, tm), 0)
    head = jnp.zeros((ROUTE_SEL, tm), F32)
    for k in range(TOP_K):
        e_k = jnp.sum(jnp.where(hots[k], row_f, 0.0), axis=0, keepdims=True)
        r_k = jnp.sum(jnp.where(hots[k], rank_t, 0.0), axis=0, keepdims=True)
        head = jnp.where(sub == k, e_k, head)
        head = jnp.where(sub == TOP_K + k, es[k] / den, head)
        head = jnp.where(sub == 2 * TOP_K + k, r_k, head)
    pad = jnp.zeros((LANES - ROUTE_SEL - N_EXP, tm), F32)
    route = jnp.concatenate([head, sel_t, pad], axis=0).T
    route_ref[...] = route
    cnt_ref[...] = jnp.sum(route, axis=0, keepdims=True)


def _merge(oa, hm, ga, gb, x, gt1, sc2, sh2, g2, wa, wb, wo, wr, br, rows_per_mod, tile0, n_all, prev=None):
    n = x.shape[0]
    nt = n // TOK_TILE
    nt_all = n_all // TOK_TILE
    tiles_per_group = nt // gt1.shape[0]
    tok = lambda w: pl.BlockSpec((TOK_TILE, w), lambda i: (i, 0))
    mod = pl.BlockSpec((None, rows_per_mod, D_MODEL), lambda i: (i // tiles_per_group, 0, 0))
    res = lambda shape: pl.BlockSpec(shape, lambda i: (0, 0), pipeline_mode=pl.Buffered(1))
    out_tok = lambda w: pl.BlockSpec((TOK_TILE, w), lambda i: (tile0 + i, 0))
    in_specs = [tok(W_VA), tok(W_M), tok(D_MODEL), tok(D_MODEL), tok(D_MODEL), mod, mod, mod,
                pl.BlockSpec((1, D_MODEL), lambda i: (0, 0)),
                res((W_VA, D_MODEL)), res((W_M, D_MODEL)), res((D_MODEL, D_MODEL)), res((D_MODEL, LANES)),
                pl.BlockSpec((1, LANES), lambda i: (0, 0))]
    args = [oa, hm, ga, gb, x, gt1, sc2, sh2, g2, wa, wb, wo, wr, br]
    aliases = {}
    if prev is not None:
        in_specs += [pl.BlockSpec(memory_space=pl.ANY)] * 4
        aliases = {len(args) + j: j for j in range(4)}
        args += list(prev)
    return pl.pallas_call(
        _merge_body,
        grid=(nt,),
        in_specs=in_specs,
        out_specs=[out_tok(D_MODEL), out_tok(D_MODEL), out_tok(LANES),
                   pl.BlockSpec((None, 1, LANES), lambda i: (tile0 + i, 0, 0))],
        out_shape=[jax.ShapeDtypeStruct((n_all, D_MODEL), F32), jax.ShapeDtypeStruct((n_all, D_MODEL), BF16),
                   jax.ShapeDtypeStruct((n_all, LANES), F32), jax.ShapeDtypeStruct((nt_all, 1, LANES), F32)],
        input_output_aliases=aliases,
        compiler_params=_params(("arbitrary",)),
        name="merge",
    )(*args)


def _segment_copies(src, dst, sem, src_row, dst_row, n_groups, max_groups):
    out = []
    bit = 1
    while bit * 2 <= max_groups:
        bit *= 2
    while bit >= 1:
        off = (n_groups // (2 * bit)) * (2 * bit) * ROW_ALIGN
        rows = bit * ROW_ALIGN
        cp = pltpu.make_async_copy(src.at[pl.ds(pl.multiple_of(src_row + off, ROW_ALIGN), rows)],
                                   dst.at[pl.ds(pl.multiple_of(dst_row + off, ROW_ALIGN), rows)], sem)
        out.append(((n_groups // bit) % 2 == 1, cp))
        bit //= 2
    return out


def _run_copies(copies):
    for pred, cp in copies:
        pl.when(pred)(cp.start)
    for pred, cp in copies:
        pl.when(pred)(cp.wait)


def _slot_rows(route_t, loff_col, k):
    e_row = route_t[k:k + 1, :]
    r_row = route_t[2 * TOP_K + k:2 * TOP_K + k + 1, :]
    sub = lax.broadcasted_iota(jnp.int32, (LANES, route_t.shape[1]), 0).astype(F32)
    return jnp.sum(jnp.where(sub == e_row, loff_col, 0.0), axis=0, keepdims=True) + r_row


def _for_groups(n_groups, table_ref, base, make_copy, start):
    def body(i, carry):
        for j in range(GROUP_UNROLL):
            g = i * GROUP_UNROLL + j
            cp = make_copy(g, table_ref[base + g])
            if start:
                cp.start()
            else:
                cp.wait()
        return carry
    lax.fori_loop(0, (n_groups + GROUP_UNROLL - 1) // GROUP_UNROLL, body, 0)


def _dispatch_body(ng_ref, dt_ref, lo_ref, sv_ref, rm_ref, tail_ref, h2_ref, route_ref, soff_ref, xs_ref,
                   loc_scr, carry_scr, zero_scr, sem, *, nt):
    t = pl.program_id(0)
    buf = t % 2

    def copies_of(tt):
        def make(g, row):
            return pltpu.make_async_copy(
                loc_scr.at[tt % 2, pl.ds(pl.multiple_of(g * ROW_ALIGN, ROW_ALIGN), ROW_ALIGN)],
                xs_ref.at[pl.ds(pl.multiple_of(row, ROW_ALIGN), ROW_ALIGN)], sem.at[tt % 2])
        return ng_ref[tt], dt_ref, tt * GROUPS, make

    @pl.when(t == 0)
    def _():
        carry_scr[...] = jnp.zeros(carry_scr.shape, BF16)
        loc_scr[...] = jnp.zeros(loc_scr.shape, BF16)

    @pl.when(t >= 2)
    def _():
        _for_groups(*copies_of(t - 2), start=False)

    route_t = route_ref[...].T
    soff_col = soff_ref[...]
    slots = [_slot_rows(route_t, soff_col, k) for k in range(TOP_K)]

    def sort_rows(m):
        r_i = lax.broadcasted_iota(jnp.int32, (m, TOK_TILE), 0).astype(F32)
        onehot = jnp.zeros((m, TOK_TILE), F32)
        for k in range(TOP_K):
            onehot = jnp.where(r_i == slots[k], 1.0, onehot)
        loc_scr[buf, 0:m, :] = _dot(onehot.astype(BF16), h2_ref[...]).astype(BF16)

    fits = ng_ref[t] * ROW_ALIGN <= M_LOC_SHORT
    pl.when(fits)(lambda: sort_rows(M_LOC_SHORT))
    pl.when(jnp.logical_not(fits))(lambda: sort_rows(M_LOC))

    for e in range(N_EXP):
        lo = pl.multiple_of(lo_ref[t * N_EXP + e], ROW_ALIGN)
        sv = pl.multiple_of(sv_ref[t * N_EXP + e], ROW_ALIGN)
        loc_scr[buf, pl.ds(lo, ROW_ALIGN), :] = loc_scr[buf, pl.ds(lo, ROW_ALIGN), :] + carry_scr[e]
        pending = loc_scr[buf, pl.ds(sv, ROW_ALIGN), :]
        carry_scr[e] = jnp.where(rm_ref[t * N_EXP + e] > 0, pending, jnp.zeros_like(pending))

    _for_groups(*copies_of(t), start=True)

    @pl.when(t == nt - 1)
    def _():
        if nt >= 2:
            _for_groups(*copies_of(t - 1), start=False)
        _for_groups(*copies_of(t), start=False)
        zero_scr[...] = jnp.zeros(zero_scr.shape, BF16)
        tails = []
        for e in range(N_EXP):
            tails += _segment_copies(zero_scr, xs_ref, sem.at[0], 0, tail_ref[e], tail_ref[N_EXP + e],
                                     FFN_TILE // ROW_ALIGN - 1)
        _run_copies(tails)


def _dispatch(meta, h2, route, soff_col):
    nt = h2.shape[0] // TOK_TILE
    grid_spec = pltpu.PrefetchScalarGridSpec(
        num_scalar_prefetch=6,
        grid=(nt,),
        in_specs=[pl.BlockSpec((TOK_TILE, D_MODEL), lambda t, *_: (t, 0)),
                  pl.BlockSpec((TOK_TILE, LANES), lambda t, *_: (t, 0)),
                  pl.BlockSpec((None, LANES, 1), lambda t, *_: (t, 0, 0))],
        out_specs=pl.BlockSpec(memory_space=pl.ANY),
        scratch_shapes=[pltpu.VMEM((2, M_LOC + ROW_ALIGN, D_MODEL), BF16),
                        pltpu.VMEM((N_EXP, ROW_ALIGN, D_MODEL), BF16),
                        pltpu.VMEM((FFN_TILE, D_MODEL), BF16), pltpu.SemaphoreType.DMA((2,))],
    )
    return pl.pallas_call(
        functools.partial(_dispatch_body, nt=nt),
        grid_spec=grid_spec,
        out_shape=jax.ShapeDtypeStruct((meta["rows"] + 2 * M_LOC, D_MODEL), BF16),
        compiler_params=_params(("arbitrary",)),
        name="moe_dispatch",
    )(meta["n_groups"], meta["dispatch_rows"], meta["slot_start"], meta["carry_start"], meta["carry_rows"],
      meta["tail"], h2, route, soff_col)


def _ffn_body(be_ref, nu_ref, nx_ref, sl_ref, x_ref, bgu_ref, bd_ref, wgu_hbm, wd_hbm, y_ref,
              wgu_f32, wd_f32, wgu_scr, wd_scr, sem):
    i = pl.program_id(0)
    expert = be_ref[i]
    slot = sl_ref[i]
    prev = be_ref[jnp.maximum(i - 1, 0)]

    def weight_copies(e, s):
        return (pltpu.make_async_copy(wgu_hbm.at[e], wgu_f32.at[s], sem.at[s]),
                pltpu.make_async_copy(wd_hbm.at[e], wd_f32.at[s], sem.at[s]))

    @pl.when(i == 0)
    def _():
        for cp in weight_copies(expert, slot):
            cp.start()

    @pl.when(jnp.logical_and(i < nu_ref[0], jnp.logical_or(i == 0, expert != prev)))
    def _():
        for cp in weight_copies(expert, slot):
            cp.wait()
        wgu_scr[...] = wgu_f32[slot].astype(BF16)
        wd_scr[...] = wd_f32[slot].astype(BF16)

        @pl.when(nx_ref[i] >= 0)
        def _():
            for cp in weight_copies(nx_ref[i], 1 - slot):
                cp.start()

    @pl.when(i < nu_ref[0])
    def _():
        gu = _dot(x_ref[...], wgu_scr[...]) + bgu_ref[...]
        gate = jnp.minimum(gu[:, :D_FF], SWIGLU_LIMIT)
        up = jnp.clip(gu[:, D_FF:], -SWIGLU_LIMIT, SWIGLU_LIMIT)
        act = (up + 1.0) * gate * _sigmoid(SWIGLU_ALPHA * gate)
        y_ref[...] = (_dot(act.astype(BF16), wd_scr[...]) + bd_ref[...]).astype(BF16)


def _ffn(meta, xs, w_gu, b_gu, w_down, b_down):
    rows = meta["rows"]
    nblk = rows // FFN_TILE
    row_blk = pl.BlockSpec((FFN_TILE, D_MODEL), lambda i, be, nu, *_: (jnp.minimum(i, nu[0] - 1), 0))
    grid_spec = pltpu.PrefetchScalarGridSpec(
        num_scalar_prefetch=4,
        grid=(nblk,),
        in_specs=[row_blk,
                  pl.BlockSpec((None, 1, 2 * D_FF), lambda i, be, *_: (be[i], 0, 0)),
                  pl.BlockSpec((None, 1, D_MODEL), lambda i, be, *_: (be[i], 0, 0)),
                  pl.BlockSpec(memory_space=pl.ANY), pl.BlockSpec(memory_space=pl.ANY)],
        out_specs=row_blk,
        scratch_shapes=[pltpu.VMEM((2, D_MODEL, 2 * D_FF), F32), pltpu.VMEM((2, D_FF, D_MODEL), F32),
                        pltpu.VMEM((D_MODEL, 2 * D_FF), BF16), pltpu.VMEM((D_FF, D_MODEL), BF16),
                        pltpu.SemaphoreType.DMA((2,))],
    )
    return pl.pallas_call(
        _ffn_body,
        grid_spec=grid_spec,
        out_shape=jax.ShapeDtypeStruct((rows, D_MODEL), BF16),
        compiler_params=_params(("arbitrary",)),
        name="moe_ffn",
    )(meta["blk_exp"], meta["n_used"], meta["next_exp"], meta["blk_slot"], xs,
      b_gu.reshape(N_EXP, 1, 2 * D_FF), b_down.reshape(N_EXP, 1, D_MODEL), w_gu, w_down)


def _combine_body(ng_ref, ct_ref, route_ref, soff_ref, x2_ref, gtp_ref, gts_ref, gf_ref, ys_ref,
                  yp_ref, ysm_ref, loc_scr, moe_scr, sem, *, nt, nt_prompt, final):
    t = pl.program_id(0)
    buf = t % 2

    def copies_of(tt):
        def make(g, row):
            return pltpu.make_async_copy(
                ys_ref.at[pl.ds(pl.multiple_of(row, ROW_ALIGN), ROW_ALIGN)],
                loc_scr.at[tt % 2, pl.ds(pl.multiple_of(g * ROW_ALIGN, ROW_ALIGN), ROW_ALIGN)], sem.at[tt % 2])
        return ng_ref[tt], ct_ref, tt * GROUPS, make

    @pl.when(t == 0)
    def _():
        loc_scr[...] = jnp.zeros(loc_scr.shape, BF16)
        _for_groups(*copies_of(0), start=True)

    @pl.when(t + 1 < nt)
    def _():
        _for_groups(*copies_of(t + 1), start=True)

    _for_groups(*copies_of(t), start=False)

    route = route_ref[...]
    soff_row = soff_ref[...]
    lane = lax.broadcasted_iota(jnp.int32, (TOK_TILE, LANES), 1).astype(F32)
    slots = [jnp.sum(jnp.where(lane == route[:, k:k + 1], soff_row, 0.0), axis=1, keepdims=True)
             + route[:, 2 * TOP_K + k:2 * TOP_K + k + 1] for k in range(TOP_K)]

    def weighted_sum(m):
        c_i = lax.broadcasted_iota(jnp.int32, (TOK_TILE, m), 1).astype(F32)
        wmat = jnp.zeros((TOK_TILE, m), F32)
        for k in range(TOP_K):
            wmat = jnp.where(c_i == slots[k], route[:, TOP_K + k:TOP_K + k + 1], wmat)
        moe_scr[...] = _dot(wmat.astype(BF16), loc_scr[buf, 0:m, :])

    fits = ng_ref[t] * ROW_ALIGN <= M_LOC_SHORT
    pl.when(fits)(lambda: weighted_sum(M_LOC_SHORT))
    pl.when(jnp.logical_not(fits))(lambda: weighted_sum(M_LOC))

    gate = jnp.where(t >= nt_prompt, gts_ref[...], gtp_ref[...])
    xo = x2_ref[...] + gate * moe_scr[...]
    if final:
        ms = jnp.mean(xo * xo, axis=1, keepdims=True)
        xo = (xo * lax.rsqrt(ms + EPS)) * gf_ref[...]

    @pl.when(t < nt_prompt)
    def _():
        yp_ref[...] = xo

    @pl.when(t >= nt_prompt)
    def _():
        ysm_ref[...] = xo


def _combine(meta, route, soff_row, x2, gt2_p, gt2_s, g_final, ys, n_prompt, n_sample, final):
    nt = x2.shape[0] // TOK_TILE
    nt_prompt = n_prompt // TOK_TILE
    tiles_per_batch = nt_prompt // gt2_p.shape[0]
    grid_spec = pltpu.PrefetchScalarGridSpec(
        num_scalar_prefetch=2,
        grid=(nt,),
        in_specs=[pl.BlockSpec((TOK_TILE, LANES), lambda t, *_: (t, 0)),
                  pl.BlockSpec((None, 1, LANES), lambda t, *_: (t, 0, 0)),
                  pl.BlockSpec((TOK_TILE, D_MODEL), lambda t, *_: (t, 0)),
                  pl.BlockSpec((None, 1, D_MODEL),
                               lambda t, *_: (jnp.minimum(t, nt_prompt - 1) // tiles_per_batch, 0, 0)),
                  pl.BlockSpec((TOK_TILE, D_MODEL), lambda t, *_: (0, 0)),
                  pl.BlockSpec((1, D_MODEL), lambda t, *_: (0, 0)),
                  pl.BlockSpec(memory_space=pl.ANY)],
        out_specs=[pl.BlockSpec((TOK_TILE, D_MODEL), lambda t, *_: (jnp.minimum(t, nt_prompt - 1), 0)),
                   pl.BlockSpec((TOK_TILE, D_MODEL), lambda t, *_: (0, 0))],
        scratch_shapes=[pltpu.VMEM((2, M_LOC, D_MODEL), BF16), pltpu.VMEM((TOK_TILE, D_MODEL), F32),
                        pltpu.SemaphoreType.DMA((2,))],
    )
    return pl.pallas_call(
        functools.partial(_combine_body, nt=nt, nt_prompt=nt_prompt, final=final),
        grid_spec=grid_spec,
        out_shape=[jax.ShapeDtypeStruct((n_prompt, D_MODEL), F32), jax.ShapeDtypeStruct((n_sample, D_MODEL), F32)],
        compiler_params=_params(("arbitrary",)),
        name="moe_combine",
    )(meta["n_groups"], meta["combine_rows"], route, soff_row, x2, gt2_p, gt2_s, g_final, ys)


def _moe_offsets(cnt):
    nt = cnt.shape[0]
    ra = ROW_ALIGN
    prefix = jnp.cumsum(cnt, axis=0) - cnt
    total = jnp.sum(cnt, axis=0)
    pending = prefix % ra
    used = pending + cnt
    seg = (used + ra - 1) // ra * ra
    lo = jnp.cumsum(seg, axis=1) - seg
    n_groups = jnp.sum(seg, axis=1) // ra
    gpad = (total + FFN_TILE - 1) // FFN_TILE * FFN_TILE
    gstart = jnp.cumsum(gpad) - gpad
    base = gstart[None, :] + prefix // ra * ra
    last = (jnp.arange(nt) == nt - 1)[:, None]
    n_write = jnp.where(last, seg // ra, used // ra)
    carry_start = lo + used // ra * ra
    carry_rows = jnp.where(last, 0, used % ra)
    g = jnp.arange(GROUPS)
    slot_end = (lo + seg) // ra
    owner = jnp.minimum(jnp.sum(g[None, :, None] >= slot_end[:, None, :], axis=2), N_EXP - 1)
    pick = lambda a: jnp.sum(jnp.where(owner[:, :, None] == jnp.arange(N_EXP), a[:, None, :], 0), axis=2)
    k = g[None, :] - pick(lo) // ra
    row = pick(base) + k * ra
    valid = g[None, :] < n_groups[:, None]
    rows = (nt * TOK_TILE * TOP_K + N_EXP * (FFN_TILE - 1) + FFN_TILE - 1) // FFN_TILE * FFN_TILE
    spare = rows + (jnp.arange(nt) % 2)[:, None] * M_LOC + g[None, :] * ra
    combine_rows = jnp.where(valid, row, 0)
    dispatch_rows = jnp.where(valid & (k < pick(n_write)), row, spare)

    nblk_e = gpad // FFN_TILE
    blk_end = jnp.cumsum(nblk_e)
    n_used = jnp.maximum(blk_end[-1], 1)
    blk = jnp.minimum(jnp.arange(rows // FFN_TILE, dtype=jnp.int32), n_used - 1)
    blk_exp = jnp.minimum(jnp.sum(blk[:, None] >= blk_end[None, :], axis=1), N_EXP - 1)
    experts = jnp.arange(N_EXP)
    following = lax.cummin(jnp.where(nblk_e > 0, experts, N_EXP), reverse=True)
    next_of = jnp.concatenate([following[1:], jnp.full((1,), N_EXP, following.dtype)])
    next_of = jnp.where(next_of >= N_EXP, -1, next_of)
    parity = (jnp.cumsum(nblk_e > 0) - 1) % 2
    pick_e = lambda a: jnp.sum(jnp.where(blk_exp[:, None] == experts[None, :], a[None, :], 0), axis=1)
    total16 = (total + ra - 1) // ra * ra
    tail = jnp.concatenate([gstart + total16, (gpad - total16) // ra])
    i32 = lambda a: a.astype(jnp.int32).reshape(-1)
    return dict(n_groups=i32(n_groups), dispatch_rows=i32(dispatch_rows), combine_rows=i32(combine_rows),
                slot_start=i32(lo), carry_start=i32(carry_start), carry_rows=i32(carry_rows), tail=i32(tail),
                blk_exp=i32(blk_exp), n_used=i32(n_used), next_exp=i32(pick_e(next_of)), blk_slot=i32(pick_e(parity)),
                slot_off=(lo + pending).astype(F32), rows=rows)


def _pack_w_in(w_in):
    pad = jnp.zeros((D_MODEL, LANES - N_GATE), w_in.dtype)
    g0 = C_GA
    return jnp.concatenate([w_in[:, :g0], w_in[:, g0 + N_GATE:], w_in[:, g0:g0 + N_GATE], pad],
                           axis=1).astype(BF16)


def _pad_lanes(a, value=0.0):
    return jnp.pad(a, [(0, 0)] * (a.ndim - 1) + [(0, LANES - a.shape[-1])], constant_values=value)


def kernel(x_prompt, x_sample, c_prompt, c_sample, cache_k, cache_v, state_conv, state_C, state_n, state_m, page_table, w_ada, b_ada, g_norm1, g_norm2, w_in, b_gates, lambda_q1, lambda_k1, lambda_q2, lambda_k2, g_subln, w_conv, b_conv, g_mnorm, w_up_a, w_up_b, w_out, w_router, b_router, w_gu, b_gu, w_down, b_down, g_final):
    B, S, D = x_prompt.shape
    Bd, Td, _ = x_sample.shape
    depth = w_in.shape[0]
    n_pool = cache_k.shape[1]
    past_len = page_table.shape[1] * PAGE_SIZE
    n_p, n_s = B * S, Bd * Td
    n_all = n_p + n_s
    assert D == D_MODEL and n_s == TOK_TILE and S % MLSTM_CHUNK == 0 and n_p % TOK_TILE == 0
    assert page_table.shape[1] % PAGES_PER_STEP == 0

    cos_p, sin_p = _rope_tables(np.arange(S))
    cos_s, sin_s = _rope_tables(np.tile(past_len + np.arange(Td), Bd))
    hp = x_prompt.reshape(n_p, D)
    hs = x_sample.reshape(n_s, D)
    c_all = jnp.concatenate([c_prompt, c_sample], axis=0)
    outs = [[] for _ in range(12)]

    for l in range(depth):
        lam_init = 0.8 - 0.6 * math.exp(-0.3 * l)
        mod = _ada(c_all, w_ada[l], b_ada[l])
        mods = [mod[:, j * D:(j + 1) * D] for j in range(6)]
        mp = [m[:B].reshape(B, 1, D) for m in mods]
        ms_ = [jnp.repeat(m[B:], Td, axis=0).reshape(1, n_s, D) for m in mods]
        w_packed = _pack_w_in(w_in[l])
        g1 = g_norm1[l].reshape(1, D)
        lam_vec = jnp.stack([lambda_q1[l], lambda_k1[l], lambda_q2[l], lambda_k2[l]])
        gsub = g_subln[l].reshape(1, DV_A)
        bg = _pad_lanes(b_gates[l].reshape(1, N_GATE))
        cw, cb = w_conv[l], b_conv[l].reshape(1, 2 * W_M)

        (q_p, _, kb_p, v_p, _, qk_p, vm_p, om_p, ga_p, gb_p, gt_p, vt_p, kt_p) = _inproj(
            hp, mp[1], mp[0], g1, cos_p, sin_p, w_packed, 1)
        (q_s, k_s, kb_s, v_s, vb_s, qk_s, vm_s, om_s, ga_s, gb_s, gt_s, _, _) = _inproj(
            hs, ms_[1], ms_[0], g1, cos_s, sin_s, w_packed, n_s)
        k_p = jnp.transpose(kt_p.reshape(B, H_A, 2, DK_A, S), (0, 4, 1, 2, 3))

        kt_pool = jnp.transpose(cache_k[l], (0, 2, 3, 4, 1)).reshape(n_pool, W_QA, PAGE_SIZE)
        v_pool = cache_v[l].reshape(n_pool, PAGE_SIZE * H_A, DV_A)
        oa_p, oa_s = _attention(q_p, kb_p, vt_p, q_s, kb_s, vb_s, kt_pool, v_pool, page_table, lam_vec, gsub,
                                B, S, lam_init, Td)

        zeros = lambda *shape: jnp.zeros(shape, F32)
        hm_p, cst_p, C_p, nn_p, m_p = _mlstm(qk_p, vm_p, om_p, gt_p, cw, cb, bg, g_mnorm[l],
                                             zeros(B, CONV_W - 1, 2 * W_M), zeros(B, H_M, DH_M, DH_M),
                                             zeros(B, H_M, DH_M), zeros(B, 1, LANES), B, S)
        hm_s, cst_s, C_s, nn_s, m_s = _mlstm(qk_s, vm_s, om_s, gt_s, cw, cb, bg, g_mnorm[l],
                                             state_conv[l], state_C[l], state_n[l],
                                             _pad_lanes(state_m[l]).reshape(Bd, 1, LANES), Bd, Td)

        wa, wb, wo = w_up_a[l].astype(BF16), w_up_b[l].astype(BF16), w_out[l].astype(BF16)
        wr = _pad_lanes(w_router[l]).astype(BF16)
        br = _pad_lanes(b_router[l].reshape(1, N_EXP))
        g2 = g_norm2[l].reshape(1, D)
        part = _merge(oa_p, hm_p, ga_p, gb_p, hp, mp[2], mp[4], mp[3], g2, wa, wb, wo, wr, br, 1, 0, n_all)
        x2, h2, route, cnt = _merge(oa_s, hm_s, ga_s, gb_s, hs, ms_[2], ms_[4], ms_[3], g2, wa, wb, wo, wr, br,
                                    n_s, n_p // TOK_TILE, n_all, prev=part)

        meta = _moe_offsets(jnp.round(cnt[:, 0, ROUTE_SEL:ROUTE_SEL + N_EXP]).astype(jnp.int32))
        soff = _pad_lanes(meta["slot_off"])
        xs = _dispatch(meta, h2, route, soff[:, :, None])
        ys = _ffn(meta, xs, w_gu[l], b_gu[l], w_down[l], b_down[l])
        final = l == depth - 1
        hp, hs = _combine(meta, route, soff[:, None, :], x2, mp[5], ms_[5][0], g_final.reshape(1, D), ys,
                          n_p, n_s, final)

        for j, a in enumerate([k_p, v_p.reshape(B, S, H_A, DV_A), cst_p, C_p, nn_p,
                               m_p[:, 0, :H_M],
                               k_s.reshape(Bd, Td, H_A, 2, DK_A), v_s.reshape(Bd, Td, H_A, DV_A), cst_s, C_s, nn_s,
                               m_s[:, 0, :H_M]]):
            outs[j].append(a)

    return (hp.reshape(B, S, D), hs.reshape(Bd, Td, D)) + tuple(jnp.stack(o) for o in outs)
```

```python
import functools
import math

import numpy as np
import jax
import jax.numpy as jnp
from jax import lax
from jax.experimental import pallas as pl
from jax.experimental.pallas import tpu as pltpu

F32 = jnp.float32
BF16 = jnp.bfloat16

D_MODEL = 1024
H_A = 4
DK_A = 64
DV_A = 2 * DK_A
ROPE_THETA = 10000.0
H_M = 4
DH_M = 128
CONV_W = 4
N_EXP = 32
TOP_K = 4
D_FF = D_MODEL
SWIGLU_LIMIT = 7.0
SWIGLU_ALPHA = 1.702
EPS = 1e-6
PAGE_SIZE = 128

W_QA = H_A * 2 * DK_A
W_VA = H_A * DV_A
W_M = H_M * DH_M
N_GATE = 2 * H_M

LANES = 128
ROW_ALIGN = 16
TOK_TILE = 256
FFN_TILE = 256
ATT_TILE = 256
ATT_HEADS = H_A
MLSTM_CHUNK = 256
PAGES_PER_STEP = 16
CHUNKS_PER_STEP = 2
NEG_BIG = -1e30
LOG2_E = math.log2(math.e)
VMEM_LIMIT = 56 * 1024 * 1024

C_QA, C_KA, C_VA, C_QK, C_VM, C_OM, C_GATES = 0, 512, 1024, 1536, 2560, 3072, 3584
M_LOC = ((TOK_TILE * TOP_K + 2 * N_EXP * (ROW_ALIGN - 1)) + 255) // 256 * 256
GROUPS = M_LOC // ROW_ALIGN
M_LOC_SHORT = TOK_TILE * TOP_K + N_EXP * ROW_ALIGN
ROUTE_SEL = 32
GROUP_UNROLL = 4


def _dot(a, b):
    return jnp.dot(a, b, preferred_element_type=F32)


def _dot_nt(a, b):
    return lax.dot_general(a, b, (((1,), (1,)), ((), ())), preferred_element_type=F32)


def _sigmoid(x):
    return 0.5 * jnp.tanh(0.5 * x) + 0.5


def _start_all(copies):
    for cp in copies:
        cp.start()


def _params(sem):
    return pltpu.CompilerParams(dimension_semantics=sem, vmem_limit_bytes=VMEM_LIMIT)


def _ada_body(c_ref, w_ref, b_ref, o_ref):
    c = c_ref[...]
    s = c * _sigmoid(c)
    s_hi = s.astype(BF16)
    s_lo = (s - s_hi.astype(F32)).astype(BF16)
    w = w_ref[...]
    w_hi = w.astype(BF16)
    w_lo = (w - w_hi.astype(F32)).astype(BF16)
    o_ref[...] = _dot(s_hi, w_hi) + _dot(s_lo, w_hi) + _dot(s_hi, w_lo) + b_ref[...]


def _ada(c_all, w_ada, b_ada):
    rows = c_all.shape[0]
    n_out = w_ada.shape[1]
    blk = D_MODEL
    return pl.pallas_call(
        _ada_body,
        grid=(n_out // blk,),
        in_specs=[pl.BlockSpec((rows, D_MODEL), lambda j: (0, 0)),
                  pl.BlockSpec((D_MODEL, blk), lambda j: (0, j)),
                  pl.BlockSpec((1, blk), lambda j: (0, j))],
        out_specs=pl.BlockSpec((rows, blk), lambda j: (0, j)),
        out_shape=jax.ShapeDtypeStruct((rows, n_out), F32),
        compiler_params=_params(("arbitrary",)),
        name="ada",
    )(c_all, w_ada, b_ada.reshape(1, n_out))


def _rope(z, cos, sin):
    lane = lax.broadcasted_iota(jnp.int32, (z.shape[0], LANES), 1)
    first_half = (lane % DK_A) < (DK_A // 2)
    out = []
    for h in range(H_A):
        xh = z[:, h * LANES:(h + 1) * LANES]
        partner = jnp.where(first_half, pltpu.roll(xh, LANES - DK_A // 2, 1), pltpu.roll(xh, DK_A // 2, 1))
        out.append(xh * cos + partner * sin)
    return jnp.concatenate(out, axis=1)


def _conv_silu(u, ext, cw_ref, cb_ref):
    rows = u.shape[0]
    full = jnp.concatenate([ext, u], axis=0)
    conv = cb_ref[...] + cw_ref[CONV_W - 1:CONV_W, :] * u
    for j in range(CONV_W - 1):
        conv = conv + cw_ref[j:j + 1, :] * pltpu.roll(full, CONV_W - 1 - j, 0)[8:8 + rows]
    return conv * _sigmoid(conv)


def _inproj_body(x_ref, sc_ref, sh_ref, g_ref, cos_ref, sin_ref, w_ref, wg_ref, wt_ref,
                 q_ref, k_ref, kb_ref, v_ref, vb_ref, qk_ref, vm_ref, om_ref, ga_ref, gb_ref, gt_ref, vt_ref, kt_ref):
    x = x_ref[...]
    ms = jnp.mean(x * x, axis=1, keepdims=True)
    h = (x * lax.rsqrt(ms + EPS)) * g_ref[...] * (1.0 + sc_ref[...]) + sh_ref[...]
    hb = h.astype(BF16)
    cos = cos_ref[...]
    sin = sin_ref[...]

    def seg(lo, n):
        return _dot(hb, w_ref[:, lo:lo + n])

    q = _rope(seg(C_QA, W_QA), cos, sin) * (DK_A ** -0.5 * LOG2_E)
    q_ref[...] = q.astype(BF16)
    k = _rope(seg(C_KA, W_QA), cos, sin)
    k_ref[...] = k
    kb_ref[...] = k.astype(BF16)
    kt_ref[...] = k.T
    v = seg(C_VA, W_VA)
    for h in range(H_A):
        v_ref[pl.ds(h, v.shape[0], stride=H_A), :] = v[:, h * DV_A:(h + 1) * DV_A]
    vb_ref[...] = v.astype(BF16)
    vt_ref[...] = v.T.astype(BF16)
    qk_ref[...] = seg(C_QK, 2 * W_M)
    vm_ref[...] = seg(C_VM, W_M).astype(BF16)
    om_ref[...] = seg(C_OM, W_M)
    ga_ref[...] = _dot(hb, wg_ref[:, :D_MODEL])
    gb_ref[...] = _dot(hb, wg_ref[:, D_MODEL:])
    gt_ref[...] = _dot(hb, wt_ref[...])


def _inproj(x, sc, sh, g1, cos, sin, w_parts, rows_per_mod):
    n = x.shape[0]
    nt = n // TOK_TILE
    tiles_per_group = nt // sc.shape[0]
    tab_tiles = cos.shape[0] // TOK_TILE
    tok = lambda w: pl.BlockSpec((TOK_TILE, w), lambda i: (i, 0))
    mod = pl.BlockSpec((None, rows_per_mod, D_MODEL), lambda i: (i // tiles_per_group, 0, 0))
    tab = pl.BlockSpec((TOK_TILE, LANES), lambda i: (i % tab_tiles, 0))
    tok_out = lambda w, dt: (tok(w), jax.ShapeDtypeStruct((n, w), dt))
    outs = [
        tok_out(W_QA, BF16),
        tok_out(W_QA, F32), tok_out(W_QA, BF16),
        (pl.BlockSpec((TOK_TILE * H_A, DV_A), lambda i: (i, 0)), jax.ShapeDtypeStruct((n * H_A, DV_A), F32)),
        tok_out(W_VA, BF16),
        tok_out(2 * W_M, F32), tok_out(W_M, BF16), tok_out(W_M, F32),
        tok_out(D_MODEL, F32), tok_out(D_MODEL, F32), tok_out(LANES, F32),
        (pl.BlockSpec((None, W_VA, TOK_TILE), lambda i: (i, 0, 0)), jax.ShapeDtypeStruct((nt, W_VA, TOK_TILE), BF16)),
        (pl.BlockSpec((None, W_QA, TOK_TILE), lambda i: (i // tab_tiles, 0, i % tab_tiles)),
         jax.ShapeDtypeStruct((nt // tab_tiles, W_QA, tab_tiles * TOK_TILE), F32)),
    ]
    return pl.pallas_call(
        _inproj_body,
        grid=(nt,),
        in_specs=[tok(D_MODEL), mod, mod, pl.BlockSpec((1, D_MODEL), lambda i: (0, 0)), tab, tab,
                  *[pl.BlockSpec(w.shape, lambda i: (0, 0), pipeline_mode=pl.Buffered(1)) for w in w_parts]],
        out_specs=[spec for spec, _ in outs],
        out_shape=[shape for _, shape in outs],
        compiler_params=_params(("arbitrary",)),
        name="inproj",
    )(x, sc, sh, g1, cos, sin, *w_parts)


def _rope_tables(pos):
    half = DK_A // 2
    inv = ROPE_THETA ** (-np.arange(half, dtype=np.float64) * 2.0 / DK_A)
    ang = np.asarray(pos, np.float64)[:, None] * inv[None, :]
    cos = np.cos(ang)
    sin = np.sin(ang)
    cos64 = np.concatenate([cos, cos], axis=1)
    sin64 = np.concatenate([-sin, sin], axis=1)
    return (jnp.asarray(np.tile(cos64, (1, LANES // DK_A)), F32),
            jnp.asarray(np.tile(sin64, (1, LANES // DK_A)), F32))


def _lambda_value(lam_ref, lam_init):
    lv = lam_ref[...]
    l1 = jnp.sum(lv[0:1, :] * lv[1:2, :], axis=1, keepdims=True)
    l2 = jnp.sum(lv[2:3, :] * lv[3:4, :], axis=1, keepdims=True)
    return jnp.exp(l1) - jnp.exp(l2) + lam_init


def _subln(o, g, lam_init):
    ms = jnp.mean(o * o, axis=1, keepdims=True)
    return (o * lax.rsqrt(ms + EPS)) * g * (1.0 - lam_init)


def _prompt_tile(q_ref, k_ref, vt_ref, lam, g_ref, o_ref, m_scr, acc_scr, i, lam_init):
    tq = ATT_TILE
    lane = lax.broadcasted_iota(jnp.int32, (tq, LANES), 1)
    qs = []
    for hh in range(ATT_HEADS):
        q = q_ref[:, hh * LANES:(hh + 1) * LANES]
        zero = jnp.zeros_like(q)
        qs.append(jnp.concatenate([jnp.where(lane < DK_A, q, zero), jnp.where(lane >= DK_A, q, zero)], axis=0))
    for hh in range(ATT_HEADS):
        m_scr[hh][...] = jnp.full(m_scr[hh].shape, NEG_BIG, F32)
        acc_scr[hh][...] = jnp.zeros(acc_scr[hh].shape, F32)
    ones = jnp.ones((ROW_ALIGN, tq), BF16)

    def scores(hh, j):
        start = pl.multiple_of(j * tq, tq)
        return _dot_nt(k_ref[pl.ds(start, tq), hh * LANES:(hh + 1) * LANES], qs[hh])

    def update_all(j, mask):
        sts = [scores(hh, j) for hh in range(ATT_HEADS)]
        if mask is not None:
            sts = [jnp.where(mask, st, NEG_BIG) for st in sts]
        pts, alphas = [], []
        for hh in range(ATT_HEADS):
            m_old = m_scr[hh][...]
            m_new = jnp.maximum(m_old, jnp.max(sts[hh], axis=0, keepdims=True))
            alphas.append(jnp.exp2(m_old - m_new))
            pts.append(jnp.exp2(sts[hh] - m_new).astype(BF16))
            m_scr[hh][...] = m_new
        for hh in range(ATT_HEADS):
            vt = jnp.concatenate([vt_ref[j, hh * LANES:(hh + 1) * LANES, :], ones], axis=0)
            acc_scr[hh][...] = alphas[hh] * acc_scr[hh][...] + _dot(vt, pts[hh])

    def off_diag(j, carry):
        update_all(j, None)
        return carry

    lax.fori_loop(0, i, off_diag, 0)
    key = lax.broadcasted_iota(jnp.int32, (tq, 2 * tq), 0)
    qry = lax.broadcasted_iota(jnp.int32, (tq, 2 * tq), 1) % tq
    update_all(i, key <= qry)
    for hh in range(ATT_HEADS):
        acc = acc_scr[hh][...]
        ot = acc[:DV_A] / acc[DV_A:DV_A + 1]
        at = ot[:, :tq] - lam * ot[:, tq:]
        ms = jnp.mean(at * at, axis=0, keepdims=True)
        at = (at * lax.rsqrt(ms + EPS)) * g_ref[...] * (1.0 - lam_init)
        o_ref[:, hh * LANES:(hh + 1) * LANES] = at.T.astype(BF16)


def _sample_chunks(pt_ref, q_ref, kn_ref, vn_ref, lam, g_ref, kpool_ref, vpool_ref, o_ref,
                   kbuf, vbuf, sem, m_scr, l_scr, acc_scr, step, parts, *, lam_init, n_chunks, n_seq, t_new):
    steps_per_seq = n_chunks // CHUNKS_PER_STEP
    b = step // steps_per_seq
    c0 = (step % steps_per_seq) * CHUNKS_PER_STEP
    opens = 0 in parts
    closes = CHUNKS_PER_STEP - 1 in parts

    def chunk_copies(bb, c, slot):
        out = []
        for j in range(PAGES_PER_STEP):
            page = pt_ref[bb, c * PAGES_PER_STEP + j]
            out.append(pltpu.make_async_copy(kpool_ref.at[page], kbuf.at[slot, j], sem.at[slot]))
            out.append(pltpu.make_async_copy(vpool_ref.at[page], vbuf.at[slot, j], sem.at[slot]))
        return out

    if opens:
        @pl.when(step == 0)
        def _():
            _start_all(chunk_copies(0, 0, 0) + chunk_copies(0, 1, 1))

    q = q_ref[...].astype(F32)
    qt = jnp.concatenate([q] * (2 * H_A), axis=0)
    row = lax.broadcasted_iota(jnp.int32, qt.shape, 0)
    col = lax.broadcasted_iota(jnp.int32, qt.shape, 1)
    qbd = jnp.where(col // DK_A == row // t_new, qt, 0.0).astype(BF16)

    if opens:
        @pl.when(c0 == 0)
        def _():
            m_scr[...] = jnp.full(m_scr.shape, NEG_BIG, F32)
            l_scr[...] = jnp.zeros(l_scr.shape, F32)
            acc_scr[...] = jnp.zeros(acc_scr.shape, F32)

    rows_h = 2 * t_new

    def update(s, v_of_head):
        m_old = m_scr[...]
        m_new = jnp.maximum(m_old, jnp.max(s, axis=1, keepdims=True))
        alpha = jnp.exp2(m_old - m_new)
        p = jnp.exp2(s - m_new)
        l_scr[...] = alpha * l_scr[...] + jnp.sum(p, axis=1, keepdims=True)
        pb = p.astype(BF16)
        pv = [_dot(pb[h * rows_h:(h + 1) * rows_h, :], v_of_head(h)) for h in range(H_A)]
        acc_scr[...] = alpha * acc_scr[...] + jnp.concatenate(pv, axis=0)
        m_scr[...] = m_new

    def chunk(c, slot):
        for cp in chunk_copies(b, c, slot):
            cp.wait()
        kt = jnp.concatenate([kbuf[slot, j].astype(BF16) for j in range(PAGES_PER_STEP)], axis=1)

        def cached_v(h):
            return jnp.concatenate([vbuf.at[slot, j][pl.ds(h, PAGE_SIZE, stride=H_A), :].astype(BF16)
                                    for j in range(PAGES_PER_STEP)], axis=0)

        update(_dot(qbd, kt), cached_v)

        @pl.when(c + 2 < n_chunks)
        def _():
            _start_all(chunk_copies(b, c + 2, slot))

        @pl.when(jnp.logical_and(c + 2 >= n_chunks, b + 1 < n_seq))
        def _():
            _start_all(chunk_copies(b + 1, c + 2 - n_chunks, slot))

    for j in parts:
        chunk(c0 + j, j % 2)

    def finish():
        zpad = jnp.zeros((PAGE_SIZE - t_new, W_QA), F32)
        kn = jnp.concatenate([kn_ref[...].astype(F32), zpad], axis=0).astype(BF16)
        vn = jnp.concatenate([vn_ref[...].astype(F32), zpad], axis=0).astype(BF16)
        s = _dot_nt(qbd, kn)
        row = lax.broadcasted_iota(jnp.int32, s.shape, 0) % t_new
        col = lax.broadcasted_iota(jnp.int32, s.shape, 1)
        update(jnp.where(col <= row, s, NEG_BIG), lambda h: vn[:, h * DV_A:(h + 1) * DV_A])
        o = acc_scr[...] / l_scr[...]
        outs = []
        for h in range(H_A):
            r0 = h * rows_h
            outs.append(_subln(o[r0:r0 + t_new] - lam * o[r0 + t_new:r0 + rows_h], g_ref[...], lam_init))
        o_ref[...] = jnp.concatenate(outs, axis=1).astype(BF16)

    if closes:
        pl.when(c0 + CHUNKS_PER_STEP == n_chunks)(finish)


def _attention_body(pt_ref, q_ref, k_ref, vt_ref, lam_ref, gcol_ref, qs_ref, kn_ref, vn_ref, grow_ref,
                    kpool_ref, vpool_ref, o_ref, os_ref, *scratch, lam_init, n_chunks, n_seq, t_new):
    m_scr, acc_scr = scratch[:ATT_HEADS], scratch[ATT_HEADS:2 * ATT_HEADS]
    kbuf, vbuf, sem, ms_scr, ls_scr, accs_scr = scratch[2 * ATT_HEADS:]
    step = pl.program_id(0) * pl.num_programs(1) + pl.program_id(1)
    lam = _lambda_value(lam_ref, lam_init)
    sample = functools.partial(_sample_chunks, pt_ref, qs_ref, kn_ref, vn_ref, lam, grow_ref, kpool_ref, vpool_ref,
                               os_ref, kbuf, vbuf, sem, ms_scr, ls_scr, accs_scr, step,
                               lam_init=lam_init, n_chunks=n_chunks, n_seq=n_seq, t_new=t_new)
    sample((0,))
    _prompt_tile(q_ref, k_ref, vt_ref, lam, gcol_ref, o_ref, m_scr, acc_scr, pl.program_id(1), lam_init)
    sample((1,))


def _attention(q, k, vt, q_s, k_new, v_new, cache_k, cache_v, page_table, lam_vec, g_subln, batch, seq,
               lam_init, t_new):
    nq = seq // ATT_TILE
    bd, n_pages = page_table.shape
    n_chunks = n_pages // PAGES_PER_STEP
    steps_per_seq = n_chunks // CHUNKS_PER_STEP
    assert ATT_HEADS == H_A and CHUNKS_PER_STEP % 2 == 0 and n_chunks % CHUNKS_PER_STEP == 0
    assert batch * nq == bd * steps_per_seq
    n_rows = 2 * H_A * t_new
    kv = pl.BlockSpec((seq, W_QA), lambda b, i, pt: (b, 0))
    vts = pl.BlockSpec((nq, W_VA, ATT_TILE), lambda b, i, pt: (b, 0, 0))
    qo = pl.BlockSpec((ATT_TILE, W_QA), lambda b, i, pt: (b * nq + i, 0))
    new = pl.BlockSpec((None, t_new, W_QA), lambda b, i, pt: ((b * nq + i) // steps_per_seq, 0, 0))
    const = lambda shape: pl.BlockSpec(shape, lambda b, i, pt: (0, 0))
    page_buf = pltpu.VMEM((2, PAGES_PER_STEP, W_QA, PAGE_SIZE), F32)
    grid_spec = pltpu.PrefetchScalarGridSpec(
        num_scalar_prefetch=1,
        grid=(batch, nq),
        in_specs=[qo, kv, vts, const((4, DK_A)), const((DV_A, 1)), new, new, new, const((1, DV_A)),
                  pl.BlockSpec(memory_space=pl.ANY), pl.BlockSpec(memory_space=pl.ANY)],
        out_specs=[qo, new],
        scratch_shapes=[pltpu.VMEM((1, 2 * ATT_TILE), F32)] * ATT_HEADS
                       + [pltpu.VMEM((DV_A + ROW_ALIGN, 2 * ATT_TILE), F32)] * ATT_HEADS
                       + [page_buf, page_buf, pltpu.SemaphoreType.DMA((2,)), pltpu.VMEM((n_rows, 1), F32),
                          pltpu.VMEM((n_rows, 1), F32), pltpu.VMEM((n_rows, DV_A), F32)],
    )
    o_p, o_s = pl.pallas_call(
        functools.partial(_attention_body, lam_init=lam_init, n_chunks=n_chunks, n_seq=bd, t_new=t_new),
        grid_spec=grid_spec,
        out_shape=[jax.ShapeDtypeStruct((batch * seq, W_VA), BF16), jax.ShapeDtypeStruct((bd, t_new, W_VA), BF16)],
        compiler_params=_params(("arbitrary", "arbitrary")),
        name="attention",
    )(page_table, q, k, vt, lam_vec, g_subln.reshape(DV_A, 1), q_s.reshape(bd, t_new, W_QA),
      k_new.reshape(bd, t_new, W_QA), v_new.reshape(bd, t_new, W_VA), g_subln, cache_k, cache_v)
    return o_p, o_s.reshape(bd * t_new, W_VA)


def _mlstm_body(qk_ref, vm_ref, om_ref, gt_ref, cw_ref, cb_ref, bg_ref, gm_ref, cbuf_ref, c0_ref, n0_ref, m0_ref,
                h_ref, cst_ref, cout_ref, nout_ref, mout_ref, ext_scr, c_scr, n_scr, m_scr, *, tb, L, nc):
    c_idx = pl.program_id(1)

    @pl.when(c_idx == 0)
    def _():
        ext_scr[...] = jnp.zeros(ext_scr.shape, F32)
        ext_scr[8 - (CONV_W - 1):8, :] = cbuf_ref[...]
        c_scr[...] = c0_ref[...]
        n_scr[...] = n0_ref[...]
        m_scr[...] = m0_ref[...]

    pad = L - tb
    u = qk_ref[...]
    if pad:
        u = jnp.concatenate([u, jnp.zeros((pad, u.shape[1]), F32)], axis=0)
    a = _conv_silu(u, ext_scr[...], cw_ref, cb_ref)
    if not pad:
        ext_scr[...] = u[L - 8:L]

    @pl.when(c_idx == nc - 1)
    def _():
        cst_ref[...] = qk_ref[tb - (CONV_W - 1):tb, :]

    g = gt_ref[...] + bg_ref[...]
    li = g
    lf = jnp.minimum(g, 0.0) - jnp.log1p(jnp.exp(-jnp.abs(g)))
    if pad:
        zpad = jnp.zeros((pad, LANES), F32)
        li = jnp.concatenate([li, zpad + NEG_BIG], axis=0)
        lf = jnp.concatenate([lf, zpad], axis=0)
    row = lax.broadcasted_iota(jnp.int32, (L, LANES), 0)
    lane = lax.broadcasted_iota(jnp.int32, (L, LANES), 1)
    bcum = lf
    shift = 1
    while shift < L:
        bcum = bcum + jnp.where(row >= shift, pltpu.roll(bcum, shift, 0), 0.0)
        shift *= 2
    gates = jnp.where(lane < H_M, li, bcum)
    gates_t = gates.T
    tri = lax.broadcasted_iota(jnp.int32, (L, L), 0) >= lax.broadcasted_iota(jnp.int32, (L, L), 1)
    m_all = m_scr[...]
    lane1 = lax.broadcasted_iota(jnp.int32, (1, LANES), 1)
    m_next = m_all
    vall = vm_ref[...]
    if pad:
        vall = jnp.concatenate([vall, jnp.zeros((pad, vall.shape[1]), BF16)], axis=0)

    for h in range(H_M):
        li_col = gates[:, h:h + 1]
        b_col = gates[:, H_M + h:H_M + h + 1]
        li_row = gates_t[h:h + 1, :]
        b_row = gates_t[H_M + h:H_M + h + 1, :]
        m_prev = m_all[:, h:h + 1]
        b_last = b_col[L - 1:L, :]
        log_d = jnp.where(tri, b_col - b_row + li_row, NEG_BIG)
        inter = b_col + m_prev
        mt = jnp.maximum(inter, jnp.max(log_d, axis=1, keepdims=True))
        q = a[:, h * DH_M:(h + 1) * DH_M]
        k = a[:, W_M + h * DH_M:W_M + (h + 1) * DH_M] * (DH_M ** -0.5)
        v = vall[:, h * DH_M:(h + 1) * DH_M]
        qb = q.astype(BF16)
        s = _dot_nt(qb, k.astype(BF16)) * jnp.exp(log_d - mt)
        ei = jnp.exp(inter - mt)
        c_old = c_scr[h]
        n_old = n_scr[h:h + 1, :]
        num = ei * _dot(qb, c_old.astype(BF16)) + _dot(s.astype(BF16), v)
        den = ei * jnp.sum(q * n_old, axis=1, keepdims=True) + jnp.sum(s, axis=1, keepdims=True)
        hh = num / jnp.maximum(jnp.abs(den), jnp.exp(-mt))
        g_col = b_last - b_col + li_col
        bl = b_last + m_prev
        m_new = jnp.maximum(bl, jnp.max(g_col, axis=0, keepdims=True))
        wg = jnp.exp(g_col - m_new)
        decay = jnp.exp(bl - m_new)
        kw = k * wg
        c_scr[h] = decay * c_old + _dot(kw.T.astype(BF16), v)
        n_scr[h:h + 1, :] = decay * n_old + jnp.sum(kw, axis=0, keepdims=True)
        m_next = jnp.where(lane1 == h, m_new, m_next)
        ms = jnp.mean(hh * hh, axis=1, keepdims=True)
        hn = (hh * lax.rsqrt(ms + EPS)) * gm_ref[h:h + 1, :]
        og = _sigmoid(om_ref[:, h * DH_M:(h + 1) * DH_M])
        h_ref[:, h * DH_M:(h + 1) * DH_M] = (hn[:tb] * og).astype(BF16)

    m_scr[...] = m_next

    @pl.when(c_idx == nc - 1)
    def _():
        cout_ref[...] = c_scr[...]
        nout_ref[...] = n_scr[...]
        mout_ref[...] = m_scr[...]


def _mlstm(qk, vm, om, gt, w_conv, b_conv, b_gates_pad, g_mnorm, conv_buf, c0, n0, m0_pad, batch, seq):
    tb = min(seq, MLSTM_CHUNK)
    L = max(tb, LANES)
    nc = seq // tb
    tok = lambda w: pl.BlockSpec((None, tb, w), lambda b, c: (b * nc + c, 0, 0))
    chunks = lambda a: a.reshape(batch * nc, tb, a.shape[-1])
    const = lambda shape: pl.BlockSpec(shape, lambda b, c: (0,) * len(shape))
    per_b = lambda shape: pl.BlockSpec((None,) + shape, lambda b, c: (b,) + (0,) * len(shape))
    h, cst, c_out, n_out, m_out = pl.pallas_call(
        functools.partial(_mlstm_body, tb=tb, L=L, nc=nc),
        grid=(batch, nc),
        in_specs=[tok(2 * W_M), tok(W_M), tok(W_M), tok(LANES), const((CONV_W, 2 * W_M)), const((1, 2 * W_M)),
                  const((1, LANES)), const((H_M, DH_M)), per_b((CONV_W - 1, 2 * W_M)),
                  per_b((H_M, DH_M, DH_M)), per_b((H_M, DH_M)), per_b((1, LANES))],
        out_specs=[tok(W_M), per_b((CONV_W - 1, 2 * W_M)), per_b((H_M, DH_M, DH_M)), per_b((H_M, DH_M)),
                   per_b((1, LANES))],
        out_shape=[jax.ShapeDtypeStruct((batch * nc, tb, W_M), BF16),
                   jax.ShapeDtypeStruct((batch, CONV_W - 1, 2 * W_M), F32),
                   jax.ShapeDtypeStruct((batch, H_M, DH_M, DH_M), F32),
                   jax.ShapeDtypeStruct((batch, H_M, DH_M), F32),
                   jax.ShapeDtypeStruct((batch, 1, LANES), F32)],
        scratch_shapes=[pltpu.VMEM((8, 2 * W_M), F32), pltpu.VMEM((H_M, DH_M, DH_M), F32),
                        pltpu.VMEM((H_M, DH_M), F32), pltpu.VMEM((1, LANES), F32)],
        compiler_params=_params(("arbitrary", "arbitrary")),
        name="mlstm",
    )(chunks(qk), chunks(vm), chunks(om), chunks(gt), w_conv, b_conv, b_gates_pad, g_mnorm, conv_buf, c0, n0,
      m0_pad)
    return h.reshape(batch * seq, W_M), cst, c_out, n_out, m_out


def _merge_body(oa_ref, hm_ref, ga_ref, gb_ref, x_ref, gt1_ref, sc2_ref, sh2_ref, g2_ref, wa_ref, wb_ref, wo_ref,
                wr_ref, br_ref, *rest):
    x2_ref, h2_ref, route_ref, cnt_ref = rest[-4:]
    ya = _dot(oa_ref[...], wa_ref[...])
    yb = _dot(hm_ref[...], wb_ref[...])
    mix = _sigmoid(ga_ref[...]) * ya + _sigmoid(gb_ref[...]) * yb
    y = _dot(mix.astype(BF16), wo_ref[...])
    x2 = x_ref[...] + gt1_ref[...] * y
    x2_ref[...] = x2
    ms = jnp.mean(x2 * x2, axis=1, keepdims=True)
    h2 = (x2 * lax.rsqrt(ms + EPS)) * g2_ref[...] * (1.0 + sc2_ref[...]) + sh2_ref[...]
    h2b = h2.astype(BF16)
    h2_ref[...] = h2b

    tm = h2b.shape[0]
    logits_t = (_dot(h2b, wr_ref[...]) + br_ref[...]).T[:N_EXP, :]
    row = lax.broadcasted_iota(jnp.int32, (N_EXP, tm), 0)
    row_f = row.astype(F32)
    work = logits_t
    vals, hots = [], []
    for _ in range(TOP_K):
        mx = jnp.max(work, axis=0, keepdims=True)
        idx = jnp.min(jnp.where(work == mx, row, N_EXP), axis=0, keepdims=True)
        hot = row == idx
        vals.append(mx)
        hots.append(hot)
        work = jnp.where(hot, 2.0 * NEG_BIG, work)
    es = [jnp.exp(v - vals[0]) for v in vals]
    den = es[0]
    for e in es[1:]:
        den = den + e
    sel_t = jnp.zeros((N_EXP, tm), F32)
    for hot in hots:
        sel_t = jnp.where(hot, 1.0, sel_t)
    r_i = lax.broadcasted_iota(jnp.int32, (tm, tm), 0)
    c_i = lax.broadcasted_iota(jnp.int32, (tm, tm), 1)
    rank_t = _dot(sel_t.astype(BF16), jnp.where(r_i < c_i, 1.0, 0.0).astype(BF16))
    sub = lax.broadcasted_iota(jnp.int32, (ROUTE_SEL, tm), 0)
    head = jnp.zeros((ROUTE_SEL, tm), F32)
    for k in range(TOP_K):
        e_k = jnp.sum(jnp.where(hots[k], row_f, 0.0), axis=0, keepdims=True)
        r_k = jnp.sum(jnp.where(hots[k], rank_t, 0.0), axis=0, keepdims=True)
        head = jnp.where(sub == k, e_k, head)
        head = jnp.where(sub == TOP_K + k, es[k] / den, head)
        head = jnp.where(sub == 2 * TOP_K + k, r_k, head)
    pad = jnp.zeros((LANES - ROUTE_SEL - N_EXP, tm), F32)
    route = jnp.concatenate([head, sel_t, pad], axis=0).T
    route_ref[...] = route
    cnt_ref[...] = jnp.sum(route, axis=0, keepdims=True)


def _merge(oa, hm, ga, gb, x, gt1, sc2, sh2, g2, wa, wb, wo, wr, br, rows_per_mod, tile0, n_all, prev=None):
    n = x.shape[0]
    nt = n // TOK_TILE
    nt_all = n_all // TOK_TILE
    tiles_per_group = nt // gt1.shape[0]
    tok = lambda w: pl.BlockSpec((TOK_TILE, w), lambda i: (i, 0))
    mod = pl.BlockSpec((None, rows_per_mod, D_MODEL), lambda i: (i // tiles_per_group, 0, 0))
    res = lambda shape: pl.BlockSpec(shape, lambda i: (0, 0), pipeline_mode=pl.Buffered(1))
    out_tok = lambda w: pl.BlockSpec((TOK_TILE, w), lambda i: (tile0 + i, 0))
    in_specs = [tok(W_VA), tok(W_M), tok(D_MODEL), tok(D_MODEL), tok(D_MODEL), mod, mod, mod,
                pl.BlockSpec((1, D_MODEL), lambda i: (0, 0)),
                res((W_VA, D_MODEL)), res((W_M, D_MODEL)), res((D_MODEL, D_MODEL)), res((D_MODEL, LANES)),
                pl.BlockSpec((1, LANES), lambda i: (0, 0))]
    args = [oa, hm, ga, gb, x, gt1, sc2, sh2, g2, wa, wb, wo, wr, br]
    aliases = {}
    if prev is not None:
        in_specs += [pl.BlockSpec(memory_space=pl.ANY)] * 4
        aliases = {len(args) + j: j for j in range(4)}
        args += list(prev)
    return pl.pallas_call(
        _merge_body,
        grid=(nt,),
        in_specs=in_specs,
        out_specs=[out_tok(D_MODEL), out_tok(D_MODEL), out_tok(LANES),
                   pl.BlockSpec((None, 1, LANES), lambda i: (tile0 + i, 0, 0))],
        out_shape=[jax.ShapeDtypeStruct((n_all, D_MODEL), F32), jax.ShapeDtypeStruct((n_all, D_MODEL), BF16),
                   jax.ShapeDtypeStruct((n_all, LANES), F32), jax.ShapeDtypeStruct((nt_all, 1, LANES), F32)],
        input_output_aliases=aliases,
        compiler_params=_params(("arbitrary",)),
        name="merge",
    )(*args)


def _segment_copies(src, dst, sem, src_row, dst_row, n_groups, max_groups):
    out = []
    bit = 1
    while bit * 2 <= max_groups:
        bit *= 2
    while bit >= 1:
        off = (n_groups // (2 * bit)) * (2 * bit) * ROW_ALIGN
        rows = bit * ROW_ALIGN
        cp = pltpu.make_async_copy(src.at[pl.ds(pl.multiple_of(src_row + off, ROW_ALIGN), rows)],
                                   dst.at[pl.ds(pl.multiple_of(dst_row + off, ROW_ALIGN), rows)], sem)
        out.append(((n_groups // bit) % 2 == 1, cp))
        bit //= 2
    return out


def _run_copies(copies):
    for pred, cp in copies:
        pl.when(pred)(cp.start)
    for pred, cp in copies:
        pl.when(pred)(cp.wait)


def _slot_rows(route_t, loff_col, k):
    e_row = route_t[k:k + 1, :]
    r_row = route_t[2 * TOP_K + k:2 * TOP_K + k + 1, :]
    sub = lax.broadcasted_iota(jnp.int32, (LANES, route_t.shape[1]), 0).astype(F32)
    return jnp.sum(jnp.where(sub == e_row, loff_col, 0.0), axis=0, keepdims=True) + r_row


def _for_groups(n_groups, table_ref, base, make_copy, start):
    def body(i, carry):
        for j in range(GROUP_UNROLL):
            g = i * GROUP_UNROLL + j
            cp = make_copy(g, table_ref[base + g])
            if start:
                cp.start()
            else:
                cp.wait()
        return carry
    lax.fori_loop(0, (n_groups + GROUP_UNROLL - 1) // GROUP_UNROLL, body, 0)


def _dispatch_body(ng_ref, dt_ref, lo_ref, sv_ref, rm_ref, tail_ref, h2_ref, route_ref, soff_ref, xs_ref,
                   loc_scr, carry_scr, zero_scr, sem, *, nt):
    t = pl.program_id(0)
    buf = t % 2

    def copies_of(tt):
        def make(g, row):
            return pltpu.make_async_copy(
                loc_scr.at[tt % 2, pl.ds(pl.multiple_of(g * ROW_ALIGN, ROW_ALIGN), ROW_ALIGN)],
                xs_ref.at[pl.ds(pl.multiple_of(row, ROW_ALIGN), ROW_ALIGN)], sem.at[tt % 2])
        return ng_ref[tt], dt_ref, tt * GROUPS, make

    @pl.when(t == 0)
    def _():
        carry_scr[...] = jnp.zeros(carry_scr.shape, BF16)
        loc_scr[...] = jnp.zeros(loc_scr.shape, BF16)

    @pl.when(t >= 2)
    def _():
        _for_groups(*copies_of(t - 2), start=False)

    route_t = route_ref[...].T
    soff_col = soff_ref[...]
    slots = [_slot_rows(route_t, soff_col, k) for k in range(TOP_K)]

    def sort_rows(m):
        r_i = lax.broadcasted_iota(jnp.int32, (m, TOK_TILE), 0).astype(F32)
        onehot = jnp.zeros((m, TOK_TILE), F32)
        for k in range(TOP_K):
            onehot = jnp.where(r_i == slots[k], 1.0, onehot)
        loc_scr[buf, 0:m, :] = _dot(onehot.astype(BF16), h2_ref[...]).astype(BF16)

    fits = ng_ref[t] * ROW_ALIGN <= M_LOC_SHORT
    pl.when(fits)(lambda: sort_rows(M_LOC_SHORT))
    pl.when(jnp.logical_not(fits))(lambda: sort_rows(M_LOC))

    for e in range(N_EXP):
        lo = pl.multiple_of(lo_ref[t * N_EXP + e], ROW_ALIGN)
        sv = pl.multiple_of(sv_ref[t * N_EXP + e], ROW_ALIGN)
        loc_scr[buf, pl.ds(lo, ROW_ALIGN), :] = loc_scr[buf, pl.ds(lo, ROW_ALIGN), :] + carry_scr[e]
        pending = loc_scr[buf, pl.ds(sv, ROW_ALIGN), :]
        carry_scr[e] = jnp.where(rm_ref[t * N_EXP + e] > 0, pending, jnp.zeros_like(pending))

    _for_groups(*copies_of(t), start=True)

    @pl.when(t == nt - 1)
    def _():
        if nt >= 2:
            _for_groups(*copies_of(t - 1), start=False)
        _for_groups(*copies_of(t), start=False)
        zero_scr[...] = jnp.zeros(zero_scr.shape, BF16)
        tails = []
        for e in range(N_EXP):
            tails += _segment_copies(zero_scr, xs_ref, sem.at[0], 0, tail_ref[e], tail_ref[N_EXP + e],
                                     FFN_TILE // ROW_ALIGN - 1)
        _run_copies(tails)


def _dispatch(meta, h2, route, soff_col):
    nt = h2.shape[0] // TOK_TILE
    grid_spec = pltpu.PrefetchScalarGridSpec(
        num_scalar_prefetch=6,
        grid=(nt,),
        in_specs=[pl.BlockSpec((TOK_TILE, D_MODEL), lambda t, *_: (t, 0)),
                  pl.BlockSpec((TOK_TILE, LANES), lambda t, *_: (t, 0)),
                  pl.BlockSpec((None, LANES, 1), lambda t, *_: (t, 0, 0))],
        out_specs=pl.BlockSpec(memory_space=pl.ANY),
        scratch_shapes=[pltpu.VMEM((2, M_LOC + ROW_ALIGN, D_MODEL), BF16),
                        pltpu.VMEM((N_EXP, ROW_ALIGN, D_MODEL), BF16),
                        pltpu.VMEM((FFN_TILE, D_MODEL), BF16), pltpu.SemaphoreType.DMA((2,))],
    )
    return pl.pallas_call(
        functools.partial(_dispatch_body, nt=nt),
        grid_spec=grid_spec,
        out_shape=jax.ShapeDtypeStruct((meta["rows"] + 2 * M_LOC, D_MODEL), BF16),
        compiler_params=_params(("arbitrary",)),
        name="moe_dispatch",
    )(meta["n_groups"], meta["dispatch_rows"], meta["slot_start"], meta["carry_start"], meta["carry_rows"],
      meta["tail"], h2, route, soff_col)


def _ffn_body(be_ref, nu_ref, nx_ref, sl_ref, x_ref, bgu_ref, bd_ref, wgu_hbm, wd_hbm, y_ref,
              wgu_f32, wd_f32, wgu_scr, wd_scr, sem):
    i = pl.program_id(0)
    expert = be_ref[i]
    slot = sl_ref[i]
    prev = be_ref[jnp.maximum(i - 1, 0)]

    def weight_copies(e, s):
        return (pltpu.make_async_copy(wgu_hbm.at[e], wgu_f32.at[s], sem.at[s]),
                pltpu.make_async_copy(wd_hbm.at[e], wd_f32.at[s], sem.at[s]))

    @pl.when(i == 0)
    def _():
        for cp in weight_copies(expert, slot):
            cp.start()

    @pl.when(jnp.logical_and(i < nu_ref[0], jnp.logical_or(i == 0, expert != prev)))
    def _():
        for cp in weight_copies(expert, slot):
            cp.wait()
        wgu_scr[...] = wgu_f32[slot].astype(BF16)
        wd_scr[...] = wd_f32[slot].astype(BF16)

        @pl.when(nx_ref[i] >= 0)
        def _():
            for cp in weight_copies(nx_ref[i], 1 - slot):
                cp.start()

    @pl.when(i < nu_ref[0])
    def _():
        gu = _dot(x_ref[...], wgu_scr[...]) + bgu_ref[...]
        gate = jnp.minimum(gu[:, :D_FF], SWIGLU_LIMIT)
        up = jnp.clip(gu[:, D_FF:], -SWIGLU_LIMIT, SWIGLU_LIMIT)
        act = (up + 1.0) * gate * _sigmoid(SWIGLU_ALPHA * gate)
        y_ref[...] = (_dot(act.astype(BF16), wd_scr[...]) + bd_ref[...]).astype(BF16)


def _ffn(meta, xs, w_gu, b_gu, w_down, b_down):
    rows = meta["rows"]
    nblk = rows // FFN_TILE
    row_blk = pl.BlockSpec((FFN_TILE, D_MODEL), lambda i, be, nu, *_: (jnp.minimum(i, nu[0] - 1), 0))
    grid_spec = pltpu.PrefetchScalarGridSpec(
        num_scalar_prefetch=4,
        grid=(nblk,),
        in_specs=[row_blk,
                  pl.BlockSpec((None, 1, 2 * D_FF), lambda i, be, *_: (be[i], 0, 0)),
                  pl.BlockSpec((None, 1, D_MODEL), lambda i, be, *_: (be[i], 0, 0)),
                  pl.BlockSpec(memory_space=pl.ANY), pl.BlockSpec(memory_space=pl.ANY)],
        out_specs=row_blk,
        scratch_shapes=[pltpu.VMEM((2, D_MODEL, 2 * D_FF), F32), pltpu.VMEM((2, D_FF, D_MODEL), F32),
                        pltpu.VMEM((D_MODEL, 2 * D_FF), BF16), pltpu.VMEM((D_FF, D_MODEL), BF16),
                        pltpu.SemaphoreType.DMA((2,))],
    )
    return pl.pallas_call(
        _ffn_body,
        grid_spec=grid_spec,
        out_shape=jax.ShapeDtypeStruct((rows, D_MODEL), BF16),
        compiler_params=_params(("arbitrary",)),
        name="moe_ffn",
    )(meta["blk_exp"], meta["n_used"], meta["next_exp"], meta["blk_slot"], xs,
      b_gu.reshape(N_EXP, 1, 2 * D_FF), b_down.reshape(N_EXP, 1, D_MODEL), w_gu, w_down)


def _combine_body(ng_ref, ct_ref, route_ref, soff_ref, x2_ref, gtp_ref, gts_ref, gf_ref, ys_ref,
                  yp_ref, ysm_ref, loc_scr, moe_scr, sem, *, nt, nt_prompt, final):
    t = pl.program_id(0)
    buf = t % 2

    def copies_of(tt):
        def make(g, row):
            return pltpu.make_async_copy(
                ys_ref.at[pl.ds(pl.multiple_of(row, ROW_ALIGN), ROW_ALIGN)],
                loc_scr.at[tt % 2, pl.ds(pl.multiple_of(g * ROW_ALIGN, ROW_ALIGN), ROW_ALIGN)], sem.at[tt % 2])
        return ng_ref[tt], ct_ref, tt * GROUPS, make

    @pl.when(t == 0)
    def _():
        loc_scr[...] = jnp.zeros(loc_scr.shape, BF16)
        _for_groups(*copies_of(0), start=True)

    @pl.when(t + 1 < nt)
    def _():
        _for_groups(*copies_of(t + 1), start=True)

    _for_groups(*copies_of(t), start=False)

    route = route_ref[...]
    soff_row = soff_ref[...]
    lane = lax.broadcasted_iota(jnp.int32, (TOK_TILE, LANES), 1).astype(F32)
    slots = [jnp.sum(jnp.where(lane == route[:, k:k + 1], soff_row, 0.0), axis=1, keepdims=True)
             + route[:, 2 * TOP_K + k:2 * TOP_K + k + 1] for k in range(TOP_K)]

    def weighted_sum(m):
        c_i = lax.broadcasted_iota(jnp.int32, (TOK_TILE, m), 1).astype(F32)
        wmat = jnp.zeros((TOK_TILE, m), F32)
        for k in range(TOP_K):
            wmat = jnp.where(c_i == slots[k], route[:, TOP_K + k:TOP_K + k + 1], wmat)
        moe_scr[...] = _dot(wmat.astype(BF16), loc_scr[buf, 0:m, :])

    fits = ng_ref[t] * ROW_ALIGN <= M_LOC_SHORT
    pl.when(fits)(lambda: weighted_sum(M_LOC_SHORT))
    pl.when(jnp.logical_not(fits))(lambda: weighted_sum(M_LOC))

    gate = jnp.where(t >= nt_prompt, gts_ref[...], gtp_ref[...])
    xo = x2_ref[...] + gate * moe_scr[...]
    if final:
        ms = jnp.mean(xo * xo, axis=1, keepdims=True)
        xo = (xo * lax.rsqrt(ms + EPS)) * gf_ref[...]

    @pl.when(t < nt_prompt)
    def _():
        yp_ref[...] = xo

    @pl.when(t >= nt_prompt)
    def _():
        ysm_ref[...] = xo


def _combine(meta, route, soff_row, x2, gt2_p, gt2_s, g_final, ys, n_prompt, n_sample, final):
    nt = x2.shape[0] // TOK_TILE
    nt_prompt = n_prompt // TOK_TILE
    tiles_per_batch = nt_prompt // gt2_p.shape[0]
    grid_spec = pltpu.PrefetchScalarGridSpec(
        num_scalar_prefetch=2,
        grid=(nt,),
        in_specs=[pl.BlockSpec((TOK_TILE, LANES), lambda t, *_: (t, 0)),
                  pl.BlockSpec((None, 1, LANES), lambda t, *_: (t, 0, 0)),
                  pl.BlockSpec((TOK_TILE, D_MODEL), lambda t, *_: (t, 0)),
                  pl.BlockSpec((None, 1, D_MODEL),
                               lambda t, *_: (jnp.minimum(t, nt_prompt - 1) // tiles_per_batch, 0, 0)),
                  pl.BlockSpec((TOK_TILE, D_MODEL), lambda t, *_: (0, 0)),
                  pl.BlockSpec((1, D_MODEL), lambda t, *_: (0, 0)),
                  pl.BlockSpec(memory_space=pl.ANY)],
        out_specs=[pl.BlockSpec((TOK_TILE, D_MODEL), lambda t, *_: (jnp.minimum(t, nt_prompt - 1), 0)),
                   pl.BlockSpec((TOK_TILE, D_MODEL), lambda t, *_: (0, 0))],
        scratch_shapes=[pltpu.VMEM((2, M_LOC, D_MODEL), BF16), pltpu.VMEM((TOK_TILE, D_MODEL), F32),
                        pltpu.SemaphoreType.DMA((2,))],
    )
    return pl.pallas_call(
        functools.partial(_combine_body, nt=nt, nt_prompt=nt_prompt, final=final),
        grid_spec=grid_spec,
        out_shape=[jax.ShapeDtypeStruct((n_prompt, D_MODEL), F32), jax.ShapeDtypeStruct((n_sample, D_MODEL), F32)],
        compiler_params=_params(("arbitrary",)),
        name="moe_combine",
    )(meta["n_groups"], meta["combine_rows"], route, soff_row, x2, gt2_p, gt2_s, g_final, ys)


def _moe_offsets(cnt):
    nt = cnt.shape[0]
    ra = ROW_ALIGN
    prefix = jnp.cumsum(cnt, axis=0) - cnt
    total = jnp.sum(cnt, axis=0)
    pending = prefix % ra
    used = pending + cnt
    seg = (used + ra - 1) // ra * ra
    lo = jnp.cumsum(seg, axis=1) - seg
    n_groups = jnp.sum(seg, axis=1) // ra
    gpad = (total + FFN_TILE - 1) // FFN_TILE * FFN_TILE
    gstart = jnp.cumsum(gpad) - gpad
    base = gstart[None, :] + prefix // ra * ra
    last = (jnp.arange(nt) == nt - 1)[:, None]
    n_write = jnp.where(last, seg // ra, used // ra)
    carry_start = lo + used // ra * ra
    carry_rows = jnp.where(last, 0, used % ra)
    g = jnp.arange(GROUPS)
    slot_end = (lo + seg) // ra
    owner = jnp.minimum(jnp.sum(g[None, :, None] >= slot_end[:, None, :], axis=2), N_EXP - 1)
    pick = lambda a: jnp.sum(jnp.where(owner[:, :, None] == jnp.arange(N_EXP), a[:, None, :], 0), axis=2)
    k = g[None, :] - pick(lo) // ra
    row = pick(base) + k * ra
    valid = g[None, :] < n_groups[:, None]
    rows = (nt * TOK_TILE * TOP_K + N_EXP * (FFN_TILE - 1) + FFN_TILE - 1) // FFN_TILE * FFN_TILE
    spare = rows + (jnp.arange(nt) % 2)[:, None] * M_LOC + g[None, :] * ra
    combine_rows = jnp.where(valid, row, 0)
    dispatch_rows = jnp.where(valid & (k < pick(n_write)), row, spare)

    nblk_e = gpad // FFN_TILE
    blk_end = jnp.cumsum(nblk_e)
    n_used = jnp.maximum(blk_end[-1], 1)
    blk = jnp.minimum(jnp.arange(rows // FFN_TILE, dtype=jnp.int32), n_used - 1)
    blk_exp = jnp.minimum(jnp.sum(blk[:, None] >= blk_end[None, :], axis=1), N_EXP - 1)
    experts = jnp.arange(N_EXP)
    following = lax.cummin(jnp.where(nblk_e > 0, experts, N_EXP), reverse=True)
    next_of = jnp.concatenate([following[1:], jnp.full((1,), N_EXP, following.dtype)])
    next_of = jnp.where(next_of >= N_EXP, -1, next_of)
    parity = (jnp.cumsum(nblk_e > 0) - 1) % 2
    pick_e = lambda a: jnp.sum(jnp.where(blk_exp[:, None] == experts[None, :], a[None, :], 0), axis=1)
    total16 = (total + ra - 1) // ra * ra
    tail = jnp.concatenate([gstart + total16, (gpad - total16) // ra])
    i32 = lambda a: a.astype(jnp.int32).reshape(-1)
    return dict(n_groups=i32(n_groups), dispatch_rows=i32(dispatch_rows), combine_rows=i32(combine_rows),
                slot_start=i32(lo), carry_start=i32(carry_start), carry_rows=i32(carry_rows), tail=i32(tail),
                blk_exp=i32(blk_exp), n_used=i32(n_used), next_exp=i32(pick_e(next_of)), blk_slot=i32(pick_e(parity)),
                slot_off=(lo + pending).astype(F32), rows=rows)


def _split_w_in(w_in):
    g0 = C_GATES
    return (w_in[:, :g0].astype(BF16), w_in[:, g0 + N_GATE:].astype(BF16),
            _pad_lanes(w_in[:, g0:g0 + N_GATE]).astype(BF16))


def _pad_lanes(a, value=0.0):
    return jnp.pad(a, [(0, 0)] * (a.ndim - 1) + [(0, LANES - a.shape[-1])], constant_values=value)


def kernel(x_prompt, x_sample, c_prompt, c_sample, cache_k, cache_v, state_conv, state_C, state_n, state_m, page_table, w_ada, b_ada, g_norm1, g_norm2, w_in, b_gates, lambda_q1, lambda_k1, lambda_q2, lambda_k2, g_subln, w_conv, b_conv, g_mnorm, w_up_a, w_up_b, w_out, w_router, b_router, w_gu, b_gu, w_down, b_down, g_final):
    B, S, D = x_prompt.shape
    Bd, Td, _ = x_sample.shape
    depth = w_in.shape[0]
    n_pool = cache_k.shape[1]
    past_len = page_table.shape[1] * PAGE_SIZE
    n_p, n_s = B * S, Bd * Td
    n_all = n_p + n_s
    assert D == D_MODEL and n_s == TOK_TILE and S % MLSTM_CHUNK == 0 and n_p % TOK_TILE == 0
    assert page_table.shape[1] % PAGES_PER_STEP == 0

    cos_p, sin_p = _rope_tables(np.arange(S))
    cos_s, sin_s = _rope_tables(np.tile(past_len + np.arange(Td), Bd))
    hp = x_prompt.reshape(n_p, D)
    hs = x_sample.reshape(n_s, D)
    c_all = jnp.concatenate([c_prompt, c_sample], axis=0)
    outs = [[] for _ in range(12)]

    for l in range(depth):
        lam_init = 0.8 - 0.6 * math.exp(-0.3 * l)
        mod = _ada(c_all, w_ada[l], b_ada[l])
        mods = [mod[:, j * D:(j + 1) * D] for j in range(6)]
        mp = [m[:B].reshape(B, 1, D) for m in mods]
        ms_ = [jnp.repeat(m[B:], Td, axis=0).reshape(1, n_s, D) for m in mods]
        w_parts = _split_w_in(w_in[l])
        g1 = g_norm1[l].reshape(1, D)
        lam_vec = jnp.stack([lambda_q1[l], lambda_k1[l], lambda_q2[l], lambda_k2[l]])
        gsub = g_subln[l].reshape(1, DV_A)
        bg = _pad_lanes(b_gates[l].reshape(1, N_GATE))
        cw, cb = w_conv[l], b_conv[l].reshape(1, 2 * W_M)

        (q_p, _, kb_p, v_p, _, qk_p, vm_p, om_p, ga_p, gb_p, gt_p, vt_p, kt_p) = _inproj(
            hp, mp[1], mp[0], g1, cos_p, sin_p, w_parts, 1)
        (q_s, k_s, kb_s, v_s, vb_s, qk_s, vm_s, om_s, ga_s, gb_s, gt_s, _, _) = _inproj(
            hs, ms_[1], ms_[0], g1, cos_s, sin_s, w_parts, n_s)
        k_p = jnp.transpose(kt_p.reshape(B, H_A, 2, DK_A, S), (0, 4, 1, 2, 3))

        kt_pool = jnp.transpose(cache_k[l], (0, 2, 3, 4, 1)).reshape(n_pool, W_QA, PAGE_SIZE)
        v_pool = cache_v[l].reshape(n_pool, PAGE_SIZE * H_A, DV_A)
        oa_p, oa_s = _attention(q_p, kb_p, vt_p, q_s, kb_s, vb_s, kt_pool, v_pool, page_table, lam_vec, gsub,
                                B, S, lam_init, Td)

        zeros = lambda *shape: jnp.zeros(shape, F32)
        hm_p, cst_p, C_p, nn_p, m_p = _mlstm(qk_p, vm_p, om_p, gt_p, cw, cb, bg, g_mnorm[l],
                                             zeros(B, CONV_W - 1, 2 * W_M), zeros(B, H_M, DH_M, DH_M),
                                             zeros(B, H_M, DH_M), zeros(B, 1, LANES), B, S)
        hm_s, cst_s, C_s, nn_s, m_s = _mlstm(qk_s, vm_s, om_s, gt_s, cw, cb, bg, g_mnorm[l],
                                             state_conv[l], state_C[l], state_n[l],
                                             _pad_lanes(state_m[l]).reshape(Bd, 1, LANES), Bd, Td)

        wa, wb, wo = w_up_a[l].astype(BF16), w_up_b[l].astype(BF16), w_out[l].astype(BF16)
        wr = _pad_lanes(w_router[l]).astype(BF16)
        br = _pad_lanes(b_router[l].reshape(1, N_EXP))
        g2 = g_norm2[l].reshape(1, D)
        part = _merge(oa_p, hm_p, ga_p, gb_p, hp, mp[2], mp[4], mp[3], g2, wa, wb, wo, wr, br, 1, 0, n_all)
        x2, h2, route, cnt = _merge(oa_s, hm_s, ga_s, gb_s, hs, ms_[2], ms_[4], ms_[3], g2, wa, wb, wo, wr, br,
                                    n_s, n_p // TOK_TILE, n_all, prev=part)

        meta = _moe_offsets(jnp.round(cnt[:, 0, ROUTE_SEL:ROUTE_SEL + N_EXP]).astype(jnp.int32))
        soff = _pad_lanes(meta["slot_off"])
        xs = _dispatch(meta, h2, route, soff[:, :, None])
        ys = _ffn(meta, xs, w_gu[l], b_gu[l], w_down[l], b_down[l])
        final = l == depth - 1
        hp, hs = _combine(meta, route, soff[:, None, :], x2, mp[5], ms_[5][0], g_final.reshape(1, D), ys,
                          n_p, n_s, final)

        for j, a in enumerate([k_p, v_p.reshape(B, S, H_A, DV_A), cst_p, C_p, nn_p,
                               m_p[:, 0, :H_M],
                               k_s.reshape(Bd, Td, H_A, 2, DK_A), v_s.reshape(Bd, Td, H_A, DV_A), cst_s, C_s, nn_s,
                               m_s[:, 0, :H_M]]):
            outs[j].append(a)

    return (hp.reshape(B, S, D), hs.reshape(Bd, Td, D)) + tuple(jnp.stack(o) for o in outs)
```

```python
import functools
import math

import numpy as np
import jax
import jax.numpy as jnp
from jax import lax
from jax.experimental import pallas as pl
from jax.experimental.pallas import tpu as pltpu

F32 = jnp.float32
BF16 = jnp.bfloat16

D_MODEL = 1024
H_A = 4
DK_A = 64
DV_A = 2 * DK_A
ROPE_THETA = 10000.0
H_M = 4
DH_M = 128
CONV_W = 4
N_EXP = 32
TOP_K = 4
D_FF = D_MODEL
SWIGLU_LIMIT = 7.0
SWIGLU_ALPHA = 1.702
EPS = 1e-6
PAGE_SIZE = 128

W_QA = H_A * 2 * DK_A
W_VA = H_A * DV_A
W_M = H_M * DH_M
N_GATE = 2 * H_M

LANES = 128
ROW_ALIGN = 16
TOK_TILE = 256
FFN_TILE = 512
ATT_TILE = 256
ATT_HEADS = H_A
MLSTM_CHUNK = 256
PAGES_PER_STEP = 16
CHUNKS_PER_STEP = 2
NEG_BIG = -1e30
LOG2_E = math.log2(math.e)
VMEM_LIMIT = 56 * 1024 * 1024

C_QA, C_KA, C_VA, C_QK, C_VM, C_OM, C_GATES = 0, 512, 1024, 1536, 2560, 3072, 3584
M_LOC = ((TOK_TILE * TOP_K + 2 * N_EXP * (ROW_ALIGN - 1)) + 255) // 256 * 256
GROUPS = M_LOC // ROW_ALIGN
M_LOC_SHORT = TOK_TILE * TOP_K + N_EXP * ROW_ALIGN
ROUTE_SEL = 32
GROUP_UNROLL = 4


def _dot(a, b):
    return jnp.dot(a, b, preferred_element_type=F32)


def _dot_nt(a, b):
    return lax.dot_general(a, b, (((1,), (1,)), ((), ())), preferred_element_type=F32)


def _sigmoid(x):
    return 0.5 * jnp.tanh(0.5 * x) + 0.5


def _start_all(copies):
    for cp in copies:
        cp.start()


def _params(sem):
    return pltpu.CompilerParams(dimension_semantics=sem, vmem_limit_bytes=VMEM_LIMIT)


def _ada_body(c_ref, w_ref, b_ref, o_ref):
    c = c_ref[...]
    s = c * _sigmoid(c)
    s_hi = s.astype(BF16)
    s_lo = (s - s_hi.astype(F32)).astype(BF16)
    w = w_ref[...]
    w_hi = w.astype(BF16)
    w_lo = (w - w_hi.astype(F32)).astype(BF16)
    o_ref[...] = _dot(s_hi, w_hi) + _dot(s_lo, w_hi) + _dot(s_hi, w_lo) + b_ref[...]


def _ada(c_all, w_ada, b_ada):
    rows = c_all.shape[0]
    n_out = w_ada.shape[1]
    blk = D_MODEL
    return pl.pallas_call(
        _ada_body,
        grid=(n_out // blk,),
        in_specs=[pl.BlockSpec((rows, D_MODEL), lambda j: (0, 0)),
                  pl.BlockSpec((D_MODEL, blk), lambda j: (0, j)),
                  pl.BlockSpec((1, blk), lambda j: (0, j))],
        out_specs=pl.BlockSpec((rows, blk), lambda j: (0, j)),
        out_shape=jax.ShapeDtypeStruct((rows, n_out), F32),
        compiler_params=_params(("arbitrary",)),
        name="ada",
    )(c_all, w_ada, b_ada.reshape(1, n_out))


def _rope(z, cos, sin):
    lane = lax.broadcasted_iota(jnp.int32, (z.shape[0], LANES), 1)
    first_half = (lane % DK_A) < (DK_A // 2)
    out = []
    for h in range(H_A):
        xh = z[:, h * LANES:(h + 1) * LANES]
        partner = jnp.where(first_half, pltpu.roll(xh, LANES - DK_A // 2, 1), pltpu.roll(xh, DK_A // 2, 1))
        out.append(xh * cos + partner * sin)
    return jnp.concatenate(out, axis=1)


def _conv_silu(u, ext, cw_ref, cb_ref):
    rows = u.shape[0]
    full = jnp.concatenate([ext, u], axis=0)
    conv = cb_ref[...] + cw_ref[CONV_W - 1:CONV_W, :] * u
    for j in range(CONV_W - 1):
        conv = conv + cw_ref[j:j + 1, :] * pltpu.roll(full, CONV_W - 1 - j, 0)[8:8 + rows]
    return conv * _sigmoid(conv)


def _inproj_body(x_ref, sc_ref, sh_ref, g_ref, cos_ref, sin_ref, w_ref, wg_ref, wt_ref,
                 q_ref, k_ref, kb_ref, v_ref, vb_ref, qk_ref, vm_ref, om_ref, ga_ref, gb_ref, gt_ref, vt_ref, kt_ref):
    x = x_ref[...]
    ms = jnp.mean(x * x, axis=1, keepdims=True)
    h = (x * lax.rsqrt(ms + EPS)) * g_ref[...] * (1.0 + sc_ref[...]) + sh_ref[...]
    hb = h.astype(BF16)
    cos = cos_ref[...]
    sin = sin_ref[...]

    def seg(lo, n):
        return _dot(hb, w_ref[:, lo:lo + n])

    q = _rope(seg(C_QA, W_QA), cos, sin) * (DK_A ** -0.5 * LOG2_E)
    q_ref[...] = q.astype(BF16)
    k = _rope(seg(C_KA, W_QA), cos, sin)
    k_ref[...] = k
    kb_ref[...] = k.astype(BF16)
    kt_ref[...] = k.T
    v = seg(C_VA, W_VA)
    for h in range(H_A):
        v_ref[pl.ds(h, v.shape[0], stride=H_A), :] = v[:, h * DV_A:(h + 1) * DV_A]
    vb_ref[...] = v.astype(BF16)
    vt_ref[...] = v.T.astype(BF16)
    qk_ref[...] = seg(C_QK, 2 * W_M)
    vm_ref[...] = seg(C_VM, W_M).astype(BF16)
    om_ref[...] = seg(C_OM, W_M)
    ga_ref[...] = _dot(hb, wg_ref[:, :D_MODEL])
    gb_ref[...] = _dot(hb, wg_ref[:, D_MODEL:])
    gt_ref[...] = _dot(hb, wt_ref[...])


def _inproj(x, sc, sh, g1, cos, sin, w_parts, rows_per_mod):
    n = x.shape[0]
    nt = n // TOK_TILE
    tiles_per_group = nt // sc.shape[0]
    tab_tiles = cos.shape[0] // TOK_TILE
    tok = lambda w: pl.BlockSpec((TOK_TILE, w), lambda i: (i, 0))
    mod = pl.BlockSpec((None, rows_per_mod, D_MODEL), lambda i: (i // tiles_per_group, 0, 0))
    tab = pl.BlockSpec((TOK_TILE, LANES), lambda i: (i % tab_tiles, 0))
    tok_out = lambda w, dt: (tok(w), jax.ShapeDtypeStruct((n, w), dt))
    outs = [
        tok_out(W_QA, BF16),
        tok_out(W_QA, F32), tok_out(W_QA, BF16),
        (pl.BlockSpec((TOK_TILE * H_A, DV_A), lambda i: (i, 0)), jax.ShapeDtypeStruct((n * H_A, DV_A), F32)),
        tok_out(W_VA, BF16),
        tok_out(2 * W_M, F32), tok_out(W_M, BF16), tok_out(W_M, F32),
        tok_out(D_MODEL, F32), tok_out(D_MODEL, F32), tok_out(LANES, F32),
        (pl.BlockSpec((None, W_VA, TOK_TILE), lambda i: (i, 0, 0)), jax.ShapeDtypeStruct((nt, W_VA, TOK_TILE), BF16)),
        (pl.BlockSpec((None, W_QA, TOK_TILE), lambda i: (i // tab_tiles, 0, i % tab_tiles)),
         jax.ShapeDtypeStruct((nt // tab_tiles, W_QA, tab_tiles * TOK_TILE), F32)),
    ]
    return pl.pallas_call(
        _inproj_body,
        grid=(nt,),
        in_specs=[tok(D_MODEL), mod, mod, pl.BlockSpec((1, D_MODEL), lambda i: (0, 0)), tab, tab,
                  *[pl.BlockSpec(w.shape, lambda i: (0, 0), pipeline_mode=pl.Buffered(1)) for w in w_parts]],
        out_specs=[spec for spec, _ in outs],
        out_shape=[shape for _, shape in outs],
        compiler_params=_params(("arbitrary",)),
        name="inproj",
    )(x, sc, sh, g1, cos, sin, *w_parts)


def _rope_tables(pos):
    half = DK_A // 2
    inv = ROPE_THETA ** (-np.arange(half, dtype=np.float64) * 2.0 / DK_A)
    ang = np.asarray(pos, np.float64)[:, None] * inv[None, :]
    cos = np.cos(ang)
    sin = np.sin(ang)
    cos64 = np.concatenate([cos, cos], axis=1)
    sin64 = np.concatenate([-sin, sin], axis=1)
    return (jnp.asarray(np.tile(cos64, (1, LANES // DK_A)), F32),
            jnp.asarray(np.tile(sin64, (1, LANES // DK_A)), F32))


def _lambda_value(lam_ref, lam_init):
    lv = lam_ref[...]
    l1 = jnp.sum(lv[0:1, :] * lv[1:2, :], axis=1, keepdims=True)
    l2 = jnp.sum(lv[2:3, :] * lv[3:4, :], axis=1, keepdims=True)
    return jnp.exp(l1) - jnp.exp(l2) + lam_init


def _subln(o, g, lam_init):
    ms = jnp.mean(o * o, axis=1, keepdims=True)
    return (o * lax.rsqrt(ms + EPS)) * g * (1.0 - lam_init)


def _prompt_tile(q_ref, k_ref, vt_ref, lam, g_ref, o_ref, m_scr, acc_scr, i, lam_init):
    tq = ATT_TILE
    lane = lax.broadcasted_iota(jnp.int32, (tq, LANES), 1)
    qs = []
    for hh in range(ATT_HEADS):
        q = q_ref[:, hh * LANES:(hh + 1) * LANES]
        zero = jnp.zeros_like(q)
        qs.append(jnp.concatenate([jnp.where(lane < DK_A, q, zero), jnp.where(lane >= DK_A, q, zero)], axis=0))
    for hh in range(ATT_HEADS):
        m_scr[hh][...] = jnp.full(m_scr[hh].shape, NEG_BIG, F32)
        acc_scr[hh][...] = jnp.zeros(acc_scr[hh].shape, F32)
    ones = jnp.ones((ROW_ALIGN, tq), BF16)

    def scores(hh, j):
        start = pl.multiple_of(j * tq, tq)
        return _dot_nt(k_ref[pl.ds(start, tq), hh * LANES:(hh + 1) * LANES], qs[hh])

    def update_all(j, mask):
        sts = [scores(hh, j) for hh in range(ATT_HEADS)]
        if mask is not None:
            sts = [jnp.where(mask, st, NEG_BIG) for st in sts]
        pts, alphas = [], []
        for hh in range(ATT_HEADS):
            m_old = m_scr[hh][...]
            m_new = jnp.maximum(m_old, jnp.max(sts[hh], axis=0, keepdims=True))
            alphas.append(jnp.exp2(m_old - m_new))
            pts.append(jnp.exp2(sts[hh] - m_new).astype(BF16))
            m_scr[hh][...] = m_new
        for hh in range(ATT_HEADS):
            vt = jnp.concatenate([vt_ref[j, hh * LANES:(hh + 1) * LANES, :], ones], axis=0)
            acc_scr[hh][...] = alphas[hh] * acc_scr[hh][...] + _dot(vt, pts[hh])

    def off_diag(j, carry):
        update_all(j, None)
        return carry

    lax.fori_loop(0, i, off_diag, 0)
    key = lax.broadcasted_iota(jnp.int32, (tq, 2 * tq), 0)
    qry = lax.broadcasted_iota(jnp.int32, (tq, 2 * tq), 1) % tq
    update_all(i, key <= qry)
    for hh in range(ATT_HEADS):
        acc = acc_scr[hh][...]
        ot = acc[:DV_A] / acc[DV_A:DV_A + 1]
        at = ot[:, :tq] - lam * ot[:, tq:]
        ms = jnp.mean(at * at, axis=0, keepdims=True)
        at = (at * lax.rsqrt(ms + EPS)) * g_ref[...] * (1.0 - lam_init)
        o_ref[:, hh * LANES:(hh + 1) * LANES] = at.T.astype(BF16)


def _sample_chunks(pt_ref, q_ref, kn_ref, vn_ref, lam, g_ref, kpool_ref, vpool_ref, o_ref,
                   kbuf, vbuf, sem, m_scr, l_scr, acc_scr, step, parts, *, lam_init, n_chunks, n_seq, t_new):
    steps_per_seq = n_chunks // CHUNKS_PER_STEP
    b = step // steps_per_seq
    c0 = (step % steps_per_seq) * CHUNKS_PER_STEP
    opens = 0 in parts
    closes = CHUNKS_PER_STEP - 1 in parts

    def chunk_copies(bb, c, slot):
        out = []
        for j in range(PAGES_PER_STEP):
            page = pt_ref[bb, c * PAGES_PER_STEP + j]
            out.append(pltpu.make_async_copy(kpool_ref.at[page], kbuf.at[slot, j], sem.at[slot]))
            out.append(pltpu.make_async_copy(vpool_ref.at[page], vbuf.at[slot, j], sem.at[slot]))
        return out

    if opens:
        @pl.when(step == 0)
        def _():
            _start_all(chunk_copies(0, 0, 0) + chunk_copies(0, 1, 1))

    q = q_ref[...].astype(F32)
    qt = jnp.concatenate([q] * (2 * H_A), axis=0)
    row = lax.broadcasted_iota(jnp.int32, qt.shape, 0)
    col = lax.broadcasted_iota(jnp.int32, qt.shape, 1)
    qbd = jnp.where(col // DK_A == row // t_new, qt, 0.0).astype(BF16)

    if opens:
        @pl.when(c0 == 0)
        def _():
            m_scr[...] = jnp.full(m_scr.shape, NEG_BIG, F32)
            l_scr[...] = jnp.zeros(l_scr.shape, F32)
            acc_scr[...] = jnp.zeros(acc_scr.shape, F32)

    rows_h = 2 * t_new

    def update(s, v_of_head):
        m_old = m_scr[...]
        m_new = jnp.maximum(m_old, jnp.max(s, axis=1, keepdims=True))
        alpha = jnp.exp2(m_old - m_new)
        p = jnp.exp2(s - m_new)
        l_scr[...] = alpha * l_scr[...] + jnp.sum(p, axis=1, keepdims=True)
        pb = p.astype(BF16)
        pv = [_dot(pb[h * rows_h:(h + 1) * rows_h, :], v_of_head(h)) for h in range(H_A)]
        acc_scr[...] = alpha * acc_scr[...] + jnp.concatenate(pv, axis=0)
        m_scr[...] = m_new

    def chunk(c, slot):
        for cp in chunk_copies(b, c, slot):
            cp.wait()
        kt = jnp.concatenate([kbuf[slot, j].astype(BF16) for j in range(PAGES_PER_STEP)], axis=1)

        def cached_v(h):
            return jnp.concatenate([vbuf.at[slot, j][pl.ds(h, PAGE_SIZE, stride=H_A), :].astype(BF16)
                                    for j in range(PAGES_PER_STEP)], axis=0)

        update(_dot(qbd, kt), cached_v)

        @pl.when(c + 2 < n_chunks)
        def _():
            _start_all(chunk_copies(b, c + 2, slot))

        @pl.when(jnp.logical_and(c + 2 >= n_chunks, b + 1 < n_seq))
        def _():
            _start_all(chunk_copies(b + 1, c + 2 - n_chunks, slot))

    for j in parts:
        chunk(c0 + j, j % 2)

    def finish():
        zpad = jnp.zeros((PAGE_SIZE - t_new, W_QA), F32)
        kn = jnp.concatenate([kn_ref[...].astype(F32), zpad], axis=0).astype(BF16)
        vn = jnp.concatenate([vn_ref[...].astype(F32), zpad], axis=0).astype(BF16)
        s = _dot_nt(qbd, kn)
        row = lax.broadcasted_iota(jnp.int32, s.shape, 0) % t_new
        col = lax.broadcasted_iota(jnp.int32, s.shape, 1)
        update(jnp.where(col <= row, s, NEG_BIG), lambda h: vn[:, h * DV_A:(h + 1) * DV_A])
        o = acc_scr[...] / l_scr[...]
        outs = []
        for h in range(H_A):
            r0 = h * rows_h
            outs.append(_subln(o[r0:r0 + t_new] - lam * o[r0 + t_new:r0 + rows_h], g_ref[...], lam_init))
        o_ref[...] = jnp.concatenate(outs, axis=1).astype(BF16)

    if closes:
        pl.when(c0 + CHUNKS_PER_STEP == n_chunks)(finish)


def _attention_body(pt_ref, q_ref, k_ref, vt_ref, lam_ref, gcol_ref, qs_ref, kn_ref, vn_ref, grow_ref,
                    kpool_ref, vpool_ref, o_ref, os_ref, *scratch, lam_init, n_chunks, n_seq, t_new):
    m_scr, acc_scr = scratch[:ATT_HEADS], scratch[ATT_HEADS:2 * ATT_HEADS]
    kbuf, vbuf, sem, ms_scr, ls_scr, accs_scr = scratch[2 * ATT_HEADS:]
    step = pl.program_id(0) * pl.num_programs(1) + pl.program_id(1)
    lam = _lambda_value(lam_ref, lam_init)
    sample = functools.partial(_sample_chunks, pt_ref, qs_ref, kn_ref, vn_ref, lam, grow_ref, kpool_ref, vpool_ref,
                               os_ref, kbuf, vbuf, sem, ms_scr, ls_scr, accs_scr, step,
                               lam_init=lam_init, n_chunks=n_chunks, n_seq=n_seq, t_new=t_new)
    sample((0,))
    _prompt_tile(q_ref, k_ref, vt_ref, lam, gcol_ref, o_ref, m_scr, acc_scr, pl.program_id(1), lam_init)
    sample((1,))


def _attention(q, k, vt, q_s, k_new, v_new, cache_k, cache_v, page_table, lam_vec, g_subln, batch, seq,
               lam_init, t_new):
    nq = seq // ATT_TILE
    bd, n_pages = page_table.shape
    n_chunks = n_pages // PAGES_PER_STEP
    steps_per_seq = n_chunks // CHUNKS_PER_STEP
    assert ATT_HEADS == H_A and CHUNKS_PER_STEP % 2 == 0 and n_chunks % CHUNKS_PER_STEP == 0
    assert batch * nq == bd * steps_per_seq
    n_rows = 2 * H_A * t_new
    kv = pl.BlockSpec((seq, W_QA), lambda b, i, pt: (b, 0))
    vts = pl.BlockSpec((nq, W_VA, ATT_TILE), lambda b, i, pt: (b, 0, 0))
    qo = pl.BlockSpec((ATT_TILE, W_QA), lambda b, i, pt: (b * nq + i, 0))
    new = pl.BlockSpec((None, t_new, W_QA), lambda b, i, pt: ((b * nq + i) // steps_per_seq, 0, 0))
    const = lambda shape: pl.BlockSpec(shape, lambda b, i, pt: (0, 0))
    page_buf = pltpu.VMEM((2, PAGES_PER_STEP, W_QA, PAGE_SIZE), F32)
    grid_spec = pltpu.PrefetchScalarGridSpec(
        num_scalar_prefetch=1,
        grid=(batch, nq),
        in_specs=[qo, kv, vts, const((4, DK_A)), const((DV_A, 1)), new, new, new, const((1, DV_A)),
                  pl.BlockSpec(memory_space=pl.ANY), pl.BlockSpec(memory_space=pl.ANY)],
        out_specs=[qo, new],
        scratch_shapes=[pltpu.VMEM((1, 2 * ATT_TILE), F32)] * ATT_HEADS
                       + [pltpu.VMEM((DV_A + ROW_ALIGN, 2 * ATT_TILE), F32)] * ATT_HEADS
                       + [page_buf, page_buf, pltpu.SemaphoreType.DMA((2,)), pltpu.VMEM((n_rows, 1), F32),
                          pltpu.VMEM((n_rows, 1), F32), pltpu.VMEM((n_rows, DV_A), F32)],
    )
    o_p, o_s = pl.pallas_call(
        functools.partial(_attention_body, lam_init=lam_init, n_chunks=n_chunks, n_seq=bd, t_new=t_new),
        grid_spec=grid_spec,
        out_shape=[jax.ShapeDtypeStruct((batch * seq, W_VA), BF16), jax.ShapeDtypeStruct((bd, t_new, W_VA), BF16)],
        compiler_params=_params(("arbitrary", "arbitrary")),
        name="attention",
    )(page_table, q, k, vt, lam_vec, g_subln.reshape(DV_A, 1), q_s.reshape(bd, t_new, W_QA),
      k_new.reshape(bd, t_new, W_QA), v_new.reshape(bd, t_new, W_VA), g_subln, cache_k, cache_v)
    return o_p, o_s.reshape(bd * t_new, W_VA)


def _mlstm_body(qk_ref, vm_ref, om_ref, gt_ref, cw_ref, cb_ref, bg_ref, gm_ref, cbuf_ref, c0_ref, n0_ref, m0_ref,
                h_ref, cst_ref, cout_ref, nout_ref, mout_ref, ext_scr, c_scr, n_scr, m_scr, *, tb, L, nc):
    c_idx = pl.program_id(1)

    @pl.when(c_idx == 0)
    def _():
        ext_scr[...] = jnp.zeros(ext_scr.shape, F32)
        ext_scr[8 - (CONV_W - 1):8, :] = cbuf_ref[...]
        c_scr[...] = c0_ref[...]
        n_scr[...] = n0_ref[...]
        m_scr[...] = m0_ref[...]

    pad = L - tb
    u = qk_ref[...]
    if pad:
        u = jnp.concatenate([u, jnp.zeros((pad, u.shape[1]), F32)], axis=0)
    a = _conv_silu(u, ext_scr[...], cw_ref, cb_ref)
    if not pad:
        ext_scr[...] = u[L - 8:L]

    @pl.when(c_idx == nc - 1)
    def _():
        cst_ref[...] = qk_ref[tb - (CONV_W - 1):tb, :]

    g = gt_ref[...] + bg_ref[...]
    li = g
    lf = jnp.minimum(g, 0.0) - jnp.log1p(jnp.exp(-jnp.abs(g)))
    if pad:
        zpad = jnp.zeros((pad, LANES), F32)
        li = jnp.concatenate([li, zpad + NEG_BIG], axis=0)
        lf = jnp.concatenate([lf, zpad], axis=0)
    row = lax.broadcasted_iota(jnp.int32, (L, LANES), 0)
    lane = lax.broadcasted_iota(jnp.int32, (L, LANES), 1)
    bcum = lf
    shift = 1
    while shift < L:
        bcum = bcum + jnp.where(row >= shift, pltpu.roll(bcum, shift, 0), 0.0)
        shift *= 2
    gates = jnp.where(lane < H_M, li, bcum)
    gates_t = gates.T
    tri = lax.broadcasted_iota(jnp.int32, (L, L), 0) >= lax.broadcasted_iota(jnp.int32, (L, L), 1)
    m_all = m_scr[...]
    lane1 = lax.broadcasted_iota(jnp.int32, (1, LANES), 1)
    m_next = m_all
    vall = vm_ref[...]
    if pad:
        vall = jnp.concatenate([vall, jnp.zeros((pad, vall.shape[1]), BF16)], axis=0)

    for h in range(H_M):
        li_col = gates[:, h:h + 1]
        b_col = gates[:, H_M + h:H_M + h + 1]
        src_row = gates_t[h:h + 1, :] - gates_t[H_M + h:H_M + h + 1, :]
        m_prev = m_all[:, h:h + 1]
        b_last = b_col[L - 1:L, :]
        log_d = jnp.where(tri, b_col + src_row, NEG_BIG)
        inter = b_col + m_prev
        mt = jnp.maximum(inter, jnp.max(log_d, axis=1, keepdims=True))
        q = a[:, h * DH_M:(h + 1) * DH_M]
        k = a[:, W_M + h * DH_M:W_M + (h + 1) * DH_M] * (DH_M ** -0.5)
        v = vall[:, h * DH_M:(h + 1) * DH_M]
        qb = q.astype(BF16)
        s = _dot_nt(qb, k.astype(BF16)) * jnp.exp(log_d - mt)
        ei = jnp.exp(inter - mt)
        c_old = c_scr[h]
        n_old = n_scr[h:h + 1, :]
        num = ei * _dot(qb, c_old.astype(BF16)) + _dot(s.astype(BF16), v)
        den = ei * jnp.sum(q * n_old, axis=1, keepdims=True) + jnp.sum(s, axis=1, keepdims=True)
        hh = num / jnp.maximum(jnp.abs(den), jnp.exp(-mt))
        g_col = b_last - b_col + li_col
        bl = b_last + m_prev
        m_new = jnp.maximum(bl, jnp.max(g_col, axis=0, keepdims=True))
        wg = jnp.exp(g_col - m_new)
        decay = jnp.exp(bl - m_new)
        kw = k * wg
        c_scr[h] = decay * c_old + _dot(kw.T.astype(BF16), v)
        n_scr[h:h + 1, :] = decay * n_old + jnp.sum(kw, axis=0, keepdims=True)
        m_next = jnp.where(lane1 == h, m_new, m_next)
        ms = jnp.mean(hh * hh, axis=1, keepdims=True)
        hn = (hh * lax.rsqrt(ms + EPS)) * gm_ref[h:h + 1, :]
        og = _sigmoid(om_ref[:, h * DH_M:(h + 1) * DH_M])
        h_ref[:, h * DH_M:(h + 1) * DH_M] = (hn[:tb] * og).astype(BF16)

    m_scr[...] = m_next

    @pl.when(c_idx == nc - 1)
    def _():
        cout_ref[...] = c_scr[...]
        nout_ref[...] = n_scr[...]
        mout_ref[...] = m_scr[...]


def _mlstm(qk, vm, om, gt, w_conv, b_conv, b_gates_pad, g_mnorm, conv_buf, c0, n0, m0_pad, batch, seq):
    tb = min(seq, MLSTM_CHUNK)
    L = max(tb, LANES)
    nc = seq // tb
    tok = lambda w: pl.BlockSpec((None, tb, w), lambda b, c: (b * nc + c, 0, 0))
    chunks = lambda a: a.reshape(batch * nc, tb, a.shape[-1])
    const = lambda shape: pl.BlockSpec(shape, lambda b, c: (0,) * len(shape))
    per_b = lambda shape: pl.BlockSpec((None,) + shape, lambda b, c: (b,) + (0,) * len(shape))
    h, cst, c_out, n_out, m_out = pl.pallas_call(
        functools.partial(_mlstm_body, tb=tb, L=L, nc=nc),
        grid=(batch, nc),
        in_specs=[tok(2 * W_M), tok(W_M), tok(W_M), tok(LANES), const((CONV_W, 2 * W_M)), const((1, 2 * W_M)),
                  const((1, LANES)), const((H_M, DH_M)), per_b((CONV_W - 1, 2 * W_M)),
                  per_b((H_M, DH_M, DH_M)), per_b((H_M, DH_M)), per_b((1, LANES))],
        out_specs=[tok(W_M), per_b((CONV_W - 1, 2 * W_M)), per_b((H_M, DH_M, DH_M)), per_b((H_M, DH_M)),
                   per_b((1, LANES))],
        out_shape=[jax.ShapeDtypeStruct((batch * nc, tb, W_M), BF16),
                   jax.ShapeDtypeStruct((batch, CONV_W - 1, 2 * W_M), F32),
                   jax.ShapeDtypeStruct((batch, H_M, DH_M, DH_M), F32),
                   jax.ShapeDtypeStruct((batch, H_M, DH_M), F32),
                   jax.ShapeDtypeStruct((batch, 1, LANES), F32)],
        scratch_shapes=[pltpu.VMEM((8, 2 * W_M), F32), pltpu.VMEM((H_M, DH_M, DH_M), F32),
                        pltpu.VMEM((H_M, DH_M), F32), pltpu.VMEM((1, LANES), F32)],
        compiler_params=_params(("arbitrary", "arbitrary")),
        name="mlstm",
    )(chunks(qk), chunks(vm), chunks(om), chunks(gt), w_conv, b_conv, b_gates_pad, g_mnorm, conv_buf, c0, n0,
      m0_pad)
    return h.reshape(batch * seq, W_M), cst, c_out, n_out, m_out


def _merge_body(oa_ref, hm_ref, ga_ref, gb_ref, x_ref, gt1_ref, sc2_ref, sh2_ref, g2_ref, wa_ref, wb_ref, wo_ref,
                wr_ref, br_ref, *rest):
    x2_ref, h2_ref, route_ref, cnt_ref = rest[-4:]
    ya = _dot(oa_ref[...], wa_ref[...])
    yb = _dot(hm_ref[...], wb_ref[...])
    mix = _sigmoid(ga_ref[...]) * ya + _sigmoid(gb_ref[...]) * yb
    y = _dot(mix.astype(BF16), wo_ref[...])
    x2 = x_ref[...] + gt1_ref[...] * y
    x2_ref[...] = x2
    ms = jnp.mean(x2 * x2, axis=1, keepdims=True)
    h2 = (x2 * lax.rsqrt(ms + EPS)) * g2_ref[...] * (1.0 + sc2_ref[...]) + sh2_ref[...]
    h2b = h2.astype(BF16)
    h2_ref[...] = h2b

    tm = h2b.shape[0]
    logits_t = (_dot(h2b, wr_ref[...]) + br_ref[...]).T[:N_EXP, :]
    row = lax.broadcasted_iota(jnp.int32, (N_EXP, tm), 0)
    row_f = row.astype(F32)
    work = logits_t
    vals, hots = [], []
    for _ in range(TOP_K):
        mx = jnp.max(work, axis=0, keepdims=True)
        idx = jnp.min(jnp.where(work == mx, row, N_EXP), axis=0, keepdims=True)
        hot = row == idx
        vals.append(mx)
        hots.append(hot)
        work = jnp.where(hot, 2.0 * NEG_BIG, work)
    es = [jnp.exp(v - vals[0]) for v in vals]
    den = es[0]
    for e in es[1:]:
        den = den + e
    sel_t = jnp.zeros((N_EXP, tm), F32)
    for hot in hots:
        sel_t = jnp.where(hot, 1.0, sel_t)
    r_i = lax.broadcasted_iota(jnp.int32, (tm, tm), 0)
    c_i = lax.broadcasted_iota(jnp.int32, (tm, tm), 1)
    rank_t = _dot(sel_t.astype(BF16), jnp.where(r_i < c_i, 1.0, 0.0).astype(BF16))
    sub = lax.broadcasted_iota(jnp.int32, (ROUTE_SEL, tm), 0)
    head = jnp.zeros((ROUTE_SEL, tm), F32)
    for k in range(TOP_K):
        e_k = jnp.sum(jnp.where(hots[k], row_f, 0.0), axis=0, keepdims=True)
        r_k = jnp.sum(jnp.where(hots[k], rank_t, 0.0), axis=0, keepdims=True)
        head = jnp.where(sub == k, e_k, head)
        head = jnp.where(sub == TOP_K + k, es[k] / den, head)
        head = jnp.where(sub == 2 * TOP_K + k, r_k, head)
    pad = jnp.zeros((LANES - ROUTE_SEL - N_EXP, tm), F32)
    route = jnp.concatenate([head, sel_t, pad], axis=0).T
    route_ref[...] = route
    cnt_ref[...] = jnp.sum(route, axis=0, keepdims=True)


def _merge(oa, hm, ga, gb, x, gt1, sc2, sh2, g2, wa, wb, wo, wr, br, rows_per_mod, tile0, n_all, prev=None):
    n = x.shape[0]
    nt = n // TOK_TILE
    nt_all = n_all // TOK_TILE
    tiles_per_group = nt // gt1.shape[0]
    tok = lambda w: pl.BlockSpec((TOK_TILE, w), lambda i: (i, 0))
    mod = pl.BlockSpec((None, rows_per_mod, D_MODEL), lambda i: (i // tiles_per_group, 0, 0))
    res = lambda shape: pl.BlockSpec(shape, lambda i: (0, 0), pipeline_mode=pl.Buffered(1))
    out_tok = lambda w: pl.BlockSpec((TOK_TILE, w), lambda i: (tile0 + i, 0))
    in_specs = [tok(W_VA), tok(W_M), tok(D_MODEL), tok(D_MODEL), tok(D_MODEL), mod, mod, mod,
                pl.BlockSpec((1, D_MODEL), lambda i: (0, 0)),
                res((W_VA, D_MODEL)), res((W_M, D_MODEL)), res((D_MODEL, D_MODEL)), res((D_MODEL, LANES)),
                pl.BlockSpec((1, LANES), lambda i: (0, 0))]
    args = [oa, hm, ga, gb, x, gt1, sc2, sh2, g2, wa, wb, wo, wr, br]
    aliases = {}
    if prev is not None:
        in_specs += [pl.BlockSpec(memory_space=pl.ANY)] * 4
        aliases = {len(args) + j: j for j in range(4)}
        args += list(prev)
    return pl.pallas_call(
        _merge_body,
        grid=(nt,),
        in_specs=in_specs,
        out_specs=[out_tok(D_MODEL), out_tok(D_MODEL), out_tok(LANES),
                   pl.BlockSpec((None, 1, LANES), lambda i: (tile0 + i, 0, 0))],
        out_shape=[jax.ShapeDtypeStruct((n_all, D_MODEL), F32), jax.ShapeDtypeStruct((n_all, D_MODEL), BF16),
                   jax.ShapeDtypeStruct((n_all, LANES), F32), jax.ShapeDtypeStruct((nt_all, 1, LANES), F32)],
        input_output_aliases=aliases,
        compiler_params=_params(("arbitrary",)),
        name="merge",
    )(*args)


def _segment_copies(src, dst, sem, src_row, dst_row, n_groups, max_groups):
    out = []
    bit = 1
    while bit * 2 <= max_groups:
        bit *= 2
    while bit >= 1:
        off = (n_groups // (2 * bit)) * (2 * bit) * ROW_ALIGN
        rows = bit * ROW_ALIGN
        cp = pltpu.make_async_copy(src.at[pl.ds(pl.multiple_of(src_row + off, ROW_ALIGN), rows)],
                                   dst.at[pl.ds(pl.multiple_of(dst_row + off, ROW_ALIGN), rows)], sem)
        out.append(((n_groups // bit) % 2 == 1, cp))
        bit //= 2
    return out


def _run_copies(copies):
    for pred, cp in copies:
        pl.when(pred)(cp.start)
    for pred, cp in copies:
        pl.when(pred)(cp.wait)


def _slot_rows(route_t, loff_col, k):
    e_row = route_t[k:k + 1, :]
    r_row = route_t[2 * TOP_K + k:2 * TOP_K + k + 1, :]
    sub = lax.broadcasted_iota(jnp.int32, (LANES, route_t.shape[1]), 0).astype(F32)
    return jnp.sum(jnp.where(sub == e_row, loff_col, 0.0), axis=0, keepdims=True) + r_row


def _for_groups(n_groups, table_ref, base, make_copy, start):
    def body(i, carry):
        for j in range(GROUP_UNROLL):
            g = i * GROUP_UNROLL + j
            cp = make_copy(g, table_ref[base + g])
            if start:
                cp.start()
            else:
                cp.wait()
        return carry
    lax.fori_loop(0, (n_groups + GROUP_UNROLL - 1) // GROUP_UNROLL, body, 0)


def _dispatch_body(ng_ref, dt_ref, lo_ref, sv_ref, rm_ref, tail_ref, h2_ref, route_ref, soff_ref, xs_ref,
                   loc_scr, carry_scr, zero_scr, sem, *, nt):
    t = pl.program_id(0)
    buf = t % 2

    def copies_of(tt):
        def make(g, row):
            return pltpu.make_async_copy(
                loc_scr.at[tt % 2, pl.ds(pl.multiple_of(g * ROW_ALIGN, ROW_ALIGN), ROW_ALIGN)],
                xs_ref.at[pl.ds(pl.multiple_of(row, ROW_ALIGN), ROW_ALIGN)], sem.at[tt % 2])
        return ng_ref[tt], dt_ref, tt * GROUPS, make

    @pl.when(t == 0)
    def _():
        carry_scr[...] = jnp.zeros(carry_scr.shape, BF16)
        loc_scr[...] = jnp.zeros(loc_scr.shape, BF16)

    @pl.when(t >= 2)
    def _():
        _for_groups(*copies_of(t - 2), start=False)

    route_t = route_ref[...].T
    soff_col = soff_ref[...]
    slots = [_slot_rows(route_t, soff_col, k) for k in range(TOP_K)]

    def sort_rows(m):
        r_i = lax.broadcasted_iota(jnp.int32, (m, TOK_TILE), 0).astype(F32)
        onehot = jnp.zeros((m, TOK_TILE), F32)
        for k in range(TOP_K):
            onehot = jnp.where(r_i == slots[k], 1.0, onehot)
        loc_scr[buf, 0:m, :] = _dot(onehot.astype(BF16), h2_ref[...]).astype(BF16)

    fits = ng_ref[t] * ROW_ALIGN <= M_LOC_SHORT
    pl.when(fits)(lambda: sort_rows(M_LOC_SHORT))
    pl.when(jnp.logical_not(fits))(lambda: sort_rows(M_LOC))

    for e in range(N_EXP):
        lo = pl.multiple_of(lo_ref[t * N_EXP + e], ROW_ALIGN)
        sv = pl.multiple_of(sv_ref[t * N_EXP + e], ROW_ALIGN)
        loc_scr[buf, pl.ds(lo, ROW_ALIGN), :] = loc_scr[buf, pl.ds(lo, ROW_ALIGN), :] + carry_scr[e]
        pending = loc_scr[buf, pl.ds(sv, ROW_ALIGN), :]
        carry_scr[e] = jnp.where(rm_ref[t * N_EXP + e] > 0, pending, jnp.zeros_like(pending))

    _for_groups(*copies_of(t), start=True)

    @pl.when(t == nt - 1)
    def _():
        if nt >= 2:
            _for_groups(*copies_of(t - 1), start=False)
        _for_groups(*copies_of(t), start=False)
        zero_scr[...] = jnp.zeros(zero_scr.shape, BF16)
        tails = []
        for e in range(N_EXP):
            tails += _segment_copies(zero_scr, xs_ref, sem.at[0], 0, tail_ref[e], tail_ref[N_EXP + e],
                                     FFN_TILE // ROW_ALIGN - 1)
        _run_copies(tails)


def _dispatch(meta, h2, route, soff_col):
    nt = h2.shape[0] // TOK_TILE
    grid_spec = pltpu.PrefetchScalarGridSpec(
        num_scalar_prefetch=6,
        grid=(nt,),
        in_specs=[pl.BlockSpec((TOK_TILE, D_MODEL), lambda t, *_: (t, 0)),
                  pl.BlockSpec((TOK_TILE, LANES), lambda t, *_: (t, 0)),
                  pl.BlockSpec((None, LANES, 1), lambda t, *_: (t, 0, 0))],
        out_specs=pl.BlockSpec(memory_space=pl.ANY),
        scratch_shapes=[pltpu.VMEM((2, M_LOC + ROW_ALIGN, D_MODEL), BF16),
                        pltpu.VMEM((N_EXP, ROW_ALIGN, D_MODEL), BF16),
                        pltpu.VMEM((FFN_TILE, D_MODEL), BF16), pltpu.SemaphoreType.DMA((2,))],
    )
    return pl.pallas_call(
        functools.partial(_dispatch_body, nt=nt),
        grid_spec=grid_spec,
        out_shape=jax.ShapeDtypeStruct((meta["rows"] + 2 * M_LOC, D_MODEL), BF16),
        compiler_params=_params(("arbitrary",)),
        name="moe_dispatch",
    )(meta["n_groups"], meta["dispatch_rows"], meta["slot_start"], meta["carry_start"], meta["carry_rows"],
      meta["tail"], h2, route, soff_col)


def _ffn_body(be_ref, nu_ref, nx_ref, sl_ref, x_ref, bgu_ref, bd_ref, wgu_hbm, wd_hbm, y_ref,
              wgu_f32, wd_f32, wgu_scr, wd_scr, sem):
    i = pl.program_id(0)
    expert = be_ref[i]
    slot = sl_ref[i]
    prev = be_ref[jnp.maximum(i - 1, 0)]

    def weight_copies(e, s):
        return (pltpu.make_async_copy(wgu_hbm.at[e], wgu_f32.at[s], sem.at[s]),
                pltpu.make_async_copy(wd_hbm.at[e], wd_f32.at[s], sem.at[s]))

    @pl.when(i == 0)
    def _():
        for cp in weight_copies(expert, slot):
            cp.start()

    @pl.when(jnp.logical_and(i < nu_ref[0], jnp.logical_or(i == 0, expert != prev)))
    def _():
        for cp in weight_copies(expert, slot):
            cp.wait()
        wgu_scr[...] = wgu_f32[slot].astype(BF16)
        wd_scr[...] = wd_f32[slot].astype(BF16)

        @pl.when(nx_ref[i] >= 0)
        def _():
            for cp in weight_copies(nx_ref[i], 1 - slot):
                cp.start()

    @pl.when(i < nu_ref[0])
    def _():
        gu = _dot(x_ref[...], wgu_scr[...]) + bgu_ref[...]
        gate = jnp.minimum(gu[:, :D_FF], SWIGLU_LIMIT)
        up = jnp.clip(gu[:, D_FF:], -SWIGLU_LIMIT, SWIGLU_LIMIT)
        act = (up + 1.0) * gate * _sigmoid(SWIGLU_ALPHA * gate)
        y_ref[...] = (_dot(act.astype(BF16), wd_scr[...]) + bd_ref[...]).astype(BF16)


def _ffn(meta, xs, w_gu, b_gu, w_down, b_down):
    rows = meta["rows"]
    nblk = rows // FFN_TILE
    row_blk = pl.BlockSpec((FFN_TILE, D_MODEL), lambda i, be, nu, *_: (jnp.minimum(i, nu[0] - 1), 0))
    grid_spec = pltpu.PrefetchScalarGridSpec(
        num_scalar_prefetch=4,
        grid=(nblk,),
        in_specs=[row_blk,
                  pl.BlockSpec((None, 1, 2 * D_FF), lambda i, be, *_: (be[i], 0, 0)),
                  pl.BlockSpec((None, 1, D_MODEL), lambda i, be, *_: (be[i], 0, 0)),
                  pl.BlockSpec(memory_space=pl.ANY), pl.BlockSpec(memory_space=pl.ANY)],
        out_specs=row_blk,
        scratch_shapes=[pltpu.VMEM((2, D_MODEL, 2 * D_FF), F32), pltpu.VMEM((2, D_FF, D_MODEL), F32),
                        pltpu.VMEM((D_MODEL, 2 * D_FF), BF16), pltpu.VMEM((D_FF, D_MODEL), BF16),
                        pltpu.SemaphoreType.DMA((2,))],
    )
    return pl.pallas_call(
        _ffn_body,
        grid_spec=grid_spec,
        out_shape=jax.ShapeDtypeStruct((rows, D_MODEL), BF16),
        compiler_params=_params(("arbitrary",)),
        name="moe_ffn",
    )(meta["blk_exp"], meta["n_used"], meta["next_exp"], meta["blk_slot"], xs,
      b_gu.reshape(N_EXP, 1, 2 * D_FF), b_down.reshape(N_EXP, 1, D_MODEL), w_gu, w_down)


def _combine_body(ng_ref, ct_ref, route_ref, soff_ref, x2_ref, gtp_ref, gts_ref, gf_ref, ys_ref,
                  yp_ref, ysm_ref, loc_scr, moe_scr, sem, *, nt, nt_prompt, final):
    t = pl.program_id(0)
    buf = t % 2

    def copies_of(tt):
        def make(g, row):
            return pltpu.make_async_copy(
                ys_ref.at[pl.ds(pl.multiple_of(row, ROW_ALIGN), ROW_ALIGN)],
                loc_scr.at[tt % 2, pl.ds(pl.multiple_of(g * ROW_ALIGN, ROW_ALIGN), ROW_ALIGN)], sem.at[tt % 2])
        return ng_ref[tt], ct_ref, tt * GROUPS, make

    @pl.when(t == 0)
    def _():
        loc_scr[...] = jnp.zeros(loc_scr.shape, BF16)
        _for_groups(*copies_of(0), start=True)

    @pl.when(t + 1 < nt)
    def _():
        _for_groups(*copies_of(t + 1), start=True)

    _for_groups(*copies_of(t), start=False)

    route = route_ref[...]
    soff_row = soff_ref[...]
    lane = lax.broadcasted_iota(jnp.int32, (TOK_TILE, LANES), 1).astype(F32)
    slots = [jnp.sum(jnp.where(lane == route[:, k:k + 1], soff_row, 0.0), axis=1, keepdims=True)
             + route[:, 2 * TOP_K + k:2 * TOP_K + k + 1] for k in range(TOP_K)]

    def weighted_sum(m):
        c_i = lax.broadcasted_iota(jnp.int32, (TOK_TILE, m), 1).astype(F32)
        wmat = jnp.zeros((TOK_TILE, m), F32)
        for k in range(TOP_K):
            wmat = jnp.where(c_i == slots[k], route[:, TOP_K + k:TOP_K + k + 1], wmat)
        moe_scr[...] = _dot(wmat.astype(BF16), loc_scr[buf, 0:m, :])

    fits = ng_ref[t] * ROW_ALIGN <= M_LOC_SHORT
    pl.when(fits)(lambda: weighted_sum(M_LOC_SHORT))
    pl.when(jnp.logical_not(fits))(lambda: weighted_sum(M_LOC))

    gate = jnp.where(t >= nt_prompt, gts_ref[...], gtp_ref[...])
    xo = x2_ref[...] + gate * moe_scr[...]
    if final:
        ms = jnp.mean(xo * xo, axis=1, keepdims=True)
        xo = (xo * lax.rsqrt(ms + EPS)) * gf_ref[...]

    @pl.when(t < nt_prompt)
    def _():
        yp_ref[...] = xo

    @pl.when(t >= nt_prompt)
    def _():
        ysm_ref[...] = xo


def _combine(meta, route, soff_row, x2, gt2_p, gt2_s, g_final, ys, n_prompt, n_sample, final):
    nt = x2.shape[0] // TOK_TILE
    nt_prompt = n_prompt // TOK_TILE
    tiles_per_batch = nt_prompt // gt2_p.shape[0]
    grid_spec = pltpu.PrefetchScalarGridSpec(
        num_scalar_prefetch=2,
        grid=(nt,),
        in_specs=[pl.BlockSpec((TOK_TILE, LANES), lambda t, *_: (t, 0)),
                  pl.BlockSpec((None, 1, LANES), lambda t, *_: (t, 0, 0)),
                  pl.BlockSpec((TOK_TILE, D_MODEL), lambda t, *_: (t, 0)),
                  pl.BlockSpec((None, 1, D_MODEL),
                               lambda t, *_: (jnp.minimum(t, nt_prompt - 1) // tiles_per_batch, 0, 0)),
                  pl.BlockSpec((TOK_TILE, D_MODEL), lambda t, *_: (0, 0)),
                  pl.BlockSpec((1, D_MODEL), lambda t, *_: (0, 0)),
                  pl.BlockSpec(memory_space=pl.ANY)],
        out_specs=[pl.BlockSpec((TOK_TILE, D_MODEL), lambda t, *_: (jnp.minimum(t, nt_prompt - 1), 0)),
                   pl.BlockSpec((TOK_TILE, D_MODEL), lambda t, *_: (0, 0))],
        scratch_shapes=[pltpu.VMEM((2, M_LOC, D_MODEL), BF16), pltpu.VMEM((TOK_TILE, D_MODEL), F32),
                        pltpu.SemaphoreType.DMA((2,))],
    )
    return pl.pallas_call(
        functools.partial(_combine_body, nt=nt, nt_prompt=nt_prompt, final=final),
        grid_spec=grid_spec,
        out_shape=[jax.ShapeDtypeStruct((n_prompt, D_MODEL), F32), jax.ShapeDtypeStruct((n_sample, D_MODEL), F32)],
        compiler_params=_params(("arbitrary",)),
        name="moe_combine",
    )(meta["n_groups"], meta["combine_rows"], route, soff_row, x2, gt2_p, gt2_s, g_final, ys)


def _moe_offsets(cnt):
    nt = cnt.shape[0]
    ra = ROW_ALIGN
    prefix = jnp.cumsum(cnt, axis=0) - cnt
    total = jnp.sum(cnt, axis=0)
    pending = prefix % ra
    used = pending + cnt
    seg = (used + ra - 1) // ra * ra
    lo = jnp.cumsum(seg, axis=1) - seg
    n_groups = jnp.sum(seg, axis=1) // ra
    gpad = (total + FFN_TILE - 1) // FFN_TILE * FFN_TILE
    gstart = jnp.cumsum(gpad) - gpad
    base = gstart[None, :] + prefix // ra * ra
    last = (jnp.arange(nt) == nt - 1)[:, None]
    n_write = jnp.where(last, seg // ra, used // ra)
    carry_start = lo + used // ra * ra
    carry_rows = jnp.where(last, 0, used % ra)
    g = jnp.arange(GROUPS)
    slot_end = (lo + seg) // ra
    owner = jnp.minimum(jnp.sum(g[None, :, None] >= slot_end[:, None, :], axis=2), N_EXP - 1)
    pick = lambda a: jnp.sum(jnp.where(owner[:, :, None] == jnp.arange(N_EXP), a[:, None, :], 0), axis=2)
    k = g[None, :] - pick(lo) // ra
    row = pick(base) + k * ra
    valid = g[None, :] < n_groups[:, None]
    rows = (nt * TOK_TILE * TOP_K + N_EXP * (FFN_TILE - 1) + FFN_TILE - 1) // FFN_TILE * FFN_TILE
    spare = rows + (jnp.arange(nt) % 2)[:, None] * M_LOC + g[None, :] * ra
    combine_rows = jnp.where(valid, row, 0)
    dispatch_rows = jnp.where(valid & (k < pick(n_write)), row, spare)

    nblk_e = gpad // FFN_TILE
    blk_end = jnp.cumsum(nblk_e)
    n_used = jnp.maximum(blk_end[-1], 1)
    blk = jnp.minimum(jnp.arange(rows // FFN_TILE, dtype=jnp.int32), n_used - 1)
    blk_exp = jnp.minimum(jnp.sum(blk[:, None] >= blk_end[None, :], axis=1), N_EXP - 1)
    experts = jnp.arange(N_EXP)
    following = lax.cummin(jnp.where(nblk_e > 0, experts, N_EXP), reverse=True)
    next_of = jnp.concatenate([following[1:], jnp.full((1,), N_EXP, following.dtype)])
    next_of = jnp.where(next_of >= N_EXP, -1, next_of)
    parity = (jnp.cumsum(nblk_e > 0) - 1) % 2
    pick_e = lambda a: jnp.sum(jnp.where(blk_exp[:, None] == experts[None, :], a[None, :], 0), axis=1)
    total16 = (total + ra - 1) // ra * ra
    tail = jnp.concatenate([gstart + total16, (gpad - total16) // ra])
    i32 = lambda a: a.astype(jnp.int32).reshape(-1)
    return dict(n_groups=i32(n_groups), dispatch_rows=i32(dispatch_rows), combine_rows=i32(combine_rows),
                slot_start=i32(lo), carry_start=i32(carry_start), carry_rows=i32(carry_rows), tail=i32(tail),
                blk_exp=i32(blk_exp), n_used=i32(n_used), next_exp=i32(pick_e(next_of)), blk_slot=i32(pick_e(parity)),
                slot_off=(lo + pending).astype(F32), rows=rows)


def _split_w_in(w_in):
    g0 = C_GATES
    return (w_in[:, :g0].astype(BF16), w_in[:, g0 + N_GATE:].astype(BF16),
            _pad_lanes(w_in[:, g0:g0 + N_GATE]).astype(BF16))


def _pad_lanes(a, value=0.0):
    return jnp.pad(a, [(0, 0)] * (a.ndim - 1) + [(0, LANES - a.shape[-1])], constant_values=value)


def kernel(x_prompt, x_sample, c_prompt, c_sample, cache_k, cache_v, state_conv, state_C, state_n, state_m, page_table, w_ada, b_ada, g_norm1, g_norm2, w_in, b_gates, lambda_q1, lambda_k1, lambda_q2, lambda_k2, g_subln, w_conv, b_conv, g_mnorm, w_up_a, w_up_b, w_out, w_router, b_router, w_gu, b_gu, w_down, b_down, g_final):
    B, S, D = x_prompt.shape
    Bd, Td, _ = x_sample.shape
    depth = w_in.shape[0]
    n_pool = cache_k.shape[1]
    past_len = page_table.shape[1] * PAGE_SIZE
    n_p, n_s = B * S, Bd * Td
    n_all = n_p + n_s
    assert D == D_MODEL and n_s == TOK_TILE and S % MLSTM_CHUNK == 0 and n_p % TOK_TILE == 0
    assert page_table.shape[1] % PAGES_PER_STEP == 0

    cos_p, sin_p = _rope_tables(np.arange(S))
    cos_s, sin_s = _rope_tables(np.tile(past_len + np.arange(Td), Bd))
    hp = x_prompt.reshape(n_p, D)
    hs = x_sample.reshape(n_s, D)
    c_all = jnp.concatenate([c_prompt, c_sample], axis=0)
    outs = [[] for _ in range(12)]

    for l in range(depth):
        lam_init = 0.8 - 0.6 * math.exp(-0.3 * l)
        mod = _ada(c_all, w_ada[l], b_ada[l])
        mods = [mod[:, j * D:(j + 1) * D] for j in range(6)]
        mp = [m[:B].reshape(B, 1, D) for m in mods]
        ms_ = [jnp.repeat(m[B:], Td, axis=0).reshape(1, n_s, D) for m in mods]
        w_parts = _split_w_in(w_in[l])
        g1 = g_norm1[l].reshape(1, D)
        lam_vec = jnp.stack([lambda_q1[l], lambda_k1[l], lambda_q2[l], lambda_k2[l]])
        gsub = g_subln[l].reshape(1, DV_A)
        bg = _pad_lanes(b_gates[l].reshape(1, N_GATE))
        cw, cb = w_conv[l], b_conv[l].reshape(1, 2 * W_M)

        (q_p, _, kb_p, v_p, _, qk_p, vm_p, om_p, ga_p, gb_p, gt_p, vt_p, kt_p) = _inproj(
            hp, mp[1], mp[0], g1, cos_p, sin_p, w_parts, 1)
        (q_s, k_s, kb_s, v_s, vb_s, qk_s, vm_s, om_s, ga_s, gb_s, gt_s, _, _) = _inproj(
            hs, ms_[1], ms_[0], g1, cos_s, sin_s, w_parts, n_s)
        k_p = jnp.transpose(kt_p.reshape(B, H_A, 2, DK_A, S), (0, 4, 1, 2, 3))

        kt_pool = jnp.transpose(cache_k[l], (0, 2, 3, 4, 1)).reshape(n_pool, W_QA, PAGE_SIZE)
        v_pool = cache_v[l].reshape(n_pool, PAGE_SIZE * H_A, DV_A)
        oa_p, oa_s = _attention(q_p, kb_p, vt_p, q_s, kb_s, vb_s, kt_pool, v_pool, page_table, lam_vec, gsub,
                                B, S, lam_init, Td)

        zeros = lambda *shape: jnp.zeros(shape, F32)
        hm_p, cst_p, C_p, nn_p, m_p = _mlstm(qk_p, vm_p, om_p, gt_p, cw, cb, bg, g_mnorm[l],
                                             zeros(B, CONV_W - 1, 2 * W_M), zeros(B, H_M, DH_M, DH_M),
                                             zeros(B, H_M, DH_M), zeros(B, 1, LANES), B, S)
        hm_s, cst_s, C_s, nn_s, m_s = _mlstm(qk_s, vm_s, om_s, gt_s, cw, cb, bg, g_mnorm[l],
                                             state_conv[l], state_C[l], state_n[l],
                                             _pad_lanes(state_m[l]).reshape(Bd, 1, LANES), Bd, Td)

        wa, wb, wo = w_up_a[l].astype(BF16), w_up_b[l].astype(BF16), w_out[l].astype(BF16)
        wr = _pad_lanes(w_router[l]).astype(BF16)
        br = _pad_lanes(b_router[l].reshape(1, N_EXP))
        g2 = g_norm2[l].reshape(1, D)
        part = _merge(oa_p, hm_p, ga_p, gb_p, hp, mp[2], mp[4], mp[3], g2, wa, wb, wo, wr, br, 1, 0, n_all)
        x2, h2, route, cnt = _merge(oa_s, hm_s, ga_s, gb_s, hs, ms_[2], ms_[4], ms_[3], g2, wa, wb, wo, wr, br,
                                    n_s, n_p // TOK_TILE, n_all, prev=part)

        meta = _moe_offsets(jnp.round(cnt[:, 0, ROUTE_SEL:ROUTE_SEL + N_EXP]).astype(jnp.int32))
        soff = _pad_lanes(meta["slot_off"])
        xs = _dispatch(meta, h2, route, soff[:, :, None])
        ys = _ffn(meta, xs, w_gu[l], b_gu[l], w_down[l], b_down[l])
        final = l == depth - 1
        hp, hs = _combine(meta, route, soff[:, None, :], x2, mp[5], ms_[5][0], g_final.reshape(1, D), ys,
                          n_p, n_s, final)

        for j, a in enumerate([k_p, v_p.reshape(B, S, H_A, DV_A), cst_p, C_p, nn_p,
                               m_p[:, 0, :H_M],
                               k_s.reshape(Bd, Td, H_A, 2, DK_A), v_s.reshape(Bd, Td, H_A, DV_A), cst_s, C_s, nn_s,
                               m_s[:, 0, :H_M]]):
            outs[j].append(a)

    return (hp.reshape(B, S, D), hs.reshape(Bd, Td, D)) + tuple(jnp.stack(o) for o in outs)
```

```python
import functools
import math

import numpy as np
import jax
import jax.numpy as jnp
from jax import lax
from jax.experimental import pallas as pl
from jax.experimental.pallas import tpu as pltpu

F32 = jnp.float32
BF16 = jnp.bfloat16

D_MODEL = 1024
H_A = 4
DK_A = 64
DV_A = 2 * DK_A
ROPE_THETA = 10000.0
H_M = 4
DH_M = 128
CONV_W = 4
N_EXP = 32
TOP_K = 4
D_FF = D_MODEL
SWIGLU_LIMIT = 7.0
SWIGLU_ALPHA = 1.702
EPS = 1e-6
PAGE_SIZE = 128

W_QA = H_A * 2 * DK_A
W_VA = H_A * DV_A
W_M = H_M * DH_M
N_GATE = 2 * H_M

LANES = 128
ROW_ALIGN = 16
TOK_TILE = 256
PROJ_TILE = 512
FFN_TILE = 512
ATT_TILE = 256
ATT_HEADS = H_A
MLSTM_CHUNK = 256
PAGES_PER_STEP = 16
CHUNKS_PER_STEP = 2
NEG_BIG = -1e30
LOG2_E = math.log2(math.e)
VMEM_LIMIT = 56 * 1024 * 1024

C_QA, C_KA, C_VA, C_QK, C_VM, C_OM, C_GATES = 0, 512, 1024, 1536, 2560, 3072, 3584
M_LOC = ((TOK_TILE * TOP_K + 2 * N_EXP * (ROW_ALIGN - 1)) + 255) // 256 * 256
GROUPS = M_LOC // ROW_ALIGN
M_LOC_SHORT = TOK_TILE * TOP_K + N_EXP * ROW_ALIGN
ROUTE_SEL = 32
GROUP_UNROLL = 4


def _dot(a, b):
    return jnp.dot(a, b, preferred_element_type=F32)


def _dot_nt(a, b):
    return lax.dot_general(a, b, (((1,), (1,)), ((), ())), preferred_element_type=F32)


def _sigmoid(x):
    return 0.5 * jnp.tanh(0.5 * x) + 0.5


def _start_all(copies):
    for cp in copies:
        cp.start()


def _params(sem):
    return pltpu.CompilerParams(dimension_semantics=sem, vmem_limit_bytes=VMEM_LIMIT)


def _ada_body(c_ref, w_ref, b_ref, o_ref):
    c = c_ref[...]
    s = c * _sigmoid(c)
    s_hi = s.astype(BF16)
    s_lo = (s - s_hi.astype(F32)).astype(BF16)
    w = w_ref[...]
    w_hi = w.astype(BF16)
    w_lo = (w - w_hi.astype(F32)).astype(BF16)
    o_ref[...] = _dot(s_hi, w_hi) + _dot(s_lo, w_hi) + _dot(s_hi, w_lo) + b_ref[...]


def _ada(c_all, w_ada, b_ada):
    rows = c_all.shape[0]
    n_out = w_ada.shape[1]
    blk = D_MODEL
    return pl.pallas_call(
        _ada_body,
        grid=(n_out // blk,),
        in_specs=[pl.BlockSpec((rows, D_MODEL), lambda j: (0, 0)),
                  pl.BlockSpec((D_MODEL, blk), lambda j: (0, j)),
                  pl.BlockSpec((1, blk), lambda j: (0, j))],
        out_specs=pl.BlockSpec((rows, blk), lambda j: (0, j)),
        out_shape=jax.ShapeDtypeStruct((rows, n_out), F32),
        compiler_params=_params(("arbitrary",)),
        name="ada",
    )(c_all, w_ada, b_ada.reshape(1, n_out))


def _rope(z, cos, sin):
    lane = lax.broadcasted_iota(jnp.int32, (z.shape[0], LANES), 1)
    first_half = (lane % DK_A) < (DK_A // 2)
    out = []
    for h in range(H_A):
        xh = z[:, h * LANES:(h + 1) * LANES]
        partner = jnp.where(first_half, pltpu.roll(xh, LANES - DK_A // 2, 1), pltpu.roll(xh, DK_A // 2, 1))
        out.append(xh * cos + partner * sin)
    return jnp.concatenate(out, axis=1)


def _conv_silu(u, ext, cw_ref, cb_ref):
    rows = u.shape[0]
    full = jnp.concatenate([ext, u], axis=0)
    conv = cb_ref[...] + cw_ref[CONV_W - 1:CONV_W, :] * u
    for j in range(CONV_W - 1):
        conv = conv + cw_ref[j:j + 1, :] * pltpu.roll(full, CONV_W - 1 - j, 0)[8:8 + rows]
    return conv * _sigmoid(conv)


def _inproj_body(x_ref, sc_ref, sh_ref, g_ref, cos_ref, sin_ref, w_ref, wg_ref, wt_ref,
                 q_ref, k_ref, kb_ref, v_ref, vb_ref, qk_ref, vm_ref, om_ref, ga_ref, gb_ref, gt_ref, vt_ref, kt_ref):
    x = x_ref[...]
    ms = jnp.mean(x * x, axis=1, keepdims=True)
    h = (x * lax.rsqrt(ms + EPS)) * g_ref[...] * (1.0 + sc_ref[...]) + sh_ref[...]
    hb = h.astype(BF16)
    cos = cos_ref[...]
    sin = sin_ref[...]

    def seg(lo, n):
        return _dot(hb, w_ref[:, lo:lo + n])

    q = _rope(seg(C_QA, W_QA), cos, sin) * (DK_A ** -0.5 * LOG2_E)
    q_ref[...] = q.astype(BF16)
    k = _rope(seg(C_KA, W_QA), cos, sin)
    k_ref[...] = k
    kb_ref[...] = k.astype(BF16)
    kt_ref[...] = k.T
    v = seg(C_VA, W_VA)
    for h in range(H_A):
        v_ref[pl.ds(h, v.shape[0], stride=H_A), :] = v[:, h * DV_A:(h + 1) * DV_A]
    vb_ref[...] = v.astype(BF16)
    for s in range(vt_ref.shape[0]):
        vt_ref[s] = v[s * ATT_TILE:(s + 1) * ATT_TILE].T.astype(BF16)
    qk_ref[...] = seg(C_QK, 2 * W_M)
    vm_ref[...] = seg(C_VM, W_M).astype(BF16)
    om_ref[...] = seg(C_OM, W_M)
    ga_ref[...] = _dot(hb, wg_ref[:, :D_MODEL])
    gb_ref[...] = _dot(hb, wg_ref[:, D_MODEL:])
    gt_ref[...] = _dot(hb, wt_ref[...])


def _inproj(x, sc, sh, g1, cos, sin, w_parts, rows_per_mod, tile):
    n = x.shape[0]
    nt = n // tile
    tiles_per_group = nt // sc.shape[0]
    tab_tiles = cos.shape[0] // tile
    slabs = tile // ATT_TILE
    tok = lambda w: pl.BlockSpec((tile, w), lambda i: (i, 0))
    mod = pl.BlockSpec((None, rows_per_mod, D_MODEL), lambda i: (i // tiles_per_group, 0, 0))
    tab = pl.BlockSpec((tile, LANES), lambda i: (i % tab_tiles, 0))
    tok_out = lambda w, dt: (tok(w), jax.ShapeDtypeStruct((n, w), dt))
    outs = [
        tok_out(W_QA, BF16),
        tok_out(W_QA, F32), tok_out(W_QA, BF16),
        (pl.BlockSpec((tile * H_A, DV_A), lambda i: (i, 0)), jax.ShapeDtypeStruct((n * H_A, DV_A), F32)),
        tok_out(W_VA, BF16),
        tok_out(2 * W_M, F32), tok_out(W_M, BF16), tok_out(W_M, F32),
        tok_out(D_MODEL, F32), tok_out(D_MODEL, F32), tok_out(LANES, F32),
        (pl.BlockSpec((slabs, W_VA, ATT_TILE), lambda i: (i, 0, 0)),
         jax.ShapeDtypeStruct((n // ATT_TILE, W_VA, ATT_TILE), BF16)),
        (pl.BlockSpec((None, W_QA, tile), lambda i: (i // tab_tiles, 0, i % tab_tiles)),
         jax.ShapeDtypeStruct((nt // tab_tiles, W_QA, tab_tiles * tile), F32)),
    ]
    return pl.pallas_call(
        _inproj_body,
        grid=(nt,),
        in_specs=[tok(D_MODEL), mod, mod, pl.BlockSpec((1, D_MODEL), lambda i: (0, 0)), tab, tab,
                  *[pl.BlockSpec(w.shape, lambda i: (0, 0), pipeline_mode=pl.Buffered(1)) for w in w_parts]],
        out_specs=[spec for spec, _ in outs],
        out_shape=[shape for _, shape in outs],
        compiler_params=_params(("arbitrary",)),
        name="inproj",
    )(x, sc, sh, g1, cos, sin, *w_parts)


def _rope_tables(pos):
    half = DK_A // 2
    inv = ROPE_THETA ** (-np.arange(half, dtype=np.float64) * 2.0 / DK_A)
    ang = np.asarray(pos, np.float64)[:, None] * inv[None, :]
    cos = np.cos(ang)
    sin = np.sin(ang)
    cos64 = np.concatenate([cos, cos], axis=1)
    sin64 = np.concatenate([-sin, sin], axis=1)
    return (jnp.asarray(np.tile(cos64, (1, LANES // DK_A)), F32),
            jnp.asarray(np.tile(sin64, (1, LANES // DK_A)), F32))


def _lambda_value(lam_ref, lam_init):
    lv = lam_ref[...]
    l1 = jnp.sum(lv[0:1, :] * lv[1:2, :], axis=1, keepdims=True)
    l2 = jnp.sum(lv[2:3, :] * lv[3:4, :], axis=1, keepdims=True)
    return jnp.exp(l1) - jnp.exp(l2) + lam_init


def _subln(o, g, lam_init):
    ms = jnp.mean(o * o, axis=1, keepdims=True)
    return (o * lax.rsqrt(ms + EPS)) * g * (1.0 - lam_init)


def _prompt_tile(q_ref, k_ref, vt_ref, lam, g_ref, o_ref, m_scr, acc_scr, i, lam_init):
    tq = ATT_TILE
    lane = lax.broadcasted_iota(jnp.int32, (tq, LANES), 1)
    qs = []
    for hh in range(ATT_HEADS):
        q = q_ref[:, hh * LANES:(hh + 1) * LANES]
        zero = jnp.zeros_like(q)
        qs.append(jnp.concatenate([jnp.where(lane < DK_A, q, zero), jnp.where(lane >= DK_A, q, zero)], axis=0))
    for hh in range(ATT_HEADS):
        m_scr[hh][...] = jnp.full(m_scr[hh].shape, NEG_BIG, F32)
        acc_scr[hh][...] = jnp.zeros(acc_scr[hh].shape, F32)
    ones = jnp.ones((ROW_ALIGN, tq), BF16)

    def scores(hh, j):
        start = pl.multiple_of(j * tq, tq)
        return _dot_nt(k_ref[pl.ds(start, tq), hh * LANES:(hh + 1) * LANES], qs[hh])

    def update_all(j, mask):
        sts = [scores(hh, j) for hh in range(ATT_HEADS)]
        if mask is not None:
            sts = [jnp.where(mask, st, NEG_BIG) for st in sts]
        pts, alphas = [], []
        for hh in range(ATT_HEADS):
            m_old = m_scr[hh][...]
            m_new = jnp.maximum(m_old, jnp.max(sts[hh], axis=0, keepdims=True))
            alphas.append(jnp.exp2(m_old - m_new))
            pts.append(jnp.exp2(sts[hh] - m_new).astype(BF16))
            m_scr[hh][...] = m_new
        for hh in range(ATT_HEADS):
            vt = jnp.concatenate([vt_ref[j, hh * LANES:(hh + 1) * LANES, :], ones], axis=0)
            acc_scr[hh][...] = alphas[hh] * acc_scr[hh][...] + _dot(vt, pts[hh])

    def off_diag(j, carry):
        update_all(j, None)
        return carry

    lax.fori_loop(0, i, off_diag, 0)
    key = lax.broadcasted_iota(jnp.int32, (tq, 2 * tq), 0)
    qry = lax.broadcasted_iota(jnp.int32, (tq, 2 * tq), 1) % tq
    update_all(i, key <= qry)
    for hh in range(ATT_HEADS):
        acc = acc_scr[hh][...]
        ot = acc[:DV_A] / acc[DV_A:DV_A + 1]
        at = ot[:, :tq] - lam * ot[:, tq:]
        ms = jnp.mean(at * at, axis=0, keepdims=True)
        at = (at * lax.rsqrt(ms + EPS)) * g_ref[...] * (1.0 - lam_init)
        o_ref[:, hh * LANES:(hh + 1) * LANES] = at.T.astype(BF16)


def _sample_chunks(pt_ref, q_ref, kn_ref, vn_ref, lam, g_ref, kpool_ref, vpool_ref, o_ref,
                   kbuf, vbuf, sem, m_scr, l_scr, acc_scr, step, parts, *, lam_init, n_chunks, n_seq, t_new):
    steps_per_seq = n_chunks // CHUNKS_PER_STEP
    b = step // steps_per_seq
    c0 = (step % steps_per_seq) * CHUNKS_PER_STEP
    opens = 0 in parts
    closes = CHUNKS_PER_STEP - 1 in parts

    def chunk_copies(bb, c, slot):
        out = []
        for j in range(PAGES_PER_STEP):
            page = pt_ref[bb, c * PAGES_PER_STEP + j]
            out.append(pltpu.make_async_copy(kpool_ref.at[page], kbuf.at[slot, j], sem.at[slot]))
            out.append(pltpu.make_async_copy(vpool_ref.at[page], vbuf.at[slot, j], sem.at[slot]))
        return out

    if opens:
        @pl.when(step == 0)
        def _():
            _start_all(chunk_copies(0, 0, 0) + chunk_copies(0, 1, 1))

    q = q_ref[...].astype(F32)
    qt = jnp.concatenate([q] * (2 * H_A), axis=0)
    row = lax.broadcasted_iota(jnp.int32, qt.shape, 0)
    col = lax.broadcasted_iota(jnp.int32, qt.shape, 1)
    qbd = jnp.where(col // DK_A == row // t_new, qt, 0.0).astype(BF16)

    if opens:
        @pl.when(c0 == 0)
        def _():
            m_scr[...] = jnp.full(m_scr.shape, NEG_BIG, F32)
            l_scr[...] = jnp.zeros(l_scr.shape, F32)
            acc_scr[...] = jnp.zeros(acc_scr.shape, F32)

    rows_h = 2 * t_new

    def update(s, v_of_head):
        m_old = m_scr[...]
        m_new = jnp.maximum(m_old, jnp.max(s, axis=1, keepdims=True))
        alpha = jnp.exp2(m_old - m_new)
        p = jnp.exp2(s - m_new)
        l_scr[...] = alpha * l_scr[...] + jnp.sum(p, axis=1, keepdims=True)
        pb = p.astype(BF16)
        pv = [_dot(pb[h * rows_h:(h + 1) * rows_h, :], v_of_head(h)) for h in range(H_A)]
        acc_scr[...] = alpha * acc_scr[...] + jnp.concatenate(pv, axis=0)
        m_scr[...] = m_new

    def chunk(c, slot):
        for cp in chunk_copies(b, c, slot):
            cp.wait()
        kt = jnp.concatenate([kbuf[slot, j].astype(BF16) for j in range(PAGES_PER_STEP)], axis=1)

        def cached_v(h):
            return jnp.concatenate([vbuf.at[slot, j][pl.ds(h, PAGE_SIZE, stride=H_A), :].astype(BF16)
                                    for j in range(PAGES_PER_STEP)], axis=0)

        update(_dot(qbd, kt), cached_v)

        @pl.when(c + 2 < n_chunks)
        def _():
            _start_all(chunk_copies(b, c + 2, slot))

        @pl.when(jnp.logical_and(c + 2 >= n_chunks, b + 1 < n_seq))
        def _():
            _start_all(chunk_copies(b + 1, c + 2 - n_chunks, slot))

    for j in parts:
        chunk(c0 + j, j % 2)

    def finish():
        zpad = jnp.zeros((PAGE_SIZE - t_new, W_QA), F32)
        kn = jnp.concatenate([kn_ref[...].astype(F32), zpad], axis=0).astype(BF16)
        vn = jnp.concatenate([vn_ref[...].astype(F32), zpad], axis=0).astype(BF16)
        s = _dot_nt(qbd, kn)
        row = lax.broadcasted_iota(jnp.int32, s.shape, 0) % t_new
        col = lax.broadcasted_iota(jnp.int32, s.shape, 1)
        update(jnp.where(col <= row, s, NEG_BIG), lambda h: vn[:, h * DV_A:(h + 1) * DV_A])
        o = acc_scr[...] / l_scr[...]
        outs = []
        for h in range(H_A):
            r0 = h * rows_h
            outs.append(_subln(o[r0:r0 + t_new] - lam * o[r0 + t_new:r0 + rows_h], g_ref[...], lam_init))
        o_ref[...] = jnp.concatenate(outs, axis=1).astype(BF16)

    if closes:
        pl.when(c0 + CHUNKS_PER_STEP == n_chunks)(finish)


def _attention_body(pt_ref, q_ref, k_ref, vt_ref, lam_ref, gcol_ref, qs_ref, kn_ref, vn_ref, grow_ref,
                    kpool_ref, vpool_ref, o_ref, os_ref, *scratch, lam_init, n_chunks, n_seq, t_new):
    m_scr, acc_scr = scratch[:ATT_HEADS], scratch[ATT_HEADS:2 * ATT_HEADS]
    kbuf, vbuf, sem, ms_scr, ls_scr, accs_scr = scratch[2 * ATT_HEADS:]
    step = pl.program_id(0) * pl.num_programs(1) + pl.program_id(1)
    lam = _lambda_value(lam_ref, lam_init)
    sample = functools.partial(_sample_chunks, pt_ref, qs_ref, kn_ref, vn_ref, lam, grow_ref, kpool_ref, vpool_ref,
                               os_ref, kbuf, vbuf, sem, ms_scr, ls_scr, accs_scr, step,
                               lam_init=lam_init, n_chunks=n_chunks, n_seq=n_seq, t_new=t_new)
    sample((0,))
    _prompt_tile(q_ref, k_ref, vt_ref, lam, gcol_ref, o_ref, m_scr, acc_scr, pl.program_id(1), lam_init)
    sample((1,))


def _attention(q, k, vt, q_s, k_new, v_new, cache_k, cache_v, page_table, lam_vec, g_subln, batch, seq,
               lam_init, t_new):
    nq = seq // ATT_TILE
    bd, n_pages = page_table.shape
    n_chunks = n_pages // PAGES_PER_STEP
    steps_per_seq = n_chunks // CHUNKS_PER_STEP
    assert ATT_HEADS == H_A and CHUNKS_PER_STEP % 2 == 0 and n_chunks % CHUNKS_PER_STEP == 0
    assert batch * nq == bd * steps_per_seq
    n_rows = 2 * H_A * t_new
    kv = pl.BlockSpec((seq, W_QA), lambda b, i, pt: (b, 0))
    vts = pl.BlockSpec((nq, W_VA, ATT_TILE), lambda b, i, pt: (b, 0, 0))
    qo = pl.BlockSpec((ATT_TILE, W_QA), lambda b, i, pt: (b * nq + i, 0))
    new = pl.BlockSpec((None, t_new, W_QA), lambda b, i, pt: ((b * nq + i) // steps_per_seq, 0, 0))
    const = lambda shape: pl.BlockSpec(shape, lambda b, i, pt: (0, 0))
    page_buf = pltpu.VMEM((2, PAGES_PER_STEP, W_QA, PAGE_SIZE), F32)
    grid_spec = pltpu.PrefetchScalarGridSpec(
        num_scalar_prefetch=1,
        grid=(batch, nq),
        in_specs=[qo, kv, vts, const((4, DK_A)), const((DV_A, 1)), new, new, new, const((1, DV_A)),
                  pl.BlockSpec(memory_space=pl.ANY), pl.BlockSpec(memory_space=pl.ANY)],
        out_specs=[qo, new],
        scratch_shapes=[pltpu.VMEM((1, 2 * ATT_TILE), F32)] * ATT_HEADS
                       + [pltpu.VMEM((DV_A + ROW_ALIGN, 2 * ATT_TILE), F32)] * ATT_HEADS
                       + [page_buf, page_buf, pltpu.SemaphoreType.DMA((2,)), pltpu.VMEM((n_rows, 1), F32),
                          pltpu.VMEM((n_rows, 1), F32), pltpu.VMEM((n_rows, DV_A), F32)],
    )
    o_p, o_s = pl.pallas_call(
        functools.partial(_attention_body, lam_init=lam_init, n_chunks=n_chunks, n_seq=bd, t_new=t_new),
        grid_spec=grid_spec,
        out_shape=[jax.ShapeDtypeStruct((batch * seq, W_VA), BF16), jax.ShapeDtypeStruct((bd, t_new, W_VA), BF16)],
        compiler_params=_params(("arbitrary", "arbitrary")),
        name="attention",
    )(page_table, q, k, vt, lam_vec, g_subln.reshape(DV_A, 1), q_s.reshape(bd, t_new, W_QA),
      k_new.reshape(bd, t_new, W_QA), v_new.reshape(bd, t_new, W_VA), g_subln, cache_k, cache_v)
    return o_p, o_s.reshape(bd * t_new, W_VA)


def _mlstm_body(qk_ref, vm_ref, om_ref, gt_ref, cw_ref, cb_ref, bg_ref, gm_ref, cbuf_ref, c0_ref, n0_ref, m0_ref,
                h_ref, cst_ref, cout_ref, nout_ref, mout_ref, ext_scr, c_scr, n_scr, m_scr, *, tb, L, nc):
    c_idx = pl.program_id(1)

    @pl.when(c_idx == 0)
    def _():
        ext_scr[...] = jnp.zeros(ext_scr.shape, F32)
        ext_scr[8 - (CONV_W - 1):8, :] = cbuf_ref[...]
        c_scr[...] = c0_ref[...]
        n_scr[...] = n0_ref[...]
        m_scr[...] = m0_ref[...]

    pad = L - tb
    u = qk_ref[...]
    if pad:
        u = jnp.concatenate([u, jnp.zeros((pad, u.shape[1]), F32)], axis=0)
    a = _conv_silu(u, ext_scr[...], cw_ref, cb_ref)
    if not pad:
        ext_scr[...] = u[L - 8:L]

    @pl.when(c_idx == nc - 1)
    def _():
        cst_ref[...] = qk_ref[tb - (CONV_W - 1):tb, :]

    g = gt_ref[...] + bg_ref[...]
    li = g
    lf = jnp.minimum(g, 0.0) - jnp.log1p(jnp.exp(-jnp.abs(g)))
    if pad:
        zpad = jnp.zeros((pad, LANES), F32)
        li = jnp.concatenate([li, zpad + NEG_BIG], axis=0)
        lf = jnp.concatenate([lf, zpad], axis=0)
    row = lax.broadcasted_iota(jnp.int32, (L, LANES), 0)
    lane = lax.broadcasted_iota(jnp.int32, (L, LANES), 1)
    bcum = lf
    shift = 1
    while shift < L:
        bcum = bcum + jnp.where(row >= shift, pltpu.roll(bcum, shift, 0), 0.0)
        shift *= 2
    gates = jnp.where(lane < H_M, li, bcum)
    gates_t = gates.T
    tri = lax.broadcasted_iota(jnp.int32, (L, L), 0) >= lax.broadcasted_iota(jnp.int32, (L, L), 1)
    m_all = m_scr[...]
    lane1 = lax.broadcasted_iota(jnp.int32, (1, LANES), 1)
    m_next = m_all
    vall = vm_ref[...]
    if pad:
        vall = jnp.concatenate([vall, jnp.zeros((pad, vall.shape[1]), BF16)], axis=0)

    for h in range(H_M):
        li_col = gates[:, h:h + 1]
        b_col = gates[:, H_M + h:H_M + h + 1]
        src_row = gates_t[h:h + 1, :] - gates_t[H_M + h:H_M + h + 1, :]
        m_prev = m_all[:, h:h + 1]
        b_last = b_col[L - 1:L, :]
        log_d = jnp.where(tri, b_col + src_row, NEG_BIG)
        inter = b_col + m_prev
        mt = jnp.maximum(inter, jnp.max(log_d, axis=1, keepdims=True))
        q = a[:, h * DH_M:(h + 1) * DH_M]
        k = a[:, W_M + h * DH_M:W_M + (h + 1) * DH_M] * (DH_M ** -0.5)
        v = vall[:, h * DH_M:(h + 1) * DH_M]
        qb = q.astype(BF16)
        s = _dot_nt(qb, k.astype(BF16)) * jnp.exp(log_d - mt)
        ei = jnp.exp(inter - mt)
        c_old = c_scr[h]
        n_old = n_scr[h:h + 1, :]
        num = ei * _dot(qb, c_old.astype(BF16)) + _dot(s.astype(BF16), v)
        den = ei * jnp.sum(q * n_old, axis=1, keepdims=True) + jnp.sum(s, axis=1, keepdims=True)
        hh = num / jnp.maximum(jnp.abs(den), jnp.exp(-mt))
        g_col = b_last - b_col + li_col
        bl = b_last + m_prev
        m_new = jnp.maximum(bl, jnp.max(g_col, axis=0, keepdims=True))
        wg = jnp.exp(g_col - m_new)
        decay = jnp.exp(bl - m_new)
        kw = k * wg
        c_scr[h] = decay * c_old + _dot(kw.T.astype(BF16), v)
        n_scr[h:h + 1, :] = decay * n_old + jnp.sum(kw, axis=0, keepdims=True)
        m_next = jnp.where(lane1 == h, m_new, m_next)
        ms = jnp.mean(hh * hh, axis=1, keepdims=True)
        hn = (hh * lax.rsqrt(ms + EPS)) * gm_ref[h:h + 1, :]
        og = _sigmoid(om_ref[:, h * DH_M:(h + 1) * DH_M])
        h_ref[:, h * DH_M:(h + 1) * DH_M] = (hn[:tb] * og).astype(BF16)

    m_scr[...] = m_next

    @pl.when(c_idx == nc - 1)
    def _():
        cout_ref[...] = c_scr[...]
        nout_ref[...] = n_scr[...]
        mout_ref[...] = m_scr[...]


def _mlstm(qk, vm, om, gt, w_conv, b_conv, b_gates_pad, g_mnorm, conv_buf, c0, n0, m0_pad, batch, seq):
    tb = min(seq, MLSTM_CHUNK)
    L = max(tb, LANES)
    nc = seq // tb
    tok = lambda w: pl.BlockSpec((None, tb, w), lambda b, c: (b * nc + c, 0, 0))
    chunks = lambda a: a.reshape(batch * nc, tb, a.shape[-1])
    const = lambda shape: pl.BlockSpec(shape, lambda b, c: (0,) * len(shape))
    per_b = lambda shape: pl.BlockSpec((None,) + shape, lambda b, c: (b,) + (0,) * len(shape))
    h, cst, c_out, n_out, m_out = pl.pallas_call(
        functools.partial(_mlstm_body, tb=tb, L=L, nc=nc),
        grid=(batch, nc),
        in_specs=[tok(2 * W_M), tok(W_M), tok(W_M), tok(LANES), const((CONV_W, 2 * W_M)), const((1, 2 * W_M)),
                  const((1, LANES)), const((H_M, DH_M)), per_b((CONV_W - 1, 2 * W_M)),
                  per_b((H_M, DH_M, DH_M)), per_b((H_M, DH_M)), per_b((1, LANES))],
        out_specs=[tok(W_M), per_b((CONV_W - 1, 2 * W_M)), per_b((H_M, DH_M, DH_M)), per_b((H_M, DH_M)),
                   per_b((1, LANES))],
        out_shape=[jax.ShapeDtypeStruct((batch * nc, tb, W_M), BF16),
                   jax.ShapeDtypeStruct((batch, CONV_W - 1, 2 * W_M), F32),
                   jax.ShapeDtypeStruct((batch, H_M, DH_M, DH_M), F32),
                   jax.ShapeDtypeStruct((batch, H_M, DH_M), F32),
                   jax.ShapeDtypeStruct((batch, 1, LANES), F32)],
        scratch_shapes=[pltpu.VMEM((8, 2 * W_M), F32), pltpu.VMEM((H_M, DH_M, DH_M), F32),
                        pltpu.VMEM((H_M, DH_M), F32), pltpu.VMEM((1, LANES), F32)],
        compiler_params=_params(("arbitrary", "arbitrary")),
        name="mlstm",
    )(chunks(qk), chunks(vm), chunks(om), chunks(gt), w_conv, b_conv, b_gates_pad, g_mnorm, conv_buf, c0, n0,
      m0_pad)
    return h.reshape(batch * seq, W_M), cst, c_out, n_out, m_out


def _merge_body(oa_ref, hm_ref, ga_ref, gb_ref, x_ref, gt1_ref, sc2_ref, sh2_ref, g2_ref, wa_ref, wb_ref, wo_ref,
                wr_ref, br_ref, *rest):
    x2_ref, h2_ref, route_ref, cnt_ref = rest[-4:]
    ya = _dot(oa_ref[...], wa_ref[...])
    yb = _dot(hm_ref[...], wb_ref[...])
    mix = _sigmoid(ga_ref[...]) * ya + _sigmoid(gb_ref[...]) * yb
    y = _dot(mix.astype(BF16), wo_ref[...])
    x2 = x_ref[...] + gt1_ref[...] * y
    x2_ref[...] = x2
    ms = jnp.mean(x2 * x2, axis=1, keepdims=True)
    h2 = (x2 * lax.rsqrt(ms + EPS)) * g2_ref[...] * (1.0 + sc2_ref[...]) + sh2_ref[...]
    h2b = h2.astype(BF16)
    h2_ref[...] = h2b

    tm = h2b.shape[0]
    logits_t = (_dot(h2b, wr_ref[...]) + br_ref[...]).T[:N_EXP, :]
    row = lax.broadcasted_iota(jnp.int32, (N_EXP, tm), 0)
    row_f = row.astype(F32)
    work = logits_t
    vals, hots = [], []
    for _ in range(TOP_K):
        mx = jnp.max(work, axis=0, keepdims=True)
        idx = jnp.min(jnp.where(work == mx, row, N_EXP), axis=0, keepdims=True)
        hot = row == idx
        vals.append(mx)
        hots.append(hot)
        work = jnp.where(hot, 2.0 * NEG_BIG, work)
    es = [jnp.exp(v - vals[0]) for v in vals]
    den = es[0]
    for e in es[1:]:
        den = den + e
    sel_t = jnp.zeros((N_EXP, tm), F32)
    for hot in hots:
        sel_t = jnp.where(hot, 1.0, sel_t)
    r_i = lax.broadcasted_iota(jnp.int32, (tm, tm), 0)
    c_i = lax.broadcasted_iota(jnp.int32, (tm, tm), 1)
    rank_t = _dot(sel_t.astype(BF16), jnp.where(r_i < c_i, 1.0, 0.0).astype(BF16))
    sub = lax.broadcasted_iota(jnp.int32, (ROUTE_SEL, tm), 0)
    head = jnp.zeros((ROUTE_SEL, tm), F32)
    for k in range(TOP_K):
        e_k = jnp.sum(jnp.where(hots[k], row_f, 0.0), axis=0, keepdims=True)
        r_k = jnp.sum(jnp.where(hots[k], rank_t, 0.0), axis=0, keepdims=True)
        head = jnp.where(sub == k, e_k, head)
        head = jnp.where(sub == TOP_K + k, es[k] / den, head)
        head = jnp.where(sub == 2 * TOP_K + k, r_k, head)
    pad = jnp.zeros((LANES - ROUTE_SEL - N_EXP, tm), F32)
    route = jnp.concatenate([head, sel_t, pad], axis=0).T
    route_ref[...] = route
    cnt_ref[...] = jnp.sum(route, axis=0, keepdims=True)


def _merge(oa, hm, ga, gb, x, gt1, sc2, sh2, g2, wa, wb, wo, wr, br, rows_per_mod, tile0, n_all, prev=None):
    n = x.shape[0]
    nt = n // TOK_TILE
    nt_all = n_all // TOK_TILE
    tiles_per_group = nt // gt1.shape[0]
    tok = lambda w: pl.BlockSpec((TOK_TILE, w), lambda i: (i, 0))
    mod = pl.BlockSpec((None, rows_per_mod, D_MODEL), lambda i: (i // tiles_per_group, 0, 0))
    res = lambda shape: pl.BlockSpec(shape, lambda i: (0, 0), pipeline_mode=pl.Buffered(1))
    out_tok = lambda w: pl.BlockSpec((TOK_TILE, w), lambda i: (tile0 + i, 0))
    in_specs = [tok(W_VA), tok(W_M), tok(D_MODEL), tok(D_MODEL), tok(D_MODEL), mod, mod, mod,
                pl.BlockSpec((1, D_MODEL), lambda i: (0, 0)),
                res((W_VA, D_MODEL)), res((W_M, D_MODEL)), res((D_MODEL, D_MODEL)), res((D_MODEL, LANES)),
                pl.BlockSpec((1, LANES), lambda i: (0, 0))]
    args = [oa, hm, ga, gb, x, gt1, sc2, sh2, g2, wa, wb, wo, wr, br]
    aliases = {}
    if prev is not None:
        in_specs += [pl.BlockSpec(memory_space=pl.ANY)] * 4
        aliases = {len(args) + j: j for j in range(4)}
        args += list(prev)
    return pl.pallas_call(
        _merge_body,
        grid=(nt,),
        in_specs=in_specs,
        out_specs=[out_tok(D_MODEL), out_tok(D_MODEL), out_tok(LANES),
                   pl.BlockSpec((None, 1, LANES), lambda i: (tile0 + i, 0, 0))],
        out_shape=[jax.ShapeDtypeStruct((n_all, D_MODEL), F32), jax.ShapeDtypeStruct((n_all, D_MODEL), BF16),
                   jax.ShapeDtypeStruct((n_all, LANES), F32), jax.ShapeDtypeStruct((nt_all, 1, LANES), F32)],
        input_output_aliases=aliases,
        compiler_params=_params(("arbitrary",)),
        name="merge",
    )(*args)


def _segment_copies(src, dst, sem, src_row, dst_row, n_groups, max_groups):
    out = []
    bit = 1
    while bit * 2 <= max_groups:
        bit *= 2
    while bit >= 1:
        off = (n_groups // (2 * bit)) * (2 * bit) * ROW_ALIGN
        rows = bit * ROW_ALIGN
        cp = pltpu.make_async_copy(src.at[pl.ds(pl.multiple_of(src_row + off, ROW_ALIGN), rows)],
                                   dst.at[pl.ds(pl.multiple_of(dst_row + off, ROW_ALIGN), rows)], sem)
        out.append(((n_groups // bit) % 2 == 1, cp))
        bit //= 2
    return out


def _run_copies(copies):
    for pred, cp in copies:
        pl.when(pred)(cp.start)
    for pred, cp in copies:
        pl.when(pred)(cp.wait)


def _slot_rows(route_t, loff_col, k):
    e_row = route_t[k:k + 1, :]
    r_row = route_t[2 * TOP_K + k:2 * TOP_K + k + 1, :]
    sub = lax.broadcasted_iota(jnp.int32, (LANES, route_t.shape[1]), 0).astype(F32)
    return jnp.sum(jnp.where(sub == e_row, loff_col, 0.0), axis=0, keepdims=True) + r_row


def _for_groups(n_groups, table_ref, base, make_copy, start):
    def body(i, carry):
        for j in range(GROUP_UNROLL):
            g = i * GROUP_UNROLL + j
            cp = make_copy(g, table_ref[base + g])
            if start:
                cp.start()
            else:
                cp.wait()
        return carry
    lax.fori_loop(0, (n_groups + GROUP_UNROLL - 1) // GROUP_UNROLL, body, 0)


def _dispatch_body(ng_ref, dt_ref, lo_ref, sv_ref, rm_ref, tail_ref, h2_ref, route_ref, soff_ref, xs_ref,
                   loc_scr, carry_scr, zero_scr, sem, *, nt):
    t = pl.program_id(0)
    buf = t % 2

    def copies_of(tt):
        def make(g, row):
            return pltpu.make_async_copy(
                loc_scr.at[tt % 2, pl.ds(pl.multiple_of(g * ROW_ALIGN, ROW_ALIGN), ROW_ALIGN)],
                xs_ref.at[pl.ds(pl.multiple_of(row, ROW_ALIGN), ROW_ALIGN)], sem.at[tt % 2])
        return ng_ref[tt], dt_ref, tt * GROUPS, make

    @pl.when(t == 0)
    def _():
        carry_scr[...] = jnp.zeros(carry_scr.shape, BF16)
        loc_scr[...] = jnp.zeros(loc_scr.shape, BF16)

    @pl.when(t >= 2)
    def _():
        _for_groups(*copies_of(t - 2), start=False)

    route_t = route_ref[...].T
    soff_col = soff_ref[...]
    slots = [_slot_rows(route_t, soff_col, k) for k in range(TOP_K)]

    def sort_rows(m):
        r_i = lax.broadcasted_iota(jnp.int32, (m, TOK_TILE), 0).astype(F32)
        onehot = jnp.zeros((m, TOK_TILE), F32)
        for k in range(TOP_K):
            onehot = jnp.where(r_i == slots[k], 1.0, onehot)
        loc_scr[buf, 0:m, :] = _dot(onehot.astype(BF16), h2_ref[...]).astype(BF16)

    fits = ng_ref[t] * ROW_ALIGN <= M_LOC_SHORT
    pl.when(fits)(lambda: sort_rows(M_LOC_SHORT))
    pl.when(jnp.logical_not(fits))(lambda: sort_rows(M_LOC))

    for e in range(N_EXP):
        lo = pl.multiple_of(lo_ref[t * N_EXP + e], ROW_ALIGN)
        sv = pl.multiple_of(sv_ref[t * N_EXP + e], ROW_ALIGN)
        loc_scr[buf, pl.ds(lo, ROW_ALIGN), :] = loc_scr[buf, pl.ds(lo, ROW_ALIGN), :] + carry_scr[e]
        pending = loc_scr[buf, pl.ds(sv, ROW_ALIGN), :]
        carry_scr[e] = jnp.where(rm_ref[t * N_EXP + e] > 0, pending, jnp.zeros_like(pending))

    _for_groups(*copies_of(t), start=True)

    @pl.when(t == nt - 1)
    def _():
        if nt >= 2:
            _for_groups(*copies_of(t - 1), start=False)
        _for_groups(*copies_of(t), start=False)
        zero_scr[...] = jnp.zeros(zero_scr.shape, BF16)
        tails = []
        for e in range(N_EXP):
            tails += _segment_copies(zero_scr, xs_ref, sem.at[0], 0, tail_ref[e], tail_ref[N_EXP + e],
                                     FFN_TILE // ROW_ALIGN - 1)
        _run_copies(tails)


def _dispatch(meta, h2, route, soff_col):
    nt = h2.shape[0] // TOK_TILE
    grid_spec = pltpu.PrefetchScalarGridSpec(
        num_scalar_prefetch=6,
        grid=(nt,),
        in_specs=[pl.BlockSpec((TOK_TILE, D_MODEL), lambda t, *_: (t, 0)),
                  pl.BlockSpec((TOK_TILE, LANES), lambda t, *_: (t, 0)),
                  pl.BlockSpec((None, LANES, 1), lambda t, *_: (t, 0, 0))],
        out_specs=pl.BlockSpec(memory_space=pl.ANY),
        scratch_shapes=[pltpu.VMEM((2, M_LOC + ROW_ALIGN, D_MODEL), BF16),
                        pltpu.VMEM((N_EXP, ROW_ALIGN, D_MODEL), BF16),
                        pltpu.VMEM((FFN_TILE, D_MODEL), BF16), pltpu.SemaphoreType.DMA((2,))],
    )
    return pl.pallas_call(
        functools.partial(_dispatch_body, nt=nt),
        grid_spec=grid_spec,
        out_shape=jax.ShapeDtypeStruct((meta["rows"] + 2 * M_LOC, D_MODEL), BF16),
        compiler_params=_params(("arbitrary",)),
        name="moe_dispatch",
    )(meta["n_groups"], meta["dispatch_rows"], meta["slot_start"], meta["carry_start"], meta["carry_rows"],
      meta["tail"], h2, route, soff_col)


def _ffn_body(be_ref, nu_ref, nx_ref, sl_ref, x_ref, bgu_ref, bd_ref, wgu_hbm, wd_hbm, y_ref,
              wgu_f32, wd_f32, wgu_scr, wd_scr, sem):
    i = pl.program_id(0)
    expert = be_ref[i]
    slot = sl_ref[i]
    prev = be_ref[jnp.maximum(i - 1, 0)]

    def weight_copies(e, s):
        return (pltpu.make_async_copy(wgu_hbm.at[e], wgu_f32.at[s], sem.at[s]),
                pltpu.make_async_copy(wd_hbm.at[e], wd_f32.at[s], sem.at[s]))

    @pl.when(i == 0)
    def _():
        for cp in weight_copies(expert, slot):
            cp.start()

    @pl.when(jnp.logical_and(i < nu_ref[0], jnp.logical_or(i == 0, expert != prev)))
    def _():
        for cp in weight_copies(expert, slot):
            cp.wait()
        wgu_scr[...] = wgu_f32[slot].astype(BF16)
        wd_scr[...] = wd_f32[slot].astype(BF16)

        @pl.when(nx_ref[i] >= 0)
        def _():
            for cp in weight_copies(nx_ref[i], 1 - slot):
                cp.start()

    @pl.when(i < nu_ref[0])
    def _():
        gu = _dot(x_ref[...], wgu_scr[...]) + bgu_ref[...]
        gate = jnp.minimum(gu[:, :D_FF], SWIGLU_LIMIT)
        up = jnp.clip(gu[:, D_FF:], -SWIGLU_LIMIT, SWIGLU_LIMIT)
        act = (up + 1.0) * gate * _sigmoid(SWIGLU_ALPHA * gate)
        y_ref[...] = (_dot(act.astype(BF16), wd_scr[...]) + bd_ref[...]).astype(BF16)


def _ffn(meta, xs, w_gu, b_gu, w_down, b_down):
    rows = meta["rows"]
    nblk = rows // FFN_TILE
    row_blk = pl.BlockSpec((FFN_TILE, D_MODEL), lambda i, be, nu, *_: (jnp.minimum(i, nu[0] - 1), 0))
    grid_spec = pltpu.PrefetchScalarGridSpec(
        num_scalar_prefetch=4,
        grid=(nblk,),
        in_specs=[row_blk,
                  pl.BlockSpec((None, 1, 2 * D_FF), lambda i, be, *_: (be[i], 0, 0)),
                  pl.BlockSpec((None, 1, D_MODEL), lambda i, be, *_: (be[i], 0, 0)),
                  pl.BlockSpec(memory_space=pl.ANY), pl.BlockSpec(memory_space=pl.ANY)],
        out_specs=row_blk,
        scratch_shapes=[pltpu.VMEM((2, D_MODEL, 2 * D_FF), F32), pltpu.VMEM((2, D_FF, D_MODEL), F32),
                        pltpu.VMEM((D_MODEL, 2 * D_FF), BF16), pltpu.VMEM((D_FF, D_MODEL), BF16),
                        pltpu.SemaphoreType.DMA((2,))],
    )
    return pl.pallas_call(
        _ffn_body,
        grid_spec=grid_spec,
        out_shape=jax.ShapeDtypeStruct((rows, D_MODEL), BF16),
        compiler_params=_params(("arbitrary",)),
        name="moe_ffn",
    )(meta["blk_exp"], meta["n_used"], meta["next_exp"], meta["blk_slot"], xs,
      b_gu.reshape(N_EXP, 1, 2 * D_FF), b_down.reshape(N_EXP, 1, D_MODEL), w_gu, w_down)


def _combine_body(ng_ref, ct_ref, route_ref, soff_ref, x2_ref, gtp_ref, gts_ref, gf_ref, ys_ref,
                  yp_ref, ysm_ref, loc_scr, moe_scr, sem, *, nt, nt_prompt, final):
    t = pl.program_id(0)
    buf = t % 2

    def copies_of(tt):
        def make(g, row):
            return pltpu.make_async_copy(
                ys_ref.at[pl.ds(pl.multiple_of(row, ROW_ALIGN), ROW_ALIGN)],
                loc_scr.at[tt % 2, pl.ds(pl.multiple_of(g * ROW_ALIGN, ROW_ALIGN), ROW_ALIGN)], sem.at[tt % 2])
        return ng_ref[tt], ct_ref, tt * GROUPS, make

    @pl.when(t == 0)
    def _():
        loc_scr[...] = jnp.zeros(loc_scr.shape, BF16)
        _for_groups(*copies_of(0), start=True)

    @pl.when(t + 1 < nt)
    def _():
        _for_groups(*copies_of(t + 1), start=True)

    _for_groups(*copies_of(t), start=False)

    route = route_ref[...]
    soff_row = soff_ref[...]
    lane = lax.broadcasted_iota(jnp.int32, (TOK_TILE, LANES), 1).astype(F32)
    slots = [jnp.sum(jnp.where(lane == route[:, k:k + 1], soff_row, 0.0), axis=1, keepdims=True)
             + route[:, 2 * TOP_K + k:2 * TOP_K + k + 1] for k in range(TOP_K)]

    def weighted_sum(m):
        c_i = lax.broadcasted_iota(jnp.int32, (TOK_TILE, m), 1).astype(F32)
        wmat = jnp.zeros((TOK_TILE, m), F32)
        for k in range(TOP_K):
            wmat = jnp.where(c_i == slots[k], route[:, TOP_K + k:TOP_K + k + 1], wmat)
        moe_scr[...] = _dot(wmat.astype(BF16), loc_scr[buf, 0:m, :])

    fits = ng_ref[t] * ROW_ALIGN <= M_LOC_SHORT
    pl.when(fits)(lambda: weighted_sum(M_LOC_SHORT))
    pl.when(jnp.logical_not(fits))(lambda: weighted_sum(M_LOC))

    gate = jnp.where(t >= nt_prompt, gts_ref[...], gtp_ref[...])
    xo = x2_ref[...] + gate * moe_scr[...]
    if final:
        ms = jnp.mean(xo * xo, axis=1, keepdims=True)
        xo = (xo * lax.rsqrt(ms + EPS)) * gf_ref[...]

    @pl.when(t < nt_prompt)
    def _():
        yp_ref[...] = xo

    @pl.when(t >= nt_prompt)
    def _():
        ysm_ref[...] = xo


def _combine(meta, route, soff_row, x2, gt2_p, gt2_s, g_final, ys, n_prompt, n_sample, final):
    nt = x2.shape[0] // TOK_TILE
    nt_prompt = n_prompt // TOK_TILE
    tiles_per_batch = nt_prompt // gt2_p.shape[0]
    grid_spec = pltpu.PrefetchScalarGridSpec(
        num_scalar_prefetch=2,
        grid=(nt,),
        in_specs=[pl.BlockSpec((TOK_TILE, LANES), lambda t, *_: (t, 0)),
                  pl.BlockSpec((None, 1, LANES), lambda t, *_: (t, 0, 0)),
                  pl.BlockSpec((TOK_TILE, D_MODEL), lambda t, *_: (t, 0)),
                  pl.BlockSpec((None, 1, D_MODEL),
                               lambda t, *_: (jnp.minimum(t, nt_prompt - 1) // tiles_per_batch, 0, 0)),
                  pl.BlockSpec((TOK_TILE, D_MODEL), lambda t, *_: (0, 0)),
                  pl.BlockSpec((1, D_MODEL), lambda t, *_: (0, 0)),
                  pl.BlockSpec(memory_space=pl.ANY)],
        out_specs=[pl.BlockSpec((TOK_TILE, D_MODEL), lambda t, *_: (jnp.minimum(t, nt_prompt - 1), 0)),
                   pl.BlockSpec((TOK_TILE, D_MODEL), lambda t, *_: (0, 0))],
        scratch_shapes=[pltpu.VMEM((2, M_LOC, D_MODEL), BF16), pltpu.VMEM((TOK_TILE, D_MODEL), F32),
                        pltpu.SemaphoreType.DMA((2,))],
    )
    return pl.pallas_call(
        functools.partial(_combine_body, nt=nt, nt_prompt=nt_prompt, final=final),
        grid_spec=grid_spec,
        out_shape=[jax.ShapeDtypeStruct((n_prompt, D_MODEL), F32), jax.ShapeDtypeStruct((n_sample, D_MODEL), F32)],
        compiler_params=_params(("arbitrary",)),
        name="moe_combine",
    )(meta["n_groups"], meta["combine_rows"], route, soff_row, x2, gt2_p, gt2_s, g_final, ys)


def _moe_offsets(cnt):
    nt = cnt.shape[0]
    ra = ROW_ALIGN
    prefix = jnp.cumsum(cnt, axis=0) - cnt
    total = jnp.sum(cnt, axis=0)
    pending = prefix % ra
    used = pending + cnt
    seg = (used + ra - 1) // ra * ra
    lo = jnp.cumsum(seg, axis=1) - seg
    n_groups = jnp.sum(seg, axis=1) // ra
    gpad = (total + FFN_TILE - 1) // FFN_TILE * FFN_TILE
    gstart = jnp.cumsum(gpad) - gpad
    base = gstart[None, :] + prefix // ra * ra
    last = (jnp.arange(nt) == nt - 1)[:, None]
    n_write = jnp.where(last, seg // ra, used // ra)
    carry_start = lo + used // ra * ra
    carry_rows = jnp.where(last, 0, used % ra)
    g = jnp.arange(GROUPS)
    slot_end = (lo + seg) // ra
    owner = jnp.minimum(jnp.sum(g[None, :, None] >= slot_end[:, None, :], axis=2), N_EXP - 1)
    pick = lambda a: jnp.sum(jnp.where(owner[:, :, None] == jnp.arange(N_EXP), a[:, None, :], 0), axis=2)
    k = g[None, :] - pick(lo) // ra
    row = pick(base) + k * ra
    valid = g[None, :] < n_groups[:, None]
    rows = (nt * TOK_TILE * TOP_K + N_EXP * (FFN_TILE - 1) + FFN_TILE - 1) // FFN_TILE * FFN_TILE
    spare = rows + (jnp.arange(nt) % 2)[:, None] * M_LOC + g[None, :] * ra
    combine_rows = jnp.where(valid, row, 0)
    dispatch_rows = jnp.where(valid & (k < pick(n_write)), row, spare)

    nblk_e = gpad // FFN_TILE
    blk_end = jnp.cumsum(nblk_e)
    n_used = jnp.maximum(blk_end[-1], 1)
    blk = jnp.minimum(jnp.arange(rows // FFN_TILE, dtype=jnp.int32), n_used - 1)
    blk_exp = jnp.minimum(jnp.sum(blk[:, None] >= blk_end[None, :], axis=1), N_EXP - 1)
    experts = jnp.arange(N_EXP)
    following = lax.cummin(jnp.where(nblk_e > 0, experts, N_EXP), reverse=True)
    next_of = jnp.concatenate([following[1:], jnp.full((1,), N_EXP, following.dtype)])
    next_of = jnp.where(next_of >= N_EXP, -1, next_of)
    parity = (jnp.cumsum(nblk_e > 0) - 1) % 2
    pick_e = lambda a: jnp.sum(jnp.where(blk_exp[:, None] == experts[None, :], a[None, :], 0), axis=1)
    total16 = (total + ra - 1) // ra * ra
    tail = jnp.concatenate([gstart + total16, (gpad - total16) // ra])
    i32 = lambda a: a.astype(jnp.int32).reshape(-1)
    return dict(n_groups=i32(n_groups), dispatch_rows=i32(dispatch_rows), combine_rows=i32(combine_rows),
                slot_start=i32(lo), carry_start=i32(carry_start), carry_rows=i32(carry_rows), tail=i32(tail),
                blk_exp=i32(blk_exp), n_used=i32(n_used), next_exp=i32(pick_e(next_of)), blk_slot=i32(pick_e(parity)),
                slot_off=(lo + pending).astype(F32), rows=rows)


def _split_w_in(w_in):
    g0 = C_GATES
    return (w_in[:, :g0].astype(BF16), w_in[:, g0 + N_GATE:].astype(BF16),
            _pad_lanes(w_in[:, g0:g0 + N_GATE]).astype(BF16))


def _pad_lanes(a, value=0.0):
    return jnp.pad(a, [(0, 0)] * (a.ndim - 1) + [(0, LANES - a.shape[-1])], constant_values=value)


def kernel(x_prompt, x_sample, c_prompt, c_sample, cache_k, cache_v, state_conv, state_C, state_n, state_m, page_table, w_ada, b_ada, g_norm1, g_norm2, w_in, b_gates, lambda_q1, lambda_k1, lambda_q2, lambda_k2, g_subln, w_conv, b_conv, g_mnorm, w_up_a, w_up_b, w_out, w_router, b_router, w_gu, b_gu, w_down, b_down, g_final):
    B, S, D = x_prompt.shape
    Bd, Td, _ = x_sample.shape
    depth = w_in.shape[0]
    n_pool = cache_k.shape[1]
    past_len = page_table.shape[1] * PAGE_SIZE
    n_p, n_s = B * S, Bd * Td
    n_all = n_p + n_s
    assert D == D_MODEL and n_s == TOK_TILE and S % MLSTM_CHUNK == 0 and n_p % TOK_TILE == 0
    assert page_table.shape[1] % PAGES_PER_STEP == 0

    cos_p, sin_p = _rope_tables(np.arange(S))
    cos_s, sin_s = _rope_tables(np.tile(past_len + np.arange(Td), Bd))
    hp = x_prompt.reshape(n_p, D)
    hs = x_sample.reshape(n_s, D)
    c_all = jnp.concatenate([c_prompt, c_sample], axis=0)
    outs = [[] for _ in range(12)]

    for l in range(depth):
        lam_init = 0.8 - 0.6 * math.exp(-0.3 * l)
        mod = _ada(c_all, w_ada[l], b_ada[l])
        mods = [mod[:, j * D:(j + 1) * D] for j in range(6)]
        mp = [m[:B].reshape(B, 1, D) for m in mods]
        ms_ = [jnp.repeat(m[B:], Td, axis=0).reshape(1, n_s, D) for m in mods]
        w_parts = _split_w_in(w_in[l])
        g1 = g_norm1[l].reshape(1, D)
        lam_vec = jnp.stack([lambda_q1[l], lambda_k1[l], lambda_q2[l], lambda_k2[l]])
        gsub = g_subln[l].reshape(1, DV_A)
        bg = _pad_lanes(b_gates[l].reshape(1, N_GATE))
        cw, cb = w_conv[l], b_conv[l].reshape(1, 2 * W_M)

        (q_p, _, kb_p, v_p, _, qk_p, vm_p, om_p, ga_p, gb_p, gt_p, vt_p, kt_p) = _inproj(
            hp, mp[1], mp[0], g1, cos_p, sin_p, w_parts, 1, PROJ_TILE)
        (q_s, k_s, kb_s, v_s, vb_s, qk_s, vm_s, om_s, ga_s, gb_s, gt_s, _, _) = _inproj(
            hs, ms_[1], ms_[0], g1, cos_s, sin_s, w_parts, n_s, n_s)
        k_p = jnp.transpose(kt_p.reshape(B, H_A, 2, DK_A, S), (0, 4, 1, 2, 3))

        kt_pool = jnp.transpose(cache_k[l], (0, 2, 3, 4, 1)).reshape(n_pool, W_QA, PAGE_SIZE)
        v_pool = cache_v[l].reshape(n_pool, PAGE_SIZE * H_A, DV_A)
        oa_p, oa_s = _attention(q_p, kb_p, vt_p, q_s, kb_s, vb_s, kt_pool, v_pool, page_table, lam_vec, gsub,
                                B, S, lam_init, Td)

        zeros = lambda *shape: jnp.zeros(shape, F32)
        hm_p, cst_p, C_p, nn_p, m_p = _mlstm(qk_p, vm_p, om_p, gt_p, cw, cb, bg, g_mnorm[l],
                                             zeros(B, CONV_W - 1, 2 * W_M), zeros(B, H_M, DH_M, DH_M),
                                             zeros(B, H_M, DH_M), zeros(B, 1, LANES), B, S)
        hm_s, cst_s, C_s, nn_s, m_s = _mlstm(qk_s, vm_s, om_s, gt_s, cw, cb, bg, g_mnorm[l],
                                             state_conv[l], state_C[l], state_n[l],
                                             _pad_lanes(state_m[l]).reshape(Bd, 1, LANES), Bd, Td)

        wa, wb, wo = w_up_a[l].astype(BF16), w_up_b[l].astype(BF16), w_out[l].astype(BF16)
        wr = _pad_lanes(w_router[l]).astype(BF16)
        br = _pad_lanes(b_router[l].reshape(1, N_EXP))
        g2 = g_norm2[l].reshape(1, D)
        part = _merge(oa_p, hm_p, ga_p, gb_p, hp, mp[2], mp[4], mp[3], g2, wa, wb, wo, wr, br, 1, 0, n_all)
        x2, h2, route, cnt = _merge(oa_s, hm_s, ga_s, gb_s, hs, ms_[2], ms_[4], ms_[3], g2, wa, wb, wo, wr, br,
                                    n_s, n_p // TOK_TILE, n_all, prev=part)

        meta = _moe_offsets(jnp.round(cnt[:, 0, ROUTE_SEL:ROUTE_SEL + N_EXP]).astype(jnp.int32))
        soff = _pad_lanes(meta["slot_off"])
        xs = _dispatch(meta, h2, route, soff[:, :, None])
        ys = _ffn(meta, xs, w_gu[l], b_gu[l], w_down[l], b_down[l])
        final = l == depth - 1
        hp, hs = _combine(meta, route, soff[:, None, :], x2, mp[5], ms_[5][0], g_final.reshape(1, D), ys,
                          n_p, n_s, final)

        for j, a in enumerate([k_p, v_p.reshape(B, S, H_A, DV_A), cst_p, C_p, nn_p,
                               m_p[:, 0, :H_M],
                               k_s.reshape(Bd, Td, H_A, 2, DK_A), v_s.reshape(Bd, Td, H_A, DV_A), cst_s, C_s, nn_s,
                               m_s[:, 0, :H_M]]):
            outs[j].append(a)

    return (hp.reshape(B, S, D), hs.reshape(Bd, Td, D)) + tuple(jnp.stack(o) for o in outs)
```

```python
import functools
import math

import numpy as np
import jax
import jax.numpy as jnp
from jax import lax
from jax.experimental import pallas as pl
from jax.experimental.pallas import tpu as pltpu

F32 = jnp.float32
BF16 = jnp.bfloat16

D_MODEL = 1024
H_A = 4
DK_A = 64
DV_A = 2 * DK_A
ROPE_THETA = 10000.0
H_M = 4
DH_M = 128
CONV_W = 4
N_EXP = 32
TOP_K = 4
D_FF = D_MODEL
SWIGLU_LIMIT = 7.0
SWIGLU_ALPHA = 1.702
EPS = 1e-6
PAGE_SIZE = 128

W_QA = H_A * 2 * DK_A
W_VA = H_A * DV_A
W_M = H_M * DH_M
N_GATE = 2 * H_M

LANES = 128
ROW_ALIGN = 16
TOK_TILE = 256
PROJ_TILE = 512
FFN_TILE = 512
ATT_TILE = 256
ATT_HEADS = H_A
MLSTM_CHUNK = 256
PAGES_PER_STEP = 16
CHUNKS_PER_STEP = 2
NEG_BIG = -1e30
LOG2_E = math.log2(math.e)
VMEM_LIMIT = 56 * 1024 * 1024

C_QA, C_KA, C_VA, C_QK, C_VM, C_OM, C_GATES = 0, 512, 1024, 1536, 2560, 3072, 3584
M_LOC = ((TOK_TILE * TOP_K + 2 * N_EXP * (ROW_ALIGN - 1)) + 255) // 256 * 256
GROUPS = M_LOC // ROW_ALIGN
M_LOC_SHORT = TOK_TILE * TOP_K + N_EXP * ROW_ALIGN
ROUTE_SEL = 32
GROUP_UNROLL = 4


def _dot(a, b):
    return jnp.dot(a, b, preferred_element_type=F32)


def _dot_nt(a, b):
    return lax.dot_general(a, b, (((1,), (1,)), ((), ())), preferred_element_type=F32)


def _sigmoid(x):
    return 0.5 * jnp.tanh(0.5 * x) + 0.5


def _start_all(copies):
    for cp in copies:
        cp.start()


def _params(sem):
    return pltpu.CompilerParams(dimension_semantics=sem, vmem_limit_bytes=VMEM_LIMIT)


def _ada_body(c_ref, w_ref, b_ref, o_ref):
    c = c_ref[...]
    s = c * _sigmoid(c)
    s_hi = s.astype(BF16)
    s_lo = (s - s_hi.astype(F32)).astype(BF16)
    w = w_ref[...]
    w_hi = w.astype(BF16)
    w_lo = (w - w_hi.astype(F32)).astype(BF16)
    o_ref[...] = _dot(s_hi, w_hi) + _dot(s_lo, w_hi) + _dot(s_hi, w_lo) + b_ref[...]


def _ada(c_all, w_ada, b_ada):
    rows = c_all.shape[0]
    n_out = w_ada.shape[1]
    blk = D_MODEL
    return pl.pallas_call(
        _ada_body,
        grid=(n_out // blk,),
        in_specs=[pl.BlockSpec((rows, D_MODEL), lambda j: (0, 0)),
                  pl.BlockSpec((D_MODEL, blk), lambda j: (0, j)),
                  pl.BlockSpec((1, blk), lambda j: (0, j))],
        out_specs=pl.BlockSpec((rows, blk), lambda j: (0, j)),
        out_shape=jax.ShapeDtypeStruct((rows, n_out), F32),
        compiler_params=_params(("arbitrary",)),
        name="ada",
    )(c_all, w_ada, b_ada.reshape(1, n_out))


def _rope(z, cos, sin):
    lane = lax.broadcasted_iota(jnp.int32, (z.shape[0], LANES), 1)
    first_half = (lane % DK_A) < (DK_A // 2)
    out = []
    for h in range(H_A):
        xh = z[:, h * LANES:(h + 1) * LANES]
        partner = jnp.where(first_half, pltpu.roll(xh, LANES - DK_A // 2, 1), pltpu.roll(xh, DK_A // 2, 1))
        out.append(xh * cos + partner * sin)
    return jnp.concatenate(out, axis=1)


def _conv_silu(u, ext, cw_ref, cb_ref):
    rows = u.shape[0]
    full = jnp.concatenate([ext, u], axis=0)
    conv = cb_ref[...] + cw_ref[CONV_W - 1:CONV_W, :] * u
    for j in range(CONV_W - 1):
        conv = conv + cw_ref[j:j + 1, :] * pltpu.roll(full, CONV_W - 1 - j, 0)[8:8 + rows]
    return conv * _sigmoid(conv)


def _inproj_body(x_ref, sc_ref, sh_ref, g_ref, cos_ref, sin_ref, w_ref, wg_ref, wt_ref,
                 q_ref, k_ref, kb_ref, v_ref, vb_ref, qk_ref, vm_ref, om_ref, ga_ref, gb_ref, gt_ref, vt_ref, kt_ref):
    x = x_ref[...]
    ms = jnp.mean(x * x, axis=1, keepdims=True)
    h = (x * lax.rsqrt(ms + EPS)) * g_ref[...] * (1.0 + sc_ref[...]) + sh_ref[...]
    hb = h.astype(BF16)
    cos = cos_ref[...]
    sin = sin_ref[...]

    def seg(lo, n):
        return _dot(hb, w_ref[:, lo:lo + n])

    q = _rope(seg(C_QA, W_QA), cos, sin) * (DK_A ** -0.5 * LOG2_E)
    q_ref[...] = q.astype(BF16)
    k = _rope(seg(C_KA, W_QA), cos, sin)
    k_ref[...] = k
    kb_ref[...] = k.astype(BF16)
    kt_ref[...] = k.T
    v = seg(C_VA, W_VA)
    for h in range(H_A):
        v_ref[pl.ds(h, v.shape[0], stride=H_A), :] = v[:, h * DV_A:(h + 1) * DV_A]
    vb_ref[...] = v.astype(BF16)
    for s in range(vt_ref.shape[0]):
        vt_ref[s] = v[s * ATT_TILE:(s + 1) * ATT_TILE].T.astype(BF16)
    qk_ref[...] = seg(C_QK, 2 * W_M)
    vm_ref[...] = seg(C_VM, W_M).astype(BF16)
    om_ref[...] = seg(C_OM, W_M)
    ga_ref[...] = _dot(hb, wg_ref[:, :D_MODEL])
    gb_ref[...] = _dot(hb, wg_ref[:, D_MODEL:])
    gt_ref[...] = _dot(hb, wt_ref[...])


def _inproj(x, sc, sh, g1, cos, sin, w_parts, rows_per_mod, tile):
    n = x.shape[0]
    nt = n // tile
    tiles_per_group = nt // sc.shape[0]
    tab_tiles = cos.shape[0] // tile
    slabs = tile // ATT_TILE
    tok = lambda w: pl.BlockSpec((tile, w), lambda i: (i, 0))
    mod = pl.BlockSpec((None, rows_per_mod, D_MODEL), lambda i: (i // tiles_per_group, 0, 0))
    tab = pl.BlockSpec((tile, LANES), lambda i: (i % tab_tiles, 0))
    tok_out = lambda w, dt: (tok(w), jax.ShapeDtypeStruct((n, w), dt))
    outs = [
        tok_out(W_QA, BF16),
        tok_out(W_QA, F32), tok_out(W_QA, BF16),
        (pl.BlockSpec((tile * H_A, DV_A), lambda i: (i, 0)), jax.ShapeDtypeStruct((n * H_A, DV_A), F32)),
        tok_out(W_VA, BF16),
        tok_out(2 * W_M, F32), tok_out(W_M, BF16), tok_out(W_M, F32),
        tok_out(D_MODEL, F32), tok_out(D_MODEL, F32), tok_out(LANES, F32),
        (pl.BlockSpec((slabs, W_VA, ATT_TILE), lambda i: (i, 0, 0)),
         jax.ShapeDtypeStruct((n // ATT_TILE, W_VA, ATT_TILE), BF16)),
        (pl.BlockSpec((None, W_QA, tile), lambda i: (i // tab_tiles, 0, i % tab_tiles)),
         jax.ShapeDtypeStruct((nt // tab_tiles, W_QA, tab_tiles * tile), F32)),
    ]
    return pl.pallas_call(
        _inproj_body,
        grid=(nt,),
        in_specs=[tok(D_MODEL), mod, mod, pl.BlockSpec((1, D_MODEL), lambda i: (0, 0)), tab, tab,
                  *[pl.BlockSpec(w.shape, lambda i: (0, 0), pipeline_mode=pl.Buffered(1)) for w in w_parts]],
        out_specs=[spec for spec, _ in outs],
        out_shape=[shape for _, shape in outs],
        compiler_params=_params(("arbitrary",)),
        name="inproj",
    )(x, sc, sh, g1, cos, sin, *w_parts)


def _rope_tables(pos):
    half = DK_A // 2
    inv = ROPE_THETA ** (-np.arange(half, dtype=np.float64) * 2.0 / DK_A)
    ang = np.asarray(pos, np.float64)[:, None] * inv[None, :]
    cos = np.cos(ang)
    sin = np.sin(ang)
    cos64 = np.concatenate([cos, cos], axis=1)
    sin64 = np.concatenate([-sin, sin], axis=1)
    return (jnp.asarray(np.tile(cos64, (1, LANES // DK_A)), F32),
            jnp.asarray(np.tile(sin64, (1, LANES // DK_A)), F32))


def _lambda_value(lam_ref, lam_init):
    lv = lam_ref[...]
    l1 = jnp.sum(lv[0:1, :] * lv[1:2, :], axis=1, keepdims=True)
    l2 = jnp.sum(lv[2:3, :] * lv[3:4, :], axis=1, keepdims=True)
    return jnp.exp(l1) - jnp.exp(l2) + lam_init


def _subln(o, g, lam_init):
    ms = jnp.mean(o * o, axis=1, keepdims=True)
    return (o * lax.rsqrt(ms + EPS)) * g * (1.0 - lam_init)


def _prompt_tile(q_ref, k_ref, vt_ref, lam, g_ref, o_ref, m_scr, acc_scr, i, lam_init):
    tq = ATT_TILE
    lane = lax.broadcasted_iota(jnp.int32, (tq, LANES), 1)
    qs = []
    for hh in range(ATT_HEADS):
        q = q_ref[:, hh * LANES:(hh + 1) * LANES]
        zero = jnp.zeros_like(q)
        qs.append(jnp.concatenate([jnp.where(lane < DK_A, q, zero), jnp.where(lane >= DK_A, q, zero)], axis=0))
    for hh in range(ATT_HEADS):
        m_scr[hh][...] = jnp.full(m_scr[hh].shape, NEG_BIG, F32)
        acc_scr[hh][...] = jnp.zeros(acc_scr[hh].shape, F32)
    ones = jnp.ones((ROW_ALIGN, tq), BF16)

    def scores(hh, j):
        start = pl.multiple_of(j * tq, tq)
        return _dot_nt(k_ref[pl.ds(start, tq), hh * LANES:(hh + 1) * LANES], qs[hh])

    def update_all(j, mask):
        sts = [scores(hh, j) for hh in range(ATT_HEADS)]
        if mask is not None:
            sts = [jnp.where(mask, st, NEG_BIG) for st in sts]
        pts, alphas = [], []
        for hh in range(ATT_HEADS):
            m_old = m_scr[hh][...]
            m_new = jnp.maximum(m_old, jnp.max(sts[hh], axis=0, keepdims=True))
            alphas.append(jnp.exp2(m_old - m_new))
            pts.append(jnp.exp2(sts[hh] - m_new).astype(BF16))
            m_scr[hh][...] = m_new
        for hh in range(ATT_HEADS):
            vt = jnp.concatenate([vt_ref[j, hh * LANES:(hh + 1) * LANES, :], ones], axis=0)
            acc_scr[hh][...] = alphas[hh] * acc_scr[hh][...] + _dot(vt, pts[hh])

    def off_diag(j, carry):
        update_all(j, None)
        return carry

    lax.fori_loop(0, i, off_diag, 0)
    key = lax.broadcasted_iota(jnp.int32, (tq, 2 * tq), 0)
    qry = lax.broadcasted_iota(jnp.int32, (tq, 2 * tq), 1) % tq
    update_all(i, key <= qry)
    for hh in range(ATT_HEADS):
        acc = acc_scr[hh][...]
        ot = acc[:DV_A] / acc[DV_A:DV_A + 1]
        at = ot[:, :tq] - lam * ot[:, tq:]
        ms = jnp.mean(at * at, axis=0, keepdims=True)
        at = (at * lax.rsqrt(ms + EPS)) * g_ref[...] * (1.0 - lam_init)
        o_ref[:, hh * LANES:(hh + 1) * LANES] = at.T.astype(BF16)


def _sample_chunks(pt_ref, q_ref, kn_ref, vn_ref, lam, g_ref, kpool_ref, vpool_ref, o_ref,
                   kbuf, vbuf, sem, m_scr, l_scr, acc_scr, step, parts, *, lam_init, n_chunks, n_seq, t_new):
    steps_per_seq = n_chunks // CHUNKS_PER_STEP
    b = step // steps_per_seq
    c0 = (step % steps_per_seq) * CHUNKS_PER_STEP
    opens = 0 in parts
    closes = CHUNKS_PER_STEP - 1 in parts

    def chunk_copies(bb, c, slot):
        out = []
        for j in range(PAGES_PER_STEP):
            page = pt_ref[bb, c * PAGES_PER_STEP + j]
            out.append(pltpu.make_async_copy(kpool_ref.at[page], kbuf.at[slot, j], sem.at[slot]))
            out.append(pltpu.make_async_copy(vpool_ref.at[page], vbuf.at[slot, j], sem.at[slot]))
        return out

    if opens:
        @pl.when(step == 0)
        def _():
            _start_all(chunk_copies(0, 0, 0) + chunk_copies(0, 1, 1))

    q = q_ref[...].astype(F32)
    qt = jnp.concatenate([q] * (2 * H_A), axis=0)
    row = lax.broadcasted_iota(jnp.int32, qt.shape, 0)
    col = lax.broadcasted_iota(jnp.int32, qt.shape, 1)
    qbd = jnp.where(col // DK_A == row // t_new, qt, 0.0).astype(BF16)

    if opens:
        @pl.when(c0 == 0)
        def _():
            m_scr[...] = jnp.full(m_scr.shape, NEG_BIG, F32)
            l_scr[...] = jnp.zeros(l_scr.shape, F32)
            acc_scr[...] = jnp.zeros(acc_scr.shape, F32)

    rows_h = 2 * t_new

    def update(s, v_of_head):
        m_old = m_scr[...]
        m_new = jnp.maximum(m_old, jnp.max(s, axis=1, keepdims=True))
        alpha = jnp.exp2(m_old - m_new)
        p = jnp.exp2(s - m_new)
        l_scr[...] = alpha * l_scr[...] + jnp.sum(p, axis=1, keepdims=True)
        pb = p.astype(BF16)
        pv = [_dot(pb[h * rows_h:(h + 1) * rows_h, :], v_of_head(h)) for h in range(H_A)]
        acc_scr[...] = alpha * acc_scr[...] + jnp.concatenate(pv, axis=0)
        m_scr[...] = m_new

    def chunk(c, slot):
        for cp in chunk_copies(b, c, slot):
            cp.wait()
        kt = jnp.concatenate([kbuf[slot, j].astype(BF16) for j in range(PAGES_PER_STEP)], axis=1)

        def cached_v(h):
            return jnp.concatenate([vbuf.at[slot, j][pl.ds(h, PAGE_SIZE, stride=H_A), :].astype(BF16)
                                    for j in range(PAGES_PER_STEP)], axis=0)

        update(_dot(qbd, kt), cached_v)

        @pl.when(c + 2 < n_chunks)
        def _():
            _start_all(chunk_copies(b, c + 2, slot))

        @pl.when(jnp.logical_and(c + 2 >= n_chunks, b + 1 < n_seq))
        def _():
            _start_all(chunk_copies(b + 1, c + 2 - n_chunks, slot))

    for j in parts:
        chunk(c0 + j, j % 2)

    def finish():
        zpad = jnp.zeros((PAGE_SIZE - t_new, W_QA), F32)
        kn = jnp.concatenate([kn_ref[...].astype(F32), zpad], axis=0).astype(BF16)
        vn = jnp.concatenate([vn_ref[...].astype(F32), zpad], axis=0).astype(BF16)
        s = _dot_nt(qbd, kn)
        row = lax.broadcasted_iota(jnp.int32, s.shape, 0) % t_new
        col = lax.broadcasted_iota(jnp.int32, s.shape, 1)
        update(jnp.where(col <= row, s, NEG_BIG), lambda h: vn[:, h * DV_A:(h + 1) * DV_A])
        o = acc_scr[...] / l_scr[...]
        outs = []
        for h in range(H_A):
            r0 = h * rows_h
            outs.append(_subln(o[r0:r0 + t_new] - lam * o[r0 + t_new:r0 + rows_h], g_ref[...], lam_init))
        o_ref[...] = jnp.concatenate(outs, axis=1).astype(BF16)

    if closes:
        pl.when(c0 + CHUNKS_PER_STEP == n_chunks)(finish)


def _attention_body(pt_ref, q_ref, k_ref, vt_ref, lam_ref, gcol_ref, qs_ref, kn_ref, vn_ref, grow_ref,
                    kpool_ref, vpool_ref, o_ref, os_ref, *scratch, lam_init, n_chunks, n_seq, t_new):
    m_scr, acc_scr = scratch[:ATT_HEADS], scratch[ATT_HEADS:2 * ATT_HEADS]
    kbuf, vbuf, sem, ms_scr, ls_scr, accs_scr = scratch[2 * ATT_HEADS:]
    step = pl.program_id(0) * pl.num_programs(1) + pl.program_id(1)
    lam = _lambda_value(lam_ref, lam_init)
    sample = functools.partial(_sample_chunks, pt_ref, qs_ref, kn_ref, vn_ref, lam, grow_ref, kpool_ref, vpool_ref,
                               os_ref, kbuf, vbuf, sem, ms_scr, ls_scr, accs_scr, step,
                               lam_init=lam_init, n_chunks=n_chunks, n_seq=n_seq, t_new=t_new)
    sample((0,))
    _prompt_tile(q_ref, k_ref, vt_ref, lam, gcol_ref, o_ref, m_scr, acc_scr, pl.program_id(1), lam_init)
    sample((1,))


def _attention(q, k, vt, q_s, k_new, v_new, cache_k, cache_v, page_table, lam_vec, g_subln, batch, seq,
               lam_init, t_new):
    nq = seq // ATT_TILE
    bd, n_pages = page_table.shape
    n_chunks = n_pages // PAGES_PER_STEP
    steps_per_seq = n_chunks // CHUNKS_PER_STEP
    assert ATT_HEADS == H_A and CHUNKS_PER_STEP % 2 == 0 and n_chunks % CHUNKS_PER_STEP == 0
    assert batch * nq == bd * steps_per_seq
    n_rows = 2 * H_A * t_new
    kv = pl.BlockSpec((seq, W_QA), lambda b, i, pt: (b, 0))
    vts = pl.BlockSpec((nq, W_VA, ATT_TILE), lambda b, i, pt: (b, 0, 0))
    qo = pl.BlockSpec((ATT_TILE, W_QA), lambda b, i, pt: (b * nq + i, 0))
    new = pl.BlockSpec((None, t_new, W_QA), lambda b, i, pt: ((b * nq + i) // steps_per_seq, 0, 0))
    const = lambda shape: pl.BlockSpec(shape, lambda b, i, pt: (0, 0))
    page_buf = pltpu.VMEM((2, PAGES_PER_STEP, W_QA, PAGE_SIZE), F32)
    grid_spec = pltpu.PrefetchScalarGridSpec(
        num_scalar_prefetch=1,
        grid=(batch, nq),
        in_specs=[qo, kv, vts, const((4, DK_A)), const((DV_A, 1)), new, new, new, const((1, DV_A)),
                  pl.BlockSpec(memory_space=pl.ANY), pl.BlockSpec(memory_space=pl.ANY)],
        out_specs=[qo, new],
        scratch_shapes=[pltpu.VMEM((1, 2 * ATT_TILE), F32)] * ATT_HEADS
                       + [pltpu.VMEM((DV_A + ROW_ALIGN, 2 * ATT_TILE), F32)] * ATT_HEADS
                       + [page_buf, page_buf, pltpu.SemaphoreType.DMA((2,)), pltpu.VMEM((n_rows, 1), F32),
                          pltpu.VMEM((n_rows, 1), F32), pltpu.VMEM((n_rows, DV_A), F32)],
    )
    o_p, o_s = pl.pallas_call(
        functools.partial(_attention_body, lam_init=lam_init, n_chunks=n_chunks, n_seq=bd, t_new=t_new),
        grid_spec=grid_spec,
        out_shape=[jax.ShapeDtypeStruct((batch * seq, W_VA), BF16), jax.ShapeDtypeStruct((bd, t_new, W_VA), BF16)],
        compiler_params=_params(("arbitrary", "arbitrary")),
        name="attention",
    )(page_table, q, k, vt, lam_vec, g_subln.reshape(DV_A, 1), q_s.reshape(bd, t_new, W_QA),
      k_new.reshape(bd, t_new, W_QA), v_new.reshape(bd, t_new, W_VA), g_subln, cache_k, cache_v)
    return o_p, o_s.reshape(bd * t_new, W_VA)


def _mlstm_body(qk_ref, vm_ref, om_ref, gt_ref, cw_ref, cb_ref, bg_ref, gm_ref, cbuf_ref, c0_ref, n0_ref, m0_ref,
                h_ref, cst_ref, cout_ref, nout_ref, mout_ref, ext_scr, c_scr, n_scr, m_scr, *, tb, L, nc):
    c_idx = pl.program_id(1)

    @pl.when(c_idx == 0)
    def _():
        ext_scr[...] = jnp.zeros(ext_scr.shape, F32)
        ext_scr[8 - (CONV_W - 1):8, :] = cbuf_ref[...]
        c_scr[...] = c0_ref[...]
        n_scr[...] = n0_ref[...]
        m_scr[...] = m0_ref[...]

    pad = L - tb
    u = qk_ref[...]
    if pad:
        u = jnp.concatenate([u, jnp.zeros((pad, u.shape[1]), F32)], axis=0)
    a = _conv_silu(u, ext_scr[...], cw_ref, cb_ref)
    if not pad:
        ext_scr[...] = u[L - 8:L]

    @pl.when(c_idx == nc - 1)
    def _():
        cst_ref[...] = qk_ref[tb - (CONV_W - 1):tb, :]

    g = gt_ref[...] + bg_ref[...]
    li = g
    lf = jnp.minimum(g, 0.0) - jnp.log1p(jnp.exp(-jnp.abs(g)))
    if pad:
        zpad = jnp.zeros((pad, LANES), F32)
        li = jnp.concatenate([li, zpad + NEG_BIG], axis=0)
        lf = jnp.concatenate([lf, zpad], axis=0)
    row = lax.broadcasted_iota(jnp.int32, (L, LANES), 0)
    lane = lax.broadcasted_iota(jnp.int32, (L, LANES), 1)
    bcum = lf
    shift = 1
    while shift < L:
        bcum = bcum + jnp.where(row >= shift, pltpu.roll(bcum, shift, 0), 0.0)
        shift *= 2
    gates = jnp.where(lane < H_M, li, bcum)
    gates_t = gates.T
    tri = lax.broadcasted_iota(jnp.int32, (L, L), 0) >= lax.broadcasted_iota(jnp.int32, (L, L), 1)
    m_all = m_scr[...]
    lane1 = lax.broadcasted_iota(jnp.int32, (1, LANES), 1)
    m_next = m_all
    vall = vm_ref[...]
    if pad:
        vall = jnp.concatenate([vall, jnp.zeros((pad, vall.shape[1]), BF16)], axis=0)

    for h in range(H_M):
        li_col = gates[:, h:h + 1]
        b_col = gates[:, H_M + h:H_M + h + 1]
        src_row = gates_t[h:h + 1, :] - gates_t[H_M + h:H_M + h + 1, :]
        m_prev = m_all[:, h:h + 1]
        b_last = b_col[L - 1:L, :]
        log_d = jnp.where(tri, b_col + src_row, NEG_BIG)
        inter = b_col + m_prev
        mt = jnp.maximum(inter, jnp.max(log_d, axis=1, keepdims=True))
        q = a[:, h * DH_M:(h + 1) * DH_M]
        k = a[:, W_M + h * DH_M:W_M + (h + 1) * DH_M] * (DH_M ** -0.5)
        v = vall[:, h * DH_M:(h + 1) * DH_M]
        qb = q.astype(BF16)
        s = _dot_nt(qb, k.astype(BF16)) * jnp.exp(log_d - mt)
        ei = jnp.exp(inter - mt)
        c_old = c_scr[h]
        n_old = n_scr[h:h + 1, :]
        num = ei * _dot(qb, c_old.astype(BF16)) + _dot(s.astype(BF16), v)
        den = ei * jnp.sum(q * n_old, axis=1, keepdims=True) + jnp.sum(s, axis=1, keepdims=True)
        hh = num / jnp.maximum(jnp.abs(den), jnp.exp(-mt))
        g_col = b_last - b_col + li_col
        bl = b_last + m_prev
        m_new = jnp.maximum(bl, jnp.max(g_col, axis=0, keepdims=True))
        wg = jnp.exp(g_col - m_new)
        decay = jnp.exp(bl - m_new)
        kw = k * wg
        c_scr[h] = decay * c_old + _dot(kw.T.astype(BF16), v)
        n_scr[h:h + 1, :] = decay * n_old + jnp.sum(kw, axis=0, keepdims=True)
        m_next = jnp.where(lane1 == h, m_new, m_next)
        ms = jnp.mean(hh * hh, axis=1, keepdims=True)
        hn = (hh * lax.rsqrt(ms + EPS)) * gm_ref[h:h + 1, :]
        og = _sigmoid(om_ref[:, h * DH_M:(h + 1) * DH_M])
        h_ref[:, h * DH_M:(h + 1) * DH_M] = (hn[:tb] * og).astype(BF16)

    m_scr[...] = m_next

    @pl.when(c_idx == nc - 1)
    def _():
        cout_ref[...] = c_scr[...]
        nout_ref[...] = n_scr[...]
        mout_ref[...] = m_scr[...]


def _mlstm(qk, vm, om, gt, w_conv, b_conv, b_gates_pad, g_mnorm, conv_buf, c0, n0, m0_pad, batch, seq):
    tb = min(seq, MLSTM_CHUNK)
    L = max(tb, LANES)
    nc = seq // tb
    tok = lambda w: pl.BlockSpec((None, tb, w), lambda b, c: (b * nc + c, 0, 0))
    chunks = lambda a: a.reshape(batch * nc, tb, a.shape[-1])
    const = lambda shape: pl.BlockSpec(shape, lambda b, c: (0,) * len(shape))
    per_b = lambda shape: pl.BlockSpec((None,) + shape, lambda b, c: (b,) + (0,) * len(shape))
    h, cst, c_out, n_out, m_out = pl.pallas_call(
        functools.partial(_mlstm_body, tb=tb, L=L, nc=nc),
        grid=(batch, nc),
        in_specs=[tok(2 * W_M), tok(W_M), tok(W_M), tok(LANES), const((CONV_W, 2 * W_M)), const((1, 2 * W_M)),
                  const((1, LANES)), const((H_M, DH_M)), per_b((CONV_W - 1, 2 * W_M)),
                  per_b((H_M, DH_M, DH_M)), per_b((H_M, DH_M)), per_b((1, LANES))],
        out_specs=[tok(W_M), per_b((CONV_W - 1, 2 * W_M)), per_b((H_M, DH_M, DH_M)), per_b((H_M, DH_M)),
                   per_b((1, LANES))],
        out_shape=[jax.ShapeDtypeStruct((batch * nc, tb, W_M), BF16),
                   jax.ShapeDtypeStruct((batch, CONV_W - 1, 2 * W_M), F32),
                   jax.ShapeDtypeStruct((batch, H_M, DH_M, DH_M), F32),
                   jax.ShapeDtypeStruct((batch, H_M, DH_M), F32),
                   jax.ShapeDtypeStruct((batch, 1, LANES), F32)],
        scratch_shapes=[pltpu.VMEM((8, 2 * W_M), F32), pltpu.VMEM((H_M, DH_M, DH_M), F32),
                        pltpu.VMEM((H_M, DH_M), F32), pltpu.VMEM((1, LANES), F32)],
        compiler_params=_params(("arbitrary", "arbitrary")),
        name="mlstm",
    )(chunks(qk), chunks(vm), chunks(om), chunks(gt), w_conv, b_conv, b_gates_pad, g_mnorm, conv_buf, c0, n0,
      m0_pad)
    return h.reshape(batch * seq, W_M), cst, c_out, n_out, m_out


def _merge_body(oa_ref, hm_ref, ga_ref, gb_ref, x_ref, gt1_ref, sc2_ref, sh2_ref, g2_ref, wa_ref, wb_ref, wo_ref,
                wr_ref, br_ref, *rest):
    x2_ref, h2_ref, route_ref, cnt_ref = rest[-4:]
    ya = _dot(oa_ref[...], wa_ref[...])
    yb = _dot(hm_ref[...], wb_ref[...])
    mix = _sigmoid(ga_ref[...]) * ya + _sigmoid(gb_ref[...]) * yb
    y = _dot(mix.astype(BF16), wo_ref[...])
    x2 = x_ref[...] + gt1_ref[...] * y
    x2_ref[...] = x2
    ms = jnp.mean(x2 * x2, axis=1, keepdims=True)
    h2 = (x2 * lax.rsqrt(ms + EPS)) * g2_ref[...] * (1.0 + sc2_ref[...]) + sh2_ref[...]
    h2b = h2.astype(BF16)
    h2_ref[...] = h2b

    tm = h2b.shape[0]
    logits_t = (_dot(h2b, wr_ref[...]) + br_ref[...]).T[:N_EXP, :]
    row = lax.broadcasted_iota(jnp.int32, (N_EXP, tm), 0)
    row_f = row.astype(F32)
    work = logits_t
    vals, hots = [], []
    for _ in range(TOP_K):
        mx = jnp.max(work, axis=0, keepdims=True)
        idx = jnp.min(jnp.where(work == mx, row, N_EXP), axis=0, keepdims=True)
        hot = row == idx
        vals.append(mx)
        hots.append(hot)
        work = jnp.where(hot, 2.0 * NEG_BIG, work)
    es = [jnp.exp(v - vals[0]) for v in vals]
    den = es[0]
    for e in es[1:]:
        den = den + e
    sel_t = jnp.zeros((N_EXP, tm), F32)
    for hot in hots:
        sel_t = jnp.where(hot, 1.0, sel_t)
    r_i = lax.broadcasted_iota(jnp.int32, (tm, tm), 0)
    c_i = lax.broadcasted_iota(jnp.int32, (tm, tm), 1)
    rank_t = _dot(sel_t.astype(BF16), jnp.where(r_i < c_i, 1.0, 0.0).astype(BF16))
    sub = lax.broadcasted_iota(jnp.int32, (ROUTE_SEL, tm), 0)
    head = jnp.zeros((ROUTE_SEL, tm), F32)
    for k in range(TOP_K):
        e_k = jnp.sum(jnp.where(hots[k], row_f, 0.0), axis=0, keepdims=True)
        r_k = jnp.sum(jnp.where(hots[k], rank_t, 0.0), axis=0, keepdims=True)
        head = jnp.where(sub == k, e_k, head)
        head = jnp.where(sub == TOP_K + k, es[k] / den, head)
        head = jnp.where(sub == 2 * TOP_K + k, r_k, head)
    pad = jnp.zeros((LANES - ROUTE_SEL - N_EXP, tm), F32)
    route = jnp.concatenate([head, sel_t, pad], axis=0).T
    route_ref[...] = route
    cnt_ref[...] = jnp.sum(route, axis=0, keepdims=True)


def _merge(oa, hm, ga, gb, x, gt1, sc2, sh2, g2, wa, wb, wo, wr, br, rows_per_mod, tile0, n_all, prev=None):
    n = x.shape[0]
    nt = n // TOK_TILE
    nt_all = n_all // TOK_TILE
    tiles_per_group = nt // gt1.shape[0]
    tok = lambda w: pl.BlockSpec((TOK_TILE, w), lambda i: (i, 0))
    mod = pl.BlockSpec((None, rows_per_mod, D_MODEL), lambda i: (i // tiles_per_group, 0, 0))
    res = lambda shape: pl.BlockSpec(shape, lambda i: (0, 0), pipeline_mode=pl.Buffered(1))
    out_tok = lambda w: pl.BlockSpec((TOK_TILE, w), lambda i: (tile0 + i, 0))
    in_specs = [tok(W_VA), tok(W_M), tok(D_MODEL), tok(D_MODEL), tok(D_MODEL), mod, mod, mod,
                pl.BlockSpec((1, D_MODEL), lambda i: (0, 0)),
                res((W_VA, D_MODEL)), res((W_M, D_MODEL)), res((D_MODEL, D_MODEL)), res((D_MODEL, LANES)),
                pl.BlockSpec((1, LANES), lambda i: (0, 0))]
    args = [oa, hm, ga, gb, x, gt1, sc2, sh2, g2, wa, wb, wo, wr, br]
    aliases = {}
    if prev is not None:
        in_specs += [pl.BlockSpec(memory_space=pl.ANY)] * 4
        aliases = {len(args) + j: j for j in range(4)}
        args += list(prev)
    return pl.pallas_call(
        _merge_body,
        grid=(nt,),
        in_specs=in_specs,
        out_specs=[out_tok(D_MODEL), out_tok(D_MODEL), out_tok(LANES),
                   pl.BlockSpec((None, 1, LANES), lambda i: (tile0 + i, 0, 0))],
        out_shape=[jax.ShapeDtypeStruct((n_all, D_MODEL), F32), jax.ShapeDtypeStruct((n_all, D_MODEL), BF16),
                   jax.ShapeDtypeStruct((n_all, LANES), F32), jax.ShapeDtypeStruct((nt_all, 1, LANES), F32)],
        input_output_aliases=aliases,
        compiler_params=_params(("arbitrary",)),
        name="merge",
    )(*args)


def _segment_copies(src, dst, sem, src_row, dst_row, n_groups, max_groups):
    out = []
    bit = 1
    while bit * 2 <= max_groups:
        bit *= 2
    while bit >= 1:
        off = (n_groups // (2 * bit)) * (2 * bit) * ROW_ALIGN
        rows = bit * ROW_ALIGN
        cp = pltpu.make_async_copy(src.at[pl.ds(pl.multiple_of(src_row + off, ROW_ALIGN), rows)],
                                   dst.at[pl.ds(pl.multiple_of(dst_row + off, ROW_ALIGN), rows)], sem)
        out.append(((n_groups // bit) % 2 == 1, cp))
        bit //= 2
    return out


def _run_copies(copies):
    for pred, cp in copies:
        pl.when(pred)(cp.start)
    for pred, cp in copies:
        pl.when(pred)(cp.wait)


def _slot_rows(route_t, loff_col, k):
    e_row = route_t[k:k + 1, :]
    r_row = route_t[2 * TOP_K + k:2 * TOP_K + k + 1, :]
    sub = lax.broadcasted_iota(jnp.int32, (LANES, route_t.shape[1]), 0).astype(F32)
    return jnp.sum(jnp.where(sub == e_row, loff_col, 0.0), axis=0, keepdims=True) + r_row


def _for_groups(n_groups, table_ref, base, make_copy, start):
    def body(i, carry):
        for j in range(GROUP_UNROLL):
            g = i * GROUP_UNROLL + j
            cp = make_copy(g, table_ref[base + g])
            if start:
                cp.start()
            else:
                cp.wait()
        return carry
    lax.fori_loop(0, (n_groups + GROUP_UNROLL - 1) // GROUP_UNROLL, body, 0)


def _dispatch_body(ng_ref, dt_ref, lo_ref, sv_ref, rm_ref, tail_ref, h2_ref, route_ref, soff_ref, xs_ref,
                   loc_scr, carry_scr, zero_scr, sem, *, nt):
    t = pl.program_id(0)
    buf = t % 2

    def copies_of(tt):
        def make(g, row):
            return pltpu.make_async_copy(
                loc_scr.at[tt % 2, pl.ds(pl.multiple_of(g * ROW_ALIGN, ROW_ALIGN), ROW_ALIGN)],
                xs_ref.at[pl.ds(pl.multiple_of(row, ROW_ALIGN), ROW_ALIGN)], sem.at[tt % 2])
        return ng_ref[tt], dt_ref, tt * GROUPS, make

    @pl.when(t == 0)
    def _():
        carry_scr[...] = jnp.zeros(carry_scr.shape, BF16)
        loc_scr[...] = jnp.zeros(loc_scr.shape, BF16)

    @pl.when(t >= 2)
    def _():
        _for_groups(*copies_of(t - 2), start=False)

    route_t = route_ref[...].T
    soff_col = soff_ref[...]
    slots = [_slot_rows(route_t, soff_col, k) for k in range(TOP_K)]

    def sort_rows(m):
        r_i = lax.broadcasted_iota(jnp.int32, (m, TOK_TILE), 0).astype(F32)
        onehot = jnp.zeros((m, TOK_TILE), F32)
        for k in range(TOP_K):
            onehot = jnp.where(r_i == slots[k], 1.0, onehot)
        loc_scr[buf, 0:m, :] = _dot(onehot.astype(BF16), h2_ref[...]).astype(BF16)

    fits = ng_ref[t] * ROW_ALIGN <= M_LOC_SHORT
    pl.when(fits)(lambda: sort_rows(M_LOC_SHORT))
    pl.when(jnp.logical_not(fits))(lambda: sort_rows(M_LOC))

    for e in range(N_EXP):
        lo = pl.multiple_of(lo_ref[t * N_EXP + e], ROW_ALIGN)
        sv = pl.multiple_of(sv_ref[t * N_EXP + e], ROW_ALIGN)
        loc_scr[buf, pl.ds(lo, ROW_ALIGN), :] = loc_scr[buf, pl.ds(lo, ROW_ALIGN), :] + carry_scr[e]
        pending = loc_scr[buf, pl.ds(sv, ROW_ALIGN), :]
        carry_scr[e] = jnp.where(rm_ref[t * N_EXP + e] > 0, pending, jnp.zeros_like(pending))

    _for_groups(*copies_of(t), start=True)

    @pl.when(t == nt - 1)
    def _():
        if nt >= 2:
            _for_groups(*copies_of(t - 1), start=False)
        _for_groups(*copies_of(t), start=False)
        zero_scr[...] = jnp.zeros(zero_scr.shape, BF16)
        tails = []
        for e in range(N_EXP):
            tails += _segment_copies(zero_scr, xs_ref, sem.at[0], 0, tail_ref[e], tail_ref[N_EXP + e],
                                     FFN_TILE // ROW_ALIGN - 1)
        _run_copies(tails)


def _dispatch(meta, h2, route, soff_col):
    nt = h2.shape[0] // TOK_TILE
    grid_spec = pltpu.PrefetchScalarGridSpec(
        num_scalar_prefetch=6,
        grid=(nt,),
        in_specs=[pl.BlockSpec((TOK_TILE, D_MODEL), lambda t, *_: (t, 0)),
                  pl.BlockSpec((TOK_TILE, LANES), lambda t, *_: (t, 0)),
                  pl.BlockSpec((None, LANES, 1), lambda t, *_: (t, 0, 0))],
        out_specs=pl.BlockSpec(memory_space=pl.ANY),
        scratch_shapes=[pltpu.VMEM((2, M_LOC + ROW_ALIGN, D_MODEL), BF16),
                        pltpu.VMEM((N_EXP, ROW_ALIGN, D_MODEL), BF16),
                        pltpu.VMEM((FFN_TILE, D_MODEL), BF16), pltpu.SemaphoreType.DMA((2,))],
    )
    return pl.pallas_call(
        functools.partial(_dispatch_body, nt=nt),
        grid_spec=grid_spec,
        out_shape=jax.ShapeDtypeStruct((meta["rows"] + 2 * M_LOC, D_MODEL), BF16),
        compiler_params=_params(("arbitrary",)),
        name="moe_dispatch",
    )(meta["n_groups"], meta["dispatch_rows"], meta["slot_start"], meta["carry_start"], meta["carry_rows"],
      meta["tail"], h2, route, soff_col)


def _ffn_body(be_ref, nu_ref, nx_ref, sl_ref, hf_ref, x_ref, bgu_ref, bd_ref, wgu_hbm, wd_hbm, y_ref,
              wgu_f32, wd_f32, wgu_scr, wd_scr, sem):
    i = pl.program_id(0)
    expert = be_ref[i]
    slot = sl_ref[i]
    prev = be_ref[jnp.maximum(i - 1, 0)]

    def weight_copies(e, s):
        return (pltpu.make_async_copy(wgu_hbm.at[e], wgu_f32.at[s], sem.at[s]),
                pltpu.make_async_copy(wd_hbm.at[e], wd_f32.at[s], sem.at[s]))

    @pl.when(i == 0)
    def _():
        for cp in weight_copies(expert, slot):
            cp.start()

    @pl.when(jnp.logical_and(i < nu_ref[0], jnp.logical_or(i == 0, expert != prev)))
    def _():
        for cp in weight_copies(expert, slot):
            cp.wait()
        wgu_scr[...] = wgu_f32[slot].astype(BF16)
        wd_scr[...] = wd_f32[slot].astype(BF16)

        @pl.when(nx_ref[i] >= 0)
        def _():
            for cp in weight_copies(nx_ref[i], 1 - slot):
                cp.start()

    def expert_rows(m):
        gu = _dot(x_ref[0:m, :], wgu_scr[...]) + bgu_ref[...]
        gate = jnp.minimum(gu[:, :D_FF], SWIGLU_LIMIT)
        up = jnp.clip(gu[:, D_FF:], -SWIGLU_LIMIT, SWIGLU_LIMIT)
        act = (up + 1.0) * gate * _sigmoid(SWIGLU_ALPHA * gate)
        y_ref[0:m, :] = (_dot(act.astype(BF16), wd_scr[...]) + bd_ref[...]).astype(BF16)

    active = i < nu_ref[0]
    pl.when(jnp.logical_and(active, hf_ref[i] == 0))(lambda: expert_rows(FFN_TILE))
    pl.when(jnp.logical_and(active, hf_ref[i] != 0))(lambda: expert_rows(FFN_TILE // 2))


def _ffn(meta, xs, w_gu, b_gu, w_down, b_down):
    rows = meta["rows"]
    nblk = rows // FFN_TILE
    row_blk = pl.BlockSpec((FFN_TILE, D_MODEL), lambda i, be, nu, *_: (jnp.minimum(i, nu[0] - 1), 0))
    grid_spec = pltpu.PrefetchScalarGridSpec(
        num_scalar_prefetch=5,
        grid=(nblk,),
        in_specs=[row_blk,
                  pl.BlockSpec((None, 1, 2 * D_FF), lambda i, be, *_: (be[i], 0, 0)),
                  pl.BlockSpec((None, 1, D_MODEL), lambda i, be, *_: (be[i], 0, 0)),
                  pl.BlockSpec(memory_space=pl.ANY), pl.BlockSpec(memory_space=pl.ANY)],
        out_specs=row_blk,
        scratch_shapes=[pltpu.VMEM((2, D_MODEL, 2 * D_FF), F32), pltpu.VMEM((2, D_FF, D_MODEL), F32),
                        pltpu.VMEM((D_MODEL, 2 * D_FF), BF16), pltpu.VMEM((D_FF, D_MODEL), BF16),
                        pltpu.SemaphoreType.DMA((2,))],
    )
    return pl.pallas_call(
        _ffn_body,
        grid_spec=grid_spec,
        out_shape=jax.ShapeDtypeStruct((rows, D_MODEL), BF16),
        compiler_params=_params(("arbitrary",)),
        name="moe_ffn",
    )(meta["blk_exp"], meta["n_used"], meta["next_exp"], meta["blk_slot"], meta["blk_half"], xs,
      b_gu.reshape(N_EXP, 1, 2 * D_FF), b_down.reshape(N_EXP, 1, D_MODEL), w_gu, w_down)


def _combine_body(ng_ref, ct_ref, route_ref, soff_ref, x2_ref, gtp_ref, gts_ref, gf_ref, ys_ref,
                  yp_ref, ysm_ref, loc_scr, moe_scr, sem, *, nt, nt_prompt, final):
    t = pl.program_id(0)
    buf = t % 2

    def copies_of(tt):
        def make(g, row):
            return pltpu.make_async_copy(
                ys_ref.at[pl.ds(pl.multiple_of(row, ROW_ALIGN), ROW_ALIGN)],
                loc_scr.at[tt % 2, pl.ds(pl.multiple_of(g * ROW_ALIGN, ROW_ALIGN), ROW_ALIGN)], sem.at[tt % 2])
        return ng_ref[tt], ct_ref, tt * GROUPS, make

    @pl.when(t == 0)
    def _():
        loc_scr[...] = jnp.zeros(loc_scr.shape, BF16)
        _for_groups(*copies_of(0), start=True)

    @pl.when(t + 1 < nt)
    def _():
        _for_groups(*copies_of(t + 1), start=True)

    _for_groups(*copies_of(t), start=False)

    route = route_ref[...]
    soff_row = soff_ref[...]
    lane = lax.broadcasted_iota(jnp.int32, (TOK_TILE, LANES), 1).astype(F32)
    slots = [jnp.sum(jnp.where(lane == route[:, k:k + 1], soff_row, 0.0), axis=1, keepdims=True)
             + route[:, 2 * TOP_K + k:2 * TOP_K + k + 1] for k in range(TOP_K)]

    def weighted_sum(m):
        c_i = lax.broadcasted_iota(jnp.int32, (TOK_TILE, m), 1).astype(F32)
        wmat = jnp.zeros((TOK_TILE, m), F32)
        for k in range(TOP_K):
            wmat = jnp.where(c_i == slots[k], route[:, TOP_K + k:TOP_K + k + 1], wmat)
        moe_scr[...] = _dot(wmat.astype(BF16), loc_scr[buf, 0:m, :])

    fits = ng_ref[t] * ROW_ALIGN <= M_LOC_SHORT
    pl.when(fits)(lambda: weighted_sum(M_LOC_SHORT))
    pl.when(jnp.logical_not(fits))(lambda: weighted_sum(M_LOC))

    gate = jnp.where(t >= nt_prompt, gts_ref[...], gtp_ref[...])
    xo = x2_ref[...] + gate * moe_scr[...]
    if final:
        ms = jnp.mean(xo * xo, axis=1, keepdims=True)
        xo = (xo * lax.rsqrt(ms + EPS)) * gf_ref[...]

    @pl.when(t < nt_prompt)
    def _():
        yp_ref[...] = xo

    @pl.when(t >= nt_prompt)
    def _():
        ysm_ref[...] = xo


def _combine(meta, route, soff_row, x2, gt2_p, gt2_s, g_final, ys, n_prompt, n_sample, final):
    nt = x2.shape[0] // TOK_TILE
    nt_prompt = n_prompt // TOK_TILE
    tiles_per_batch = nt_prompt // gt2_p.shape[0]
    grid_spec = pltpu.PrefetchScalarGridSpec(
        num_scalar_prefetch=2,
        grid=(nt,),
        in_specs=[pl.BlockSpec((TOK_TILE, LANES), lambda t, *_: (t, 0)),
                  pl.BlockSpec((None, 1, LANES), lambda t, *_: (t, 0, 0)),
                  pl.BlockSpec((TOK_TILE, D_MODEL), lambda t, *_: (t, 0)),
                  pl.BlockSpec((None, 1, D_MODEL),
                               lambda t, *_: (jnp.minimum(t, nt_prompt - 1) // tiles_per_batch, 0, 0)),
                  pl.BlockSpec((TOK_TILE, D_MODEL), lambda t, *_: (0, 0)),
                  pl.BlockSpec((1, D_MODEL), lambda t, *_: (0, 0)),
                  pl.BlockSpec(memory_space=pl.ANY)],
        out_specs=[pl.BlockSpec((TOK_TILE, D_MODEL), lambda t, *_: (jnp.minimum(t, nt_prompt - 1), 0)),
                   pl.BlockSpec((TOK_TILE, D_MODEL), lambda t, *_: (0, 0))],
        scratch_shapes=[pltpu.VMEM((2, M_LOC, D_MODEL), BF16), pltpu.VMEM((TOK_TILE, D_MODEL), F32),
                        pltpu.SemaphoreType.DMA((2,))],
    )
    return pl.pallas_call(
        functools.partial(_combine_body, nt=nt, nt_prompt=nt_prompt, final=final),
        grid_spec=grid_spec,
        out_shape=[jax.ShapeDtypeStruct((n_prompt, D_MODEL), F32), jax.ShapeDtypeStruct((n_sample, D_MODEL), F32)],
        compiler_params=_params(("arbitrary",)),
        name="moe_combine",
    )(meta["n_groups"], meta["combine_rows"], route, soff_row, x2, gt2_p, gt2_s, g_final, ys)


def _moe_offsets(cnt):
    nt = cnt.shape[0]
    ra = ROW_ALIGN
    prefix = jnp.cumsum(cnt, axis=0) - cnt
    total = jnp.sum(cnt, axis=0)
    pending = prefix % ra
    used = pending + cnt
    seg = (used + ra - 1) // ra * ra
    lo = jnp.cumsum(seg, axis=1) - seg
    n_groups = jnp.sum(seg, axis=1) // ra
    gpad = (total + FFN_TILE - 1) // FFN_TILE * FFN_TILE
    gstart = jnp.cumsum(gpad) - gpad
    base = gstart[None, :] + prefix // ra * ra
    last = (jnp.arange(nt) == nt - 1)[:, None]
    n_write = jnp.where(last, seg // ra, used // ra)
    carry_start = lo + used // ra * ra
    carry_rows = jnp.where(last, 0, used % ra)
    g = jnp.arange(GROUPS)
    slot_end = (lo + seg) // ra
    owner = jnp.minimum(jnp.sum(g[None, :, None] >= slot_end[:, None, :], axis=2), N_EXP - 1)
    pick = lambda a: jnp.sum(jnp.where(owner[:, :, None] == jnp.arange(N_EXP), a[:, None, :], 0), axis=2)
    k = g[None, :] - pick(lo) // ra
    row = pick(base) + k * ra
    valid = g[None, :] < n_groups[:, None]
    rows = (nt * TOK_TILE * TOP_K + N_EXP * (FFN_TILE - 1) + FFN_TILE - 1) // FFN_TILE * FFN_TILE
    spare = rows + (jnp.arange(nt) % 2)[:, None] * M_LOC + g[None, :] * ra
    combine_rows = jnp.where(valid, row, 0)
    dispatch_rows = jnp.where(valid & (k < pick(n_write)), row, spare)

    nblk_e = gpad // FFN_TILE
    blk_end = jnp.cumsum(nblk_e)
    n_used = jnp.maximum(blk_end[-1], 1)
    blk = jnp.minimum(jnp.arange(rows // FFN_TILE, dtype=jnp.int32), n_used - 1)
    blk_exp = jnp.minimum(jnp.sum(blk[:, None] >= blk_end[None, :], axis=1), N_EXP - 1)
    experts = jnp.arange(N_EXP)
    following = lax.cummin(jnp.where(nblk_e > 0, experts, N_EXP), reverse=True)
    next_of = jnp.concatenate([following[1:], jnp.full((1,), N_EXP, following.dtype)])
    next_of = jnp.where(next_of >= N_EXP, -1, next_of)
    parity = (jnp.cumsum(nblk_e > 0) - 1) % 2
    pick_e = lambda a: jnp.sum(jnp.where(blk_exp[:, None] == experts[None, :], a[None, :], 0), axis=1)
    blk_rows = pick_e(total) - (blk - pick_e(blk_end - nblk_e)) * FFN_TILE
    blk_half = blk_rows <= FFN_TILE // 2
    total16 = (total + ra - 1) // ra * ra
    tail = jnp.concatenate([gstart + total16, (gpad - total16) // ra])
    i32 = lambda a: a.astype(jnp.int32).reshape(-1)
    return dict(n_groups=i32(n_groups), dispatch_rows=i32(dispatch_rows), combine_rows=i32(combine_rows),
                slot_start=i32(lo), carry_start=i32(carry_start), carry_rows=i32(carry_rows), tail=i32(tail),
                blk_exp=i32(blk_exp), n_used=i32(n_used), next_exp=i32(pick_e(next_of)), blk_slot=i32(pick_e(parity)),
                blk_half=i32(blk_half),
                slot_off=(lo + pending).astype(F32), rows=rows)


def _split_w_in(w_in):
    g0 = C_GATES
    return (w_in[:, :g0].astype(BF16), w_in[:, g0 + N_GATE:].astype(BF16),
            _pad_lanes(w_in[:, g0:g0 + N_GATE]).astype(BF16))


def _pad_lanes(a, value=0.0):
    return jnp.pad(a, [(0, 0)] * (a.ndim - 1) + [(0, LANES - a.shape[-1])], constant_values=value)


def kernel(x_prompt, x_sample, c_prompt, c_sample, cache_k, cache_v, state_conv, state_C, state_n, state_m, page_table, w_ada, b_ada, g_norm1, g_norm2, w_in, b_gates, lambda_q1, lambda_k1, lambda_q2, lambda_k2, g_subln, w_conv, b_conv, g_mnorm, w_up_a, w_up_b, w_out, w_router, b_router, w_gu, b_gu, w_down, b_down, g_final):
    B, S, D = x_prompt.shape
    Bd, Td, _ = x_sample.shape
    depth = w_in.shape[0]
    n_pool = cache_k.shape[1]
    past_len = page_table.shape[1] * PAGE_SIZE
    n_p, n_s = B * S, Bd * Td
    n_all = n_p + n_s
    assert D == D_MODEL and n_s == TOK_TILE and S % MLSTM_CHUNK == 0 and n_p % TOK_TILE == 0
    assert page_table.shape[1] % PAGES_PER_STEP == 0

    cos_p, sin_p = _rope_tables(np.arange(S))
    cos_s, sin_s = _rope_tables(np.tile(past_len + np.arange(Td), Bd))
    hp = x_prompt.reshape(n_p, D)
    hs = x_sample.reshape(n_s, D)
    c_all = jnp.concatenate([c_prompt, c_sample], axis=0)
    outs = [[] for _ in range(12)]

    for l in range(depth):
        lam_init = 0.8 - 0.6 * math.exp(-0.3 * l)
        mod = _ada(c_all, w_ada[l], b_ada[l])
        mods = [mod[:, j * D:(j + 1) * D] for j in range(6)]
        mp = [m[:B].reshape(B, 1, D) for m in mods]
        ms_ = [jnp.repeat(m[B:], Td, axis=0).reshape(1, n_s, D) for m in mods]
        w_parts = _split_w_in(w_in[l])
        g1 = g_norm1[l].reshape(1, D)
        lam_vec = jnp.stack([lambda_q1[l], lambda_k1[l], lambda_q2[l], lambda_k2[l]])
        gsub = g_subln[l].reshape(1, DV_A)
        bg = _pad_lanes(b_gates[l].reshape(1, N_GATE))
        cw, cb = w_conv[l], b_conv[l].reshape(1, 2 * W_M)

        (q_p, _, kb_p, v_p, _, qk_p, vm_p, om_p, ga_p, gb_p, gt_p, vt_p, kt_p) = _inproj(
            hp, mp[1], mp[0], g1, cos_p, sin_p, w_parts, 1, PROJ_TILE)
        (q_s, k_s, kb_s, v_s, vb_s, qk_s, vm_s, om_s, ga_s, gb_s, gt_s, _, _) = _inproj(
            hs, ms_[1], ms_[0], g1, cos_s, sin_s, w_parts, n_s, n_s)
        k_p = jnp.transpose(kt_p.reshape(B, H_A, 2, DK_A, S), (0, 4, 1, 2, 3))

        kt_pool = jnp.transpose(cache_k[l], (0, 2, 3, 4, 1)).reshape(n_pool, W_QA, PAGE_SIZE)
        v_pool = cache_v[l].reshape(n_pool, PAGE_SIZE * H_A, DV_A)
        oa_p, oa_s = _attention(q_p, kb_p, vt_p, q_s, kb_s, vb_s, kt_pool, v_pool, page_table, lam_vec, gsub,
                                B, S, lam_init, Td)

        zeros = lambda *shape: jnp.zeros(shape, F32)
        hm_p, cst_p, C_p, nn_p, m_p = _mlstm(qk_p, vm_p, om_p, gt_p, cw, cb, bg, g_mnorm[l],
                                             zeros(B, CONV_W - 1, 2 * W_M), zeros(B, H_M, DH_M, DH_M),
                                             zeros(B, H_M, DH_M), zeros(B, 1, LANES), B, S)
        hm_s, cst_s, C_s, nn_s, m_s = _mlstm(qk_s, vm_s, om_s, gt_s, cw, cb, bg, g_mnorm[l],
                                             state_conv[l], state_C[l], state_n[l],
                                             _pad_lanes(state_m[l]).reshape(Bd, 1, LANES), Bd, Td)

        wa, wb, wo = w_up_a[l].astype(BF16), w_up_b[l].astype(BF16), w_out[l].astype(BF16)
        wr = _pad_lanes(w_router[l]).astype(BF16)
        br = _pad_lanes(b_router[l].reshape(1, N_EXP))
        g2 = g_norm2[l].reshape(1, D)
        part = _merge(oa_p, hm_p, ga_p, gb_p, hp, mp[2], mp[4], mp[3], g2, wa, wb, wo, wr, br, 1, 0, n_all)
        x2, h2, route, cnt = _merge(oa_s, hm_s, ga_s, gb_s, hs, ms_[2], ms_[4], ms_[3], g2, wa, wb, wo, wr, br,
                                    n_s, n_p // TOK_TILE, n_all, prev=part)

        meta = _moe_offsets(jnp.round(cnt[:, 0, ROUTE_SEL:ROUTE_SEL + N_EXP]).astype(jnp.int32))
        soff = _pad_lanes(meta["slot_off"])
        xs = _dispatch(meta, h2, route, soff[:, :, None])
        ys = _ffn(meta, xs, w_gu[l], b_gu[l], w_down[l], b_down[l])
        final = l == depth - 1
        hp, hs = _combine(meta, route, soff[:, None, :], x2, mp[5], ms_[5][0], g_final.reshape(1, D), ys,
                          n_p, n_s, final)

        for j, a in enumerate([k_p, v_p.reshape(B, S, H_A, DV_A), cst_p, C_p, nn_p,
                               m_p[:, 0, :H_M],
                               k_s.reshape(Bd, Td, H_A, 2, DK_A), v_s.reshape(Bd, Td, H_A, DV_A), cst_s, C_s, nn_s,
                               m_s[:, 0, :H_M]]):
            outs[j].append(a)

    return (hp.reshape(B, S, D), hs.reshape(Bd, Td, D)) + tuple(jnp.stack(o) for o in outs)
```

```python
import functools
import math

import numpy as np
import jax
import jax.numpy as jnp
from jax import lax
from jax.experimental import pallas as pl
from jax.experimental.pallas import tpu as pltpu

F32 = jnp.float32
BF16 = jnp.bfloat16

D_MODEL = 1024
H_A = 4
DK_A = 64
DV_A = 2 * DK_A
ROPE_THETA = 10000.0
H_M = 4
DH_M = 128
CONV_W = 4
N_EXP = 32
TOP_K = 4
D_FF = D_MODEL
SWIGLU_LIMIT = 7.0
SWIGLU_ALPHA = 1.702
EPS = 1e-6
PAGE_SIZE = 128

W_QA = H_A * 2 * DK_A
W_VA = H_A * DV_A
W_M = H_M * DH_M
N_GATE = 2 * H_M

LANES = 128
ROW_ALIGN = 16
TOK_TILE = 256
PROJ_TILE = 512
FFN_TILE = 512
ATT_TILE = 256
ATT_HEADS = H_A
MLSTM_CHUNK = 256
PAGES_PER_STEP = 16
CHUNKS_PER_STEP = 2
NEG_BIG = -1e30
LOG2_E = math.log2(math.e)
VMEM_LIMIT = 56 * 1024 * 1024

C_QA, C_KA, C_VA, C_QK, C_VM, C_OM, C_GATES = 0, 512, 1024, 1536, 2560, 3072, 3584
M_LOC = ((TOK_TILE * TOP_K + 2 * N_EXP * (ROW_ALIGN - 1)) + 255) // 256 * 256
GROUPS = M_LOC // ROW_ALIGN
M_LOC_SHORT = TOK_TILE * TOP_K + N_EXP * ROW_ALIGN
ROUTE_SEL = 32
GROUP_UNROLL = 4


def _dot(a, b):
    return jnp.dot(a, b, preferred_element_type=F32)


def _dot_nt(a, b):
    return lax.dot_general(a, b, (((1,), (1,)), ((), ())), preferred_element_type=F32)


def _sigmoid(x):
    return 0.5 * jnp.tanh(0.5 * x) + 0.5


def _start_all(copies):
    for cp in copies:
        cp.start()


def _params(sem):
    return pltpu.CompilerParams(dimension_semantics=sem, vmem_limit_bytes=VMEM_LIMIT)


def _ada_body(c_ref, w_ref, b_ref, o_ref):
    c = c_ref[...]
    s = c * _sigmoid(c)
    s_hi = s.astype(BF16)
    s_lo = (s - s_hi.astype(F32)).astype(BF16)
    w = w_ref[...]
    w_hi = w.astype(BF16)
    w_lo = (w - w_hi.astype(F32)).astype(BF16)
    o_ref[...] = _dot(s_hi, w_hi) + _dot(s_lo, w_hi) + _dot(s_hi, w_lo) + b_ref[...]


def _ada(c_all, w_ada, b_ada):
    rows = c_all.shape[0]
    n_out = w_ada.shape[1]
    blk = D_MODEL
    return pl.pallas_call(
        _ada_body,
        grid=(n_out // blk,),
        in_specs=[pl.BlockSpec((rows, D_MODEL), lambda j: (0, 0)),
                  pl.BlockSpec((D_MODEL, blk), lambda j: (0, j)),
                  pl.BlockSpec((1, blk), lambda j: (0, j))],
        out_specs=pl.BlockSpec((rows, blk), lambda j: (0, j)),
        out_shape=jax.ShapeDtypeStruct((rows, n_out), F32),
        compiler_params=_params(("arbitrary",)),
        name="ada",
    )(c_all, w_ada, b_ada.reshape(1, n_out))


def _rope(z, cos, sin):
    lane = lax.broadcasted_iota(jnp.int32, (z.shape[0], LANES), 1)
    first_half = (lane % DK_A) < (DK_A // 2)
    out = []
    for h in range(H_A):
        xh = z[:, h * LANES:(h + 1) * LANES]
        partner = jnp.where(first_half, pltpu.roll(xh, LANES - DK_A // 2, 1), pltpu.roll(xh, DK_A // 2, 1))
        out.append(xh * cos + partner * sin)
    return jnp.concatenate(out, axis=1)


def _conv_silu(u, ext, cw_ref, cb_ref):
    rows = u.shape[0]
    full = jnp.concatenate([ext, u], axis=0)
    conv = cb_ref[...] + cw_ref[CONV_W - 1:CONV_W, :] * u
    for j in range(CONV_W - 1):
        conv = conv + cw_ref[j:j + 1, :] * pltpu.roll(full, CONV_W - 1 - j, 0)[8:8 + rows]
    return conv * _sigmoid(conv)


def _inproj_body(x_ref, sc_ref, sh_ref, g_ref, cos_ref, sin_ref, w_ref, wg_ref, wt_ref,
                 q_ref, k_ref, kb_ref, v_ref, vb_ref, qk_ref, vm_ref, om_ref, ga_ref, gb_ref, gt_ref, vt_ref, kt_ref):
    x = x_ref[...]
    ms = jnp.mean(x * x, axis=1, keepdims=True)
    h = (x * lax.rsqrt(ms + EPS)) * g_ref[...] * (1.0 + sc_ref[...]) + sh_ref[...]
    hb = h.astype(BF16)
    cos = cos_ref[...]
    sin = sin_ref[...]

    def seg(lo, n):
        return _dot(hb, w_ref[:, lo:lo + n])

    q = _rope(seg(C_QA, W_QA), cos, sin) * (DK_A ** -0.5 * LOG2_E)
    q_ref[...] = q.astype(BF16)
    k = _rope(seg(C_KA, W_QA), cos, sin)
    k_ref[...] = k
    kb_ref[...] = k.astype(BF16)
    kt_ref[...] = k.T
    v = seg(C_VA, W_VA)
    for h in range(H_A):
        v_ref[pl.ds(h, v.shape[0], stride=H_A), :] = v[:, h * DV_A:(h + 1) * DV_A]
    vb_ref[...] = v.astype(BF16)
    for s in range(vt_ref.shape[0]):
        vt_ref[s] = v[s * ATT_TILE:(s + 1) * ATT_TILE].T.astype(BF16)
    qk_ref[...] = seg(C_QK, 2 * W_M)
    vm_ref[...] = seg(C_VM, W_M).astype(BF16)
    om_ref[...] = seg(C_OM, W_M)
    ga_ref[...] = _dot(hb, wg_ref[:, :D_MODEL])
    gb_ref[...] = _dot(hb, wg_ref[:, D_MODEL:])
    gt_ref[...] = _dot(hb, wt_ref[...])


def _inproj(x, sc, sh, g1, cos, sin, w_parts, rows_per_mod, tile):
    n = x.shape[0]
    nt = n // tile
    tiles_per_group = nt // sc.shape[0]
    tab_tiles = cos.shape[0] // tile
    slabs = tile // ATT_TILE
    tok = lambda w: pl.BlockSpec((tile, w), lambda i: (i, 0))
    mod = pl.BlockSpec((None, rows_per_mod, D_MODEL), lambda i: (i // tiles_per_group, 0, 0))
    tab = pl.BlockSpec((tile, LANES), lambda i: (i % tab_tiles, 0))
    tok_out = lambda w, dt: (tok(w), jax.ShapeDtypeStruct((n, w), dt))
    outs = [
        tok_out(W_QA, BF16),
        tok_out(W_QA, F32), tok_out(W_QA, BF16),
        (pl.BlockSpec((tile * H_A, DV_A), lambda i: (i, 0)), jax.ShapeDtypeStruct((n * H_A, DV_A), F32)),
        tok_out(W_VA, BF16),
        tok_out(2 * W_M, F32), tok_out(W_M, BF16), tok_out(W_M, F32),
        tok_out(D_MODEL, F32), tok_out(D_MODEL, F32), tok_out(LANES, F32),
        (pl.BlockSpec((slabs, W_VA, ATT_TILE), lambda i: (i, 0, 0)),
         jax.ShapeDtypeStruct((n // ATT_TILE, W_VA, ATT_TILE), BF16)),
        (pl.BlockSpec((None, W_QA, tile), lambda i: (i // tab_tiles, 0, i % tab_tiles)),
         jax.ShapeDtypeStruct((nt // tab_tiles, W_QA, tab_tiles * tile), F32)),
    ]
    return pl.pallas_call(
        _inproj_body,
        grid=(nt,),
        in_specs=[tok(D_MODEL), mod, mod, pl.BlockSpec((1, D_MODEL), lambda i: (0, 0)), tab, tab,
                  *[pl.BlockSpec(w.shape, lambda i: (0, 0), pipeline_mode=pl.Buffered(1)) for w in w_parts]],
        out_specs=[spec for spec, _ in outs],
        out_shape=[shape for _, shape in outs],
        compiler_params=_params(("arbitrary",)),
        name="inproj",
    )(x, sc, sh, g1, cos, sin, *w_parts)


def _rope_tables(pos):
    half = DK_A // 2
    inv = ROPE_THETA ** (-np.arange(half, dtype=np.float64) * 2.0 / DK_A)
    ang = np.asarray(pos, np.float64)[:, None] * inv[None, :]
    cos = np.cos(ang)
    sin = np.sin(ang)
    cos64 = np.concatenate([cos, cos], axis=1)
    sin64 = np.concatenate([-sin, sin], axis=1)
    return (jnp.asarray(np.tile(cos64, (1, LANES // DK_A)), F32),
            jnp.asarray(np.tile(sin64, (1, LANES // DK_A)), F32))


def _lambda_value(lam_ref, lam_init):
    lv = lam_ref[...]
    l1 = jnp.sum(lv[0:1, :] * lv[1:2, :], axis=1, keepdims=True)
    l2 = jnp.sum(lv[2:3, :] * lv[3:4, :], axis=1, keepdims=True)
    return jnp.exp(l1) - jnp.exp(l2) + lam_init


def _subln(o, g, lam_init):
    ms = jnp.mean(o * o, axis=1, keepdims=True)
    return (o * lax.rsqrt(ms + EPS)) * g * (1.0 - lam_init)


def _prompt_tile(q_ref, k_ref, vt_ref, lam, g_ref, o_ref, m_scr, acc_scr, i, lam_init):
    tq = ATT_TILE
    lane = lax.broadcasted_iota(jnp.int32, (tq, LANES), 1)
    qs = []
    for hh in range(ATT_HEADS):
        q = q_ref[:, hh * LANES:(hh + 1) * LANES]
        zero = jnp.zeros_like(q)
        qs.append(jnp.concatenate([jnp.where(lane < DK_A, q, zero), jnp.where(lane >= DK_A, q, zero)], axis=0))
    for hh in range(ATT_HEADS):
        m_scr[hh][...] = jnp.full(m_scr[hh].shape, NEG_BIG, F32)
        acc_scr[hh][...] = jnp.zeros(acc_scr[hh].shape, F32)
    ones = jnp.ones((ROW_ALIGN, tq), BF16)

    def scores(hh, j):
        start = pl.multiple_of(j * tq, tq)
        return _dot_nt(k_ref[pl.ds(start, tq), hh * LANES:(hh + 1) * LANES], qs[hh])

    def update_all(j, mask):
        sts = [scores(hh, j) for hh in range(ATT_HEADS)]
        if mask is not None:
            sts = [jnp.where(mask, st, NEG_BIG) for st in sts]
        pts, alphas = [], []
        for hh in range(ATT_HEADS):
            m_old = m_scr[hh][...]
            m_new = jnp.maximum(m_old, jnp.max(sts[hh], axis=0, keepdims=True))
            alphas.append(jnp.exp2(m_old - m_new))
            pts.append(jnp.exp2(sts[hh] - m_new).astype(BF16))
            m_scr[hh][...] = m_new
        for hh in range(ATT_HEADS):
            vt = jnp.concatenate([vt_ref[j, hh * LANES:(hh + 1) * LANES, :], ones], axis=0)
            acc_scr[hh][...] = alphas[hh] * acc_scr[hh][...] + _dot(vt, pts[hh])

    def off_diag(j, carry):
        update_all(j, None)
        return carry

    lax.fori_loop(0, i, off_diag, 0)
    key = lax.broadcasted_iota(jnp.int32, (tq, 2 * tq), 0)
    qry = lax.broadcasted_iota(jnp.int32, (tq, 2 * tq), 1) % tq
    update_all(i, key <= qry)
    for hh in range(ATT_HEADS):
        acc = acc_scr[hh][...]
        ot = acc[:DV_A] / acc[DV_A:DV_A + 1]
        at = ot[:, :tq] - lam * ot[:, tq:]
        ms = jnp.mean(at * at, axis=0, keepdims=True)
        at = (at * lax.rsqrt(ms + EPS)) * g_ref[...] * (1.0 - lam_init)
        o_ref[:, hh * LANES:(hh + 1) * LANES] = at.T.astype(BF16)


def _sample_chunks(pt_ref, q_ref, kn_ref, vn_ref, lam, g_ref, kpool_ref, vpool_ref, o_ref,
                   kbuf, vbuf, sem, m_scr, l_scr, acc_scr, step, parts, *, lam_init, n_chunks, n_seq, t_new):
    steps_per_seq = n_chunks // CHUNKS_PER_STEP
    b = step // steps_per_seq
    c0 = (step % steps_per_seq) * CHUNKS_PER_STEP
    opens = 0 in parts
    closes = CHUNKS_PER_STEP - 1 in parts

    def chunk_copies(bb, c, slot):
        out = []
        for j in range(PAGES_PER_STEP):
            page = pt_ref[bb, c * PAGES_PER_STEP + j]
            out.append(pltpu.make_async_copy(kpool_ref.at[page], kbuf.at[slot, j], sem.at[slot]))
            out.append(pltpu.make_async_copy(vpool_ref.at[page], vbuf.at[slot, j], sem.at[slot]))
        return out

    if opens:
        @pl.when(step == 0)
        def _():
            _start_all(chunk_copies(0, 0, 0) + chunk_copies(0, 1, 1))

    q = q_ref[...].astype(F32)
    qt = jnp.concatenate([q] * (2 * H_A), axis=0)
    row = lax.broadcasted_iota(jnp.int32, qt.shape, 0)
    col = lax.broadcasted_iota(jnp.int32, qt.shape, 1)
    qbd = jnp.where(col // DK_A == row // t_new, qt, 0.0).astype(BF16)

    if opens:
        @pl.when(c0 == 0)
        def _():
            m_scr[...] = jnp.full(m_scr.shape, NEG_BIG, F32)
            l_scr[...] = jnp.zeros(l_scr.shape, F32)
            acc_scr[...] = jnp.zeros(acc_scr.shape, F32)

    rows_h = 2 * t_new

    def update(s, v):
        m_old = m_scr[...]
        m_new = jnp.maximum(m_old, jnp.max(s, axis=1, keepdims=True))
        alpha = jnp.exp2(m_old - m_new)
        p = jnp.exp2(s - m_new)
        l_scr[...] = alpha * l_scr[...] + jnp.sum(p, axis=1, keepdims=True)
        pv = _dot(p.astype(BF16), v)
        own = [pv[h * rows_h:(h + 1) * rows_h, h * DV_A:(h + 1) * DV_A] for h in range(H_A)]
        acc_scr[...] = alpha * acc_scr[...] + jnp.concatenate(own, axis=0)
        m_scr[...] = m_new

    def chunk(c, slot):
        for cp in chunk_copies(b, c, slot):
            cp.wait()
        kt = jnp.concatenate([kbuf[slot, j].astype(BF16) for j in range(PAGES_PER_STEP)], axis=1)

        def page_v(j):
            return jnp.concatenate([vbuf.at[slot, j][pl.ds(h, PAGE_SIZE, stride=H_A), :].astype(BF16)
                                    for h in range(H_A)], axis=1)

        v = jnp.concatenate([page_v(j) for j in range(PAGES_PER_STEP)], axis=0)
        update(_dot(qbd, kt), v)

        @pl.when(c + 2 < n_chunks)
        def _():
            _start_all(chunk_copies(b, c + 2, slot))

        @pl.when(jnp.logical_and(c + 2 >= n_chunks, b + 1 < n_seq))
        def _():
            _start_all(chunk_copies(b + 1, c + 2 - n_chunks, slot))

    for j in parts:
        chunk(c0 + j, j % 2)

    def finish():
        zpad = jnp.zeros((PAGE_SIZE - t_new, W_QA), F32)
        kn = jnp.concatenate([kn_ref[...].astype(F32), zpad], axis=0).astype(BF16)
        vn = jnp.concatenate([vn_ref[...].astype(F32), zpad], axis=0).astype(BF16)
        s = _dot_nt(qbd, kn)
        row = lax.broadcasted_iota(jnp.int32, s.shape, 0) % t_new
        col = lax.broadcasted_iota(jnp.int32, s.shape, 1)
        update(jnp.where(col <= row, s, NEG_BIG), vn)
        o = acc_scr[...] / l_scr[...]
        outs = []
        for h in range(H_A):
            r0 = h * rows_h
            outs.append(_subln(o[r0:r0 + t_new] - lam * o[r0 + t_new:r0 + rows_h], g_ref[...], lam_init))
        o_ref[...] = jnp.concatenate(outs, axis=1).astype(BF16)

    if closes:
        pl.when(c0 + CHUNKS_PER_STEP == n_chunks)(finish)


def _attention_body(pt_ref, q_ref, k_ref, vt_ref, lam_ref, gcol_ref, qs_ref, kn_ref, vn_ref, grow_ref,
                    kpool_ref, vpool_ref, o_ref, os_ref, *scratch, lam_init, n_chunks, n_seq, t_new):
    m_scr, acc_scr = scratch[:ATT_HEADS], scratch[ATT_HEADS:2 * ATT_HEADS]
    kbuf, vbuf, sem, ms_scr, ls_scr, accs_scr = scratch[2 * ATT_HEADS:]
    step = pl.program_id(0) * pl.num_programs(1) + pl.program_id(1)
    lam = _lambda_value(lam_ref, lam_init)
    sample = functools.partial(_sample_chunks, pt_ref, qs_ref, kn_ref, vn_ref, lam, grow_ref, kpool_ref, vpool_ref,
                               os_ref, kbuf, vbuf, sem, ms_scr, ls_scr, accs_scr, step,
                               lam_init=lam_init, n_chunks=n_chunks, n_seq=n_seq, t_new=t_new)
    sample((0,))
    _prompt_tile(q_ref, k_ref, vt_ref, lam, gcol_ref, o_ref, m_scr, acc_scr, pl.program_id(1), lam_init)
    sample((1,))


def _attention(q, k, vt, q_s, k_new, v_new, cache_k, cache_v, page_table, lam_vec, g_subln, batch, seq,
               lam_init, t_new):
    nq = seq // ATT_TILE
    bd, n_pages = page_table.shape
    n_chunks = n_pages // PAGES_PER_STEP
    steps_per_seq = n_chunks // CHUNKS_PER_STEP
    assert ATT_HEADS == H_A and CHUNKS_PER_STEP % 2 == 0 and n_chunks % CHUNKS_PER_STEP == 0
    assert batch * nq == bd * steps_per_seq
    n_rows = 2 * H_A * t_new
    kv = pl.BlockSpec((seq, W_QA), lambda b, i, pt: (b, 0))
    vts = pl.BlockSpec((nq, W_VA, ATT_TILE), lambda b, i, pt: (b, 0, 0))
    qo = pl.BlockSpec((ATT_TILE, W_QA), lambda b, i, pt: (b * nq + i, 0))
    new = pl.BlockSpec((None, t_new, W_QA), lambda b, i, pt: ((b * nq + i) // steps_per_seq, 0, 0))
    const = lambda shape: pl.BlockSpec(shape, lambda b, i, pt: (0, 0))
    page_buf = pltpu.VMEM((2, PAGES_PER_STEP, W_QA, PAGE_SIZE), F32)
    grid_spec = pltpu.PrefetchScalarGridSpec(
        num_scalar_prefetch=1,
        grid=(batch, nq),
        in_specs=[qo, kv, vts, const((4, DK_A)), const((DV_A, 1)), new, new, new, const((1, DV_A)),
                  pl.BlockSpec(memory_space=pl.ANY), pl.BlockSpec(memory_space=pl.ANY)],
        out_specs=[qo, new],
        scratch_shapes=[pltpu.VMEM((1, 2 * ATT_TILE), F32)] * ATT_HEADS
                       + [pltpu.VMEM((DV_A + ROW_ALIGN, 2 * ATT_TILE), F32)] * ATT_HEADS
                       + [page_buf, page_buf, pltpu.SemaphoreType.DMA((2,)), pltpu.VMEM((n_rows, 1), F32),
                          pltpu.VMEM((n_rows, 1), F32), pltpu.VMEM((n_rows, DV_A), F32)],
    )
    o_p, o_s = pl.pallas_call(
        functools.partial(_attention_body, lam_init=lam_init, n_chunks=n_chunks, n_seq=bd, t_new=t_new),
        grid_spec=grid_spec,
        out_shape=[jax.ShapeDtypeStruct((batch * seq, W_VA), BF16), jax.ShapeDtypeStruct((bd, t_new, W_VA), BF16)],
        compiler_params=_params(("arbitrary", "arbitrary")),
        name="attention",
    )(page_table, q, k, vt, lam_vec, g_subln.reshape(DV_A, 1), q_s.reshape(bd, t_new, W_QA),
      k_new.reshape(bd, t_new, W_QA), v_new.reshape(bd, t_new, W_VA), g_subln, cache_k, cache_v)
    return o_p, o_s.reshape(bd * t_new, W_VA)


def _mlstm_body(qk_ref, vm_ref, om_ref, gt_ref, cw_ref, cb_ref, bg_ref, gm_ref, cbuf_ref, c0_ref, n0_ref, m0_ref,
                h_ref, cst_ref, cout_ref, nout_ref, mout_ref, ext_scr, c_scr, n_scr, m_scr, *, tb, L, nc):
    c_idx = pl.program_id(1)

    @pl.when(c_idx == 0)
    def _():
        ext_scr[...] = jnp.zeros(ext_scr.shape, F32)
        ext_scr[8 - (CONV_W - 1):8, :] = cbuf_ref[...]
        c_scr[...] = c0_ref[...]
        n_scr[...] = n0_ref[...]
        m_scr[...] = m0_ref[...]

    pad = L - tb
    u = qk_ref[...]
    if pad:
        u = jnp.concatenate([u, jnp.zeros((pad, u.shape[1]), F32)], axis=0)
    a = _conv_silu(u, ext_scr[...], cw_ref, cb_ref)
    if not pad:
        ext_scr[...] = u[L - 8:L]

    @pl.when(c_idx == nc - 1)
    def _():
        cst_ref[...] = qk_ref[tb - (CONV_W - 1):tb, :]

    g = gt_ref[...] + bg_ref[...]
    li = g
    lf = jnp.minimum(g, 0.0) - jnp.log1p(jnp.exp(-jnp.abs(g)))
    if pad:
        zpad = jnp.zeros((pad, LANES), F32)
        li = jnp.concatenate([li, zpad + NEG_BIG], axis=0)
        lf = jnp.concatenate([lf, zpad], axis=0)
    row = lax.broadcasted_iota(jnp.int32, (L, LANES), 0)
    lane = lax.broadcasted_iota(jnp.int32, (L, LANES), 1)
    bcum = lf
    shift = 1
    while shift < L:
        bcum = bcum + jnp.where(row >= shift, pltpu.roll(bcum, shift, 0), 0.0)
        shift *= 2
    gates = jnp.where(lane < H_M, li, bcum)
    gates_t = gates.T
    tri = lax.broadcasted_iota(jnp.int32, (L, L), 0) >= lax.broadcasted_iota(jnp.int32, (L, L), 1)
    m_all = m_scr[...]
    lane1 = lax.broadcasted_iota(jnp.int32, (1, LANES), 1)
    m_next = m_all
    vall = vm_ref[...]
    if pad:
        vall = jnp.concatenate([vall, jnp.zeros((pad, vall.shape[1]), BF16)], axis=0)

    for h in range(H_M):
        li_col = gates[:, h:h + 1]
        b_col = gates[:, H_M + h:H_M + h + 1]
        src_row = gates_t[h:h + 1, :] - gates_t[H_M + h:H_M + h + 1, :]
        m_prev = m_all[:, h:h + 1]
        b_last = b_col[L - 1:L, :]
        log_d = jnp.where(tri, b_col + src_row, NEG_BIG)
        inter = b_col + m_prev
        mt = jnp.maximum(inter, jnp.max(log_d, axis=1, keepdims=True))
        q = a[:, h * DH_M:(h + 1) * DH_M]
        k = a[:, W_M + h * DH_M:W_M + (h + 1) * DH_M] * (DH_M ** -0.5)
        v = vall[:, h * DH_M:(h + 1) * DH_M]
        qb = q.astype(BF16)
        s = _dot_nt(qb, k.astype(BF16)) * jnp.exp(log_d - mt)
        ei = jnp.exp(inter - mt)
        c_old = c_scr[h]
        n_old = n_scr[h:h + 1, :]
        num = ei * _dot(qb, c_old.astype(BF16)) + _dot(s.astype(BF16), v)
        den = ei * jnp.sum(q * n_old, axis=1, keepdims=True) + jnp.sum(s, axis=1, keepdims=True)
        hh = num / jnp.maximum(jnp.abs(den), jnp.exp(-mt))
        g_col = b_last - b_col + li_col
        bl = b_last + m_prev
        m_new = jnp.maximum(bl, jnp.max(g_col, axis=0, keepdims=True))
        wg = jnp.exp(g_col - m_new)
        decay = jnp.exp(bl - m_new)
        kw = k * wg
        c_scr[h] = decay * c_old + _dot(kw.T.astype(BF16), v)
        n_scr[h:h + 1, :] = decay * n_old + jnp.sum(kw, axis=0, keepdims=True)
        m_next = jnp.where(lane1 == h, m_new, m_next)
        ms = jnp.mean(hh * hh, axis=1, keepdims=True)
        hn = (hh * lax.rsqrt(ms + EPS)) * gm_ref[h:h + 1, :]
        og = _sigmoid(om_ref[:, h * DH_M:(h + 1) * DH_M])
        h_ref[:, h * DH_M:(h + 1) * DH_M] = (hn[:tb] * og).astype(BF16)

    m_scr[...] = m_next

    @pl.when(c_idx == nc - 1)
    def _():
        cout_ref[...] = c_scr[...]
        nout_ref[...] = n_scr[...]
        mout_ref[...] = m_scr[...]


def _mlstm(qk, vm, om, gt, w_conv, b_conv, b_gates_pad, g_mnorm, conv_buf, c0, n0, m0_pad, batch, seq):
    tb = min(seq, MLSTM_CHUNK)
    L = max(tb, LANES)
    nc = seq // tb
    tok = lambda w: pl.BlockSpec((None, tb, w), lambda b, c: (b * nc + c, 0, 0))
    chunks = lambda a: a.reshape(batch * nc, tb, a.shape[-1])
    const = lambda shape: pl.BlockSpec(shape, lambda b, c: (0,) * len(shape))
    per_b = lambda shape: pl.BlockSpec((None,) + shape, lambda b, c: (b,) + (0,) * len(shape))
    h, cst, c_out, n_out, m_out = pl.pallas_call(
        functools.partial(_mlstm_body, tb=tb, L=L, nc=nc),
        grid=(batch, nc),
        in_specs=[tok(2 * W_M), tok(W_M), tok(W_M), tok(LANES), const((CONV_W, 2 * W_M)), const((1, 2 * W_M)),
                  const((1, LANES)), const((H_M, DH_M)), per_b((CONV_W - 1, 2 * W_M)),
                  per_b((H_M, DH_M, DH_M)), per_b((H_M, DH_M)), per_b((1, LANES))],
        out_specs=[tok(W_M), per_b((CONV_W - 1, 2 * W_M)), per_b((H_M, DH_M, DH_M)), per_b((H_M, DH_M)),
                   per_b((1, LANES))],
        out_shape=[jax.ShapeDtypeStruct((batch * nc, tb, W_M), BF16),
                   jax.ShapeDtypeStruct((batch, CONV_W - 1, 2 * W_M), F32),
                   jax.ShapeDtypeStruct((batch, H_M, DH_M, DH_M), F32),
                   jax.ShapeDtypeStruct((batch, H_M, DH_M), F32),
                   jax.ShapeDtypeStruct((batch, 1, LANES), F32)],
        scratch_shapes=[pltpu.VMEM((8, 2 * W_M), F32), pltpu.VMEM((H_M, DH_M, DH_M), F32),
                        pltpu.VMEM((H_M, DH_M), F32), pltpu.VMEM((1, LANES), F32)],
        compiler_params=_params(("arbitrary", "arbitrary")),
        name="mlstm",
    )(chunks(qk), chunks(vm), chunks(om), chunks(gt), w_conv, b_conv, b_gates_pad, g_mnorm, conv_buf, c0, n0,
      m0_pad)
    return h.reshape(batch * seq, W_M), cst, c_out, n_out, m_out


def _merge_body(oa_ref, hm_ref, ga_ref, gb_ref, x_ref, gt1_ref, sc2_ref, sh2_ref, g2_ref, wa_ref, wb_ref, wo_ref,
                wr_ref, br_ref, *rest):
    x2_ref, h2_ref, route_ref, cnt_ref = rest[-4:]
    ya = _dot(oa_ref[...], wa_ref[...])
    yb = _dot(hm_ref[...], wb_ref[...])
    mix = _sigmoid(ga_ref[...]) * ya + _sigmoid(gb_ref[...]) * yb
    y = _dot(mix.astype(BF16), wo_ref[...])
    x2 = x_ref[...] + gt1_ref[...] * y
    x2_ref[...] = x2
    ms = jnp.mean(x2 * x2, axis=1, keepdims=True)
    h2 = (x2 * lax.rsqrt(ms + EPS)) * g2_ref[...] * (1.0 + sc2_ref[...]) + sh2_ref[...]
    h2b = h2.astype(BF16)
    h2_ref[...] = h2b

    tm = h2b.shape[0]
    logits_t = (_dot(h2b, wr_ref[...]) + br_ref[...]).T[:N_EXP, :]
    row = lax.broadcasted_iota(jnp.int32, (N_EXP, tm), 0)
    row_f = row.astype(F32)
    work = logits_t
    vals, hots = [], []
    for _ in range(TOP_K):
        mx = jnp.max(work, axis=0, keepdims=True)
        idx = jnp.min(jnp.where(work == mx, row, N_EXP), axis=0, keepdims=True)
        hot = row == idx
        vals.append(mx)
        hots.append(hot)
        work = jnp.where(hot, 2.0 * NEG_BIG, work)
    es = [jnp.exp(v - vals[0]) for v in vals]
    den = es[0]
    for e in es[1:]:
        den = den + e
    sel_t = jnp.zeros((N_EXP, tm), F32)
    for hot in hots:
        sel_t = jnp.where(hot, 1.0, sel_t)
    r_i = lax.broadcasted_iota(jnp.int32, (tm, tm), 0)
    c_i = lax.broadcasted_iota(jnp.int32, (tm, tm), 1)
    rank_t = _dot(sel_t.astype(BF16), jnp.where(r_i < c_i, 1.0, 0.0).astype(BF16))
    sub = lax.broadcasted_iota(jnp.int32, (ROUTE_SEL, tm), 0)
    head = jnp.zeros((ROUTE_SEL, tm), F32)
    for k in range(TOP_K):
        e_k = jnp.sum(jnp.where(hots[k], row_f, 0.0), axis=0, keepdims=True)
        r_k = jnp.sum(jnp.where(hots[k], rank_t, 0.0), axis=0, keepdims=True)
        head = jnp.where(sub == k, e_k, head)
        head = jnp.where(sub == TOP_K + k, es[k] / den, head)
        head = jnp.where(sub == 2 * TOP_K + k, r_k, head)
    pad = jnp.zeros((LANES - ROUTE_SEL - N_EXP, tm), F32)
    route = jnp.concatenate([head, sel_t, pad], axis=0).T
    route_ref[...] = route
    cnt_ref[...] = jnp.sum(route, axis=0, keepdims=True)


def _merge(oa, hm, ga, gb, x, gt1, sc2, sh2, g2, wa, wb, wo, wr, br, rows_per_mod, tile0, n_all, prev=None):
    n = x.shape[0]
    nt = n // TOK_TILE
    nt_all = n_all // TOK_TILE
    tiles_per_group = nt // gt1.shape[0]
    tok = lambda w: pl.BlockSpec((TOK_TILE, w), lambda i: (i, 0))
    mod = pl.BlockSpec((None, rows_per_mod, D_MODEL), lambda i: (i // tiles_per_group, 0, 0))
    res = lambda shape: pl.BlockSpec(shape, lambda i: (0, 0), pipeline_mode=pl.Buffered(1))
    out_tok = lambda w: pl.BlockSpec((TOK_TILE, w), lambda i: (tile0 + i, 0))
    in_specs = [tok(W_VA), tok(W_M), tok(D_MODEL), tok(D_MODEL), tok(D_MODEL), mod, mod, mod,
                pl.BlockSpec((1, D_MODEL), lambda i: (0, 0)),
                res((W_VA, D_MODEL)), res((W_M, D_MODEL)), res((D_MODEL, D_MODEL)), res((D_MODEL, LANES)),
                pl.BlockSpec((1, LANES), lambda i: (0, 0))]
    args = [oa, hm, ga, gb, x, gt1, sc2, sh2, g2, wa, wb, wo, wr, br]
    aliases = {}
    if prev is not None:
        in_specs += [pl.BlockSpec(memory_space=pl.ANY)] * 4
        aliases = {len(args) + j: j for j in range(4)}
        args += list(prev)
    return pl.pallas_call(
        _merge_body,
        grid=(nt,),
        in_specs=in_specs,
        out_specs=[out_tok(D_MODEL), out_tok(D_MODEL), out_tok(LANES),
                   pl.BlockSpec((None, 1, LANES), lambda i: (tile0 + i, 0, 0))],
        out_shape=[jax.ShapeDtypeStruct((n_all, D_MODEL), F32), jax.ShapeDtypeStruct((n_all, D_MODEL), BF16),
                   jax.ShapeDtypeStruct((n_all, LANES), F32), jax.ShapeDtypeStruct((nt_all, 1, LANES), F32)],
        input_output_aliases=aliases,
        compiler_params=_params(("arbitrary",)),
        name="merge",
    )(*args)


def _segment_copies(src, dst, sem, src_row, dst_row, n_groups, max_groups):
    out = []
    bit = 1
    while bit * 2 <= max_groups:
        bit *= 2
    while bit >= 1:
        off = (n_groups // (2 * bit)) * (2 * bit) * ROW_ALIGN
        rows = bit * ROW_ALIGN
        cp = pltpu.make_async_copy(src.at[pl.ds(pl.multiple_of(src_row + off, ROW_ALIGN), rows)],
                                   dst.at[pl.ds(pl.multiple_of(dst_row + off, ROW_ALIGN), rows)], sem)
        out.append(((n_groups // bit) % 2 == 1, cp))
        bit //= 2
    return out


def _run_copies(copies):
    for pred, cp in copies:
        pl.when(pred)(cp.start)
    for pred, cp in copies:
        pl.when(pred)(cp.wait)


def _slot_rows(route_t, loff_col, k):
    e_row = route_t[k:k + 1, :]
    r_row = route_t[2 * TOP_K + k:2 * TOP_K + k + 1, :]
    sub = lax.broadcasted_iota(jnp.int32, (LANES, route_t.shape[1]), 0).astype(F32)
    return jnp.sum(jnp.where(sub == e_row, loff_col, 0.0), axis=0, keepdims=True) + r_row


def _for_groups(n_groups, table_ref, base, make_copy, start):
    def body(i, carry):
        for j in range(GROUP_UNROLL):
            g = i * GROUP_UNROLL + j
            cp = make_copy(g, table_ref[base + g])
            if start:
                cp.start()
            else:
                cp.wait()
        return carry
    lax.fori_loop(0, (n_groups + GROUP_UNROLL - 1) // GROUP_UNROLL, body, 0)


def _dispatch_body(ng_ref, dt_ref, lo_ref, sv_ref, rm_ref, tail_ref, h2_ref, route_ref, soff_ref, xs_ref,
                   loc_scr, carry_scr, zero_scr, sem, *, nt):
    t = pl.program_id(0)
    buf = t % 2

    def copies_of(tt):
        def make(g, row):
            return pltpu.make_async_copy(
                loc_scr.at[tt % 2, pl.ds(pl.multiple_of(g * ROW_ALIGN, ROW_ALIGN), ROW_ALIGN)],
                xs_ref.at[pl.ds(pl.multiple_of(row, ROW_ALIGN), ROW_ALIGN)], sem.at[tt % 2])
        return ng_ref[tt], dt_ref, tt * GROUPS, make

    @pl.when(t == 0)
    def _():
        carry_scr[...] = jnp.zeros(carry_scr.shape, BF16)
        loc_scr[...] = jnp.zeros(loc_scr.shape, BF16)

    @pl.when(t >= 2)
    def _():
        _for_groups(*copies_of(t - 2), start=False)

    route_t = route_ref[...].T
    soff_col = soff_ref[...]
    slots = [_slot_rows(route_t, soff_col, k) for k in range(TOP_K)]

    def sort_rows(m):
        r_i = lax.broadcasted_iota(jnp.int32, (m, TOK_TILE), 0).astype(F32)
        onehot = jnp.zeros((m, TOK_TILE), F32)
        for k in range(TOP_K):
            onehot = jnp.where(r_i == slots[k], 1.0, onehot)
        loc_scr[buf, 0:m, :] = _dot(onehot.astype(BF16), h2_ref[...]).astype(BF16)

    fits = ng_ref[t] * ROW_ALIGN <= M_LOC_SHORT
    pl.when(fits)(lambda: sort_rows(M_LOC_SHORT))
    pl.when(jnp.logical_not(fits))(lambda: sort_rows(M_LOC))

    for e in range(N_EXP):
        lo = pl.multiple_of(lo_ref[t * N_EXP + e], ROW_ALIGN)
        sv = pl.multiple_of(sv_ref[t * N_EXP + e], ROW_ALIGN)
        loc_scr[buf, pl.ds(lo, ROW_ALIGN), :] = loc_scr[buf, pl.ds(lo, ROW_ALIGN), :] + carry_scr[e]
        pending = loc_scr[buf, pl.ds(sv, ROW_ALIGN), :]
        carry_scr[e] = jnp.where(rm_ref[t * N_EXP + e] > 0, pending, jnp.zeros_like(pending))

    _for_groups(*copies_of(t), start=True)

    @pl.when(t == nt - 1)
    def _():
        if nt >= 2:
            _for_groups(*copies_of(t - 1), start=False)
        _for_groups(*copies_of(t), start=False)
        zero_scr[...] = jnp.zeros(zero_scr.shape, BF16)
        tails = []
        for e in range(N_EXP):
            tails += _segment_copies(zero_scr, xs_ref, sem.at[0], 0, tail_ref[e], tail_ref[N_EXP + e],
                                     FFN_TILE // ROW_ALIGN - 1)
        _run_copies(tails)


def _dispatch(meta, h2, route, soff_col):
    nt = h2.shape[0] // TOK_TILE
    grid_spec = pltpu.PrefetchScalarGridSpec(
        num_scalar_prefetch=6,
        grid=(nt,),
        in_specs=[pl.BlockSpec((TOK_TILE, D_MODEL), lambda t, *_: (t, 0)),
                  pl.BlockSpec((TOK_TILE, LANES), lambda t, *_: (t, 0)),
                  pl.BlockSpec((None, LANES, 1), lambda t, *_: (t, 0, 0))],
        out_specs=pl.BlockSpec(memory_space=pl.ANY),
        scratch_shapes=[pltpu.VMEM((2, M_LOC + ROW_ALIGN, D_MODEL), BF16),
                        pltpu.VMEM((N_EXP, ROW_ALIGN, D_MODEL), BF16),
                        pltpu.VMEM((FFN_TILE, D_MODEL), BF16), pltpu.SemaphoreType.DMA((2,))],
    )
    return pl.pallas_call(
        functools.partial(_dispatch_body, nt=nt),
        grid_spec=grid_spec,
        out_shape=jax.ShapeDtypeStruct((meta["rows"] + 2 * M_LOC, D_MODEL), BF16),
        compiler_params=_params(("arbitrary",)),
        name="moe_dispatch",
    )(meta["n_groups"], meta["dispatch_rows"], meta["slot_start"], meta["carry_start"], meta["carry_rows"],
      meta["tail"], h2, route, soff_col)


def _ffn_body(be_ref, nu_ref, nx_ref, sl_ref, hf_ref, x_ref, bgu_ref, bd_ref, wgu_hbm, wd_hbm, y_ref,
              wgu_f32, wd_f32, wgu_scr, wd_scr, sem):
    i = pl.program_id(0)
    expert = be_ref[i]
    slot = sl_ref[i]
    prev = be_ref[jnp.maximum(i - 1, 0)]

    def weight_copies(e, s):
        return (pltpu.make_async_copy(wgu_hbm.at[e], wgu_f32.at[s], sem.at[s]),
                pltpu.make_async_copy(wd_hbm.at[e], wd_f32.at[s], sem.at[s]))

    @pl.when(i == 0)
    def _():
        for cp in weight_copies(expert, slot):
            cp.start()

    @pl.when(jnp.logical_and(i < nu_ref[0], jnp.logical_or(i == 0, expert != prev)))
    def _():
        for cp in weight_copies(expert, slot):
            cp.wait()
        wgu_scr[...] = wgu_f32[slot].astype(BF16)
        wd_scr[...] = wd_f32[slot].astype(BF16)

        @pl.when(nx_ref[i] >= 0)
        def _():
            for cp in weight_copies(nx_ref[i], 1 - slot):
                cp.start()

    def expert_rows(m):
        gu = _dot(x_ref[0:m, :], wgu_scr[...]) + bgu_ref[...]
        gate = jnp.minimum(gu[:, :D_FF], SWIGLU_LIMIT)
        up = jnp.clip(gu[:, D_FF:], -SWIGLU_LIMIT, SWIGLU_LIMIT)
        act = (up + 1.0) * gate * _sigmoid(SWIGLU_ALPHA * gate)
        y_ref[0:m, :] = (_dot(act.astype(BF16), wd_scr[...]) + bd_ref[...]).astype(BF16)

    active = i < nu_ref[0]
    pl.when(jnp.logical_and(active, hf_ref[i] == 0))(lambda: expert_rows(FFN_TILE))
    pl.when(jnp.logical_and(active, hf_ref[i] != 0))(lambda: expert_rows(FFN_TILE // 2))


def _ffn(meta, xs, w_gu, b_gu, w_down, b_down):
    rows = meta["rows"]
    nblk = rows // FFN_TILE
    row_blk = pl.BlockSpec((FFN_TILE, D_MODEL), lambda i, be, nu, *_: (jnp.minimum(i, nu[0] - 1), 0))
    grid_spec = pltpu.PrefetchScalarGridSpec(
        num_scalar_prefetch=5,
        grid=(nblk,),
        in_specs=[row_blk,
                  pl.BlockSpec((None, 1, 2 * D_FF), lambda i, be, *_: (be[i], 0, 0)),
                  pl.BlockSpec((None, 1, D_MODEL), lambda i, be, *_: (be[i], 0, 0)),
                  pl.BlockSpec(memory_space=pl.ANY), pl.BlockSpec(memory_space=pl.ANY)],
        out_specs=row_blk,
        scratch_shapes=[pltpu.VMEM((2, D_MODEL, 2 * D_FF), F32), pltpu.VMEM((2, D_FF, D_MODEL), F32),
                        pltpu.VMEM((D_MODEL, 2 * D_FF), BF16), pltpu.VMEM((D_FF, D_MODEL), BF16),
                        pltpu.SemaphoreType.DMA((2,))],
    )
    return pl.pallas_call(
        _ffn_body,
        grid_spec=grid_spec,
        out_shape=jax.ShapeDtypeStruct((rows, D_MODEL), BF16),
        compiler_params=_params(("arbitrary",)),
        name="moe_ffn",
    )(meta["blk_exp"], meta["n_used"], meta["next_exp"], meta["blk_slot"], meta["blk_half"], xs,
      b_gu.reshape(N_EXP, 1, 2 * D_FF), b_down.reshape(N_EXP, 1, D_MODEL), w_gu, w_down)


def _combine_body(ng_ref, ct_ref, route_ref, soff_ref, x2_ref, gtp_ref, gts_ref, gf_ref, ys_ref,
                  yp_ref, ysm_ref, loc_scr, moe_scr, sem, *, nt, nt_prompt, final):
    t = pl.program_id(0)
    buf = t % 2

    def copies_of(tt):
        def make(g, row):
            return pltpu.make_async_copy(
                ys_ref.at[pl.ds(pl.multiple_of(row, ROW_ALIGN), ROW_ALIGN)],
                loc_scr.at[tt % 2, pl.ds(pl.multiple_of(g * ROW_ALIGN, ROW_ALIGN), ROW_ALIGN)], sem.at[tt % 2])
        return ng_ref[tt], ct_ref, tt * GROUPS, make

    @pl.when(t == 0)
    def _():
        loc_scr[...] = jnp.zeros(loc_scr.shape, BF16)
        _for_groups(*copies_of(0), start=True)

    @pl.when(t + 1 < nt)
    def _():
        _for_groups(*copies_of(t + 1), start=True)

    _for_groups(*copies_of(t), start=False)

    route = route_ref[...]
    soff_row = soff_ref[...]
    lane = lax.broadcasted_iota(jnp.int32, (TOK_TILE, LANES), 1).astype(F32)
    slots = [jnp.sum(jnp.where(lane == route[:, k:k + 1], soff_row, 0.0), axis=1, keepdims=True)
             + route[:, 2 * TOP_K + k:2 * TOP_K + k + 1] for k in range(TOP_K)]

    def weighted_sum(m):
        c_i = lax.broadcasted_iota(jnp.int32, (TOK_TILE, m), 1).astype(F32)
        wmat = jnp.zeros((TOK_TILE, m), F32)
        for k in range(TOP_K):
            wmat = jnp.where(c_i == slots[k], route[:, TOP_K + k:TOP_K + k + 1], wmat)
        moe_scr[...] = _dot(wmat.astype(BF16), loc_scr[buf, 0:m, :])

    fits = ng_ref[t] * ROW_ALIGN <= M_LOC_SHORT
    pl.when(fits)(lambda: weighted_sum(M_LOC_SHORT))
    pl.when(jnp.logical_not(fits))(lambda: weighted_sum(M_LOC))

    gate = jnp.where(t >= nt_prompt, gts_ref[...], gtp_ref[...])
    xo = x2_ref[...] + gate * moe_scr[...]
    if final:
        ms = jnp.mean(xo * xo, axis=1, keepdims=True)
        xo = (xo * lax.rsqrt(ms + EPS)) * gf_ref[...]

    @pl.when(t < nt_prompt)
    def _():
        yp_ref[...] = xo

    @pl.when(t >= nt_prompt)
    def _():
        ysm_ref[...] = xo


def _combine(meta, route, soff_row, x2, gt2_p, gt2_s, g_final, ys, n_prompt, n_sample, final):
    nt = x2.shape[0] // TOK_TILE
    nt_prompt = n_prompt // TOK_TILE
    tiles_per_batch = nt_prompt // gt2_p.shape[0]
    grid_spec = pltpu.PrefetchScalarGridSpec(
        num_scalar_prefetch=2,
        grid=(nt,),
        in_specs=[pl.BlockSpec((TOK_TILE, LANES), lambda t, *_: (t, 0)),
                  pl.BlockSpec((None, 1, LANES), lambda t, *_: (t, 0, 0)),
                  pl.BlockSpec((TOK_TILE, D_MODEL), lambda t, *_: (t, 0)),
                  pl.BlockSpec((None, 1, D_MODEL),
                               lambda t, *_: (jnp.minimum(t, nt_prompt - 1) // tiles_per_batch, 0, 0)),
                  pl.BlockSpec((TOK_TILE, D_MODEL), lambda t, *_: (0, 0)),
                  pl.BlockSpec((1, D_MODEL), lambda t, *_: (0, 0)),
                  pl.BlockSpec(memory_space=pl.ANY)],
        out_specs=[pl.BlockSpec((TOK_TILE, D_MODEL), lambda t, *_: (jnp.minimum(t, nt_prompt - 1), 0)),
                   pl.BlockSpec((TOK_TILE, D_MODEL), lambda t, *_: (0, 0))],
        scratch_shapes=[pltpu.VMEM((2, M_LOC, D_MODEL), BF16), pltpu.VMEM((TOK_TILE, D_MODEL), F32),
                        pltpu.SemaphoreType.DMA((2,))],
    )
    return pl.pallas_call(
        functools.partial(_combine_body, nt=nt, nt_prompt=nt_prompt, final=final),
        grid_spec=grid_spec,
        out_shape=[jax.ShapeDtypeStruct((n_prompt, D_MODEL), F32), jax.ShapeDtypeStruct((n_sample, D_MODEL), F32)],
        compiler_params=_params(("arbitrary",)),
        name="moe_combine",
    )(meta["n_groups"], meta["combine_rows"], route, soff_row, x2, gt2_p, gt2_s, g_final, ys)


def _moe_offsets(cnt):
    nt = cnt.shape[0]
    ra = ROW_ALIGN
    prefix = jnp.cumsum(cnt, axis=0) - cnt
    total = jnp.sum(cnt, axis=0)
    pending = prefix % ra
    used = pending + cnt
    seg = (used + ra - 1) // ra * ra
    lo = jnp.cumsum(seg, axis=1) - seg
    n_groups = jnp.sum(seg, axis=1) // ra
    gpad = (total + FFN_TILE - 1) // FFN_TILE * FFN_TILE
    gstart = jnp.cumsum(gpad) - gpad
    base = gstart[None, :] + prefix // ra * ra
    last = (jnp.arange(nt) == nt - 1)[:, None]
    n_write = jnp.where(last, seg // ra, used // ra)
    carry_start = lo + used // ra * ra
    carry_rows = jnp.where(last, 0, used % ra)
    g = jnp.arange(GROUPS)
    slot_end = (lo + seg) // ra
    owner = jnp.minimum(jnp.sum(g[None, :, None] >= slot_end[:, None, :], axis=2), N_EXP - 1)
    pick = lambda a: jnp.sum(jnp.where(owner[:, :, None] == jnp.arange(N_EXP), a[:, None, :], 0), axis=2)
    k = g[None, :] - pick(lo) // ra
    row = pick(base) + k * ra
    valid = g[None, :] < n_groups[:, None]
    rows = (nt * TOK_TILE * TOP_K + N_EXP * (FFN_TILE - 1) + FFN_TILE - 1) // FFN_TILE * FFN_TILE
    spare = rows + (jnp.arange(nt) % 2)[:, None] * M_LOC + g[None, :] * ra
    combine_rows = jnp.where(valid, row, 0)
    dispatch_rows = jnp.where(valid & (k < pick(n_write)), row, spare)

    nblk_e = gpad // FFN_TILE
    blk_end = jnp.cumsum(nblk_e)
    n_used = jnp.maximum(blk_end[-1], 1)
    blk = jnp.minimum(jnp.arange(rows // FFN_TILE, dtype=jnp.int32), n_used - 1)
    blk_exp = jnp.minimum(jnp.sum(blk[:, None] >= blk_end[None, :], axis=1), N_EXP - 1)
    experts = jnp.arange(N_EXP)
    following = lax.cummin(jnp.where(nblk_e > 0, experts, N_EXP), reverse=True)
    next_of = jnp.concatenate([following[1:], jnp.full((1,), N_EXP, following.dtype)])
    next_of = jnp.where(next_of >= N_EXP, -1, next_of)
    parity = (jnp.cumsum(nblk_e > 0) - 1) % 2
    pick_e = lambda a: jnp.sum(jnp.where(blk_exp[:, None] == experts[None, :], a[None, :], 0), axis=1)
    blk_rows = pick_e(total) - (blk - pick_e(blk_end - nblk_e)) * FFN_TILE
    blk_half = blk_rows <= FFN_TILE // 2
    total16 = (total + ra - 1) // ra * ra
    tail = jnp.concatenate([gstart + total16, (gpad - total16) // ra])
    i32 = lambda a: a.astype(jnp.int32).reshape(-1)
    return dict(n_groups=i32(n_groups), dispatch_rows=i32(dispatch_rows), combine_rows=i32(combine_rows),
                slot_start=i32(lo), carry_start=i32(carry_start), carry_rows=i32(carry_rows), tail=i32(tail),
                blk_exp=i32(blk_exp), n_used=i32(n_used), next_exp=i32(pick_e(next_of)), blk_slot=i32(pick_e(parity)),
                blk_half=i32(blk_half),
                slot_off=(lo + pending).astype(F32), rows=rows)


def _split_w_in(w_in):
    g0 = C_GATES
    return (w_in[:, :g0].astype(BF16), w_in[:, g0 + N_GATE:].astype(BF16),
            _pad_lanes(w_in[:, g0:g0 + N_GATE]).astype(BF16))


def _pad_lanes(a, value=0.0):
    return jnp.pad(a, [(0, 0)] * (a.ndim - 1) + [(0, LANES - a.shape[-1])], constant_values=value)


def kernel(x_prompt, x_sample, c_prompt, c_sample, cache_k, cache_v, state_conv, state_C, state_n, state_m, page_table, w_ada, b_ada, g_norm1, g_norm2, w_in, b_gates, lambda_q1, lambda_k1, lambda_q2, lambda_k2, g_subln, w_conv, b_conv, g_mnorm, w_up_a, w_up_b, w_out, w_router, b_router, w_gu, b_gu, w_down, b_down, g_final):
    B, S, D = x_prompt.shape
    Bd, Td, _ = x_sample.shape
    depth = w_in.shape[0]
    n_pool = cache_k.shape[1]
    past_len = page_table.shape[1] * PAGE_SIZE
    n_p, n_s = B * S, Bd * Td
    n_all = n_p + n_s
    assert D == D_MODEL and n_s == TOK_TILE and S % MLSTM_CHUNK == 0 and n_p % TOK_TILE == 0
    assert page_table.shape[1] % PAGES_PER_STEP == 0

    cos_p, sin_p = _rope_tables(np.arange(S))
    cos_s, sin_s = _rope_tables(np.tile(past_len + np.arange(Td), Bd))
    hp = x_prompt.reshape(n_p, D)
    hs = x_sample.reshape(n_s, D)
    c_all = jnp.concatenate([c_prompt, c_sample], axis=0)
    outs = [[] for _ in range(12)]

    for l in range(depth):
        lam_init = 0.8 - 0.6 * math.exp(-0.3 * l)
        mod = _ada(c_all, w_ada[l], b_ada[l])
        mods = [mod[:, j * D:(j + 1) * D] for j in range(6)]
        mp = [m[:B].reshape(B, 1, D) for m in mods]
        ms_ = [jnp.repeat(m[B:], Td, axis=0).reshape(1, n_s, D) for m in mods]
        w_parts = _split_w_in(w_in[l])
        g1 = g_norm1[l].reshape(1, D)
        lam_vec = jnp.stack([lambda_q1[l], lambda_k1[l], lambda_q2[l], lambda_k2[l]])
        gsub = g_subln[l].reshape(1, DV_A)
        bg = _pad_lanes(b_gates[l].reshape(1, N_GATE))
        cw, cb = w_conv[l], b_conv[l].reshape(1, 2 * W_M)

        (q_p, _, kb_p, v_p, _, qk_p, vm_p, om_p, ga_p, gb_p, gt_p, vt_p, kt_p) = _inproj(
            hp, mp[1], mp[0], g1, cos_p, sin_p, w_parts, 1, PROJ_TILE)
        (q_s, k_s, kb_s, v_s, vb_s, qk_s, vm_s, om_s, ga_s, gb_s, gt_s, _, _) = _inproj(
            hs, ms_[1], ms_[0], g1, cos_s, sin_s, w_parts, n_s, n_s)
        k_p = jnp.transpose(kt_p.reshape(B, H_A, 2, DK_A, S), (0, 4, 1, 2, 3))

        kt_pool = jnp.transpose(cache_k[l], (0, 2, 3, 4, 1)).reshape(n_pool, W_QA, PAGE_SIZE)
        v_pool = cache_v[l].reshape(n_pool, PAGE_SIZE * H_A, DV_A)
        oa_p, oa_s = _attention(q_p, kb_p, vt_p, q_s, kb_s, vb_s, kt_pool, v_pool, page_table, lam_vec, gsub,
                                B, S, lam_init, Td)

        zeros = lambda *shape: jnp.zeros(shape, F32)
        hm_p, cst_p, C_p, nn_p, m_p = _mlstm(qk_p, vm_p, om_p, gt_p, cw, cb, bg, g_mnorm[l],
                                             zeros(B, CONV_W - 1, 2 * W_M), zeros(B, H_M, DH_M, DH_M),
                                             zeros(B, H_M, DH_M), zeros(B, 1, LANES), B, S)
        hm_s, cst_s, C_s, nn_s, m_s = _mlstm(qk_s, vm_s, om_s, gt_s, cw, cb, bg, g_mnorm[l],
                                             state_conv[l], state_C[l], state_n[l],
                                             _pad_lanes(state_m[l]).reshape(Bd, 1, LANES), Bd, Td)

        wa, wb, wo = w_up_a[l].astype(BF16), w_up_b[l].astype(BF16), w_out[l].astype(BF16)
        wr = _pad_lanes(w_router[l]).astype(BF16)
        br = _pad_lanes(b_router[l].reshape(1, N_EXP))
        g2 = g_norm2[l].reshape(1, D)
        part = _merge(oa_p, hm_p, ga_p, gb_p, hp, mp[2], mp[4], mp[3], g2, wa, wb, wo, wr, br, 1, 0, n_all)
        x2, h2, route, cnt = _merge(oa_s, hm_s, ga_s, gb_s, hs, ms_[2], ms_[4], ms_[3], g2, wa, wb, wo, wr, br,
                                    n_s, n_p // TOK_TILE, n_all, prev=part)

        meta = _moe_offsets(jnp.round(cnt[:, 0, ROUTE_SEL:ROUTE_SEL + N_EXP]).astype(jnp.int32))
        soff = _pad_lanes(meta["slot_off"])
        xs = _dispatch(meta, h2, route, soff[:, :, None])
        ys = _ffn(meta, xs, w_gu[l], b_gu[l], w_down[l], b_down[l])
        final = l == depth - 1
        hp, hs = _combine(meta, route, soff[:, None, :], x2, mp[5], ms_[5][0], g_final.reshape(1, D), ys,
                          n_p, n_s, final)

        for j, a in enumerate([k_p, v_p.reshape(B, S, H_A, DV_A), cst_p, C_p, nn_p,
                               m_p[:, 0, :H_M],
                               k_s.reshape(Bd, Td, H_A, 2, DK_A), v_s.reshape(Bd, Td, H_A, DV_A), cst_s, C_s, nn_s,
                               m_s[:, 0, :H_M]]):
            outs[j].append(a)

    return (hp.reshape(B, S, D), hs.reshape(Bd, Td, D)) + tuple(jnp.stack(o) for o in outs)
```

```python
import functools
import math

import numpy as np
import jax
import jax.numpy as jnp
from jax import lax
from jax.experimental import pallas as pl
from jax.experimental.pallas import tpu as pltpu

F32 = jnp.float32
BF16 = jnp.bfloat16

D_MODEL = 1024
H_A = 4
DK_A = 64
DV_A = 2 * DK_A
ROPE_THETA = 10000.0
H_M = 4
DH_M = 128
CONV_W = 4
N_EXP = 32
TOP_K = 4
D_FF = D_MODEL
SWIGLU_LIMIT = 7.0
SWIGLU_ALPHA = 1.702
EPS = 1e-6
PAGE_SIZE = 128

W_QA = H_A * 2 * DK_A
W_VA = H_A * DV_A
W_M = H_M * DH_M
N_GATE = 2 * H_M

LANES = 128
ROW_ALIGN = 16
TOK_TILE = 256
PROJ_TILE = 512
FFN_TILE = 512
ATT_TILE = 256
ATT_HEADS = H_A
MLSTM_CHUNK = 256
PAGES_PER_STEP = 16
CHUNKS_PER_STEP = 2
NEG_BIG = -1e30
LOG2_E = math.log2(math.e)
VMEM_LIMIT = 56 * 1024 * 1024

C_QA, C_KA, C_VA, C_QK, C_VM, C_OM, C_GATES = 0, 512, 1024, 1536, 2560, 3072, 3584
M_LOC = ((TOK_TILE * TOP_K + 2 * N_EXP * (ROW_ALIGN - 1)) + 255) // 256 * 256
GROUPS = M_LOC // ROW_ALIGN
M_LOC_SHORT = TOK_TILE * TOP_K + N_EXP * ROW_ALIGN
ROUTE_SEL = 32
GROUP_UNROLL = 4


def _dot(a, b):
    return jnp.dot(a, b, preferred_element_type=F32)


def _dot_nt(a, b):
    return lax.dot_general(a, b, (((1,), (1,)), ((), ())), preferred_element_type=F32)


def _sigmoid(x):
    return 0.5 * jnp.tanh(0.5 * x) + 0.5


def _start_all(copies):
    for cp in copies:
        cp.start()


def _params(sem):
    return pltpu.CompilerParams(dimension_semantics=sem, vmem_limit_bytes=VMEM_LIMIT)


def _ada_body(c_ref, w_ref, b_ref, o_ref):
    c = c_ref[...]
    s = c * _sigmoid(c)
    s_hi = s.astype(BF16)
    s_lo = (s - s_hi.astype(F32)).astype(BF16)
    w = w_ref[...]
    w_hi = w.astype(BF16)
    w_lo = (w - w_hi.astype(F32)).astype(BF16)
    o_ref[...] = _dot(s_hi, w_hi) + _dot(s_lo, w_hi) + _dot(s_hi, w_lo) + b_ref[...]


def _ada(c_all, w_ada, b_ada):
    rows = c_all.shape[0]
    n_out = w_ada.shape[1]
    blk = D_MODEL
    return pl.pallas_call(
        _ada_body,
        grid=(n_out // blk,),
        in_specs=[pl.BlockSpec((rows, D_MODEL), lambda j: (0, 0)),
                  pl.BlockSpec((D_MODEL, blk), lambda j: (0, j)),
                  pl.BlockSpec((1, blk), lambda j: (0, j))],
        out_specs=pl.BlockSpec((rows, blk), lambda j: (0, j)),
        out_shape=jax.ShapeDtypeStruct((rows, n_out), F32),
        compiler_params=_params(("arbitrary",)),
        name="ada",
    )(c_all, w_ada, b_ada.reshape(1, n_out))


def _rope(z, cos, sin):
    lane = lax.broadcasted_iota(jnp.int32, (z.shape[0], LANES), 1)
    first_half = (lane % DK_A) < (DK_A // 2)
    out = []
    for h in range(H_A):
        xh = z[:, h * LANES:(h + 1) * LANES]
        partner = jnp.where(first_half, pltpu.roll(xh, LANES - DK_A // 2, 1), pltpu.roll(xh, DK_A // 2, 1))
        out.append(xh * cos + partner * sin)
    return jnp.concatenate(out, axis=1)


def _conv_silu(u, ext, cw_ref, cb_ref):
    rows = u.shape[0]
    full = jnp.concatenate([ext, u], axis=0)
    conv = cb_ref[...] + cw_ref[CONV_W - 1:CONV_W, :] * u
    for j in range(CONV_W - 1):
        conv = conv + cw_ref[j:j + 1, :] * pltpu.roll(full, CONV_W - 1 - j, 0)[8:8 + rows]
    return conv * _sigmoid(conv)


def _inproj_body(x_ref, sc_ref, sh_ref, g_ref, cos_ref, sin_ref, w_ref, wg_ref, wt_ref,
                 q_ref, k_ref, kb_ref, v_ref, vb_ref, qk_ref, vm_ref, om_ref, ga_ref, gb_ref, gt_ref, vt_ref, kt_ref):
    x = x_ref[...]
    ms = jnp.mean(x * x, axis=1, keepdims=True)
    h = (x * lax.rsqrt(ms + EPS)) * g_ref[...] * (1.0 + sc_ref[...]) + sh_ref[...]
    hb = h.astype(BF16)
    cos = cos_ref[...]
    sin = sin_ref[...]

    def seg(lo, n):
        return _dot(hb, w_ref[:, lo:lo + n])

    q = _rope(seg(C_QA, W_QA), cos, sin) * (DK_A ** -0.5 * LOG2_E)
    q_ref[...] = q.astype(BF16)
    k = _rope(seg(C_KA, W_QA), cos, sin)
    k_ref[...] = k
    kb_ref[...] = k.astype(BF16)
    kt_ref[...] = k.T
    v = seg(C_VA, W_VA)
    for h in range(H_A):
        v_ref[pl.ds(h, v.shape[0], stride=H_A), :] = v[:, h * DV_A:(h + 1) * DV_A]
    vb_ref[...] = v.astype(BF16)
    for s in range(vt_ref.shape[0]):
        vt_ref[s] = v[s * ATT_TILE:(s + 1) * ATT_TILE].T.astype(BF16)
    qk_ref[...] = seg(C_QK, 2 * W_M)
    vm_ref[...] = seg(C_VM, W_M).astype(BF16)
    om_ref[...] = seg(C_OM, W_M)
    ga_ref[...] = _dot(hb, wg_ref[:, :D_MODEL])
    gb_ref[...] = _dot(hb, wg_ref[:, D_MODEL:])
    gt_ref[...] = _dot(hb, wt_ref[...])


def _inproj(x, sc, sh, g1, cos, sin, w_parts, rows_per_mod, tile):
    n = x.shape[0]
    nt = n // tile
    tiles_per_group = nt // sc.shape[0]
    tab_tiles = cos.shape[0] // tile
    slabs = tile // ATT_TILE
    tok = lambda w: pl.BlockSpec((tile, w), lambda i: (i, 0))
    mod = pl.BlockSpec((None, rows_per_mod, D_MODEL), lambda i: (i // tiles_per_group, 0, 0))
    tab = pl.BlockSpec((tile, LANES), lambda i: (i % tab_tiles, 0))
    tok_out = lambda w, dt: (tok(w), jax.ShapeDtypeStruct((n, w), dt))
    outs = [
        tok_out(W_QA, BF16),
        tok_out(W_QA, F32), tok_out(W_QA, BF16),
        (pl.BlockSpec((tile * H_A, DV_A), lambda i: (i, 0)), jax.ShapeDtypeStruct((n * H_A, DV_A), F32)),
        tok_out(W_VA, BF16),
        tok_out(2 * W_M, F32), tok_out(W_M, BF16), tok_out(W_M, F32),
        tok_out(D_MODEL, F32), tok_out(D_MODEL, F32), tok_out(LANES, F32),
        (pl.BlockSpec((slabs, W_VA, ATT_TILE), lambda i: (i, 0, 0)),
         jax.ShapeDtypeStruct((n // ATT_TILE, W_VA, ATT_TILE), BF16)),
        (pl.BlockSpec((None, W_QA, tile), lambda i: (i // tab_tiles, 0, i % tab_tiles)),
         jax.ShapeDtypeStruct((nt // tab_tiles, W_QA, tab_tiles * tile), F32)),
    ]
    return pl.pallas_call(
        _inproj_body,
        grid=(nt,),
        in_specs=[tok(D_MODEL), mod, mod, pl.BlockSpec((1, D_MODEL), lambda i: (0, 0)), tab, tab,
                  *[pl.BlockSpec(w.shape, lambda i: (0, 0), pipeline_mode=pl.Buffered(1)) for w in w_parts]],
        out_specs=[spec for spec, _ in outs],
        out_shape=[shape for _, shape in outs],
        compiler_params=_params(("arbitrary",)),
        name="inproj",
    )(x, sc, sh, g1, cos, sin, *w_parts)


def _rope_tables(pos):
    half = DK_A // 2
    inv = ROPE_THETA ** (-np.arange(half, dtype=np.float64) * 2.0 / DK_A)
    ang = np.asarray(pos, np.float64)[:, None] * inv[None, :]
    cos = np.cos(ang)
    sin = np.sin(ang)
    cos64 = np.concatenate([cos, cos], axis=1)
    sin64 = np.concatenate([-sin, sin], axis=1)
    return (jnp.asarray(np.tile(cos64, (1, LANES // DK_A)), F32),
            jnp.asarray(np.tile(sin64, (1, LANES // DK_A)), F32))


def _lambda_value(lam_ref, lam_init):
    lv = lam_ref[...]
    l1 = jnp.sum(lv[0:1, :] * lv[1:2, :], axis=1, keepdims=True)
    l2 = jnp.sum(lv[2:3, :] * lv[3:4, :], axis=1, keepdims=True)
    return jnp.exp(l1) - jnp.exp(l2) + lam_init


def _subln(o, g, lam_init):
    ms = jnp.mean(o * o, axis=1, keepdims=True)
    return (o * lax.rsqrt(ms + EPS)) * g * (1.0 - lam_init)


def _prompt_tile(q_ref, k_ref, vt_ref, lam, g_ref, o_ref, m_scr, acc_scr, i, lam_init):
    tq = ATT_TILE
    lane = lax.broadcasted_iota(jnp.int32, (tq, LANES), 1)
    qs = []
    for hh in range(ATT_HEADS):
        q = q_ref[:, hh * LANES:(hh + 1) * LANES]
        zero = jnp.zeros_like(q)
        qs.append(jnp.concatenate([jnp.where(lane < DK_A, q, zero), jnp.where(lane >= DK_A, q, zero)], axis=0))
    for hh in range(ATT_HEADS):
        m_scr[hh][...] = jnp.full(m_scr[hh].shape, NEG_BIG, F32)
        acc_scr[hh][...] = jnp.zeros(acc_scr[hh].shape, F32)
    ones = jnp.ones((ROW_ALIGN, tq), BF16)

    def scores(hh, j):
        start = pl.multiple_of(j * tq, tq)
        return _dot_nt(k_ref[pl.ds(start, tq), hh * LANES:(hh + 1) * LANES], qs[hh])

    def update_all(j, mask):
        sts = [scores(hh, j) for hh in range(ATT_HEADS)]
        if mask is not None:
            sts = [jnp.where(mask, st, NEG_BIG) for st in sts]
        pts, alphas = [], []
        for hh in range(ATT_HEADS):
            m_old = m_scr[hh][...]
            m_new = jnp.maximum(m_old, jnp.max(sts[hh], axis=0, keepdims=True))
            alphas.append(jnp.exp2(m_old - m_new))
            pts.append(jnp.exp2(sts[hh] - m_new).astype(BF16))
            m_scr[hh][...] = m_new
        for hh in range(ATT_HEADS):
            vt = jnp.concatenate([vt_ref[j, hh * LANES:(hh + 1) * LANES, :], ones], axis=0)
            acc_scr[hh][...] = alphas[hh] * acc_scr[hh][...] + _dot(vt, pts[hh])

    def off_diag(j, carry):
        update_all(j, None)
        return carry

    lax.fori_loop(0, i, off_diag, 0)
    key = lax.broadcasted_iota(jnp.int32, (tq, 2 * tq), 0)
    qry = lax.broadcasted_iota(jnp.int32, (tq, 2 * tq), 1) % tq
    update_all(i, key <= qry)
    for hh in range(ATT_HEADS):
        acc = acc_scr[hh][...]
        ot = acc[:DV_A] / acc[DV_A:DV_A + 1]
        at = ot[:, :tq] - lam * ot[:, tq:]
        ms = jnp.mean(at * at, axis=0, keepdims=True)
        at = (at * lax.rsqrt(ms + EPS)) * g_ref[...] * (1.0 - lam_init)
        o_ref[:, hh * LANES:(hh + 1) * LANES] = at.T.astype(BF16)


def _sample_chunks(pt_ref, q_ref, kn_ref, vn_ref, lam, g_ref, kpool_ref, vpool_ref, o_ref,
                   kbuf, vbuf, sem, m_scr, l_scr, acc_scr, step, parts, *, lam_init, n_chunks, n_seq, t_new):
    steps_per_seq = n_chunks // CHUNKS_PER_STEP
    b = step // steps_per_seq
    c0 = (step % steps_per_seq) * CHUNKS_PER_STEP
    opens = 0 in parts
    closes = CHUNKS_PER_STEP - 1 in parts

    def chunk_copies(bb, c, slot):
        out = []
        for j in range(PAGES_PER_STEP):
            page = pt_ref[bb, c * PAGES_PER_STEP + j]
            out.append(pltpu.make_async_copy(kpool_ref.at[page], kbuf.at[slot, j], sem.at[slot]))
            out.append(pltpu.make_async_copy(vpool_ref.at[page], vbuf.at[slot, j], sem.at[slot]))
        return out

    if opens:
        @pl.when(step == 0)
        def _():
            _start_all(chunk_copies(0, 0, 0) + chunk_copies(0, 1, 1))

    q = q_ref[...].astype(F32)
    qt = jnp.concatenate([q] * (2 * H_A), axis=0)
    row = lax.broadcasted_iota(jnp.int32, qt.shape, 0)
    col = lax.broadcasted_iota(jnp.int32, qt.shape, 1)
    qbd = jnp.where(col // DK_A == row // t_new, qt, 0.0).astype(BF16)

    if opens:
        @pl.when(c0 == 0)
        def _():
            m_scr[...] = jnp.full(m_scr.shape, NEG_BIG, F32)
            l_scr[...] = jnp.zeros(l_scr.shape, F32)
            acc_scr[...] = jnp.zeros(acc_scr.shape, F32)

    rows_h = 2 * t_new

    def update(s, v):
        m_old = m_scr[...]
        m_new = jnp.maximum(m_old, jnp.max(s, axis=1, keepdims=True))
        alpha = jnp.exp2(m_old - m_new)
        p = jnp.exp2(s - m_new)
        l_scr[...] = alpha * l_scr[...] + jnp.sum(p, axis=1, keepdims=True)
        pv = _dot(p.astype(BF16), v)
        own = [pv[h * rows_h:(h + 1) * rows_h, h * DV_A:(h + 1) * DV_A] for h in range(H_A)]
        acc_scr[...] = alpha * acc_scr[...] + jnp.concatenate(own, axis=0)
        m_scr[...] = m_new

    def chunk(c, slot):
        for cp in chunk_copies(b, c, slot):
            cp.wait()
        kt = jnp.concatenate([kbuf[slot, j].astype(BF16) for j in range(PAGES_PER_STEP)], axis=1)

        def page_v(j):
            return jnp.concatenate([vbuf.at[slot, j][pl.ds(h, PAGE_SIZE, stride=H_A), :].astype(BF16)
                                    for h in range(H_A)], axis=1)

        v = jnp.concatenate([page_v(j) for j in range(PAGES_PER_STEP)], axis=0)
        update(_dot(qbd, kt), v)

        @pl.when(c + 2 < n_chunks)
        def _():
            _start_all(chunk_copies(b, c + 2, slot))

        @pl.when(jnp.logical_and(c + 2 >= n_chunks, b + 1 < n_seq))
        def _():
            _start_all(chunk_copies(b + 1, c + 2 - n_chunks, slot))

    for j in parts:
        chunk(c0 + j, j % 2)

    def finish():
        zpad = jnp.zeros((PAGE_SIZE - t_new, W_QA), F32)
        kn = jnp.concatenate([kn_ref[...].astype(F32), zpad], axis=0).astype(BF16)
        vn = jnp.concatenate([vn_ref[...].astype(F32), zpad], axis=0).astype(BF16)
        s = _dot_nt(qbd, kn)
        row = lax.broadcasted_iota(jnp.int32, s.shape, 0) % t_new
        col = lax.broadcasted_iota(jnp.int32, s.shape, 1)
        update(jnp.where(col <= row, s, NEG_BIG), vn)
        o = acc_scr[...] / l_scr[...]
        outs = []
        for h in range(H_A):
            r0 = h * rows_h
            outs.append(_subln(o[r0:r0 + t_new] - lam * o[r0 + t_new:r0 + rows_h], g_ref[...], lam_init))
        o_ref[...] = jnp.concatenate(outs, axis=1).astype(BF16)

    if closes:
        pl.when(c0 + CHUNKS_PER_STEP == n_chunks)(finish)


def _attention_body(pt_ref, q_ref, k_ref, vt_ref, lam_ref, gcol_ref, qs_ref, kn_ref, vn_ref, grow_ref,
                    kpool_ref, vpool_ref, o_ref, os_ref, *scratch, lam_init, n_chunks, n_seq, t_new):
    m_scr, acc_scr = scratch[:ATT_HEADS], scratch[ATT_HEADS:2 * ATT_HEADS]
    kbuf, vbuf, sem, ms_scr, ls_scr, accs_scr = scratch[2 * ATT_HEADS:]
    step = pl.program_id(0) * pl.num_programs(1) + pl.program_id(1)
    lam = _lambda_value(lam_ref, lam_init)
    sample = functools.partial(_sample_chunks, pt_ref, qs_ref, kn_ref, vn_ref, lam, grow_ref, kpool_ref, vpool_ref,
                               os_ref, kbuf, vbuf, sem, ms_scr, ls_scr, accs_scr, step,
                               lam_init=lam_init, n_chunks=n_chunks, n_seq=n_seq, t_new=t_new)
    sample((0,))
    _prompt_tile(q_ref, k_ref, vt_ref, lam, gcol_ref, o_ref, m_scr, acc_scr, pl.program_id(1), lam_init)
    sample((1,))


def _attention(q, k, vt, q_s, k_new, v_new, cache_k, cache_v, page_table, lam_vec, g_subln, batch, seq,
               lam_init, t_new):
    nq = seq // ATT_TILE
    bd, n_pages = page_table.shape
    n_chunks = n_pages // PAGES_PER_STEP
    steps_per_seq = n_chunks // CHUNKS_PER_STEP
    assert ATT_HEADS == H_A and CHUNKS_PER_STEP % 2 == 0 and n_chunks % CHUNKS_PER_STEP == 0
    assert batch * nq == bd * steps_per_seq
    n_rows = 2 * H_A * t_new
    kv = pl.BlockSpec((seq, W_QA), lambda b, i, pt: (b, 0))
    vts = pl.BlockSpec((nq, W_VA, ATT_TILE), lambda b, i, pt: (b, 0, 0))
    qo = pl.BlockSpec((ATT_TILE, W_QA), lambda b, i, pt: (b * nq + i, 0))
    new = pl.BlockSpec((None, t_new, W_QA), lambda b, i, pt: ((b * nq + i) // steps_per_seq, 0, 0))
    const = lambda shape: pl.BlockSpec(shape, lambda b, i, pt: (0, 0))
    page_buf = pltpu.VMEM((2, PAGES_PER_STEP, W_QA, PAGE_SIZE), F32)
    grid_spec = pltpu.PrefetchScalarGridSpec(
        num_scalar_prefetch=1,
        grid=(batch, nq),
        in_specs=[qo, kv, vts, const((4, DK_A)), const((DV_A, 1)), new, new, new, const((1, DV_A)),
                  pl.BlockSpec(memory_space=pl.ANY), pl.BlockSpec(memory_space=pl.ANY)],
        out_specs=[qo, new],
        scratch_shapes=[pltpu.VMEM((1, 2 * ATT_TILE), F32)] * ATT_HEADS
                       + [pltpu.VMEM((DV_A + ROW_ALIGN, 2 * ATT_TILE), F32)] * ATT_HEADS
                       + [page_buf, page_buf, pltpu.SemaphoreType.DMA((2,)), pltpu.VMEM((n_rows, 1), F32),
                          pltpu.VMEM((n_rows, 1), F32), pltpu.VMEM((n_rows, DV_A), F32)],
    )
    o_p, o_s = pl.pallas_call(
        functools.partial(_attention_body, lam_init=lam_init, n_chunks=n_chunks, n_seq=bd, t_new=t_new),
        grid_spec=grid_spec,
        out_shape=[jax.ShapeDtypeStruct((batch * seq, W_VA), BF16), jax.ShapeDtypeStruct((bd, t_new, W_VA), BF16)],
        compiler_params=_params(("arbitrary", "arbitrary")),
        name="attention",
    )(page_table, q, k, vt, lam_vec, g_subln.reshape(DV_A, 1), q_s.reshape(bd, t_new, W_QA),
      k_new.reshape(bd, t_new, W_QA), v_new.reshape(bd, t_new, W_VA), g_subln, cache_k, cache_v)
    return o_p, o_s.reshape(bd * t_new, W_VA)


def _mlstm_body(qk_ref, vm_ref, om_ref, gt_ref, cw_ref, cb_ref, bg_ref, gm_ref, cbuf_ref, c0_ref, n0_ref, m0_ref,
                h_ref, cst_ref, cout_ref, nout_ref, mout_ref, ext_scr, c_scr, n_scr, m_scr, *, tb, L, nc):
    c_idx = pl.program_id(1)

    @pl.when(c_idx == 0)
    def _():
        ext_scr[...] = jnp.zeros(ext_scr.shape, F32)
        ext_scr[8 - (CONV_W - 1):8, :] = cbuf_ref[...]
        c_scr[...] = c0_ref[...]
        n_scr[...] = n0_ref[...]
        m_scr[...] = m0_ref[...]

    pad = L - tb
    u = qk_ref[...]
    if pad:
        u = jnp.concatenate([u, jnp.zeros((pad, u.shape[1]), F32)], axis=0)
    a = _conv_silu(u, ext_scr[...], cw_ref, cb_ref)
    if not pad:
        ext_scr[...] = u[L - 8:L]

    @pl.when(c_idx == nc - 1)
    def _():
        cst_ref[...] = qk_ref[tb - (CONV_W - 1):tb, :]

    g = gt_ref[...] + bg_ref[...]
    li = g
    lf = jnp.minimum(g, 0.0) - jnp.log1p(jnp.exp(-jnp.abs(g)))
    if pad:
        zpad = jnp.zeros((pad, LANES), F32)
        li = jnp.concatenate([li, zpad + NEG_BIG], axis=0)
        lf = jnp.concatenate([lf, zpad], axis=0)
    row = lax.broadcasted_iota(jnp.int32, (L, LANES), 0)
    lane = lax.broadcasted_iota(jnp.int32, (L, LANES), 1)
    bcum = lf
    shift = 1
    while shift < L:
        bcum = bcum + jnp.where(row >= shift, pltpu.roll(bcum, shift, 0), 0.0)
        shift *= 2
    gates = jnp.where(lane < H_M, li, bcum)
    gates_t = gates.T
    tri = lax.broadcasted_iota(jnp.int32, (L, L), 0) >= lax.broadcasted_iota(jnp.int32, (L, L), 1)
    m_all = m_scr[...]
    lane1 = lax.broadcasted_iota(jnp.int32, (1, LANES), 1)
    m_next = m_all
    vall = vm_ref[...]
    if pad:
        vall = jnp.concatenate([vall, jnp.zeros((pad, vall.shape[1]), BF16)], axis=0)

    for h in range(H_M):
        li_col = gates[:, h:h + 1]
        b_col = gates[:, H_M + h:H_M + h + 1]
        src_row = gates_t[h:h + 1, :] - gates_t[H_M + h:H_M + h + 1, :]
        m_prev = m_all[:, h:h + 1]
        b_last = b_col[L - 1:L, :]
        log_d = jnp.where(tri, b_col + src_row, NEG_BIG)
        inter = b_col + m_prev
        mt = jnp.maximum(inter, jnp.max(log_d, axis=1, keepdims=True))
        q = a[:, h * DH_M:(h + 1) * DH_M]
        k = a[:, W_M + h * DH_M:W_M + (h + 1) * DH_M] * (DH_M ** -0.5)
        v = vall[:, h * DH_M:(h + 1) * DH_M]
        qb = q.astype(BF16)
        s = _dot_nt(qb, k.astype(BF16)) * jnp.exp(log_d - mt)
        ei = jnp.exp(inter - mt)
        c_old = c_scr[h]
        n_old = n_scr[h:h + 1, :]
        num = ei * _dot(qb, c_old.astype(BF16)) + _dot(s.astype(BF16), v)
        den = ei * jnp.sum(q * n_old, axis=1, keepdims=True) + jnp.sum(s, axis=1, keepdims=True)
        hh = num / jnp.maximum(jnp.abs(den), jnp.exp(-mt))
        g_col = b_last - b_col + li_col
        bl = b_last + m_prev
        m_new = jnp.maximum(bl, jnp.max(g_col, axis=0, keepdims=True))
        wg = jnp.exp(g_col - m_new)
        decay = jnp.exp(bl - m_new)
        kw = k * wg
        c_scr[h] = decay * c_old + _dot(kw.T.astype(BF16), v)
        n_scr[h:h + 1, :] = decay * n_old + jnp.sum(kw, axis=0, keepdims=True)
        m_next = jnp.where(lane1 == h, m_new, m_next)
        ms = jnp.mean(hh * hh, axis=1, keepdims=True)
        hn = (hh * lax.rsqrt(ms + EPS)) * gm_ref[h:h + 1, :]
        og = _sigmoid(om_ref[:, h * DH_M:(h + 1) * DH_M])
        h_ref[:, h * DH_M:(h + 1) * DH_M] = (hn[:tb] * og).astype(BF16)

    m_scr[...] = m_next

    @pl.when(c_idx == nc - 1)
    def _():
        cout_ref[...] = c_scr[...]
        nout_ref[...] = n_scr[...]
        mout_ref[...] = m_scr[...]


def _mlstm(qk, vm, om, gt, w_conv, b_conv, b_gates_pad, g_mnorm, conv_buf, c0, n0, m0_pad, batch, seq):
    tb = min(seq, MLSTM_CHUNK)
    L = max(tb, LANES)
    nc = seq // tb
    tok = lambda w: pl.BlockSpec((None, tb, w), lambda b, c: (b * nc + c, 0, 0))
    chunks = lambda a: a.reshape(batch * nc, tb, a.shape[-1])
    const = lambda shape: pl.BlockSpec(shape, lambda b, c: (0,) * len(shape))
    per_b = lambda shape: pl.BlockSpec((None,) + shape, lambda b, c: (b,) + (0,) * len(shape))
    h, cst, c_out, n_out, m_out = pl.pallas_call(
        functools.partial(_mlstm_body, tb=tb, L=L, nc=nc),
        grid=(batch, nc),
        in_specs=[tok(2 * W_M), tok(W_M), tok(W_M), tok(LANES), const((CONV_W, 2 * W_M)), const((1, 2 * W_M)),
                  const((1, LANES)), const((H_M, DH_M)), per_b((CONV_W - 1, 2 * W_M)),
                  per_b((H_M, DH_M, DH_M)), per_b((H_M, DH_M)), per_b((1, LANES))],
        out_specs=[tok(W_M), per_b((CONV_W - 1, 2 * W_M)), per_b((H_M, DH_M, DH_M)), per_b((H_M, DH_M)),
                   per_b((1, LANES))],
        out_shape=[jax.ShapeDtypeStruct((batch * nc, tb, W_M), BF16),
                   jax.ShapeDtypeStruct((batch, CONV_W - 1, 2 * W_M), F32),
                   jax.ShapeDtypeStruct((batch, H_M, DH_M, DH_M), F32),
                   jax.ShapeDtypeStruct((batch, H_M, DH_M), F32),
                   jax.ShapeDtypeStruct((batch, 1, LANES), F32)],
        scratch_shapes=[pltpu.VMEM((8, 2 * W_M), F32), pltpu.VMEM((H_M, DH_M, DH_M), F32),
                        pltpu.VMEM((H_M, DH_M), F32), pltpu.VMEM((1, LANES), F32)],
        compiler_params=_params(("arbitrary", "arbitrary")),
        name="mlstm",
    )(chunks(qk), chunks(vm), chunks(om), chunks(gt), w_conv, b_conv, b_gates_pad, g_mnorm, conv_buf, c0, n0,
      m0_pad)
    return h.reshape(batch * seq, W_M), cst, c_out, n_out, m_out


def _merge_body(oa_ref, hm_ref, ga_ref, gb_ref, x_ref, gt1_ref, sc2_ref, sh2_ref, g2_ref, wa_ref, wb_ref, wo_ref,
                wr_ref, br_ref, *rest):
    x2_ref, h2_ref, route_ref, cnt_ref = rest[-4:]
    ya = _dot(oa_ref[...], wa_ref[...])
    yb = _dot(hm_ref[...], wb_ref[...])
    mix = _sigmoid(ga_ref[...]) * ya + _sigmoid(gb_ref[...]) * yb
    y = _dot(mix.astype(BF16), wo_ref[...])
    x2 = x_ref[...] + gt1_ref[...] * y
    x2_ref[...] = x2
    ms = jnp.mean(x2 * x2, axis=1, keepdims=True)
    h2 = (x2 * lax.rsqrt(ms + EPS)) * g2_ref[...] * (1.0 + sc2_ref[...]) + sh2_ref[...]
    h2b = h2.astype(BF16)
    h2_ref[...] = h2b

    tm = h2b.shape[0]
    logits_t = (_dot(h2b, wr_ref[...]) + br_ref[...]).T[:N_EXP, :]
    row = lax.broadcasted_iota(jnp.int32, (N_EXP, tm), 0)
    row_f = row.astype(F32)
    work = logits_t
    vals, hots = [], []
    for _ in range(TOP_K):
        mx = jnp.max(work, axis=0, keepdims=True)
        idx = jnp.min(jnp.where(work == mx, row, N_EXP), axis=0, keepdims=True)
        hot = row == idx
        vals.append(mx)
        hots.append(hot)
        work = jnp.where(hot, 2.0 * NEG_BIG, work)
    es = [jnp.exp(v - vals[0]) for v in vals]
    den = es[0]
    for e in es[1:]:
        den = den + e
    sel_t = jnp.zeros((N_EXP, tm), F32)
    for hot in hots:
        sel_t = jnp.where(hot, 1.0, sel_t)
    r_i = lax.broadcasted_iota(jnp.int32, (tm, tm), 0)
    c_i = lax.broadcasted_iota(jnp.int32, (tm, tm), 1)
    earlier = jnp.logical_and(r_i < c_i, r_i // TOK_TILE == c_i // TOK_TILE)
    rank_t = _dot(sel_t.astype(BF16), jnp.where(earlier, 1.0, 0.0).astype(BF16))
    sub = lax.broadcasted_iota(jnp.int32, (ROUTE_SEL, tm), 0)
    head = jnp.zeros((ROUTE_SEL, tm), F32)
    for k in range(TOP_K):
        e_k = jnp.sum(jnp.where(hots[k], row_f, 0.0), axis=0, keepdims=True)
        r_k = jnp.sum(jnp.where(hots[k], rank_t, 0.0), axis=0, keepdims=True)
        head = jnp.where(sub == k, e_k, head)
        head = jnp.where(sub == TOP_K + k, es[k] / den, head)
        head = jnp.where(sub == 2 * TOP_K + k, r_k, head)
    pad = jnp.zeros((LANES - ROUTE_SEL - N_EXP, tm), F32)
    route = jnp.concatenate([head, sel_t, pad], axis=0).T
    route_ref[...] = route
    for s in range(tm // TOK_TILE):
        cnt_ref[s] = jnp.sum(route[s * TOK_TILE:(s + 1) * TOK_TILE], axis=0, keepdims=True)


def _merge(oa, hm, ga, gb, x, gt1, sc2, sh2, g2, wa, wb, wo, wr, br, rows_per_mod, tile, row0, n_all, prev=None):
    n = x.shape[0]
    nt = n // tile
    nt_all = n_all // TOK_TILE
    tile0 = row0 // tile
    sub = tile // TOK_TILE
    tiles_per_group = nt // gt1.shape[0]
    tok = lambda w: pl.BlockSpec((tile, w), lambda i: (i, 0))
    mod = pl.BlockSpec((None, rows_per_mod, D_MODEL), lambda i: (i // tiles_per_group, 0, 0))
    res = lambda shape: pl.BlockSpec(shape, lambda i: (0, 0), pipeline_mode=pl.Buffered(1))
    out_tok = lambda w: pl.BlockSpec((tile, w), lambda i: (tile0 + i, 0))
    in_specs = [tok(W_VA), tok(W_M), tok(D_MODEL), tok(D_MODEL), tok(D_MODEL), mod, mod, mod,
                pl.BlockSpec((1, D_MODEL), lambda i: (0, 0)),
                res((W_VA, D_MODEL)), res((W_M, D_MODEL)), res((D_MODEL, D_MODEL)), res((D_MODEL, LANES)),
                pl.BlockSpec((1, LANES), lambda i: (0, 0))]
    args = [oa, hm, ga, gb, x, gt1, sc2, sh2, g2, wa, wb, wo, wr, br]
    aliases = {}
    if prev is not None:
        in_specs += [pl.BlockSpec(memory_space=pl.ANY)] * 4
        aliases = {len(args) + j: j for j in range(4)}
        args += list(prev)
    return pl.pallas_call(
        _merge_body,
        grid=(nt,),
        in_specs=in_specs,
        out_specs=[out_tok(D_MODEL), out_tok(D_MODEL), out_tok(LANES),
                   pl.BlockSpec((sub, 1, LANES), lambda i: (tile0 + i, 0, 0))],
        out_shape=[jax.ShapeDtypeStruct((n_all, D_MODEL), F32), jax.ShapeDtypeStruct((n_all, D_MODEL), BF16),
                   jax.ShapeDtypeStruct((n_all, LANES), F32), jax.ShapeDtypeStruct((nt_all, 1, LANES), F32)],
        input_output_aliases=aliases,
        compiler_params=_params(("arbitrary",)),
        name="merge",
    )(*args)


def _segment_copies(src, dst, sem, src_row, dst_row, n_groups, max_groups):
    out = []
    bit = 1
    while bit * 2 <= max_groups:
        bit *= 2
    while bit >= 1:
        off = (n_groups // (2 * bit)) * (2 * bit) * ROW_ALIGN
        rows = bit * ROW_ALIGN
        cp = pltpu.make_async_copy(src.at[pl.ds(pl.multiple_of(src_row + off, ROW_ALIGN), rows)],
                                   dst.at[pl.ds(pl.multiple_of(dst_row + off, ROW_ALIGN), rows)], sem)
        out.append(((n_groups // bit) % 2 == 1, cp))
        bit //= 2
    return out


def _run_copies(copies):
    for pred, cp in copies:
        pl.when(pred)(cp.start)
    for pred, cp in copies:
        pl.when(pred)(cp.wait)


def _slot_rows(route_t, loff_col, k):
    e_row = route_t[k:k + 1, :]
    r_row = route_t[2 * TOP_K + k:2 * TOP_K + k + 1, :]
    sub = lax.broadcasted_iota(jnp.int32, (LANES, route_t.shape[1]), 0).astype(F32)
    return jnp.sum(jnp.where(sub == e_row, loff_col, 0.0), axis=0, keepdims=True) + r_row


def _for_groups(n_groups, table_ref, base, make_copy, start):
    def body(i, carry):
        for j in range(GROUP_UNROLL):
            g = i * GROUP_UNROLL + j
            cp = make_copy(g, table_ref[base + g])
            if start:
                cp.start()
            else:
                cp.wait()
        return carry
    lax.fori_loop(0, (n_groups + GROUP_UNROLL - 1) // GROUP_UNROLL, body, 0)


def _dispatch_body(ng_ref, dt_ref, lo_ref, sv_ref, rm_ref, tail_ref, h2_ref, route_ref, soff_ref, xs_ref,
                   loc_scr, carry_scr, zero_scr, sem, *, nt):
    t = pl.program_id(0)
    buf = t % 2

    def copies_of(tt):
        def make(g, row):
            return pltpu.make_async_copy(
                loc_scr.at[tt % 2, pl.ds(pl.multiple_of(g * ROW_ALIGN, ROW_ALIGN), ROW_ALIGN)],
                xs_ref.at[pl.ds(pl.multiple_of(row, ROW_ALIGN), ROW_ALIGN)], sem.at[tt % 2])
        return ng_ref[tt], dt_ref, tt * GROUPS, make

    @pl.when(t == 0)
    def _():
        carry_scr[...] = jnp.zeros(carry_scr.shape, BF16)
        loc_scr[...] = jnp.zeros(loc_scr.shape, BF16)

    @pl.when(t >= 2)
    def _():
        _for_groups(*copies_of(t - 2), start=False)

    route_t = route_ref[...].T
    soff_col = soff_ref[...]
    slots = [_slot_rows(route_t, soff_col, k) for k in range(TOP_K)]

    def sort_rows(m):
        r_i = lax.broadcasted_iota(jnp.int32, (m, TOK_TILE), 0).astype(F32)
        onehot = jnp.zeros((m, TOK_TILE), F32)
        for k in range(TOP_K):
            onehot = jnp.where(r_i == slots[k], 1.0, onehot)
        loc_scr[buf, 0:m, :] = _dot(onehot.astype(BF16), h2_ref[...]).astype(BF16)

    fits = ng_ref[t] * ROW_ALIGN <= M_LOC_SHORT
    pl.when(fits)(lambda: sort_rows(M_LOC_SHORT))
    pl.when(jnp.logical_not(fits))(lambda: sort_rows(M_LOC))

    for e in range(N_EXP):
        lo = pl.multiple_of(lo_ref[t * N_EXP + e], ROW_ALIGN)
        sv = pl.multiple_of(sv_ref[t * N_EXP + e], ROW_ALIGN)
        loc_scr[buf, pl.ds(lo, ROW_ALIGN), :] = loc_scr[buf, pl.ds(lo, ROW_ALIGN), :] + carry_scr[e]
        pending = loc_scr[buf, pl.ds(sv, ROW_ALIGN), :]
        carry_scr[e] = jnp.where(rm_ref[t * N_EXP + e] > 0, pending, jnp.zeros_like(pending))

    _for_groups(*copies_of(t), start=True)

    @pl.when(t == nt - 1)
    def _():
        if nt >= 2:
            _for_groups(*copies_of(t - 1), start=False)
        _for_groups(*copies_of(t), start=False)
        zero_scr[...] = jnp.zeros(zero_scr.shape, BF16)
        tails = []
        for e in range(N_EXP):
            tails += _segment_copies(zero_scr, xs_ref, sem.at[0], 0, tail_ref[e], tail_ref[N_EXP + e],
                                     FFN_TILE // ROW_ALIGN - 1)
        _run_copies(tails)


def _dispatch(meta, h2, route, soff_col):
    nt = h2.shape[0] // TOK_TILE
    grid_spec = pltpu.PrefetchScalarGridSpec(
        num_scalar_prefetch=6,
        grid=(nt,),
        in_specs=[pl.BlockSpec((TOK_TILE, D_MODEL), lambda t, *_: (t, 0)),
                  pl.BlockSpec((TOK_TILE, LANES), lambda t, *_: (t, 0)),
                  pl.BlockSpec((None, LANES, 1), lambda t, *_: (t, 0, 0))],
        out_specs=pl.BlockSpec(memory_space=pl.ANY),
        scratch_shapes=[pltpu.VMEM((2, M_LOC + ROW_ALIGN, D_MODEL), BF16),
                        pltpu.VMEM((N_EXP, ROW_ALIGN, D_MODEL), BF16),
                        pltpu.VMEM((FFN_TILE, D_MODEL), BF16), pltpu.SemaphoreType.DMA((2,))],
    )
    return pl.pallas_call(
        functools.partial(_dispatch_body, nt=nt),
        grid_spec=grid_spec,
        out_shape=jax.ShapeDtypeStruct((meta["rows"] + 2 * M_LOC, D_MODEL), BF16),
        compiler_params=_params(("arbitrary",)),
        name="moe_dispatch",
    )(meta["n_groups"], meta["dispatch_rows"], meta["slot_start"], meta["carry_start"], meta["carry_rows"],
      meta["tail"], h2, route, soff_col)


def _ffn_body(be_ref, nu_ref, nx_ref, sl_ref, hf_ref, x_ref, bgu_ref, bd_ref, wgu_hbm, wd_hbm, y_ref,
              wgu_f32, wd_f32, wgu_scr, wd_scr, sem):
    i = pl.program_id(0)
    expert = be_ref[i]
    slot = sl_ref[i]
    prev = be_ref[jnp.maximum(i - 1, 0)]

    def weight_copies(e, s):
        return (pltpu.make_async_copy(wgu_hbm.at[e], wgu_f32.at[s], sem.at[s]),
                pltpu.make_async_copy(wd_hbm.at[e], wd_f32.at[s], sem.at[s]))

    @pl.when(i == 0)
    def _():
        for cp in weight_copies(expert, slot):
            cp.start()

    @pl.when(jnp.logical_and(i < nu_ref[0], jnp.logical_or(i == 0, expert != prev)))
    def _():
        for cp in weight_copies(expert, slot):
            cp.wait()
        wgu_scr[...] = wgu_f32[slot].astype(BF16)
        wd_scr[...] = wd_f32[slot].astype(BF16)

        @pl.when(nx_ref[i] >= 0)
        def _():
            for cp in weight_copies(nx_ref[i], 1 - slot):
                cp.start()

    def expert_rows(m):
        gu = _dot(x_ref[0:m, :], wgu_scr[...]) + bgu_ref[...]
        gate = jnp.minimum(gu[:, :D_FF], SWIGLU_LIMIT)
        up = jnp.clip(gu[:, D_FF:], -SWIGLU_LIMIT, SWIGLU_LIMIT)
        act = (up + 1.0) * gate * _sigmoid(SWIGLU_ALPHA * gate)
        y_ref[0:m, :] = (_dot(act.astype(BF16), wd_scr[...]) + bd_ref[...]).astype(BF16)

    active = i < nu_ref[0]
    pl.when(jnp.logical_and(active, hf_ref[i] == 0))(lambda: expert_rows(FFN_TILE))
    pl.when(jnp.logical_and(active, hf_ref[i] != 0))(lambda: expert_rows(FFN_TILE // 2))


def _ffn(meta, xs, w_gu, b_gu, w_down, b_down):
    rows = meta["rows"]
    nblk = rows // FFN_TILE
    row_blk = pl.BlockSpec((FFN_TILE, D_MODEL), lambda i, be, nu, *_: (jnp.minimum(i, nu[0] - 1), 0))
    grid_spec = pltpu.PrefetchScalarGridSpec(
        num_scalar_prefetch=5,
        grid=(nblk,),
        in_specs=[row_blk,
                  pl.BlockSpec((None, 1, 2 * D_FF), lambda i, be, *_: (be[i], 0, 0)),
                  pl.BlockSpec((None, 1, D_MODEL), lambda i, be, *_: (be[i], 0, 0)),
                  pl.BlockSpec(memory_space=pl.ANY), pl.BlockSpec(memory_space=pl.ANY)],
        out_specs=row_blk,
        scratch_shapes=[pltpu.VMEM((2, D_MODEL, 2 * D_FF), F32), pltpu.VMEM((2, D_FF, D_MODEL), F32),
                        pltpu.VMEM((D_MODEL, 2 * D_FF), BF16), pltpu.VMEM((D_FF, D_MODEL), BF16),
                        pltpu.SemaphoreType.DMA((2,))],
    )
    return pl.pallas_call(
        _ffn_body,
        grid_spec=grid_spec,
        out_shape=jax.ShapeDtypeStruct((rows, D_MODEL), BF16),
        compiler_params=_params(("arbitrary",)),
        name="moe_ffn",
    )(meta["blk_exp"], meta["n_used"], meta["next_exp"], meta["blk_slot"], meta["blk_half"], xs,
      b_gu.reshape(N_EXP, 1, 2 * D_FF), b_down.reshape(N_EXP, 1, D_MODEL), w_gu, w_down)


def _combine_body(ng_ref, ct_ref, route_ref, soff_ref, x2_ref, gtp_ref, gts_ref, gf_ref, ys_ref,
                  yp_ref, ysm_ref, loc_scr, moe_scr, sem, *, nt, nt_prompt, final):
    t = pl.program_id(0)
    buf = t % 2

    def copies_of(tt):
        def make(g, row):
            return pltpu.make_async_copy(
                ys_ref.at[pl.ds(pl.multiple_of(row, ROW_ALIGN), ROW_ALIGN)],
                loc_scr.at[tt % 2, pl.ds(pl.multiple_of(g * ROW_ALIGN, ROW_ALIGN), ROW_ALIGN)], sem.at[tt % 2])
        return ng_ref[tt], ct_ref, tt * GROUPS, make

    @pl.when(t == 0)
    def _():
        loc_scr[...] = jnp.zeros(loc_scr.shape, BF16)
        _for_groups(*copies_of(0), start=True)

    @pl.when(t + 1 < nt)
    def _():
        _for_groups(*copies_of(t + 1), start=True)

    _for_groups(*copies_of(t), start=False)

    route = route_ref[...]
    soff_row = soff_ref[...]
    lane = lax.broadcasted_iota(jnp.int32, (TOK_TILE, LANES), 1).astype(F32)
    slots = [jnp.sum(jnp.where(lane == route[:, k:k + 1], soff_row, 0.0), axis=1, keepdims=True)
             + route[:, 2 * TOP_K + k:2 * TOP_K + k + 1] for k in range(TOP_K)]

    def weighted_sum(m):
        c_i = lax.broadcasted_iota(jnp.int32, (TOK_TILE, m), 1).astype(F32)
        wmat = jnp.zeros((TOK_TILE, m), F32)
        for k in range(TOP_K):
            wmat = jnp.where(c_i == slots[k], route[:, TOP_K + k:TOP_K + k + 1], wmat)
        moe_scr[...] = _dot(wmat.astype(BF16), loc_scr[buf, 0:m, :])

    fits = ng_ref[t] * ROW_ALIGN <= M_LOC_SHORT
    pl.when(fits)(lambda: weighted_sum(M_LOC_SHORT))
    pl.when(jnp.logical_not(fits))(lambda: weighted_sum(M_LOC))

    gate = jnp.where(t >= nt_prompt, gts_ref[...], gtp_ref[...])
    xo = x2_ref[...] + gate * moe_scr[...]
    if final:
        ms = jnp.mean(xo * xo, axis=1, keepdims=True)
        xo = (xo * lax.rsqrt(ms + EPS)) * gf_ref[...]

    @pl.when(t < nt_prompt)
    def _():
        yp_ref[...] = xo

    @pl.when(t >= nt_prompt)
    def _():
        ysm_ref[...] = xo


def _combine(meta, route, soff_row, x2, gt2_p, gt2_s, g_final, ys, n_prompt, n_sample, final):
    nt = x2.shape[0] // TOK_TILE
    nt_prompt = n_prompt // TOK_TILE
    tiles_per_batch = nt_prompt // gt2_p.shape[0]
    grid_spec = pltpu.PrefetchScalarGridSpec(
        num_scalar_prefetch=2,
        grid=(nt,),
        in_specs=[pl.BlockSpec((TOK_TILE, LANES), lambda t, *_: (t, 0)),
                  pl.BlockSpec((None, 1, LANES), lambda t, *_: (t, 0, 0)),
                  pl.BlockSpec((TOK_TILE, D_MODEL), lambda t, *_: (t, 0)),
                  pl.BlockSpec((None, 1, D_MODEL),
                               lambda t, *_: (jnp.minimum(t, nt_prompt - 1) // tiles_per_batch, 0, 0)),
                  pl.BlockSpec((TOK_TILE, D_MODEL), lambda t, *_: (0, 0)),
                  pl.BlockSpec((1, D_MODEL), lambda t, *_: (0, 0)),
                  pl.BlockSpec(memory_space=pl.ANY)],
        out_specs=[pl.BlockSpec((TOK_TILE, D_MODEL), lambda t, *_: (jnp.minimum(t, nt_prompt - 1), 0)),
                   pl.BlockSpec((TOK_TILE, D_MODEL), lambda t, *_: (0, 0))],
        scratch_shapes=[pltpu.VMEM((2, M_LOC, D_MODEL), BF16), pltpu.VMEM((TOK_TILE, D_MODEL), F32),
                        pltpu.SemaphoreType.DMA((2,))],
    )
    return pl.pallas_call(
        functools.partial(_combine_body, nt=nt, nt_prompt=nt_prompt, final=final),
        grid_spec=grid_spec,
        out_shape=[jax.ShapeDtypeStruct((n_prompt, D_MODEL), F32), jax.ShapeDtypeStruct((n_sample, D_MODEL), F32)],
        compiler_params=_params(("arbitrary",)),
        name="moe_combine",
    )(meta["n_groups"], meta["combine_rows"], route, soff_row, x2, gt2_p, gt2_s, g_final, ys)


def _moe_offsets(cnt):
    nt = cnt.shape[0]
    ra = ROW_ALIGN
    prefix = jnp.cumsum(cnt, axis=0) - cnt
    total = jnp.sum(cnt, axis=0)
    pending = prefix % ra
    used = pending + cnt
    seg = (used + ra - 1) // ra * ra
    lo = jnp.cumsum(seg, axis=1) - seg
    n_groups = jnp.sum(seg, axis=1) // ra
    gpad = (total + FFN_TILE - 1) // FFN_TILE * FFN_TILE
    gstart = jnp.cumsum(gpad) - gpad
    base = gstart[None, :] + prefix // ra * ra
    last = (jnp.arange(nt) == nt - 1)[:, None]
    n_write = jnp.where(last, seg // ra, used // ra)
    carry_start = lo + used // ra * ra
    carry_rows = jnp.where(last, 0, used % ra)
    g = jnp.arange(GROUPS)
    slot_end = (lo + seg) // ra
    owner = jnp.minimum(jnp.sum(g[None, :, None] >= slot_end[:, None, :], axis=2), N_EXP - 1)
    pick = lambda a: jnp.sum(jnp.where(owner[:, :, None] == jnp.arange(N_EXP), a[:, None, :], 0), axis=2)
    k = g[None, :] - pick(lo) // ra
    row = pick(base) + k * ra
    valid = g[None, :] < n_groups[:, None]
    rows = (nt * TOK_TILE * TOP_K + N_EXP * (FFN_TILE - 1) + FFN_TILE - 1) // FFN_TILE * FFN_TILE
    spare = rows + (jnp.arange(nt) % 2)[:, None] * M_LOC + g[None, :] * ra
    combine_rows = jnp.where(valid, row, 0)
    dispatch_rows = jnp.where(valid & (k < pick(n_write)), row, spare)

    nblk_e = gpad // FFN_TILE
    blk_end = jnp.cumsum(nblk_e)
    n_used = jnp.maximum(blk_end[-1], 1)
    blk = jnp.minimum(jnp.arange(rows // FFN_TILE, dtype=jnp.int32), n_used - 1)
    blk_exp = jnp.minimum(jnp.sum(blk[:, None] >= blk_end[None, :], axis=1), N_EXP - 1)
    experts = jnp.arange(N_EXP)
    following = lax.cummin(jnp.where(nblk_e > 0, experts, N_EXP), reverse=True)
    next_of = jnp.concatenate([following[1:], jnp.full((1,), N_EXP, following.dtype)])
    next_of = jnp.where(next_of >= N_EXP, -1, next_of)
    parity = (jnp.cumsum(nblk_e > 0) - 1) % 2
    pick_e = lambda a: jnp.sum(jnp.where(blk_exp[:, None] == experts[None, :], a[None, :], 0), axis=1)
    blk_rows = pick_e(total) - (blk - pick_e(blk_end - nblk_e)) * FFN_TILE
    blk_half = blk_rows <= FFN_TILE // 2
    total16 = (total + ra - 1) // ra * ra
    tail = jnp.concatenate([gstart + total16, (gpad - total16) // ra])
    i32 = lambda a: a.astype(jnp.int32).reshape(-1)
    return dict(n_groups=i32(n_groups), dispatch_rows=i32(dispatch_rows), combine_rows=i32(combine_rows),
                slot_start=i32(lo), carry_start=i32(carry_start), carry_rows=i32(carry_rows), tail=i32(tail),
                blk_exp=i32(blk_exp), n_used=i32(n_used), next_exp=i32(pick_e(next_of)), blk_slot=i32(pick_e(parity)),
                blk_half=i32(blk_half),
                slot_off=(lo + pending).astype(F32), rows=rows)


def _split_w_in(w_in):
    g0 = C_GATES
    return (w_in[:, :g0].astype(BF16), w_in[:, g0 + N_GATE:].astype(BF16),
            _pad_lanes(w_in[:, g0:g0 + N_GATE]).astype(BF16))


def _pad_lanes(a, value=0.0):
    return jnp.pad(a, [(0, 0)] * (a.ndim - 1) + [(0, LANES - a.shape[-1])], constant_values=value)


def kernel(x_prompt, x_sample, c_prompt, c_sample, cache_k, cache_v, state_conv, state_C, state_n, state_m, page_table, w_ada, b_ada, g_norm1, g_norm2, w_in, b_gates, lambda_q1, lambda_k1, lambda_q2, lambda_k2, g_subln, w_conv, b_conv, g_mnorm, w_up_a, w_up_b, w_out, w_router, b_router, w_gu, b_gu, w_down, b_down, g_final):
    B, S, D = x_prompt.shape
    Bd, Td, _ = x_sample.shape
    depth = w_in.shape[0]
    n_pool = cache_k.shape[1]
    past_len = page_table.shape[1] * PAGE_SIZE
    n_p, n_s = B * S, Bd * Td
    n_all = n_p + n_s
    assert D == D_MODEL and n_s == TOK_TILE and S % MLSTM_CHUNK == 0 and n_p % TOK_TILE == 0
    assert page_table.shape[1] % PAGES_PER_STEP == 0

    cos_p, sin_p = _rope_tables(np.arange(S))
    cos_s, sin_s = _rope_tables(np.tile(past_len + np.arange(Td), Bd))
    hp = x_prompt.reshape(n_p, D)
    hs = x_sample.reshape(n_s, D)
    c_all = jnp.concatenate([c_prompt, c_sample], axis=0)
    outs = [[] for _ in range(12)]

    for l in range(depth):
        lam_init = 0.8 - 0.6 * math.exp(-0.3 * l)
        mod = _ada(c_all, w_ada[l], b_ada[l])
        mods = [mod[:, j * D:(j + 1) * D] for j in range(6)]
        mp = [m[:B].reshape(B, 1, D) for m in mods]
        ms_ = [jnp.repeat(m[B:], Td, axis=0).reshape(1, n_s, D) for m in mods]
        w_parts = _split_w_in(w_in[l])
        g1 = g_norm1[l].reshape(1, D)
        lam_vec = jnp.stack([lambda_q1[l], lambda_k1[l], lambda_q2[l], lambda_k2[l]])
        gsub = g_subln[l].reshape(1, DV_A)
        bg = _pad_lanes(b_gates[l].reshape(1, N_GATE))
        cw, cb = w_conv[l], b_conv[l].reshape(1, 2 * W_M)

        (q_p, _, kb_p, v_p, _, qk_p, vm_p, om_p, ga_p, gb_p, gt_p, vt_p, kt_p) = _inproj(
            hp, mp[1], mp[0], g1, cos_p, sin_p, w_parts, 1, PROJ_TILE)
        (q_s, k_s, kb_s, v_s, vb_s, qk_s, vm_s, om_s, ga_s, gb_s, gt_s, _, _) = _inproj(
            hs, ms_[1], ms_[0], g1, cos_s, sin_s, w_parts, n_s, n_s)
        k_p = jnp.transpose(kt_p.reshape(B, H_A, 2, DK_A, S), (0, 4, 1, 2, 3))

        kt_pool = jnp.transpose(cache_k[l], (0, 2, 3, 4, 1)).reshape(n_pool, W_QA, PAGE_SIZE)
        v_pool = cache_v[l].reshape(n_pool, PAGE_SIZE * H_A, DV_A)
        oa_p, oa_s = _attention(q_p, kb_p, vt_p, q_s, kb_s, vb_s, kt_pool, v_pool, page_table, lam_vec, gsub,
                                B, S, lam_init, Td)

        zeros = lambda *shape: jnp.zeros(shape, F32)
        hm_p, cst_p, C_p, nn_p, m_p = _mlstm(qk_p, vm_p, om_p, gt_p, cw, cb, bg, g_mnorm[l],
                                             zeros(B, CONV_W - 1, 2 * W_M), zeros(B, H_M, DH_M, DH_M),
                                             zeros(B, H_M, DH_M), zeros(B, 1, LANES), B, S)
        hm_s, cst_s, C_s, nn_s, m_s = _mlstm(qk_s, vm_s, om_s, gt_s, cw, cb, bg, g_mnorm[l],
                                             state_conv[l], state_C[l], state_n[l],
                                             _pad_lanes(state_m[l]).reshape(Bd, 1, LANES), Bd, Td)

        wa, wb, wo = w_up_a[l].astype(BF16), w_up_b[l].astype(BF16), w_out[l].astype(BF16)
        wr = _pad_lanes(w_router[l]).astype(BF16)
        br = _pad_lanes(b_router[l].reshape(1, N_EXP))
        g2 = g_norm2[l].reshape(1, D)
        part = _merge(oa_p, hm_p, ga_p, gb_p, hp, mp[2], mp[4], mp[3], g2, wa, wb, wo, wr, br,
                      1, PROJ_TILE, 0, n_all)
        x2, h2, route, cnt = _merge(oa_s, hm_s, ga_s, gb_s, hs, ms_[2], ms_[4], ms_[3], g2, wa, wb, wo, wr, br,
                                    n_s, n_s, n_p, n_all, prev=part)

        meta = _moe_offsets(jnp.round(cnt[:, 0, ROUTE_SEL:ROUTE_SEL + N_EXP]).astype(jnp.int32))
        soff = _pad_lanes(meta["slot_off"])
        xs = _dispatch(meta, h2, route, soff[:, :, None])
        ys = _ffn(meta, xs, w_gu[l], b_gu[l], w_down[l], b_down[l])
        final = l == depth - 1
        hp, hs = _combine(meta, route, soff[:, None, :], x2, mp[5], ms_[5][0], g_final.reshape(1, D), ys,
                          n_p, n_s, final)

        for j, a in enumerate([k_p, v_p.reshape(B, S, H_A, DV_A), cst_p, C_p, nn_p,
                               m_p[:, 0, :H_M],
                               k_s.reshape(Bd, Td, H_A, 2, DK_A), v_s.reshape(Bd, Td, H_A, DV_A), cst_s, C_s, nn_s,
                               m_s[:, 0, :H_M]]):
            outs[j].append(a)

    return (hp.reshape(B, S, D), hs.reshape(Bd, Td, D)) + tuple(jnp.stack(o) for o in outs)
```

```python
import functools
import math

import numpy as np
import jax
import jax.numpy as jnp
from jax import lax
from jax.experimental import pallas as pl
from jax.experimental.pallas import tpu as pltpu

F32 = jnp.float32
BF16 = jnp.bfloat16

D_MODEL = 1024
H_A = 4
DK_A = 64
DV_A = 2 * DK_A
ROPE_THETA = 10000.0
H_M = 4
DH_M = 128
CONV_W = 4
N_EXP = 32
TOP_K = 4
D_FF = D_MODEL
SWIGLU_LIMIT = 7.0
SWIGLU_ALPHA = 1.702
EPS = 1e-6
PAGE_SIZE = 128

W_QA = H_A * 2 * DK_A
W_VA = H_A * DV_A
W_M = H_M * DH_M
N_GATE = 2 * H_M

LANES = 128
ROW_ALIGN = 16
TOK_TILE = 256
PROJ_TILE = 512
FFN_TILE = 512
ATT_TILE = 256
ATT_HEADS = H_A
MLSTM_CHUNK = 256
PAGES_PER_STEP = 16
CHUNKS_PER_STEP = 2
NEG_BIG = -1e30
LOG2_E = math.log2(math.e)
VMEM_LIMIT = 56 * 1024 * 1024

C_QA, C_KA, C_VA, C_QK, C_VM, C_OM, C_GATES = 0, 512, 1024, 1536, 2560, 3072, 3584
M_LOC = ((TOK_TILE * TOP_K + 2 * N_EXP * (ROW_ALIGN - 1)) + 255) // 256 * 256
GROUPS = M_LOC // ROW_ALIGN
ONEHOT_SLAB = 256
M_LOC_SHORT = TOK_TILE * TOP_K + N_EXP * ROW_ALIGN
ROUTE_SEL = 32
GROUP_UNROLL = 4


def _dot(a, b):
    return jnp.dot(a, b, preferred_element_type=F32)


def _dot_nt(a, b):
    return lax.dot_general(a, b, (((1,), (1,)), ((), ())), preferred_element_type=F32)


def _sigmoid(x):
    return 0.5 * jnp.tanh(0.5 * x) + 0.5


def _start_all(copies):
    for cp in copies:
        cp.start()


def _params(sem):
    return pltpu.CompilerParams(dimension_semantics=sem, vmem_limit_bytes=VMEM_LIMIT)


def _ada_body(c_ref, w_ref, b_ref, o_ref):
    c = c_ref[...]
    s = c * _sigmoid(c)
    s_hi = s.astype(BF16)
    s_lo = (s - s_hi.astype(F32)).astype(BF16)
    w = w_ref[...]
    w_hi = w.astype(BF16)
    w_lo = (w - w_hi.astype(F32)).astype(BF16)
    o_ref[...] = _dot(s_hi, w_hi) + _dot(s_lo, w_hi) + _dot(s_hi, w_lo) + b_ref[...]


def _ada(c_all, w_ada, b_ada):
    rows = c_all.shape[0]
    n_out = w_ada.shape[1]
    blk = D_MODEL
    return pl.pallas_call(
        _ada_body,
        grid=(n_out // blk,),
        in_specs=[pl.BlockSpec((rows, D_MODEL), lambda j: (0, 0)),
                  pl.BlockSpec((D_MODEL, blk), lambda j: (0, j)),
                  pl.BlockSpec((1, blk), lambda j: (0, j))],
        out_specs=pl.BlockSpec((rows, blk), lambda j: (0, j)),
        out_shape=jax.ShapeDtypeStruct((rows, n_out), F32),
        compiler_params=_params(("arbitrary",)),
        name="ada",
    )(c_all, w_ada, b_ada.reshape(1, n_out))


def _rope(z, cos, sin):
    lane = lax.broadcasted_iota(jnp.int32, (z.shape[0], LANES), 1)
    first_half = (lane % DK_A) < (DK_A // 2)
    out = []
    for h in range(H_A):
        xh = z[:, h * LANES:(h + 1) * LANES]
        partner = jnp.where(first_half, pltpu.roll(xh, LANES - DK_A // 2, 1), pltpu.roll(xh, DK_A // 2, 1))
        out.append(xh * cos + partner * sin)
    return jnp.concatenate(out, axis=1)


def _conv_silu(u, ext, cw_ref, cb_ref):
    rows = u.shape[0]
    full = jnp.concatenate([ext, u], axis=0)
    conv = cb_ref[...] + cw_ref[CONV_W - 1:CONV_W, :] * u
    for j in range(CONV_W - 1):
        conv = conv + cw_ref[j:j + 1, :] * pltpu.roll(full, CONV_W - 1 - j, 0)[8:8 + rows]
    return conv * _sigmoid(conv)


def _inproj_body(x_ref, sc_ref, sh_ref, g_ref, cos_ref, sin_ref, w_ref, wg_ref, wt_ref,
                 q_ref, k_ref, kb_ref, v_ref, vb_ref, qk_ref, vm_ref, om_ref, ga_ref, gb_ref, gt_ref, vt_ref, kt_ref):
    x = x_ref[...]
    ms = jnp.mean(x * x, axis=1, keepdims=True)
    h = (x * lax.rsqrt(ms + EPS)) * g_ref[...] * (1.0 + sc_ref[...]) + sh_ref[...]
    hb = h.astype(BF16)
    cos = cos_ref[...]
    sin = sin_ref[...]

    def seg(lo, n):
        return _dot(hb, w_ref[:, lo:lo + n])

    q = _rope(seg(C_QA, W_QA), cos, sin) * (DK_A ** -0.5 * LOG2_E)
    q_ref[...] = q.astype(BF16)
    k = _rope(seg(C_KA, W_QA), cos, sin)
    k_ref[...] = k
    kb_ref[...] = k.astype(BF16)
    kt_ref[...] = k.T
    v = seg(C_VA, W_VA)
    for h in range(H_A):
        v_ref[pl.ds(h, v.shape[0], stride=H_A), :] = v[:, h * DV_A:(h + 1) * DV_A]
    vb_ref[...] = v.astype(BF16)
    for s in range(vt_ref.shape[0]):
        vt_ref[s] = v[s * ATT_TILE:(s + 1) * ATT_TILE].T.astype(BF16)
    qk_ref[...] = seg(C_QK, 2 * W_M)
    vm_ref[...] = seg(C_VM, W_M).astype(BF16)
    om_ref[...] = seg(C_OM, W_M)
    ga_ref[...] = _dot(hb, wg_ref[:, :D_MODEL])
    gb_ref[...] = _dot(hb, wg_ref[:, D_MODEL:])
    gt_ref[...] = _dot(hb, wt_ref[...])


def _inproj(x, sc, sh, g1, cos, sin, w_parts, rows_per_mod, tile):
    n = x.shape[0]
    nt = n // tile
    tiles_per_group = nt // sc.shape[0]
    tab_tiles = cos.shape[0] // tile
    slabs = tile // ATT_TILE
    tok = lambda w: pl.BlockSpec((tile, w), lambda i: (i, 0))
    mod = pl.BlockSpec((None, rows_per_mod, D_MODEL), lambda i: (i // tiles_per_group, 0, 0))
    tab = pl.BlockSpec((tile, LANES), lambda i: (i % tab_tiles, 0))
    tok_out = lambda w, dt: (tok(w), jax.ShapeDtypeStruct((n, w), dt))
    outs = [
        tok_out(W_QA, BF16),
        tok_out(W_QA, F32), tok_out(W_QA, BF16),
        (pl.BlockSpec((tile * H_A, DV_A), lambda i: (i, 0)), jax.ShapeDtypeStruct((n * H_A, DV_A), F32)),
        tok_out(W_VA, BF16),
        tok_out(2 * W_M, F32), tok_out(W_M, BF16), tok_out(W_M, F32),
        tok_out(D_MODEL, F32), tok_out(D_MODEL, F32), tok_out(LANES, F32),
        (pl.BlockSpec((slabs, W_VA, ATT_TILE), lambda i: (i, 0, 0)),
         jax.ShapeDtypeStruct((n // ATT_TILE, W_VA, ATT_TILE), BF16)),
        (pl.BlockSpec((None, W_QA, tile), lambda i: (i // tab_tiles, 0, i % tab_tiles)),
         jax.ShapeDtypeStruct((nt // tab_tiles, W_QA, tab_tiles * tile), F32)),
    ]
    return pl.pallas_call(
        _inproj_body,
        grid=(nt,),
        in_specs=[tok(D_MODEL), mod, mod, pl.BlockSpec((1, D_MODEL), lambda i: (0, 0)), tab, tab,
                  *[pl.BlockSpec(w.shape, lambda i: (0, 0), pipeline_mode=pl.Buffered(1)) for w in w_parts]],
        out_specs=[spec for spec, _ in outs],
        out_shape=[shape for _, shape in outs],
        compiler_params=_params(("arbitrary",)),
        name="inproj",
    )(x, sc, sh, g1, cos, sin, *w_parts)


def _rope_tables(pos):
    half = DK_A // 2
    inv = ROPE_THETA ** (-np.arange(half, dtype=np.float64) * 2.0 / DK_A)
    ang = np.asarray(pos, np.float64)[:, None] * inv[None, :]
    cos = np.cos(ang)
    sin = np.sin(ang)
    cos64 = np.concatenate([cos, cos], axis=1)
    sin64 = np.concatenate([-sin, sin], axis=1)
    return (jnp.asarray(np.tile(cos64, (1, LANES // DK_A)), F32),
            jnp.asarray(np.tile(sin64, (1, LANES // DK_A)), F32))


def _lambda_value(lam_ref, lam_init):
    lv = lam_ref[...]
    l1 = jnp.sum(lv[0:1, :] * lv[1:2, :], axis=1, keepdims=True)
    l2 = jnp.sum(lv[2:3, :] * lv[3:4, :], axis=1, keepdims=True)
    return jnp.exp(l1) - jnp.exp(l2) + lam_init


def _subln(o, g, lam_init):
    ms = jnp.mean(o * o, axis=1, keepdims=True)
    return (o * lax.rsqrt(ms + EPS)) * g * (1.0 - lam_init)


def _prompt_tile(q_ref, k_ref, vt_ref, lam, g_ref, o_ref, m_scr, acc_scr, i, lam_init):
    tq = ATT_TILE
    lane = lax.broadcasted_iota(jnp.int32, (tq, LANES), 1)
    qs = []
    for hh in range(ATT_HEADS):
        q = q_ref[:, hh * LANES:(hh + 1) * LANES]
        zero = jnp.zeros_like(q)
        qs.append(jnp.concatenate([jnp.where(lane < DK_A, q, zero), jnp.where(lane >= DK_A, q, zero)], axis=0))
    for hh in range(ATT_HEADS):
        m_scr[hh][...] = jnp.full(m_scr[hh].shape, NEG_BIG, F32)
        acc_scr[hh][...] = jnp.zeros(acc_scr[hh].shape, F32)
    ones = jnp.ones((ROW_ALIGN, tq), BF16)

    def scores(hh, j):
        start = pl.multiple_of(j * tq, tq)
        return _dot_nt(k_ref[pl.ds(start, tq), hh * LANES:(hh + 1) * LANES], qs[hh])

    def update_all(j, mask):
        sts = [scores(hh, j) for hh in range(ATT_HEADS)]
        if mask is not None:
            sts = [jnp.where(mask, st, NEG_BIG) for st in sts]
        pts, alphas = [], []
        for hh in range(ATT_HEADS):
            m_old = m_scr[hh][...]
            m_new = jnp.maximum(m_old, jnp.max(sts[hh], axis=0, keepdims=True))
            alphas.append(jnp.exp2(m_old - m_new))
            pts.append(jnp.exp2(sts[hh] - m_new).astype(BF16))
            m_scr[hh][...] = m_new
        for hh in range(ATT_HEADS):
            vt = jnp.concatenate([vt_ref[j, hh * LANES:(hh + 1) * LANES, :], ones], axis=0)
            acc_scr[hh][...] = alphas[hh] * acc_scr[hh][...] + _dot(vt, pts[hh])

    def off_diag(j, carry):
        update_all(j, None)
        return carry

    lax.fori_loop(0, i, off_diag, 0)
    key = lax.broadcasted_iota(jnp.int32, (tq, 2 * tq), 0)
    qry = lax.broadcasted_iota(jnp.int32, (tq, 2 * tq), 1) % tq
    update_all(i, key <= qry)
    for hh in range(ATT_HEADS):
        acc = acc_scr[hh][...]
        ot = acc[:DV_A] / acc[DV_A:DV_A + 1]
        at = ot[:, :tq] - lam * ot[:, tq:]
        ms = jnp.mean(at * at, axis=0, keepdims=True)
        at = (at * lax.rsqrt(ms + EPS)) * g_ref[...] * (1.0 - lam_init)
        o_ref[:, hh * LANES:(hh + 1) * LANES] = at.T.astype(BF16)


def _sample_chunks(pt_ref, q_ref, kn_ref, vn_ref, lam, g_ref, kpool_ref, vpool_ref, o_ref,
                   kbuf, vbuf, sem, m_scr, l_scr, acc_scr, step, parts, *, lam_init, n_chunks, n_seq, t_new):
    steps_per_seq = n_chunks // CHUNKS_PER_STEP
    b = step // steps_per_seq
    c0 = (step % steps_per_seq) * CHUNKS_PER_STEP
    opens = 0 in parts
    closes = CHUNKS_PER_STEP - 1 in parts

    def chunk_copies(bb, c, slot):
        out = []
        for j in range(PAGES_PER_STEP):
            page = pt_ref[bb, c * PAGES_PER_STEP + j]
            out.append(pltpu.make_async_copy(kpool_ref.at[page], kbuf.at[slot, j], sem.at[slot]))
            out.append(pltpu.make_async_copy(vpool_ref.at[page], vbuf.at[slot, j], sem.at[slot]))
        return out

    if opens:
        @pl.when(step == 0)
        def _():
            _start_all(chunk_copies(0, 0, 0) + chunk_copies(0, 1, 1))

    q = q_ref[...].astype(F32)
    qt = jnp.concatenate([q] * (2 * H_A), axis=0)
    row = lax.broadcasted_iota(jnp.int32, qt.shape, 0)
    col = lax.broadcasted_iota(jnp.int32, qt.shape, 1)
    qbd = jnp.where(col // DK_A == row // t_new, qt, 0.0).astype(BF16)

    if opens:
        @pl.when(c0 == 0)
        def _():
            m_scr[...] = jnp.full(m_scr.shape, NEG_BIG, F32)
            l_scr[...] = jnp.zeros(l_scr.shape, F32)
            acc_scr[...] = jnp.zeros(acc_scr.shape, F32)

    rows_h = 2 * t_new

    def update(s, v):
        m_old = m_scr[...]
        m_new = jnp.maximum(m_old, jnp.max(s, axis=1, keepdims=True))
        alpha = jnp.exp2(m_old - m_new)
        p = jnp.exp2(s - m_new)
        l_scr[...] = alpha * l_scr[...] + jnp.sum(p, axis=1, keepdims=True)
        pv = _dot(p.astype(BF16), v)
        own = [pv[h * rows_h:(h + 1) * rows_h, h * DV_A:(h + 1) * DV_A] for h in range(H_A)]
        acc_scr[...] = alpha * acc_scr[...] + jnp.concatenate(own, axis=0)
        m_scr[...] = m_new

    def chunk(c, slot):
        for cp in chunk_copies(b, c, slot):
            cp.wait()
        kt = jnp.concatenate([kbuf[slot, j].astype(BF16) for j in range(PAGES_PER_STEP)], axis=1)

        def page_v(j):
            return jnp.concatenate([vbuf.at[slot, j][pl.ds(h, PAGE_SIZE, stride=H_A), :].astype(BF16)
                                    for h in range(H_A)], axis=1)

        v = jnp.concatenate([page_v(j) for j in range(PAGES_PER_STEP)], axis=0)
        update(_dot(qbd, kt), v)

        @pl.when(c + 2 < n_chunks)
        def _():
            _start_all(chunk_copies(b, c + 2, slot))

        @pl.when(jnp.logical_and(c + 2 >= n_chunks, b + 1 < n_seq))
        def _():
            _start_all(chunk_copies(b + 1, c + 2 - n_chunks, slot))

    for j in parts:
        chunk(c0 + j, j % 2)

    def finish():
        zpad = jnp.zeros((PAGE_SIZE - t_new, W_QA), F32)
        kn = jnp.concatenate([kn_ref[...].astype(F32), zpad], axis=0).astype(BF16)
        vn = jnp.concatenate([vn_ref[...].astype(F32), zpad], axis=0).astype(BF16)
        s = _dot_nt(qbd, kn)
        row = lax.broadcasted_iota(jnp.int32, s.shape, 0) % t_new
        col = lax.broadcasted_iota(jnp.int32, s.shape, 1)
        update(jnp.where(col <= row, s, NEG_BIG), vn)
        o = acc_scr[...] / l_scr[...]
        outs = []
        for h in range(H_A):
            r0 = h * rows_h
            outs.append(_subln(o[r0:r0 + t_new] - lam * o[r0 + t_new:r0 + rows_h], g_ref[...], lam_init))
        o_ref[...] = jnp.concatenate(outs, axis=1).astype(BF16)

    if closes:
        pl.when(c0 + CHUNKS_PER_STEP == n_chunks)(finish)


def _attention_body(pt_ref, q_ref, k_ref, vt_ref, lam_ref, gcol_ref, qs_ref, kn_ref, vn_ref, grow_ref,
                    kpool_ref, vpool_ref, o_ref, os_ref, *scratch, lam_init, n_chunks, n_seq, t_new):
    m_scr, acc_scr = scratch[:ATT_HEADS], scratch[ATT_HEADS:2 * ATT_HEADS]
    kbuf, vbuf, sem, ms_scr, ls_scr, accs_scr = scratch[2 * ATT_HEADS:]
    step = pl.program_id(0) * pl.num_programs(1) + pl.program_id(1)
    lam = _lambda_value(lam_ref, lam_init)
    sample = functools.partial(_sample_chunks, pt_ref, qs_ref, kn_ref, vn_ref, lam, grow_ref, kpool_ref, vpool_ref,
                               os_ref, kbuf, vbuf, sem, ms_scr, ls_scr, accs_scr, step,
                               lam_init=lam_init, n_chunks=n_chunks, n_seq=n_seq, t_new=t_new)
    sample((0,))
    _prompt_tile(q_ref, k_ref, vt_ref, lam, gcol_ref, o_ref, m_scr, acc_scr, pl.program_id(1), lam_init)
    sample((1,))


def _attention(q, k, vt, q_s, k_new, v_new, cache_k, cache_v, page_table, lam_vec, g_subln, batch, seq,
               lam_init, t_new):
    nq = seq // ATT_TILE
    bd, n_pages = page_table.shape
    n_chunks = n_pages // PAGES_PER_STEP
    steps_per_seq = n_chunks // CHUNKS_PER_STEP
    assert ATT_HEADS == H_A and CHUNKS_PER_STEP % 2 == 0 and n_chunks % CHUNKS_PER_STEP == 0
    assert batch * nq == bd * steps_per_seq
    n_rows = 2 * H_A * t_new
    kv = pl.BlockSpec((seq, W_QA), lambda b, i, pt: (b, 0))
    vts = pl.BlockSpec((nq, W_VA, ATT_TILE), lambda b, i, pt: (b, 0, 0))
    qo = pl.BlockSpec((ATT_TILE, W_QA), lambda b, i, pt: (b * nq + i, 0))
    new = pl.BlockSpec((None, t_new, W_QA), lambda b, i, pt: ((b * nq + i) // steps_per_seq, 0, 0))
    const = lambda shape: pl.BlockSpec(shape, lambda b, i, pt: (0, 0))
    page_buf = pltpu.VMEM((2, PAGES_PER_STEP, W_QA, PAGE_SIZE), F32)
    grid_spec = pltpu.PrefetchScalarGridSpec(
        num_scalar_prefetch=1,
        grid=(batch, nq),
        in_specs=[qo, kv, vts, const((4, DK_A)), const((DV_A, 1)), new, new, new, const((1, DV_A)),
                  pl.BlockSpec(memory_space=pl.ANY), pl.BlockSpec(memory_space=pl.ANY)],
        out_specs=[qo, new],
        scratch_shapes=[pltpu.VMEM((1, 2 * ATT_TILE), F32)] * ATT_HEADS
                       + [pltpu.VMEM((DV_A + ROW_ALIGN, 2 * ATT_TILE), F32)] * ATT_HEADS
                       + [page_buf, page_buf, pltpu.SemaphoreType.DMA((2,)), pltpu.VMEM((n_rows, 1), F32),
                          pltpu.VMEM((n_rows, 1), F32), pltpu.VMEM((n_rows, DV_A), F32)],
    )
    o_p, o_s = pl.pallas_call(
        functools.partial(_attention_body, lam_init=lam_init, n_chunks=n_chunks, n_seq=bd, t_new=t_new),
        grid_spec=grid_spec,
        out_shape=[jax.ShapeDtypeStruct((batch * seq, W_VA), BF16), jax.ShapeDtypeStruct((bd, t_new, W_VA), BF16)],
        compiler_params=_params(("arbitrary", "arbitrary")),
        name="attention",
    )(page_table, q, k, vt, lam_vec, g_subln.reshape(DV_A, 1), q_s.reshape(bd, t_new, W_QA),
      k_new.reshape(bd, t_new, W_QA), v_new.reshape(bd, t_new, W_VA), g_subln, cache_k, cache_v)
    return o_p, o_s.reshape(bd * t_new, W_VA)


def _mlstm_body(qk_ref, vm_ref, om_ref, gt_ref, cw_ref, cb_ref, bg_ref, gm_ref, cbuf_ref, c0_ref, n0_ref, m0_ref,
                h_ref, cst_ref, cout_ref, nout_ref, mout_ref, ext_scr, c_scr, n_scr, m_scr, *, tb, L, nc):
    c_idx = pl.program_id(1)

    @pl.when(c_idx == 0)
    def _():
        ext_scr[...] = jnp.zeros(ext_scr.shape, F32)
        ext_scr[8 - (CONV_W - 1):8, :] = cbuf_ref[...]
        c_scr[...] = c0_ref[...]
        n_scr[...] = n0_ref[...]
        m_scr[...] = m0_ref[...]

    pad = L - tb
    u = qk_ref[...]
    if pad:
        u = jnp.concatenate([u, jnp.zeros((pad, u.shape[1]), F32)], axis=0)
    a = _conv_silu(u, ext_scr[...], cw_ref, cb_ref)
    if not pad:
        ext_scr[...] = u[L - 8:L]

    @pl.when(c_idx == nc - 1)
    def _():
        cst_ref[...] = qk_ref[tb - (CONV_W - 1):tb, :]

    g = gt_ref[...] + bg_ref[...]
    li = g
    lf = jnp.minimum(g, 0.0) - jnp.log1p(jnp.exp(-jnp.abs(g)))
    if pad:
        zpad = jnp.zeros((pad, LANES), F32)
        li = jnp.concatenate([li, zpad + NEG_BIG], axis=0)
        lf = jnp.concatenate([lf, zpad], axis=0)
    row = lax.broadcasted_iota(jnp.int32, (L, LANES), 0)
    lane = lax.broadcasted_iota(jnp.int32, (L, LANES), 1)
    bcum = lf
    shift = 1
    while shift < L:
        bcum = bcum + jnp.where(row >= shift, pltpu.roll(bcum, shift, 0), 0.0)
        shift *= 2
    gates = jnp.where(lane < H_M, li, bcum)
    gates_t = gates.T
    tri = lax.broadcasted_iota(jnp.int32, (L, L), 0) >= lax.broadcasted_iota(jnp.int32, (L, L), 1)
    m_all = m_scr[...]
    lane1 = lax.broadcasted_iota(jnp.int32, (1, LANES), 1)
    m_next = m_all
    vall = vm_ref[...]
    if pad:
        vall = jnp.concatenate([vall, jnp.zeros((pad, vall.shape[1]), BF16)], axis=0)

    for h in range(H_M):
        li_col = gates[:, h:h + 1]
        b_col = gates[:, H_M + h:H_M + h + 1]
        src_row = gates_t[h:h + 1, :] - gates_t[H_M + h:H_M + h + 1, :]
        m_prev = m_all[:, h:h + 1]
        b_last = b_col[L - 1:L, :]
        log_d = jnp.where(tri, b_col + src_row, NEG_BIG)
        inter = b_col + m_prev
        mt = jnp.maximum(inter, jnp.max(log_d, axis=1, keepdims=True))
        q = a[:, h * DH_M:(h + 1) * DH_M]
        k = a[:, W_M + h * DH_M:W_M + (h + 1) * DH_M] * (DH_M ** -0.5)
        v = vall[:, h * DH_M:(h + 1) * DH_M]
        qb = q.astype(BF16)
        s = _dot_nt(qb, k.astype(BF16)) * jnp.exp(log_d - mt)
        ei = jnp.exp(inter - mt)
        c_old = c_scr[h]
        n_old = n_scr[h:h + 1, :]
        num = ei * _dot(qb, c_old.astype(BF16)) + _dot(s.astype(BF16), v)
        den = ei * jnp.sum(q * n_old, axis=1, keepdims=True) + jnp.sum(s, axis=1, keepdims=True)
        hh = num / jnp.maximum(jnp.abs(den), jnp.exp(-mt))
        g_col = b_last - b_col + li_col
        bl = b_last + m_prev
        m_new = jnp.maximum(bl, jnp.max(g_col, axis=0, keepdims=True))
        wg = jnp.exp(g_col - m_new)
        decay = jnp.exp(bl - m_new)
        kw = k * wg
        c_scr[h] = decay * c_old + _dot(kw.T.astype(BF16), v)
        n_scr[h:h + 1, :] = decay * n_old + jnp.sum(kw, axis=0, keepdims=True)
        m_next = jnp.where(lane1 == h, m_new, m_next)
        ms = jnp.mean(hh * hh, axis=1, keepdims=True)
        hn = (hh * lax.rsqrt(ms + EPS)) * gm_ref[h:h + 1, :]
        og = _sigmoid(om_ref[:, h * DH_M:(h + 1) * DH_M])
        h_ref[:, h * DH_M:(h + 1) * DH_M] = (hn[:tb] * og).astype(BF16)

    m_scr[...] = m_next

    @pl.when(c_idx == nc - 1)
    def _():
        cout_ref[...] = c_scr[...]
        nout_ref[...] = n_scr[...]
        mout_ref[...] = m_scr[...]


def _mlstm(qk, vm, om, gt, w_conv, b_conv, b_gates_pad, g_mnorm, conv_buf, c0, n0, m0_pad, batch, seq):
    tb = min(seq, MLSTM_CHUNK)
    L = max(tb, LANES)
    nc = seq // tb
    tok = lambda w: pl.BlockSpec((None, tb, w), lambda b, c: (b * nc + c, 0, 0))
    chunks = lambda a: a.reshape(batch * nc, tb, a.shape[-1])
    const = lambda shape: pl.BlockSpec(shape, lambda b, c: (0,) * len(shape))
    per_b = lambda shape: pl.BlockSpec((None,) + shape, lambda b, c: (b,) + (0,) * len(shape))
    h, cst, c_out, n_out, m_out = pl.pallas_call(
        functools.partial(_mlstm_body, tb=tb, L=L, nc=nc),
        grid=(batch, nc),
        in_specs=[tok(2 * W_M), tok(W_M), tok(W_M), tok(LANES), const((CONV_W, 2 * W_M)), const((1, 2 * W_M)),
                  const((1, LANES)), const((H_M, DH_M)), per_b((CONV_W - 1, 2 * W_M)),
                  per_b((H_M, DH_M, DH_M)), per_b((H_M, DH_M)), per_b((1, LANES))],
        out_specs=[tok(W_M), per_b((CONV_W - 1, 2 * W_M)), per_b((H_M, DH_M, DH_M)), per_b((H_M, DH_M)),
                   per_b((1, LANES))],
        out_shape=[jax.ShapeDtypeStruct((batch * nc, tb, W_M), BF16),
                   jax.ShapeDtypeStruct((batch, CONV_W - 1, 2 * W_M), F32),
                   jax.ShapeDtypeStruct((batch, H_M, DH_M, DH_M), F32),
                   jax.ShapeDtypeStruct((batch, H_M, DH_M), F32),
                   jax.ShapeDtypeStruct((batch, 1, LANES), F32)],
        scratch_shapes=[pltpu.VMEM((8, 2 * W_M), F32), pltpu.VMEM((H_M, DH_M, DH_M), F32),
                        pltpu.VMEM((H_M, DH_M), F32), pltpu.VMEM((1, LANES), F32)],
        compiler_params=_params(("arbitrary", "arbitrary")),
        name="mlstm",
    )(chunks(qk), chunks(vm), chunks(om), chunks(gt), w_conv, b_conv, b_gates_pad, g_mnorm, conv_buf, c0, n0,
      m0_pad)
    return h.reshape(batch * seq, W_M), cst, c_out, n_out, m_out


def _merge_body(oa_ref, hm_ref, ga_ref, gb_ref, x_ref, gt1_ref, sc2_ref, sh2_ref, g2_ref, wa_ref, wb_ref, wo_ref,
                wr_ref, br_ref, *rest):
    x2_ref, h2_ref, route_ref, cnt_ref = rest[-4:]
    ya = _dot(oa_ref[...], wa_ref[...])
    yb = _dot(hm_ref[...], wb_ref[...])
    mix = _sigmoid(ga_ref[...]) * ya + _sigmoid(gb_ref[...]) * yb
    y = _dot(mix.astype(BF16), wo_ref[...])
    x2 = x_ref[...] + gt1_ref[...] * y
    x2_ref[...] = x2
    ms = jnp.mean(x2 * x2, axis=1, keepdims=True)
    h2 = (x2 * lax.rsqrt(ms + EPS)) * g2_ref[...] * (1.0 + sc2_ref[...]) + sh2_ref[...]
    h2b = h2.astype(BF16)
    h2_ref[...] = h2b

    tm = h2b.shape[0]
    logits_t = (_dot(h2b, wr_ref[...]) + br_ref[...]).T[:N_EXP, :]
    row = lax.broadcasted_iota(jnp.int32, (N_EXP, tm), 0)
    row_f = row.astype(F32)
    work = logits_t
    vals, hots = [], []
    for _ in range(TOP_K):
        mx = jnp.max(work, axis=0, keepdims=True)
        idx = jnp.min(jnp.where(work == mx, row, N_EXP), axis=0, keepdims=True)
        hot = row == idx
        vals.append(mx)
        hots.append(hot)
        work = jnp.where(hot, 2.0 * NEG_BIG, work)
    es = [jnp.exp(v - vals[0]) for v in vals]
    den = es[0]
    for e in es[1:]:
        den = den + e
    sel_t = jnp.zeros((N_EXP, tm), F32)
    for hot in hots:
        sel_t = jnp.where(hot, 1.0, sel_t)
    r_i = lax.broadcasted_iota(jnp.int32, (tm, tm), 0)
    c_i = lax.broadcasted_iota(jnp.int32, (tm, tm), 1)
    earlier = jnp.logical_and(r_i < c_i, r_i // TOK_TILE == c_i // TOK_TILE)
    rank_t = _dot(sel_t.astype(BF16), jnp.where(earlier, 1.0, 0.0).astype(BF16))
    sub = lax.broadcasted_iota(jnp.int32, (ROUTE_SEL, tm), 0)
    head = jnp.zeros((ROUTE_SEL, tm), F32)
    for k in range(TOP_K):
        e_k = jnp.sum(jnp.where(hots[k], row_f, 0.0), axis=0, keepdims=True)
        r_k = jnp.sum(jnp.where(hots[k], rank_t, 0.0), axis=0, keepdims=True)
        head = jnp.where(sub == k, e_k, head)
        head = jnp.where(sub == TOP_K + k, es[k] / den, head)
        head = jnp.where(sub == 2 * TOP_K + k, r_k, head)
    pad = jnp.zeros((LANES - ROUTE_SEL - N_EXP, tm), F32)
    route = jnp.concatenate([head, sel_t, pad], axis=0).T
    route_ref[...] = route
    for s in range(tm // TOK_TILE):
        cnt_ref[s] = jnp.sum(route[s * TOK_TILE:(s + 1) * TOK_TILE], axis=0, keepdims=True)


def _merge(oa, hm, ga, gb, x, gt1, sc2, sh2, g2, wa, wb, wo, wr, br, rows_per_mod, tile, row0, n_all, prev=None):
    n = x.shape[0]
    nt = n // tile
    nt_all = n_all // TOK_TILE
    tile0 = row0 // tile
    sub = tile // TOK_TILE
    tiles_per_group = nt // gt1.shape[0]
    tok = lambda w: pl.BlockSpec((tile, w), lambda i: (i, 0))
    mod = pl.BlockSpec((None, rows_per_mod, D_MODEL), lambda i: (i // tiles_per_group, 0, 0))
    res = lambda shape: pl.BlockSpec(shape, lambda i: (0, 0), pipeline_mode=pl.Buffered(1))
    out_tok = lambda w: pl.BlockSpec((tile, w), lambda i: (tile0 + i, 0))
    in_specs = [tok(W_VA), tok(W_M), tok(D_MODEL), tok(D_MODEL), tok(D_MODEL), mod, mod, mod,
                pl.BlockSpec((1, D_MODEL), lambda i: (0, 0)),
                res((W_VA, D_MODEL)), res((W_M, D_MODEL)), res((D_MODEL, D_MODEL)), res((D_MODEL, LANES)),
                pl.BlockSpec((1, LANES), lambda i: (0, 0))]
    args = [oa, hm, ga, gb, x, gt1, sc2, sh2, g2, wa, wb, wo, wr, br]
    aliases = {}
    if prev is not None:
        in_specs += [pl.BlockSpec(memory_space=pl.ANY)] * 4
        aliases = {len(args) + j: j for j in range(4)}
        args += list(prev)
    return pl.pallas_call(
        _merge_body,
        grid=(nt,),
        in_specs=in_specs,
        out_specs=[out_tok(D_MODEL), out_tok(D_MODEL), out_tok(LANES),
                   pl.BlockSpec((sub, 1, LANES), lambda i: (tile0 + i, 0, 0))],
        out_shape=[jax.ShapeDtypeStruct((n_all, D_MODEL), F32), jax.ShapeDtypeStruct((n_all, D_MODEL), BF16),
                   jax.ShapeDtypeStruct((n_all, LANES), F32), jax.ShapeDtypeStruct((nt_all, 1, LANES), F32)],
        input_output_aliases=aliases,
        compiler_params=_params(("arbitrary",)),
        name="merge",
    )(*args)


def _segment_copies(src, dst, sem, src_row, dst_row, n_groups, max_groups):
    out = []
    bit = 1
    while bit * 2 <= max_groups:
        bit *= 2
    while bit >= 1:
        off = (n_groups // (2 * bit)) * (2 * bit) * ROW_ALIGN
        rows = bit * ROW_ALIGN
        cp = pltpu.make_async_copy(src.at[pl.ds(pl.multiple_of(src_row + off, ROW_ALIGN), rows)],
                                   dst.at[pl.ds(pl.multiple_of(dst_row + off, ROW_ALIGN), rows)], sem)
        out.append(((n_groups // bit) % 2 == 1, cp))
        bit //= 2
    return out


def _run_copies(copies):
    for pred, cp in copies:
        pl.when(pred)(cp.start)
    for pred, cp in copies:
        pl.when(pred)(cp.wait)


def _slot_rows(route_t, loff_col, k):
    e_row = route_t[k:k + 1, :]
    r_row = route_t[2 * TOP_K + k:2 * TOP_K + k + 1, :]
    sub = lax.broadcasted_iota(jnp.int32, (LANES, route_t.shape[1]), 0).astype(F32)
    return jnp.sum(jnp.where(sub == e_row, loff_col, 0.0), axis=0, keepdims=True) + r_row


def _for_groups(n_groups, table_ref, base, make_copy, start):
    def body(i, carry):
        for j in range(GROUP_UNROLL):
            g = i * GROUP_UNROLL + j
            cp = make_copy(g, table_ref[base + g])
            if start:
                cp.start()
            else:
                cp.wait()
        return carry
    lax.fori_loop(0, (n_groups + GROUP_UNROLL - 1) // GROUP_UNROLL, body, 0)


def _dispatch_body(ng_ref, dt_ref, lo_ref, sv_ref, rm_ref, tail_ref, h2_ref, route_ref, soff_ref, xs_ref,
                   loc_scr, carry_scr, zero_scr, sem, *, nt):
    t = pl.program_id(0)
    buf = t % 2

    def copies_of(tt):
        def make(g, row):
            return pltpu.make_async_copy(
                loc_scr.at[tt % 2, pl.ds(pl.multiple_of(g * ROW_ALIGN, ROW_ALIGN), ROW_ALIGN)],
                xs_ref.at[pl.ds(pl.multiple_of(row, ROW_ALIGN), ROW_ALIGN)], sem.at[tt % 2])
        return ng_ref[tt], dt_ref, tt * GROUPS, make

    @pl.when(t == 0)
    def _():
        carry_scr[...] = jnp.zeros(carry_scr.shape, BF16)
        loc_scr[...] = jnp.zeros(loc_scr.shape, BF16)

    @pl.when(t >= 2)
    def _():
        _for_groups(*copies_of(t - 2), start=False)

    route_t = route_ref[...].T
    soff_col = soff_ref[...]
    slots = [_slot_rows(route_t, soff_col, k) for k in range(TOP_K)]

    def sort_rows(m):
        for lo in range(0, m, ONEHOT_SLAB):
            r_i = lax.broadcasted_iota(jnp.int32, (ONEHOT_SLAB, TOK_TILE), 0).astype(F32) + float(lo)
            onehot = jnp.zeros((ONEHOT_SLAB, TOK_TILE), F32)
            for k in range(TOP_K):
                onehot = jnp.where(r_i == slots[k], 1.0, onehot)
            loc_scr[buf, lo:lo + ONEHOT_SLAB, :] = _dot(onehot.astype(BF16), h2_ref[...]).astype(BF16)

    fits = ng_ref[t] * ROW_ALIGN <= M_LOC_SHORT
    pl.when(fits)(lambda: sort_rows(M_LOC_SHORT))
    pl.when(jnp.logical_not(fits))(lambda: sort_rows(M_LOC))

    for e in range(N_EXP):
        lo = pl.multiple_of(lo_ref[t * N_EXP + e], ROW_ALIGN)
        sv = pl.multiple_of(sv_ref[t * N_EXP + e], ROW_ALIGN)
        loc_scr[buf, pl.ds(lo, ROW_ALIGN), :] = loc_scr[buf, pl.ds(lo, ROW_ALIGN), :] + carry_scr[e]
        pending = loc_scr[buf, pl.ds(sv, ROW_ALIGN), :]
        carry_scr[e] = jnp.where(rm_ref[t * N_EXP + e] > 0, pending, jnp.zeros_like(pending))

    _for_groups(*copies_of(t), start=True)

    @pl.when(t == nt - 1)
    def _():
        if nt >= 2:
            _for_groups(*copies_of(t - 1), start=False)
        _for_groups(*copies_of(t), start=False)
        zero_scr[...] = jnp.zeros(zero_scr.shape, BF16)
        tails = []
        for e in range(N_EXP):
            tails += _segment_copies(zero_scr, xs_ref, sem.at[0], 0, tail_ref[e], tail_ref[N_EXP + e],
                                     FFN_TILE // ROW_ALIGN - 1)
        _run_copies(tails)


def _dispatch(meta, h2, route, soff_col):
    nt = h2.shape[0] // TOK_TILE
    grid_spec = pltpu.PrefetchScalarGridSpec(
        num_scalar_prefetch=6,
        grid=(nt,),
        in_specs=[pl.BlockSpec((TOK_TILE, D_MODEL), lambda t, *_: (t, 0)),
                  pl.BlockSpec((TOK_TILE, LANES), lambda t, *_: (t, 0)),
                  pl.BlockSpec((None, LANES, 1), lambda t, *_: (t, 0, 0))],
        out_specs=pl.BlockSpec(memory_space=pl.ANY),
        scratch_shapes=[pltpu.VMEM((2, M_LOC + ROW_ALIGN, D_MODEL), BF16),
                        pltpu.VMEM((N_EXP, ROW_ALIGN, D_MODEL), BF16),
                        pltpu.VMEM((FFN_TILE, D_MODEL), BF16), pltpu.SemaphoreType.DMA((2,))],
    )
    return pl.pallas_call(
        functools.partial(_dispatch_body, nt=nt),
        grid_spec=grid_spec,
        out_shape=jax.ShapeDtypeStruct((meta["rows"] + 2 * M_LOC, D_MODEL), BF16),
        compiler_params=_params(("arbitrary",)),
        name="moe_dispatch",
    )(meta["n_groups"], meta["dispatch_rows"], meta["slot_start"], meta["carry_start"], meta["carry_rows"],
      meta["tail"], h2, route, soff_col)


def _ffn_body(be_ref, nu_ref, nx_ref, sl_ref, hf_ref, x_ref, bgu_ref, bd_ref, wgu_hbm, wd_hbm, y_ref,
              wgu_f32, wd_f32, wgu_scr, wd_scr, sem):
    i = pl.program_id(0)
    expert = be_ref[i]
    slot = sl_ref[i]
    prev = be_ref[jnp.maximum(i - 1, 0)]

    def weight_copies(e, s):
        return (pltpu.make_async_copy(wgu_hbm.at[e], wgu_f32.at[s], sem.at[s]),
                pltpu.make_async_copy(wd_hbm.at[e], wd_f32.at[s], sem.at[s]))

    @pl.when(i == 0)
    def _():
        for cp in weight_copies(expert, slot):
            cp.start()

    @pl.when(jnp.logical_and(i < nu_ref[0], jnp.logical_or(i == 0, expert != prev)))
    def _():
        for cp in weight_copies(expert, slot):
            cp.wait()
        wgu_scr[...] = wgu_f32[slot].astype(BF16)
        wd_scr[...] = wd_f32[slot].astype(BF16)

        @pl.when(nx_ref[i] >= 0)
        def _():
            for cp in weight_copies(nx_ref[i], 1 - slot):
                cp.start()

    def expert_rows(m):
        gu = _dot(x_ref[0:m, :], wgu_scr[...]) + bgu_ref[...]
        gate = jnp.minimum(gu[:, :D_FF], SWIGLU_LIMIT)
        up = jnp.clip(gu[:, D_FF:], -SWIGLU_LIMIT, SWIGLU_LIMIT)
        act = (up + 1.0) * gate * _sigmoid(SWIGLU_ALPHA * gate)
        y_ref[0:m, :] = (_dot(act.astype(BF16), wd_scr[...]) + bd_ref[...]).astype(BF16)

    active = i < nu_ref[0]
    pl.when(jnp.logical_and(active, hf_ref[i] == 0))(lambda: expert_rows(FFN_TILE))
    pl.when(jnp.logical_and(active, hf_ref[i] != 0))(lambda: expert_rows(FFN_TILE // 2))


def _ffn(meta, xs, w_gu, b_gu, w_down, b_down):
    rows = meta["rows"]
    nblk = rows // FFN_TILE
    row_blk = pl.BlockSpec((FFN_TILE, D_MODEL), lambda i, be, nu, *_: (jnp.minimum(i, nu[0] - 1), 0))
    grid_spec = pltpu.PrefetchScalarGridSpec(
        num_scalar_prefetch=5,
        grid=(nblk,),
        in_specs=[row_blk,
                  pl.BlockSpec((None, 1, 2 * D_FF), lambda i, be, *_: (be[i], 0, 0)),
                  pl.BlockSpec((None, 1, D_MODEL), lambda i, be, *_: (be[i], 0, 0)),
                  pl.BlockSpec(memory_space=pl.ANY), pl.BlockSpec(memory_space=pl.ANY)],
        out_specs=row_blk,
        scratch_shapes=[pltpu.VMEM((2, D_MODEL, 2 * D_FF), F32), pltpu.VMEM((2, D_FF, D_MODEL), F32),
                        pltpu.VMEM((D_MODEL, 2 * D_FF), BF16), pltpu.VMEM((D_FF, D_MODEL), BF16),
                        pltpu.SemaphoreType.DMA((2,))],
    )
    return pl.pallas_call(
        _ffn_body,
        grid_spec=grid_spec,
        out_shape=jax.ShapeDtypeStruct((rows, D_MODEL), BF16),
        compiler_params=_params(("arbitrary",)),
        name="moe_ffn",
    )(meta["blk_exp"], meta["n_used"], meta["next_exp"], meta["blk_slot"], meta["blk_half"], xs,
      b_gu.reshape(N_EXP, 1, 2 * D_FF), b_down.reshape(N_EXP, 1, D_MODEL), w_gu, w_down)


def _combine_body(ng_ref, ct_ref, route_ref, soff_ref, x2_ref, gtp_ref, gts_ref, gf_ref, ys_ref,
                  yp_ref, ysm_ref, loc_scr, moe_scr, sem, *, nt, nt_prompt, final):
    t = pl.program_id(0)
    buf = t % 2

    def copies_of(tt):
        def make(g, row):
            return pltpu.make_async_copy(
                ys_ref.at[pl.ds(pl.multiple_of(row, ROW_ALIGN), ROW_ALIGN)],
                loc_scr.at[tt % 2, pl.ds(pl.multiple_of(g * ROW_ALIGN, ROW_ALIGN), ROW_ALIGN)], sem.at[tt % 2])
        return ng_ref[tt], ct_ref, tt * GROUPS, make

    @pl.when(t == 0)
    def _():
        loc_scr[...] = jnp.zeros(loc_scr.shape, BF16)
        _for_groups(*copies_of(0), start=True)

    @pl.when(t + 1 < nt)
    def _():
        _for_groups(*copies_of(t + 1), start=True)

    _for_groups(*copies_of(t), start=False)

    route = route_ref[...]
    soff_row = soff_ref[...]
    lane = lax.broadcasted_iota(jnp.int32, (TOK_TILE, LANES), 1).astype(F32)
    slots = [jnp.sum(jnp.where(lane == route[:, k:k + 1], soff_row, 0.0), axis=1, keepdims=True)
             + route[:, 2 * TOP_K + k:2 * TOP_K + k + 1] for k in range(TOP_K)]

    def weighted_sum(m):
        c_i = lax.broadcasted_iota(jnp.int32, (TOK_TILE, m), 1).astype(F32)
        wmat = jnp.zeros((TOK_TILE, m), F32)
        for k in range(TOP_K):
            wmat = jnp.where(c_i == slots[k], route[:, TOP_K + k:TOP_K + k + 1], wmat)
        moe_scr[...] = _dot(wmat.astype(BF16), loc_scr[buf, 0:m, :])

    fits = ng_ref[t] * ROW_ALIGN <= M_LOC_SHORT
    pl.when(fits)(lambda: weighted_sum(M_LOC_SHORT))
    pl.when(jnp.logical_not(fits))(lambda: weighted_sum(M_LOC))

    gate = jnp.where(t >= nt_prompt, gts_ref[...], gtp_ref[...])
    xo = x2_ref[...] + gate * moe_scr[...]
    if final:
        ms = jnp.mean(xo * xo, axis=1, keepdims=True)
        xo = (xo * lax.rsqrt(ms + EPS)) * gf_ref[...]

    @pl.when(t < nt_prompt)
    def _():
        yp_ref[...] = xo

    @pl.when(t >= nt_prompt)
    def _():
        ysm_ref[...] = xo


def _combine(meta, route, soff_row, x2, gt2_p, gt2_s, g_final, ys, n_prompt, n_sample, final):
    nt = x2.shape[0] // TOK_TILE
    nt_prompt = n_prompt // TOK_TILE
    tiles_per_batch = nt_prompt // gt2_p.shape[0]
    grid_spec = pltpu.PrefetchScalarGridSpec(
        num_scalar_prefetch=2,
        grid=(nt,),
        in_specs=[pl.BlockSpec((TOK_TILE, LANES), lambda t, *_: (t, 0)),
                  pl.BlockSpec((None, 1, LANES), lambda t, *_: (t, 0, 0)),
                  pl.BlockSpec((TOK_TILE, D_MODEL), lambda t, *_: (t, 0)),
                  pl.BlockSpec((None, 1, D_MODEL),
                               lambda t, *_: (jnp.minimum(t, nt_prompt - 1) // tiles_per_batch, 0, 0)),
                  pl.BlockSpec((TOK_TILE, D_MODEL), lambda t, *_: (0, 0)),
                  pl.BlockSpec((1, D_MODEL), lambda t, *_: (0, 0)),
                  pl.BlockSpec(memory_space=pl.ANY)],
        out_specs=[pl.BlockSpec((TOK_TILE, D_MODEL), lambda t, *_: (jnp.minimum(t, nt_prompt - 1), 0)),
                   pl.BlockSpec((TOK_TILE, D_MODEL), lambda t, *_: (0, 0))],
        scratch_shapes=[pltpu.VMEM((2, M_LOC, D_MODEL), BF16), pltpu.VMEM((TOK_TILE, D_MODEL), F32),
                        pltpu.SemaphoreType.DMA((2,))],
    )
    return pl.pallas_call(
        functools.partial(_combine_body, nt=nt, nt_prompt=nt_prompt, final=final),
        grid_spec=grid_spec,
        out_shape=[jax.ShapeDtypeStruct((n_prompt, D_MODEL), F32), jax.ShapeDtypeStruct((n_sample, D_MODEL), F32)],
        compiler_params=_params(("arbitrary",)),
        name="moe_combine",
    )(meta["n_groups"], meta["combine_rows"], route, soff_row, x2, gt2_p, gt2_s, g_final, ys)


def _moe_offsets(cnt):
    nt = cnt.shape[0]
    ra = ROW_ALIGN
    prefix = jnp.cumsum(cnt, axis=0) - cnt
    total = jnp.sum(cnt, axis=0)
    pending = prefix % ra
    used = pending + cnt
    seg = (used + ra - 1) // ra * ra
    lo = jnp.cumsum(seg, axis=1) - seg
    n_groups = jnp.sum(seg, axis=1) // ra
    gpad = (total + FFN_TILE - 1) // FFN_TILE * FFN_TILE
    gstart = jnp.cumsum(gpad) - gpad
    base = gstart[None, :] + prefix // ra * ra
    last = (jnp.arange(nt) == nt - 1)[:, None]
    n_write = jnp.where(last, seg // ra, used // ra)
    carry_start = lo + used // ra * ra
    carry_rows = jnp.where(last, 0, used % ra)
    g = jnp.arange(GROUPS)
    slot_end = (lo + seg) // ra
    owner = jnp.minimum(jnp.sum(g[None, :, None] >= slot_end[:, None, :], axis=2), N_EXP - 1)
    pick = lambda a: jnp.sum(jnp.where(owner[:, :, None] == jnp.arange(N_EXP), a[:, None, :], 0), axis=2)
    k = g[None, :] - pick(lo) // ra
    row = pick(base) + k * ra
    valid = g[None, :] < n_groups[:, None]
    rows = (nt * TOK_TILE * TOP_K + N_EXP * (FFN_TILE - 1) + FFN_TILE - 1) // FFN_TILE * FFN_TILE
    spare = rows + (jnp.arange(nt) % 2)[:, None] * M_LOC + g[None, :] * ra
    combine_rows = jnp.where(valid, row, 0)
    dispatch_rows = jnp.where(valid & (k < pick(n_write)), row, spare)

    nblk_e = gpad // FFN_TILE
    blk_end = jnp.cumsum(nblk_e)
    n_used = jnp.maximum(blk_end[-1], 1)
    blk = jnp.minimum(jnp.arange(rows // FFN_TILE, dtype=jnp.int32), n_used - 1)
    blk_exp = jnp.minimum(jnp.sum(blk[:, None] >= blk_end[None, :], axis=1), N_EXP - 1)
    experts = jnp.arange(N_EXP)
    following = lax.cummin(jnp.where(nblk_e > 0, experts, N_EXP), reverse=True)
    next_of = jnp.concatenate([following[1:], jnp.full((1,), N_EXP, following.dtype)])
    next_of = jnp.where(next_of >= N_EXP, -1, next_of)
    parity = (jnp.cumsum(nblk_e > 0) - 1) % 2
    pick_e = lambda a: jnp.sum(jnp.where(blk_exp[:, None] == experts[None, :], a[None, :], 0), axis=1)
    blk_rows = pick_e(total) - (blk - pick_e(blk_end - nblk_e)) * FFN_TILE
    blk_half = blk_rows <= FFN_TILE // 2
    total16 = (total + ra - 1) // ra * ra
    tail = jnp.concatenate([gstart + total16, (gpad - total16) // ra])
    i32 = lambda a: a.astype(jnp.int32).reshape(-1)
    return dict(n_groups=i32(n_groups), dispatch_rows=i32(dispatch_rows), combine_rows=i32(combine_rows),
                slot_start=i32(lo), carry_start=i32(carry_start), carry_rows=i32(carry_rows), tail=i32(tail),
                blk_exp=i32(blk_exp), n_used=i32(n_used), next_exp=i32(pick_e(next_of)), blk_slot=i32(pick_e(parity)),
                blk_half=i32(blk_half),
                slot_off=(lo + pending).astype(F32), rows=rows)


def _split_w_in(w_in):
    g0 = C_GATES
    return (w_in[:, :g0].astype(BF16), w_in[:, g0 + N_GATE:].astype(BF16),
            _pad_lanes(w_in[:, g0:g0 + N_GATE]).astype(BF16))


def _pad_lanes(a, value=0.0):
    return jnp.pad(a, [(0, 0)] * (a.ndim - 1) + [(0, LANES - a.shape[-1])], constant_values=value)


def kernel(x_prompt, x_sample, c_prompt, c_sample, cache_k, cache_v, state_conv, state_C, state_n, state_m, page_table, w_ada, b_ada, g_norm1, g_norm2, w_in, b_gates, lambda_q1, lambda_k1, lambda_q2, lambda_k2, g_subln, w_conv, b_conv, g_mnorm, w_up_a, w_up_b, w_out, w_router, b_router, w_gu, b_gu, w_down, b_down, g_final):
    B, S, D = x_prompt.shape
    Bd, Td, _ = x_sample.shape
    depth = w_in.shape[0]
    n_pool = cache_k.shape[1]
    past_len = page_table.shape[1] * PAGE_SIZE
    n_p, n_s = B * S, Bd * Td
    n_all = n_p + n_s
    assert D == D_MODEL and n_s == TOK_TILE and S % MLSTM_CHUNK == 0 and n_p % TOK_TILE == 0
    assert page_table.shape[1] % PAGES_PER_STEP == 0

    cos_p, sin_p = _rope_tables(np.arange(S))
    cos_s, sin_s = _rope_tables(np.tile(past_len + np.arange(Td), Bd))
    hp = x_prompt.reshape(n_p, D)
    hs = x_sample.reshape(n_s, D)
    c_all = jnp.concatenate([c_prompt, c_sample], axis=0)
    outs = [[] for _ in range(12)]

    for l in range(depth):
        lam_init = 0.8 - 0.6 * math.exp(-0.3 * l)
        mod = _ada(c_all, w_ada[l], b_ada[l])
        mods = [mod[:, j * D:(j + 1) * D] for j in range(6)]
        mp = [m[:B].reshape(B, 1, D) for m in mods]
        ms_ = [jnp.repeat(m[B:], Td, axis=0).reshape(1, n_s, D) for m in mods]
        w_parts = _split_w_in(w_in[l])
        g1 = g_norm1[l].reshape(1, D)
        lam_vec = jnp.stack([lambda_q1[l], lambda_k1[l], lambda_q2[l], lambda_k2[l]])
        gsub = g_subln[l].reshape(1, DV_A)
        bg = _pad_lanes(b_gates[l].reshape(1, N_GATE))
        cw, cb = w_conv[l], b_conv[l].reshape(1, 2 * W_M)

        (q_p, _, kb_p, v_p, _, qk_p, vm_p, om_p, ga_p, gb_p, gt_p, vt_p, kt_p) = _inproj(
            hp, mp[1], mp[0], g1, cos_p, sin_p, w_parts, 1, PROJ_TILE)
        (q_s, k_s, kb_s, v_s, vb_s, qk_s, vm_s, om_s, ga_s, gb_s, gt_s, _, _) = _inproj(
            hs, ms_[1], ms_[0], g1, cos_s, sin_s, w_parts, n_s, n_s)
        k_p = jnp.transpose(kt_p.reshape(B, H_A, 2, DK_A, S), (0, 4, 1, 2, 3))

        kt_pool = jnp.transpose(cache_k[l], (0, 2, 3, 4, 1)).reshape(n_pool, W_QA, PAGE_SIZE)
        v_pool = cache_v[l].reshape(n_pool, PAGE_SIZE * H_A, DV_A)
        oa_p, oa_s = _attention(q_p, kb_p, vt_p, q_s, kb_s, vb_s, kt_pool, v_pool, page_table, lam_vec, gsub,
                                B, S, lam_init, Td)

        zeros = lambda *shape: jnp.zeros(shape, F32)
        hm_p, cst_p, C_p, nn_p, m_p = _mlstm(qk_p, vm_p, om_p, gt_p, cw, cb, bg, g_mnorm[l],
                                             zeros(B, CONV_W - 1, 2 * W_M), zeros(B, H_M, DH_M, DH_M),
                                             zeros(B, H_M, DH_M), zeros(B, 1, LANES), B, S)
        hm_s, cst_s, C_s, nn_s, m_s = _mlstm(qk_s, vm_s, om_s, gt_s, cw, cb, bg, g_mnorm[l],
                                             state_conv[l], state_C[l], state_n[l],
                                             _pad_lanes(state_m[l]).reshape(Bd, 1, LANES), Bd, Td)

        wa, wb, wo = w_up_a[l].astype(BF16), w_up_b[l].astype(BF16), w_out[l].astype(BF16)
        wr = _pad_lanes(w_router[l]).astype(BF16)
        br = _pad_lanes(b_router[l].reshape(1, N_EXP))
        g2 = g_norm2[l].reshape(1, D)
        part = _merge(oa_p, hm_p, ga_p, gb_p, hp, mp[2], mp[4], mp[3], g2, wa, wb, wo, wr, br,
                      1, PROJ_TILE, 0, n_all)
        x2, h2, route, cnt = _merge(oa_s, hm_s, ga_s, gb_s, hs, ms_[2], ms_[4], ms_[3], g2, wa, wb, wo, wr, br,
                                    n_s, n_s, n_p, n_all, prev=part)

        meta = _moe_offsets(jnp.round(cnt[:, 0, ROUTE_SEL:ROUTE_SEL + N_EXP]).astype(jnp.int32))
        soff = _pad_lanes(meta["slot_off"])
        xs = _dispatch(meta, h2, route, soff[:, :, None])
        ys = _ffn(meta, xs, w_gu[l], b_gu[l], w_down[l], b_down[l])
        final = l == depth - 1
        hp, hs = _combine(meta, route, soff[:, None, :], x2, mp[5], ms_[5][0], g_final.reshape(1, D), ys,
                          n_p, n_s, final)

        for j, a in enumerate([k_p, v_p.reshape(B, S, H_A, DV_A), cst_p, C_p, nn_p,
                               m_p[:, 0, :H_M],
                               k_s.reshape(Bd, Td, H_A, 2, DK_A), v_s.reshape(Bd, Td, H_A, DV_A), cst_s, C_s, nn_s,
                               m_s[:, 0, :H_M]]):
            outs[j].append(a)

    return (hp.reshape(B, S, D), hs.reshape(Bd, Td, D)) + tuple(jnp.stack(o) for o in outs)
```

```python
import functools
import math

import numpy as np
import jax
import jax.numpy as jnp
from jax import lax
from jax.experimental import pallas as pl
from jax.experimental.pallas import tpu as pltpu

F32 = jnp.float32
BF16 = jnp.bfloat16

D_MODEL = 1024
H_A = 4
DK_A = 64
DV_A = 2 * DK_A
ROPE_THETA = 10000.0
H_M = 4
DH_M = 128
CONV_W = 4
N_EXP = 32
TOP_K = 4
D_FF = D_MODEL
SWIGLU_LIMIT = 7.0
SWIGLU_ALPHA = 1.702
EPS = 1e-6
PAGE_SIZE = 128

W_QA = H_A * 2 * DK_A
W_VA = H_A * DV_A
W_M = H_M * DH_M
N_GATE = 2 * H_M

LANES = 128
ROW_ALIGN = 16
TOK_TILE = 256
PROJ_TILE = 512
FFN_TILE = 512
ATT_TILE = 256
ATT_HEADS = H_A
MLSTM_CHUNK = 256
PAGES_PER_STEP = 16
CHUNKS_PER_STEP = 2
PAGE_BUFS = 3
NEG_BIG = -1e30
LOG2_E = math.log2(math.e)
VMEM_LIMIT = 56 * 1024 * 1024

C_QA, C_KA, C_VA, C_QK, C_VM, C_OM, C_GATES = 0, 512, 1024, 1536, 2560, 3072, 3584
M_LOC = ((TOK_TILE * TOP_K + 2 * N_EXP * (ROW_ALIGN - 1)) + 255) // 256 * 256
GROUPS = M_LOC // ROW_ALIGN
M_LOC_SHORT = TOK_TILE * TOP_K + N_EXP * ROW_ALIGN
ROUTE_SEL = 32
GROUP_UNROLL = 4


def _dot(a, b):
    return jnp.dot(a, b, preferred_element_type=F32)


def _dot_nt(a, b):
    return lax.dot_general(a, b, (((1,), (1,)), ((), ())), preferred_element_type=F32)


def _sigmoid(x):
    return 0.5 * jnp.tanh(0.5 * x) + 0.5


def _start_all(copies):
    for cp in copies:
        cp.start()


def _params(sem):
    return pltpu.CompilerParams(dimension_semantics=sem, vmem_limit_bytes=VMEM_LIMIT)


def _ada_body(c_ref, w_ref, b_ref, o_ref):
    c = c_ref[...]
    s = c * _sigmoid(c)
    s_hi = s.astype(BF16)
    s_lo = (s - s_hi.astype(F32)).astype(BF16)
    w = w_ref[...]
    w_hi = w.astype(BF16)
    w_lo = (w - w_hi.astype(F32)).astype(BF16)
    o_ref[...] = _dot(s_hi, w_hi) + _dot(s_lo, w_hi) + _dot(s_hi, w_lo) + b_ref[...]


def _ada(c_all, w_ada, b_ada):
    rows = c_all.shape[0]
    n_out = w_ada.shape[1]
    blk = D_MODEL
    return pl.pallas_call(
        _ada_body,
        grid=(n_out // blk,),
        in_specs=[pl.BlockSpec((rows, D_MODEL), lambda j: (0, 0)),
                  pl.BlockSpec((D_MODEL, blk), lambda j: (0, j)),
                  pl.BlockSpec((1, blk), lambda j: (0, j))],
        out_specs=pl.BlockSpec((rows, blk), lambda j: (0, j)),
        out_shape=jax.ShapeDtypeStruct((rows, n_out), F32),
        compiler_params=_params(("arbitrary",)),
        name="ada",
    )(c_all, w_ada, b_ada.reshape(1, n_out))


def _rope(z, cos, sin):
    lane = lax.broadcasted_iota(jnp.int32, (z.shape[0], LANES), 1)
    first_half = (lane % DK_A) < (DK_A // 2)
    out = []
    for h in range(H_A):
        xh = z[:, h * LANES:(h + 1) * LANES]
        partner = jnp.where(first_half, pltpu.roll(xh, LANES - DK_A // 2, 1), pltpu.roll(xh, DK_A // 2, 1))
        out.append(xh * cos + partner * sin)
    return jnp.concatenate(out, axis=1)


def _conv_silu(u, ext, cw_ref, cb_ref):
    rows = u.shape[0]
    full = jnp.concatenate([ext, u], axis=0)
    conv = cb_ref[...] + cw_ref[CONV_W - 1:CONV_W, :] * u
    for j in range(CONV_W - 1):
        conv = conv + cw_ref[j:j + 1, :] * pltpu.roll(full, CONV_W - 1 - j, 0)[8:8 + rows]
    return conv * _sigmoid(conv)


def _inproj_body(x_ref, sc_ref, sh_ref, g_ref, cos_ref, sin_ref, w_ref, wg_ref, wt_ref,
                 q_ref, k_ref, kb_ref, v_ref, vb_ref, qk_ref, vm_ref, om_ref, ga_ref, gb_ref, gt_ref, vt_ref, kt_ref):
    x = x_ref[...]
    ms = jnp.mean(x * x, axis=1, keepdims=True)
    h = (x * lax.rsqrt(ms + EPS)) * g_ref[...] * (1.0 + sc_ref[...]) + sh_ref[...]
    hb = h.astype(BF16)
    cos = cos_ref[...]
    sin = sin_ref[...]

    def seg(lo, n):
        return _dot(hb, w_ref[:, lo:lo + n])

    q = _rope(seg(C_QA, W_QA), cos, sin) * (DK_A ** -0.5 * LOG2_E)
    q_ref[...] = q.astype(BF16)
    k = _rope(seg(C_KA, W_QA), cos, sin)
    k_ref[...] = k
    kb_ref[...] = k.astype(BF16)
    kt_ref[...] = k.T
    v = seg(C_VA, W_VA)
    for h in range(H_A):
        v_ref[pl.ds(h, v.shape[0], stride=H_A), :] = v[:, h * DV_A:(h + 1) * DV_A]
    vb_ref[...] = v.astype(BF16)
    for s in range(vt_ref.shape[0]):
        vt_ref[s] = v[s * ATT_TILE:(s + 1) * ATT_TILE].T.astype(BF16)
    qk_ref[...] = seg(C_QK, 2 * W_M)
    vm_ref[...] = seg(C_VM, W_M).astype(BF16)
    om_ref[...] = seg(C_OM, W_M)
    ga_ref[...] = _dot(hb, wg_ref[:, :D_MODEL])
    gb_ref[...] = _dot(hb, wg_ref[:, D_MODEL:])
    gt_ref[...] = _dot(hb, wt_ref[...])


def _inproj(x, sc, sh, g1, cos, sin, w_parts, rows_per_mod, tile):
    n = x.shape[0]
    nt = n // tile
    tiles_per_group = nt // sc.shape[0]
    tab_tiles = cos.shape[0] // tile
    slabs = tile // ATT_TILE
    tok = lambda w: pl.BlockSpec((tile, w), lambda i: (i, 0))
    mod = pl.BlockSpec((None, rows_per_mod, D_MODEL), lambda i: (i // tiles_per_group, 0, 0))
    tab = pl.BlockSpec((tile, LANES), lambda i: (i % tab_tiles, 0))
    tok_out = lambda w, dt: (tok(w), jax.ShapeDtypeStruct((n, w), dt))
    outs = [
        tok_out(W_QA, BF16),
        tok_out(W_QA, F32), tok_out(W_QA, BF16),
        (pl.BlockSpec((tile * H_A, DV_A), lambda i: (i, 0)), jax.ShapeDtypeStruct((n * H_A, DV_A), F32)),
        tok_out(W_VA, BF16),
        tok_out(2 * W_M, F32), tok_out(W_M, BF16), tok_out(W_M, F32),
        tok_out(D_MODEL, F32), tok_out(D_MODEL, F32), tok_out(LANES, F32),
        (pl.BlockSpec((slabs, W_VA, ATT_TILE), lambda i: (i, 0, 0)),
         jax.ShapeDtypeStruct((n // ATT_TILE, W_VA, ATT_TILE), BF16)),
        (pl.BlockSpec((None, W_QA, tile), lambda i: (i // tab_tiles, 0, i % tab_tiles)),
         jax.ShapeDtypeStruct((nt // tab_tiles, W_QA, tab_tiles * tile), F32)),
    ]
    return pl.pallas_call(
        _inproj_body,
        grid=(nt,),
        in_specs=[tok(D_MODEL), mod, mod, pl.BlockSpec((1, D_MODEL), lambda i: (0, 0)), tab, tab,
                  *[pl.BlockSpec(w.shape, lambda i: (0, 0), pipeline_mode=pl.Buffered(1)) for w in w_parts]],
        out_specs=[spec for spec, _ in outs],
        out_shape=[shape for _, shape in outs],
        compiler_params=_params(("arbitrary",)),
        name="inproj",
    )(x, sc, sh, g1, cos, sin, *w_parts)


def _rope_tables(pos):
    half = DK_A // 2
    inv = ROPE_THETA ** (-np.arange(half, dtype=np.float64) * 2.0 / DK_A)
    ang = np.asarray(pos, np.float64)[:, None] * inv[None, :]
    cos = np.cos(ang)
    sin = np.sin(ang)
    cos64 = np.concatenate([cos, cos], axis=1)
    sin64 = np.concatenate([-sin, sin], axis=1)
    return (jnp.asarray(np.tile(cos64, (1, LANES // DK_A)), F32),
            jnp.asarray(np.tile(sin64, (1, LANES // DK_A)), F32))


def _lambda_value(lam_ref, lam_init):
    lv = lam_ref[...]
    l1 = jnp.sum(lv[0:1, :] * lv[1:2, :], axis=1, keepdims=True)
    l2 = jnp.sum(lv[2:3, :] * lv[3:4, :], axis=1, keepdims=True)
    return jnp.exp(l1) - jnp.exp(l2) + lam_init


def _subln(o, g, lam_init):
    ms = jnp.mean(o * o, axis=1, keepdims=True)
    return (o * lax.rsqrt(ms + EPS)) * g * (1.0 - lam_init)


def _prompt_tile(q_ref, k_ref, vt_ref, lam, g_ref, o_ref, m_scr, acc_scr, i, lam_init):
    tq = ATT_TILE
    lane = lax.broadcasted_iota(jnp.int32, (tq, LANES), 1)
    qs = []
    for hh in range(ATT_HEADS):
        q = q_ref[:, hh * LANES:(hh + 1) * LANES]
        zero = jnp.zeros_like(q)
        qs.append(jnp.concatenate([jnp.where(lane < DK_A, q, zero), jnp.where(lane >= DK_A, q, zero)], axis=0))
    for hh in range(ATT_HEADS):
        m_scr[hh][...] = jnp.full(m_scr[hh].shape, NEG_BIG, F32)
        acc_scr[hh][...] = jnp.zeros(acc_scr[hh].shape, F32)
    ones = jnp.ones((ROW_ALIGN, tq), BF16)

    def scores(hh, j):
        start = pl.multiple_of(j * tq, tq)
        return _dot_nt(k_ref[pl.ds(start, tq), hh * LANES:(hh + 1) * LANES], qs[hh])

    def update_all(j, mask):
        sts = [scores(hh, j) for hh in range(ATT_HEADS)]
        if mask is not None:
            sts = [jnp.where(mask, st, NEG_BIG) for st in sts]
        pts, alphas = [], []
        for hh in range(ATT_HEADS):
            m_old = m_scr[hh][...]
            m_new = jnp.maximum(m_old, jnp.max(sts[hh], axis=0, keepdims=True))
            alphas.append(jnp.exp2(m_old - m_new))
            pts.append(jnp.exp2(sts[hh] - m_new).astype(BF16))
            m_scr[hh][...] = m_new
        for hh in range(ATT_HEADS):
            vt = jnp.concatenate([vt_ref[j, hh * LANES:(hh + 1) * LANES, :], ones], axis=0)
            acc_scr[hh][...] = alphas[hh] * acc_scr[hh][...] + _dot(vt, pts[hh])

    def off_diag(j, carry):
        update_all(j, None)
        return carry

    lax.fori_loop(0, i, off_diag, 0)
    key = lax.broadcasted_iota(jnp.int32, (tq, 2 * tq), 0)
    qry = lax.broadcasted_iota(jnp.int32, (tq, 2 * tq), 1) % tq
    update_all(i, key <= qry)
    for hh in range(ATT_HEADS):
        acc = acc_scr[hh][...]
        ot = acc[:DV_A] / acc[DV_A:DV_A + 1]
        at = ot[:, :tq] - lam * ot[:, tq:]
        ms = jnp.mean(at * at, axis=0, keepdims=True)
        at = (at * lax.rsqrt(ms + EPS)) * g_ref[...] * (1.0 - lam_init)
        o_ref[:, hh * LANES:(hh + 1) * LANES] = at.T.astype(BF16)


def _sample_chunks(pt_ref, q_ref, kn_ref, vn_ref, lam, g_ref, kpool_ref, vpool_ref, o_ref,
                   kbuf, vbuf, sem, m_scr, l_scr, acc_scr, step, parts, *, lam_init, n_chunks, n_seq, t_new):
    steps_per_seq = n_chunks // CHUNKS_PER_STEP
    b = step // steps_per_seq
    c0 = (step % steps_per_seq) * CHUNKS_PER_STEP
    opens = 0 in parts
    closes = CHUNKS_PER_STEP - 1 in parts

    def chunk_copies(bb, c, slot):
        out = []
        for j in range(PAGES_PER_STEP):
            page = pt_ref[bb, c * PAGES_PER_STEP + j]
            out.append(pltpu.make_async_copy(kpool_ref.at[page], kbuf.at[slot, j], sem.at[slot]))
            out.append(pltpu.make_async_copy(vpool_ref.at[page], vbuf.at[slot, j], sem.at[slot]))
        return out

    if opens:
        @pl.when(step == 0)
        def _():
            for g in range(PAGE_BUFS):
                _start_all(chunk_copies(0, g, g))

    q = q_ref[...].astype(F32)
    qt = jnp.concatenate([q] * (2 * H_A), axis=0)
    row = lax.broadcasted_iota(jnp.int32, qt.shape, 0)
    col = lax.broadcasted_iota(jnp.int32, qt.shape, 1)
    qbd = jnp.where(col // DK_A == row // t_new, qt, 0.0).astype(BF16)

    if opens:
        @pl.when(c0 == 0)
        def _():
            m_scr[...] = jnp.full(m_scr.shape, NEG_BIG, F32)
            l_scr[...] = jnp.zeros(l_scr.shape, F32)
            acc_scr[...] = jnp.zeros(acc_scr.shape, F32)

    rows_h = 2 * t_new

    def update(s, v):
        m_old = m_scr[...]
        m_new = jnp.maximum(m_old, jnp.max(s, axis=1, keepdims=True))
        alpha = jnp.exp2(m_old - m_new)
        p = jnp.exp2(s - m_new)
        l_scr[...] = alpha * l_scr[...] + jnp.sum(p, axis=1, keepdims=True)
        pv = _dot(p.astype(BF16), v)
        own = [pv[h * rows_h:(h + 1) * rows_h, h * DV_A:(h + 1) * DV_A] for h in range(H_A)]
        acc_scr[...] = alpha * acc_scr[...] + jnp.concatenate(own, axis=0)
        m_scr[...] = m_new

    def chunk(c, slot):
        for cp in chunk_copies(b, c, slot):
            cp.wait()
        kt = jnp.concatenate([kbuf[slot, j].astype(BF16) for j in range(PAGES_PER_STEP)], axis=1)

        def page_v(j):
            return jnp.concatenate([vbuf.at[slot, j][pl.ds(h, PAGE_SIZE, stride=H_A), :].astype(BF16)
                                    for h in range(H_A)], axis=1)

        v = jnp.concatenate([page_v(j) for j in range(PAGES_PER_STEP)], axis=0)
        update(_dot(qbd, kt), v)

        @pl.when(c + PAGE_BUFS < n_chunks)
        def _():
            _start_all(chunk_copies(b, c + PAGE_BUFS, slot))

        @pl.when(jnp.logical_and(c + PAGE_BUFS >= n_chunks, b + 1 < n_seq))
        def _():
            _start_all(chunk_copies(b + 1, c + PAGE_BUFS - n_chunks, slot))

    for j in parts:
        chunk(c0 + j, (step * CHUNKS_PER_STEP + j) % PAGE_BUFS)

    def finish():
        zpad = jnp.zeros((PAGE_SIZE - t_new, W_QA), F32)
        kn = jnp.concatenate([kn_ref[...].astype(F32), zpad], axis=0).astype(BF16)
        vn = jnp.concatenate([vn_ref[...].astype(F32), zpad], axis=0).astype(BF16)
        s = _dot_nt(qbd, kn)
        row = lax.broadcasted_iota(jnp.int32, s.shape, 0) % t_new
        col = lax.broadcasted_iota(jnp.int32, s.shape, 1)
        update(jnp.where(col <= row, s, NEG_BIG), vn)
        o = acc_scr[...] / l_scr[...]
        outs = []
        for h in range(H_A):
            r0 = h * rows_h
            outs.append(_subln(o[r0:r0 + t_new] - lam * o[r0 + t_new:r0 + rows_h], g_ref[...], lam_init))
        o_ref[...] = jnp.concatenate(outs, axis=1).astype(BF16)

    if closes:
        pl.when(c0 + CHUNKS_PER_STEP == n_chunks)(finish)


def _attention_body(pt_ref, q_ref, k_ref, vt_ref, lam_ref, gcol_ref, qs_ref, kn_ref, vn_ref, grow_ref,
                    kpool_ref, vpool_ref, o_ref, os_ref, *scratch, lam_init, n_chunks, n_seq, t_new):
    m_scr, acc_scr = scratch[:ATT_HEADS], scratch[ATT_HEADS:2 * ATT_HEADS]
    kbuf, vbuf, sem, ms_scr, ls_scr, accs_scr = scratch[2 * ATT_HEADS:]
    step = pl.program_id(0) * pl.num_programs(1) + pl.program_id(1)
    lam = _lambda_value(lam_ref, lam_init)
    sample = functools.partial(_sample_chunks, pt_ref, qs_ref, kn_ref, vn_ref, lam, grow_ref, kpool_ref, vpool_ref,
                               os_ref, kbuf, vbuf, sem, ms_scr, ls_scr, accs_scr, step,
                               lam_init=lam_init, n_chunks=n_chunks, n_seq=n_seq, t_new=t_new)
    sample((0,))
    _prompt_tile(q_ref, k_ref, vt_ref, lam, gcol_ref, o_ref, m_scr, acc_scr, pl.program_id(1), lam_init)
    sample((1,))


def _attention(q, k, vt, q_s, k_new, v_new, cache_k, cache_v, page_table, lam_vec, g_subln, batch, seq,
               lam_init, t_new):
    nq = seq // ATT_TILE
    bd, n_pages = page_table.shape
    n_chunks = n_pages // PAGES_PER_STEP
    steps_per_seq = n_chunks // CHUNKS_PER_STEP
    assert ATT_HEADS == H_A and CHUNKS_PER_STEP % 2 == 0 and n_chunks % CHUNKS_PER_STEP == 0
    assert batch * nq == bd * steps_per_seq and n_chunks >= PAGE_BUFS
    n_rows = 2 * H_A * t_new
    kv = pl.BlockSpec((seq, W_QA), lambda b, i, pt: (b, 0))
    vts = pl.BlockSpec((nq, W_VA, ATT_TILE), lambda b, i, pt: (b, 0, 0))
    qo = pl.BlockSpec((ATT_TILE, W_QA), lambda b, i, pt: (b * nq + i, 0))
    new = pl.BlockSpec((None, t_new, W_QA), lambda b, i, pt: ((b * nq + i) // steps_per_seq, 0, 0))
    const = lambda shape: pl.BlockSpec(shape, lambda b, i, pt: (0, 0))
    page_buf = pltpu.VMEM((PAGE_BUFS, PAGES_PER_STEP, W_QA, PAGE_SIZE), F32)
    grid_spec = pltpu.PrefetchScalarGridSpec(
        num_scalar_prefetch=1,
        grid=(batch, nq),
        in_specs=[qo, kv, vts, const((4, DK_A)), const((DV_A, 1)), new, new, new, const((1, DV_A)),
                  pl.BlockSpec(memory_space=pl.ANY), pl.BlockSpec(memory_space=pl.ANY)],
        out_specs=[qo, new],
        scratch_shapes=[pltpu.VMEM((1, 2 * ATT_TILE), F32)] * ATT_HEADS
                       + [pltpu.VMEM((DV_A + ROW_ALIGN, 2 * ATT_TILE), F32)] * ATT_HEADS
                       + [page_buf, page_buf, pltpu.SemaphoreType.DMA((PAGE_BUFS,)), pltpu.VMEM((n_rows, 1), F32),
                          pltpu.VMEM((n_rows, 1), F32), pltpu.VMEM((n_rows, DV_A), F32)],
    )
    o_p, o_s = pl.pallas_call(
        functools.partial(_attention_body, lam_init=lam_init, n_chunks=n_chunks, n_seq=bd, t_new=t_new),
        grid_spec=grid_spec,
        out_shape=[jax.ShapeDtypeStruct((batch * seq, W_VA), BF16), jax.ShapeDtypeStruct((bd, t_new, W_VA), BF16)],
        compiler_params=_params(("arbitrary", "arbitrary")),
        name="attention",
    )(page_table, q, k, vt, lam_vec, g_subln.reshape(DV_A, 1), q_s.reshape(bd, t_new, W_QA),
      k_new.reshape(bd, t_new, W_QA), v_new.reshape(bd, t_new, W_VA), g_subln, cache_k, cache_v)
    return o_p, o_s.reshape(bd * t_new, W_VA)


def _mlstm_body(qk_ref, vm_ref, om_ref, gt_ref, cw_ref, cb_ref, bg_ref, gm_ref, cbuf_ref, c0_ref, n0_ref, m0_ref,
                h_ref, cst_ref, cout_ref, nout_ref, mout_ref, ext_scr, c_scr, n_scr, m_scr, *, tb, L, nc):
    c_idx = pl.program_id(1)

    @pl.when(c_idx == 0)
    def _():
        ext_scr[...] = jnp.zeros(ext_scr.shape, F32)
        ext_scr[8 - (CONV_W - 1):8, :] = cbuf_ref[...]
        c_scr[...] = c0_ref[...]
        n_scr[...] = n0_ref[...]
        m_scr[...] = m0_ref[...]

    pad = L - tb
    u = qk_ref[...]
    if pad:
        u = jnp.concatenate([u, jnp.zeros((pad, u.shape[1]), F32)], axis=0)
    a = _conv_silu(u, ext_scr[...], cw_ref, cb_ref)
    if not pad:
        ext_scr[...] = u[L - 8:L]

    @pl.when(c_idx == nc - 1)
    def _():
        cst_ref[...] = qk_ref[tb - (CONV_W - 1):tb, :]

    g = gt_ref[...] + bg_ref[...]
    li = g
    lf = jnp.minimum(g, 0.0) - jnp.log1p(jnp.exp(-jnp.abs(g)))
    if pad:
        zpad = jnp.zeros((pad, LANES), F32)
        li = jnp.concatenate([li, zpad + NEG_BIG], axis=0)
        lf = jnp.concatenate([lf, zpad], axis=0)
    row = lax.broadcasted_iota(jnp.int32, (L, LANES), 0)
    lane = lax.broadcasted_iota(jnp.int32, (L, LANES), 1)
    bcum = lf
    shift = 1
    while shift < L:
        bcum = bcum + jnp.where(row >= shift, pltpu.roll(bcum, shift, 0), 0.0)
        shift *= 2
    gates = jnp.where(lane < H_M, li, bcum)
    gates_t = gates.T
    tri = lax.broadcasted_iota(jnp.int32, (L, L), 0) >= lax.broadcasted_iota(jnp.int32, (L, L), 1)
    m_all = m_scr[...]
    lane1 = lax.broadcasted_iota(jnp.int32, (1, LANES), 1)
    m_next = m_all
    vall = vm_ref[...]
    if pad:
        vall = jnp.concatenate([vall, jnp.zeros((pad, vall.shape[1]), BF16)], axis=0)

    for h in range(H_M):
        li_col = gates[:, h:h + 1]
        b_col = gates[:, H_M + h:H_M + h + 1]
        src_row = gates_t[h:h + 1, :] - gates_t[H_M + h:H_M + h + 1, :]
        m_prev = m_all[:, h:h + 1]
        b_last = b_col[L - 1:L, :]
        log_d = jnp.where(tri, b_col + src_row, NEG_BIG)
        inter = b_col + m_prev
        mt = jnp.maximum(inter, jnp.max(log_d, axis=1, keepdims=True))
        q = a[:, h * DH_M:(h + 1) * DH_M]
        k = a[:, W_M + h * DH_M:W_M + (h + 1) * DH_M] * (DH_M ** -0.5)
        v = vall[:, h * DH_M:(h + 1) * DH_M]
        qb = q.astype(BF16)
        s = _dot_nt(qb, k.astype(BF16)) * jnp.exp(log_d - mt)
        ei = jnp.exp(inter - mt)
        c_old = c_scr[h]
        n_old = n_scr[h:h + 1, :]
        num = ei * _dot(qb, c_old.astype(BF16)) + _dot(s.astype(BF16), v)
        den = ei * jnp.sum(q * n_old, axis=1, keepdims=True) + jnp.sum(s, axis=1, keepdims=True)
        hh = num / jnp.maximum(jnp.abs(den), jnp.exp(-mt))
        g_col = b_last - b_col + li_col
        bl = b_last + m_prev
        m_new = jnp.maximum(bl, jnp.max(g_col, axis=0, keepdims=True))
        wg = jnp.exp(g_col - m_new)
        decay = jnp.exp(bl - m_new)
        kw = k * wg
        c_scr[h] = decay * c_old + _dot(kw.T.astype(BF16), v)
        n_scr[h:h + 1, :] = decay * n_old + jnp.sum(kw, axis=0, keepdims=True)
        m_next = jnp.where(lane1 == h, m_new, m_next)
        ms = jnp.mean(hh * hh, axis=1, keepdims=True)
        hn = (hh * lax.rsqrt(ms + EPS)) * gm_ref[h:h + 1, :]
        og = _sigmoid(om_ref[:, h * DH_M:(h + 1) * DH_M])
        h_ref[:, h * DH_M:(h + 1) * DH_M] = (hn[:tb] * og).astype(BF16)

    m_scr[...] = m_next

    @pl.when(c_idx == nc - 1)
    def _():
        cout_ref[...] = c_scr[...]
        nout_ref[...] = n_scr[...]
        mout_ref[...] = m_scr[...]


def _mlstm(qk, vm, om, gt, w_conv, b_conv, b_gates_pad, g_mnorm, conv_buf, c0, n0, m0_pad, batch, seq):
    tb = min(seq, MLSTM_CHUNK)
    L = max(tb, LANES)
    nc = seq // tb
    tok = lambda w: pl.BlockSpec((None, tb, w), lambda b, c: (b * nc + c, 0, 0))
    chunks = lambda a: a.reshape(batch * nc, tb, a.shape[-1])
    const = lambda shape: pl.BlockSpec(shape, lambda b, c: (0,) * len(shape))
    per_b = lambda shape: pl.BlockSpec((None,) + shape, lambda b, c: (b,) + (0,) * len(shape))
    h, cst, c_out, n_out, m_out = pl.pallas_call(
        functools.partial(_mlstm_body, tb=tb, L=L, nc=nc),
        grid=(batch, nc),
        in_specs=[tok(2 * W_M), tok(W_M), tok(W_M), tok(LANES), const((CONV_W, 2 * W_M)), const((1, 2 * W_M)),
                  const((1, LANES)), const((H_M, DH_M)), per_b((CONV_W - 1, 2 * W_M)),
                  per_b((H_M, DH_M, DH_M)), per_b((H_M, DH_M)), per_b((1, LANES))],
        out_specs=[tok(W_M), per_b((CONV_W - 1, 2 * W_M)), per_b((H_M, DH_M, DH_M)), per_b((H_M, DH_M)),
                   per_b((1, LANES))],
        out_shape=[jax.ShapeDtypeStruct((batch * nc, tb, W_M), BF16),
                   jax.ShapeDtypeStruct((batch, CONV_W - 1, 2 * W_M), F32),
                   jax.ShapeDtypeStruct((batch, H_M, DH_M, DH_M), F32),
                   jax.ShapeDtypeStruct((batch, H_M, DH_M), F32),
                   jax.ShapeDtypeStruct((batch, 1, LANES), F32)],
        scratch_shapes=[pltpu.VMEM((8, 2 * W_M), F32), pltpu.VMEM((H_M, DH_M, DH_M), F32),
                        pltpu.VMEM((H_M, DH_M), F32), pltpu.VMEM((1, LANES), F32)],
        compiler_params=_params(("arbitrary", "arbitrary")),
        name="mlstm",
    )(chunks(qk), chunks(vm), chunks(om), chunks(gt), w_conv, b_conv, b_gates_pad, g_mnorm, conv_buf, c0, n0,
      m0_pad)
    return h.reshape(batch * seq, W_M), cst, c_out, n_out, m_out


def _merge_body(oa_ref, hm_ref, ga_ref, gb_ref, x_ref, gt1_ref, sc2_ref, sh2_ref, g2_ref, wa_ref, wb_ref, wo_ref,
                wr_ref, br_ref, *rest):
    x2_ref, h2_ref, route_ref, cnt_ref = rest[-4:]
    ya = _dot(oa_ref[...], wa_ref[...])
    yb = _dot(hm_ref[...], wb_ref[...])
    mix = _sigmoid(ga_ref[...]) * ya + _sigmoid(gb_ref[...]) * yb
    y = _dot(mix.astype(BF16), wo_ref[...])
    x2 = x_ref[...] + gt1_ref[...] * y
    x2_ref[...] = x2
    ms = jnp.mean(x2 * x2, axis=1, keepdims=True)
    h2 = (x2 * lax.rsqrt(ms + EPS)) * g2_ref[...] * (1.0 + sc2_ref[...]) + sh2_ref[...]
    h2b = h2.astype(BF16)
    h2_ref[...] = h2b

    tm = h2b.shape[0]
    logits_t = (_dot(h2b, wr_ref[...]) + br_ref[...]).T[:N_EXP, :]
    row = lax.broadcasted_iota(jnp.int32, (N_EXP, tm), 0)
    row_f = row.astype(F32)
    work = logits_t
    vals, hots = [], []
    for _ in range(TOP_K):
        mx = jnp.max(work, axis=0, keepdims=True)
        idx = jnp.min(jnp.where(work == mx, row, N_EXP), axis=0, keepdims=True)
        hot = row == idx
        vals.append(mx)
        hots.append(hot)
        work = jnp.where(hot, 2.0 * NEG_BIG, work)
    es = [jnp.exp(v - vals[0]) for v in vals]
    den = es[0]
    for e in es[1:]:
        den = den + e
    sel_t = jnp.zeros((N_EXP, tm), F32)
    for hot in hots:
        sel_t = jnp.where(hot, 1.0, sel_t)
    r_i = lax.broadcasted_iota(jnp.int32, (tm, tm), 0)
    c_i = lax.broadcasted_iota(jnp.int32, (tm, tm), 1)
    earlier = jnp.logical_and(r_i < c_i, r_i // TOK_TILE == c_i // TOK_TILE)
    rank_t = _dot(sel_t.astype(BF16), jnp.where(earlier, 1.0, 0.0).astype(BF16))
    sub = lax.broadcasted_iota(jnp.int32, (ROUTE_SEL, tm), 0)
    head = jnp.zeros((ROUTE_SEL, tm), F32)
    for k in range(TOP_K):
        e_k = jnp.sum(jnp.where(hots[k], row_f, 0.0), axis=0, keepdims=True)
        r_k = jnp.sum(jnp.where(hots[k], rank_t, 0.0), axis=0, keepdims=True)
        head = jnp.where(sub == k, e_k, head)
        head = jnp.where(sub == TOP_K + k, es[k] / den, head)
        head = jnp.where(sub == 2 * TOP_K + k, r_k, head)
    pad = jnp.zeros((LANES - ROUTE_SEL - N_EXP, tm), F32)
    route = jnp.concatenate([head, sel_t, pad], axis=0).T
    route_ref[...] = route
    for s in range(tm // TOK_TILE):
        cnt_ref[s] = jnp.sum(route[s * TOK_TILE:(s + 1) * TOK_TILE], axis=0, keepdims=True)


def _merge(oa, hm, ga, gb, x, gt1, sc2, sh2, g2, wa, wb, wo, wr, br, rows_per_mod, tile, row0, n_all, prev=None):
    n = x.shape[0]
    nt = n // tile
    nt_all = n_all // TOK_TILE
    tile0 = row0 // tile
    sub = tile // TOK_TILE
    tiles_per_group = nt // gt1.shape[0]
    tok = lambda w: pl.BlockSpec((tile, w), lambda i: (i, 0))
    mod = pl.BlockSpec((None, rows_per_mod, D_MODEL), lambda i: (i // tiles_per_group, 0, 0))
    res = lambda shape: pl.BlockSpec(shape, lambda i: (0, 0), pipeline_mode=pl.Buffered(1))
    out_tok = lambda w: pl.BlockSpec((tile, w), lambda i: (tile0 + i, 0))
    in_specs = [tok(W_VA), tok(W_M), tok(D_MODEL), tok(D_MODEL), tok(D_MODEL), mod, mod, mod,
                pl.BlockSpec((1, D_MODEL), lambda i: (0, 0)),
                res((W_VA, D_MODEL)), res((W_M, D_MODEL)), res((D_MODEL, D_MODEL)), res((D_MODEL, LANES)),
                pl.BlockSpec((1, LANES), lambda i: (0, 0))]
    args = [oa, hm, ga, gb, x, gt1, sc2, sh2, g2, wa, wb, wo, wr, br]
    aliases = {}
    if prev is not None:
        in_specs += [pl.BlockSpec(memory_space=pl.ANY)] * 4
        aliases = {len(args) + j: j for j in range(4)}
        args += list(prev)
    return pl.pallas_call(
        _merge_body,
        grid=(nt,),
        in_specs=in_specs,
        out_specs=[out_tok(D_MODEL), out_tok(D_MODEL), out_tok(LANES),
                   pl.BlockSpec((sub, 1, LANES), lambda i: (tile0 + i, 0, 0))],
        out_shape=[jax.ShapeDtypeStruct((n_all, D_MODEL), F32), jax.ShapeDtypeStruct((n_all, D_MODEL), BF16),
                   jax.ShapeDtypeStruct((n_all, LANES), F32), jax.ShapeDtypeStruct((nt_all, 1, LANES), F32)],
        input_output_aliases=aliases,
        compiler_params=_params(("arbitrary",)),
        name="merge",
    )(*args)


def _segment_copies(src, dst, sem, src_row, dst_row, n_groups, max_groups):
    out = []
    bit = 1
    while bit * 2 <= max_groups:
        bit *= 2
    while bit >= 1:
        off = (n_groups // (2 * bit)) * (2 * bit) * ROW_ALIGN
        rows = bit * ROW_ALIGN
        cp = pltpu.make_async_copy(src.at[pl.ds(pl.multiple_of(src_row + off, ROW_ALIGN), rows)],
                                   dst.at[pl.ds(pl.multiple_of(dst_row + off, ROW_ALIGN), rows)], sem)
        out.append(((n_groups // bit) % 2 == 1, cp))
        bit //= 2
    return out


def _run_copies(copies):
    for pred, cp in copies:
        pl.when(pred)(cp.start)
    for pred, cp in copies:
        pl.when(pred)(cp.wait)


def _slot_rows(route_t, loff_col, k):
    e_row = route_t[k:k + 1, :]
    r_row = route_t[2 * TOP_K + k:2 * TOP_K + k + 1, :]
    sub = lax.broadcasted_iota(jnp.int32, (LANES, route_t.shape[1]), 0).astype(F32)
    return jnp.sum(jnp.where(sub == e_row, loff_col, 0.0), axis=0, keepdims=True) + r_row


def _for_groups(n_groups, table_ref, base, make_copy, start):
    def body(i, carry):
        for j in range(GROUP_UNROLL):
            g = i * GROUP_UNROLL + j
            cp = make_copy(g, table_ref[base + g])
            if start:
                cp.start()
            else:
                cp.wait()
        return carry
    lax.fori_loop(0, (n_groups + GROUP_UNROLL - 1) // GROUP_UNROLL, body, 0)


def _dispatch_body(ng_ref, dt_ref, lo_ref, sv_ref, rm_ref, tail_ref, h2_ref, route_ref, soff_ref, xs_ref,
                   loc_scr, carry_scr, zero_scr, sem, *, nt):
    t = pl.program_id(0)
    buf = t % 2

    def copies_of(tt):
        def make(g, row):
            return pltpu.make_async_copy(
                loc_scr.at[tt % 2, pl.ds(pl.multiple_of(g * ROW_ALIGN, ROW_ALIGN), ROW_ALIGN)],
                xs_ref.at[pl.ds(pl.multiple_of(row, ROW_ALIGN), ROW_ALIGN)], sem.at[tt % 2])
        return ng_ref[tt], dt_ref, tt * GROUPS, make

    @pl.when(t == 0)
    def _():
        carry_scr[...] = jnp.zeros(carry_scr.shape, BF16)
        loc_scr[...] = jnp.zeros(loc_scr.shape, BF16)

    @pl.when(t >= 2)
    def _():
        _for_groups(*copies_of(t - 2), start=False)

    route_t = route_ref[...].T
    soff_col = soff_ref[...]
    slots = [_slot_rows(route_t, soff_col, k) for k in range(TOP_K)]

    def sort_rows(m):
        r_i = lax.broadcasted_iota(jnp.int32, (m, TOK_TILE), 0).astype(F32)
        onehot = jnp.zeros((m, TOK_TILE), F32)
        for k in range(TOP_K):
            onehot = jnp.where(r_i == slots[k], 1.0, onehot)
        loc_scr[buf, 0:m, :] = _dot(onehot.astype(BF16), h2_ref[...]).astype(BF16)

    fits = ng_ref[t] * ROW_ALIGN <= M_LOC_SHORT
    pl.when(fits)(lambda: sort_rows(M_LOC_SHORT))
    pl.when(jnp.logical_not(fits))(lambda: sort_rows(M_LOC))

    for e in range(N_EXP):
        lo = pl.multiple_of(lo_ref[t * N_EXP + e], ROW_ALIGN)
        sv = pl.multiple_of(sv_ref[t * N_EXP + e], ROW_ALIGN)
        loc_scr[buf, pl.ds(lo, ROW_ALIGN), :] = loc_scr[buf, pl.ds(lo, ROW_ALIGN), :] + carry_scr[e]
        pending = loc_scr[buf, pl.ds(sv, ROW_ALIGN), :]
        carry_scr[e] = jnp.where(rm_ref[t * N_EXP + e] > 0, pending, jnp.zeros_like(pending))

    _for_groups(*copies_of(t), start=True)

    @pl.when(t == nt - 1)
    def _():
        if nt >= 2:
            _for_groups(*copies_of(t - 1), start=False)
        _for_groups(*copies_of(t), start=False)
        zero_scr[...] = jnp.zeros(zero_scr.shape, BF16)
        tails = []
        for e in range(N_EXP):
            tails += _segment_copies(zero_scr, xs_ref, sem.at[0], 0, tail_ref[e], tail_ref[N_EXP + e],
                                     FFN_TILE // ROW_ALIGN - 1)
        _run_copies(tails)


def _dispatch(meta, h2, route, soff_col):
    nt = h2.shape[0] // TOK_TILE
    grid_spec = pltpu.PrefetchScalarGridSpec(
        num_scalar_prefetch=6,
        grid=(nt,),
        in_specs=[pl.BlockSpec((TOK_TILE, D_MODEL), lambda t, *_: (t, 0)),
                  pl.BlockSpec((TOK_TILE, LANES), lambda t, *_: (t, 0)),
                  pl.BlockSpec((None, LANES, 1), lambda t, *_: (t, 0, 0))],
        out_specs=pl.BlockSpec(memory_space=pl.ANY),
        scratch_shapes=[pltpu.VMEM((2, M_LOC + ROW_ALIGN, D_MODEL), BF16),
                        pltpu.VMEM((N_EXP, ROW_ALIGN, D_MODEL), BF16),
                        pltpu.VMEM((FFN_TILE, D_MODEL), BF16), pltpu.SemaphoreType.DMA((2,))],
    )
    return pl.pallas_call(
        functools.partial(_dispatch_body, nt=nt),
        grid_spec=grid_spec,
        out_shape=jax.ShapeDtypeStruct((meta["rows"] + 2 * M_LOC, D_MODEL), BF16),
        compiler_params=_params(("arbitrary",)),
        name="moe_dispatch",
    )(meta["n_groups"], meta["dispatch_rows"], meta["slot_start"], meta["carry_start"], meta["carry_rows"],
      meta["tail"], h2, route, soff_col)


def _ffn_body(be_ref, nu_ref, nx_ref, sl_ref, hf_ref, x_ref, bgu_ref, bd_ref, wgu_hbm, wd_hbm, y_ref,
              wgu_f32, wd_f32, wgu_scr, wd_scr, sem):
    i = pl.program_id(0)
    expert = be_ref[i]
    slot = sl_ref[i]
    prev = be_ref[jnp.maximum(i - 1, 0)]

    def weight_copies(e, s):
        return (pltpu.make_async_copy(wgu_hbm.at[e], wgu_f32.at[s], sem.at[s]),
                pltpu.make_async_copy(wd_hbm.at[e], wd_f32.at[s], sem.at[s]))

    @pl.when(i == 0)
    def _():
        for cp in weight_copies(expert, slot):
            cp.start()

    @pl.when(jnp.logical_and(i < nu_ref[0], jnp.logical_or(i == 0, expert != prev)))
    def _():
        for cp in weight_copies(expert, slot):
            cp.wait()
        wgu_scr[...] = wgu_f32[slot].astype(BF16)
        wd_scr[...] = wd_f32[slot].astype(BF16)

        @pl.when(nx_ref[i] >= 0)
        def _():
            for cp in weight_copies(nx_ref[i], 1 - slot):
                cp.start()

    def expert_rows(m):
        gu = _dot(x_ref[0:m, :], wgu_scr[...]) + bgu_ref[...]
        gate = jnp.minimum(gu[:, :D_FF], SWIGLU_LIMIT)
        up = jnp.clip(gu[:, D_FF:], -SWIGLU_LIMIT, SWIGLU_LIMIT)
        act = (up + 1.0) * gate * _sigmoid(SWIGLU_ALPHA * gate)
        y_ref[0:m, :] = (_dot(act.astype(BF16), wd_scr[...]) + bd_ref[...]).astype(BF16)

    active = i < nu_ref[0]
    pl.when(jnp.logical_and(active, hf_ref[i] == 0))(lambda: expert_rows(FFN_TILE))
    pl.when(jnp.logical_and(active, hf_ref[i] != 0))(lambda: expert_rows(FFN_TILE // 2))


def _ffn(meta, xs, w_gu, b_gu, w_down, b_down):
    rows = meta["rows"]
    nblk = rows // FFN_TILE
    row_blk = pl.BlockSpec((FFN_TILE, D_MODEL), lambda i, be, nu, *_: (jnp.minimum(i, nu[0] - 1), 0))
    grid_spec = pltpu.PrefetchScalarGridSpec(
        num_scalar_prefetch=5,
        grid=(nblk,),
        in_specs=[row_blk,
                  pl.BlockSpec((None, 1, 2 * D_FF), lambda i, be, *_: (be[i], 0, 0)),
                  pl.BlockSpec((None, 1, D_MODEL), lambda i, be, *_: (be[i], 0, 0)),
                  pl.BlockSpec(memory_space=pl.ANY), pl.BlockSpec(memory_space=pl.ANY)],
        out_specs=row_blk,
        scratch_shapes=[pltpu.VMEM((2, D_MODEL, 2 * D_FF), F32), pltpu.VMEM((2, D_FF, D_MODEL), F32),
                        pltpu.VMEM((D_MODEL, 2 * D_FF), BF16), pltpu.VMEM((D_FF, D_MODEL), BF16),
                        pltpu.SemaphoreType.DMA((2,))],
    )
    return pl.pallas_call(
        _ffn_body,
        grid_spec=grid_spec,
        out_shape=jax.ShapeDtypeStruct((rows, D_MODEL), BF16),
        compiler_params=_params(("arbitrary",)),
        name="moe_ffn",
    )(meta["blk_exp"], meta["n_used"], meta["next_exp"], meta["blk_slot"], meta["blk_half"], xs,
      b_gu.reshape(N_EXP, 1, 2 * D_FF), b_down.reshape(N_EXP, 1, D_MODEL), w_gu, w_down)


def _combine_body(ng_ref, ct_ref, route_ref, soff_ref, x2_ref, gtp_ref, gts_ref, gf_ref, ys_ref,
                  yp_ref, ysm_ref, loc_scr, moe_scr, sem, *, nt, nt_prompt, final):
    t = pl.program_id(0)
    buf = t % 2

    def copies_of(tt):
        def make(g, row):
            return pltpu.make_async_copy(
                ys_ref.at[pl.ds(pl.multiple_of(row, ROW_ALIGN), ROW_ALIGN)],
                loc_scr.at[tt % 2, pl.ds(pl.multiple_of(g * ROW_ALIGN, ROW_ALIGN), ROW_ALIGN)], sem.at[tt % 2])
        return ng_ref[tt], ct_ref, tt * GROUPS, make

    @pl.when(t == 0)
    def _():
        loc_scr[...] = jnp.zeros(loc_scr.shape, BF16)
        _for_groups(*copies_of(0), start=True)

    @pl.when(t + 1 < nt)
    def _():
        _for_groups(*copies_of(t + 1), start=True)

    _for_groups(*copies_of(t), start=False)

    route = route_ref[...]
    soff_row = soff_ref[...]
    lane = lax.broadcasted_iota(jnp.int32, (TOK_TILE, LANES), 1).astype(F32)
    slots = [jnp.sum(jnp.where(lane == route[:, k:k + 1], soff_row, 0.0), axis=1, keepdims=True)
             + route[:, 2 * TOP_K + k:2 * TOP_K + k + 1] for k in range(TOP_K)]

    def weighted_sum(m):
        c_i = lax.broadcasted_iota(jnp.int32, (TOK_TILE, m), 1).astype(F32)
        wmat = jnp.zeros((TOK_TILE, m), F32)
        for k in range(TOP_K):
            wmat = jnp.where(c_i == slots[k], route[:, TOP_K + k:TOP_K + k + 1], wmat)
        moe_scr[...] = _dot(wmat.astype(BF16), loc_scr[buf, 0:m, :])

    fits = ng_ref[t] * ROW_ALIGN <= M_LOC_SHORT
    pl.when(fits)(lambda: weighted_sum(M_LOC_SHORT))
    pl.when(jnp.logical_not(fits))(lambda: weighted_sum(M_LOC))

    gate = jnp.where(t >= nt_prompt, gts_ref[...], gtp_ref[...])
    xo = x2_ref[...] + gate * moe_scr[...]
    if final:
        ms = jnp.mean(xo * xo, axis=1, keepdims=True)
        xo = (xo * lax.rsqrt(ms + EPS)) * gf_ref[...]

    @pl.when(t < nt_prompt)
    def _():
        yp_ref[...] = xo

    @pl.when(t >= nt_prompt)
    def _():
        ysm_ref[...] = xo


def _combine(meta, route, soff_row, x2, gt2_p, gt2_s, g_final, ys, n_prompt, n_sample, final):
    nt = x2.shape[0] // TOK_TILE
    nt_prompt = n_prompt // TOK_TILE
    tiles_per_batch = nt_prompt // gt2_p.shape[0]
    grid_spec = pltpu.PrefetchScalarGridSpec(
        num_scalar_prefetch=2,
        grid=(nt,),
        in_specs=[pl.BlockSpec((TOK_TILE, LANES), lambda t, *_: (t, 0)),
                  pl.BlockSpec((None, 1, LANES), lambda t, *_: (t, 0, 0)),
                  pl.BlockSpec((TOK_TILE, D_MODEL), lambda t, *_: (t, 0)),
                  pl.BlockSpec((None, 1, D_MODEL),
                               lambda t, *_: (jnp.minimum(t, nt_prompt - 1) // tiles_per_batch, 0, 0)),
                  pl.BlockSpec((TOK_TILE, D_MODEL), lambda t, *_: (0, 0)),
                  pl.BlockSpec((1, D_MODEL), lambda t, *_: (0, 0)),
                  pl.BlockSpec(memory_space=pl.ANY)],
        out_specs=[pl.BlockSpec((TOK_TILE, D_MODEL), lambda t, *_: (jnp.minimum(t, nt_prompt - 1), 0)),
                   pl.BlockSpec((TOK_TILE, D_MODEL), lambda t, *_: (0, 0))],
        scratch_shapes=[pltpu.VMEM((2, M_LOC, D_MODEL), BF16), pltpu.VMEM((TOK_TILE, D_MODEL), F32),
                        pltpu.SemaphoreType.DMA((2,))],
    )
    return pl.pallas_call(
        functools.partial(_combine_body, nt=nt, nt_prompt=nt_prompt, final=final),
        grid_spec=grid_spec,
        out_shape=[jax.ShapeDtypeStruct((n_prompt, D_MODEL), F32), jax.ShapeDtypeStruct((n_sample, D_MODEL), F32)],
        compiler_params=_params(("arbitrary",)),
        name="moe_combine",
    )(meta["n_groups"], meta["combine_rows"], route, soff_row, x2, gt2_p, gt2_s, g_final, ys)


def _moe_offsets(cnt):
    nt = cnt.shape[0]
    ra = ROW_ALIGN
    prefix = jnp.cumsum(cnt, axis=0) - cnt
    total = jnp.sum(cnt, axis=0)
    pending = prefix % ra
    used = pending + cnt
    seg = (used + ra - 1) // ra * ra
    lo = jnp.cumsum(seg, axis=1) - seg
    n_groups = jnp.sum(seg, axis=1) // ra
    gpad = (total + FFN_TILE - 1) // FFN_TILE * FFN_TILE
    gstart = jnp.cumsum(gpad) - gpad
    base = gstart[None, :] + prefix // ra * ra
    last = (jnp.arange(nt) == nt - 1)[:, None]
    n_write = jnp.where(last, seg // ra, used // ra)
    carry_start = lo + used // ra * ra
    carry_rows = jnp.where(last, 0, used % ra)
    g = jnp.arange(GROUPS)
    slot_end = (lo + seg) // ra
    owner = jnp.minimum(jnp.sum(g[None, :, None] >= slot_end[:, None, :], axis=2), N_EXP - 1)
    pick = lambda a: jnp.sum(jnp.where(owner[:, :, None] == jnp.arange(N_EXP), a[:, None, :], 0), axis=2)
    k = g[None, :] - pick(lo) // ra
    row = pick(base) + k * ra
    valid = g[None, :] < n_groups[:, None]
    rows = (nt * TOK_TILE * TOP_K + N_EXP * (FFN_TILE - 1) + FFN_TILE - 1) // FFN_TILE * FFN_TILE
    spare = rows + (jnp.arange(nt) % 2)[:, None] * M_LOC + g[None, :] * ra
    combine_rows = jnp.where(valid, row, 0)
    dispatch_rows = jnp.where(valid & (k < pick(n_write)), row, spare)

    nblk_e = gpad // FFN_TILE
    blk_end = jnp.cumsum(nblk_e)
    n_used = jnp.maximum(blk_end[-1], 1)
    blk = jnp.minimum(jnp.arange(rows // FFN_TILE, dtype=jnp.int32), n_used - 1)
    blk_exp = jnp.minimum(jnp.sum(blk[:, None] >= blk_end[None, :], axis=1), N_EXP - 1)
    experts = jnp.arange(N_EXP)
    following = lax.cummin(jnp.where(nblk_e > 0, experts, N_EXP), reverse=True)
    next_of = jnp.concatenate([following[1:], jnp.full((1,), N_EXP, following.dtype)])
    next_of = jnp.where(next_of >= N_EXP, -1, next_of)
    parity = (jnp.cumsum(nblk_e > 0) - 1) % 2
    pick_e = lambda a: jnp.sum(jnp.where(blk_exp[:, None] == experts[None, :], a[None, :], 0), axis=1)
    blk_rows = pick_e(total) - (blk - pick_e(blk_end - nblk_e)) * FFN_TILE
    blk_half = blk_rows <= FFN_TILE // 2
    total16 = (total + ra - 1) // ra * ra
    tail = jnp.concatenate([gstart + total16, (gpad - total16) // ra])
    i32 = lambda a: a.astype(jnp.int32).reshape(-1)
    return dict(n_groups=i32(n_groups), dispatch_rows=i32(dispatch_rows), combine_rows=i32(combine_rows),
                slot_start=i32(lo), carry_start=i32(carry_start), carry_rows=i32(carry_rows), tail=i32(tail),
                blk_exp=i32(blk_exp), n_used=i32(n_used), next_exp=i32(pick_e(next_of)), blk_slot=i32(pick_e(parity)),
                blk_half=i32(blk_half),
                slot_off=(lo + pending).astype(F32), rows=rows)


def _split_w_in(w_in):
    g0 = C_GATES
    return (w_in[:, :g0].astype(BF16), w_in[:, g0 + N_GATE:].astype(BF16),
            _pad_lanes(w_in[:, g0:g0 + N_GATE]).astype(BF16))


def _pad_lanes(a, value=0.0):
    return jnp.pad(a, [(0, 0)] * (a.ndim - 1) + [(0, LANES - a.shape[-1])], constant_values=value)


def kernel(x_prompt, x_sample, c_prompt, c_sample, cache_k, cache_v, state_conv, state_C, state_n, state_m, page_table, w_ada, b_ada, g_norm1, g_norm2, w_in, b_gates, lambda_q1, lambda_k1, lambda_q2, lambda_k2, g_subln, w_conv, b_conv, g_mnorm, w_up_a, w_up_b, w_out, w_router, b_router, w_gu, b_gu, w_down, b_down, g_final):
    B, S, D = x_prompt.shape
    Bd, Td, _ = x_sample.shape
    depth = w_in.shape[0]
    n_pool = cache_k.shape[1]
    past_len = page_table.shape[1] * PAGE_SIZE
    n_p, n_s = B * S, Bd * Td
    n_all = n_p + n_s
    assert D == D_MODEL and n_s == TOK_TILE and S % MLSTM_CHUNK == 0 and n_p % TOK_TILE == 0
    assert page_table.shape[1] % PAGES_PER_STEP == 0

    cos_p, sin_p = _rope_tables(np.arange(S))
    cos_s, sin_s = _rope_tables(np.tile(past_len + np.arange(Td), Bd))
    hp = x_prompt.reshape(n_p, D)
    hs = x_sample.reshape(n_s, D)
    c_all = jnp.concatenate([c_prompt, c_sample], axis=0)
    outs = [[] for _ in range(12)]

    for l in range(depth):
        lam_init = 0.8 - 0.6 * math.exp(-0.3 * l)
        mod = _ada(c_all, w_ada[l], b_ada[l])
        mods = [mod[:, j * D:(j + 1) * D] for j in range(6)]
        mp = [m[:B].reshape(B, 1, D) for m in mods]
        ms_ = [jnp.repeat(m[B:], Td, axis=0).reshape(1, n_s, D) for m in mods]
        w_parts = _split_w_in(w_in[l])
        g1 = g_norm1[l].reshape(1, D)
        lam_vec = jnp.stack([lambda_q1[l], lambda_k1[l], lambda_q2[l], lambda_k2[l]])
        gsub = g_subln[l].reshape(1, DV_A)
        bg = _pad_lanes(b_gates[l].reshape(1, N_GATE))
        cw, cb = w_conv[l], b_conv[l].reshape(1, 2 * W_M)

        (q_p, _, kb_p, v_p, _, qk_p, vm_p, om_p, ga_p, gb_p, gt_p, vt_p, kt_p) = _inproj(
            hp, mp[1], mp[0], g1, cos_p, sin_p, w_parts, 1, PROJ_TILE)
        (q_s, k_s, kb_s, v_s, vb_s, qk_s, vm_s, om_s, ga_s, gb_s, gt_s, _, _) = _inproj(
            hs, ms_[1], ms_[0], g1, cos_s, sin_s, w_parts, n_s, n_s)
        k_p = jnp.transpose(kt_p.reshape(B, H_A, 2, DK_A, S), (0, 4, 1, 2, 3))

        kt_pool = jnp.transpose(cache_k[l], (0, 2, 3, 4, 1)).reshape(n_pool, W_QA, PAGE_SIZE)
        v_pool = cache_v[l].reshape(n_pool, PAGE_SIZE * H_A, DV_A)
        oa_p, oa_s = _attention(q_p, kb_p, vt_p, q_s, kb_s, vb_s, kt_pool, v_pool, page_table, lam_vec, gsub,
                                B, S, lam_init, Td)

        zeros = lambda *shape: jnp.zeros(shape, F32)
        hm_p, cst_p, C_p, nn_p, m_p = _mlstm(qk_p, vm_p, om_p, gt_p, cw, cb, bg, g_mnorm[l],
                                             zeros(B, CONV_W - 1, 2 * W_M), zeros(B, H_M, DH_M, DH_M),
                                             zeros(B, H_M, DH_M), zeros(B, 1, LANES), B, S)
        hm_s, cst_s, C_s, nn_s, m_s = _mlstm(qk_s, vm_s, om_s, gt_s, cw, cb, bg, g_mnorm[l],
                                             state_conv[l], state_C[l], state_n[l],
                                             _pad_lanes(state_m[l]).reshape(Bd, 1, LANES), Bd, Td)

        wa, wb, wo = w_up_a[l].astype(BF16), w_up_b[l].astype(BF16), w_out[l].astype(BF16)
        wr = _pad_lanes(w_router[l]).astype(BF16)
        br = _pad_lanes(b_router[l].reshape(1, N_EXP))
        g2 = g_norm2[l].reshape(1, D)
        part = _merge(oa_p, hm_p, ga_p, gb_p, hp, mp[2], mp[4], mp[3], g2, wa, wb, wo, wr, br,
                      1, PROJ_TILE, 0, n_all)
        x2, h2, route, cnt = _merge(oa_s, hm_s, ga_s, gb_s, hs, ms_[2], ms_[4], ms_[3], g2, wa, wb, wo, wr, br,
                                    n_s, n_s, n_p, n_all, prev=part)

        meta = _moe_offsets(jnp.round(cnt[:, 0, ROUTE_SEL:ROUTE_SEL + N_EXP]).astype(jnp.int32))
        soff = _pad_lanes(meta["slot_off"])
        xs = _dispatch(meta, h2, route, soff[:, :, None])
        ys = _ffn(meta, xs, w_gu[l], b_gu[l], w_down[l], b_down[l])
        final = l == depth - 1
        hp, hs = _combine(meta, route, soff[:, None, :], x2, mp[5], ms_[5][0], g_final.reshape(1, D), ys,
                          n_p, n_s, final)

        for j, a in enumerate([k_p, v_p.reshape(B, S, H_A, DV_A), cst_p, C_p, nn_p,
                               m_p[:, 0, :H_M],
                               k_s.reshape(Bd, Td, H_A, 2, DK_A), v_s.reshape(Bd, Td, H_A, DV_A), cst_s, C_s, nn_s,
                               m_s[:, 0, :H_M]]):
            outs[j].append(a)

    return (hp.reshape(B, S, D), hs.reshape(Bd, Td, D)) + tuple(jnp.stack(o) for o in outs)
```

```python
import functools
import math

import numpy as np
import jax
import jax.numpy as jnp
from jax import lax
from jax.experimental import pallas as pl
from jax.experimental.pallas import tpu as pltpu

F32 = jnp.float32
BF16 = jnp.bfloat16

D_MODEL = 1024
H_A = 4
DK_A = 64
DV_A = 2 * DK_A
ROPE_THETA = 10000.0
H_M = 4
DH_M = 128
CONV_W = 4
N_EXP = 32
TOP_K = 4
D_FF = D_MODEL
SWIGLU_LIMIT = 7.0
SWIGLU_ALPHA = 1.702
EPS = 1e-6
PAGE_SIZE = 128

W_QA = H_A * 2 * DK_A
W_VA = H_A * DV_A
W_M = H_M * DH_M
N_GATE = 2 * H_M

LANES = 128
ROW_ALIGN = 16
TOK_TILE = 256
PROJ_TILE = 512
FFN_TILE = 512
ATT_TILE = 256
ATT_HEADS = H_A
MLSTM_CHUNK = 256
PAGES_PER_STEP = 16
CHUNKS_PER_STEP = 2
NEG_BIG = -1e30
LOG2_E = math.log2(math.e)
VMEM_LIMIT = 56 * 1024 * 1024

C_QA, C_KA, C_VA, C_QK, C_VM, C_OM, C_GATES = 0, 512, 1024, 1536, 2560, 3072, 3584
M_LOC = ((TOK_TILE * TOP_K + 2 * N_EXP * (ROW_ALIGN - 1)) + 255) // 256 * 256
GROUPS = M_LOC // ROW_ALIGN
M_LOC_SHORT = TOK_TILE * TOP_K + N_EXP * ROW_ALIGN
ROUTE_SEL = 32
GROUP_UNROLL = 4


def _dot(a, b):
    return jnp.dot(a, b, preferred_element_type=F32)


def _dot_nt(a, b):
    return lax.dot_general(a, b, (((1,), (1,)), ((), ())), preferred_element_type=F32)


def _sigmoid(x):
    return 0.5 * jnp.tanh(0.5 * x) + 0.5


def _start_all(copies):
    for cp in copies:
        cp.start()


def _params(sem):
    return pltpu.CompilerParams(dimension_semantics=sem, vmem_limit_bytes=VMEM_LIMIT)


def _ada_body(c_ref, w_ref, b_ref, o_ref):
    c = c_ref[...]
    s = c * _sigmoid(c)
    s_hi = s.astype(BF16)
    s_lo = (s - s_hi.astype(F32)).astype(BF16)
    w = w_ref[...]
    w_hi = w.astype(BF16)
    w_lo = (w - w_hi.astype(F32)).astype(BF16)
    o_ref[...] = _dot(s_hi, w_hi) + _dot(s_lo, w_hi) + _dot(s_hi, w_lo) + b_ref[...]


def _ada(c_all, w_ada, b_ada):
    rows = c_all.shape[0]
    n_out = w_ada.shape[1]
    blk = D_MODEL
    return pl.pallas_call(
        _ada_body,
        grid=(n_out // blk,),
        in_specs=[pl.BlockSpec((rows, D_MODEL), lambda j: (0, 0)),
                  pl.BlockSpec((D_MODEL, blk), lambda j: (0, j)),
                  pl.BlockSpec((1, blk), lambda j: (0, j))],
        out_specs=pl.BlockSpec((rows, blk), lambda j: (0, j)),
        out_shape=jax.ShapeDtypeStruct((rows, n_out), F32),
        compiler_params=_params(("arbitrary",)),
        name="ada",
    )(c_all, w_ada, b_ada.reshape(1, n_out))


def _rope(z, cos, sin):
    lane = lax.broadcasted_iota(jnp.int32, (z.shape[0], LANES), 1)
    first_half = (lane % DK_A) < (DK_A // 2)
    out = []
    for h in range(H_A):
        xh = z[:, h * LANES:(h + 1) * LANES]
        partner = jnp.where(first_half, pltpu.roll(xh, LANES - DK_A // 2, 1), pltpu.roll(xh, DK_A // 2, 1))
        out.append(xh * cos + partner * sin)
    return jnp.concatenate(out, axis=1)


def _conv_silu(u, ext, cw_ref, cb_ref):
    rows = u.shape[0]
    full = jnp.concatenate([ext, u], axis=0)
    conv = cb_ref[...] + cw_ref[CONV_W - 1:CONV_W, :] * u
    for j in range(CONV_W - 1):
        conv = conv + cw_ref[j:j + 1, :] * pltpu.roll(full, CONV_W - 1 - j, 0)[8:8 + rows]
    return conv * _sigmoid(conv)


def _inproj_body(x_ref, sc_ref, sh_ref, g_ref, cos_ref, sin_ref, w_ref, wg_ref, wt_ref,
                 q_ref, k_ref, kb_ref, v_ref, vb_ref, qk_ref, vm_ref, om_ref, ga_ref, gb_ref, gt_ref, vt_ref, kt_ref):
    x = x_ref[...]
    ms = jnp.mean(x * x, axis=1, keepdims=True)
    h = (x * lax.rsqrt(ms + EPS)) * g_ref[...] * (1.0 + sc_ref[...]) + sh_ref[...]
    hb = h.astype(BF16)
    cos = cos_ref[...]
    sin = sin_ref[...]

    def seg(lo, n):
        return _dot(hb, w_ref[:, lo:lo + n])

    q = _rope(seg(C_QA, W_QA), cos, sin) * (DK_A ** -0.5 * LOG2_E)
    q_ref[...] = q.astype(BF16)
    k = _rope(seg(C_KA, W_QA), cos, sin)
    k_ref[...] = k
    kb_ref[...] = k.astype(BF16)
    kt_ref[...] = k.T
    v = seg(C_VA, W_VA)
    for h in range(H_A):
        v_ref[pl.ds(h, v.shape[0], stride=H_A), :] = v[:, h * DV_A:(h + 1) * DV_A]
    vb_ref[...] = v.astype(BF16)
    for s in range(vt_ref.shape[0]):
        vt_ref[s] = v[s * ATT_TILE:(s + 1) * ATT_TILE].T.astype(BF16)
    qk_ref[...] = seg(C_QK, 2 * W_M)
    vm_ref[...] = seg(C_VM, W_M).astype(BF16)
    om_ref[...] = seg(C_OM, W_M)
    ga_ref[...] = _dot(hb, wg_ref[:, :D_MODEL])
    gb_ref[...] = _dot(hb, wg_ref[:, D_MODEL:])
    gt_ref[...] = _dot(hb, wt_ref[...])


def _inproj(x, sc, sh, g1, cos, sin, w_parts, rows_per_mod, tile):
    n = x.shape[0]
    nt = n // tile
    tiles_per_group = nt // sc.shape[0]
    tab_tiles = cos.shape[0] // tile
    slabs = tile // ATT_TILE
    tok = lambda w: pl.BlockSpec((tile, w), lambda i: (i, 0))
    mod = pl.BlockSpec((None, rows_per_mod, D_MODEL), lambda i: (i // tiles_per_group, 0, 0))
    tab = pl.BlockSpec((tile, LANES), lambda i: (i % tab_tiles, 0))
    tok_out = lambda w, dt: (tok(w), jax.ShapeDtypeStruct((n, w), dt))
    outs = [
        tok_out(W_QA, BF16),
        tok_out(W_QA, F32), tok_out(W_QA, BF16),
        (pl.BlockSpec((tile * H_A, DV_A), lambda i: (i, 0)), jax.ShapeDtypeStruct((n * H_A, DV_A), F32)),
        tok_out(W_VA, BF16),
        tok_out(2 * W_M, F32), tok_out(W_M, BF16), tok_out(W_M, F32),
        tok_out(D_MODEL, F32), tok_out(D_MODEL, F32), tok_out(LANES, F32),
        (pl.BlockSpec((slabs, W_VA, ATT_TILE), lambda i: (i, 0, 0)),
         jax.ShapeDtypeStruct((n // ATT_TILE, W_VA, ATT_TILE), BF16)),
        (pl.BlockSpec((None, W_QA, tile), lambda i: (i // tab_tiles, 0, i % tab_tiles)),
         jax.ShapeDtypeStruct((nt // tab_tiles, W_QA, tab_tiles * tile), F32)),
    ]
    return pl.pallas_call(
        _inproj_body,
        grid=(nt,),
        in_specs=[tok(D_MODEL), mod, mod, pl.BlockSpec((1, D_MODEL), lambda i: (0, 0)), tab, tab,
                  *[pl.BlockSpec(w.shape, lambda i: (0, 0), pipeline_mode=pl.Buffered(1)) for w in w_parts]],
        out_specs=[spec for spec, _ in outs],
        out_shape=[shape for _, shape in outs],
        compiler_params=_params(("arbitrary",)),
        name="inproj",
    )(x, sc, sh, g1, cos, sin, *w_parts)


def _rope_tables(pos):
    half = DK_A // 2
    inv = ROPE_THETA ** (-np.arange(half, dtype=np.float64) * 2.0 / DK_A)
    ang = np.asarray(pos, np.float64)[:, None] * inv[None, :]
    cos = np.cos(ang)
    sin = np.sin(ang)
    cos64 = np.concatenate([cos, cos], axis=1)
    sin64 = np.concatenate([-sin, sin], axis=1)
    return (jnp.asarray(np.tile(cos64, (1, LANES // DK_A)), F32),
            jnp.asarray(np.tile(sin64, (1, LANES // DK_A)), F32))


def _lambda_value(lam_ref, lam_init):
    lv = lam_ref[...]
    l1 = jnp.sum(lv[0:1, :] * lv[1:2, :], axis=1, keepdims=True)
    l2 = jnp.sum(lv[2:3, :] * lv[3:4, :], axis=1, keepdims=True)
    return jnp.exp(l1) - jnp.exp(l2) + lam_init


def _subln(o, g, lam_init):
    ms = jnp.mean(o * o, axis=1, keepdims=True)
    return (o * lax.rsqrt(ms + EPS)) * g * (1.0 - lam_init)


def _prompt_tile(q_ref, k_ref, vt_ref, lam, g_ref, o_ref, m_scr, acc_scr, i, lam_init):
    tq = ATT_TILE
    lane = lax.broadcasted_iota(jnp.int32, (tq, LANES), 1)
    qs = []
    for hh in range(ATT_HEADS):
        q = q_ref[:, hh * LANES:(hh + 1) * LANES]
        zero = jnp.zeros_like(q)
        qs.append(jnp.concatenate([jnp.where(lane < DK_A, q, zero), jnp.where(lane >= DK_A, q, zero)], axis=0))
    for hh in range(ATT_HEADS):
        m_scr[hh][...] = jnp.full(m_scr[hh].shape, NEG_BIG, F32)
        acc_scr[hh][...] = jnp.zeros(acc_scr[hh].shape, F32)
    ones = jnp.ones((ROW_ALIGN, tq), BF16)

    def scores(hh, j):
        start = pl.multiple_of(j * tq, tq)
        return _dot_nt(k_ref[pl.ds(start, tq), hh * LANES:(hh + 1) * LANES], qs[hh])

    def update_all(j, mask):
        sts = [scores(hh, j) for hh in range(ATT_HEADS)]
        if mask is not None:
            sts = [jnp.where(mask, st, NEG_BIG) for st in sts]
        pts, alphas = [], []
        for hh in range(ATT_HEADS):
            m_old = m_scr[hh][...]
            m_new = jnp.maximum(m_old, jnp.max(sts[hh], axis=0, keepdims=True))
            alphas.append(jnp.exp2(m_old - m_new))
            pts.append(jnp.exp2(sts[hh] - m_new).astype(BF16))
            m_scr[hh][...] = m_new
        for hh in range(ATT_HEADS):
            vt = jnp.concatenate([vt_ref[j, hh * LANES:(hh + 1) * LANES, :], ones], axis=0)
            acc_scr[hh][...] = alphas[hh] * acc_scr[hh][...] + _dot(vt, pts[hh])

    def off_diag(j, carry):
        update_all(j, None)
        return carry

    lax.fori_loop(0, i, off_diag, 0)
    key = lax.broadcasted_iota(jnp.int32, (tq, 2 * tq), 0)
    qry = lax.broadcasted_iota(jnp.int32, (tq, 2 * tq), 1) % tq
    update_all(i, key <= qry)
    for hh in range(ATT_HEADS):
        acc = acc_scr[hh][...]
        ot = acc[:DV_A] / acc[DV_A:DV_A + 1]
        at = ot[:, :tq] - lam * ot[:, tq:]
        ms = jnp.mean(at * at, axis=0, keepdims=True)
        at = (at * lax.rsqrt(ms + EPS)) * g_ref[...] * (1.0 - lam_init)
        o_ref[:, hh * LANES:(hh + 1) * LANES] = at.T.astype(BF16)


def _sample_chunks(pt_ref, q_ref, kn_ref, vn_ref, lam, g_ref, kpool_ref, vpool_ref, o_ref,
                   kbuf, vbuf, sem, m_scr, l_scr, acc_scr, step, parts, *, lam_init, n_chunks, n_seq, t_new):
    steps_per_seq = n_chunks // CHUNKS_PER_STEP
    b = step // steps_per_seq
    c0 = (step % steps_per_seq) * CHUNKS_PER_STEP
    opens = 0 in parts
    closes = CHUNKS_PER_STEP - 1 in parts

    def chunk_copies(bb, c, slot):
        out = []
        for j in range(PAGES_PER_STEP):
            page = pt_ref[bb, c * PAGES_PER_STEP + j]
            out.append(pltpu.make_async_copy(kpool_ref.at[page], kbuf.at[slot, j], sem.at[slot]))
            out.append(pltpu.make_async_copy(vpool_ref.at[page], vbuf.at[slot, j], sem.at[slot]))
        return out

    if opens:
        @pl.when(step == 0)
        def _():
            _start_all(chunk_copies(0, 0, 0) + chunk_copies(0, 1, 1))

    q = q_ref[...].astype(F32)
    qt = jnp.concatenate([q] * (2 * H_A), axis=0)
    row = lax.broadcasted_iota(jnp.int32, qt.shape, 0)
    col = lax.broadcasted_iota(jnp.int32, qt.shape, 1)
    qbd = jnp.where(col // DK_A == row // t_new, qt, 0.0).astype(BF16)

    if opens:
        @pl.when(c0 == 0)
        def _():
            m_scr[...] = jnp.full(m_scr.shape, NEG_BIG, F32)
            l_scr[...] = jnp.zeros(l_scr.shape, F32)
            acc_scr[...] = jnp.zeros(acc_scr.shape, F32)

    rows_h = 2 * t_new

    def update(s, v):
        m_old = m_scr[...]
        m_new = jnp.maximum(m_old, jnp.max(s, axis=1, keepdims=True))
        alpha = jnp.exp2(m_old - m_new)
        p = jnp.exp2(s - m_new)
        l_scr[...] = alpha * l_scr[...] + jnp.sum(p, axis=1, keepdims=True)
        pv = _dot(p.astype(BF16), v)
        own = [pv[h * rows_h:(h + 1) * rows_h, h * DV_A:(h + 1) * DV_A] for h in range(H_A)]
        acc_scr[...] = alpha * acc_scr[...] + jnp.concatenate(own, axis=0)
        m_scr[...] = m_new

    def chunk(c, slot):
        for cp in chunk_copies(b, c, slot):
            cp.wait()
        kt = jnp.concatenate([kbuf[slot, j].astype(BF16) for j in range(PAGES_PER_STEP)], axis=1)

        def page_v(j):
            return jnp.concatenate([vbuf.at[slot, j][pl.ds(h, PAGE_SIZE, stride=H_A), :].astype(BF16)
                                    for h in range(H_A)], axis=1)

        v = jnp.concatenate([page_v(j) for j in range(PAGES_PER_STEP)], axis=0)
        update(_dot(qbd, kt), v)

        @pl.when(c + 2 < n_chunks)
        def _():
            _start_all(chunk_copies(b, c + 2, slot))

        @pl.when(jnp.logical_and(c + 2 >= n_chunks, b + 1 < n_seq))
        def _():
            _start_all(chunk_copies(b + 1, c + 2 - n_chunks, slot))

    for j in parts:
        chunk(c0 + j, j % 2)

    def finish():
        zpad = jnp.zeros((PAGE_SIZE - t_new, W_QA), F32)
        kn = jnp.concatenate([kn_ref[...].astype(F32), zpad], axis=0).astype(BF16)
        vn = jnp.concatenate([vn_ref[...].astype(F32), zpad], axis=0).astype(BF16)
        s = _dot_nt(qbd, kn)
        row = lax.broadcasted_iota(jnp.int32, s.shape, 0) % t_new
        col = lax.broadcasted_iota(jnp.int32, s.shape, 1)
        update(jnp.where(col <= row, s, NEG_BIG), vn)
        o = acc_scr[...] / l_scr[...]
        outs = []
        for h in range(H_A):
            r0 = h * rows_h
            outs.append(_subln(o[r0:r0 + t_new] - lam * o[r0 + t_new:r0 + rows_h], g_ref[...], lam_init))
        o_ref[...] = jnp.concatenate(outs, axis=1).astype(BF16)

    if closes:
        pl.when(c0 + CHUNKS_PER_STEP == n_chunks)(finish)


def _attention_body(pt_ref, q_ref, k_ref, vt_ref, lam_ref, gcol_ref, qs_ref, kn_ref, vn_ref, grow_ref,
                    kpool_ref, vpool_ref, o_ref, os_ref, *scratch, lam_init, n_chunks, n_seq, t_new):
    m_scr, acc_scr = scratch[:ATT_HEADS], scratch[ATT_HEADS:2 * ATT_HEADS]
    kbuf, vbuf, sem, ms_scr, ls_scr, accs_scr = scratch[2 * ATT_HEADS:]
    step = pl.program_id(0) * pl.num_programs(1) + pl.program_id(1)
    lam = _lambda_value(lam_ref, lam_init)
    sample = functools.partial(_sample_chunks, pt_ref, qs_ref, kn_ref, vn_ref, lam, grow_ref, kpool_ref, vpool_ref,
                               os_ref, kbuf, vbuf, sem, ms_scr, ls_scr, accs_scr, step,
                               lam_init=lam_init, n_chunks=n_chunks, n_seq=n_seq, t_new=t_new)
    _prompt_tile(q_ref, k_ref, vt_ref, lam, gcol_ref, o_ref, m_scr, acc_scr, pl.program_id(1), lam_init)
    sample((0, 1))


def _attention(q, k, vt, q_s, k_new, v_new, cache_k, cache_v, page_table, lam_vec, g_subln, batch, seq,
               lam_init, t_new):
    nq = seq // ATT_TILE
    bd, n_pages = page_table.shape
    n_chunks = n_pages // PAGES_PER_STEP
    steps_per_seq = n_chunks // CHUNKS_PER_STEP
    assert ATT_HEADS == H_A and CHUNKS_PER_STEP % 2 == 0 and n_chunks % CHUNKS_PER_STEP == 0
    assert batch * nq == bd * steps_per_seq
    n_rows = 2 * H_A * t_new
    kv = pl.BlockSpec((seq, W_QA), lambda b, i, pt: (b, 0))
    vts = pl.BlockSpec((nq, W_VA, ATT_TILE), lambda b, i, pt: (b, 0, 0))
    qo = pl.BlockSpec((ATT_TILE, W_QA), lambda b, i, pt: (b * nq + i, 0))
    new = pl.BlockSpec((None, t_new, W_QA), lambda b, i, pt: ((b * nq + i) // steps_per_seq, 0, 0))
    const = lambda shape: pl.BlockSpec(shape, lambda b, i, pt: (0, 0))
    page_buf = pltpu.VMEM((2, PAGES_PER_STEP, W_QA, PAGE_SIZE), F32)
    grid_spec = pltpu.PrefetchScalarGridSpec(
        num_scalar_prefetch=1,
        grid=(batch, nq),
        in_specs=[qo, kv, vts, const((4, DK_A)), const((DV_A, 1)), new, new, new, const((1, DV_A)),
                  pl.BlockSpec(memory_space=pl.ANY), pl.BlockSpec(memory_space=pl.ANY)],
        out_specs=[qo, new],
        scratch_shapes=[pltpu.VMEM((1, 2 * ATT_TILE), F32)] * ATT_HEADS
                       + [pltpu.VMEM((DV_A + ROW_ALIGN, 2 * ATT_TILE), F32)] * ATT_HEADS
                       + [page_buf, page_buf, pltpu.SemaphoreType.DMA((2,)), pltpu.VMEM((n_rows, 1), F32),
                          pltpu.VMEM((n_rows, 1), F32), pltpu.VMEM((n_rows, DV_A), F32)],
    )
    o_p, o_s = pl.pallas_call(
        functools.partial(_attention_body, lam_init=lam_init, n_chunks=n_chunks, n_seq=bd, t_new=t_new),
        grid_spec=grid_spec,
        out_shape=[jax.ShapeDtypeStruct((batch * seq, W_VA), BF16), jax.ShapeDtypeStruct((bd, t_new, W_VA), BF16)],
        compiler_params=_params(("arbitrary", "arbitrary")),
        name="attention",
    )(page_table, q, k, vt, lam_vec, g_subln.reshape(DV_A, 1), q_s.reshape(bd, t_new, W_QA),
      k_new.reshape(bd, t_new, W_QA), v_new.reshape(bd, t_new, W_VA), g_subln, cache_k, cache_v)
    return o_p, o_s.reshape(bd * t_new, W_VA)


def _mlstm_body(qk_ref, vm_ref, om_ref, gt_ref, cw_ref, cb_ref, bg_ref, gm_ref, cbuf_ref, c0_ref, n0_ref, m0_ref,
                h_ref, cst_ref, cout_ref, nout_ref, mout_ref, ext_scr, c_scr, n_scr, m_scr, *, tb, L, nc):
    c_idx = pl.program_id(1)

    @pl.when(c_idx == 0)
    def _():
        ext_scr[...] = jnp.zeros(ext_scr.shape, F32)
        ext_scr[8 - (CONV_W - 1):8, :] = cbuf_ref[...]
        c_scr[...] = c0_ref[...]
        n_scr[...] = n0_ref[...]
        m_scr[...] = m0_ref[...]

    pad = L - tb
    u = qk_ref[...]
    if pad:
        u = jnp.concatenate([u, jnp.zeros((pad, u.shape[1]), F32)], axis=0)
    a = _conv_silu(u, ext_scr[...], cw_ref, cb_ref)
    if not pad:
        ext_scr[...] = u[L - 8:L]

    @pl.when(c_idx == nc - 1)
    def _():
        cst_ref[...] = qk_ref[tb - (CONV_W - 1):tb, :]

    g = gt_ref[...] + bg_ref[...]
    li = g
    lf = jnp.minimum(g, 0.0) - jnp.log1p(jnp.exp(-jnp.abs(g)))
    if pad:
        zpad = jnp.zeros((pad, LANES), F32)
        li = jnp.concatenate([li, zpad + NEG_BIG], axis=0)
        lf = jnp.concatenate([lf, zpad], axis=0)
    row = lax.broadcasted_iota(jnp.int32, (L, LANES), 0)
    lane = lax.broadcasted_iota(jnp.int32, (L, LANES), 1)
    bcum = lf
    shift = 1
    while shift < L:
        bcum = bcum + jnp.where(row >= shift, pltpu.roll(bcum, shift, 0), 0.0)
        shift *= 2
    gates = jnp.where(lane < H_M, li, bcum)
    gates_t = gates.T
    tri = lax.broadcasted_iota(jnp.int32, (L, L), 0) >= lax.broadcasted_iota(jnp.int32, (L, L), 1)
    m_all = m_scr[...]
    lane1 = lax.broadcasted_iota(jnp.int32, (1, LANES), 1)
    m_next = m_all
    vall = vm_ref[...]
    if pad:
        vall = jnp.concatenate([vall, jnp.zeros((pad, vall.shape[1]), BF16)], axis=0)

    for h in range(H_M):
        li_col = gates[:, h:h + 1]
        b_col = gates[:, H_M + h:H_M + h + 1]
        src_row = gates_t[h:h + 1, :] - gates_t[H_M + h:H_M + h + 1, :]
        m_prev = m_all[:, h:h + 1]
        b_last = b_col[L - 1:L, :]
        log_d = jnp.where(tri, b_col + src_row, NEG_BIG)
        inter = b_col + m_prev
        mt = jnp.maximum(inter, jnp.max(log_d, axis=1, keepdims=True))
        q = a[:, h * DH_M:(h + 1) * DH_M]
        k = a[:, W_M + h * DH_M:W_M + (h + 1) * DH_M] * (DH_M ** -0.5)
        v = vall[:, h * DH_M:(h + 1) * DH_M]
        qb = q.astype(BF16)
        s = _dot_nt(qb, k.astype(BF16)) * jnp.exp(log_d - mt)
        ei = jnp.exp(inter - mt)
        c_old = c_scr[h]
        n_old = n_scr[h:h + 1, :]
        num = ei * _dot(qb, c_old.astype(BF16)) + _dot(s.astype(BF16), v)
        den = ei * jnp.sum(q * n_old, axis=1, keepdims=True) + jnp.sum(s, axis=1, keepdims=True)
        hh = num / jnp.maximum(jnp.abs(den), jnp.exp(-mt))
        g_col = b_last - b_col + li_col
        bl = b_last + m_prev
        m_new = jnp.maximum(bl, jnp.max(g_col, axis=0, keepdims=True))
        wg = jnp.exp(g_col - m_new)
        decay = jnp.exp(bl - m_new)
        kw = k * wg
        c_scr[h] = decay * c_old + _dot(kw.T.astype(BF16), v)
        n_scr[h:h + 1, :] = decay * n_old + jnp.sum(kw, axis=0, keepdims=True)
        m_next = jnp.where(lane1 == h, m_new, m_next)
        ms = jnp.mean(hh * hh, axis=1, keepdims=True)
        hn = (hh * lax.rsqrt(ms + EPS)) * gm_ref[h:h + 1, :]
        og = _sigmoid(om_ref[:, h * DH_M:(h + 1) * DH_M])
        h_ref[:, h * DH_M:(h + 1) * DH_M] = (hn[:tb] * og).astype(BF16)

    m_scr[...] = m_next

    @pl.when(c_idx == nc - 1)
    def _():
        cout_ref[...] = c_scr[...]
        nout_ref[...] = n_scr[...]
        mout_ref[...] = m_scr[...]


def _mlstm(qk, vm, om, gt, w_conv, b_conv, b_gates_pad, g_mnorm, conv_buf, c0, n0, m0_pad, batch, seq):
    tb = min(seq, MLSTM_CHUNK)
    L = max(tb, LANES)
    nc = seq // tb
    tok = lambda w: pl.BlockSpec((None, tb, w), lambda b, c: (b * nc + c, 0, 0))
    chunks = lambda a: a.reshape(batch * nc, tb, a.shape[-1])
    const = lambda shape: pl.BlockSpec(shape, lambda b, c: (0,) * len(shape))
    per_b = lambda shape: pl.BlockSpec((None,) + shape, lambda b, c: (b,) + (0,) * len(shape))
    h, cst, c_out, n_out, m_out = pl.pallas_call(
        functools.partial(_mlstm_body, tb=tb, L=L, nc=nc),
        grid=(batch, nc),
        in_specs=[tok(2 * W_M), tok(W_M), tok(W_M), tok(LANES), const((CONV_W, 2 * W_M)), const((1, 2 * W_M)),
                  const((1, LANES)), const((H_M, DH_M)), per_b((CONV_W - 1, 2 * W_M)),
                  per_b((H_M, DH_M, DH_M)), per_b((H_M, DH_M)), per_b((1, LANES))],
        out_specs=[tok(W_M), per_b((CONV_W - 1, 2 * W_M)), per_b((H_M, DH_M, DH_M)), per_b((H_M, DH_M)),
                   per_b((1, LANES))],
        out_shape=[jax.ShapeDtypeStruct((batch * nc, tb, W_M), BF16),
                   jax.ShapeDtypeStruct((batch, CONV_W - 1, 2 * W_M), F32),
                   jax.ShapeDtypeStruct((batch, H_M, DH_M, DH_M), F32),
                   jax.ShapeDtypeStruct((batch, H_M, DH_M), F32),
                   jax.ShapeDtypeStruct((batch, 1, LANES), F32)],
        scratch_shapes=[pltpu.VMEM((8, 2 * W_M), F32), pltpu.VMEM((H_M, DH_M, DH_M), F32),
                        pltpu.VMEM((H_M, DH_M), F32), pltpu.VMEM((1, LANES), F32)],
        compiler_params=_params(("arbitrary", "arbitrary")),
        name="mlstm",
    )(chunks(qk), chunks(vm), chunks(om), chunks(gt), w_conv, b_conv, b_gates_pad, g_mnorm, conv_buf, c0, n0,
      m0_pad)
    return h.reshape(batch * seq, W_M), cst, c_out, n_out, m_out


def _merge_body(oa_ref, hm_ref, ga_ref, gb_ref, x_ref, gt1_ref, sc2_ref, sh2_ref, g2_ref, wa_ref, wb_ref, wo_ref,
                wr_ref, br_ref, *rest):
    x2_ref, h2_ref, route_ref, cnt_ref = rest[-4:]
    ya = _dot(oa_ref[...], wa_ref[...])
    yb = _dot(hm_ref[...], wb_ref[...])
    mix = _sigmoid(ga_ref[...]) * ya + _sigmoid(gb_ref[...]) * yb
    y = _dot(mix.astype(BF16), wo_ref[...])
    x2 = x_ref[...] + gt1_ref[...] * y
    x2_ref[...] = x2
    ms = jnp.mean(x2 * x2, axis=1, keepdims=True)
    h2 = (x2 * lax.rsqrt(ms + EPS)) * g2_ref[...] * (1.0 + sc2_ref[...]) + sh2_ref[...]
    h2b = h2.astype(BF16)
    h2_ref[...] = h2b

    tm = h2b.shape[0]
    logits_t = (_dot(h2b, wr_ref[...]) + br_ref[...]).T[:N_EXP, :]
    row = lax.broadcasted_iota(jnp.int32, (N_EXP, tm), 0)
    row_f = row.astype(F32)
    work = logits_t
    vals, hots = [], []
    for _ in range(TOP_K):
        mx = jnp.max(work, axis=0, keepdims=True)
        idx = jnp.min(jnp.where(work == mx, row, N_EXP), axis=0, keepdims=True)
        hot = row == idx
        vals.append(mx)
        hots.append(hot)
        work = jnp.where(hot, 2.0 * NEG_BIG, work)
    es = [jnp.exp(v - vals[0]) for v in vals]
    den = es[0]
    for e in es[1:]:
        den = den + e
    sel_t = jnp.zeros((N_EXP, tm), F32)
    for hot in hots:
        sel_t = jnp.where(hot, 1.0, sel_t)
    r_i = lax.broadcasted_iota(jnp.int32, (tm, tm), 0)
    c_i = lax.broadcasted_iota(jnp.int32, (tm, tm), 1)
    earlier = jnp.logical_and(r_i < c_i, r_i // TOK_TILE == c_i // TOK_TILE)
    rank_t = _dot(sel_t.astype(BF16), jnp.where(earlier, 1.0, 0.0).astype(BF16))
    sub = lax.broadcasted_iota(jnp.int32, (ROUTE_SEL, tm), 0)
    head = jnp.zeros((ROUTE_SEL, tm), F32)
    for k in range(TOP_K):
        e_k = jnp.sum(jnp.where(hots[k], row_f, 0.0), axis=0, keepdims=True)
        r_k = jnp.sum(jnp.where(hots[k], rank_t, 0.0), axis=0, keepdims=True)
        head = jnp.where(sub == k, e_k, head)
        head = jnp.where(sub == TOP_K + k, es[k] / den, head)
        head = jnp.where(sub == 2 * TOP_K + k, r_k, head)
    pad = jnp.zeros((LANES - ROUTE_SEL - N_EXP, tm), F32)
    route = jnp.concatenate([head, sel_t, pad], axis=0).T
    route_ref[...] = route
    for s in range(tm // TOK_TILE):
        cnt_ref[s] = jnp.sum(route[s * TOK_TILE:(s + 1) * TOK_TILE], axis=0, keepdims=True)


def _merge(oa, hm, ga, gb, x, gt1, sc2, sh2, g2, wa, wb, wo, wr, br, rows_per_mod, tile, row0, n_all, prev=None):
    n = x.shape[0]
    nt = n // tile
    nt_all = n_all // TOK_TILE
    tile0 = row0 // tile
    sub = tile // TOK_TILE
    tiles_per_group = nt // gt1.shape[0]
    tok = lambda w: pl.BlockSpec((tile, w), lambda i: (i, 0))
    mod = pl.BlockSpec((None, rows_per_mod, D_MODEL), lambda i: (i // tiles_per_group, 0, 0))
    res = lambda shape: pl.BlockSpec(shape, lambda i: (0, 0), pipeline_mode=pl.Buffered(1))
    out_tok = lambda w: pl.BlockSpec((tile, w), lambda i: (tile0 + i, 0))
    in_specs = [tok(W_VA), tok(W_M), tok(D_MODEL), tok(D_MODEL), tok(D_MODEL), mod, mod, mod,
                pl.BlockSpec((1, D_MODEL), lambda i: (0, 0)),
                res((W_VA, D_MODEL)), res((W_M, D_MODEL)), res((D_MODEL, D_MODEL)), res((D_MODEL, LANES)),
                pl.BlockSpec((1, LANES), lambda i: (0, 0))]
    args = [oa, hm, ga, gb, x, gt1, sc2, sh2, g2, wa, wb, wo, wr, br]
    aliases = {}
    if prev is not None:
        in_specs += [pl.BlockSpec(memory_space=pl.ANY)] * 4
        aliases = {len(args) + j: j for j in range(4)}
        args += list(prev)
    return pl.pallas_call(
        _merge_body,
        grid=(nt,),
        in_specs=in_specs,
        out_specs=[out_tok(D_MODEL), out_tok(D_MODEL), out_tok(LANES),
                   pl.BlockSpec((sub, 1, LANES), lambda i: (tile0 + i, 0, 0))],
        out_shape=[jax.ShapeDtypeStruct((n_all, D_MODEL), F32), jax.ShapeDtypeStruct((n_all, D_MODEL), BF16),
                   jax.ShapeDtypeStruct((n_all, LANES), F32), jax.ShapeDtypeStruct((nt_all, 1, LANES), F32)],
        input_output_aliases=aliases,
        compiler_params=_params(("arbitrary",)),
        name="merge",
    )(*args)


def _segment_copies(src, dst, sem, src_row, dst_row, n_groups, max_groups):
    out = []
    bit = 1
    while bit * 2 <= max_groups:
        bit *= 2
    while bit >= 1:
        off = (n_groups // (2 * bit)) * (2 * bit) * ROW_ALIGN
        rows = bit * ROW_ALIGN
        cp = pltpu.make_async_copy(src.at[pl.ds(pl.multiple_of(src_row + off, ROW_ALIGN), rows)],
                                   dst.at[pl.ds(pl.multiple_of(dst_row + off, ROW_ALIGN), rows)], sem)
        out.append(((n_groups // bit) % 2 == 1, cp))
        bit //= 2
    return out


def _run_copies(copies):
    for pred, cp in copies:
        pl.when(pred)(cp.start)
    for pred, cp in copies:
        pl.when(pred)(cp.wait)


def _slot_rows(route_t, loff_col, k):
    e_row = route_t[k:k + 1, :]
    r_row = route_t[2 * TOP_K + k:2 * TOP_K + k + 1, :]
    sub = lax.broadcasted_iota(jnp.int32, (LANES, route_t.shape[1]), 0).astype(F32)
    return jnp.sum(jnp.where(sub == e_row, loff_col, 0.0), axis=0, keepdims=True) + r_row


def _for_groups(n_groups, table_ref, base, make_copy, start):
    def body(i, carry):
        for j in range(GROUP_UNROLL):
            g = i * GROUP_UNROLL + j
            cp = make_copy(g, table_ref[base + g])
            if start:
                cp.start()
            else:
                cp.wait()
        return carry
    lax.fori_loop(0, (n_groups + GROUP_UNROLL - 1) // GROUP_UNROLL, body, 0)


def _dispatch_body(ng_ref, dt_ref, lo_ref, sv_ref, rm_ref, tail_ref, h2_ref, route_ref, soff_ref, xs_ref,
                   loc_scr, carry_scr, zero_scr, sem, *, nt):
    t = pl.program_id(0)
    buf = t % 2

    def copies_of(tt):
        def make(g, row):
            return pltpu.make_async_copy(
                loc_scr.at[tt % 2, pl.ds(pl.multiple_of(g * ROW_ALIGN, ROW_ALIGN), ROW_ALIGN)],
                xs_ref.at[pl.ds(pl.multiple_of(row, ROW_ALIGN), ROW_ALIGN)], sem.at[tt % 2])
        return ng_ref[tt], dt_ref, tt * GROUPS, make

    @pl.when(t == 0)
    def _():
        carry_scr[...] = jnp.zeros(carry_scr.shape, BF16)
        loc_scr[...] = jnp.zeros(loc_scr.shape, BF16)

    @pl.when(t >= 2)
    def _():
        _for_groups(*copies_of(t - 2), start=False)

    route_t = route_ref[...].T
    soff_col = soff_ref[...]
    slots = [_slot_rows(route_t, soff_col, k) for k in range(TOP_K)]

    def sort_rows(m):
        r_i = lax.broadcasted_iota(jnp.int32, (m, TOK_TILE), 0).astype(F32)
        onehot = jnp.zeros((m, TOK_TILE), F32)
        for k in range(TOP_K):
            onehot = jnp.where(r_i == slots[k], 1.0, onehot)
        loc_scr[buf, 0:m, :] = _dot(onehot.astype(BF16), h2_ref[...]).astype(BF16)

    fits = ng_ref[t] * ROW_ALIGN <= M_LOC_SHORT
    pl.when(fits)(lambda: sort_rows(M_LOC_SHORT))
    pl.when(jnp.logical_not(fits))(lambda: sort_rows(M_LOC))

    for e in range(N_EXP):
        lo = pl.multiple_of(lo_ref[t * N_EXP + e], ROW_ALIGN)
        sv = pl.multiple_of(sv_ref[t * N_EXP + e], ROW_ALIGN)
        loc_scr[buf, pl.ds(lo, ROW_ALIGN), :] = loc_scr[buf, pl.ds(lo, ROW_ALIGN), :] + carry_scr[e]
        pending = loc_scr[buf, pl.ds(sv, ROW_ALIGN), :]
        carry_scr[e] = jnp.where(rm_ref[t * N_EXP + e] > 0, pending, jnp.zeros_like(pending))

    _for_groups(*copies_of(t), start=True)

    @pl.when(t == nt - 1)
    def _():
        if nt >= 2:
            _for_groups(*copies_of(t - 1), start=False)
        _for_groups(*copies_of(t), start=False)
        zero_scr[...] = jnp.zeros(zero_scr.shape, BF16)
        tails = []
        for e in range(N_EXP):
            tails += _segment_copies(zero_scr, xs_ref, sem.at[0], 0, tail_ref[e], tail_ref[N_EXP + e],
                                     FFN_TILE // ROW_ALIGN - 1)
        _run_copies(tails)


def _dispatch(meta, h2, route, soff_col):
    nt = h2.shape[0] // TOK_TILE
    grid_spec = pltpu.PrefetchScalarGridSpec(
        num_scalar_prefetch=6,
        grid=(nt,),
        in_specs=[pl.BlockSpec((TOK_TILE, D_MODEL), lambda t, *_: (t, 0)),
                  pl.BlockSpec((TOK_TILE, LANES), lambda t, *_: (t, 0)),
                  pl.BlockSpec((None, LANES, 1), lambda t, *_: (t, 0, 0))],
        out_specs=pl.BlockSpec(memory_space=pl.ANY),
        scratch_shapes=[pltpu.VMEM((2, M_LOC + ROW_ALIGN, D_MODEL), BF16),
                        pltpu.VMEM((N_EXP, ROW_ALIGN, D_MODEL), BF16),
                        pltpu.VMEM((FFN_TILE, D_MODEL), BF16), pltpu.SemaphoreType.DMA((2,))],
    )
    return pl.pallas_call(
        functools.partial(_dispatch_body, nt=nt),
        grid_spec=grid_spec,
        out_shape=jax.ShapeDtypeStruct((meta["rows"] + 2 * M_LOC, D_MODEL), BF16),
        compiler_params=_params(("arbitrary",)),
        name="moe_dispatch",
    )(meta["n_groups"], meta["dispatch_rows"], meta["slot_start"], meta["carry_start"], meta["carry_rows"],
      meta["tail"], h2, route, soff_col)


def _ffn_body(be_ref, nu_ref, nx_ref, sl_ref, hf_ref, x_ref, bgu_ref, bd_ref, wgu_hbm, wd_hbm, y_ref,
              wgu_f32, wd_f32, wgu_scr, wd_scr, sem):
    i = pl.program_id(0)
    expert = be_ref[i]
    slot = sl_ref[i]
    prev = be_ref[jnp.maximum(i - 1, 0)]

    def weight_copies(e, s):
        return (pltpu.make_async_copy(wgu_hbm.at[e], wgu_f32.at[s], sem.at[s]),
                pltpu.make_async_copy(wd_hbm.at[e], wd_f32.at[s], sem.at[s]))

    @pl.when(i == 0)
    def _():
        for cp in weight_copies(expert, slot):
            cp.start()

    @pl.when(jnp.logical_and(i < nu_ref[0], jnp.logical_or(i == 0, expert != prev)))
    def _():
        for cp in weight_copies(expert, slot):
            cp.wait()
        wgu_scr[...] = wgu_f32[slot].astype(BF16)
        wd_scr[...] = wd_f32[slot].astype(BF16)

        @pl.when(nx_ref[i] >= 0)
        def _():
            for cp in weight_copies(nx_ref[i], 1 - slot):
                cp.start()

    def expert_rows(m):
        gu = _dot(x_ref[0:m, :], wgu_scr[...]) + bgu_ref[...]
        gate = jnp.minimum(gu[:, :D_FF], SWIGLU_LIMIT)
        up = jnp.clip(gu[:, D_FF:], -SWIGLU_LIMIT, SWIGLU_LIMIT)
        act = (up + 1.0) * gate * _sigmoid(SWIGLU_ALPHA * gate)
        y_ref[0:m, :] = (_dot(act.astype(BF16), wd_scr[...]) + bd_ref[...]).astype(BF16)

    active = i < nu_ref[0]
    pl.when(jnp.logical_and(active, hf_ref[i] == 0))(lambda: expert_rows(FFN_TILE))
    pl.when(jnp.logical_and(active, hf_ref[i] != 0))(lambda: expert_rows(FFN_TILE // 2))


def _ffn(meta, xs, w_gu, b_gu, w_down, b_down):
    rows = meta["rows"]
    nblk = rows // FFN_TILE
    row_blk = pl.BlockSpec((FFN_TILE, D_MODEL), lambda i, be, nu, *_: (jnp.minimum(i, nu[0] - 1), 0))
    grid_spec = pltpu.PrefetchScalarGridSpec(
        num_scalar_prefetch=5,
        grid=(nblk,),
        in_specs=[row_blk,
                  pl.BlockSpec((None, 1, 2 * D_FF), lambda i, be, *_: (be[i], 0, 0)),
                  pl.BlockSpec((None, 1, D_MODEL), lambda i, be, *_: (be[i], 0, 0)),
                  pl.BlockSpec(memory_space=pl.ANY), pl.BlockSpec(memory_space=pl.ANY)],
        out_specs=row_blk,
        scratch_shapes=[pltpu.VMEM((2, D_MODEL, 2 * D_FF), F32), pltpu.VMEM((2, D_FF, D_MODEL), F32),
                        pltpu.VMEM((D_MODEL, 2 * D_FF), BF16), pltpu.VMEM((D_FF, D_MODEL), BF16),
                        pltpu.SemaphoreType.DMA((2,))],
    )
    return pl.pallas_call(
        _ffn_body,
        grid_spec=grid_spec,
        out_shape=jax.ShapeDtypeStruct((rows, D_MODEL), BF16),
        compiler_params=_params(("arbitrary",)),
        name="moe_ffn",
    )(meta["blk_exp"], meta["n_used"], meta["next_exp"], meta["blk_slot"], meta["blk_half"], xs,
      b_gu.reshape(N_EXP, 1, 2 * D_FF), b_down.reshape(N_EXP, 1, D_MODEL), w_gu, w_down)


def _combine_body(ng_ref, ct_ref, route_ref, soff_ref, x2_ref, gtp_ref, gts_ref, gf_ref, ys_ref,
                  yp_ref, ysm_ref, loc_scr, moe_scr, sem, *, nt, nt_prompt, final):
    t = pl.program_id(0)
    buf = t % 2

    def copies_of(tt):
        def make(g, row):
            return pltpu.make_async_copy(
                ys_ref.at[pl.ds(pl.multiple_of(row, ROW_ALIGN), ROW_ALIGN)],
                loc_scr.at[tt % 2, pl.ds(pl.multiple_of(g * ROW_ALIGN, ROW_ALIGN), ROW_ALIGN)], sem.at[tt % 2])
        return ng_ref[tt], ct_ref, tt * GROUPS, make

    @pl.when(t == 0)
    def _():
        loc_scr[...] = jnp.zeros(loc_scr.shape, BF16)
        _for_groups(*copies_of(0), start=True)

    @pl.when(t + 1 < nt)
    def _():
        _for_groups(*copies_of(t + 1), start=True)

    _for_groups(*copies_of(t), start=False)

    route = route_ref[...]
    soff_row = soff_ref[...]
    lane = lax.broadcasted_iota(jnp.int32, (TOK_TILE, LANES), 1).astype(F32)
    slots = [jnp.sum(jnp.where(lane == route[:, k:k + 1], soff_row, 0.0), axis=1, keepdims=True)
             + route[:, 2 * TOP_K + k:2 * TOP_K + k + 1] for k in range(TOP_K)]

    def weighted_sum(m):
        c_i = lax.broadcasted_iota(jnp.int32, (TOK_TILE, m), 1).astype(F32)
        wmat = jnp.zeros((TOK_TILE, m), F32)
        for k in range(TOP_K):
            wmat = jnp.where(c_i == slots[k], route[:, TOP_K + k:TOP_K + k + 1], wmat)
        moe_scr[...] = _dot(wmat.astype(BF16), loc_scr[buf, 0:m, :])

    fits = ng_ref[t] * ROW_ALIGN <= M_LOC_SHORT
    pl.when(fits)(lambda: weighted_sum(M_LOC_SHORT))
    pl.when(jnp.logical_not(fits))(lambda: weighted_sum(M_LOC))

    gate = jnp.where(t >= nt_prompt, gts_ref[...], gtp_ref[...])
    xo = x2_ref[...] + gate * moe_scr[...]
    if final:
        ms = jnp.mean(xo * xo, axis=1, keepdims=True)
        xo = (xo * lax.rsqrt(ms + EPS)) * gf_ref[...]

    @pl.when(t < nt_prompt)
    def _():
        yp_ref[...] = xo

    @pl.when(t >= nt_prompt)
    def _():
        ysm_ref[...] = xo


def _combine(meta, route, soff_row, x2, gt2_p, gt2_s, g_final, ys, n_prompt, n_sample, final):
    nt = x2.shape[0] // TOK_TILE
    nt_prompt = n_prompt // TOK_TILE
    tiles_per_batch = nt_prompt // gt2_p.shape[0]
    grid_spec = pltpu.PrefetchScalarGridSpec(
        num_scalar_prefetch=2,
        grid=(nt,),
        in_specs=[pl.BlockSpec((TOK_TILE, LANES), lambda t, *_: (t, 0)),
                  pl.BlockSpec((None, 1, LANES), lambda t, *_: (t, 0, 0)),
                  pl.BlockSpec((TOK_TILE, D_MODEL), lambda t, *_: (t, 0)),
                  pl.BlockSpec((None, 1, D_MODEL),
                               lambda t, *_: (jnp.minimum(t, nt_prompt - 1) // tiles_per_batch, 0, 0)),
                  pl.BlockSpec((TOK_TILE, D_MODEL), lambda t, *_: (0, 0)),
                  pl.BlockSpec((1, D_MODEL), lambda t, *_: (0, 0)),
                  pl.BlockSpec(memory_space=pl.ANY)],
        out_specs=[pl.BlockSpec((TOK_TILE, D_MODEL), lambda t, *_: (jnp.minimum(t, nt_prompt - 1), 0)),
                   pl.BlockSpec((TOK_TILE, D_MODEL), lambda t, *_: (0, 0))],
        scratch_shapes=[pltpu.VMEM((2, M_LOC, D_MODEL), BF16), pltpu.VMEM((TOK_TILE, D_MODEL), F32),
                        pltpu.SemaphoreType.DMA((2,))],
    )
    return pl.pallas_call(
        functools.partial(_combine_body, nt=nt, nt_prompt=nt_prompt, final=final),
        grid_spec=grid_spec,
        out_shape=[jax.ShapeDtypeStruct((n_prompt, D_MODEL), F32), jax.ShapeDtypeStruct((n_sample, D_MODEL), F32)],
        compiler_params=_params(("arbitrary",)),
        name="moe_combine",
    )(meta["n_groups"], meta["combine_rows"], route, soff_row, x2, gt2_p, gt2_s, g_final, ys)


def _moe_offsets(cnt):
    nt = cnt.shape[0]
    ra = ROW_ALIGN
    prefix = jnp.cumsum(cnt, axis=0) - cnt
    total = jnp.sum(cnt, axis=0)
    pending = prefix % ra
    used = pending + cnt
    seg = (used + ra - 1) // ra * ra
    lo = jnp.cumsum(seg, axis=1) - seg
    n_groups = jnp.sum(seg, axis=1) // ra
    gpad = (total + FFN_TILE - 1) // FFN_TILE * FFN_TILE
    gstart = jnp.cumsum(gpad) - gpad
    base = gstart[None, :] + prefix // ra * ra
    last = (jnp.arange(nt) == nt - 1)[:, None]
    n_write = jnp.where(last, seg // ra, used // ra)
    carry_start = lo + used // ra * ra
    carry_rows = jnp.where(last, 0, used % ra)
    g = jnp.arange(GROUPS)
    slot_end = (lo + seg) // ra
    owner = jnp.minimum(jnp.sum(g[None, :, None] >= slot_end[:, None, :], axis=2), N_EXP - 1)
    pick = lambda a: jnp.sum(jnp.where(owner[:, :, None] == jnp.arange(N_EXP), a[:, None, :], 0), axis=2)
    k = g[None, :] - pick(lo) // ra
    row = pick(base) + k * ra
    valid = g[None, :] < n_groups[:, None]
    rows = (nt * TOK_TILE * TOP_K + N_EXP * (FFN_TILE - 1) + FFN_TILE - 1) // FFN_TILE * FFN_TILE
    spare = rows + (jnp.arange(nt) % 2)[:, None] * M_LOC + g[None, :] * ra
    combine_rows = jnp.where(valid, row, 0)
    dispatch_rows = jnp.where(valid & (k < pick(n_write)), row, spare)

    nblk_e = gpad // FFN_TILE
    blk_end = jnp.cumsum(nblk_e)
    n_used = jnp.maximum(blk_end[-1], 1)
    blk = jnp.minimum(jnp.arange(rows // FFN_TILE, dtype=jnp.int32), n_used - 1)
    blk_exp = jnp.minimum(jnp.sum(blk[:, None] >= blk_end[None, :], axis=1), N_EXP - 1)
    experts = jnp.arange(N_EXP)
    following = lax.cummin(jnp.where(nblk_e > 0, experts, N_EXP), reverse=True)
    next_of = jnp.concatenate([following[1:], jnp.full((1,), N_EXP, following.dtype)])
    next_of = jnp.where(next_of >= N_EXP, -1, next_of)
    parity = (jnp.cumsum(nblk_e > 0) - 1) % 2
    pick_e = lambda a: jnp.sum(jnp.where(blk_exp[:, None] == experts[None, :], a[None, :], 0), axis=1)
    blk_rows = pick_e(total) - (blk - pick_e(blk_end - nblk_e)) * FFN_TILE
    blk_half = blk_rows <= FFN_TILE // 2
    total16 = (total + ra - 1) // ra * ra
    tail = jnp.concatenate([gstart + total16, (gpad - total16) // ra])
    i32 = lambda a: a.astype(jnp.int32).reshape(-1)
    return dict(n_groups=i32(n_groups), dispatch_rows=i32(dispatch_rows), combine_rows=i32(combine_rows),
                slot_start=i32(lo), carry_start=i32(carry_start), carry_rows=i32(carry_rows), tail=i32(tail),
                blk_exp=i32(blk_exp), n_used=i32(n_used), next_exp=i32(pick_e(next_of)), blk_slot=i32(pick_e(parity)),
                blk_half=i32(blk_half),
                slot_off=(lo + pending).astype(F32), rows=rows)


def _split_w_in(w_in):
    g0 = C_GATES
    return (w_in[:, :g0].astype(BF16), w_in[:, g0 + N_GATE:].astype(BF16),
            _pad_lanes(w_in[:, g0:g0 + N_GATE]).astype(BF16))


def _pad_lanes(a, value=0.0):
    return jnp.pad(a, [(0, 0)] * (a.ndim - 1) + [(0, LANES - a.shape[-1])], constant_values=value)


def kernel(x_prompt, x_sample, c_prompt, c_sample, cache_k, cache_v, state_conv, state_C, state_n, state_m, page_table, w_ada, b_ada, g_norm1, g_norm2, w_in, b_gates, lambda_q1, lambda_k1, lambda_q2, lambda_k2, g_subln, w_conv, b_conv, g_mnorm, w_up_a, w_up_b, w_out, w_router, b_router, w_gu, b_gu, w_down, b_down, g_final):
    B, S, D = x_prompt.shape
    Bd, Td, _ = x_sample.shape
    depth = w_in.shape[0]
    n_pool = cache_k.shape[1]
    past_len = page_table.shape[1] * PAGE_SIZE
    n_p, n_s = B * S, Bd * Td
    n_all = n_p + n_s
    assert D == D_MODEL and n_s == TOK_TILE and S % MLSTM_CHUNK == 0 and n_p % TOK_TILE == 0
    assert page_table.shape[1] % PAGES_PER_STEP == 0

    cos_p, sin_p = _rope_tables(np.arange(S))
    cos_s, sin_s = _rope_tables(np.tile(past_len + np.arange(Td), Bd))
    hp = x_prompt.reshape(n_p, D)
    hs = x_sample.reshape(n_s, D)
    c_all = jnp.concatenate([c_prompt, c_sample], axis=0)
    outs = [[] for _ in range(12)]

    for l in range(depth):
        lam_init = 0.8 - 0.6 * math.exp(-0.3 * l)
        mod = _ada(c_all, w_ada[l], b_ada[l])
        mods = [mod[:, j * D:(j + 1) * D] for j in range(6)]
        mp = [m[:B].reshape(B, 1, D) for m in mods]
        ms_ = [jnp.repeat(m[B:], Td, axis=0).reshape(1, n_s, D) for m in mods]
        w_parts = _split_w_in(w_in[l])
        g1 = g_norm1[l].reshape(1, D)
        lam_vec = jnp.stack([lambda_q1[l], lambda_k1[l], lambda_q2[l], lambda_k2[l]])
        gsub = g_subln[l].reshape(1, DV_A)
        bg = _pad_lanes(b_gates[l].reshape(1, N_GATE))
        cw, cb = w_conv[l], b_conv[l].reshape(1, 2 * W_M)

        (q_p, _, kb_p, v_p, _, qk_p, vm_p, om_p, ga_p, gb_p, gt_p, vt_p, kt_p) = _inproj(
            hp, mp[1], mp[0], g1, cos_p, sin_p, w_parts, 1, PROJ_TILE)
        (q_s, k_s, kb_s, v_s, vb_s, qk_s, vm_s, om_s, ga_s, gb_s, gt_s, _, _) = _inproj(
            hs, ms_[1], ms_[0], g1, cos_s, sin_s, w_parts, n_s, n_s)
        k_p = jnp.transpose(kt_p.reshape(B, H_A, 2, DK_A, S), (0, 4, 1, 2, 3))

        kt_pool = jnp.transpose(cache_k[l], (0, 2, 3, 4, 1)).reshape(n_pool, W_QA, PAGE_SIZE)
        v_pool = cache_v[l].reshape(n_pool, PAGE_SIZE * H_A, DV_A)
        oa_p, oa_s = _attention(q_p, kb_p, vt_p, q_s, kb_s, vb_s, kt_pool, v_pool, page_table, lam_vec, gsub,
                                B, S, lam_init, Td)

        zeros = lambda *shape: jnp.zeros(shape, F32)
        hm_p, cst_p, C_p, nn_p, m_p = _mlstm(qk_p, vm_p, om_p, gt_p, cw, cb, bg, g_mnorm[l],
                                             zeros(B, CONV_W - 1, 2 * W_M), zeros(B, H_M, DH_M, DH_M),
                                             zeros(B, H_M, DH_M), zeros(B, 1, LANES), B, S)
        hm_s, cst_s, C_s, nn_s, m_s = _mlstm(qk_s, vm_s, om_s, gt_s, cw, cb, bg, g_mnorm[l],
                                             state_conv[l], state_C[l], state_n[l],
                                             _pad_lanes(state_m[l]).reshape(Bd, 1, LANES), Bd, Td)

        wa, wb, wo = w_up_a[l].astype(BF16), w_up_b[l].astype(BF16), w_out[l].astype(BF16)
        wr = _pad_lanes(w_router[l]).astype(BF16)
        br = _pad_lanes(b_router[l].reshape(1, N_EXP))
        g2 = g_norm2[l].reshape(1, D)
        part = _merge(oa_p, hm_p, ga_p, gb_p, hp, mp[2], mp[4], mp[3], g2, wa, wb, wo, wr, br,
                      1, PROJ_TILE, 0, n_all)
        x2, h2, route, cnt = _merge(oa_s, hm_s, ga_s, gb_s, hs, ms_[2], ms_[4], ms_[3], g2, wa, wb, wo, wr, br,
                                    n_s, n_s, n_p, n_all, prev=part)

        meta = _moe_offsets(jnp.round(cnt[:, 0, ROUTE_SEL:ROUTE_SEL + N_EXP]).astype(jnp.int32))
        soff = _pad_lanes(meta["slot_off"])
        xs = _dispatch(meta, h2, route, soff[:, :, None])
        ys = _ffn(meta, xs, w_gu[l], b_gu[l], w_down[l], b_down[l])
        final = l == depth - 1
        hp, hs = _combine(meta, route, soff[:, None, :], x2, mp[5], ms_[5][0], g_final.reshape(1, D), ys,
                          n_p, n_s, final)

        for j, a in enumerate([k_p, v_p.reshape(B, S, H_A, DV_A), cst_p, C_p, nn_p,
                               m_p[:, 0, :H_M],
                               k_s.reshape(Bd, Td, H_A, 2, DK_A), v_s.reshape(Bd, Td, H_A, DV_A), cst_s, C_s, nn_s,
                               m_s[:, 0, :H_M]]):
            outs[j].append(a)

    return (hp.reshape(B, S, D), hs.reshape(Bd, Td, D)) + tuple(jnp.stack(o) for o in outs)
```
